```python
import math
import jax, jax.numpy as jnp
from jax import lax
import numpy as np

D_MODEL = 1024
BATCH = 8
SEQ = 8192
DEPTH = 2

CHUNK = 64
HEAD_DIM = 64
N_HEADS_FOX = 8
N_HEADS_CHUNK = 8
WIDTH_FOX = N_HEADS_FOX * HEAD_DIM
WIDTH_CHUNK = N_HEADS_CHUNK * HEAD_DIM
LEFT_CHUNKS = 8
BAND_CHUNKS = LEFT_CHUNKS + 1
REL_CLIP = 128
Q_BLOCK = 128
D_FF = 2816
CONV_WIDTH = 3
LN_EPS = 1e-5
N_MOD = 6
PROJ_SIZES = (WIDTH_FOX, WIDTH_FOX, WIDTH_FOX, N_HEADS_FOX,
              WIDTH_CHUNK, WIDTH_CHUNK, WIDTH_CHUNK, D_MODEL, D_MODEL)
PROJ_COLS = sum(PROJ_SIZES)

kernel_name = "fox_chunkattn_gated_hybrid_deepnorm_adaln"


def _layer_norm(x, gain=None, bias=None):
    xf = x.astype(jnp.float32)
    mu = jnp.mean(xf, axis=-1, keepdims=True)
    var = jnp.mean(jnp.square(xf - mu), axis=-1, keepdims=True)
    y = (xf - mu) * lax.rsqrt(var + LN_EPS)
    if gain is not None:
        y = y * gain.astype(jnp.float32) + bias.astype(jnp.float32)
    return y.astype(x.dtype)


def _forgetting_attention(q, k, v, f_logit):
    b, s, h, dh = q.shape
    n_blk = s // Q_BLOCK
    log_f = jax.nn.log_sigmoid(f_logit.astype(jnp.float32))
    cum = jnp.cumsum(log_f, axis=1).transpose(0, 2, 1)
    scale = 1.0 / math.sqrt(dh)
    k_pos = jnp.arange(s)
    q_blocks = q.reshape(b, n_blk, Q_BLOCK, h, dh).transpose(1, 0, 2, 3, 4)
    cum_blocks = cum.reshape(b, h, n_blk, Q_BLOCK).transpose(2, 0, 1, 3)
    neg = jnp.finfo(jnp.float32).min

    def block(args):
        qb, cq, i = args
        q_pos = i * Q_BLOCK + jnp.arange(Q_BLOCK)
        logits = jnp.einsum('bqhd,bkhd->bhqk', qb, k).astype(jnp.float32) * scale
        logits = logits + cq[:, :, :, None] - cum[:, :, None, :]
        mask = k_pos[None, :] <= q_pos[:, None]
        logits = jnp.where(mask[None, None], logits, neg)
        p = jax.nn.softmax(logits, axis=-1).astype(v.dtype)
        return jnp.einsum('bhqk,bkhd->bqhd', p, v)

    out = lax.map(block, (q_blocks, cum_blocks, jnp.arange(n_blk)))
    return out.transpose(1, 0, 2, 3, 4).reshape(b, s, h * dh)


def _chunk_band_attention(q, k, v, rel_table):
    b, s, h, dh = q.shape
    n_c = s // CHUNK
    band = BAND_CHUNKS * CHUNK
    qc = q.reshape(b, n_c, CHUNK, h, dh)
    pad = ((0, 0), (LEFT_CHUNKS, 0), (0, 0), (0, 0), (0, 0))
    kp = jnp.pad(k.reshape(b, n_c, CHUNK, h, dh), pad)
    vp = jnp.pad(v.reshape(b, n_c, CHUNK, h, dh), pad)
    band_idx = jnp.arange(n_c)[:, None] + jnp.arange(BAND_CHUNKS)[None, :]
    kb = kp[:, band_idx].reshape(b, n_c, band, h, dh)
    vb = vp[:, band_idx].reshape(b, n_c, band, h, dh)
    valid = jnp.repeat(band_idx >= LEFT_CHUNKS, CHUNK, axis=1)
    q_off = LEFT_CHUNKS * CHUNK + np.arange(CHUNK)
    rel = np.clip(q_off[:, None] - np.arange(band)[None, :], -REL_CLIP, REL_CLIP) + REL_CLIP
    bias = rel_table[:, rel].astype(jnp.float32)
    scale = 1.0 / math.sqrt(dh)
    logits = jnp.einsum('bcqhd,bckhd->bchqk', qc, kb).astype(jnp.float32) * scale
    logits = logits + bias[None, None]
    logits = jnp.where(valid[None, :, None, None, :], logits, jnp.finfo(jnp.float32).min)
    p = jax.nn.softmax(logits, axis=-1).astype(v.dtype)
    out = jnp.einsum('bchqk,bckhd->bcqhd', p, vb)
    return out.reshape(b, s, h * dh)


def _causal_depthwise_conv(u, w, bias):
    s = u.shape[1]
    up = jnp.pad(u, ((0, 0), (CONV_WIDTH - 1, 0), (0, 0)))
    y = bias
    for j in range(CONV_WIDTH):
        y = y + w[j] * up[:, j:j + s]
    return y


def _fwd_setup_inputs(seed: int = 0) -> dict:
    key = jax.random.key(seed)
    ks = jax.random.split(key, 20)
    beta = (8.0 * DEPTH) ** -0.25
    f32 = jnp.float32
    nrm = lambda k, shape: jax.random.normal(k, shape, f32)
    col_scale = np.concatenate([
        np.ones(2 * WIDTH_FOX), beta * np.ones(WIDTH_FOX), 0.5 * np.ones(N_HEADS_FOX),
        np.ones(2 * WIDTH_CHUNK), beta * np.ones(WIDTH_CHUNK), np.ones(2 * D_MODEL)]).astype(np.float32)
    w_in = nrm(ks[2], (DEPTH, D_MODEL, PROJ_COLS)) * (D_MODEL ** -0.5) * jnp.asarray(col_scale)
    b_f = jnp.linspace(1.0, 5.0, N_HEADS_FOX, dtype=f32)[None, :] + 0.1 * nrm(ks[3], (DEPTH, N_HEADS_FOX))
    rel_bias = 0.5 * nrm(ks[4], (DEPTH, N_HEADS_CHUNK, 2 * REL_CLIP + 1))
    w_br_fox = nrm(ks[5], (DEPTH, WIDTH_FOX, D_MODEL)) * WIDTH_FOX ** -0.5
    w_br_chunk = nrm(ks[6], (DEPTH, WIDTH_CHUNK, D_MODEL)) * WIDTH_CHUNK ** -0.5
    w_out = nrm(ks[7], (DEPTH, D_MODEL, D_MODEL)) * (D_MODEL ** -0.5) * beta
    w_up = nrm(ks[8], (DEPTH, D_MODEL, 2 * D_FF)) * D_MODEL ** -0.5
    conv_w = 0.3 * nrm(ks[9], (DEPTH, CONV_WIDTH, 2 * D_FF)) + jnp.array([0.0, 0.0, 1.0], f32)[None, :, None]
    conv_b = 0.01 * nrm(ks[10], (DEPTH, 2 * D_FF))
    w_down = nrm(ks[11], (DEPTH, D_FF, D_MODEL)) * (D_FF ** -0.5) * beta
    w_ada = 0.2 * nrm(ks[12], (DEPTH, D_MODEL, N_MOD * D_MODEL)) * D_MODEL ** -0.5
    b_ada = 0.01 * nrm(ks[13], (DEPTH, N_MOD * D_MODEL))
    ln1_g = 1.0 + 0.05 * nrm(ks[14], (DEPTH, D_MODEL))
    ln1_b = 0.01 * nrm(ks[15], (DEPTH, D_MODEL))
    ln2_g = 1.0 + 0.05 * nrm(ks[16], (DEPTH, D_MODEL))
    ln2_b = 0.01 * nrm(ks[17], (DEPTH, D_MODEL))
    x = nrm(ks[0], (BATCH, SEQ, D_MODEL))
    c = nrm(ks[1], (BATCH, D_MODEL))
    return {"x": x, "c": c, "w_in": w_in, "b_f": b_f, "rel_bias": rel_bias,
            "w_br_fox": w_br_fox, "w_br_chunk": w_br_chunk, "w_out": w_out,
            "w_up": w_up, "conv_w": conv_w, "conv_b": conv_b, "w_down": w_down,
            "w_ada": w_ada, "b_ada": b_ada, "ln1_g": ln1_g, "ln1_b": ln1_b,
            "ln2_g": ln2_g, "ln2_b": ln2_b}


def _fwd_reference(x, c, w_in, b_f, rel_bias, w_br_fox, w_br_chunk, w_out, w_up, conv_w,
              conv_b, w_down, w_ada, b_ada, ln1_g, ln1_b, ln2_g, ln2_b):
    alpha = (2.0 * DEPTH) ** 0.25
    b, s, _ = x.shape
    split_points = list(np.cumsum(PROJ_SIZES)[:-1])
    cond = jax.nn.silu(c)
    for l in range(DEPTH):
        mod = cond @ w_ada[l] + b_ada[l]
        sh1, sc1, g1, sh2, sc2, g2 = jnp.split(mod[:, None, :], N_MOD, axis=-1)

        h = _layer_norm(x) * (1.0 + sc1) + sh1
        proj = h @ w_in[l]
        q_a, k_a, v_a, f_a, q_c, k_c, v_c, gate_a, gate_c = jnp.split(proj, split_points, axis=-1)
        heads_a = lambda t: t.reshape(b, s, N_HEADS_FOX, HEAD_DIM)
        heads_c = lambda t: t.reshape(b, s, N_HEADS_CHUNK, HEAD_DIM)
        o_a = _forgetting_attention(heads_a(q_a), heads_a(k_a), heads_a(v_a), f_a + b_f[l])
        o_c = _chunk_band_attention(heads_c(q_c), heads_c(k_c), heads_c(v_c), rel_bias[l])
        merged = (jax.nn.sigmoid(gate_a) * (o_a @ w_br_fox[l])
                  + jax.nn.sigmoid(gate_c) * (o_c @ w_br_chunk[l]))
        mix = merged @ w_out[l]
        x = _layer_norm(alpha * x + (1.0 + g1) * mix, ln1_g[l], ln1_b[l])

        h = _layer_norm(x) * (1.0 + sc2) + sh2
        u = _causal_depthwise_conv(h @ w_up[l], conv_w[l], conv_b[l])
        a, val = jnp.split(u, 2, axis=-1)
        y = (jax.nn.silu(a) * val) @ w_down[l]
        x = _layer_norm(alpha * x + (1.0 + g2) * y, ln2_g[l], ln2_b[l])
    return x


import jax as _jax
import jax.numpy as _jnp

TWIN_FORMAT = 'train_step'
FWD_PARAMS = ['x', 'c', 'w_in', 'b_f', 'rel_bias', 'w_br_fox', 'w_br_chunk', 'w_out', 'w_up', 'conv_w', 'conv_b', 'w_down', 'w_ada', 'b_ada', 'ln1_g', 'ln1_b', 'ln2_g', 'ln2_b']
TWIN_WEIGHTS = ['w_in', 'b_f', 'rel_bias', 'w_br_fox', 'w_br_chunk', 'w_out', 'w_up', 'conv_w', 'conv_b', 'w_down', 'w_ada', 'b_ada', 'ln1_g', 'ln1_b', 'ln2_g', 'ln2_b']
TWIN_DIFF_INPUT = 'x'
TWIN_INPUTS = ['x', 'c', 'w_in', 'b_f', 'rel_bias', 'w_br_fox', 'w_br_chunk', 'w_out', 'w_up', 'conv_w', 'conv_b', 'w_down', 'w_ada', 'b_ada', 'ln1_g', 'ln1_b', 'ln2_g', 'ln2_b', 'loss_target', 'm_w_in', 'm_b_f', 'm_rel_bias', 'm_w_br_fox', 'm_w_br_chunk', 'm_w_out', 'm_w_up', 'm_conv_w', 'm_conv_b', 'm_w_down', 'm_w_ada', 'm_b_ada', 'm_ln1_g', 'm_ln1_b', 'm_ln2_g', 'm_ln2_b', 'v_w_in', 'v_b_f', 'v_rel_bias', 'v_w_br_fox', 'v_w_br_chunk', 'v_w_out', 'v_w_up', 'v_conv_w', 'v_conv_b', 'v_w_down', 'v_w_ada', 'v_b_ada', 'v_ln1_g', 'v_ln1_b', 'v_ln2_g', 'v_ln2_b']
TWIN_OUTPUTS = ['loss', 'grad_x', 'grad_w_in', 'grad_b_f', 'grad_rel_bias', 'grad_w_br_fox', 'grad_w_br_chunk', 'grad_w_out', 'grad_w_up', 'grad_conv_w', 'grad_conv_b', 'grad_w_down', 'grad_w_ada', 'grad_b_ada', 'grad_ln1_g', 'grad_ln1_b', 'grad_ln2_g', 'grad_ln2_b', 'delta_w_in', 'delta_b_f', 'delta_rel_bias', 'delta_w_br_fox', 'delta_w_br_chunk', 'delta_w_out', 'delta_w_up', 'delta_conv_w', 'delta_conv_b', 'delta_w_down', 'delta_w_ada', 'delta_b_ada', 'delta_ln1_g', 'delta_ln1_b', 'delta_ln2_g', 'delta_ln2_b', 'new_m_w_in', 'new_m_b_f', 'new_m_rel_bias', 'new_m_w_br_fox', 'new_m_w_br_chunk', 'new_m_w_out', 'new_m_w_up', 'new_m_conv_w', 'new_m_conv_b', 'new_m_w_down', 'new_m_w_ada', 'new_m_b_ada', 'new_m_ln1_g', 'new_m_ln1_b', 'new_m_ln2_g', 'new_m_ln2_b', 'new_v_w_in', 'new_v_b_f', 'new_v_rel_bias', 'new_v_w_br_fox', 'new_v_w_br_chunk', 'new_v_w_out', 'new_v_w_up', 'new_v_conv_w', 'new_v_conv_b', 'new_v_w_down', 'new_v_w_ada', 'new_v_b_ada', 'new_v_ln1_g', 'new_v_ln1_b', 'new_v_ln2_g', 'new_v_ln2_b']
TWIN_LEAF_KINDS = {'loss': 'loss', 'grad_x': 'grad_x', 'grad_w_in': 'grad_w', 'grad_b_f': 'grad_w', 'grad_rel_bias': 'grad_w', 'grad_w_br_fox': 'grad_w', 'grad_w_br_chunk': 'grad_w', 'grad_w_out': 'grad_w', 'grad_w_up': 'grad_w', 'grad_conv_w': 'grad_w', 'grad_conv_b': 'grad_w', 'grad_w_down': 'grad_w', 'grad_w_ada': 'grad_w', 'grad_b_ada': 'grad_w', 'grad_ln1_g': 'grad_w', 'grad_ln1_b': 'grad_w', 'grad_ln2_g': 'grad_w', 'grad_ln2_b': 'grad_w', 'delta_w_in': 'delta_w', 'delta_b_f': 'delta_w', 'delta_rel_bias': 'delta_w', 'delta_w_br_fox': 'delta_w', 'delta_w_br_chunk': 'delta_w', 'delta_w_out': 'delta_w', 'delta_w_up': 'delta_w', 'delta_conv_w': 'delta_w', 'delta_conv_b': 'delta_w', 'delta_w_down': 'delta_w', 'delta_w_ada': 'delta_w', 'delta_b_ada': 'delta_w', 'delta_ln1_g': 'delta_w', 'delta_ln1_b': 'delta_w', 'delta_ln2_g': 'delta_w', 'delta_ln2_b': 'delta_w', 'new_m_w_in': 'new_m', 'new_m_b_f': 'new_m', 'new_m_rel_bias': 'new_m', 'new_m_w_br_fox': 'new_m', 'new_m_w_br_chunk': 'new_m', 'new_m_w_out': 'new_m', 'new_m_w_up': 'new_m', 'new_m_conv_w': 'new_m', 'new_m_conv_b': 'new_m', 'new_m_w_down': 'new_m', 'new_m_w_ada': 'new_m', 'new_m_b_ada': 'new_m', 'new_m_ln1_g': 'new_m', 'new_m_ln1_b': 'new_m', 'new_m_ln2_g': 'new_m', 'new_m_ln2_b': 'new_m', 'new_v_w_in': 'new_v', 'new_v_b_f': 'new_v', 'new_v_rel_bias': 'new_v', 'new_v_w_br_fox': 'new_v', 'new_v_w_br_chunk': 'new_v', 'new_v_w_out': 'new_v', 'new_v_w_up': 'new_v', 'new_v_conv_w': 'new_v', 'new_v_conv_b': 'new_v', 'new_v_w_down': 'new_v', 'new_v_w_ada': 'new_v', 'new_v_b_ada': 'new_v', 'new_v_ln1_g': 'new_v', 'new_v_ln1_b': 'new_v', 'new_v_ln2_g': 'new_v', 'new_v_ln2_b': 'new_v'}


def _forward(args):
    return _fwd_reference(*[args[k] for k in FWD_PARAMS])


def _output_shape():
    def fwd():
        inp = _fwd_setup_inputs(0)
        return _fwd_reference(*[inp[k] for k in FWD_PARAMS])
    out = _jax.eval_shape(fwd)
    return out.shape, out.dtype

N_MICROBATCH = 1
ADAM_LR = 0.001
ADAM_B1 = 0.9
ADAM_B2 = 0.999
ADAM_EPS = 1e-08
ADAM_WD = 0.01
ADAM_STEP = 10
PER_EXAMPLE_BATCH_AXIS = {'x': 0, 'c': 0, 'loss_target': 0}
SHARED_INPUTS = []
_WEIGHT_DTYPES = {'w_in': _jnp.float32, 'b_f': _jnp.float32, 'rel_bias': _jnp.float32, 'w_br_fox': _jnp.float32, 'w_br_chunk': _jnp.float32, 'w_out': _jnp.float32, 'w_up': _jnp.float32, 'conv_w': _jnp.float32, 'conv_b': _jnp.float32, 'w_down': _jnp.float32, 'w_ada': _jnp.float32, 'b_ada': _jnp.float32, 'ln1_g': _jnp.float32, 'ln1_b': _jnp.float32, 'ln2_g': _jnp.float32, 'ln2_b': _jnp.float32}
MOMENT_SCALE = {'w_in': 1.105642e-02, 'b_f': 4.697904e-02, 'rel_bias': 2.779245e-03, 'w_br_fox': 9.287620e-03, 'w_br_chunk': 5.321781e-03, 'w_out': 2.138516e-02, 'w_up': 4.363900e-02, 'conv_w': 3.998539e-02, 'conv_b': 4.103892e-02, 'w_down': 1.433507e-01, 'w_ada': 5.000250e-02, 'b_ada': 1.326750e-01, 'ln1_g': 5.451783e+00, 'ln1_b': 4.861219e-01, 'ln2_g': 4.593941e+01, 'ln2_b': 9.298067e-01}


def _to_microbatches(a, axis):
    t = _jnp.moveaxis(a, axis, 0)
    t = t.reshape((N_MICROBATCH, t.shape[0] // N_MICROBATCH) + t.shape[1:])
    return _jnp.moveaxis(t, 1, axis + 1)


def setup_inputs(seed: int = 0) -> dict:
    inp = _fwd_setup_inputs(seed)
    key = _jax.random.fold_in(_jax.random.key(seed), 7919)
    shape, _ = _output_shape()
    out = dict(inp)
    out["loss_target"] = _jax.random.normal(_jax.random.fold_in(key, 0), shape, _jnp.float32)
    for i, name in enumerate(TWIN_WEIGHTS):
        w = inp[name].astype(_jnp.float32)
        if MOMENT_SCALE is None:
            s = _jnp.sqrt(_jnp.mean(_jnp.square(w)) + 1e-30)
        else:
            s = MOMENT_SCALE[name]
        km, kv = _jax.random.split(_jax.random.fold_in(key, i + 1))
        out[name] = w
        out["m_" + name] = s * _jax.random.normal(km, w.shape, _jnp.float32)
        out["v_" + name] = (s * s) * _jax.random.uniform(kv, w.shape, _jnp.float32, 0.5, 1.5)
    if N_MICROBATCH > 1:
        for name, axis in PER_EXAMPLE_BATCH_AXIS.items():
            out[name] = _to_microbatches(out[name], axis)
    return {'x': out['x'], 'c': out['c'], 'w_in': out['w_in'], 'b_f': out['b_f'], 'rel_bias': out['rel_bias'], 'w_br_fox': out['w_br_fox'], 'w_br_chunk': out['w_br_chunk'], 'w_out': out['w_out'], 'w_up': out['w_up'], 'conv_w': out['conv_w'], 'conv_b': out['conv_b'], 'w_down': out['w_down'], 'w_ada': out['w_ada'], 'b_ada': out['b_ada'], 'ln1_g': out['ln1_g'], 'ln1_b': out['ln1_b'], 'ln2_g': out['ln2_g'], 'ln2_b': out['ln2_b'], 'loss_target': out['loss_target'], 'm_w_in': out['m_w_in'], 'm_b_f': out['m_b_f'], 'm_rel_bias': out['m_rel_bias'], 'm_w_br_fox': out['m_w_br_fox'], 'm_w_br_chunk': out['m_w_br_chunk'], 'm_w_out': out['m_w_out'], 'm_w_up': out['m_w_up'], 'm_conv_w': out['m_conv_w'], 'm_conv_b': out['m_conv_b'], 'm_w_down': out['m_w_down'], 'm_w_ada': out['m_w_ada'], 'm_b_ada': out['m_b_ada'], 'm_ln1_g': out['m_ln1_g'], 'm_ln1_b': out['m_ln1_b'], 'm_ln2_g': out['m_ln2_g'], 'm_ln2_b': out['m_ln2_b'], 'v_w_in': out['v_w_in'], 'v_b_f': out['v_b_f'], 'v_rel_bias': out['v_rel_bias'], 'v_w_br_fox': out['v_w_br_fox'], 'v_w_br_chunk': out['v_w_br_chunk'], 'v_w_out': out['v_w_out'], 'v_w_up': out['v_w_up'], 'v_conv_w': out['v_conv_w'], 'v_conv_b': out['v_conv_b'], 'v_w_down': out['v_w_down'], 'v_w_ada': out['v_w_ada'], 'v_b_ada': out['v_b_ada'], 'v_ln1_g': out['v_ln1_g'], 'v_ln1_b': out['v_ln1_b'], 'v_ln2_g': out['v_ln2_g'], 'v_ln2_b': out['v_ln2_b']}


def _loss(weights, diff, rest, loss_target):
    with _jax.named_scope("forward"):
        args = {**rest, TWIN_DIFF_INPUT: diff, **{k: w.astype(_WEIGHT_DTYPES[k]) for k, w in weights.items()}}
        y = _forward(args)
    with _jax.named_scope("loss_head"):
        err = _jnp.square(y.astype(_jnp.float32) - loss_target)
        return 0.5 * _jnp.sum(_jnp.mean(err, axis=-1)) if err.ndim else 0.5 * err


def _adamw(w, g, m, v):
    m = ADAM_B1 * m + (1.0 - ADAM_B1) * g
    v = ADAM_B2 * v + (1.0 - ADAM_B2) * _jnp.square(g)
    m_hat = m / (1.0 - ADAM_B1 ** ADAM_STEP)
    v_hat = v / (1.0 - ADAM_B2 ** ADAM_STEP)
    delta = -ADAM_LR * (m_hat / (_jnp.sqrt(v_hat) + ADAM_EPS) + ADAM_WD * w)
    return delta, m, v


def reference(x, c, w_in, b_f, rel_bias, w_br_fox, w_br_chunk, w_out, w_up, conv_w, conv_b, w_down, w_ada, b_ada, ln1_g, ln1_b, ln2_g, ln2_b, loss_target, m_w_in, m_b_f, m_rel_bias, m_w_br_fox, m_w_br_chunk, m_w_out, m_w_up, m_conv_w, m_conv_b, m_w_down, m_w_ada, m_b_ada, m_ln1_g, m_ln1_b, m_ln2_g, m_ln2_b, v_w_in, v_b_f, v_rel_bias, v_w_br_fox, v_w_br_chunk, v_w_out, v_w_up, v_conv_w, v_conv_b, v_w_down, v_w_ada, v_b_ada, v_ln1_g, v_ln1_b, v_ln2_g, v_ln2_b):
    given = dict(x=x, c=c, w_in=w_in, b_f=b_f, rel_bias=rel_bias, w_br_fox=w_br_fox, w_br_chunk=w_br_chunk, w_out=w_out, w_up=w_up, conv_w=conv_w, conv_b=conv_b, w_down=w_down, w_ada=w_ada, b_ada=b_ada, ln1_g=ln1_g, ln1_b=ln1_b, ln2_g=ln2_g, ln2_b=ln2_b, loss_target=loss_target, m_w_in=m_w_in, m_b_f=m_b_f, m_rel_bias=m_rel_bias, m_w_br_fox=m_w_br_fox, m_w_br_chunk=m_w_br_chunk, m_w_out=m_w_out, m_w_up=m_w_up, m_conv_w=m_conv_w, m_conv_b=m_conv_b, m_w_down=m_w_down, m_w_ada=m_w_ada, m_b_ada=m_b_ada, m_ln1_g=m_ln1_g, m_ln1_b=m_ln1_b, m_ln2_g=m_ln2_g, m_ln2_b=m_ln2_b, v_w_in=v_w_in, v_b_f=v_b_f, v_rel_bias=v_rel_bias, v_w_br_fox=v_w_br_fox, v_w_br_chunk=v_w_br_chunk, v_w_out=v_w_out, v_w_up=v_w_up, v_conv_w=v_conv_w, v_conv_b=v_conv_b, v_w_down=v_w_down, v_w_ada=v_w_ada, v_b_ada=v_b_ada, v_ln1_g=v_ln1_g, v_ln1_b=v_ln1_b, v_ln2_g=v_ln2_g, v_ln2_b=v_ln2_b)
    weights = {n: given[n] for n in TWIN_WEIGHTS}
    shared = {n: given[n] for n in SHARED_INPUTS}
    per_example = {n: given[n] for n in ['x', 'c']}
    grad_fn = _jax.value_and_grad(_loss, argnums=(0, 1))

    def one_microbatch(ex, loss_target):
        ex = dict(ex)
        diff = ex.pop(TWIN_DIFF_INPUT)
        return grad_fn(weights, diff, {**shared, **ex}, loss_target)

    if N_MICROBATCH == 1:
        loss, (grad_w, grad_x) = one_microbatch(per_example, given["loss_target"])
    else:
        def body(carry, xs):
            loss_sum, grad_sum = carry
            l_k, (gw_k, gx_k) = one_microbatch(xs[0], xs[1])
            with _jax.named_scope("update"):
                return (loss_sum + l_k, _jax.tree.map(_jnp.add, grad_sum, gw_k)), gx_k

        init = (_jnp.zeros((), _jnp.float32), _jax.tree.map(_jnp.zeros_like, weights))
        (loss, grad_w), grad_x = _jax.lax.scan(body, init, (per_example, given["loss_target"]))
    with _jax.named_scope("update"):
        delta_w, new_m, new_v = {}, {}, {}
        for n in TWIN_WEIGHTS:
            delta_w[n], new_m[n], new_v[n] = _adamw(weights[n], grad_w[n], given["m_" + n], given["v_" + n])
    return (loss, grad_x, *[grad_w[n] for n in TWIN_WEIGHTS], *[delta_w[n] for n in TWIN_WEIGHTS],
            *[new_m[n] for n in TWIN_WEIGHTS], *[new_v[n] for n in TWIN_WEIGHTS])
```

```python
import functools
import math

import jax
import jax.numpy as jnp
import numpy as np
from jax import lax
from jax.experimental import pallas as pl
from jax.experimental.pallas import tpu as pltpu

F32 = jnp.float32
BF16 = jnp.bfloat16
HI = lax.Precision.HIGHEST
MESH_ID = pl.DeviceIdType.MESH
AXES = ("x", "y", "c")
N_DEV = 8

N_HEADS = 8
HEAD_DIM = 64
WIDTH = N_HEADS * HEAD_DIM
CHUNK = 64
LEFT_CHUNKS = 8
BAND = (LEFT_CHUNKS + 1) * CHUNK
REL_CLIP = 128
LN_EPS = 1e-5
N_MOD = 6
QK_SCALE = 1.0 / math.sqrt(HEAD_DIM)
NEG = -1e30
AUG = 128
N_SPLIT = 3
TQ = 512
SUB = 2 * CHUNK
SUBK = BAND + CHUNK

ADAM_LR, ADAM_B1, ADAM_B2, ADAM_EPS, ADAM_WD, ADAM_STEP = 0.001, 0.9, 0.999, 1e-08, 0.01, 10

VMEM_LIMIT = 56 * 2 ** 20


def _pcall(body, **kw):
    return pl.pallas_call(body, **kw)


def _cp(*sem):
    return pltpu.CompilerParams(dimension_semantics=sem if sem else None, vmem_limit_bytes=VMEM_LIMIT)


def _my_place():
    return lax.axis_index("x"), lax.axis_index("y"), lax.axis_index("c")


def _all_gather(x, name):
    r, c = x.shape

    def body(x_ref, out_ref, send_sems, recv_sems, local_sem):
        mx, my, mc = _my_place()
        me, sibling = (mx, my, mc), (mx, my, 1 - mc)
        chips = [(1 - mx, my), (mx, 1 - my), (1 - mx, 1 - my)]

        def slot(px, py, pc):
            return out_ref.at[4 * px + 2 * py + pc]

        def copy(k, block, to, src=None):
            return pltpu.make_async_remote_copy(
                src_ref=slot(*block) if src is None else src, dst_ref=slot(*block),
                send_sem=send_sems.at[k], recv_sem=recv_sems.at[k], device_id=to, device_id_type=MESH_ID)

        mine = pltpu.make_async_copy(x_ref, slot(*me), local_sem)
        mine.start()
        first = [copy(0, me, sibling, src=x_ref)]
        first += [copy(1 + j, me, (*chip, mc), src=x_ref) for j, chip in enumerate(chips)]
        for cp in first:
            cp.start()
        passed = [copy(4 + j, (*chip, mc), sibling) for j, chip in enumerate(chips)]
        for j, chip in enumerate(chips):
            copy(1 + j, (*chip, mc), me).wait_recv()
            passed[j].start()
        copy(0, sibling, me).wait_recv()
        for j, chip in enumerate(chips):
            copy(4 + j, (*chip, 1 - mc), me).wait_recv()
        for cp in first + passed:
            cp.wait_send()
        mine.wait()

    return _pcall(
        body, name=name,
        out_shape=jax.ShapeDtypeStruct((N_DEV, r, c), x.dtype),
        in_specs=[pl.BlockSpec(memory_space=pltpu.HBM)],
        out_specs=pl.BlockSpec(memory_space=pltpu.HBM),
        scratch_shapes=[pltpu.SemaphoreType.DMA((7,)), pltpu.SemaphoreType.DMA((7,)), pltpu.SemaphoreType.DMA],
    )(x)


def _all_to_all(x, name):
    _, r, c = x.shape

    def body(x_ref, out_ref, send_sems, recv_sems, local_sem):
        mx, my, mc = _my_place()
        me = 4 * mx + 2 * my + mc
        mine = pltpu.make_async_copy(x_ref.at[me], out_ref.at[me], local_sem)
        mine.start()
        copies = []
        for k in range(1, N_DEV):
            px, py, pc = mx ^ (k >> 2), my ^ ((k >> 1) & 1), mc ^ (k & 1)
            peer = 4 * px + 2 * py + pc
            copies.append(pltpu.make_async_remote_copy(
                src_ref=x_ref.at[peer], dst_ref=out_ref.at[me],
                send_sem=send_sems.at[k - 1], recv_sem=recv_sems.at[k - 1],
                device_id=(px, py, pc), device_id_type=MESH_ID))
        for cp in copies:
            cp.start()
        for cp in copies:
            cp.wait_recv()
        for cp in copies:
            cp.wait_send()
        mine.wait()

    return _pcall(
        body, name=name,
        out_shape=jax.ShapeDtypeStruct((N_DEV, r, c), x.dtype),
        in_specs=[pl.BlockSpec(memory_space=pltpu.HBM)],
        out_specs=pl.BlockSpec(memory_space=pltpu.HBM),
        scratch_shapes=[pltpu.SemaphoreType.DMA((7,)), pltpu.SemaphoreType.DMA((7,)), pltpu.SemaphoreType.DMA],
    )(x)


def _tile(n, pref, unit=128):
    if n <= pref:
        return n
    t = pref - pref % unit
    while t > unit and n % t:
        t -= unit
    assert n % t == 0, (n, pref, unit)
    return t


def _mm(a, b, *, out_dtype, name, ta=False, tb=False, tm=512, tn=512, tk=512, precision=None,
        a_halves=False, b_halves=False, o_halves=False):
    a_shape = (a.shape[1], 2 * a.shape[2]) if a_halves else a.shape
    b_shape = (b.shape[1], 2 * b.shape[2]) if b_halves else b.shape
    (k_dim, m) = a_shape if ta else a_shape[::-1]
    n = b_shape[0] if tb else b_shape[1]
    tm, tn = _tile(m, tm, 8), _tile(n // 2 if (b_halves or o_halves) else n, tn)
    tk = _tile(k_dim // 2 if a_halves else k_dim, tk)
    nk = k_dim // tk
    dims = (((0 if ta else 1,), (1 if tb else 0,)), ((), ()))

    def body(a_ref, b_ref, o_ref, acc_ref):
        k = pl.program_id(2)

        @pl.when(k == 0)
        def _():
            acc_ref[...] = jnp.zeros_like(acc_ref)

        acc_ref[...] += lax.dot_general(a_ref[...], b_ref[...], dims, preferred_element_type=F32,
                                        precision=precision)

        @pl.when(k == nk - 1)
        def _():
            o_ref[...] = acc_ref[...].astype(o_ref.dtype)

    def spec(shape2, pick, halves, per_half):
        if not halves:
            return pl.BlockSpec(shape2, pick)

        def index(i, j, k):
            r, c = pick(i, j, k)
            return (c // per_half, r, c % per_half)
        return pl.BlockSpec((None,) + shape2, index)

    assert not (ta and a_halves) and not (tb and b_halves)
    a_spec = spec((tk, tm), lambda i, j, k: (k, i), False, 0) if ta else \
        spec((tm, tk), lambda i, j, k: (i, k), a_halves, (k_dim // 2) // tk if a_halves else 0)
    b_spec = spec((tn, tk), lambda i, j, k: (j, k), False, 0) if tb else \
        spec((tk, tn), lambda i, j, k: (k, j), b_halves, (n // 2) // tn if b_halves else 0)
    o_spec = spec((tm, tn), lambda i, j, k: (i, j), o_halves, (n // 2) // tn if o_halves else 0)
    out_shape = (2, m, n // 2) if o_halves else (m, n)
    return _pcall(
        body, name=name,
        out_shape=jax.ShapeDtypeStruct(out_shape, out_dtype),
        grid=(m // tm, n // tn, nk),
        in_specs=[a_spec, b_spec],
        out_specs=o_spec,
        scratch_shapes=[pltpu.VMEM((tm, tn), F32)],
        compiler_params=_cp("parallel", "parallel", "arbitrary"),
    )(a, b)


def _ln_stats(x):
    mu = jnp.mean(x, axis=-1, keepdims=True)
    xc = x - mu
    var = jnp.mean(xc * xc, axis=-1, keepdims=True)
    rstd = lax.rsqrt(var + LN_EPS)
    return xc * rstd, rstd


def _ln_mod(x, sc, sh, name, tm=512):
    s, d = x.shape
    tm = min(tm, s)

    def body(x_ref, sc_ref, sh_ref, h_ref):
        xhat, _ = _ln_stats(x_ref[...])
        h_ref[...] = (xhat * (1.0 + sc_ref[...]) + sh_ref[...]).astype(h_ref.dtype)

    row = pl.BlockSpec((1, d), lambda i: (0, 0))
    return _pcall(
        body, name=name, out_shape=jax.ShapeDtypeStruct((s, d), BF16), grid=(s // tm,),
        in_specs=[pl.BlockSpec((tm, d), lambda i: (i, 0)), row, row],
        out_specs=pl.BlockSpec((tm, d), lambda i: (i, 0)),
        compiler_params=_cp("parallel"),
    )(x, sc, sh)


def _mm_res_ln(a, w, xres, gate, ln_g, ln_b, sc, sh, alpha, name, tm=256, tk=512):
    s, k_dim = a.shape
    d = w.shape[1]
    tm, tk = _tile(s, tm, 8), _tile(k_dim, tk)
    nk = k_dim // tk

    def body(a_ref, w_ref, x_ref, g_ref, lg_ref, lb_ref, sc_ref, sh_ref, y_ref, xn_ref, h_ref, acc_ref):
        k = pl.program_id(1)

        @pl.when(k == 0)
        def _():
            acc_ref[...] = jnp.zeros_like(acc_ref)

        acc_ref[...] += jnp.dot(a_ref[...], w_ref[...], preferred_element_type=F32)

        @pl.when(k == nk - 1)
        def _():
            y = acc_ref[...]
            y_ref[...] = y
            z = alpha * x_ref[...] + (1.0 + g_ref[...]) * y
            zhat, _ = _ln_stats(z)
            xn = zhat * lg_ref[...] + lb_ref[...]
            xn_ref[...] = xn
            xhat, _ = _ln_stats(xn)
            h_ref[...] = (xhat * (1.0 + sc_ref[...]) + sh_ref[...]).astype(h_ref.dtype)

    row = pl.BlockSpec((1, d), lambda i, k: (0, 0))
    tile = pl.BlockSpec((tm, d), lambda i, k: (i, 0))
    return _pcall(
        body, name=name,
        out_shape=(jax.ShapeDtypeStruct((s, d), F32), jax.ShapeDtypeStruct((s, d), F32),
                   jax.ShapeDtypeStruct((s, d), BF16)),
        grid=(s // tm, nk),
        in_specs=[pl.BlockSpec((tm, tk), lambda i, k: (i, k)), pl.BlockSpec((tk, d), lambda i, k: (k, 0)),
                  tile, row, row, row, row, row],
        out_specs=(tile, tile, tile),
        scratch_shapes=[pltpu.VMEM((tm, d), F32)],
        compiler_params=_cp("parallel", "arbitrary"),
    )(a, w, xres, gate, ln_g, ln_b, sc, sh)


def _colsum(v):
    return jnp.sum(v, axis=0, keepdims=True)


def _ln_res_bwd(dxn, xres, y, gate, ln_g, alpha, name, tm=256):
    s, d = dxn.shape
    tm = min(tm, s)

    def body(dxn_ref, x_ref, y_ref, g_ref, lg_ref, dres_ref, dy_ref, sums_ref):
        @pl.when(pl.program_id(0) == 0)
        def _():
            sums_ref[...] = jnp.zeros_like(sums_ref)

        dxn_v, y_v = dxn_ref[...], y_ref[...]
        one_g = 1.0 + g_ref[...]
        zhat, rstd = _ln_stats(alpha * x_ref[...] + one_g * y_v)
        dzh = dxn_v * lg_ref[...]
        dz = rstd * (dzh - jnp.mean(dzh, axis=-1, keepdims=True)
                     - zhat * jnp.mean(dzh * zhat, axis=-1, keepdims=True))
        dres_ref[...] = alpha * dz
        dy_ref[...] = (one_g * dz).astype(dy_ref.dtype)
        sums_ref[0:1, :] += _colsum(dxn_v * zhat)
        sums_ref[1:2, :] += _colsum(dxn_v)
        sums_ref[2:3, :] += _colsum(dz * y_v)

    row = pl.BlockSpec((1, d), lambda i: (0, 0))
    tile = pl.BlockSpec((tm, d), lambda i: (i, 0))
    return _pcall(
        body, name=name,
        out_shape=(jax.ShapeDtypeStruct((s, d), F32), jax.ShapeDtypeStruct((s, d), BF16),
                   jax.ShapeDtypeStruct((8, d), F32)),
        grid=(s // tm,),
        in_specs=[tile, tile, tile, row, row],
        out_specs=(tile, tile, pl.BlockSpec((8, d), lambda i: (0, 0))),
        compiler_params=_cp("arbitrary"),
    )(dxn, xres, y, gate, ln_g)


def _ln_mod_bwd(dh, x, sc, dres, name, extra=None, tm=256):
    s, d = dh.shape
    tm = min(tm, s)

    def body(*refs):
        if extra is None:
            dh_ref, x_ref, sc_ref, dres_ref, dx_ref, sums_ref = refs
            dh_v = dh_ref[...]
        else:
            dh_ref, x_ref, sc_ref, dres_ref, dh2_ref, df_ref, wf_ref, dx_ref, sums_ref = refs
            dh_v = dh_ref[...] + dh2_ref[...] + jnp.dot(df_ref[...], wf_ref[...], preferred_element_type=F32)

        @pl.when(pl.program_id(0) == 0)
        def _():
            sums_ref[...] = jnp.zeros_like(sums_ref)

        xhat, rstd = _ln_stats(x_ref[...])
        dxh = dh_v * (1.0 + sc_ref[...])
        dx_ref[...] = dres_ref[...] + rstd * (dxh - jnp.mean(dxh, axis=-1, keepdims=True)
                                             - xhat * jnp.mean(dxh * xhat, axis=-1, keepdims=True))
        sums_ref[0:1, :] += _colsum(dh_v)
        sums_ref[1:2, :] += _colsum(dh_v * xhat)

    row = pl.BlockSpec((1, d), lambda i: (0, 0))
    tile = pl.BlockSpec((tm, d), lambda i: (i, 0))
    in_specs = [tile, tile, row, tile]
    args = [dh, x, sc, dres]
    if extra is not None:
        in_specs += [tile, pl.BlockSpec((tm, AUG), lambda i: (i, 0)), pl.BlockSpec((AUG, d), lambda i: (0, 0))]
        args += list(extra)
    return _pcall(
        body, name=name,
        out_shape=(jax.ShapeDtypeStruct((s, d), F32), jax.ShapeDtypeStruct((8, d), F32)),
        grid=(s // tm,), in_specs=in_specs,
        out_specs=(tile, pl.BlockSpec((8, d), lambda i: (0, 0))),
        compiler_params=_cp("arbitrary"),
    )(*args)


def _loss_grad(xn, target, name, tm=512):
    s, d = xn.shape
    tm = min(tm, s)
    n = s // tm

    def body(x_ref, t_ref, dx_ref, loss_ref, acc_ref):
        i = pl.program_id(0)

        @pl.when(i == 0)
        def _():
            acc_ref[...] = jnp.zeros_like(acc_ref)

        err = x_ref[...] - t_ref[...]
        dx_ref[...] = err * (1.0 / d)
        acc_ref[...] += _colsum(err * err)

        @pl.when(i == n - 1)
        def _():
            loss_ref[...] = jnp.zeros_like(loss_ref) + (0.5 / d) * jnp.sum(acc_ref[...])

    tile = pl.BlockSpec((tm, d), lambda i: (i, 0))
    return _pcall(
        body, name=name,
        out_shape=(jax.ShapeDtypeStruct((s, d), F32), jax.ShapeDtypeStruct((8, 128), F32)),
        grid=(n,), in_specs=[tile, tile],
        out_specs=(tile, pl.BlockSpec((8, 128), lambda i: (0, 0))),
        scratch_shapes=[pltpu.VMEM((1, d), F32)],
        compiler_params=_cp("arbitrary"),
    )(xn, target)


def _sigmoid(v):
    return 1.0 / (1.0 + jnp.exp(-v))


def _merge_fwd(o_a, o_c, w_a, w_c, gates, name, tm=256):
    s, wd = o_a.shape
    d = w_a.shape[1]
    tm = min(tm, s)

    def body(oa_ref, oc_ref, wa_ref, wc_ref, ga_ref, gc_ref, m_ref):
        ba = jnp.dot(oa_ref[...], wa_ref[...], preferred_element_type=F32)
        bc = jnp.dot(oc_ref[...], wc_ref[...], preferred_element_type=F32)
        m_ref[...] = (_sigmoid(ga_ref[...]) * ba + _sigmoid(gc_ref[...]) * bc).astype(m_ref.dtype)

    o_spec = pl.BlockSpec((tm, wd), lambda i: (i, 0))
    w_spec = pl.BlockSpec((wd, d), lambda i: (0, 0))
    return _pcall(
        body, name=name, out_shape=jax.ShapeDtypeStruct((s, d), BF16), grid=(s // tm,),
        in_specs=[o_spec, o_spec, w_spec, w_spec,
                  pl.BlockSpec((tm, d), lambda i: (i, 0)), pl.BlockSpec((tm, d), lambda i: (i, 1))],
        out_specs=pl.BlockSpec((tm, d), lambda i: (i, 0)),
        compiler_params=_cp("parallel"),
    )(o_a, o_c, w_a, w_c, gates, gates)


def _merge_bwd(dmix, w_out_t, o_a, o_c, w_a, w_c, w_a_t, w_c_t, gates, name, tm=256):
    s, wd = o_a.shape
    d = w_a.shape[1]
    tm = min(tm, s)

    def body(dmix_ref, wot_ref, oa_ref, oc_ref, wa_ref, wc_ref, wat_ref, wct_ref, ga_ref, gc_ref,
             dba_ref, dbc_ref, doa_ref, doc_ref, dga_ref, dgc_ref):
        dm = jnp.dot(dmix_ref[...], wot_ref[...], preferred_element_type=F32)
        for o_ref, w_ref, wt_ref, g_ref, db_ref, do_ref, dg_ref in (
                (oa_ref, wa_ref, wat_ref, ga_ref, dba_ref, doa_ref, dga_ref),
                (oc_ref, wc_ref, wct_ref, gc_ref, dbc_ref, doc_ref, dgc_ref)):
            b = jnp.dot(o_ref[...], w_ref[...], preferred_element_type=F32)
            sg = _sigmoid(g_ref[...])
            db = (dm * sg).astype(BF16)
            db_ref[...] = db
            dg_ref[...] = (dm * b * sg * (1.0 - sg)).astype(dg_ref.dtype)
            do_ref[...] = jnp.dot(db, wt_ref[...], preferred_element_type=F32)

    row_d = pl.BlockSpec((tm, d), lambda i: (i, 0))
    row_w = pl.BlockSpec((tm, wd), lambda i: (i, 0))
    full = lambda shp: pl.BlockSpec(shp, lambda i: (0, 0))
    return _pcall(
        body, name=name,
        out_shape=(jax.ShapeDtypeStruct((s, d), BF16), jax.ShapeDtypeStruct((s, d), BF16),
                   jax.ShapeDtypeStruct((s, wd), F32), jax.ShapeDtypeStruct((s, wd), F32),
                   jax.ShapeDtypeStruct((s, d), BF16), jax.ShapeDtypeStruct((s, d), BF16)),
        grid=(s // tm,),
        in_specs=[row_d, full((d, d)), row_w, row_w, full((wd, d)), full((wd, d)), full((d, wd)), full((d, wd)),
                  row_d, pl.BlockSpec((tm, d), lambda i: (i, 1))],
        out_specs=(row_d, row_d, row_w, row_w, row_d, row_d),
        compiler_params=_cp("parallel"),
    )(dmix, w_out_t, o_a, o_c, w_a, w_c, w_a_t, w_c_t, gates, gates)


def _shift_down(cur, k, fill_rows):
    out = pltpu.roll(cur, k, axis=0)
    rid = lax.broadcasted_iota(jnp.int32, cur.shape, 0)
    for r, fill in enumerate(fill_rows):
        out = jnp.where(rid == r, fill, out)
    return out


def _shift_up(cur, k, fill_rows):
    n = cur.shape[0]
    out = pltpu.roll(cur, n - k, axis=0)
    rid = lax.broadcasted_iota(jnp.int32, cur.shape, 0)
    for r, fill in enumerate(fill_rows):
        out = jnp.where(rid == n - k + r, fill, out)
    return out


def _conv_rows(cur, prev8, first, w, b):
    p6 = jnp.where(first, 0.0, prev8[6:7, :])
    p7 = jnp.where(first, 0.0, prev8[7:8, :])
    m1 = _shift_down(cur, 1, [p7])
    m2 = _shift_down(cur, 2, [p6, p7])
    u = b + w[0:1, :] * m2 + w[1:2, :] * m1 + w[2:3, :] * cur
    return u, m1, m2


def _conv_specs(tr, tc, order):
    r8 = tr // 8
    at = lambda f: (lambda *g: f(*order(*g)))
    return [pl.BlockSpec((2, tr, tc), at(lambda i, j: (0, i, j))),
            pl.BlockSpec((2, 8, tc), at(lambda i, j: (0, jnp.maximum(i * r8 - 1, 0), j))),
            pl.BlockSpec((2, 3, tc), at(lambda i, j: (0, 0, j))),
            pl.BlockSpec((2, 1, tc), at(lambda i, j: (0, 0, j)))]


def _conv_act_fwd(u0, conv_w, conv_b, name, tr=512, tc=256):
    _, s, f = u0.shape
    tr, tc = _tile(s, tr, 8), _tile(f, tc)

    def body(u_ref, p_ref, w_ref, b_ref, act_ref):
        first = pl.program_id(0) == 0
        a, _, _ = _conv_rows(u_ref[0], p_ref[0], first, w_ref[0], b_ref[0])
        v, _, _ = _conv_rows(u_ref[1], p_ref[1], first, w_ref[1], b_ref[1])
        act_ref[...] = (a * _sigmoid(a) * v).astype(act_ref.dtype)

    return _pcall(
        body, name=name, out_shape=jax.ShapeDtypeStruct((s, f), BF16), grid=(s // tr, f // tc),
        in_specs=_conv_specs(tr, tc, lambda i, j: (i, j)),
        out_specs=pl.BlockSpec((tr, tc), lambda i, j: (i, j)),
        compiler_params=_cp("parallel", "parallel"),
    )(u0, u0, conv_w, conv_b)


def _conv_act_bwd(dact, u0, conv_w, conv_b, name, tr=512, tc=256):
    _, s, f = u0.shape
    tr, tc = _tile(s, tr, 8), _tile(f, tc)

    def body(dact_ref, u_ref, p_ref, w_ref, b_ref, du_ref, sums_ref):
        first = pl.program_id(1) == 0

        @pl.when(first)
        def _():
            sums_ref[...] = jnp.zeros_like(sums_ref)

        ca, cv = u_ref[0], u_ref[1]
        a, a1, a2 = _conv_rows(ca, p_ref[0], first, w_ref[0], b_ref[0])
        v, v1, v2 = _conv_rows(cv, p_ref[1], first, w_ref[1], b_ref[1])
        sg = _sigmoid(a)
        dact_v = dact_ref[...]
        da = dact_v * v * sg * (1.0 + a * (1.0 - sg))
        dv = dact_v * a * sg
        du_ref[0] = da.astype(du_ref.dtype)
        du_ref[1] = dv.astype(du_ref.dtype)
        for hf, g, shifted in ((0, da, (a2, a1, ca)), (1, dv, (v2, v1, cv))):
            for r in range(3):
                sums_ref[hf, r:r + 1, :] += _colsum(g * shifted[r])
            sums_ref[hf, 3:4, :] += _colsum(g)

    return _pcall(
        body, name=name,
        out_shape=(jax.ShapeDtypeStruct((2, s, f), BF16), jax.ShapeDtypeStruct((2, 8, f), F32)),
        grid=(f // tc, s // tr),
        in_specs=[pl.BlockSpec((tr, tc), lambda j, i: (i, j))] + _conv_specs(tr, tc, lambda j, i: (i, j)),
        out_specs=(pl.BlockSpec((2, tr, tc), lambda j, i: (0, i, j)), pl.BlockSpec((2, 8, tc), lambda j, i: (0, 0, j))),
        compiler_params=_cp("parallel", "arbitrary"),
    )(dact, u0, u0, conv_w, conv_b)


def _conv_bwd_input(du, conv_w, name, tr=512, tc=512):
    _, s, f = du.shape
    tr, tc = _tile(s, tr, 16), _tile(f, tc)
    n = s // tr
    r16 = tr // 16

    def body(du_ref, nx_ref, w_ref, o_ref):
        last = pl.program_id(1) == n - 1
        cur = du_ref[...].astype(F32)
        nx = nx_ref[...].astype(F32)
        w = w_ref[...]
        n0 = jnp.where(last, 0.0, nx[0:1, :])
        n1 = jnp.where(last, 0.0, nx[1:2, :])
        p1 = _shift_up(cur, 1, [n0])
        p2 = _shift_up(cur, 2, [n0, n1])
        o_ref[...] = (w[2:3, :] * cur + w[1:2, :] * p1 + w[0:1, :] * p2).astype(o_ref.dtype)

    return _pcall(
        body, name=name, out_shape=jax.ShapeDtypeStruct((2, s, f), BF16), grid=(2, n, f // tc),
        in_specs=[pl.BlockSpec((None, tr, tc), lambda hf, i, j: (hf, i, j)),
                  pl.BlockSpec((None, 16, tc), lambda hf, i, j: (hf, jnp.minimum((i + 1) * r16, s // 16 - 1), j)),
                  pl.BlockSpec((None, 3, tc), lambda hf, i, j: (hf, 0, j))],
        out_specs=pl.BlockSpec((None, tr, tc), lambda hf, i, j: (hf, i, j)),
        compiler_params=_cp("parallel", "parallel", "parallel"),
    )(du, du, conv_w)


def _cumsum_rows(v, reverse=False):
    r = v.shape[0]
    i128 = lax.broadcasted_iota(jnp.int32, (128, 128), 0), lax.broadcasted_iota(jnp.int32, (128, 128), 1)
    ir = lax.broadcasted_iota(jnp.int32, (r, r), 0), lax.broadcasted_iota(jnp.int32, (r, r), 1)
    in_row = (i128[0] >= i128[1] if reverse else i128[0] <= i128[1]).astype(F32)
    rows = (ir[1] > ir[0] if reverse else ir[1] < ir[0]).astype(F32)
    within = jnp.dot(v, in_row, preferred_element_type=F32, precision=HI)
    tot = jnp.broadcast_to(within[:, 0:1] if reverse else within[:, 127:128], (r, 128))
    return within + jnp.dot(rows, tot, preferred_element_type=F32, precision=HI)


def _forget_fwd(f_t, b_f, name):
    h, r, _ = f_t.shape

    def body(f_ref, b_ref, o_ref):
        z = f_ref[...] + b_ref[...]
        logf = jnp.minimum(z, 0.0) - jnp.log(1.0 + jnp.exp(-jnp.abs(z)))
        rest = _cumsum_rows(logf)
        for i in range(N_SPLIT):
            piece = rest.astype(BF16).astype(F32)
            o_ref[i] = piece
            rest = rest - piece

    return _pcall(
        body, name=name, out_shape=jax.ShapeDtypeStruct((N_SPLIT, h, r, 128), F32), grid=(h,),
        in_specs=[pl.BlockSpec((None, r, 128), lambda i: (i, 0, 0)), pl.BlockSpec((None, 1, 128), lambda i: (i, 0, 0))],
        out_specs=pl.BlockSpec((N_SPLIT, None, r, 128), lambda i: (0, i, 0, 0)),
        compiler_params=_cp("parallel"),
    )(f_t, b_f)


def _forget_bwd(d_cum, f_t, b_f, name):
    h, r, _ = f_t.shape

    def body(g_ref, f_ref, b_ref, df_ref, db_ref):
        df = _cumsum_rows(g_ref[...], reverse=True) * _sigmoid(-(f_ref[...] + b_ref[...]))
        df_ref[...] = df
        db_ref[...] = jnp.zeros_like(db_ref) + jnp.sum(df)

    blk = pl.BlockSpec((None, r, 128), lambda i: (i, 0, 0))
    one = pl.BlockSpec((None, 1, 128), lambda i: (i, 0, 0))
    return _pcall(
        body, name=name,
        out_shape=(jax.ShapeDtypeStruct((h, r, 128), F32), jax.ShapeDtypeStruct((h, 1, 128), F32)),
        grid=(h,), in_specs=[blk, blk, one], out_specs=(blk, one),
        compiler_params=_cp("parallel"),
    )(d_cum, f_t, b_f)


_NT = (((1,), (1,)), ((), ()))


def _causal_keep(tk, tq):
    return lax.broadcasted_iota(jnp.int32, (tk, tq), 0) <= lax.broadcasted_iota(jnp.int32, (tk, tq), 1)


def _fox_fwd(q_aug, k_aug, v_t, name, t=TQ):
    h, s, _ = q_aug.shape
    dh = v_t.shape[1]
    t = min(t, s)
    n = s // t

    def body(q_ref, k_ref, v_ref, o_ref, lse_ref, m_s, l_s, acc_s):
        qi, kj = pl.program_id(1), pl.program_id(2)

        @pl.when(kj == 0)
        def _():
            m_s[...] = jnp.full_like(m_s, NEG)
            l_s[...] = jnp.zeros_like(l_s)
            acc_s[...] = jnp.zeros_like(acc_s)

        def step(diag):
            st = lax.dot_general(k_ref[...], q_ref[...], _NT, preferred_element_type=F32)
            if diag:
                st = jnp.where(_causal_keep(t, t), st, NEG)
            m_prev = m_s[...]
            m_new = jnp.maximum(m_prev, jnp.max(st, axis=0, keepdims=True))
            a = jnp.exp(m_prev - m_new)
            pt = jnp.exp(st - m_new)
            l_s[...] = a * l_s[...] + jnp.sum(pt, axis=0, keepdims=True)
            acc_s[...] = a * acc_s[...] + jnp.dot(v_ref[...], pt.astype(BF16), preferred_element_type=F32)
            m_s[...] = m_new

        @pl.when(kj < qi)
        def _():
            step(False)

        @pl.when(kj == qi)
        def _():
            step(True)
            o_ref[...] = acc_s[...] / l_s[...]
            lse_ref[...] = m_s[...] + jnp.log(l_s[...])

    return _pcall(
        body, name=name,
        out_shape=(jax.ShapeDtypeStruct((h, dh, s), F32), jax.ShapeDtypeStruct((h, 1, s), F32)),
        grid=(h, n, n),
        in_specs=[pl.BlockSpec((None, t, AUG), lambda hh, i, j: (hh, i, 0)),
                  pl.BlockSpec((None, t, AUG), lambda hh, i, j: (hh, jnp.minimum(i, j), 0)),
                  pl.BlockSpec((None, dh, t), lambda hh, i, j: (hh, 0, jnp.minimum(i, j)))],
        out_specs=(pl.BlockSpec((None, dh, t), lambda hh, i, j: (hh, 0, i)),
                   pl.BlockSpec((None, 1, t), lambda hh, i, j: (hh, 0, i))),
        scratch_shapes=[pltpu.VMEM((1, t), F32), pltpu.VMEM((1, t), F32), pltpu.VMEM((dh, t), F32)],
        compiler_params=_cp("parallel", "parallel", "arbitrary"),
    )(q_aug, k_aug, v_t)


def _row_dot(a_t, b_t, name, t=2048):
    h, dh, s = a_t.shape
    t = min(t, s)

    def body(a_ref, b_ref, o_ref):
        o_ref[...] = jnp.sum(a_ref[...] * b_ref[...], axis=0, keepdims=True)

    blk = pl.BlockSpec((None, dh, t), lambda hh, i: (hh, 0, i))
    return _pcall(
        body, name=name, out_shape=jax.ShapeDtypeStruct((h, 1, s), F32), grid=(h, s // t),
        in_specs=[blk, blk], out_specs=pl.BlockSpec((None, 1, t), lambda hh, i: (hh, 0, i)),
        compiler_params=_cp("parallel", "parallel"),
    )(a_t, b_t)


def _fox_bwd(q_aug, k_aug, v, k_ts, do, lse, delta, name, t=TQ):
    h, s, _ = q_aug.shape
    dh = v.shape[2]
    t = min(t, s)
    n = s // t

    def body(q_ref, k_ref, v_ref, kt_ref, do_ref, lse_ref, dl_ref, dk_ref, dv_ref, dq_ref, dsum_ref, dqsum_ref):
        kj, qi = pl.program_id(1), pl.program_id(2)

        @pl.when(jnp.logical_and(kj == 0, qi == 0))
        def _():
            dq_ref[...] = jnp.zeros_like(dq_ref)
            dqsum_ref[...] = jnp.zeros_like(dqsum_ref)

        @pl.when(qi == 0)
        def _():
            dk_ref[...] = jnp.zeros_like(dk_ref)
            dv_ref[...] = jnp.zeros_like(dv_ref)
            dsum_ref[...] = jnp.zeros_like(dsum_ref)

        def step(diag):
            q = q_ref[...]
            st = lax.dot_general(k_ref[...], q, _NT, preferred_element_type=F32)
            if diag:
                st = jnp.where(_causal_keep(t, t), st, NEG)
            pt = jnp.exp(st - lse_ref[...])
            do_v = do_ref[...]
            dpt = lax.dot_general(v_ref[...], do_v, _NT, preferred_element_type=F32)
            ds32 = pt * (dpt - dl_ref[...])
            dsum_ref[...] += sum(ds32[:, 128 * u:128 * (u + 1)] for u in range(t // 128))
            dqsum_ref[qi] += jnp.sum(ds32, axis=0, keepdims=True)
            dst = ds32.astype(BF16)
            dv_ref[...] += jnp.dot(pt.astype(BF16), do_v, preferred_element_type=F32)
            dk_ref[...] += jnp.dot(dst, q, preferred_element_type=F32)
            dq_ref[qi] += jnp.dot(kt_ref[...], dst, preferred_element_type=F32)

        @pl.when(qi > kj)
        def _():
            step(False)

        @pl.when(qi == kj)
        def _():
            step(True)

    qmap = lambda hh, j, i: (hh, jnp.maximum(i, j), 0)
    rowq = lambda hh, j, i: (hh, 0, jnp.maximum(i, j))
    return _pcall(
        body, name=name,
        out_shape=(jax.ShapeDtypeStruct((h, s, AUG), F32), jax.ShapeDtypeStruct((h, s, dh), F32),
                   jax.ShapeDtypeStruct((h, n, dh, t), F32), jax.ShapeDtypeStruct((h, s, 128), F32),
                   jax.ShapeDtypeStruct((h, n, 1, t), F32)),
        grid=(h, n, n),
        in_specs=[pl.BlockSpec((None, t, AUG), qmap),
                  pl.BlockSpec((None, t, AUG), lambda hh, j, i: (hh, j, 0)),
                  pl.BlockSpec((None, t, dh), lambda hh, j, i: (hh, j, 0)),
                  pl.BlockSpec((None, dh, t), lambda hh, j, i: (hh, 0, j)),
                  pl.BlockSpec((None, t, dh), qmap),
                  pl.BlockSpec((None, 1, t), rowq), pl.BlockSpec((None, 1, t), rowq)],
        out_specs=(pl.BlockSpec((None, t, AUG), lambda hh, j, i: (hh, j, 0)),
                   pl.BlockSpec((None, t, dh), lambda hh, j, i: (hh, j, 0)),
                   pl.BlockSpec((None, n, dh, t), lambda hh, j, i: (hh, 0, 0, 0)),
                   pl.BlockSpec((None, t, 128), lambda hh, j, i: (hh, j, 0)),
                   pl.BlockSpec((None, n, 1, t), lambda hh, j, i: (hh, 0, 0, 0))),
        compiler_params=_cp("parallel", "arbitrary", "arbitrary"),
    )(q_aug, k_aug, v, k_ts, do, lse, delta)


def _chunk_logits(kw, q_ref, bias, first_tile, j):
    ks = kw[SUB * j:SUB * j + SUBK]
    st = lax.dot_general(ks, q_ref[SUB * j:SUB * (j + 1), :], _NT, preferred_element_type=F32) + bias
    row = lax.broadcasted_iota(jnp.int32, st.shape, 0)
    return jnp.where(jnp.logical_or(jnp.logical_not(first_tile), row >= TQ - SUB * j), st, NEG), ks


def _chunk_fwd(q, k, v_t, bias_t, name):
    h, s, dh = q.shape
    n = s // TQ
    nsub = TQ // SUB
    prev = lambda i: jnp.maximum(i - 1, 0)

    def body(q_ref, kp_ref, kc_ref, vp_ref, vc_ref, b_ref, o_ref, lse_ref):
        first_tile = pl.program_id(1) == 0
        kw = jnp.concatenate([kp_ref[...], kc_ref[...]], axis=0)
        vw = jnp.concatenate([vp_ref[...], vc_ref[...]], axis=1)
        bias = b_ref[...]
        for j in range(nsub):
            st, _ = _chunk_logits(kw, q_ref, bias, first_tile, j)
            m = jnp.max(st, axis=0, keepdims=True)
            pt = jnp.exp(st - m)
            l = jnp.sum(pt, axis=0, keepdims=True)
            ot = jnp.dot(vw[:, SUB * j:SUB * j + SUBK], pt.astype(BF16), preferred_element_type=F32)
            o_ref[:, SUB * j:SUB * (j + 1)] = ot / l
            lse_ref[:, SUB * j:SUB * (j + 1)] = m + jnp.log(l)

    return _pcall(
        body, name=name,
        out_shape=(jax.ShapeDtypeStruct((h, dh, s), F32), jax.ShapeDtypeStruct((h, 1, s), F32)),
        grid=(h, n),
        in_specs=[pl.BlockSpec((None, TQ, dh), lambda hh, i: (hh, i, 0)),
                  pl.BlockSpec((None, TQ, dh), lambda hh, i: (hh, prev(i), 0)),
                  pl.BlockSpec((None, TQ, dh), lambda hh, i: (hh, i, 0)),
                  pl.BlockSpec((None, dh, TQ), lambda hh, i: (hh, 0, prev(i))),
                  pl.BlockSpec((None, dh, TQ), lambda hh, i: (hh, 0, i)),
                  pl.BlockSpec((None, SUBK, SUB), lambda hh, i: (hh, 0, 0))],
        out_specs=(pl.BlockSpec((None, dh, TQ), lambda hh, i: (hh, 0, i)),
                   pl.BlockSpec((None, 1, TQ), lambda hh, i: (hh, 0, i))),
        compiler_params=_cp("parallel", "parallel"),
    )(q, k, k, v_t, v_t, bias_t)


def _chunk_bwd(q, k, v, k_ts, do, lse, delta, bias_t, name):
    h, s, dh = q.shape
    n = s // TQ
    nsub = TQ // SUB
    cur = lambda i: jnp.minimum(i, n - 1)
    prev = lambda i: jnp.maximum(cur(i) - 1, 0)
    done = lambda i: jnp.maximum(i - 1, 0)

    def body(q_ref, kp_ref, kc_ref, vp_ref, vc_ref, ktp_ref, ktc_ref, do_ref, lse_ref, dl_ref, b_ref,
             dq_ref, dk_ref, dv_ref, db_ref, dkw_s, dvw_s, ck_s, cv_s):
        i = pl.program_id(1)

        @pl.when(i == 0)
        def _():
            db_ref[...] = jnp.zeros_like(db_ref)
            ck_s[...] = jnp.zeros_like(ck_s)
            cv_s[...] = jnp.zeros_like(cv_s)

        dkw_s[...] = jnp.zeros_like(dkw_s)
        dvw_s[...] = jnp.zeros_like(dvw_s)

        @pl.when(i < n)
        def _():
            first_tile = i == 0
            kw = jnp.concatenate([kp_ref[...], kc_ref[...]], axis=0)
            vw = jnp.concatenate([vp_ref[...], vc_ref[...]], axis=0)
            ktw = jnp.concatenate([ktp_ref[...], ktc_ref[...]], axis=1)
            bias = b_ref[...]
            for j in range(nsub):
                qs = pl.ds(SUB * j, SUB)
                win = pl.ds(SUB * j, SUBK)
                st, _ = _chunk_logits(kw, q_ref, bias, first_tile, j)
                pt = jnp.exp(st - lse_ref[:, qs])
                do_j = do_ref[qs, :]
                dpt = lax.dot_general(vw[SUB * j:SUB * j + SUBK], do_j, _NT, preferred_element_type=F32)
                dst = pt * (dpt - dl_ref[:, qs])
                db_ref[...] += dst
                dsb = dst.astype(BF16)
                dvw_s[win, :] += jnp.dot(pt.astype(BF16), do_j, preferred_element_type=F32)
                dkw_s[win, :] += jnp.dot(dsb, q_ref[qs, :], preferred_element_type=F32)
                dq_ref[:, qs] = jnp.dot(ktw[:, SUB * j:SUB * j + SUBK], dsb, preferred_element_type=F32)

        dk_ref[...] = ck_s[...] + dkw_s[0:TQ, :]
        dv_ref[...] = cv_s[...] + dvw_s[0:TQ, :]
        ck_s[...] = dkw_s[TQ:2 * TQ, :]
        cv_s[...] = dvw_s[TQ:2 * TQ, :]

    tm = lambda m: pl.BlockSpec((None, TQ, dh), lambda hh, i: (hh, m(i), 0))
    ft = lambda m: pl.BlockSpec((None, dh, TQ), lambda hh, i: (hh, 0, m(i)))
    rw = pl.BlockSpec((None, 1, TQ), lambda hh, i: (hh, 0, cur(i)))
    bs = pl.BlockSpec((None, SUBK, SUB), lambda hh, i: (hh, 0, 0))
    return _pcall(
        body, name=name,
        out_shape=(jax.ShapeDtypeStruct((h, dh, s), F32), jax.ShapeDtypeStruct((h, s, dh), F32),
                   jax.ShapeDtypeStruct((h, s, dh), F32), jax.ShapeDtypeStruct((h, SUBK, SUB), F32)),
        grid=(h, n + 1),
        in_specs=[tm(cur), tm(prev), tm(cur), tm(prev), tm(cur), ft(prev), ft(cur), tm(cur), rw, rw, bs],
        out_specs=(ft(cur), tm(done), tm(done), bs),
        scratch_shapes=[pltpu.VMEM((2 * TQ, dh), F32), pltpu.VMEM((2 * TQ, dh), F32),
                        pltpu.VMEM((TQ, dh), F32), pltpu.VMEM((TQ, dh), F32)],
        compiler_params=_cp("parallel", "arbitrary"),
    )(q, k, k, v, v, k_ts, k_ts, do, lse, delta, bias_t)


def _mod_part(c_all, w_ada, b_ada, name):
    nl, d, n = w_ada.shape
    b = c_all.shape[0]

    def body(c_ref, w_ref, b_ref, o_ref, cond_ref):
        cv = c_ref[...]
        cond = cv * _sigmoid(cv)
        cond_ref[...] = cond
        o_ref[...] = jnp.dot(cond, w_ref[...], preferred_element_type=F32, precision=HI) + b_ref[...]

    return _pcall(
        body, name=name,
        out_shape=(jax.ShapeDtypeStruct((nl, b, n), F32), jax.ShapeDtypeStruct((b, d), F32)),
        grid=(nl,),
        in_specs=[pl.BlockSpec((b, d), lambda l: (0, 0)), pl.BlockSpec((None, d, n), lambda l: (l, 0, 0)),
                  pl.BlockSpec((None, 1, n), lambda l: (l, 0, 0))],
        out_specs=(pl.BlockSpec((None, b, n), lambda l: (l, 0, 0)), pl.BlockSpec((b, d), lambda l: (0, 0))),
        compiler_params=_cp("arbitrary"),
    )(c_all, w_ada, b_ada)


def _adamw(parts, w, m, v, name, tr=1024):
    p, r, c = parts.shape
    tr = _tile(r, tr, 16)
    c1 = 1.0 / (1.0 - ADAM_B1 ** ADAM_STEP)
    c2 = 1.0 / (1.0 - ADAM_B2 ** ADAM_STEP)

    def body(p_ref, w_ref, m_ref, v_ref, g_ref, d_ref, nm_ref, nv_ref):
        g = p_ref[0].astype(F32)
        for i in range(1, p):
            g = g + p_ref[i].astype(F32)
        nm = ADAM_B1 * m_ref[...] + (1.0 - ADAM_B1) * g
        nv = ADAM_B2 * v_ref[...] + (1.0 - ADAM_B2) * (g * g)
        g_ref[...] = g
        nm_ref[...] = nm
        nv_ref[...] = nv
        d_ref[...] = -ADAM_LR * ((nm * c1) / (jnp.sqrt(nv * c2) + ADAM_EPS) + ADAM_WD * w_ref[...])

    blk = pl.BlockSpec((tr, c), lambda i: (i, 0))
    out = jax.ShapeDtypeStruct((r, c), F32)
    return _pcall(
        body, name=name, out_shape=(out, out, out, out), grid=(r // tr,),
        in_specs=[pl.BlockSpec((p, tr, c), lambda i: (0, i, 0)), blk, blk, blk],
        out_specs=(blk, blk, blk, blk),
        compiler_params=_cp("parallel"),
    )(parts, w, m, v)


def _rows128(a, mult=16):
    flat = a.reshape(-1)
    n = flat.shape[0]
    per = 128 * mult
    pad = (-n) % per
    if pad:
        flat = jnp.concatenate([flat, jnp.zeros((pad,), a.dtype)])
    return flat.reshape(-1, 128)


def _pack(arrs, mult=16):
    pieces = [_rows128(a, mult) for a in arrs]
    return jnp.concatenate(pieces, axis=0), [p.shape[0] for p in pieces]


def _unpack(packed, rows, shapes, lead=()):
    out, at = [], 0
    for r, shp in zip(rows, shapes):
        n = int(np.prod(shp))
        piece = packed[..., at:at + r, :].reshape(lead + (r * 128,))[..., :n]
        out.append(piece.reshape(lead + tuple(shp)))
        at += r
    return out


def _heads(t):
    s = t.shape[0]
    return t.reshape(s, N_HEADS, HEAD_DIM).transpose(1, 0, 2)


def _unheads(t):
    return t.transpose(1, 0, 2).reshape(t.shape[1], WIDTH)


def _rel_bias_tile(table):
    h = table.shape[0]
    lo = REL_CLIP - (CHUNK - 1)
    n_far = LEFT_CHUNKS * CHUNK + CHUNK - 1 - REL_CLIP
    vec = jnp.concatenate([table[:, lo:2 * REL_CLIP], jnp.repeat(table[:, 2 * REL_CLIP:], n_far + 1, axis=1)], axis=1)
    rev = vec[:, ::-1]
    bias = jnp.stack([rev[:, CHUNK - 1 - q:CHUNK - 1 - q + BAND] for q in range(CHUNK)], axis=1)
    neg = jnp.full((h, CHUNK, CHUNK), NEG, F32)
    two = jnp.concatenate([jnp.concatenate([bias, neg], axis=2), jnp.concatenate([neg, bias], axis=2)], axis=1)
    return two.transpose(0, 2, 1)


def _rel_bias_tile_grad(dbias_t):
    h = dbias_t.shape[0]
    two = dbias_t.transpose(0, 2, 1)
    dbias = two[:, :CHUNK, :BAND] + two[:, CHUNK:, CHUNK:]
    n_vec = BAND + CHUNK - 1
    drev = sum(jnp.pad(dbias[:, q, :], ((0, 0), (CHUNK - 1 - q, n_vec - BAND - (CHUNK - 1 - q))))
               for q in range(CHUNK))
    dvec = drev[:, ::-1]
    lo = REL_CLIP - (CHUNK - 1)
    n_near = 2 * REL_CLIP - lo
    return jnp.concatenate([jnp.zeros((h, lo), F32), dvec[:, :n_near],
                            jnp.sum(dvec[:, n_near:], axis=1, keepdims=True)], axis=1)


BIG = ("w_in", "w_br_fox", "w_br_chunk", "w_out", "w_up", "w_down")
SMALL = ("b_f", "rel_bias", "conv_b", "b_ada", "ln1_g", "ln1_b", "ln2_g", "ln2_b")
ORDER = ("w_in", "b_f", "rel_bias", "w_br_fox", "w_br_chunk", "w_out", "w_up", "conv_w", "conv_b", "w_down",
         "w_ada", "b_ada", "ln1_g", "ln1_b", "ln2_g", "ln2_b")


def _dest_major(g, axis):
    shp = g.shape
    g = g.reshape(shp[:axis] + (N_DEV, shp[axis] // N_DEV) + shp[axis + 1:])
    return jnp.moveaxis(g, axis, 0)


def _layer_fwd(x, h1, wl, mod, next_mod, alpha):
    sh1, sc1, g1, sh2, sc2, g2 = mod
    s = x.shape[0]
    qkv = _mm(h1, wl["w_qkv"], out_dtype=BF16, name="proj_qkv", tk=1024)
    gates = _mm(h1, wl["w_gates"], out_dtype=F32, name="proj_gates", tk=1024)
    f_t = _mm(wl["w_f_t"], h1, out_dtype=F32, name="proj_f", tb=True, tk=1024)[:N_HEADS]
    f_t = f_t.reshape(N_HEADS, s // 128, 128)
    b_f = jnp.broadcast_to(wl["b_f"].reshape(N_HEADS, 1, 1), (N_HEADS, 1, 128))
    cum = _forget_fwd(f_t, b_f, "forget_fwd").reshape(N_SPLIT, N_HEADS, s)
    cum_cols = cum.transpose(1, 2, 0).astype(BF16)

    q_a, k_a, v_a, q_c, k_c, v_c = [_heads(qkv[:, WIDTH * i:WIDTH * (i + 1)]) for i in range(6)]
    ones = jnp.ones((N_HEADS, s, N_SPLIT), BF16)
    zeros = jnp.zeros((N_HEADS, s, AUG - HEAD_DIM - 2 * N_SPLIT), BF16)
    q_aug = jnp.concatenate([q_a * QK_SCALE, cum_cols, ones, zeros], axis=-1)
    k_aug = jnp.concatenate([k_a, ones, -cum_cols, zeros], axis=-1)
    o_a_t, lse_a = _fox_fwd(q_aug, k_aug, v_a.transpose(0, 2, 1), "fox_fwd")
    q_cs = q_c * QK_SCALE
    bias_t = _rel_bias_tile(wl["rel_bias"])
    o_c_t, lse_c = _chunk_fwd(q_cs, k_c, v_c.transpose(0, 2, 1), bias_t, "chunk_fwd")
    o_a = o_a_t.transpose(2, 0, 1).reshape(s, WIDTH).astype(BF16)
    o_c = o_c_t.transpose(2, 0, 1).reshape(s, WIDTH).astype(BF16)

    merged = _merge_fwd(o_a, o_c, wl["w_br_fox"], wl["w_br_chunk"], gates, "merge_fwd")
    mix, x1, h2 = _mm_res_ln(merged, wl["w_out"], x, g1, wl["ln1_g"], wl["ln1_b"], sc2, sh2, alpha, "out_ln1")
    u0 = _mm(h2, wl["w_up"], out_dtype=F32, name="ffn_up", tk=1024, tn=1408, o_halves=True)
    act = _conv_act_fwd(u0, wl["conv_w"], wl["conv_b"], "conv_act_fwd")
    y2, x2, h_next = _mm_res_ln(act, wl["w_down"], x1, g2, wl["ln2_g"], wl["ln2_b"], next_mod[0], next_mod[1],
                                alpha, "down_ln2", tk=1408)
    saved = dict(x=x, h1=h1, gates=gates, f_t=f_t, b_f=b_f, q_aug=q_aug, k_aug=k_aug, v_a=v_a, k_a=k_a,
                 q_cs=q_cs, k_c=k_c, v_c=v_c, bias_t=bias_t, o_a_t=o_a_t, o_c_t=o_c_t, lse_a=lse_a, lse_c=lse_c,
                 o_a=o_a, o_c=o_c, merged=merged, mix=mix, x1=x1, h2=h2, u0=u0, act=act, y2=y2)
    return x2, h_next, saved


def _layer_bwd(dx2, sv, wl, mod, alpha):
    sh1, sc1, g1, sh2, sc2, g2 = mod
    s = dx2.shape[0]
    g = {}
    dres2, dy2, sums = _ln_res_bwd(dx2, sv["x1"], sv["y2"], g2, wl["ln2_g"], alpha, "ln2_bwd")
    g["ln2_g"], g["ln2_b"], dg2 = sums[0], sums[1], sums[2]
    dact = _mm(dy2, wl["w_down_t"], out_dtype=F32, name="dact", tk=1024, tn=1408)
    g["w_down"] = _mm(sv["act"], dy2, out_dtype=F32, name="g_w_down", ta=True, tm=1408, tk=1024)
    du, csum = _conv_act_bwd(dact, sv["u0"], wl["conv_w"], wl["conv_b"], "conv_act_bwd")
    g["conv_w"] = jnp.concatenate([csum[0, 0:3], csum[1, 0:3]], axis=1)
    g["conv_b"] = jnp.concatenate([csum[0, 3], csum[1, 3]])
    du0 = _conv_bwd_input(du, wl["conv_w"], "conv_bwd_input")
    dh2 = _mm(du0, wl["w_up_t"], out_dtype=F32, name="dh2", tk=1408, a_halves=True)
    g["w_up"] = _mm(sv["h2"], du0, out_dtype=F32, name="g_w_up", ta=True, tk=1024, tn=1408, b_halves=True)
    dx1, sums = _ln_mod_bwd(dh2, sv["x1"], sc2, dres2, "ln_mod2_bwd")
    dsh2, dsc2 = sums[0], sums[1]

    dres1, dmix, sums = _ln_res_bwd(dx1, sv["x"], sv["mix"], g1, wl["ln1_g"], alpha, "ln1_bwd")
    g["ln1_g"], g["ln1_b"], dg1 = sums[0], sums[1], sums[2]
    g["w_out"] = _mm(sv["merged"], dmix, out_dtype=F32, name="g_w_out", ta=True, tk=1024)
    dba, dbc, do_a, do_c, dga, dgc = _merge_bwd(
        dmix, wl["w_out_t"], sv["o_a"], sv["o_c"], wl["w_br_fox"], wl["w_br_chunk"],
        wl["w_br_fox_t"], wl["w_br_chunk_t"], sv["gates"], "merge_bwd")
    g["w_br_fox"] = _mm(sv["o_a"], dba, out_dtype=F32, name="g_w_br_fox", ta=True, tk=1024)
    g["w_br_chunk"] = _mm(sv["o_c"], dbc, out_dtype=F32, name="g_w_br_chunk", ta=True, tk=1024)

    do_a_h = _heads(do_a).astype(BF16)
    delta_a = _row_dot(do_a_h.astype(F32).transpose(0, 2, 1), sv["o_a_t"], "delta_fox")
    k_ts_a = (sv["k_a"] * QK_SCALE).transpose(0, 2, 1)
    dk_aug, dv_a, dq_a_t, dsum, dqsum = _fox_bwd(sv["q_aug"], sv["k_aug"], sv["v_a"], k_ts_a, do_a_h,
                                                 sv["lse_a"], delta_a, "fox_bwd")
    dq_a = dq_a_t.transpose(1, 3, 0, 2).reshape(s, WIDTH)
    dk_a = _unheads(dk_aug[:, :, :HEAD_DIM])
    d_cum = (dqsum.reshape(N_HEADS, s) - jnp.sum(dsum, axis=-1)).reshape(N_HEADS, s // 128, 128)
    df_t, db_f = _forget_bwd(d_cum, sv["f_t"], sv["b_f"], "forget_bwd")
    g["b_f"] = db_f[:, 0, 0]
    df_t = df_t.reshape(N_HEADS, s)

    do_c_h = _heads(do_c).astype(BF16)
    delta_c = _row_dot(do_c_h.astype(F32).transpose(0, 2, 1), sv["o_c_t"], "delta_chunk")
    k_ts_c = (sv["k_c"] * QK_SCALE).transpose(0, 2, 1)
    dq_c_t, dk_c, dv_c, dbias_t = _chunk_bwd(sv["q_cs"], sv["k_c"], sv["v_c"], k_ts_c, do_c_h,
                                             sv["lse_c"], delta_c, sv["bias_t"], "chunk_bwd")
    g["rel_bias"] = _rel_bias_tile_grad(dbias_t)
    dq_c = dq_c_t.transpose(2, 0, 1).reshape(s, WIDTH)

    dqkv = jnp.concatenate([dq_a, dk_a, _unheads(dv_a), dq_c, _unheads(dk_c), _unheads(dv_c)], axis=1).astype(BF16)
    dgates = jnp.concatenate([dga, dgc], axis=1)
    df_pad = jnp.zeros((16 - N_HEADS, s), F32)
    df16 = jnp.concatenate([df_t, df_pad], axis=0).astype(BF16)
    df_cols = jnp.concatenate([df16.T, jnp.zeros((s, AUG - 16), BF16)], axis=1)

    dh1 = _mm(dqkv, wl["w_qkv_t"], out_dtype=F32, name="dh1_qkv", tk=1024)
    dh1g = _mm(dgates, wl["w_gates_t"], out_dtype=F32, name="dh1_gates", tk=1024)
    g_qkv = _mm(sv["h1"], dqkv, out_dtype=F32, name="g_w_qkv", ta=True, tk=1024)
    g_gates = _mm(sv["h1"], dgates, out_dtype=F32, name="g_w_gates", ta=True, tk=1024)
    g_f_t = _mm(df16, sv["h1"], out_dtype=F32, name="g_w_f", tk=1024)[:N_HEADS]
    g["w_in"] = jnp.concatenate([g_qkv[:, :3 * WIDTH], g_f_t.T, g_qkv[:, 3 * WIDTH:], g_gates], axis=1)
    dx, sums = _ln_mod_bwd(dh1, sv["x"], sc1, dres1, "ln_mod1_bwd", extra=(dh1g, df_cols, wl["w_f_pad"]))
    dsh1, dsc1 = sums[0], sums[1]
    g["mod"] = jnp.concatenate([dsh1, dsc1, dg1, dsh2, dsc2, dg2])
    return dx, g


def kernel(x, c, w_in, b_f, rel_bias, w_br_fox, w_br_chunk, w_out, w_up, conv_w, conv_b, w_down, w_ada, b_ada, ln1_g, ln1_b, ln2_g, ln2_b, loss_target, m_w_in, m_b_f, m_rel_bias, m_w_br_fox, m_w_br_chunk, m_w_out, m_w_up, m_conv_w, m_conv_b, m_w_down, m_w_ada, m_b_ada, m_ln1_g, m_ln1_b, m_ln2_g, m_ln2_b, v_w_in, v_b_f, v_rel_bias, v_w_br_fox, v_w_br_chunk, v_w_out, v_w_up, v_conv_w, v_conv_b, v_w_down, v_w_ada, v_b_ada, v_ln1_g, v_ln1_b, v_ln2_g, v_ln2_b):
    w = dict(w_in=w_in, b_f=b_f, rel_bias=rel_bias, w_br_fox=w_br_fox, w_br_chunk=w_br_chunk, w_out=w_out,
             w_up=w_up, conv_w=conv_w, conv_b=conv_b, w_down=w_down, w_ada=w_ada, b_ada=b_ada,
             ln1_g=ln1_g, ln1_b=ln1_b, ln2_g=ln2_g, ln2_b=ln2_b)
    m = dict(w_in=m_w_in, b_f=m_b_f, rel_bias=m_rel_bias, w_br_fox=m_w_br_fox, w_br_chunk=m_w_br_chunk,
             w_out=m_w_out, w_up=m_w_up, conv_w=m_conv_w, conv_b=m_conv_b, w_down=m_w_down, w_ada=m_w_ada,
             b_ada=m_b_ada, ln1_g=m_ln1_g, ln1_b=m_ln1_b, ln2_g=m_ln2_g, ln2_b=m_ln2_b)
    v = dict(w_in=v_w_in, b_f=v_b_f, rel_bias=v_rel_bias, w_br_fox=v_w_br_fox, w_br_chunk=v_w_br_chunk,
             w_out=v_w_out, w_up=v_w_up, conv_w=v_conv_w, conv_b=v_conv_b, w_down=v_w_down, w_ada=v_w_ada,
             b_ada=v_b_ada, ln1_g=v_ln1_g, ln1_b=v_ln1_b, ln2_g=v_ln2_g, ln2_b=v_ln2_b)
    depth, d, _ = w_in.shape
    s = x.shape[1]
    alpha = (2.0 * depth) ** 0.25
    me = 4 * lax.axis_index("x") + 2 * lax.axis_index("y") + lax.axis_index("c")
    x0 = x.reshape(s, d)
    target = loss_target.reshape(s, d)

    small_in, small_rows = _pack([c, conv_w], mult=8)
    small_all = _all_gather(small_in, "gather_c_conv")
    c_all, conv_w_all = _unpack(small_all, small_rows, [c.shape, conv_w.shape], lead=(N_DEV,))
    c_all = c_all.reshape(N_DEV, d)
    conv_w_full = conv_w_all.transpose(1, 2, 0, 3).reshape(depth, conv_w.shape[1], -1)
    n_ada = w_ada.shape[2]
    b_ada_mine = lax.dynamic_slice_in_dim(b_ada, me * n_ada, n_ada, axis=1).reshape(depth, 1, n_ada)
    mod_part, cond_all = _mod_part(c_all, w_ada, b_ada_mine, "mod_part")
    mod_all = _all_gather(mod_part.reshape(depth * N_DEV, n_ada), "gather_mod")
    mod_all = mod_all.reshape(N_DEV, depth, N_DEV, n_ada)
    mod_mine = lax.dynamic_index_in_dim(mod_all, me, axis=2, keepdims=False)
    mod_mine = mod_mine.transpose(1, 0, 2).reshape(depth, N_MOD, 1, d)

    w_pack, w_rows = _pack([w[n].astype(BF16) for n in BIG])
    w_all = _all_gather(w_pack, "gather_weights")
    parts = _unpack(w_all, w_rows, [w[n].shape for n in BIG], lead=(N_DEV,))
    full = {}
    for n, p in zip(BIG, parts):
        axis = 1 if n in ("w_out", "w_down") else 2
        p = jnp.moveaxis(p, 0, axis)
        shp = list(w[n].shape)
        shp[axis] *= N_DEV
        full[n] = p.reshape(shp)
    cols = np.cumsum([0, WIDTH, WIDTH, WIDTH, N_HEADS, WIDTH, WIDTH, WIDTH, d, d])
    layers = []
    for l in range(depth):
        wi = full["w_in"][l]
        w_qkv = jnp.concatenate([wi[:, :cols[3]], wi[:, cols[4]:cols[7]]], axis=1)
        w_gates = wi[:, cols[7]:]
        w_f_t = jnp.concatenate([wi[:, cols[3]:cols[4]].T, jnp.zeros((16 - N_HEADS, d), BF16)], axis=0)
        w_f_pad = jnp.concatenate([w_f_t, jnp.zeros((AUG - 16, d), BF16)], axis=0)
        row = lambda a: a[l].reshape(1, -1)
        layers.append(dict(
            w_qkv=w_qkv, w_gates=w_gates, w_f_t=w_f_t, w_f_pad=w_f_pad, w_qkv_t=w_qkv.T, w_gates_t=w_gates.T,
            w_br_fox=full["w_br_fox"][l], w_br_chunk=full["w_br_chunk"][l],
            w_br_fox_t=full["w_br_fox"][l].T, w_br_chunk_t=full["w_br_chunk"][l].T,
            w_out=full["w_out"][l], w_out_t=full["w_out"][l].T, w_up=full["w_up"][l], w_up_t=full["w_up"][l].T,
            w_down=full["w_down"][l], w_down_t=full["w_down"][l].T,
            conv_w=conv_w_full[l].reshape(3, 2, -1).transpose(1, 0, 2), conv_b=conv_b[l].reshape(2, 1, -1),
            b_f=b_f[l], rel_bias=rel_bias[l],
            ln1_g=row(ln1_g), ln1_b=row(ln1_b), ln2_g=row(ln2_g), ln2_b=row(ln2_b)))

    xs, saved = x0, []
    hs = _ln_mod(x0, mod_mine[0][1], mod_mine[0][0], "ln_mod1")
    for l in range(depth):
        nxt = mod_mine[min(l + 1, depth - 1)]
        xs, hs, sv = _layer_fwd(xs, hs, layers[l], list(mod_mine[l]), (nxt[1], nxt[0]), alpha)
        saved.append(sv)
    dxs, loss_part = _loss_grad(xs, target, "loss_grad")
    loss = lax.psum(loss_part[0, 0], AXES)
    grads = [None] * depth
    for l in reversed(range(depth)):
        dxs, grads[l] = _layer_bwd(dxs, saved[l], layers[l], list(mod_mine[l]), alpha)
    grad_x = dxs.reshape(1, s, d)
    stack = lambda n: jnp.stack([grads[l][n] for l in range(depth)])

    rep = ["b_ada"] + [n for n in SMALL if n != "b_ada"]
    small_g = [stack("mod")] + [stack(n) for n in rep[1:]] + [stack("conv_w")]
    small_pack, small_g_rows = _pack(small_g, mult=8)
    small_parts = _all_gather(small_pack, "gather_small_grads")
    n_rep_rows = sum(small_g_rows[:-1])
    dmod_all = _unpack(small_parts, small_g_rows[:1], [(depth, N_MOD * d)], lead=(N_DEV,))[0]
    cw_all = _unpack(small_parts[:, n_rep_rows:], small_g_rows[-1:], [small_g[-1].shape], lead=(N_DEV,))[0]
    slabs = []
    for n in BIG:
        axis = 1 if n in ("w_out", "w_down") else 2
        sl = _dest_major(stack(n), axis).astype(BF16)
        slabs.append(jnp.stack([_rows128(sl[j]) for j in range(N_DEV)]))
    big_parts = _all_to_all(jnp.concatenate(slabs, axis=1), "scatter_grads")

    out = {}
    packs = [_pack([t[n] for n in BIG])[0] for t in (w, m, v)]
    res = _adamw(big_parts, *packs, "adamw_big")
    for kind, r in zip(("grad", "delta", "new_m", "new_v"), res):
        for n, a in zip(BIG, _unpack(r, w_rows, [w[n].shape for n in BIG])):
            out[kind, n] = a

    dmod_mine = lax.dynamic_slice_in_dim(dmod_all, me * n_ada, n_ada, axis=2)
    g_ada = jnp.stack([_mm(cond_all.T, dmod_mine[:, l], out_dtype=F32, name="g_w_ada", tk=N_DEV, tn=n_ada,
                           precision=HI) for l in range(depth)])
    lp = [_rows128(t["w_ada"], 8) for t in (w, m, v)]
    res = _adamw(_rows128(g_ada, 8)[None], *lp, "adamw_ada")
    for kind, r in zip(("grad", "delta", "new_m", "new_v"), res):
        out[kind, "w_ada"] = _unpack(r, [lp[0].shape[0]], [w_ada.shape])[0]

    rp = [_pack([t[n] for n in rep], mult=8)[0] for t in (w, m, v)]
    res = _adamw(small_parts[:, :n_rep_rows], *rp, "adamw_small")
    for kind, r in zip(("grad", "delta", "new_m", "new_v"), res):
        for n, a in zip(rep, _unpack(r, small_g_rows[:-1], [w[n].shape for n in rep])):
            out[kind, n] = a

    n_cw = conv_w.shape[2]
    cw_mine = lax.dynamic_slice_in_dim(cw_all, me * n_cw, n_cw, axis=3)
    cp = [_rows128(t["conv_w"], 8) for t in (w, m, v)]
    cw_in = jnp.stack([_rows128(cw_mine[i], 8) for i in range(N_DEV)])
    res = _adamw(cw_in, *cp, "adamw_conv_w")
    for kind, r in zip(("grad", "delta", "new_m", "new_v"), res):
        out[kind, "conv_w"] = _unpack(r, [cp[0].shape[0]], [conv_w.shape])[0]

    result = [loss, grad_x]
    for kind in ("grad", "delta", "new_m", "new_v"):
        result += [out[kind, n] for n in ORDER]
    return tuple(result)
```

```python
import functools
import math

import jax
import jax.numpy as jnp
import numpy as np
from jax import lax
from jax.experimental import pallas as pl
from jax.experimental.pallas import tpu as pltpu

F32 = jnp.float32
BF16 = jnp.bfloat16
HI = lax.Precision.HIGHEST
MESH_ID = pl.DeviceIdType.MESH
AXES = ("x", "y", "c")
N_DEV = 8

N_HEADS = 8
HEAD_DIM = 64
WIDTH = N_HEADS * HEAD_DIM
CHUNK = 64
LEFT_CHUNKS = 8
BAND = (LEFT_CHUNKS + 1) * CHUNK
REL_CLIP = 128
LN_EPS = 1e-5
N_MOD = 6
QK_SCALE = 1.0 / math.sqrt(HEAD_DIM)
NEG = -1e30
AUG = 128
N_SPLIT = 3
TQ = 512
SUB = 2 * CHUNK
SUBK = BAND + CHUNK

ADAM_LR, ADAM_B1, ADAM_B2, ADAM_EPS, ADAM_WD, ADAM_STEP = 0.001, 0.9, 0.999, 1e-08, 0.01, 10

VMEM_LIMIT = 56 * 2 ** 20


def _pcall(body, **kw):
    return pl.pallas_call(body, **kw)


def _cp(*sem):
    return pltpu.CompilerParams(dimension_semantics=sem if sem else None, vmem_limit_bytes=VMEM_LIMIT)


def _my_place():
    return lax.axis_index("x"), lax.axis_index("y"), lax.axis_index("c")


def _all_gather(x, name):
    r, c = x.shape

    def body(x_ref, out_ref, send_sems, recv_sems, local_sem):
        mx, my, mc = _my_place()
        me, sibling = (mx, my, mc), (mx, my, 1 - mc)
        chips = [(1 - mx, my), (mx, 1 - my), (1 - mx, 1 - my)]

        def slot(px, py, pc):
            return out_ref.at[4 * px + 2 * py + pc]

        def copy(k, block, to, src=None):
            return pltpu.make_async_remote_copy(
                src_ref=slot(*block) if src is None else src, dst_ref=slot(*block),
                send_sem=send_sems.at[k], recv_sem=recv_sems.at[k], device_id=to, device_id_type=MESH_ID)

        mine = pltpu.make_async_copy(x_ref, slot(*me), local_sem)
        mine.start()
        first = [copy(0, me, sibling, src=x_ref)]
        first += [copy(1 + j, me, (*chip, mc), src=x_ref) for j, chip in enumerate(chips)]
        for cp in first:
            cp.start()
        passed = [copy(4 + j, (*chip, mc), sibling) for j, chip in enumerate(chips)]
        for j, chip in enumerate(chips):
            copy(1 + j, (*chip, mc), me).wait_recv()
            passed[j].start()
        copy(0, sibling, me).wait_recv()
        for j, chip in enumerate(chips):
            copy(4 + j, (*chip, 1 - mc), me).wait_recv()
        for cp in first + passed:
            cp.wait_send()
        mine.wait()

    return _pcall(
        body, name=name,
        out_shape=jax.ShapeDtypeStruct((N_DEV, r, c), x.dtype),
        in_specs=[pl.BlockSpec(memory_space=pltpu.HBM)],
        out_specs=pl.BlockSpec(memory_space=pltpu.HBM),
        scratch_shapes=[pltpu.SemaphoreType.DMA((7,)), pltpu.SemaphoreType.DMA((7,)), pltpu.SemaphoreType.DMA],
    )(x)


def _all_to_all(x, name):
    _, r, c = x.shape

    def body(x_ref, out_ref, send_sems, recv_sems, local_sem):
        mx, my, mc = _my_place()
        me = 4 * mx + 2 * my + mc
        mine = pltpu.make_async_copy(x_ref.at[me], out_ref.at[me], local_sem)
        mine.start()
        copies = []
        for k in range(1, N_DEV):
            px, py, pc = mx ^ (k >> 2), my ^ ((k >> 1) & 1), mc ^ (k & 1)
            peer = 4 * px + 2 * py + pc
            copies.append(pltpu.make_async_remote_copy(
                src_ref=x_ref.at[peer], dst_ref=out_ref.at[me],
                send_sem=send_sems.at[k - 1], recv_sem=recv_sems.at[k - 1],
                device_id=(px, py, pc), device_id_type=MESH_ID))
        for cp in copies:
            cp.start()
        for cp in copies:
            cp.wait_recv()
        for cp in copies:
            cp.wait_send()
        mine.wait()

    return _pcall(
        body, name=name,
        out_shape=jax.ShapeDtypeStruct((N_DEV, r, c), x.dtype),
        in_specs=[pl.BlockSpec(memory_space=pltpu.HBM)],
        out_specs=pl.BlockSpec(memory_space=pltpu.HBM),
        scratch_shapes=[pltpu.SemaphoreType.DMA((7,)), pltpu.SemaphoreType.DMA((7,)), pltpu.SemaphoreType.DMA],
    )(x)


def _tile(n, pref, unit=128):
    if n <= pref:
        return n
    t = pref - pref % unit
    while t > unit and n % t:
        t -= unit
    assert n % t == 0, (n, pref, unit)
    return t


def _mm(a, b, *, out_dtype, name, ta=False, tb=False, tm=1024, tn=1024, tk=512, precision=None,
        a_halves=False, b_halves=False, o_halves=False):
    a_shape = (a.shape[1], 2 * a.shape[2]) if a_halves else a.shape
    b_shape = (b.shape[1], 2 * b.shape[2]) if b_halves else b.shape
    (k_dim, m) = a_shape if ta else a_shape[::-1]
    n = b_shape[0] if tb else b_shape[1]
    tm, tn = _tile(m, tm, 8), _tile(n // 2 if (b_halves or o_halves) else n, tn)
    tk = _tile(k_dim // 2 if a_halves else k_dim, tk)
    nk = k_dim // tk
    dims = (((0 if ta else 1,), (1 if tb else 0,)), ((), ()))

    def body(a_ref, b_ref, o_ref, acc_ref):
        k = pl.program_id(2)

        @pl.when(k == 0)
        def _():
            acc_ref[...] = jnp.zeros_like(acc_ref)

        acc_ref[...] += lax.dot_general(a_ref[...], b_ref[...], dims, preferred_element_type=F32,
                                        precision=precision)

        @pl.when(k == nk - 1)
        def _():
            o_ref[...] = acc_ref[...].astype(o_ref.dtype)

    def spec(shape2, pick, halves, per_half):
        if not halves:
            return pl.BlockSpec(shape2, pick)

        def index(i, j, k):
            r, c = pick(i, j, k)
            return (c // per_half, r, c % per_half)
        return pl.BlockSpec((None,) + shape2, index)

    assert not (ta and a_halves) and not (tb and b_halves)
    a_spec = spec((tk, tm), lambda i, j, k: (k, i), False, 0) if ta else \
        spec((tm, tk), lambda i, j, k: (i, k), a_halves, (k_dim // 2) // tk if a_halves else 0)
    b_spec = spec((tn, tk), lambda i, j, k: (j, k), False, 0) if tb else \
        spec((tk, tn), lambda i, j, k: (k, j), b_halves, (n // 2) // tn if b_halves else 0)
    o_spec = spec((tm, tn), lambda i, j, k: (i, j), o_halves, (n // 2) // tn if o_halves else 0)
    out_shape = (2, m, n // 2) if o_halves else (m, n)
    return _pcall(
        body, name=name,
        out_shape=jax.ShapeDtypeStruct(out_shape, out_dtype),
        grid=(m // tm, n // tn, nk),
        in_specs=[a_spec, b_spec],
        out_specs=o_spec,
        scratch_shapes=[pltpu.VMEM((tm, tn), F32)],
        compiler_params=_cp("parallel", "parallel", "arbitrary"),
    )(a, b)


def _ln_stats(x):
    mu = jnp.mean(x, axis=-1, keepdims=True)
    xc = x - mu
    var = jnp.mean(xc * xc, axis=-1, keepdims=True)
    rstd = lax.rsqrt(var + LN_EPS)
    return xc * rstd, rstd


def _ln_mod(x, sc, sh, name, tm=512):
    s, d = x.shape
    tm = min(tm, s)

    def body(x_ref, sc_ref, sh_ref, h_ref):
        xhat, _ = _ln_stats(x_ref[...])
        h_ref[...] = (xhat * (1.0 + sc_ref[...]) + sh_ref[...]).astype(h_ref.dtype)

    row = pl.BlockSpec((1, d), lambda i: (0, 0))
    return _pcall(
        body, name=name, out_shape=jax.ShapeDtypeStruct((s, d), BF16), grid=(s // tm,),
        in_specs=[pl.BlockSpec((tm, d), lambda i: (i, 0)), row, row],
        out_specs=pl.BlockSpec((tm, d), lambda i: (i, 0)),
        compiler_params=_cp("parallel"),
    )(x, sc, sh)


def _mm_res_ln(a, w, xres, gate, ln_g, ln_b, sc, sh, alpha, name, tm=512, tk=512):
    s, k_dim = a.shape
    d = w.shape[1]
    tm, tk = _tile(s, tm, 8), _tile(k_dim, tk)
    nk = k_dim // tk

    def body(a_ref, w_ref, x_ref, g_ref, lg_ref, lb_ref, sc_ref, sh_ref, y_ref, xn_ref, h_ref, acc_ref):
        k = pl.program_id(1)

        @pl.when(k == 0)
        def _():
            acc_ref[...] = jnp.zeros_like(acc_ref)

        acc_ref[...] += jnp.dot(a_ref[...], w_ref[...], preferred_element_type=F32)

        @pl.when(k == nk - 1)
        def _():
            y = acc_ref[...]
            y_ref[...] = y
            z = alpha * x_ref[...] + (1.0 + g_ref[...]) * y
            zhat, _ = _ln_stats(z)
            xn = zhat * lg_ref[...] + lb_ref[...]
            xn_ref[...] = xn
            xhat, _ = _ln_stats(xn)
            h_ref[...] = (xhat * (1.0 + sc_ref[...]) + sh_ref[...]).astype(h_ref.dtype)

    row = pl.BlockSpec((1, d), lambda i, k: (0, 0))
    tile = pl.BlockSpec((tm, d), lambda i, k: (i, 0))
    return _pcall(
        body, name=name,
        out_shape=(jax.ShapeDtypeStruct((s, d), F32), jax.ShapeDtypeStruct((s, d), F32),
                   jax.ShapeDtypeStruct((s, d), BF16)),
        grid=(s // tm, nk),
        in_specs=[pl.BlockSpec((tm, tk), lambda i, k: (i, k)), pl.BlockSpec((tk, d), lambda i, k: (k, 0)),
                  tile, row, row, row, row, row],
        out_specs=(tile, tile, tile),
        scratch_shapes=[pltpu.VMEM((tm, d), F32)],
        compiler_params=_cp("parallel", "arbitrary"),
    )(a, w, xres, gate, ln_g, ln_b, sc, sh)


def _colsum(v):
    return jnp.sum(v, axis=0, keepdims=True)


def _ln_res_bwd(dxn, xres, y, gate, ln_g, alpha, name, tm=256):
    s, d = dxn.shape
    tm = min(tm, s)

    def body(dxn_ref, x_ref, y_ref, g_ref, lg_ref, dres_ref, dy_ref, sums_ref):
        @pl.when(pl.program_id(0) == 0)
        def _():
            sums_ref[...] = jnp.zeros_like(sums_ref)

        dxn_v, y_v = dxn_ref[...], y_ref[...]
        one_g = 1.0 + g_ref[...]
        zhat, rstd = _ln_stats(alpha * x_ref[...] + one_g * y_v)
        dzh = dxn_v * lg_ref[...]
        dz = rstd * (dzh - jnp.mean(dzh, axis=-1, keepdims=True)
                     - zhat * jnp.mean(dzh * zhat, axis=-1, keepdims=True))
        dres_ref[...] = alpha * dz
        dy_ref[...] = (one_g * dz).astype(dy_ref.dtype)
        sums_ref[0:1, :] += _colsum(dxn_v * zhat)
        sums_ref[1:2, :] += _colsum(dxn_v)
        sums_ref[2:3, :] += _colsum(dz * y_v)

    row = pl.BlockSpec((1, d), lambda i: (0, 0))
    tile = pl.BlockSpec((tm, d), lambda i: (i, 0))
    return _pcall(
        body, name=name,
        out_shape=(jax.ShapeDtypeStruct((s, d), F32), jax.ShapeDtypeStruct((s, d), BF16),
                   jax.ShapeDtypeStruct((8, d), F32)),
        grid=(s // tm,),
        in_specs=[tile, tile, tile, row, row],
        out_specs=(tile, tile, pl.BlockSpec((8, d), lambda i: (0, 0))),
        compiler_params=_cp("arbitrary"),
    )(dxn, xres, y, gate, ln_g)


def _ln_mod_bwd(dh, x, sc, dres, name, extra=None, tm=256):
    s, d = dh.shape
    tm = min(tm, s)

    def body(*refs):
        if extra is None:
            dh_ref, x_ref, sc_ref, dres_ref, dx_ref, sums_ref = refs
            dh_v = dh_ref[...]
        else:
            dh_ref, x_ref, sc_ref, dres_ref, dh2_ref, df_ref, wf_ref, dx_ref, sums_ref = refs
            dh_v = dh_ref[...] + dh2_ref[...] + jnp.dot(df_ref[...], wf_ref[...], preferred_element_type=F32)

        @pl.when(pl.program_id(0) == 0)
        def _():
            sums_ref[...] = jnp.zeros_like(sums_ref)

        xhat, rstd = _ln_stats(x_ref[...])
        dxh = dh_v * (1.0 + sc_ref[...])
        dx_ref[...] = dres_ref[...] + rstd * (dxh - jnp.mean(dxh, axis=-1, keepdims=True)
                                             - xhat * jnp.mean(dxh * xhat, axis=-1, keepdims=True))
        sums_ref[0:1, :] += _colsum(dh_v)
        sums_ref[1:2, :] += _colsum(dh_v * xhat)

    row = pl.BlockSpec((1, d), lambda i: (0, 0))
    tile = pl.BlockSpec((tm, d), lambda i: (i, 0))
    in_specs = [tile, tile, row, tile]
    args = [dh, x, sc, dres]
    if extra is not None:
        in_specs += [tile, pl.BlockSpec((tm, AUG), lambda i: (i, 0)), pl.BlockSpec((AUG, d), lambda i: (0, 0))]
        args += list(extra)
    return _pcall(
        body, name=name,
        out_shape=(jax.ShapeDtypeStruct((s, d), F32), jax.ShapeDtypeStruct((8, d), F32)),
        grid=(s // tm,), in_specs=in_specs,
        out_specs=(tile, pl.BlockSpec((8, d), lambda i: (0, 0))),
        compiler_params=_cp("arbitrary"),
    )(*args)


def _loss_grad(xn, target, name, tm=512):
    s, d = xn.shape
    tm = min(tm, s)
    n = s // tm

    def body(x_ref, t_ref, dx_ref, loss_ref, acc_ref):
        i = pl.program_id(0)

        @pl.when(i == 0)
        def _():
            acc_ref[...] = jnp.zeros_like(acc_ref)

        err = x_ref[...] - t_ref[...]
        dx_ref[...] = err * (1.0 / d)
        acc_ref[...] += _colsum(err * err)

        @pl.when(i == n - 1)
        def _():
            loss_ref[...] = jnp.zeros_like(loss_ref) + (0.5 / d) * jnp.sum(acc_ref[...])

    tile = pl.BlockSpec((tm, d), lambda i: (i, 0))
    return _pcall(
        body, name=name,
        out_shape=(jax.ShapeDtypeStruct((s, d), F32), jax.ShapeDtypeStruct((8, 128), F32)),
        grid=(n,), in_specs=[tile, tile],
        out_specs=(tile, pl.BlockSpec((8, 128), lambda i: (0, 0))),
        scratch_shapes=[pltpu.VMEM((1, d), F32)],
        compiler_params=_cp("arbitrary"),
    )(xn, target)


def _sigmoid(v):
    return 1.0 / (1.0 + jnp.exp(-v))


def _merge_fwd(o_a, o_c, w_a, w_c, gates, name, tm=256):
    s, wd = o_a.shape
    d = w_a.shape[1]
    tm = min(tm, s)

    def body(oa_ref, oc_ref, wa_ref, wc_ref, ga_ref, gc_ref, m_ref):
        ba = jnp.dot(oa_ref[...], wa_ref[...], preferred_element_type=F32)
        bc = jnp.dot(oc_ref[...], wc_ref[...], preferred_element_type=F32)
        m_ref[...] = (_sigmoid(ga_ref[...]) * ba + _sigmoid(gc_ref[...]) * bc).astype(m_ref.dtype)

    o_spec = pl.BlockSpec((tm, wd), lambda i: (i, 0))
    w_spec = pl.BlockSpec((wd, d), lambda i: (0, 0))
    return _pcall(
        body, name=name, out_shape=jax.ShapeDtypeStruct((s, d), BF16), grid=(s // tm,),
        in_specs=[o_spec, o_spec, w_spec, w_spec,
                  pl.BlockSpec((tm, d), lambda i: (i, 0)), pl.BlockSpec((tm, d), lambda i: (i, 1))],
        out_specs=pl.BlockSpec((tm, d), lambda i: (i, 0)),
        compiler_params=_cp("parallel"),
    )(o_a, o_c, w_a, w_c, gates, gates)


def _merge_bwd(dmix, w_out_t, o_a, o_c, w_a, w_c, w_a_t, w_c_t, gates, name, tm=256):
    s, wd = o_a.shape
    d = w_a.shape[1]
    tm = min(tm, s)

    def body(dmix_ref, wot_ref, oa_ref, oc_ref, wa_ref, wc_ref, wat_ref, wct_ref, ga_ref, gc_ref,
             dba_ref, dbc_ref, doa_ref, doc_ref, dg_ref):
        dm = jnp.dot(dmix_ref[...], wot_ref[...], preferred_element_type=F32)
        for half, (o_ref, w_ref, wt_ref, g_ref, db_ref, do_ref) in enumerate((
                (oa_ref, wa_ref, wat_ref, ga_ref, dba_ref, doa_ref),
                (oc_ref, wc_ref, wct_ref, gc_ref, dbc_ref, doc_ref))):
            b = jnp.dot(o_ref[...], w_ref[...], preferred_element_type=F32)
            sg = _sigmoid(g_ref[...])
            db = (dm * sg).astype(BF16)
            db_ref[...] = db
            dg_ref[:, d * half:d * (half + 1)] = (dm * b * sg * (1.0 - sg)).astype(dg_ref.dtype)
            do_ref[...] = jnp.dot(db, wt_ref[...], preferred_element_type=F32).astype(do_ref.dtype)

    row_d = pl.BlockSpec((tm, d), lambda i: (i, 0))
    row_w = pl.BlockSpec((tm, wd), lambda i: (i, 0))
    full = lambda shp: pl.BlockSpec(shp, lambda i: (0, 0))
    return _pcall(
        body, name=name,
        out_shape=(jax.ShapeDtypeStruct((s, d), BF16), jax.ShapeDtypeStruct((s, d), BF16),
                   jax.ShapeDtypeStruct((s, wd), BF16), jax.ShapeDtypeStruct((s, wd), BF16),
                   jax.ShapeDtypeStruct((s, 2 * d), BF16)),
        grid=(s // tm,),
        in_specs=[row_d, full((d, d)), row_w, row_w, full((wd, d)), full((wd, d)), full((d, wd)), full((d, wd)),
                  row_d, pl.BlockSpec((tm, d), lambda i: (i, 1))],
        out_specs=(row_d, row_d, row_w, row_w, pl.BlockSpec((tm, 2 * d), lambda i: (i, 0))),
        compiler_params=_cp("parallel"),
    )(dmix, w_out_t, o_a, o_c, w_a, w_c, w_a_t, w_c_t, gates, gates)


def _shift_down(cur, k, fill_rows):
    out = pltpu.roll(cur, k, axis=0)
    rid = lax.broadcasted_iota(jnp.int32, cur.shape, 0)
    for r, fill in enumerate(fill_rows):
        out = jnp.where(rid == r, fill, out)
    return out


def _shift_up(cur, k, fill_rows):
    n = cur.shape[0]
    out = pltpu.roll(cur, n - k, axis=0)
    rid = lax.broadcasted_iota(jnp.int32, cur.shape, 0)
    for r, fill in enumerate(fill_rows):
        out = jnp.where(rid == n - k + r, fill, out)
    return out


def _conv_rows(cur, prev8, first, w, b):
    p6 = jnp.where(first, 0.0, prev8[6:7, :])
    p7 = jnp.where(first, 0.0, prev8[7:8, :])
    m1 = _shift_down(cur, 1, [p7])
    m2 = _shift_down(cur, 2, [p6, p7])
    u = b + w[0:1, :] * m2 + w[1:2, :] * m1 + w[2:3, :] * cur
    return u, m1, m2


def _conv_specs(tr, tc, order):
    r8 = tr // 8
    at = lambda f: (lambda *g: f(*order(*g)))
    return [pl.BlockSpec((2, tr, tc), at(lambda i, j: (0, i, j))),
            pl.BlockSpec((2, 8, tc), at(lambda i, j: (0, jnp.maximum(i * r8 - 1, 0), j))),
            pl.BlockSpec((2, 3, tc), at(lambda i, j: (0, 0, j))),
            pl.BlockSpec((2, 1, tc), at(lambda i, j: (0, 0, j)))]


def _conv_act_fwd(u0, conv_w, conv_b, name, tr=512, tc=256):
    _, s, f = u0.shape
    tr, tc = _tile(s, tr, 8), _tile(f, tc)

    def body(u_ref, p_ref, w_ref, b_ref, act_ref):
        first = pl.program_id(0) == 0
        a, _, _ = _conv_rows(u_ref[0], p_ref[0], first, w_ref[0], b_ref[0])
        v, _, _ = _conv_rows(u_ref[1], p_ref[1], first, w_ref[1], b_ref[1])
        act_ref[...] = (a * _sigmoid(a) * v).astype(act_ref.dtype)

    return _pcall(
        body, name=name, out_shape=jax.ShapeDtypeStruct((s, f), BF16), grid=(s // tr, f // tc),
        in_specs=_conv_specs(tr, tc, lambda i, j: (i, j)),
        out_specs=pl.BlockSpec((tr, tc), lambda i, j: (i, j)),
        compiler_params=_cp("parallel", "parallel"),
    )(u0, u0, conv_w, conv_b)


def _act_grads(dact, a, v):
    sg = _sigmoid(a)
    return dact * v * sg * (1.0 + a * (1.0 - sg)), dact * a * sg


def _conv_act_bwd(dact, u0, conv_w, conv_b, name, tr=512, tc=256):
    _, s, f = u0.shape
    tr, tc = _tile(s, tr, 8), _tile(f, tc)
    n = s // tr
    r8 = tr // 8

    def body(dact_ref, dnext_ref, u_ref, p_ref, unext_ref, w_ref, b_ref, du0_ref, sums_ref):
        i = pl.program_id(1)
        first, last = i == 0, i == n - 1

        @pl.when(first)
        def _():
            sums_ref[...] = jnp.zeros_like(sums_ref)

        cur = (u_ref[0], u_ref[1])
        (a, a1, a2), (v, v1, v2) = [_conv_rows(cur[hf], p_ref[hf], first, w_ref[hf], b_ref[hf]) for hf in range(2)]
        da, dv = _act_grads(dact_ref[...], a, v)
        (an, _, _), (vn, _, _) = [_conv_rows(unext_ref[hf], cur[hf][tr - 8:tr, :], False, w_ref[hf], b_ref[hf])
                                  for hf in range(2)]
        dan, dvn = _act_grads(dnext_ref[...], an, vn)
        for hf, g, gn, shifted in ((0, da, dan, (a2, a1, cur[0])), (1, dv, dvn, (v2, v1, cur[1]))):
            w = w_ref[hf]
            n0 = jnp.where(last, 0.0, gn[0:1, :])
            n1 = jnp.where(last, 0.0, gn[1:2, :])
            du0 = w[2:3, :] * g + w[1:2, :] * _shift_up(g, 1, [n0]) + w[0:1, :] * _shift_up(g, 2, [n0, n1])
            du0_ref[hf] = du0.astype(du0_ref.dtype)
            for r in range(3):
                sums_ref[hf, r:r + 1, :] += _colsum(g * shifted[r])
            sums_ref[hf, 3:4, :] += _colsum(g)

    nxt = lambda i: jnp.minimum((i + 1) * r8, s // 8 - 1)
    order = lambda j, i: (i, j)
    specs = _conv_specs(tr, tc, order)
    return _pcall(
        body, name=name,
        out_shape=(jax.ShapeDtypeStruct((2, s, f), BF16), jax.ShapeDtypeStruct((2, 8, f), F32)),
        grid=(f // tc, n),
        in_specs=[pl.BlockSpec((tr, tc), lambda j, i: (i, j)), pl.BlockSpec((8, tc), lambda j, i: (nxt(i), j)),
                  specs[0], specs[1], pl.BlockSpec((2, 8, tc), lambda j, i: (0, nxt(i), j)), specs[2], specs[3]],
        out_specs=(pl.BlockSpec((2, tr, tc), lambda j, i: (0, i, j)), pl.BlockSpec((2, 8, tc), lambda j, i: (0, 0, j))),
        compiler_params=_cp("parallel", "arbitrary"),
    )(dact, dact, u0, u0, u0, conv_w, conv_b)


def _cumsum_rows(v, reverse=False):
    r = v.shape[0]
    i128 = lax.broadcasted_iota(jnp.int32, (128, 128), 0), lax.broadcasted_iota(jnp.int32, (128, 128), 1)
    ir = lax.broadcasted_iota(jnp.int32, (r, r), 0), lax.broadcasted_iota(jnp.int32, (r, r), 1)
    in_row = (i128[0] >= i128[1] if reverse else i128[0] <= i128[1]).astype(F32)
    rows = (ir[1] > ir[0] if reverse else ir[1] < ir[0]).astype(F32)
    within = jnp.dot(v, in_row, preferred_element_type=F32, precision=HI)
    tot = jnp.broadcast_to(within[:, 0:1] if reverse else within[:, 127:128], (r, 128))
    return within + jnp.dot(rows, tot, preferred_element_type=F32, precision=HI)


def _forget_fwd(f_t, b_f, name):
    h, r, _ = f_t.shape

    def body(f_ref, b_ref, o_ref):
        z = f_ref[...] + b_ref[...]
        logf = jnp.minimum(z, 0.0) - jnp.log(1.0 + jnp.exp(-jnp.abs(z)))
        rest = _cumsum_rows(logf)
        for i in range(N_SPLIT):
            piece = rest.astype(BF16).astype(F32)
            o_ref[i] = piece
            rest = rest - piece

    return _pcall(
        body, name=name, out_shape=jax.ShapeDtypeStruct((N_SPLIT, h, r, 128), F32), grid=(h,),
        in_specs=[pl.BlockSpec((None, r, 128), lambda i: (i, 0, 0)), pl.BlockSpec((None, 1, 128), lambda i: (i, 0, 0))],
        out_specs=pl.BlockSpec((N_SPLIT, None, r, 128), lambda i: (0, i, 0, 0)),
        compiler_params=_cp("parallel"),
    )(f_t, b_f)


def _forget_bwd(d_cum, f_t, b_f, name):
    h, r, _ = f_t.shape

    def body(g_ref, f_ref, b_ref, df_ref, db_ref):
        df = _cumsum_rows(g_ref[...], reverse=True) * _sigmoid(-(f_ref[...] + b_ref[...]))
        df_ref[...] = df
        db_ref[...] = jnp.zeros_like(db_ref) + jnp.sum(df)

    blk = pl.BlockSpec((None, r, 128), lambda i: (i, 0, 0))
    one = pl.BlockSpec((None, 1, 128), lambda i: (i, 0, 0))
    return _pcall(
        body, name=name,
        out_shape=(jax.ShapeDtypeStruct((h, r, 128), F32), jax.ShapeDtypeStruct((h, 1, 128), F32)),
        grid=(h,), in_specs=[blk, blk, one], out_specs=(blk, one),
        compiler_params=_cp("parallel"),
    )(d_cum, f_t, b_f)


_NT = (((1,), (1,)), ((), ()))


def _causal_keep(tk, tq):
    return lax.broadcasted_iota(jnp.int32, (tk, tq), 0) <= lax.broadcasted_iota(jnp.int32, (tk, tq), 1)


_TN = (((0,), (0,)), ((), ()))
PAIR = 2
V_FOX = 2 * WIDTH // 128
Q_CHUNK, K_CHUNK, V_CHUNK = (3 * WIDTH) // 128, (4 * WIDTH) // 128, (5 * WIDTH) // 128


def _fox_fwd(q_aug, k_aug, qkv, name, t=TQ):
    s = q_aug.shape[0]
    t = min(t, s)
    n = s // t
    dh = HEAD_DIM

    def body(q_ref, k_ref, v_ref, o_ref, lse_ref, m_s, l_s, acc_s):
        qi = pl.program_id(1)
        m_s[...] = jnp.full_like(m_s, NEG)
        l_s[...] = jnp.zeros_like(l_s)
        acc_s[...] = jnp.zeros_like(acc_s)

        def step(kj, diag):
            rows = pl.ds(pl.multiple_of(kj * t, t), t)
            for hh in range(PAIR):
                st = lax.dot_general(k_ref[rows, AUG * hh:AUG * (hh + 1)], q_ref[:, AUG * hh:AUG * (hh + 1)], _NT,
                                     preferred_element_type=F32)
                if diag:
                    st = jnp.where(_causal_keep(t, t), st, NEG)
                m_prev = m_s[hh]
                m_new = jnp.maximum(m_prev, jnp.max(st, axis=0, keepdims=True))
                a = jnp.exp(m_prev - m_new)
                pt = jnp.exp(st - m_new)
                l_s[hh] = a * l_s[hh] + jnp.sum(pt, axis=0, keepdims=True)
                acc_s[hh] = a * acc_s[hh] + lax.dot_general(v_ref[rows, dh * hh:dh * (hh + 1)], pt.astype(BF16), _TN,
                                                            preferred_element_type=F32)
                m_s[hh] = m_new

        def off_diagonal(kj, carry):
            step(kj, False)
            return carry

        lax.fori_loop(0, qi, off_diagonal, 0)
        step(qi, True)
        for hh in range(PAIR):
            o_ref[hh] = acc_s[hh] / l_s[hh]
            lse_ref[hh, 0] = m_s[hh] + jnp.log(l_s[hh])

    return _pcall(
        body, name=name,
        out_shape=(jax.ShapeDtypeStruct((N_HEADS, dh, s), F32), jax.ShapeDtypeStruct((N_HEADS, n, 1, t), F32)),
        grid=(N_HEADS // PAIR, n),
        in_specs=[pl.BlockSpec((t, PAIR * AUG), lambda hp, i: (i, hp)),
                  pl.BlockSpec((s, PAIR * AUG), lambda hp, i: (0, hp)),
                  pl.BlockSpec((s, PAIR * dh), lambda hp, i: (0, V_FOX + hp))],
        out_specs=(pl.BlockSpec((PAIR, dh, t), lambda hp, i: (hp, 0, i)),
                   pl.BlockSpec((PAIR, 1, 1, t), lambda hp, i: (hp, i, 0, 0))),
        scratch_shapes=[pltpu.VMEM((PAIR, 1, t), F32), pltpu.VMEM((PAIR, 1, t), F32), pltpu.VMEM((PAIR, dh, t), F32)],
        compiler_params=_cp("parallel", "arbitrary"),
    )(q_aug, k_aug, qkv)


def _row_dot(a_t, b_t, name, t=2048):
    h, dh, s = a_t.shape
    t = min(t, s)

    def body(a_ref, b_ref, o_ref):
        o_ref[...] = jnp.sum(a_ref[...].astype(F32) * b_ref[...], axis=0, keepdims=True)

    blk = pl.BlockSpec((None, dh, t), lambda hh, i: (hh, 0, i))
    return _pcall(
        body, name=name, out_shape=jax.ShapeDtypeStruct((h, 1, s), F32), grid=(h, s // t),
        in_specs=[blk, blk], out_specs=pl.BlockSpec((None, 1, t), lambda hh, i: (hh, 0, i)),
        compiler_params=_cp("parallel", "parallel"),
    )(a_t, b_t)


def _fox_bwd(q_aug, k_aug, qkv, do, lse, delta, name, t=TQ):
    s = q_aug.shape[0]
    dh = HEAD_DIM
    t = min(t, s)
    n = s // t

    def body(q_ref, k_ref, v_ref, do_ref, lse_ref, dl_ref, dk_ref, dv_ref, dq_ref, dsum_ref, dqsum_ref,
             dk_s, dv_s, dsum_s):
        kj = pl.program_id(1)

        @pl.when(kj == 0)
        def _():
            dq_ref[...] = jnp.zeros_like(dq_ref)
            dqsum_ref[...] = jnp.zeros_like(dqsum_ref)

        dk_s[...] = jnp.zeros_like(dk_s)
        dv_s[...] = jnp.zeros_like(dv_s)
        dsum_s[...] = jnp.zeros_like(dsum_s)

        def step(qi, diag):
            rows = pl.ds(pl.multiple_of(qi * t, t), t)
            for hh in range(PAIR):
                q = q_ref[rows, AUG * hh:AUG * (hh + 1)]
                k = k_ref[:, AUG * hh:AUG * (hh + 1)]
                st = lax.dot_general(k, q, _NT, preferred_element_type=F32)
                if diag:
                    st = jnp.where(_causal_keep(t, t), st, NEG)
                pt = jnp.exp(st - lse_ref[hh, qi])
                do_v = do_ref[rows, dh * hh:dh * (hh + 1)]
                dpt = lax.dot_general(v_ref[:, dh * hh:dh * (hh + 1)], do_v, _NT, preferred_element_type=F32)
                ds32 = pt * (dpt - dl_ref[hh, qi])
                dsum_s[hh] += sum(ds32[:, 128 * u:128 * (u + 1)] for u in range(t // 128))
                dqsum_ref[hh, qi] += jnp.sum(ds32, axis=0, keepdims=True)
                dst = ds32.astype(BF16)
                dv_s[hh] += jnp.dot(pt.astype(BF16), do_v, preferred_element_type=F32)
                dk_s[hh] += jnp.dot(dst, q[:, :dh], preferred_element_type=F32)
                dq_ref[hh, qi] += lax.dot_general(k[:, :dh], dst, _TN, preferred_element_type=F32)

        def off_diagonal(qi, carry):
            step(qi, False)
            return carry

        step(kj, True)
        lax.fori_loop(kj + 1, n, off_diagonal, 0)
        for hh in range(PAIR):
            dk_ref[:, dh * hh:dh * (hh + 1)] = dk_s[hh].astype(dk_ref.dtype)
            dv_ref[:, dh * hh:dh * (hh + 1)] = dv_s[hh].astype(dv_ref.dtype)
            dsum_ref[:, 128 * hh:128 * (hh + 1)] = dsum_s[hh]

    stat = pl.BlockSpec((PAIR, n, 1, t), lambda hp, j: (hp, 0, 0, 0))
    return _pcall(
        body, name=name,
        out_shape=(jax.ShapeDtypeStruct((s, WIDTH), BF16), jax.ShapeDtypeStruct((s, WIDTH), BF16),
                   jax.ShapeDtypeStruct((N_HEADS, n, dh, t), F32), jax.ShapeDtypeStruct((s, N_HEADS * 128), F32),
                   jax.ShapeDtypeStruct((N_HEADS, n, 1, t), F32)),
        grid=(N_HEADS // PAIR, n),
        in_specs=[pl.BlockSpec((s, PAIR * AUG), lambda hp, j: (0, hp)),
                  pl.BlockSpec((t, PAIR * AUG), lambda hp, j: (j, hp)),
                  pl.BlockSpec((t, PAIR * dh), lambda hp, j: (j, V_FOX + hp)),
                  pl.BlockSpec((s, PAIR * dh), lambda hp, j: (0, hp)),
                  stat, stat],
        out_specs=(pl.BlockSpec((t, PAIR * dh), lambda hp, j: (j, hp)),
                   pl.BlockSpec((t, PAIR * dh), lambda hp, j: (j, hp)),
                   pl.BlockSpec((PAIR, n, dh, t), lambda hp, j: (hp, 0, 0, 0)),
                   pl.BlockSpec((t, PAIR * 128), lambda hp, j: (j, hp)),
                   stat),
        scratch_shapes=[pltpu.VMEM((PAIR, t, dh), F32), pltpu.VMEM((PAIR, t, dh), F32), pltpu.VMEM((PAIR, t, 128), F32)],
        compiler_params=_cp("parallel", "arbitrary"),
    )(q_aug, k_aug, qkv, do, lse, delta)


def _chunk_logits(kw, qs, bias, first_tile, j):
    st = lax.dot_general(kw[SUB * j:SUB * j + SUBK], qs, _NT, preferred_element_type=F32) + bias
    row = lax.broadcasted_iota(jnp.int32, st.shape, 0)
    return jnp.where(jnp.logical_or(jnp.logical_not(first_tile), row >= TQ - SUB * j), st, NEG)


def _window(prev_ref, cur_ref, hh):
    cols = slice(HEAD_DIM * hh, HEAD_DIM * (hh + 1))
    return jnp.concatenate([prev_ref[:, cols], cur_ref[:, cols]], axis=0)


def _chunk_fwd(qkv, bias_t, name):
    s = qkv.shape[0]
    dh = HEAD_DIM
    n = s // TQ
    nsub = TQ // SUB
    prev = lambda i: jnp.maximum(i - 1, 0)

    def body(q_ref, kp_ref, kc_ref, vp_ref, vc_ref, b_ref, o_ref, lse_ref):
        first_tile = pl.program_id(1) == 0
        for hh in range(PAIR):
            kw, vw = _window(kp_ref, kc_ref, hh), _window(vp_ref, vc_ref, hh)
            bias = b_ref[hh]
            for j in range(nsub):
                qs = q_ref[SUB * j:SUB * (j + 1), dh * hh:dh * (hh + 1)] * QK_SCALE
                st = _chunk_logits(kw, qs, bias, first_tile, j)
                m = jnp.max(st, axis=0, keepdims=True)
                pt = jnp.exp(st - m)
                l = jnp.sum(pt, axis=0, keepdims=True)
                ot = lax.dot_general(vw[SUB * j:SUB * j + SUBK], pt.astype(BF16), _TN, preferred_element_type=F32)
                o_ref[hh, :, SUB * j:SUB * (j + 1)] = ot / l
                lse_ref[hh, 0, :, SUB * j:SUB * (j + 1)] = m + jnp.log(l)

    blk = lambda col, m: pl.BlockSpec((TQ, PAIR * dh), lambda hp, i: (m(i), col + hp))
    same = lambda i: i
    return _pcall(
        body, name=name,
        out_shape=(jax.ShapeDtypeStruct((N_HEADS, dh, s), F32), jax.ShapeDtypeStruct((N_HEADS, n, 1, TQ), F32)),
        grid=(N_HEADS // PAIR, n),
        in_specs=[blk(Q_CHUNK, same), blk(K_CHUNK, prev), blk(K_CHUNK, same), blk(V_CHUNK, prev), blk(V_CHUNK, same),
                  pl.BlockSpec((PAIR, SUBK, SUB), lambda hp, i: (hp, 0, 0))],
        out_specs=(pl.BlockSpec((PAIR, dh, TQ), lambda hp, i: (hp, 0, i)),
                   pl.BlockSpec((PAIR, 1, 1, TQ), lambda hp, i: (hp, i, 0, 0))),
        compiler_params=_cp("parallel", "parallel"),
    )(qkv, qkv, qkv, qkv, qkv, bias_t)


def _chunk_bwd(qkv, do, lse, delta, bias_t, name):
    s = qkv.shape[0]
    dh = HEAD_DIM
    n = s // TQ
    nsub = TQ // SUB
    cur = lambda i: jnp.minimum(i, n - 1)
    prev = lambda i: jnp.maximum(cur(i) - 1, 0)
    done = lambda i: jnp.maximum(i - 1, 0)

    def body(q_ref, kp_ref, kc_ref, vp_ref, vc_ref, do_ref, lse_ref, dl_ref, b_ref,
             dq_ref, dk_ref, dv_ref, db_ref, dkw_s, dvw_s, ck_s, cv_s):
        i = pl.program_id(1)

        @pl.when(i == 0)
        def _():
            db_ref[...] = jnp.zeros_like(db_ref)
            ck_s[...] = jnp.zeros_like(ck_s)
            cv_s[...] = jnp.zeros_like(cv_s)

        dkw_s[...] = jnp.zeros_like(dkw_s)
        dvw_s[...] = jnp.zeros_like(dvw_s)

        @pl.when(i < n)
        def _():
            first_tile = i == 0
            for hh in range(PAIR):
                kw, vw = _window(kp_ref, kc_ref, hh), _window(vp_ref, vc_ref, hh)
                bias = b_ref[hh]
                for j in range(nsub):
                    rows = pl.ds(SUB * j, SUB)
                    win = pl.ds(SUB * j, SUBK)
                    qs = q_ref[rows, dh * hh:dh * (hh + 1)] * QK_SCALE
                    st = _chunk_logits(kw, qs, bias, first_tile, j)
                    pt = jnp.exp(st - lse_ref[hh, 0, :, rows])
                    do_j = do_ref[rows, dh * hh:dh * (hh + 1)]
                    dpt = lax.dot_general(vw[SUB * j:SUB * j + SUBK], do_j, _NT, preferred_element_type=F32)
                    dst = pt * (dpt - dl_ref[hh, 0, :, rows])
                    db_ref[hh] += dst
                    dsb = dst.astype(BF16)
                    dvw_s[hh, win, :] += jnp.dot(pt.astype(BF16), do_j, preferred_element_type=F32)
                    dkw_s[hh, win, :] += jnp.dot(dsb, qs, preferred_element_type=F32)
                    dq_ref[hh, :, rows] = QK_SCALE * lax.dot_general(kw[SUB * j:SUB * j + SUBK], dsb, _TN,
                                                                     preferred_element_type=F32)

        for hh in range(PAIR):
            cols = slice(dh * hh, dh * (hh + 1))
            dk_ref[:, cols] = (ck_s[hh] + dkw_s[hh, 0:TQ, :]).astype(dk_ref.dtype)
            dv_ref[:, cols] = (cv_s[hh] + dvw_s[hh, 0:TQ, :]).astype(dv_ref.dtype)
        ck_s[...] = dkw_s[:, TQ:2 * TQ, :]
        cv_s[...] = dvw_s[:, TQ:2 * TQ, :]

    blk = lambda col, m: pl.BlockSpec((TQ, PAIR * dh), lambda hp, i: (m(i), col + hp))
    stat = pl.BlockSpec((PAIR, 1, 1, TQ), lambda hp, i: (hp, cur(i), 0, 0))
    bs = pl.BlockSpec((PAIR, SUBK, SUB), lambda hp, i: (hp, 0, 0))
    return _pcall(
        body, name=name,
        out_shape=(jax.ShapeDtypeStruct((N_HEADS, dh, s), F32), jax.ShapeDtypeStruct((s, WIDTH), BF16),
                   jax.ShapeDtypeStruct((s, WIDTH), BF16), jax.ShapeDtypeStruct((N_HEADS, SUBK, SUB), F32)),
        grid=(N_HEADS // PAIR, n + 1),
        in_specs=[blk(Q_CHUNK, cur), blk(K_CHUNK, prev), blk(K_CHUNK, cur), blk(V_CHUNK, prev), blk(V_CHUNK, cur),
                  blk(0, cur), stat, stat, bs],
        out_specs=(pl.BlockSpec((PAIR, dh, TQ), lambda hp, i: (hp, 0, cur(i))), blk(0, done), blk(0, done), bs),
        scratch_shapes=[pltpu.VMEM((PAIR, 2 * TQ, dh), F32), pltpu.VMEM((PAIR, 2 * TQ, dh), F32),
                        pltpu.VMEM((PAIR, TQ, dh), F32), pltpu.VMEM((PAIR, TQ, dh), F32)],
        compiler_params=_cp("parallel", "arbitrary"),
    )(qkv, qkv, qkv, qkv, qkv, do, lse, delta, bias_t)


def _mod_part(c_all, w_ada, b_ada, name):
    nl, d, n = w_ada.shape
    b = c_all.shape[0]

    def body(c_ref, w_ref, b_ref, o_ref, cond_ref):
        cv = c_ref[...]
        cond = cv * _sigmoid(cv)
        cond_ref[...] = cond
        o_ref[...] = jnp.dot(cond, w_ref[...], preferred_element_type=F32, precision=HI) + b_ref[...]

    return _pcall(
        body, name=name,
        out_shape=(jax.ShapeDtypeStruct((nl, b, n), F32), jax.ShapeDtypeStruct((b, d), F32)),
        grid=(nl,),
        in_specs=[pl.BlockSpec((b, d), lambda l: (0, 0)), pl.BlockSpec((None, d, n), lambda l: (l, 0, 0)),
                  pl.BlockSpec((None, 1, n), lambda l: (l, 0, 0))],
        out_specs=(pl.BlockSpec((None, b, n), lambda l: (l, 0, 0)), pl.BlockSpec((b, d), lambda l: (0, 0))),
        compiler_params=_cp("arbitrary"),
    )(c_all, w_ada, b_ada)


def _adamw(parts, w, m, v, name, tr=1024):
    p, r, c = parts.shape
    tr = _tile(r, tr, 16)
    c1 = 1.0 / (1.0 - ADAM_B1 ** ADAM_STEP)
    c2 = 1.0 / (1.0 - ADAM_B2 ** ADAM_STEP)

    def body(p_ref, w_ref, m_ref, v_ref, g_ref, d_ref, nm_ref, nv_ref):
        g = p_ref[0].astype(F32)
        for i in range(1, p):
            g = g + p_ref[i].astype(F32)
        nm = ADAM_B1 * m_ref[...] + (1.0 - ADAM_B1) * g
        nv = ADAM_B2 * v_ref[...] + (1.0 - ADAM_B2) * (g * g)
        g_ref[...] = g
        nm_ref[...] = nm
        nv_ref[...] = nv
        d_ref[...] = -ADAM_LR * ((nm * c1) / (jnp.sqrt(nv * c2) + ADAM_EPS) + ADAM_WD * w_ref[...])

    blk = pl.BlockSpec((tr, c), lambda i: (i, 0))
    out = jax.ShapeDtypeStruct((r, c), F32)
    return _pcall(
        body, name=name, out_shape=(out, out, out, out), grid=(r // tr,),
        in_specs=[pl.BlockSpec((p, tr, c), lambda i: (0, i, 0)), blk, blk, blk],
        out_specs=(blk, blk, blk, blk),
        compiler_params=_cp("parallel"),
    )(parts, w, m, v)


def _rows128(a, mult=16):
    flat = a.reshape(-1)
    n = flat.shape[0]
    per = 128 * mult
    pad = (-n) % per
    if pad:
        flat = jnp.concatenate([flat, jnp.zeros((pad,), a.dtype)])
    return flat.reshape(-1, 128)


def _pack(arrs, mult=16):
    pieces = [_rows128(a, mult) for a in arrs]
    return jnp.concatenate(pieces, axis=0), [p.shape[0] for p in pieces]


def _unpack(packed, rows, shapes, lead=()):
    out, at = [], 0
    for r, shp in zip(rows, shapes):
        n = int(np.prod(shp))
        piece = packed[..., at:at + r, :].reshape(lead + (r * 128,))[..., :n]
        out.append(piece.reshape(lead + tuple(shp)))
        at += r
    return out


def _rel_bias_tile(table):
    h = table.shape[0]
    lo = REL_CLIP - (CHUNK - 1)
    n_far = LEFT_CHUNKS * CHUNK + CHUNK - 1 - REL_CLIP
    vec = jnp.concatenate([table[:, lo:2 * REL_CLIP], jnp.repeat(table[:, 2 * REL_CLIP:], n_far + 1, axis=1)], axis=1)
    rev = vec[:, ::-1]
    n_vec = BAND + CHUNK - 1
    skew = jnp.tile(rev, (1, CHUNK + 1))[:, :CHUNK * (n_vec + 1)].reshape(h, CHUNK, n_vec + 1)
    bias = skew[:, ::-1, :BAND]
    neg = jnp.full((h, CHUNK, CHUNK), NEG, F32)
    two = jnp.concatenate([jnp.concatenate([bias, neg], axis=2), jnp.concatenate([neg, bias], axis=2)], axis=1)
    return two.transpose(0, 2, 1)


def _rel_bias_tile_grad(dbias_t):
    h = dbias_t.shape[0]
    two = dbias_t.transpose(0, 2, 1)
    dbias = two[:, :CHUNK, :BAND] + two[:, CHUNK:, CHUNK:]
    n_vec = BAND + CHUNK - 1
    dskew = jnp.pad(dbias[:, ::-1, :], ((0, 0), (0, 0), (0, n_vec + 1 - BAND))).reshape(h, CHUNK * (n_vec + 1))
    dskew = jnp.pad(dskew, ((0, 0), (0, (CHUNK + 1) * n_vec - CHUNK * (n_vec + 1))))
    drev = jnp.sum(dskew.reshape(h, CHUNK + 1, n_vec), axis=1)
    dvec = drev[:, ::-1]
    lo = REL_CLIP - (CHUNK - 1)
    n_near = 2 * REL_CLIP - lo
    return jnp.concatenate([jnp.zeros((h, lo), F32), dvec[:, :n_near],
                            jnp.sum(dvec[:, n_near:], axis=1, keepdims=True)], axis=1)


BIG = ("w_in", "w_br_fox", "w_br_chunk", "w_out", "w_up", "w_down")
SMALL = ("b_f", "rel_bias", "conv_b", "b_ada", "ln1_g", "ln1_b", "ln2_g", "ln2_b")
ORDER = ("w_in", "b_f", "rel_bias", "w_br_fox", "w_br_chunk", "w_out", "w_up", "conv_w", "conv_b", "w_down",
         "w_ada", "b_ada", "ln1_g", "ln1_b", "ln2_g", "ln2_b")


def _dest_major(g, axis):
    shp = g.shape
    g = g.reshape(shp[:axis] + (N_DEV, shp[axis] // N_DEV) + shp[axis + 1:])
    return jnp.moveaxis(g, axis, 0)


def _layer_fwd(x, h1, wl, mod, next_mod, alpha):
    sh1, sc1, g1, sh2, sc2, g2 = mod
    s = x.shape[0]
    qkv = _mm(h1, wl["w_qkv"], out_dtype=BF16, name="proj_qkv", tk=1024)
    gates = _mm(h1, wl["w_gates"], out_dtype=F32, name="proj_gates", tk=1024)
    f_t = _mm(wl["w_f_t"], h1, out_dtype=F32, name="proj_f", tb=True, tk=1024)[:N_HEADS]
    f_t = f_t.reshape(N_HEADS, s // 128, 128)
    b_f = jnp.broadcast_to(wl["b_f"].reshape(N_HEADS, 1, 1), (N_HEADS, 1, 128))
    cum = _forget_fwd(f_t, b_f, "forget_fwd").reshape(N_SPLIT, N_HEADS, s)
    cum_cols = cum.transpose(2, 1, 0).astype(BF16)

    ones = jnp.ones((s, N_HEADS, N_SPLIT), BF16)
    zeros = jnp.zeros((s, N_HEADS, AUG - HEAD_DIM - 2 * N_SPLIT), BF16)
    q_a = qkv[:, :WIDTH].reshape(s, N_HEADS, HEAD_DIM) * QK_SCALE
    k_a = qkv[:, WIDTH:2 * WIDTH].reshape(s, N_HEADS, HEAD_DIM)
    q_aug = jnp.concatenate([q_a, cum_cols, ones, zeros], axis=-1).reshape(s, N_HEADS * AUG)
    k_aug = jnp.concatenate([k_a, ones, -cum_cols, zeros], axis=-1).reshape(s, N_HEADS * AUG)
    o_a_t, lse_a = _fox_fwd(q_aug, k_aug, qkv, "fox_fwd")
    bias_t = _rel_bias_tile(wl["rel_bias"])
    o_c_t, lse_c = _chunk_fwd(qkv, bias_t, "chunk_fwd")
    o_a = o_a_t.transpose(2, 0, 1).reshape(s, WIDTH).astype(BF16)
    o_c = o_c_t.transpose(2, 0, 1).reshape(s, WIDTH).astype(BF16)

    merged = _merge_fwd(o_a, o_c, wl["w_br_fox"], wl["w_br_chunk"], gates, "merge_fwd")
    mix, x1, h2 = _mm_res_ln(merged, wl["w_out"], x, g1, wl["ln1_g"], wl["ln1_b"], sc2, sh2, alpha, "out_ln1")
    u0 = _mm(h2, wl["w_up"], out_dtype=F32, name="ffn_up", tk=1024, tn=1408, o_halves=True)
    act = _conv_act_fwd(u0, wl["conv_w"], wl["conv_b"], "conv_act_fwd")
    y2, x2, h_next = _mm_res_ln(act, wl["w_down"], x1, g2, wl["ln2_g"], wl["ln2_b"], next_mod[0], next_mod[1],
                                alpha, "down_ln2", tk=1408)
    saved = dict(x=x, h1=h1, gates=gates, f_t=f_t, b_f=b_f, q_aug=q_aug, k_aug=k_aug, qkv=qkv,
                 bias_t=bias_t, o_a_t=o_a_t, o_c_t=o_c_t, lse_a=lse_a, lse_c=lse_c,
                 o_a=o_a, o_c=o_c, merged=merged, mix=mix, x1=x1, h2=h2, u0=u0, act=act, y2=y2)
    return x2, h_next, saved


def _layer_bwd(dx2, sv, wl, mod, alpha):
    sh1, sc1, g1, sh2, sc2, g2 = mod
    s = dx2.shape[0]
    g = {}
    dres2, dy2, sums = _ln_res_bwd(dx2, sv["x1"], sv["y2"], g2, wl["ln2_g"], alpha, "ln2_bwd")
    g["ln2_g"], g["ln2_b"], dg2 = sums[0], sums[1], sums[2]
    dact = _mm(dy2, wl["w_down_t"], out_dtype=F32, name="dact", tk=1024, tn=1408)
    g["w_down"] = _mm(sv["act"], dy2, out_dtype=F32, name="g_w_down", ta=True, tm=1408, tk=1024)
    du0, csum = _conv_act_bwd(dact, sv["u0"], wl["conv_w"], wl["conv_b"], "conv_act_bwd")
    g["conv_w"] = jnp.concatenate([csum[0, 0:3], csum[1, 0:3]], axis=1)
    g["conv_b"] = jnp.concatenate([csum[0, 3], csum[1, 3]])
    dh2 = _mm(du0, wl["w_up_t"], out_dtype=F32, name="dh2", tk=1408, a_halves=True)
    g["w_up"] = _mm(sv["h2"], du0, out_dtype=F32, name="g_w_up", ta=True, tk=1024, tn=1408, b_halves=True)
    dx1, sums = _ln_mod_bwd(dh2, sv["x1"], sc2, dres2, "ln_mod2_bwd")
    dsh2, dsc2 = sums[0], sums[1]

    dres1, dmix, sums = _ln_res_bwd(dx1, sv["x"], sv["mix"], g1, wl["ln1_g"], alpha, "ln1_bwd")
    g["ln1_g"], g["ln1_b"], dg1 = sums[0], sums[1], sums[2]
    g["w_out"] = _mm(sv["merged"], dmix, out_dtype=F32, name="g_w_out", ta=True, tk=1024)
    dba, dbc, do_a, do_c, dgates = _merge_bwd(
        dmix, wl["w_out_t"], sv["o_a"], sv["o_c"], wl["w_br_fox"], wl["w_br_chunk"],
        wl["w_br_fox_t"], wl["w_br_chunk_t"], sv["gates"], "merge_bwd")
    g["w_br_fox"] = _mm(sv["o_a"], dba, out_dtype=F32, name="g_w_br_fox", ta=True, tk=1024)
    g["w_br_chunk"] = _mm(sv["o_c"], dbc, out_dtype=F32, name="g_w_br_chunk", ta=True, tk=1024)
    n_t = s // min(TQ, s)
    stats = lambda a: a.reshape(N_HEADS, n_t, 1, s // n_t)

    delta_a = _row_dot(do_a.T.reshape(N_HEADS, HEAD_DIM, s), sv["o_a_t"], "delta_fox")
    dk_a, dv_a, dq_a_t, dsum, dqsum = _fox_bwd(sv["q_aug"], sv["k_aug"], sv["qkv"], do_a, sv["lse_a"],
                                               stats(delta_a), "fox_bwd")
    dq_a = (dq_a_t * QK_SCALE).transpose(1, 3, 0, 2).reshape(s, WIDTH).astype(BF16)
    d_cum = dqsum.reshape(N_HEADS, s) - jnp.sum(dsum.reshape(s, N_HEADS, 128), axis=-1).T
    df_t, db_f = _forget_bwd(d_cum.reshape(N_HEADS, s // 128, 128), sv["f_t"], sv["b_f"], "forget_bwd")
    g["b_f"] = db_f[:, 0, 0]
    df_t = df_t.reshape(N_HEADS, s)

    delta_c = _row_dot(do_c.T.reshape(N_HEADS, HEAD_DIM, s), sv["o_c_t"], "delta_chunk")
    dq_c_t, dk_c, dv_c, dbias_t = _chunk_bwd(sv["qkv"], do_c, sv["lse_c"], stats(delta_c), sv["bias_t"], "chunk_bwd")
    g["rel_bias"] = _rel_bias_tile_grad(dbias_t)
    dq_c = dq_c_t.transpose(2, 0, 1).reshape(s, WIDTH).astype(BF16)

    dqkv = jnp.concatenate([dq_a, dk_a, dv_a, dq_c, dk_c, dv_c], axis=1)
    df_pad = jnp.zeros((16 - N_HEADS, s), F32)
    df16 = jnp.concatenate([df_t, df_pad], axis=0).astype(BF16)
    df_cols = jnp.concatenate([df16.T, jnp.zeros((s, AUG - 16), BF16)], axis=1)

    dh1 = _mm(dqkv, wl["w_qkv_t"], out_dtype=F32, name="dh1_qkv", tk=1024)
    dh1g = _mm(dgates, wl["w_gates_t"], out_dtype=F32, name="dh1_gates", tk=1024)
    g_qkv = _mm(sv["h1"], dqkv, out_dtype=F32, name="g_w_qkv", ta=True, tk=1024)
    g_gates = _mm(sv["h1"], dgates, out_dtype=F32, name="g_w_gates", ta=True, tk=1024)
    g_f_t = _mm(df16, sv["h1"], out_dtype=F32, name="g_w_f", tk=1024)[:N_HEADS]
    g["w_in"] = jnp.concatenate([g_qkv[:, :3 * WIDTH], g_f_t.T, g_qkv[:, 3 * WIDTH:], g_gates], axis=1)
    dx, sums = _ln_mod_bwd(dh1, sv["x"], sc1, dres1, "ln_mod1_bwd", extra=(dh1g, df_cols, wl["w_f_pad"]))
    dsh1, dsc1 = sums[0], sums[1]
    g["mod"] = jnp.concatenate([dsh1, dsc1, dg1, dsh2, dsc2, dg2])
    return dx, g


def kernel(x, c, w_in, b_f, rel_bias, w_br_fox, w_br_chunk, w_out, w_up, conv_w, conv_b, w_down, w_ada, b_ada, ln1_g, ln1_b, ln2_g, ln2_b, loss_target, m_w_in, m_b_f, m_rel_bias, m_w_br_fox, m_w_br_chunk, m_w_out, m_w_up, m_conv_w, m_conv_b, m_w_down, m_w_ada, m_b_ada, m_ln1_g, m_ln1_b, m_ln2_g, m_ln2_b, v_w_in, v_b_f, v_rel_bias, v_w_br_fox, v_w_br_chunk, v_w_out, v_w_up, v_conv_w, v_conv_b, v_w_down, v_w_ada, v_b_ada, v_ln1_g, v_ln1_b, v_ln2_g, v_ln2_b):
    w = dict(w_in=w_in, b_f=b_f, rel_bias=rel_bias, w_br_fox=w_br_fox, w_br_chunk=w_br_chunk, w_out=w_out,
             w_up=w_up, conv_w=conv_w, conv_b=conv_b, w_down=w_down, w_ada=w_ada, b_ada=b_ada,
             ln1_g=ln1_g, ln1_b=ln1_b, ln2_g=ln2_g, ln2_b=ln2_b)
    m = dict(w_in=m_w_in, b_f=m_b_f, rel_bias=m_rel_bias, w_br_fox=m_w_br_fox, w_br_chunk=m_w_br_chunk,
             w_out=m_w_out, w_up=m_w_up, conv_w=m_conv_w, conv_b=m_conv_b, w_down=m_w_down, w_ada=m_w_ada,
             b_ada=m_b_ada, ln1_g=m_ln1_g, ln1_b=m_ln1_b, ln2_g=m_ln2_g, ln2_b=m_ln2_b)
    v = dict(w_in=v_w_in, b_f=v_b_f, rel_bias=v_rel_bias, w_br_fox=v_w_br_fox, w_br_chunk=v_w_br_chunk,
             w_out=v_w_out, w_up=v_w_up, conv_w=v_conv_w, conv_b=v_conv_b, w_down=v_w_down, w_ada=v_w_ada,
             b_ada=v_b_ada, ln1_g=v_ln1_g, ln1_b=v_ln1_b, ln2_g=v_ln2_g, ln2_b=v_ln2_b)
    depth, d, _ = w_in.shape
    s = x.shape[1]
    alpha = (2.0 * depth) ** 0.25
    me = 4 * lax.axis_index("x") + 2 * lax.axis_index("y") + lax.axis_index("c")
    x0 = x.reshape(s, d)
    target = loss_target.reshape(s, d)

    small_in, small_rows = _pack([c, conv_w], mult=8)
    small_all = _all_gather(small_in, "gather_c_conv")
    c_all, conv_w_all = _unpack(small_all, small_rows, [c.shape, conv_w.shape], lead=(N_DEV,))
    c_all = c_all.reshape(N_DEV, d)
    conv_w_full = conv_w_all.transpose(1, 2, 0, 3).reshape(depth, conv_w.shape[1], -1)
    n_ada = w_ada.shape[2]
    b_ada_mine = lax.dynamic_slice_in_dim(b_ada, me * n_ada, n_ada, axis=1).reshape(depth, 1, n_ada)
    mod_part, cond_all = _mod_part(c_all, w_ada, b_ada_mine, "mod_part")
    mod_all = _all_gather(mod_part.reshape(depth * N_DEV, n_ada), "gather_mod")
    mod_all = mod_all.reshape(N_DEV, depth, N_DEV, n_ada)
    mod_mine = lax.dynamic_index_in_dim(mod_all, me, axis=2, keepdims=False)
    mod_mine = mod_mine.transpose(1, 0, 2).reshape(depth, N_MOD, 1, d)

    w_pack, w_rows = _pack([w[n].astype(BF16) for n in BIG])
    w_all = _all_gather(w_pack, "gather_weights")
    parts = _unpack(w_all, w_rows, [w[n].shape for n in BIG], lead=(N_DEV,))
    full = {}
    for n, p in zip(BIG, parts):
        axis = 1 if n in ("w_out", "w_down") else 2
        p = jnp.moveaxis(p, 0, axis)
        shp = list(w[n].shape)
        shp[axis] *= N_DEV
        full[n] = p.reshape(shp)
    cols = np.cumsum([0, WIDTH, WIDTH, WIDTH, N_HEADS, WIDTH, WIDTH, WIDTH, d, d])
    layers = []
    for l in range(depth):
        wi = full["w_in"][l]
        w_qkv = jnp.concatenate([wi[:, :cols[3]], wi[:, cols[4]:cols[7]]], axis=1)
        w_gates = wi[:, cols[7]:]
        w_f_t = jnp.concatenate([wi[:, cols[3]:cols[4]].T, jnp.zeros((16 - N_HEADS, d), BF16)], axis=0)
        w_f_pad = jnp.concatenate([w_f_t, jnp.zeros((AUG - 16, d), BF16)], axis=0)
        row = lambda a: a[l].reshape(1, -1)
        layers.append(dict(
            w_qkv=w_qkv, w_gates=w_gates, w_f_t=w_f_t, w_f_pad=w_f_pad, w_qkv_t=w_qkv.T, w_gates_t=w_gates.T,
            w_br_fox=full["w_br_fox"][l], w_br_chunk=full["w_br_chunk"][l],
            w_br_fox_t=full["w_br_fox"][l].T, w_br_chunk_t=full["w_br_chunk"][l].T,
            w_out=full["w_out"][l], w_out_t=full["w_out"][l].T, w_up=full["w_up"][l], w_up_t=full["w_up"][l].T,
            w_down=full["w_down"][l], w_down_t=full["w_down"][l].T,
            conv_w=conv_w_full[l].reshape(3, 2, -1).transpose(1, 0, 2), conv_b=conv_b[l].reshape(2, 1, -1),
            b_f=b_f[l], rel_bias=rel_bias[l],
            ln1_g=row(ln1_g), ln1_b=row(ln1_b), ln2_g=row(ln2_g), ln2_b=row(ln2_b)))

    xs, saved = x0, []
    hs = _ln_mod(x0, mod_mine[0][1], mod_mine[0][0], "ln_mod1")
    for l in range(depth):
        nxt = mod_mine[min(l + 1, depth - 1)]
        xs, hs, sv = _layer_fwd(xs, hs, layers[l], list(mod_mine[l]), (nxt[1], nxt[0]), alpha)
        saved.append(sv)
    dxs, loss_part = _loss_grad(xs, target, "loss_grad")
    loss = lax.psum(loss_part[0, 0], AXES)
    grads = [None] * depth
    for l in reversed(range(depth)):
        dxs, grads[l] = _layer_bwd(dxs, saved[l], layers[l], list(mod_mine[l]), alpha)
    grad_x = dxs.reshape(1, s, d)
    stack = lambda n: jnp.stack([grads[l][n] for l in range(depth)])

    rep = ["b_ada"] + [n for n in SMALL if n != "b_ada"]
    small_g = [stack("mod")] + [stack(n) for n in rep[1:]] + [stack("conv_w")]
    small_pack, small_g_rows = _pack(small_g, mult=8)
    small_parts = _all_gather(small_pack, "gather_small_grads")
    n_rep_rows = sum(small_g_rows[:-1])
    dmod_all = _unpack(small_parts, small_g_rows[:1], [(depth, N_MOD * d)], lead=(N_DEV,))[0]
    cw_all = _unpack(small_parts[:, n_rep_rows:], small_g_rows[-1:], [small_g[-1].shape], lead=(N_DEV,))[0]
    slabs = []
    for n in BIG:
        axis = 1 if n in ("w_out", "w_down") else 2
        sl = _dest_major(stack(n), axis).astype(BF16)
        slabs.append(jnp.stack([_rows128(sl[j]) for j in range(N_DEV)]))
    big_parts = _all_to_all(jnp.concatenate(slabs, axis=1), "scatter_grads")

    out = {}
    packs = [_pack([t[n] for n in BIG])[0] for t in (w, m, v)]
    res = _adamw(big_parts, *packs, "adamw_big")
    for kind, r in zip(("grad", "delta", "new_m", "new_v"), res):
        for n, a in zip(BIG, _unpack(r, w_rows, [w[n].shape for n in BIG])):
            out[kind, n] = a

    dmod_mine = lax.dynamic_slice_in_dim(dmod_all, me * n_ada, n_ada, axis=2)
    g_ada = jnp.stack([_mm(cond_all.T, dmod_mine[:, l], out_dtype=F32, name="g_w_ada", tk=N_DEV, tn=n_ada,
                           precision=HI) for l in range(depth)])
    lp = [_rows128(t["w_ada"], 8) for t in (w, m, v)]
    res = _adamw(_rows128(g_ada, 8)[None], *lp, "adamw_ada")
    for kind, r in zip(("grad", "delta", "new_m", "new_v"), res):
        out[kind, "w_ada"] = _unpack(r, [lp[0].shape[0]], [w_ada.shape])[0]

    rp = [_pack([t[n] for n in rep], mult=8)[0] for t in (w, m, v)]
    res = _adamw(small_parts[:, :n_rep_rows], *rp, "adamw_small")
    for kind, r in zip(("grad", "delta", "new_m", "new_v"), res):
        for n, a in zip(rep, _unpack(r, small_g_rows[:-1], [w[n].shape for n in rep])):
            out[kind, n] = a

    n_cw = conv_w.shape[2]
    cw_mine = lax.dynamic_slice_in_dim(cw_all, me * n_cw, n_cw, axis=3)
    cp = [_rows128(t["conv_w"], 8) for t in (w, m, v)]
    cw_in = jnp.stack([_rows128(cw_mine[i], 8) for i in range(N_DEV)])
    res = _adamw(cw_in, *cp, "adamw_conv_w")
    for kind, r in zip(("grad", "delta", "new_m", "new_v"), res):
        out[kind, "conv_w"] = _unpack(r, [cp[0].shape[0]], [conv_w.shape])[0]

    result = [loss, grad_x]
    for kind in ("grad", "delta", "new_m", "new_v"):
        result += [out[kind, n] for n in ORDER]
    return tuple(result)
```

```python
import functools
import math

import jax
import jax.numpy as jnp
import numpy as np
from jax import lax
from jax.experimental import pallas as pl
from jax.experimental.pallas import tpu as pltpu

F32 = jnp.float32
BF16 = jnp.bfloat16
HI = lax.Precision.HIGHEST
MESH_ID = pl.DeviceIdType.MESH
AXES = ("x", "y", "c")
N_DEV = 8

N_HEADS = 8
HEAD_DIM = 64
WIDTH = N_HEADS * HEAD_DIM
CHUNK = 64
LEFT_CHUNKS = 8
BAND = (LEFT_CHUNKS + 1) * CHUNK
REL_CLIP = 128
LN_EPS = 1e-5
N_MOD = 6
QK_SCALE = 1.0 / math.sqrt(HEAD_DIM)
NEG = -1e30
AUG = 128
N_SPLIT = 3
TQ = 512
SUB = 2 * CHUNK
SUBK = BAND + CHUNK

ADAM_LR, ADAM_B1, ADAM_B2, ADAM_EPS, ADAM_WD, ADAM_STEP = 0.001, 0.9, 0.999, 1e-08, 0.01, 10

VMEM_LIMIT = 56 * 2 ** 20


def _pcall(body, **kw):
    return pl.pallas_call(body, **kw)


def _cp(*sem):
    return pltpu.CompilerParams(dimension_semantics=sem if sem else None, vmem_limit_bytes=VMEM_LIMIT)


def _my_place():
    return lax.axis_index("x"), lax.axis_index("y"), lax.axis_index("c")


N_PEER = N_DEV - 1


def _comm_call(body, xs, out_shapes, name):
    n = len(xs)
    hbm = pl.BlockSpec(memory_space=pltpu.HBM)
    return _pcall(
        functools.partial(body, n), name=name, out_shape=out_shapes, in_specs=[hbm] * n, out_specs=[hbm] * n,
        scratch_shapes=[pltpu.SemaphoreType.DMA((N_PEER * n,)), pltpu.SemaphoreType.DMA((N_PEER * n,)),
                        pltpu.SemaphoreType.DMA((n,))],
    )(*xs)


def _all_gather(xs, name):
    def body(n, *refs):
        x_refs, out_refs, (send_sems, recv_sems, local_sems) = refs[:n], refs[n:2 * n], refs[2 * n:]
        mx, my, mc = _my_place()
        me, sibling = (mx, my, mc), (mx, my, 1 - mc)
        chips = [(1 - mx, my), (mx, 1 - my), (1 - mx, 1 - my)]

        def slot(a, px, py, pc):
            return out_refs[a].at[4 * px + 2 * py + pc]

        def copy(a, k, block, to, own=False):
            return pltpu.make_async_remote_copy(
                src_ref=x_refs[a] if own else slot(a, *block), dst_ref=slot(a, *block),
                send_sem=send_sems.at[a * N_PEER + k], recv_sem=recv_sems.at[a * N_PEER + k],
                device_id=to, device_id_type=MESH_ID)

        arrays = range(n)
        mine = [pltpu.make_async_copy(x_refs[a], slot(a, *me), local_sems.at[a]) for a in arrays]
        first = [copy(a, 0, me, sibling, own=True) for a in arrays]
        first += [copy(a, 1 + j, me, (*chip, mc), own=True) for a in arrays for j, chip in enumerate(chips)]
        for cp in mine + first:
            cp.start()
        passed = []
        for a in arrays:
            for j, chip in enumerate(chips):
                copy(a, 1 + j, (*chip, mc), me).wait_recv()
                passed.append(copy(a, 4 + j, (*chip, mc), sibling))
                passed[-1].start()
        for a in arrays:
            copy(a, 0, sibling, me).wait_recv()
            for j, chip in enumerate(chips):
                copy(a, 4 + j, (*chip, 1 - mc), me).wait_recv()
        for cp in first + passed:
            cp.wait_send()
        for cp in mine:
            cp.wait()

    return _comm_call(body, xs, [jax.ShapeDtypeStruct((N_DEV,) + x.shape, x.dtype) for x in xs], name)


def _all_to_all(xs, name):
    def body(n, *refs):
        x_refs, out_refs, (send_sems, recv_sems, local_sems) = refs[:n], refs[n:2 * n], refs[2 * n:]
        mx, my, mc = _my_place()
        me = 4 * mx + 2 * my + mc
        mine = [pltpu.make_async_copy(x_refs[a].at[me], out_refs[a].at[me], local_sems.at[a]) for a in range(n)]
        copies = []
        for a in range(n):
            for k in range(N_PEER):
                px, py, pc = mx ^ ((k + 1) >> 2), my ^ (((k + 1) >> 1) & 1), mc ^ ((k + 1) & 1)
                copies.append(pltpu.make_async_remote_copy(
                    src_ref=x_refs[a].at[4 * px + 2 * py + pc], dst_ref=out_refs[a].at[me],
                    send_sem=send_sems.at[a * N_PEER + k], recv_sem=recv_sems.at[a * N_PEER + k],
                    device_id=(px, py, pc), device_id_type=MESH_ID))
        for cp in mine + copies:
            cp.start()
        for cp in copies:
            cp.wait_recv()
        for cp in copies:
            cp.wait_send()
        for cp in mine:
            cp.wait()

    return _comm_call(body, xs, [jax.ShapeDtypeStruct(x.shape, x.dtype) for x in xs], name)


def _tile(n, pref, unit=128):
    if n <= pref:
        return n
    t = pref - pref % unit
    while t > unit and n % t:
        t -= unit
    assert n % t == 0, (n, pref, unit)
    return t


def _mm(a, b, *, out_dtype, name, ta=False, tb=False, tm=1024, tn=1024, tk=512, precision=None,
        a_halves=False, b_halves=False, o_halves=False):
    a_shape = (a.shape[1], 2 * a.shape[2]) if a_halves else a.shape
    b_shape = (b.shape[1], 2 * b.shape[2]) if b_halves else b.shape
    (k_dim, m) = a_shape if ta else a_shape[::-1]
    n = b_shape[0] if tb else b_shape[1]
    tm, tn = _tile(m, tm, 8), _tile(n // 2 if (b_halves or o_halves) else n, tn)
    tk = _tile(k_dim // 2 if a_halves else k_dim, tk)
    nk = k_dim // tk
    dims = (((0 if ta else 1,), (1 if tb else 0,)), ((), ()))

    def body(a_ref, b_ref, o_ref, acc_ref):
        k = pl.program_id(2)

        @pl.when(k == 0)
        def _():
            acc_ref[...] = jnp.zeros_like(acc_ref)

        acc_ref[...] += lax.dot_general(a_ref[...], b_ref[...], dims, preferred_element_type=F32,
                                        precision=precision)

        @pl.when(k == nk - 1)
        def _():
            o_ref[...] = acc_ref[...].astype(o_ref.dtype)

    def spec(shape2, pick, halves, per_half):
        if not halves:
            return pl.BlockSpec(shape2, pick)

        def index(i, j, k):
            r, c = pick(i, j, k)
            return (c // per_half, r, c % per_half)
        return pl.BlockSpec((None,) + shape2, index)

    assert not (ta and a_halves) and not (tb and b_halves)
    a_spec = spec((tk, tm), lambda i, j, k: (k, i), False, 0) if ta else \
        spec((tm, tk), lambda i, j, k: (i, k), a_halves, (k_dim // 2) // tk if a_halves else 0)
    b_spec = spec((tn, tk), lambda i, j, k: (j, k), False, 0) if tb else \
        spec((tk, tn), lambda i, j, k: (k, j), b_halves, (n // 2) // tn if b_halves else 0)
    o_spec = spec((tm, tn), lambda i, j, k: (i, j), o_halves, (n // 2) // tn if o_halves else 0)
    out_shape = (2, m, n // 2) if o_halves else (m, n)
    return _pcall(
        body, name=name,
        out_shape=jax.ShapeDtypeStruct(out_shape, out_dtype),
        grid=(m // tm, n // tn, nk),
        in_specs=[a_spec, b_spec],
        out_specs=o_spec,
        scratch_shapes=[pltpu.VMEM((tm, tn), F32)],
        compiler_params=_cp("parallel", "parallel", "arbitrary"),
    )(a, b)


def _ln_stats(x):
    mu = jnp.mean(x, axis=-1, keepdims=True)
    xc = x - mu
    var = jnp.mean(xc * xc, axis=-1, keepdims=True)
    rstd = lax.rsqrt(var + LN_EPS)
    return xc * rstd, rstd


def _ln_mod(x, sc, sh, name, tm=512):
    s, d = x.shape
    tm = min(tm, s)

    def body(x_ref, sc_ref, sh_ref, h_ref):
        xhat, _ = _ln_stats(x_ref[...])
        h_ref[...] = (xhat * (1.0 + sc_ref[...]) + sh_ref[...]).astype(h_ref.dtype)

    row = pl.BlockSpec((1, d), lambda i: (0, 0))
    return _pcall(
        body, name=name, out_shape=jax.ShapeDtypeStruct((s, d), BF16), grid=(s // tm,),
        in_specs=[pl.BlockSpec((tm, d), lambda i: (i, 0)), row, row],
        out_specs=pl.BlockSpec((tm, d), lambda i: (i, 0)),
        compiler_params=_cp("parallel"),
    )(x, sc, sh)


def _mm_res_ln(a, w, xres, gate, ln_g, ln_b, sc, sh, alpha, name, tm=512, tk=512):
    s, k_dim = a.shape
    d = w.shape[1]
    tm, tk = _tile(s, tm, 8), _tile(k_dim, tk)
    nk = k_dim // tk

    def body(a_ref, w_ref, x_ref, g_ref, lg_ref, lb_ref, sc_ref, sh_ref, y_ref, xn_ref, h_ref, acc_ref):
        k = pl.program_id(1)

        @pl.when(k == 0)
        def _():
            acc_ref[...] = jnp.zeros_like(acc_ref)

        acc_ref[...] += jnp.dot(a_ref[...], w_ref[...], preferred_element_type=F32)

        @pl.when(k == nk - 1)
        def _():
            y = acc_ref[...]
            y_ref[...] = y
            z = alpha * x_ref[...] + (1.0 + g_ref[...]) * y
            zhat, _ = _ln_stats(z)
            xn = zhat * lg_ref[...] + lb_ref[...]
            xn_ref[...] = xn
            xhat, _ = _ln_stats(xn)
            h_ref[...] = (xhat * (1.0 + sc_ref[...]) + sh_ref[...]).astype(h_ref.dtype)

    row = pl.BlockSpec((1, d), lambda i, k: (0, 0))
    tile = pl.BlockSpec((tm, d), lambda i, k: (i, 0))
    return _pcall(
        body, name=name,
        out_shape=(jax.ShapeDtypeStruct((s, d), F32), jax.ShapeDtypeStruct((s, d), F32),
                   jax.ShapeDtypeStruct((s, d), BF16)),
        grid=(s // tm, nk),
        in_specs=[pl.BlockSpec((tm, tk), lambda i, k: (i, k)), pl.BlockSpec((tk, d), lambda i, k: (k, 0)),
                  tile, row, row, row, row, row],
        out_specs=(tile, tile, tile),
        scratch_shapes=[pltpu.VMEM((tm, d), F32)],
        compiler_params=_cp("parallel", "arbitrary"),
    )(a, w, xres, gate, ln_g, ln_b, sc, sh)


def _colsum(v):
    return jnp.sum(v, axis=0, keepdims=True)


def _ln_res_bwd(dxn, xres, y, gate, ln_g, alpha, name, tm=256):
    s, d = dxn.shape
    tm = min(tm, s)

    def body(dxn_ref, x_ref, y_ref, g_ref, lg_ref, dres_ref, dy_ref, sums_ref):
        @pl.when(pl.program_id(0) == 0)
        def _():
            sums_ref[...] = jnp.zeros_like(sums_ref)

        dxn_v, y_v = dxn_ref[...], y_ref[...]
        one_g = 1.0 + g_ref[...]
        zhat, rstd = _ln_stats(alpha * x_ref[...] + one_g * y_v)
        dzh = dxn_v * lg_ref[...]
        dz = rstd * (dzh - jnp.mean(dzh, axis=-1, keepdims=True)
                     - zhat * jnp.mean(dzh * zhat, axis=-1, keepdims=True))
        dres_ref[...] = alpha * dz
        dy_ref[...] = (one_g * dz).astype(dy_ref.dtype)
        sums_ref[0:1, :] += _colsum(dxn_v * zhat)
        sums_ref[1:2, :] += _colsum(dxn_v)
        sums_ref[2:3, :] += _colsum(dz * y_v)

    row = pl.BlockSpec((1, d), lambda i: (0, 0))
    tile = pl.BlockSpec((tm, d), lambda i: (i, 0))
    return _pcall(
        body, name=name,
        out_shape=(jax.ShapeDtypeStruct((s, d), F32), jax.ShapeDtypeStruct((s, d), BF16),
                   jax.ShapeDtypeStruct((8, d), F32)),
        grid=(s // tm,),
        in_specs=[tile, tile, tile, row, row],
        out_specs=(tile, tile, pl.BlockSpec((8, d), lambda i: (0, 0))),
        compiler_params=_cp("arbitrary"),
    )(dxn, xres, y, gate, ln_g)


def _ln_mod_bwd(dh, x, sc, dres, name, extra=None, tm=256):
    s, d = dh.shape
    tm = min(tm, s)

    def body(*refs):
        if extra is None:
            dh_ref, x_ref, sc_ref, dres_ref, dx_ref, sums_ref = refs
            dh_v = dh_ref[...]
        else:
            dh_ref, x_ref, sc_ref, dres_ref, dh2_ref, df_ref, wf_ref, dx_ref, sums_ref = refs
            dh_v = dh_ref[...] + dh2_ref[...] + jnp.dot(df_ref[...], wf_ref[...], preferred_element_type=F32)

        @pl.when(pl.program_id(0) == 0)
        def _():
            sums_ref[...] = jnp.zeros_like(sums_ref)

        xhat, rstd = _ln_stats(x_ref[...])
        dxh = dh_v * (1.0 + sc_ref[...])
        dx_ref[...] = dres_ref[...] + rstd * (dxh - jnp.mean(dxh, axis=-1, keepdims=True)
                                             - xhat * jnp.mean(dxh * xhat, axis=-1, keepdims=True))
        sums_ref[0:1, :] += _colsum(dh_v)
        sums_ref[1:2, :] += _colsum(dh_v * xhat)

    row = pl.BlockSpec((1, d), lambda i: (0, 0))
    tile = pl.BlockSpec((tm, d), lambda i: (i, 0))
    in_specs = [tile, tile, row, tile]
    args = [dh, x, sc, dres]
    if extra is not None:
        in_specs += [tile, pl.BlockSpec((tm, AUG), lambda i: (i, 0)), pl.BlockSpec((AUG, d), lambda i: (0, 0))]
        args += list(extra)
    return _pcall(
        body, name=name,
        out_shape=(jax.ShapeDtypeStruct((s, d), F32), jax.ShapeDtypeStruct((8, d), F32)),
        grid=(s // tm,), in_specs=in_specs,
        out_specs=(tile, pl.BlockSpec((8, d), lambda i: (0, 0))),
        compiler_params=_cp("arbitrary"),
    )(*args)


def _loss_grad(xn, target, name, tm=512):
    s, d = xn.shape
    tm = min(tm, s)
    n = s // tm

    def body(x_ref, t_ref, dx_ref, loss_ref, acc_ref):
        i = pl.program_id(0)

        @pl.when(i == 0)
        def _():
            acc_ref[...] = jnp.zeros_like(acc_ref)

        err = x_ref[...] - t_ref[...]
        dx_ref[...] = err * (1.0 / d)
        acc_ref[...] += _colsum(err * err)

        @pl.when(i == n - 1)
        def _():
            loss_ref[...] = jnp.zeros_like(loss_ref) + (0.5 / d) * jnp.sum(acc_ref[...])

    tile = pl.BlockSpec((tm, d), lambda i: (i, 0))
    return _pcall(
        body, name=name,
        out_shape=(jax.ShapeDtypeStruct((s, d), F32), jax.ShapeDtypeStruct((8, 128), F32)),
        grid=(n,), in_specs=[tile, tile],
        out_specs=(tile, pl.BlockSpec((8, 128), lambda i: (0, 0))),
        scratch_shapes=[pltpu.VMEM((1, d), F32)],
        compiler_params=_cp("arbitrary"),
    )(xn, target)


def _sigmoid(v):
    return 0.5 * jnp.tanh(0.5 * v) + 0.5


def _merge_fwd(o_a, o_c, w_a, w_c, gates, name, tm=256):
    s, wd = o_a.shape
    d = w_a.shape[1]
    tm = min(tm, s)

    def body(oa_ref, oc_ref, wa_ref, wc_ref, ga_ref, gc_ref, m_ref):
        ba = jnp.dot(oa_ref[...], wa_ref[...], preferred_element_type=F32)
        bc = jnp.dot(oc_ref[...], wc_ref[...], preferred_element_type=F32)
        m_ref[...] = (_sigmoid(ga_ref[...]) * ba + _sigmoid(gc_ref[...]) * bc).astype(m_ref.dtype)

    o_spec = pl.BlockSpec((tm, wd), lambda i: (i, 0))
    w_spec = pl.BlockSpec((wd, d), lambda i: (0, 0))
    return _pcall(
        body, name=name, out_shape=jax.ShapeDtypeStruct((s, d), BF16), grid=(s // tm,),
        in_specs=[o_spec, o_spec, w_spec, w_spec,
                  pl.BlockSpec((tm, d), lambda i: (i, 0)), pl.BlockSpec((tm, d), lambda i: (i, 1))],
        out_specs=pl.BlockSpec((tm, d), lambda i: (i, 0)),
        compiler_params=_cp("parallel"),
    )(o_a, o_c, w_a, w_c, gates, gates)


def _merge_bwd(dmix, w_out_t, o_a, o_c, w_a, w_c, w_a_t, w_c_t, gates, name, tm=256):
    s, wd = o_a.shape
    d = w_a.shape[1]
    tm = min(tm, s)

    def body(dmix_ref, wot_ref, oa_ref, oc_ref, wa_ref, wc_ref, wat_ref, wct_ref, ga_ref, gc_ref,
             dba_ref, dbc_ref, doa_ref, doc_ref, dg_ref):
        dm = jnp.dot(dmix_ref[...], wot_ref[...], preferred_element_type=F32)
        for half, (o_ref, w_ref, wt_ref, g_ref, db_ref, do_ref) in enumerate((
                (oa_ref, wa_ref, wat_ref, ga_ref, dba_ref, doa_ref),
                (oc_ref, wc_ref, wct_ref, gc_ref, dbc_ref, doc_ref))):
            b = jnp.dot(o_ref[...], w_ref[...], preferred_element_type=F32)
            sg = _sigmoid(g_ref[...])
            db = (dm * sg).astype(BF16)
            db_ref[...] = db
            dg_ref[:, d * half:d * (half + 1)] = (dm * b * sg * (1.0 - sg)).astype(dg_ref.dtype)
            do_ref[...] = jnp.dot(db, wt_ref[...], preferred_element_type=F32).astype(do_ref.dtype)

    row_d = pl.BlockSpec((tm, d), lambda i: (i, 0))
    row_w = pl.BlockSpec((tm, wd), lambda i: (i, 0))
    full = lambda shp: pl.BlockSpec(shp, lambda i: (0, 0))
    return _pcall(
        body, name=name,
        out_shape=(jax.ShapeDtypeStruct((s, d), BF16), jax.ShapeDtypeStruct((s, d), BF16),
                   jax.ShapeDtypeStruct((s, wd), BF16), jax.ShapeDtypeStruct((s, wd), BF16),
                   jax.ShapeDtypeStruct((s, 2 * d), BF16)),
        grid=(s // tm,),
        in_specs=[row_d, full((d, d)), row_w, row_w, full((wd, d)), full((wd, d)), full((d, wd)), full((d, wd)),
                  row_d, pl.BlockSpec((tm, d), lambda i: (i, 1))],
        out_specs=(row_d, row_d, row_w, row_w, pl.BlockSpec((tm, 2 * d), lambda i: (i, 0))),
        compiler_params=_cp("parallel"),
    )(dmix, w_out_t, o_a, o_c, w_a, w_c, w_a_t, w_c_t, gates, gates)


def _shift_down(cur, k, fill_rows):
    out = pltpu.roll(cur, k, axis=0)
    rid = lax.broadcasted_iota(jnp.int32, cur.shape, 0)
    for r, fill in enumerate(fill_rows):
        out = jnp.where(rid == r, fill, out)
    return out


def _shift_up(cur, k, fill_rows):
    n = cur.shape[0]
    out = pltpu.roll(cur, n - k, axis=0)
    rid = lax.broadcasted_iota(jnp.int32, cur.shape, 0)
    for r, fill in enumerate(fill_rows):
        out = jnp.where(rid == n - k + r, fill, out)
    return out


def _conv_rows(cur, prev8, first, w, b):
    p6 = jnp.where(first, 0.0, prev8[6:7, :])
    p7 = jnp.where(first, 0.0, prev8[7:8, :])
    m1 = _shift_down(cur, 1, [p7])
    m2 = _shift_down(cur, 2, [p6, p7])
    u = b + w[0:1, :] * m2 + w[1:2, :] * m1 + w[2:3, :] * cur
    return u, m1, m2


def _conv_specs(tr, tc, order):
    r8 = tr // 8
    at = lambda f: (lambda *g: f(*order(*g)))
    return [pl.BlockSpec((2, tr, tc), at(lambda i, j: (0, i, j))),
            pl.BlockSpec((2, 8, tc), at(lambda i, j: (0, jnp.maximum(i * r8 - 1, 0), j))),
            pl.BlockSpec((2, 3, tc), at(lambda i, j: (0, 0, j))),
            pl.BlockSpec((2, 1, tc), at(lambda i, j: (0, 0, j)))]


def _conv_act_fwd(u0, conv_w, conv_b, name, tr=512, tc=256):
    _, s, f = u0.shape
    tr, tc = _tile(s, tr, 8), _tile(f, tc)

    def body(u_ref, p_ref, w_ref, b_ref, act_ref):
        first = pl.program_id(0) == 0
        a, _, _ = _conv_rows(u_ref[0], p_ref[0], first, w_ref[0], b_ref[0])
        v, _, _ = _conv_rows(u_ref[1], p_ref[1], first, w_ref[1], b_ref[1])
        act_ref[...] = (a * _sigmoid(a) * v).astype(act_ref.dtype)

    return _pcall(
        body, name=name, out_shape=jax.ShapeDtypeStruct((s, f), BF16), grid=(s // tr, f // tc),
        in_specs=_conv_specs(tr, tc, lambda i, j: (i, j)),
        out_specs=pl.BlockSpec((tr, tc), lambda i, j: (i, j)),
        compiler_params=_cp("parallel", "parallel"),
    )(u0, u0, conv_w, conv_b)


def _act_grads(dact, a, v):
    sg = _sigmoid(a)
    return dact * v * sg * (1.0 + a * (1.0 - sg)), dact * a * sg


def _conv_act_bwd(dact, u0, conv_w, conv_b, name, tr=512, tc=256):
    _, s, f = u0.shape
    tr, tc = _tile(s, tr, 8), _tile(f, tc)
    n = s // tr
    r8 = tr // 8

    def body(dact_ref, dnext_ref, u_ref, p_ref, unext_ref, w_ref, b_ref, du0_ref, sums_ref):
        i = pl.program_id(1)
        first, last = i == 0, i == n - 1

        @pl.when(first)
        def _():
            sums_ref[...] = jnp.zeros_like(sums_ref)

        cur = (u_ref[0], u_ref[1])
        (a, a1, a2), (v, v1, v2) = [_conv_rows(cur[hf], p_ref[hf], first, w_ref[hf], b_ref[hf]) for hf in range(2)]
        da, dv = _act_grads(dact_ref[...], a, v)
        (an, _, _), (vn, _, _) = [_conv_rows(unext_ref[hf], cur[hf][tr - 8:tr, :], False, w_ref[hf], b_ref[hf])
                                  for hf in range(2)]
        dan, dvn = _act_grads(dnext_ref[...], an, vn)
        for hf, g, gn, shifted in ((0, da, dan, (a2, a1, cur[0])), (1, dv, dvn, (v2, v1, cur[1]))):
            w = w_ref[hf]
            n0 = jnp.where(last, 0.0, gn[0:1, :])
            n1 = jnp.where(last, 0.0, gn[1:2, :])
            du0 = w[2:3, :] * g + w[1:2, :] * _shift_up(g, 1, [n0]) + w[0:1, :] * _shift_up(g, 2, [n0, n1])
            du0_ref[hf] = du0.astype(du0_ref.dtype)
            for r in range(3):
                sums_ref[hf, r:r + 1, :] += _colsum(g * shifted[r])
            sums_ref[hf, 3:4, :] += _colsum(g)

    nxt = lambda i: jnp.minimum((i + 1) * r8, s // 8 - 1)
    order = lambda j, i: (i, j)
    specs = _conv_specs(tr, tc, order)
    return _pcall(
        body, name=name,
        out_shape=(jax.ShapeDtypeStruct((2, s, f), BF16), jax.ShapeDtypeStruct((2, 8, f), F32)),
        grid=(f // tc, n),
        in_specs=[pl.BlockSpec((tr, tc), lambda j, i: (i, j)), pl.BlockSpec((8, tc), lambda j, i: (nxt(i), j)),
                  specs[0], specs[1], pl.BlockSpec((2, 8, tc), lambda j, i: (0, nxt(i), j)), specs[2], specs[3]],
        out_specs=(pl.BlockSpec((2, tr, tc), lambda j, i: (0, i, j)), pl.BlockSpec((2, 8, tc), lambda j, i: (0, 0, j))),
        compiler_params=_cp("parallel", "arbitrary"),
    )(dact, dact, u0, u0, u0, conv_w, conv_b)


def _cumsum_rows(v, reverse=False):
    r = v.shape[0]
    i128 = lax.broadcasted_iota(jnp.int32, (128, 128), 0), lax.broadcasted_iota(jnp.int32, (128, 128), 1)
    ir = lax.broadcasted_iota(jnp.int32, (r, r), 0), lax.broadcasted_iota(jnp.int32, (r, r), 1)
    in_row = (i128[0] >= i128[1] if reverse else i128[0] <= i128[1]).astype(F32)
    rows = (ir[1] > ir[0] if reverse else ir[1] < ir[0]).astype(F32)
    within = jnp.dot(v, in_row, preferred_element_type=F32, precision=HI)
    tot = jnp.broadcast_to(within[:, 0:1] if reverse else within[:, 127:128], (r, 128))
    return within + jnp.dot(rows, tot, preferred_element_type=F32, precision=HI)


def _forget_fwd(f_t, b_f, name):
    h, r, _ = f_t.shape

    def body(f_ref, b_ref, o_ref):
        z = f_ref[...] + b_ref[...]
        logf = jnp.minimum(z, 0.0) - jnp.log(1.0 + jnp.exp(-jnp.abs(z)))
        rest = _cumsum_rows(logf)
        for i in range(N_SPLIT):
            piece = rest.astype(BF16).astype(F32)
            o_ref[i] = piece
            rest = rest - piece

    return _pcall(
        body, name=name, out_shape=jax.ShapeDtypeStruct((N_SPLIT, h, r, 128), F32), grid=(h,),
        in_specs=[pl.BlockSpec((None, r, 128), lambda i: (i, 0, 0)), pl.BlockSpec((None, 1, 128), lambda i: (i, 0, 0))],
        out_specs=pl.BlockSpec((N_SPLIT, None, r, 128), lambda i: (0, i, 0, 0)),
        compiler_params=_cp("parallel"),
    )(f_t, b_f)


def _forget_bwd(d_cum, f_t, b_f, name):
    h, r, _ = f_t.shape

    def body(g_ref, f_ref, b_ref, df_ref, db_ref):
        df = _cumsum_rows(g_ref[...], reverse=True) * _sigmoid(-(f_ref[...] + b_ref[...]))
        df_ref[...] = df
        db_ref[...] = jnp.zeros_like(db_ref) + jnp.sum(df)

    blk = pl.BlockSpec((None, r, 128), lambda i: (i, 0, 0))
    one = pl.BlockSpec((None, 1, 128), lambda i: (i, 0, 0))
    return _pcall(
        body, name=name,
        out_shape=(jax.ShapeDtypeStruct((h, r, 128), F32), jax.ShapeDtypeStruct((h, 1, 128), F32)),
        grid=(h,), in_specs=[blk, blk, one], out_specs=(blk, one),
        compiler_params=_cp("parallel"),
    )(d_cum, f_t, b_f)


_NT = (((1,), (1,)), ((), ()))


def _causal_keep(tk, tq):
    return lax.broadcasted_iota(jnp.int32, (tk, tq), 0) <= lax.broadcasted_iota(jnp.int32, (tk, tq), 1)


_TN = (((0,), (0,)), ((), ()))
PAIR = 2
V_FOX = 2 * WIDTH // 128
Q_CHUNK, K_CHUNK, V_CHUNK = (3 * WIDTH) // 128, (4 * WIDTH) // 128, (5 * WIDTH) // 128


def _fox_fwd(q_aug, k_aug, qkv, name, t=TQ):
    s = q_aug.shape[0]
    t = min(t, s)
    n = s // t
    dh = HEAD_DIM

    def body(q_ref, k_ref, v_ref, o_ref, lse_ref, m_s, l_s, acc_s):
        qi = pl.program_id(1)
        m_s[...] = jnp.full_like(m_s, NEG)
        l_s[...] = jnp.zeros_like(l_s)
        acc_s[...] = jnp.zeros_like(acc_s)

        def step(kj, diag):
            rows = pl.ds(pl.multiple_of(kj * t, t), t)
            for hh in range(PAIR):
                st = lax.dot_general(k_ref[rows, AUG * hh:AUG * (hh + 1)], q_ref[:, AUG * hh:AUG * (hh + 1)], _NT,
                                     preferred_element_type=F32)
                if diag:
                    st = jnp.where(_causal_keep(t, t), st, NEG)
                m_prev = m_s[hh]
                m_new = jnp.maximum(m_prev, jnp.max(st, axis=0, keepdims=True))
                a = jnp.exp(m_prev - m_new)
                pt = jnp.exp(st - m_new)
                l_s[hh] = a * l_s[hh] + jnp.sum(pt, axis=0, keepdims=True)
                acc_s[hh] = a * acc_s[hh] + lax.dot_general(v_ref[rows, dh * hh:dh * (hh + 1)], pt.astype(BF16), _TN,
                                                            preferred_element_type=F32)
                m_s[hh] = m_new

        def off_diagonal(kj, carry):
            step(kj, False)
            return carry

        lax.fori_loop(0, qi, off_diagonal, 0)
        step(qi, True)
        for hh in range(PAIR):
            o_ref[hh] = acc_s[hh] / l_s[hh]
            lse_ref[hh, 0] = m_s[hh] + jnp.log(l_s[hh])

    return _pcall(
        body, name=name,
        out_shape=(jax.ShapeDtypeStruct((N_HEADS, dh, s), F32), jax.ShapeDtypeStruct((N_HEADS, n, 1, t), F32)),
        grid=(N_HEADS // PAIR, n),
        in_specs=[pl.BlockSpec((t, PAIR * AUG), lambda hp, i: (i, hp)),
                  pl.BlockSpec((s, PAIR * AUG), lambda hp, i: (0, hp)),
                  pl.BlockSpec((s, PAIR * dh), lambda hp, i: (0, V_FOX + hp))],
        out_specs=(pl.BlockSpec((PAIR, dh, t), lambda hp, i: (hp, 0, i)),
                   pl.BlockSpec((PAIR, 1, 1, t), lambda hp, i: (hp, i, 0, 0))),
        scratch_shapes=[pltpu.VMEM((PAIR, 1, t), F32), pltpu.VMEM((PAIR, 1, t), F32), pltpu.VMEM((PAIR, dh, t), F32)],
        compiler_params=_cp("parallel", "arbitrary"),
    )(q_aug, k_aug, qkv)


def _row_dot(a_t, b_t, name, t=2048):
    h, dh, s = a_t.shape
    t = min(t, s)

    def body(a_ref, b_ref, o_ref):
        o_ref[...] = jnp.sum(a_ref[...].astype(F32) * b_ref[...], axis=0, keepdims=True)

    blk = pl.BlockSpec((None, dh, t), lambda hh, i: (hh, 0, i))
    return _pcall(
        body, name=name, out_shape=jax.ShapeDtypeStruct((h, 1, s), F32), grid=(h, s // t),
        in_specs=[blk, blk], out_specs=pl.BlockSpec((None, 1, t), lambda hh, i: (hh, 0, i)),
        compiler_params=_cp("parallel", "parallel"),
    )(a_t, b_t)


def _fox_bwd(q_aug, k_aug, qkv, do, lse, delta, name, t=TQ):
    s = q_aug.shape[0]
    dh = HEAD_DIM
    t = min(t, s)
    n = s // t

    def body(q_ref, k_ref, v_ref, do_ref, lse_ref, dl_ref, dk_ref, dv_ref, dq_ref, dsum_ref, dqsum_ref,
             dk_s, dv_s, dsum_s):
        kj = pl.program_id(1)

        @pl.when(kj == 0)
        def _():
            dq_ref[...] = jnp.zeros_like(dq_ref)
            dqsum_ref[...] = jnp.zeros_like(dqsum_ref)

        dk_s[...] = jnp.zeros_like(dk_s)
        dv_s[...] = jnp.zeros_like(dv_s)
        dsum_s[...] = jnp.zeros_like(dsum_s)

        def step(qi, diag):
            rows = pl.ds(pl.multiple_of(qi * t, t), t)
            for hh in range(PAIR):
                q = q_ref[rows, AUG * hh:AUG * (hh + 1)]
                k = k_ref[:, AUG * hh:AUG * (hh + 1)]
                st = lax.dot_general(k, q, _NT, preferred_element_type=F32)
                if diag:
                    st = jnp.where(_causal_keep(t, t), st, NEG)
                pt = jnp.exp(st - lse_ref[hh, qi])
                do_v = do_ref[rows, dh * hh:dh * (hh + 1)]
                dpt = lax.dot_general(v_ref[:, dh * hh:dh * (hh + 1)], do_v, _NT, preferred_element_type=F32)
                ds32 = pt * (dpt - dl_ref[hh, qi])
                dsum_s[hh] += sum(ds32[:, 128 * u:128 * (u + 1)] for u in range(t // 128))
                dqsum_ref[hh, qi] += jnp.sum(ds32, axis=0, keepdims=True)
                dst = ds32.astype(BF16)
                dv_s[hh] += jnp.dot(pt.astype(BF16), do_v, preferred_element_type=F32)
                dk_s[hh] += jnp.dot(dst, q[:, :dh], preferred_element_type=F32)
                dq_ref[hh, qi] += lax.dot_general(k[:, :dh], dst, _TN, preferred_element_type=F32)

        def off_diagonal(qi, carry):
            step(qi, False)
            return carry

        step(kj, True)
        lax.fori_loop(kj + 1, n, off_diagonal, 0)
        for hh in range(PAIR):
            dk_ref[:, dh * hh:dh * (hh + 1)] = dk_s[hh].astype(dk_ref.dtype)
            dv_ref[:, dh * hh:dh * (hh + 1)] = dv_s[hh].astype(dv_ref.dtype)
            dsum_ref[:, 128 * hh:128 * (hh + 1)] = dsum_s[hh]

    stat = pl.BlockSpec((PAIR, n, 1, t), lambda hp, j: (hp, 0, 0, 0))
    return _pcall(
        body, name=name,
        out_shape=(jax.ShapeDtypeStruct((s, WIDTH), BF16), jax.ShapeDtypeStruct((s, WIDTH), BF16),
                   jax.ShapeDtypeStruct((N_HEADS, n, dh, t), F32), jax.ShapeDtypeStruct((s, N_HEADS * 128), F32),
                   jax.ShapeDtypeStruct((N_HEADS, n, 1, t), F32)),
        grid=(N_HEADS // PAIR, n),
        in_specs=[pl.BlockSpec((s, PAIR * AUG), lambda hp, j: (0, hp)),
                  pl.BlockSpec((t, PAIR * AUG), lambda hp, j: (j, hp)),
                  pl.BlockSpec((t, PAIR * dh), lambda hp, j: (j, V_FOX + hp)),
                  pl.BlockSpec((s, PAIR * dh), lambda hp, j: (0, hp)),
                  stat, stat],
        out_specs=(pl.BlockSpec((t, PAIR * dh), lambda hp, j: (j, hp)),
                   pl.BlockSpec((t, PAIR * dh), lambda hp, j: (j, hp)),
                   pl.BlockSpec((PAIR, n, dh, t), lambda hp, j: (hp, 0, 0, 0)),
                   pl.BlockSpec((t, PAIR * 128), lambda hp, j: (j, hp)),
                   stat),
        scratch_shapes=[pltpu.VMEM((PAIR, t, dh), F32), pltpu.VMEM((PAIR, t, dh), F32), pltpu.VMEM((PAIR, t, 128), F32)],
        compiler_params=_cp("parallel", "arbitrary"),
    )(q_aug, k_aug, qkv, do, lse, delta)


def _chunk_logits(kw, qs, bias, first_tile, j):
    st = lax.dot_general(kw[SUB * j:SUB * j + SUBK], qs, _NT, preferred_element_type=F32) + bias
    row = lax.broadcasted_iota(jnp.int32, st.shape, 0)
    return jnp.where(jnp.logical_or(jnp.logical_not(first_tile), row >= TQ - SUB * j), st, NEG)


def _window(prev_ref, cur_ref, hh):
    cols = slice(HEAD_DIM * hh, HEAD_DIM * (hh + 1))
    return jnp.concatenate([prev_ref[:, cols], cur_ref[:, cols]], axis=0)


def _chunk_fwd(qkv, bias_t, name):
    s = qkv.shape[0]
    dh = HEAD_DIM
    n = s // TQ
    nsub = TQ // SUB
    prev = lambda i: jnp.maximum(i - 1, 0)

    def body(q_ref, kp_ref, kc_ref, vp_ref, vc_ref, b_ref, o_ref, lse_ref):
        first_tile = pl.program_id(1) == 0
        for hh in range(PAIR):
            kw, vw = _window(kp_ref, kc_ref, hh), _window(vp_ref, vc_ref, hh)
            bias = b_ref[hh]
            for j in range(nsub):
                qs = q_ref[SUB * j:SUB * (j + 1), dh * hh:dh * (hh + 1)] * QK_SCALE
                st = _chunk_logits(kw, qs, bias, first_tile, j)
                m = jnp.max(st, axis=0, keepdims=True)
                pt = jnp.exp(st - m)
                l = jnp.sum(pt, axis=0, keepdims=True)
                ot = lax.dot_general(vw[SUB * j:SUB * j + SUBK], pt.astype(BF16), _TN, preferred_element_type=F32)
                o_ref[hh, :, SUB * j:SUB * (j + 1)] = ot / l
                lse_ref[hh, 0, :, SUB * j:SUB * (j + 1)] = m + jnp.log(l)

    blk = lambda col, m: pl.BlockSpec((TQ, PAIR * dh), lambda hp, i: (m(i), col + hp))
    same = lambda i: i
    return _pcall(
        body, name=name,
        out_shape=(jax.ShapeDtypeStruct((N_HEADS, dh, s), F32), jax.ShapeDtypeStruct((N_HEADS, n, 1, TQ), F32)),
        grid=(N_HEADS // PAIR, n),
        in_specs=[blk(Q_CHUNK, same), blk(K_CHUNK, prev), blk(K_CHUNK, same), blk(V_CHUNK, prev), blk(V_CHUNK, same),
                  pl.BlockSpec((PAIR, SUBK, SUB), lambda hp, i: (hp, 0, 0))],
        out_specs=(pl.BlockSpec((PAIR, dh, TQ), lambda hp, i: (hp, 0, i)),
                   pl.BlockSpec((PAIR, 1, 1, TQ), lambda hp, i: (hp, i, 0, 0))),
        compiler_params=_cp("parallel", "parallel"),
    )(qkv, qkv, qkv, qkv, qkv, bias_t)


def _chunk_bwd(qkv, do, lse, delta, bias_t, name):
    s = qkv.shape[0]
    dh = HEAD_DIM
    n = s // TQ
    nsub = TQ // SUB
    cur = lambda i: jnp.minimum(i, n - 1)
    prev = lambda i: jnp.maximum(cur(i) - 1, 0)
    done = lambda i: jnp.maximum(i - 1, 0)

    def body(q_ref, kp_ref, kc_ref, vp_ref, vc_ref, do_ref, lse_ref, dl_ref, b_ref,
             dq_ref, dk_ref, dv_ref, db_ref, dkw_s, dvw_s, ck_s, cv_s):
        i = pl.program_id(1)

        @pl.when(i == 0)
        def _():
            db_ref[...] = jnp.zeros_like(db_ref)
            ck_s[...] = jnp.zeros_like(ck_s)
            cv_s[...] = jnp.zeros_like(cv_s)

        dkw_s[...] = jnp.zeros_like(dkw_s)
        dvw_s[...] = jnp.zeros_like(dvw_s)

        @pl.when(i < n)
        def _():
            first_tile = i == 0
            for hh in range(PAIR):
                kw, vw = _window(kp_ref, kc_ref, hh), _window(vp_ref, vc_ref, hh)
                bias = b_ref[hh]
                for j in range(nsub):
                    rows = pl.ds(SUB * j, SUB)
                    win = pl.ds(SUB * j, SUBK)
                    qs = q_ref[rows, dh * hh:dh * (hh + 1)] * QK_SCALE
                    st = _chunk_logits(kw, qs, bias, first_tile, j)
                    pt = jnp.exp(st - lse_ref[hh, 0, :, rows])
                    do_j = do_ref[rows, dh * hh:dh * (hh + 1)]
                    dpt = lax.dot_general(vw[SUB * j:SUB * j + SUBK], do_j, _NT, preferred_element_type=F32)
                    dst = pt * (dpt - dl_ref[hh, 0, :, rows])
                    db_ref[hh] += dst
                    dsb = dst.astype(BF16)
                    dvw_s[hh, win, :] += jnp.dot(pt.astype(BF16), do_j, preferred_element_type=F32)
                    dkw_s[hh, win, :] += jnp.dot(dsb, qs, preferred_element_type=F32)
                    dq_ref[hh, :, rows] = QK_SCALE * lax.dot_general(kw[SUB * j:SUB * j + SUBK], dsb, _TN,
                                                                     preferred_element_type=F32)

        for hh in range(PAIR):
            cols = slice(dh * hh, dh * (hh + 1))
            dk_ref[:, cols] = (ck_s[hh] + dkw_s[hh, 0:TQ, :]).astype(dk_ref.dtype)
            dv_ref[:, cols] = (cv_s[hh] + dvw_s[hh, 0:TQ, :]).astype(dv_ref.dtype)
        ck_s[...] = dkw_s[:, TQ:2 * TQ, :]
        cv_s[...] = dvw_s[:, TQ:2 * TQ, :]

    blk = lambda col, m: pl.BlockSpec((TQ, PAIR * dh), lambda hp, i: (m(i), col + hp))
    stat = pl.BlockSpec((PAIR, 1, 1, TQ), lambda hp, i: (hp, cur(i), 0, 0))
    bs = pl.BlockSpec((PAIR, SUBK, SUB), lambda hp, i: (hp, 0, 0))
    return _pcall(
        body, name=name,
        out_shape=(jax.ShapeDtypeStruct((N_HEADS, dh, s), F32), jax.ShapeDtypeStruct((s, WIDTH), BF16),
                   jax.ShapeDtypeStruct((s, WIDTH), BF16), jax.ShapeDtypeStruct((N_HEADS, SUBK, SUB), F32)),
        grid=(N_HEADS // PAIR, n + 1),
        in_specs=[blk(Q_CHUNK, cur), blk(K_CHUNK, prev), blk(K_CHUNK, cur), blk(V_CHUNK, prev), blk(V_CHUNK, cur),
                  blk(0, cur), stat, stat, bs],
        out_specs=(pl.BlockSpec((PAIR, dh, TQ), lambda hp, i: (hp, 0, cur(i))), blk(0, done), blk(0, done), bs),
        scratch_shapes=[pltpu.VMEM((PAIR, 2 * TQ, dh), F32), pltpu.VMEM((PAIR, 2 * TQ, dh), F32),
                        pltpu.VMEM((PAIR, TQ, dh), F32), pltpu.VMEM((PAIR, TQ, dh), F32)],
        compiler_params=_cp("parallel", "arbitrary"),
    )(qkv, qkv, qkv, qkv, qkv, do, lse, delta, bias_t)


def _mod_part(c_all, w_ada, b_ada, name):
    nl, d, n = w_ada.shape
    b = c_all.shape[0]

    def body(c_ref, w_ref, b_ref, o_ref, cond_ref):
        cv = c_ref[...]
        cond = cv * _sigmoid(cv)
        cond_ref[...] = cond
        o_ref[...] = jnp.dot(cond, w_ref[...], preferred_element_type=F32, precision=HI) + b_ref[...]

    return _pcall(
        body, name=name,
        out_shape=(jax.ShapeDtypeStruct((nl, b, n), F32), jax.ShapeDtypeStruct((b, d), F32)),
        grid=(nl,),
        in_specs=[pl.BlockSpec((b, d), lambda l: (0, 0)), pl.BlockSpec((None, d, n), lambda l: (l, 0, 0)),
                  pl.BlockSpec((None, 1, n), lambda l: (l, 0, 0))],
        out_specs=(pl.BlockSpec((None, b, n), lambda l: (l, 0, 0)), pl.BlockSpec((b, d), lambda l: (0, 0))),
        compiler_params=_cp("arbitrary"),
    )(c_all, w_ada, b_ada)


def _adamw(parts, w, m, v, name, tr=1024):
    p, r, cp = parts.shape
    c = w.shape[1]
    tr = _tile(r, tr, 16)
    c1 = 1.0 / (1.0 - ADAM_B1 ** ADAM_STEP)
    c2 = 1.0 / (1.0 - ADAM_B2 ** ADAM_STEP)

    def body(p_ref, w_ref, m_ref, v_ref, g_ref, d_ref, nm_ref, nv_ref):
        g = p_ref[0].astype(F32)
        for i in range(1, p):
            g = g + p_ref[i].astype(F32)
        g = g[:, :c]
        nm = ADAM_B1 * m_ref[...] + (1.0 - ADAM_B1) * g
        nv = ADAM_B2 * v_ref[...] + (1.0 - ADAM_B2) * (g * g)
        g_ref[...] = g
        nm_ref[...] = nm
        nv_ref[...] = nv
        d_ref[...] = -ADAM_LR * ((nm * c1) / (jnp.sqrt(nv * c2) + ADAM_EPS) + ADAM_WD * w_ref[...])

    blk = pl.BlockSpec((tr, c), lambda i: (i, 0))
    out = jax.ShapeDtypeStruct((r, c), F32)
    return _pcall(
        body, name=name, out_shape=(out, out, out, out), grid=(r // tr,),
        in_specs=[pl.BlockSpec((p, tr, cp), lambda i: (0, i, 0)), blk, blk, blk],
        out_specs=(blk, blk, blk, blk),
        compiler_params=_cp("parallel"),
    )(parts, w, m, v)


def _pad_lanes(a):
    pad = (-a.shape[-1]) % 128
    return jnp.pad(a, [(0, 0)] * (a.ndim - 1) + [(0, pad)]) if pad else a


def _rows128(a, mult=16):
    flat = a.reshape(-1)
    n = flat.shape[0]
    per = 128 * mult
    pad = (-n) % per
    if pad:
        flat = jnp.concatenate([flat, jnp.zeros((pad,), a.dtype)])
    return flat.reshape(-1, 128)


def _pack(arrs, mult=16):
    pieces = [_rows128(a, mult) for a in arrs]
    return jnp.concatenate(pieces, axis=0), [p.shape[0] for p in pieces]


def _unpack(packed, rows, shapes, lead=()):
    out, at = [], 0
    for r, shp in zip(rows, shapes):
        n = int(np.prod(shp))
        piece = packed[..., at:at + r, :].reshape(lead + (r * 128,))[..., :n]
        out.append(piece.reshape(lead + tuple(shp)))
        at += r
    return out


def _rel_bias_tile(table):
    h = table.shape[0]
    lo = REL_CLIP - (CHUNK - 1)
    n_far = LEFT_CHUNKS * CHUNK + CHUNK - 1 - REL_CLIP
    vec = jnp.concatenate([table[:, lo:2 * REL_CLIP], jnp.repeat(table[:, 2 * REL_CLIP:], n_far + 1, axis=1)], axis=1)
    rev = vec[:, ::-1]
    n_vec = BAND + CHUNK - 1
    skew = jnp.tile(rev, (1, CHUNK + 1))[:, :CHUNK * (n_vec + 1)].reshape(h, CHUNK, n_vec + 1)
    bias = skew[:, ::-1, :BAND]
    neg = jnp.full((h, CHUNK, CHUNK), NEG, F32)
    two = jnp.concatenate([jnp.concatenate([bias, neg], axis=2), jnp.concatenate([neg, bias], axis=2)], axis=1)
    return two.transpose(0, 2, 1)


def _rel_bias_tile_grad(dbias_t):
    h = dbias_t.shape[0]
    two = dbias_t.transpose(0, 2, 1)
    dbias = two[:, :CHUNK, :BAND] + two[:, CHUNK:, CHUNK:]
    n_vec = BAND + CHUNK - 1
    dskew = jnp.pad(dbias[:, ::-1, :], ((0, 0), (0, 0), (0, n_vec + 1 - BAND))).reshape(h, CHUNK * (n_vec + 1))
    dskew = jnp.pad(dskew, ((0, 0), (0, (CHUNK + 1) * n_vec - CHUNK * (n_vec + 1))))
    drev = jnp.sum(dskew.reshape(h, CHUNK + 1, n_vec), axis=1)
    dvec = drev[:, ::-1]
    lo = REL_CLIP - (CHUNK - 1)
    n_near = 2 * REL_CLIP - lo
    return jnp.concatenate([jnp.zeros((h, lo), F32), dvec[:, :n_near],
                            jnp.sum(dvec[:, n_near:], axis=1, keepdims=True)], axis=1)


BIG = ("w_in", "w_br_fox", "w_br_chunk", "w_out", "w_up", "w_down")
SMALL = ("b_f", "rel_bias", "conv_b", "b_ada", "ln1_g", "ln1_b", "ln2_g", "ln2_b")
ORDER = ("w_in", "b_f", "rel_bias", "w_br_fox", "w_br_chunk", "w_out", "w_up", "conv_w", "conv_b", "w_down",
         "w_ada", "b_ada", "ln1_g", "ln1_b", "ln2_g", "ln2_b")


def _dest_major(g, axis):
    shp = g.shape
    g = g.reshape(shp[:axis] + (N_DEV, shp[axis] // N_DEV) + shp[axis + 1:])
    return jnp.moveaxis(g, axis, 0)


def _layer_fwd(x, h1, wl, mod, next_mod, alpha):
    sh1, sc1, g1, sh2, sc2, g2 = mod
    s = x.shape[0]
    qkv = _mm(h1, wl["w_qkv"], out_dtype=BF16, name="proj_qkv", tk=1024)
    gates = _mm(h1, wl["w_gates"], out_dtype=F32, name="proj_gates", tk=1024)
    f_t = _mm(wl["w_f_t"], h1, out_dtype=F32, name="proj_f", tb=True, tk=1024)[:N_HEADS]
    f_t = f_t.reshape(N_HEADS, s // 128, 128)
    b_f = jnp.broadcast_to(wl["b_f"].reshape(N_HEADS, 1, 1), (N_HEADS, 1, 128))
    cum = _forget_fwd(f_t, b_f, "forget_fwd").reshape(N_SPLIT, N_HEADS, s)
    cum_cols = cum.transpose(2, 1, 0).astype(BF16)

    ones = jnp.ones((s, N_HEADS, N_SPLIT), BF16)
    zeros = jnp.zeros((s, N_HEADS, AUG - HEAD_DIM - 2 * N_SPLIT), BF16)
    q_a = qkv[:, :WIDTH].reshape(s, N_HEADS, HEAD_DIM) * QK_SCALE
    k_a = qkv[:, WIDTH:2 * WIDTH].reshape(s, N_HEADS, HEAD_DIM)
    q_aug = jnp.concatenate([q_a, cum_cols, ones, zeros], axis=-1).reshape(s, N_HEADS * AUG)
    k_aug = jnp.concatenate([k_a, ones, -cum_cols, zeros], axis=-1).reshape(s, N_HEADS * AUG)
    o_a_t, lse_a = _fox_fwd(q_aug, k_aug, qkv, "fox_fwd")
    bias_t = _rel_bias_tile(wl["rel_bias"])
    o_c_t, lse_c = _chunk_fwd(qkv, bias_t, "chunk_fwd")
    o_a = o_a_t.transpose(2, 0, 1).reshape(s, WIDTH).astype(BF16)
    o_c = o_c_t.transpose(2, 0, 1).reshape(s, WIDTH).astype(BF16)

    merged = _merge_fwd(o_a, o_c, wl["w_br_fox"], wl["w_br_chunk"], gates, "merge_fwd")
    mix, x1, h2 = _mm_res_ln(merged, wl["w_out"], x, g1, wl["ln1_g"], wl["ln1_b"], sc2, sh2, alpha, "out_ln1")
    u0 = _mm(h2, wl["w_up"], out_dtype=F32, name="ffn_up", tk=1024, tn=1408, o_halves=True)
    act = _conv_act_fwd(u0, wl["conv_w"], wl["conv_b"], "conv_act_fwd")
    y2, x2, h_next = _mm_res_ln(act, wl["w_down"], x1, g2, wl["ln2_g"], wl["ln2_b"], next_mod[0], next_mod[1],
                                alpha, "down_ln2", tk=1408)
    saved = dict(x=x, h1=h1, gates=gates, f_t=f_t, b_f=b_f, q_aug=q_aug, k_aug=k_aug, qkv=qkv,
                 bias_t=bias_t, o_a_t=o_a_t, o_c_t=o_c_t, lse_a=lse_a, lse_c=lse_c,
                 o_a=o_a, o_c=o_c, merged=merged, mix=mix, x1=x1, h2=h2, u0=u0, act=act, y2=y2)
    return x2, h_next, saved


def _layer_bwd(dx2, sv, wl, mod, alpha):
    sh1, sc1, g1, sh2, sc2, g2 = mod
    s = dx2.shape[0]
    g = {}
    dres2, dy2, sums = _ln_res_bwd(dx2, sv["x1"], sv["y2"], g2, wl["ln2_g"], alpha, "ln2_bwd")
    g["ln2_g"], g["ln2_b"], dg2 = sums[0], sums[1], sums[2]
    dact = _mm(dy2, wl["w_down_t"], out_dtype=F32, name="dact", tk=1024, tn=1408)
    g["w_down"] = _mm(sv["act"], dy2, out_dtype=F32, name="g_w_down", ta=True, tm=1408, tk=1024)
    du0, csum = _conv_act_bwd(dact, sv["u0"], wl["conv_w"], wl["conv_b"], "conv_act_bwd")
    g["conv_w"] = jnp.concatenate([csum[0, 0:3], csum[1, 0:3]], axis=1)
    g["conv_b"] = jnp.concatenate([csum[0, 3], csum[1, 3]])
    dh2 = _mm(du0, wl["w_up_t"], out_dtype=F32, name="dh2", tk=1408, a_halves=True)
    g["w_up"] = _mm(sv["h2"], du0, out_dtype=F32, name="g_w_up", ta=True, tk=1024, tn=1408, b_halves=True)
    dx1, sums = _ln_mod_bwd(dh2, sv["x1"], sc2, dres2, "ln_mod2_bwd")
    dsh2, dsc2 = sums[0], sums[1]

    dres1, dmix, sums = _ln_res_bwd(dx1, sv["x"], sv["mix"], g1, wl["ln1_g"], alpha, "ln1_bwd")
    g["ln1_g"], g["ln1_b"], dg1 = sums[0], sums[1], sums[2]
    g["w_out"] = _mm(sv["merged"], dmix, out_dtype=F32, name="g_w_out", ta=True, tk=1024)
    dba, dbc, do_a, do_c, dgates = _merge_bwd(
        dmix, wl["w_out_t"], sv["o_a"], sv["o_c"], wl["w_br_fox"], wl["w_br_chunk"],
        wl["w_br_fox_t"], wl["w_br_chunk_t"], sv["gates"], "merge_bwd")
    g["w_br_fox"] = _mm(sv["o_a"], dba, out_dtype=F32, name="g_w_br_fox", ta=True, tk=1024)
    g["w_br_chunk"] = _mm(sv["o_c"], dbc, out_dtype=F32, name="g_w_br_chunk", ta=True, tk=1024)
    n_t = s // min(TQ, s)
    stats = lambda a: a.reshape(N_HEADS, n_t, 1, s // n_t)

    delta_a = _row_dot(do_a.T.reshape(N_HEADS, HEAD_DIM, s), sv["o_a_t"], "delta_fox")
    dk_a, dv_a, dq_a_t, dsum, dqsum = _fox_bwd(sv["q_aug"], sv["k_aug"], sv["qkv"], do_a, sv["lse_a"],
                                               stats(delta_a), "fox_bwd")
    dq_a = (dq_a_t * QK_SCALE).transpose(1, 3, 0, 2).reshape(s, WIDTH).astype(BF16)
    d_cum = dqsum.reshape(N_HEADS, s) - jnp.sum(dsum.reshape(s, N_HEADS, 128), axis=-1).T
    df_t, db_f = _forget_bwd(d_cum.reshape(N_HEADS, s // 128, 128), sv["f_t"], sv["b_f"], "forget_bwd")
    g["b_f"] = db_f[:, 0, 0]
    df_t = df_t.reshape(N_HEADS, s)

    delta_c = _row_dot(do_c.T.reshape(N_HEADS, HEAD_DIM, s), sv["o_c_t"], "delta_chunk")
    dq_c_t, dk_c, dv_c, dbias_t = _chunk_bwd(sv["qkv"], do_c, sv["lse_c"], stats(delta_c), sv["bias_t"], "chunk_bwd")
    g["rel_bias"] = _rel_bias_tile_grad(dbias_t)
    dq_c = dq_c_t.transpose(2, 0, 1).reshape(s, WIDTH).astype(BF16)

    dqkv = jnp.concatenate([dq_a, dk_a, dv_a, dq_c, dk_c, dv_c], axis=1)
    df_pad = jnp.zeros((16 - N_HEADS, s), F32)
    df16 = jnp.concatenate([df_t, df_pad], axis=0).astype(BF16)
    df_cols = jnp.concatenate([df16.T, jnp.zeros((s, AUG - 16), BF16)], axis=1)

    dh1 = _mm(dqkv, wl["w_qkv_t"], out_dtype=F32, name="dh1_qkv", tk=1024)
    dh1g = _mm(dgates, wl["w_gates_t"], out_dtype=F32, name="dh1_gates", tk=1024)
    g_qkv = _mm(sv["h1"], dqkv, out_dtype=F32, name="g_w_qkv", ta=True, tk=1024)
    g_gates = _mm(sv["h1"], dgates, out_dtype=F32, name="g_w_gates", ta=True, tk=1024)
    g_f_t = _mm(df16, sv["h1"], out_dtype=F32, name="g_w_f", tk=1024)[:N_HEADS]
    g["w_in"] = jnp.concatenate([g_qkv[:, :3 * WIDTH], g_f_t.T, g_qkv[:, 3 * WIDTH:], g_gates], axis=1)
    dx, sums = _ln_mod_bwd(dh1, sv["x"], sc1, dres1, "ln_mod1_bwd", extra=(dh1g, df_cols, wl["w_f_pad"]))
    dsh1, dsc1 = sums[0], sums[1]
    g["mod"] = jnp.concatenate([dsh1, dsc1, dg1, dsh2, dsc2, dg2])
    return dx, g


def kernel(x, c, w_in, b_f, rel_bias, w_br_fox, w_br_chunk, w_out, w_up, conv_w, conv_b, w_down, w_ada, b_ada, ln1_g, ln1_b, ln2_g, ln2_b, loss_target, m_w_in, m_b_f, m_rel_bias, m_w_br_fox, m_w_br_chunk, m_w_out, m_w_up, m_conv_w, m_conv_b, m_w_down, m_w_ada, m_b_ada, m_ln1_g, m_ln1_b, m_ln2_g, m_ln2_b, v_w_in, v_b_f, v_rel_bias, v_w_br_fox, v_w_br_chunk, v_w_out, v_w_up, v_conv_w, v_conv_b, v_w_down, v_w_ada, v_b_ada, v_ln1_g, v_ln1_b, v_ln2_g, v_ln2_b):
    w = dict(w_in=w_in, b_f=b_f, rel_bias=rel_bias, w_br_fox=w_br_fox, w_br_chunk=w_br_chunk, w_out=w_out,
             w_up=w_up, conv_w=conv_w, conv_b=conv_b, w_down=w_down, w_ada=w_ada, b_ada=b_ada,
             ln1_g=ln1_g, ln1_b=ln1_b, ln2_g=ln2_g, ln2_b=ln2_b)
    m = dict(w_in=m_w_in, b_f=m_b_f, rel_bias=m_rel_bias, w_br_fox=m_w_br_fox, w_br_chunk=m_w_br_chunk,
             w_out=m_w_out, w_up=m_w_up, conv_w=m_conv_w, conv_b=m_conv_b, w_down=m_w_down, w_ada=m_w_ada,
             b_ada=m_b_ada, ln1_g=m_ln1_g, ln1_b=m_ln1_b, ln2_g=m_ln2_g, ln2_b=m_ln2_b)
    v = dict(w_in=v_w_in, b_f=v_b_f, rel_bias=v_rel_bias, w_br_fox=v_w_br_fox, w_br_chunk=v_w_br_chunk,
             w_out=v_w_out, w_up=v_w_up, conv_w=v_conv_w, conv_b=v_conv_b, w_down=v_w_down, w_ada=v_w_ada,
             b_ada=v_b_ada, ln1_g=v_ln1_g, ln1_b=v_ln1_b, ln2_g=v_ln2_g, ln2_b=v_ln2_b)
    depth, d, _ = w_in.shape
    s = x.shape[1]
    alpha = (2.0 * depth) ** 0.25
    me = 4 * lax.axis_index("x") + 2 * lax.axis_index("y") + lax.axis_index("c")
    x0 = x.reshape(s, d)
    target = loss_target.reshape(s, d)

    small_in, small_rows = _pack([c, conv_w], mult=8)
    small_all = _all_gather([small_in], "gather_c_conv")[0]
    c_all, conv_w_all = _unpack(small_all, small_rows, [c.shape, conv_w.shape], lead=(N_DEV,))
    c_all = c_all.reshape(N_DEV, d)
    conv_w_full = conv_w_all.transpose(1, 2, 0, 3).reshape(depth, conv_w.shape[1], -1)
    n_ada = w_ada.shape[2]
    b_ada_mine = lax.dynamic_slice_in_dim(b_ada, me * n_ada, n_ada, axis=1).reshape(depth, 1, n_ada)
    mod_part, cond_all = _mod_part(c_all, w_ada, b_ada_mine, "mod_part")
    mod_all = _all_gather([mod_part.reshape(depth * N_DEV, n_ada)], "gather_mod")[0]
    mod_all = mod_all.reshape(N_DEV, depth, N_DEV, n_ada)
    mod_mine = lax.dynamic_index_in_dim(mod_all, me, axis=2, keepdims=False)
    mod_mine = mod_mine.transpose(1, 0, 2).reshape(depth, N_MOD, 1, d)

    parts = _all_gather([_pad_lanes(w[n].astype(BF16)) for n in BIG], "gather_weights")
    full = {}
    for n, p in zip(BIG, parts):
        p = p[..., :w[n].shape[-1]]
        axis = 1 if n in ("w_out", "w_down") else 2
        p = jnp.moveaxis(p, 0, axis)
        shp = list(w[n].shape)
        shp[axis] *= N_DEV
        full[n] = p.reshape(shp)
    cols = np.cumsum([0, WIDTH, WIDTH, WIDTH, N_HEADS, WIDTH, WIDTH, WIDTH, d, d])
    layers = []
    for l in range(depth):
        wi = full["w_in"][l]
        w_qkv = jnp.concatenate([wi[:, :cols[3]], wi[:, cols[4]:cols[7]]], axis=1)
        w_gates = wi[:, cols[7]:]
        w_f_t = jnp.concatenate([wi[:, cols[3]:cols[4]].T, jnp.zeros((16 - N_HEADS, d), BF16)], axis=0)
        w_f_pad = jnp.concatenate([w_f_t, jnp.zeros((AUG - 16, d), BF16)], axis=0)
        row = lambda a: a[l].reshape(1, -1)
        layers.append(dict(
            w_qkv=w_qkv, w_gates=w_gates, w_f_t=w_f_t, w_f_pad=w_f_pad, w_qkv_t=w_qkv.T, w_gates_t=w_gates.T,
            w_br_fox=full["w_br_fox"][l], w_br_chunk=full["w_br_chunk"][l],
            w_br_fox_t=full["w_br_fox"][l].T, w_br_chunk_t=full["w_br_chunk"][l].T,
            w_out=full["w_out"][l], w_out_t=full["w_out"][l].T, w_up=full["w_up"][l], w_up_t=full["w_up"][l].T,
            w_down=full["w_down"][l], w_down_t=full["w_down"][l].T,
            conv_w=conv_w_full[l].reshape(3, 2, -1).transpose(1, 0, 2), conv_b=conv_b[l].reshape(2, 1, -1),
            b_f=b_f[l], rel_bias=rel_bias[l],
            ln1_g=row(ln1_g), ln1_b=row(ln1_b), ln2_g=row(ln2_g), ln2_b=row(ln2_b)))

    xs, saved = x0, []
    hs = _ln_mod(x0, mod_mine[0][1], mod_mine[0][0], "ln_mod1")
    for l in range(depth):
        nxt = mod_mine[min(l + 1, depth - 1)]
        xs, hs, sv = _layer_fwd(xs, hs, layers[l], list(mod_mine[l]), (nxt[1], nxt[0]), alpha)
        saved.append(sv)
    dxs, loss_part = _loss_grad(xs, target, "loss_grad")
    loss = lax.psum(loss_part[0, 0], AXES)
    grads = [None] * depth
    for l in reversed(range(depth)):
        dxs, grads[l] = _layer_bwd(dxs, saved[l], layers[l], list(mod_mine[l]), alpha)
    grad_x = dxs.reshape(1, s, d)
    stack = lambda n: jnp.stack([grads[l][n] for l in range(depth)])

    rep = ["b_ada"] + [n for n in SMALL if n != "b_ada"]
    small_g = [stack("mod")] + [stack(n) for n in rep[1:]] + [stack("conv_w")]
    small_pack, small_g_rows = _pack(small_g, mult=8)
    small_parts = _all_gather([small_pack], "gather_small_grads")[0]
    n_rep_rows = sum(small_g_rows[:-1])
    dmod_all = _unpack(small_parts, small_g_rows[:1], [(depth, N_MOD * d)], lead=(N_DEV,))[0]
    cw_all = _unpack(small_parts[:, n_rep_rows:], small_g_rows[-1:], [small_g[-1].shape], lead=(N_DEV,))[0]
    slabs = []
    for n in BIG:
        axis = 1 if n in ("w_out", "w_down") else 2
        slabs.append(_pad_lanes(_dest_major(stack(n), axis).astype(BF16)))
    big_parts = _all_to_all(slabs, "scatter_grads")

    out = {}
    for n, parts_n in zip(BIG, big_parts):
        rows2d = lambda a: a.reshape(a.shape[:-3] + (-1, a.shape[-1]))
        res = _adamw(rows2d(parts_n), rows2d(w[n]), rows2d(m[n]), rows2d(v[n]), "adamw_" + n, tr=256)
        for kind, r in zip(("grad", "delta", "new_m", "new_v"), res):
            out[kind, n] = r.reshape(w[n].shape)

    dmod_mine = lax.dynamic_slice_in_dim(dmod_all, me * n_ada, n_ada, axis=2)
    g_ada = jnp.stack([_mm(cond_all.T, dmod_mine[:, l], out_dtype=F32, name="g_w_ada", tk=N_DEV, tn=n_ada,
                           precision=HI) for l in range(depth)])
    lp = [_rows128(t["w_ada"], 8) for t in (w, m, v)]
    res = _adamw(_rows128(g_ada, 8)[None], *lp, "adamw_ada")
    for kind, r in zip(("grad", "delta", "new_m", "new_v"), res):
        out[kind, "w_ada"] = _unpack(r, [lp[0].shape[0]], [w_ada.shape])[0]

    rp = [_pack([t[n] for n in rep], mult=8)[0] for t in (w, m, v)]
    res = _adamw(small_parts[:, :n_rep_rows], *rp, "adamw_small")
    for kind, r in zip(("grad", "delta", "new_m", "new_v"), res):
        for n, a in zip(rep, _unpack(r, small_g_rows[:-1], [w[n].shape for n in rep])):
            out[kind, n] = a

    n_cw = conv_w.shape[2]
    cw_mine = lax.dynamic_slice_in_dim(cw_all, me * n_cw, n_cw, axis=3)
    cp = [_rows128(t["conv_w"], 8) for t in (w, m, v)]
    cw_in = jnp.stack([_rows128(cw_mine[i], 8) for i in range(N_DEV)])
    res = _adamw(cw_in, *cp, "adamw_conv_w")
    for kind, r in zip(("grad", "delta", "new_m", "new_v"), res):
        out[kind, "conv_w"] = _unpack(r, [cp[0].shape[0]], [conv_w.shape])[0]

    result = [loss, grad_x]
    for kind in ("grad", "delta", "new_m", "new_v"):
        result += [out[kind, n] for n in ORDER]
    return tuple(result)
```

```python
import functools
import math

import jax
import jax.numpy as jnp
import numpy as np
from jax import lax
from jax.experimental import pallas as pl
from jax.experimental.pallas import tpu as pltpu

F32 = jnp.float32
BF16 = jnp.bfloat16
HI = lax.Precision.HIGHEST
MESH_ID = pl.DeviceIdType.MESH
AXES = ("x", "y", "c")
N_DEV = 8

N_HEADS = 8
HEAD_DIM = 64
WIDTH = N_HEADS * HEAD_DIM
CHUNK = 64
LEFT_CHUNKS = 8
BAND = (LEFT_CHUNKS + 1) * CHUNK
REL_CLIP = 128
LN_EPS = 1e-5
N_MOD = 6
QK_SCALE = 1.0 / math.sqrt(HEAD_DIM)
NEG = -1e30
AUG = 128
N_SPLIT = 3
TQ = 512
SUB = 2 * CHUNK
SUBK = BAND + CHUNK

ADAM_LR, ADAM_B1, ADAM_B2, ADAM_EPS, ADAM_WD, ADAM_STEP = 0.001, 0.9, 0.999, 1e-08, 0.01, 10

VMEM_LIMIT = 56 * 2 ** 20


def _pcall(body, **kw):
    return pl.pallas_call(body, **kw)


def _cp(*sem):
    return pltpu.CompilerParams(dimension_semantics=sem if sem else None, vmem_limit_bytes=VMEM_LIMIT)


def _my_place():
    return lax.axis_index("x"), lax.axis_index("y"), lax.axis_index("c")


N_PEER = N_DEV - 1


def _comm_call(body, xs, out_shapes, name):
    n = len(xs)
    hbm = pl.BlockSpec(memory_space=pltpu.HBM)
    return _pcall(
        functools.partial(body, n), name=name, out_shape=out_shapes, in_specs=[hbm] * n, out_specs=[hbm] * n,
        scratch_shapes=[pltpu.SemaphoreType.DMA((N_PEER * n,)), pltpu.SemaphoreType.DMA((N_PEER * n,)),
                        pltpu.SemaphoreType.DMA((n,))],
    )(*xs)


def _all_gather(xs, name):
    def body(n, *refs):
        x_refs, out_refs, (send_sems, recv_sems, local_sems) = refs[:n], refs[n:2 * n], refs[2 * n:]
        mx, my, mc = _my_place()
        me, sibling = (mx, my, mc), (mx, my, 1 - mc)
        chips = [(1 - mx, my), (mx, 1 - my), (1 - mx, 1 - my)]

        def slot(a, px, py, pc):
            return out_refs[a].at[4 * px + 2 * py + pc]

        def copy(a, k, block, to, own=False):
            return pltpu.make_async_remote_copy(
                src_ref=x_refs[a] if own else slot(a, *block), dst_ref=slot(a, *block),
                send_sem=send_sems.at[a * N_PEER + k], recv_sem=recv_sems.at[a * N_PEER + k],
                device_id=to, device_id_type=MESH_ID)

        arrays = range(n)
        mine = [pltpu.make_async_copy(x_refs[a], slot(a, *me), local_sems.at[a]) for a in arrays]
        first = [copy(a, 0, me, sibling, own=True) for a in arrays]
        first += [copy(a, 1 + j, me, (*chip, mc), own=True) for a in arrays for j, chip in enumerate(chips)]
        for cp in mine + first:
            cp.start()
        passed = []
        for a in arrays:
            for j, chip in enumerate(chips):
                copy(a, 1 + j, (*chip, mc), me).wait_recv()
                passed.append(copy(a, 4 + j, (*chip, mc), sibling))
                passed[-1].start()
        for a in arrays:
            copy(a, 0, sibling, me).wait_recv()
            for j, chip in enumerate(chips):
                copy(a, 4 + j, (*chip, 1 - mc), me).wait_recv()
        for cp in first + passed:
            cp.wait_send()
        for cp in mine:
            cp.wait()

    return _comm_call(body, xs, [jax.ShapeDtypeStruct((N_DEV,) + x.shape, x.dtype) for x in xs], name)


def _direct_exchange(scatter, x_refs, out_refs, send_sems, recv_sems, local_sems):
    mx, my, mc = _my_place()
    me = 4 * mx + 2 * my + mc
    mine, copies = [], []
    for a, (x_ref, out_ref) in enumerate(zip(x_refs, out_refs)):
        mine.append(pltpu.make_async_copy(x_ref.at[me] if scatter else x_ref, out_ref.at[me], local_sems.at[a]))
        for k in range(N_PEER):
            px, py, pc = mx ^ ((k + 1) >> 2), my ^ (((k + 1) >> 1) & 1), mc ^ ((k + 1) & 1)
            copies.append(pltpu.make_async_remote_copy(
                src_ref=x_ref.at[4 * px + 2 * py + pc] if scatter else x_ref, dst_ref=out_ref.at[me],
                send_sem=send_sems.at[a * N_PEER + k], recv_sem=recv_sems.at[a * N_PEER + k],
                device_id=(px, py, pc), device_id_type=MESH_ID))

    def start():
        for cp in mine + copies:
            cp.start()

    def wait():
        for cp in copies:
            cp.wait_recv()
        for cp in copies:
            cp.wait_send()
        for cp in mine:
            cp.wait()

    return start, wait


def _exchange_shapes(scatter, xs):
    return [jax.ShapeDtypeStruct(x.shape if scatter else (N_DEV,) + x.shape, x.dtype) for x in xs]


def _exchange_scratch(n):
    return [pltpu.SemaphoreType.DMA((N_PEER * n,)), pltpu.SemaphoreType.DMA((N_PEER * n,)),
            pltpu.SemaphoreType.DMA((n,))]


def _all_to_all(xs, name):
    def body(n, *refs):
        start, wait = _direct_exchange(True, refs[:n], refs[n:2 * n], *refs[2 * n:])
        start()
        wait()

    return _comm_call(body, xs, _exchange_shapes(True, xs), name)


def _tile(n, pref, unit=128):
    if n <= pref:
        return n
    t = pref - pref % unit
    while t > unit and n % t:
        t -= unit
    assert n % t == 0, (n, pref, unit)
    return t


def _mm(a, b, *, out_dtype, name, ta=False, tb=False, tm=1024, tn=1024, tk=512, precision=None,
        a_halves=False, b_halves=False, o_halves=False):
    a_shape = (a.shape[1], 2 * a.shape[2]) if a_halves else a.shape
    b_shape = (b.shape[1], 2 * b.shape[2]) if b_halves else b.shape
    (k_dim, m) = a_shape if ta else a_shape[::-1]
    n = b_shape[0] if tb else b_shape[1]
    tm, tn = _tile(m, tm, 8), _tile(n // 2 if (b_halves or o_halves) else n, tn)
    tk = _tile(k_dim // 2 if a_halves else k_dim, tk)
    nk = k_dim // tk
    dims = (((0 if ta else 1,), (1 if tb else 0,)), ((), ()))

    def body(a_ref, b_ref, o_ref, acc_ref):
        k = pl.program_id(2)

        @pl.when(k == 0)
        def _():
            acc_ref[...] = jnp.zeros_like(acc_ref)

        acc_ref[...] += lax.dot_general(a_ref[...], b_ref[...], dims, preferred_element_type=F32,
                                        precision=precision)

        @pl.when(k == nk - 1)
        def _():
            o_ref[...] = acc_ref[...].astype(o_ref.dtype)

    def spec(shape2, pick, halves, per_half):
        if not halves:
            return pl.BlockSpec(shape2, pick)

        def index(i, j, k):
            r, c = pick(i, j, k)
            return (c // per_half, r, c % per_half)
        return pl.BlockSpec((None,) + shape2, index)

    assert not (ta and a_halves) and not (tb and b_halves)
    a_spec = spec((tk, tm), lambda i, j, k: (k, i), False, 0) if ta else \
        spec((tm, tk), lambda i, j, k: (i, k), a_halves, (k_dim // 2) // tk if a_halves else 0)
    b_spec = spec((tn, tk), lambda i, j, k: (j, k), False, 0) if tb else \
        spec((tk, tn), lambda i, j, k: (k, j), b_halves, (n // 2) // tn if b_halves else 0)
    o_spec = spec((tm, tn), lambda i, j, k: (i, j), o_halves, (n // 2) // tn if o_halves else 0)
    out_shape = (2, m, n // 2) if o_halves else (m, n)
    return _pcall(
        body, name=name,
        out_shape=jax.ShapeDtypeStruct(out_shape, out_dtype),
        grid=(m // tm, n // tn, nk),
        in_specs=[a_spec, b_spec],
        out_specs=o_spec,
        scratch_shapes=[pltpu.VMEM((tm, tn), F32)],
        compiler_params=_cp("parallel", "parallel", "arbitrary"),
    )(a, b)


def _ln_stats(x):
    mu = jnp.mean(x, axis=-1, keepdims=True)
    xc = x - mu
    var = jnp.mean(xc * xc, axis=-1, keepdims=True)
    rstd = lax.rsqrt(var + LN_EPS)
    return xc * rstd, rstd


def _ln_mod(x, sc, sh, name, tm=512):
    s, d = x.shape
    tm = min(tm, s)

    def body(x_ref, sc_ref, sh_ref, h_ref):
        xhat, _ = _ln_stats(x_ref[...])
        h_ref[...] = (xhat * (1.0 + sc_ref[...]) + sh_ref[...]).astype(h_ref.dtype)

    row = pl.BlockSpec((1, d), lambda i: (0, 0))
    return _pcall(
        body, name=name, out_shape=jax.ShapeDtypeStruct((s, d), BF16), grid=(s // tm,),
        in_specs=[pl.BlockSpec((tm, d), lambda i: (i, 0)), row, row],
        out_specs=pl.BlockSpec((tm, d), lambda i: (i, 0)),
        compiler_params=_cp("parallel"),
    )(x, sc, sh)


def _mm_res_ln(a, w, xres, gate, ln_g, ln_b, sc, sh, alpha, name, tm=512, tk=512):
    s, k_dim = a.shape
    d = w.shape[1]
    tm, tk = _tile(s, tm, 8), _tile(k_dim, tk)
    nk = k_dim // tk

    def body(a_ref, w_ref, x_ref, g_ref, lg_ref, lb_ref, sc_ref, sh_ref, y_ref, xn_ref, h_ref, acc_ref):
        k = pl.program_id(1)

        @pl.when(k == 0)
        def _():
            acc_ref[...] = jnp.zeros_like(acc_ref)

        acc_ref[...] += jnp.dot(a_ref[...], w_ref[...], preferred_element_type=F32)

        @pl.when(k == nk - 1)
        def _():
            y = acc_ref[...]
            y_ref[...] = y
            z = alpha * x_ref[...] + (1.0 + g_ref[...]) * y
            zhat, _ = _ln_stats(z)
            xn = zhat * lg_ref[...] + lb_ref[...]
            xn_ref[...] = xn
            xhat, _ = _ln_stats(xn)
            h_ref[...] = (xhat * (1.0 + sc_ref[...]) + sh_ref[...]).astype(h_ref.dtype)

    row = pl.BlockSpec((1, d), lambda i, k: (0, 0))
    tile = pl.BlockSpec((tm, d), lambda i, k: (i, 0))
    return _pcall(
        body, name=name,
        out_shape=(jax.ShapeDtypeStruct((s, d), F32), jax.ShapeDtypeStruct((s, d), F32),
                   jax.ShapeDtypeStruct((s, d), BF16)),
        grid=(s // tm, nk),
        in_specs=[pl.BlockSpec((tm, tk), lambda i, k: (i, k)), pl.BlockSpec((tk, d), lambda i, k: (k, 0)),
                  tile, row, row, row, row, row],
        out_specs=(tile, tile, tile),
        scratch_shapes=[pltpu.VMEM((tm, d), F32)],
        compiler_params=_cp("parallel", "arbitrary"),
    )(a, w, xres, gate, ln_g, ln_b, sc, sh)


def _colsum(v):
    return jnp.sum(v, axis=0, keepdims=True)


def _ln_res_bwd(dxn, xres, y, gate, ln_g, alpha, name, tm=256):
    s, d = dxn.shape
    tm = min(tm, s)

    def body(dxn_ref, x_ref, y_ref, g_ref, lg_ref, dres_ref, dy_ref, sums_ref):
        @pl.when(pl.program_id(0) == 0)
        def _():
            sums_ref[...] = jnp.zeros_like(sums_ref)

        dxn_v, y_v = dxn_ref[...], y_ref[...]
        one_g = 1.0 + g_ref[...]
        zhat, rstd = _ln_stats(alpha * x_ref[...] + one_g * y_v)
        dzh = dxn_v * lg_ref[...]
        dz = rstd * (dzh - jnp.mean(dzh, axis=-1, keepdims=True)
                     - zhat * jnp.mean(dzh * zhat, axis=-1, keepdims=True))
        dres_ref[...] = alpha * dz
        dy_ref[...] = (one_g * dz).astype(dy_ref.dtype)
        sums_ref[0:1, :] += _colsum(dxn_v * zhat)
        sums_ref[1:2, :] += _colsum(dxn_v)
        sums_ref[2:3, :] += _colsum(dz * y_v)

    row = pl.BlockSpec((1, d), lambda i: (0, 0))
    tile = pl.BlockSpec((tm, d), lambda i: (i, 0))
    return _pcall(
        body, name=name,
        out_shape=(jax.ShapeDtypeStruct((s, d), F32), jax.ShapeDtypeStruct((s, d), BF16),
                   jax.ShapeDtypeStruct((8, d), F32)),
        grid=(s // tm,),
        in_specs=[tile, tile, tile, row, row],
        out_specs=(tile, tile, pl.BlockSpec((8, d), lambda i: (0, 0))),
        compiler_params=_cp("arbitrary"),
    )(dxn, xres, y, gate, ln_g)


def _ln_mod_bwd(dh, x, sc, dres, name, extra=None, tm=256):
    s, d = dh.shape
    tm = min(tm, s)

    def body(*refs):
        if extra is None:
            dh_ref, x_ref, sc_ref, dres_ref, dx_ref, sums_ref = refs
            dh_v = dh_ref[...]
        else:
            dh_ref, x_ref, sc_ref, dres_ref, dh2_ref, df_ref, wf_ref, dx_ref, sums_ref = refs
            dh_v = dh_ref[...] + dh2_ref[...] + jnp.dot(df_ref[...], wf_ref[...], preferred_element_type=F32)

        @pl.when(pl.program_id(0) == 0)
        def _():
            sums_ref[...] = jnp.zeros_like(sums_ref)

        xhat, rstd = _ln_stats(x_ref[...])
        dxh = dh_v * (1.0 + sc_ref[...])
        dx_ref[...] = dres_ref[...] + rstd * (dxh - jnp.mean(dxh, axis=-1, keepdims=True)
                                             - xhat * jnp.mean(dxh * xhat, axis=-1, keepdims=True))
        sums_ref[0:1, :] += _colsum(dh_v)
        sums_ref[1:2, :] += _colsum(dh_v * xhat)

    row = pl.BlockSpec((1, d), lambda i: (0, 0))
    tile = pl.BlockSpec((tm, d), lambda i: (i, 0))
    in_specs = [tile, tile, row, tile]
    args = [dh, x, sc, dres]
    if extra is not None:
        in_specs += [tile, pl.BlockSpec((tm, AUG), lambda i: (i, 0)), pl.BlockSpec((AUG, d), lambda i: (0, 0))]
        args += list(extra)
    return _pcall(
        body, name=name,
        out_shape=(jax.ShapeDtypeStruct((s, d), F32), jax.ShapeDtypeStruct((8, d), F32)),
        grid=(s // tm,), in_specs=in_specs,
        out_specs=(tile, pl.BlockSpec((8, d), lambda i: (0, 0))),
        compiler_params=_cp("arbitrary"),
    )(*args)


def _loss_grad(xn, target, name, tm=512):
    s, d = xn.shape
    tm = min(tm, s)
    n = s // tm

    def body(x_ref, t_ref, dx_ref, loss_ref, acc_ref):
        i = pl.program_id(0)

        @pl.when(i == 0)
        def _():
            acc_ref[...] = jnp.zeros_like(acc_ref)

        err = x_ref[...] - t_ref[...]
        dx_ref[...] = err * (1.0 / d)
        acc_ref[...] += _colsum(err * err)

        @pl.when(i == n - 1)
        def _():
            loss_ref[...] = jnp.zeros_like(loss_ref) + (0.5 / d) * jnp.sum(acc_ref[...])

    tile = pl.BlockSpec((tm, d), lambda i: (i, 0))
    return _pcall(
        body, name=name,
        out_shape=(jax.ShapeDtypeStruct((s, d), F32), jax.ShapeDtypeStruct((8, 128), F32)),
        grid=(n,), in_specs=[tile, tile],
        out_specs=(tile, pl.BlockSpec((8, 128), lambda i: (0, 0))),
        scratch_shapes=[pltpu.VMEM((1, d), F32)],
        compiler_params=_cp("arbitrary"),
    )(xn, target)


def _sigmoid(v):
    return 0.5 * jnp.tanh(0.5 * v) + 0.5


def _merge_fwd(o_a, o_c, w_a, w_c, gates, name, tm=256):
    s, wd = o_a.shape
    d = w_a.shape[1]
    tm = min(tm, s)

    def body(oa_ref, oc_ref, wa_ref, wc_ref, ga_ref, gc_ref, m_ref):
        ba = jnp.dot(oa_ref[...], wa_ref[...], preferred_element_type=F32)
        bc = jnp.dot(oc_ref[...], wc_ref[...], preferred_element_type=F32)
        m_ref[...] = (_sigmoid(ga_ref[...]) * ba + _sigmoid(gc_ref[...]) * bc).astype(m_ref.dtype)

    o_spec = pl.BlockSpec((tm, wd), lambda i: (i, 0))
    w_spec = pl.BlockSpec((wd, d), lambda i: (0, 0))
    return _pcall(
        body, name=name, out_shape=jax.ShapeDtypeStruct((s, d), BF16), grid=(s // tm,),
        in_specs=[o_spec, o_spec, w_spec, w_spec,
                  pl.BlockSpec((tm, d), lambda i: (i, 0)), pl.BlockSpec((tm, d), lambda i: (i, 1))],
        out_specs=pl.BlockSpec((tm, d), lambda i: (i, 0)),
        compiler_params=_cp("parallel"),
    )(o_a, o_c, w_a, w_c, gates, gates)


def _merge_bwd(dmix, w_out_t, o_a, o_c, w_a, w_c, w_a_t, w_c_t, gates, name, tm=256):
    s, wd = o_a.shape
    d = w_a.shape[1]
    tm = min(tm, s)

    def body(dmix_ref, wot_ref, oa_ref, oc_ref, wa_ref, wc_ref, wat_ref, wct_ref, ga_ref, gc_ref,
             dba_ref, dbc_ref, doa_ref, doc_ref, dg_ref):
        dm = jnp.dot(dmix_ref[...], wot_ref[...], preferred_element_type=F32)
        for half, (o_ref, w_ref, wt_ref, g_ref, db_ref, do_ref) in enumerate((
                (oa_ref, wa_ref, wat_ref, ga_ref, dba_ref, doa_ref),
                (oc_ref, wc_ref, wct_ref, gc_ref, dbc_ref, doc_ref))):
            b = jnp.dot(o_ref[...], w_ref[...], preferred_element_type=F32)
            sg = _sigmoid(g_ref[...])
            db = (dm * sg).astype(BF16)
            db_ref[...] = db
            dg_ref[:, d * half:d * (half + 1)] = (dm * b * sg * (1.0 - sg)).astype(dg_ref.dtype)
            do_ref[...] = jnp.dot(db, wt_ref[...], preferred_element_type=F32).astype(do_ref.dtype)

    row_d = pl.BlockSpec((tm, d), lambda i: (i, 0))
    row_w = pl.BlockSpec((tm, wd), lambda i: (i, 0))
    full = lambda shp: pl.BlockSpec(shp, lambda i: (0, 0))
    return _pcall(
        body, name=name,
        out_shape=(jax.ShapeDtypeStruct((s, d), BF16), jax.ShapeDtypeStruct((s, d), BF16),
                   jax.ShapeDtypeStruct((s, wd), BF16), jax.ShapeDtypeStruct((s, wd), BF16),
                   jax.ShapeDtypeStruct((s, 2 * d), BF16)),
        grid=(s // tm,),
        in_specs=[row_d, full((d, d)), row_w, row_w, full((wd, d)), full((wd, d)), full((d, wd)), full((d, wd)),
                  row_d, pl.BlockSpec((tm, d), lambda i: (i, 1))],
        out_specs=(row_d, row_d, row_w, row_w, pl.BlockSpec((tm, 2 * d), lambda i: (i, 0))),
        compiler_params=_cp("parallel"),
    )(dmix, w_out_t, o_a, o_c, w_a, w_c, w_a_t, w_c_t, gates, gates)


def _shift_down(cur, k, fill_rows):
    out = pltpu.roll(cur, k, axis=0)
    rid = lax.broadcasted_iota(jnp.int32, cur.shape, 0)
    for r, fill in enumerate(fill_rows):
        out = jnp.where(rid == r, fill, out)
    return out


def _shift_up(cur, k, fill_rows):
    n = cur.shape[0]
    out = pltpu.roll(cur, n - k, axis=0)
    rid = lax.broadcasted_iota(jnp.int32, cur.shape, 0)
    for r, fill in enumerate(fill_rows):
        out = jnp.where(rid == n - k + r, fill, out)
    return out


def _conv_rows(cur, prev8, first, w, b):
    p6 = jnp.where(first, 0.0, prev8[6:7, :])
    p7 = jnp.where(first, 0.0, prev8[7:8, :])
    m1 = _shift_down(cur, 1, [p7])
    m2 = _shift_down(cur, 2, [p6, p7])
    u = b + w[0:1, :] * m2 + w[1:2, :] * m1 + w[2:3, :] * cur
    return u, m1, m2


def _conv_specs(tr, tc, order):
    r8 = tr // 8
    at = lambda f: (lambda *g: f(*order(*g)))
    return [pl.BlockSpec((2, tr, tc), at(lambda i, j: (0, i, j))),
            pl.BlockSpec((2, 8, tc), at(lambda i, j: (0, jnp.maximum(i * r8 - 1, 0), j))),
            pl.BlockSpec((2, 3, tc), at(lambda i, j: (0, 0, j))),
            pl.BlockSpec((2, 1, tc), at(lambda i, j: (0, 0, j)))]


def _conv_act_fwd(u0, conv_w, conv_b, name, tr=512, tc=256):
    _, s, f = u0.shape
    tr, tc = _tile(s, tr, 8), _tile(f, tc)

    def body(u_ref, p_ref, w_ref, b_ref, act_ref):
        first = pl.program_id(0) == 0
        a, _, _ = _conv_rows(u_ref[0], p_ref[0], first, w_ref[0], b_ref[0])
        v, _, _ = _conv_rows(u_ref[1], p_ref[1], first, w_ref[1], b_ref[1])
        act_ref[...] = (a * _sigmoid(a) * v).astype(act_ref.dtype)

    return _pcall(
        body, name=name, out_shape=jax.ShapeDtypeStruct((s, f), BF16), grid=(s // tr, f // tc),
        in_specs=_conv_specs(tr, tc, lambda i, j: (i, j)),
        out_specs=pl.BlockSpec((tr, tc), lambda i, j: (i, j)),
        compiler_params=_cp("parallel", "parallel"),
    )(u0, u0, conv_w, conv_b)


def _act_grads(dact, a, v):
    sg = _sigmoid(a)
    return dact * v * sg * (1.0 + a * (1.0 - sg)), dact * a * sg


def _conv_act_bwd(dact, u0, conv_w, conv_b, name, tr=512, tc=256):
    _, s, f = u0.shape
    tr, tc = _tile(s, tr, 8), _tile(f, tc)
    n = s // tr
    r8 = tr // 8

    def body(dact_ref, dnext_ref, u_ref, p_ref, unext_ref, w_ref, b_ref, du0_ref, sums_ref):
        i = pl.program_id(1)
        first, last = i == 0, i == n - 1

        @pl.when(first)
        def _():
            sums_ref[...] = jnp.zeros_like(sums_ref)

        cur = (u_ref[0], u_ref[1])
        (a, a1, a2), (v, v1, v2) = [_conv_rows(cur[hf], p_ref[hf], first, w_ref[hf], b_ref[hf]) for hf in range(2)]
        da, dv = _act_grads(dact_ref[...], a, v)
        (an, _, _), (vn, _, _) = [_conv_rows(unext_ref[hf], cur[hf][tr - 8:tr, :], False, w_ref[hf], b_ref[hf])
                                  for hf in range(2)]
        dan, dvn = _act_grads(dnext_ref[...], an, vn)
        for hf, g, gn, shifted in ((0, da, dan, (a2, a1, cur[0])), (1, dv, dvn, (v2, v1, cur[1]))):
            w = w_ref[hf]
            n0 = jnp.where(last, 0.0, gn[0:1, :])
            n1 = jnp.where(last, 0.0, gn[1:2, :])
            du0 = w[2:3, :] * g + w[1:2, :] * _shift_up(g, 1, [n0]) + w[0:1, :] * _shift_up(g, 2, [n0, n1])
            du0_ref[hf] = du0.astype(du0_ref.dtype)
            for r in range(3):
                sums_ref[hf, r:r + 1, :] += _colsum(g * shifted[r])
            sums_ref[hf, 3:4, :] += _colsum(g)

    nxt = lambda i: jnp.minimum((i + 1) * r8, s // 8 - 1)
    order = lambda j, i: (i, j)
    specs = _conv_specs(tr, tc, order)
    return _pcall(
        body, name=name,
        out_shape=(jax.ShapeDtypeStruct((2, s, f), BF16), jax.ShapeDtypeStruct((2, 8, f), F32)),
        grid=(f // tc, n),
        in_specs=[pl.BlockSpec((tr, tc), lambda j, i: (i, j)), pl.BlockSpec((8, tc), lambda j, i: (nxt(i), j)),
                  specs[0], specs[1], pl.BlockSpec((2, 8, tc), lambda j, i: (0, nxt(i), j)), specs[2], specs[3]],
        out_specs=(pl.BlockSpec((2, tr, tc), lambda j, i: (0, i, j)), pl.BlockSpec((2, 8, tc), lambda j, i: (0, 0, j))),
        compiler_params=_cp("parallel", "arbitrary"),
    )(dact, dact, u0, u0, u0, conv_w, conv_b)


def _cumsum_rows(v, reverse=False):
    r = v.shape[0]
    i128 = lax.broadcasted_iota(jnp.int32, (128, 128), 0), lax.broadcasted_iota(jnp.int32, (128, 128), 1)
    ir = lax.broadcasted_iota(jnp.int32, (r, r), 0), lax.broadcasted_iota(jnp.int32, (r, r), 1)
    in_row = (i128[0] >= i128[1] if reverse else i128[0] <= i128[1]).astype(F32)
    rows = (ir[1] > ir[0] if reverse else ir[1] < ir[0]).astype(F32)
    within = jnp.dot(v, in_row, preferred_element_type=F32, precision=HI)
    tot = jnp.broadcast_to(within[:, 0:1] if reverse else within[:, 127:128], (r, 128))
    return within + jnp.dot(rows, tot, preferred_element_type=F32, precision=HI)


def _forget_fwd(f_t, b_f, name):
    h, r, _ = f_t.shape

    def body(f_ref, b_ref, o_ref):
        z = f_ref[...] + b_ref[...]
        logf = jnp.minimum(z, 0.0) - jnp.log(1.0 + jnp.exp(-jnp.abs(z)))
        rest = _cumsum_rows(logf)
        for i in range(N_SPLIT):
            piece = rest.astype(BF16).astype(F32)
            o_ref[i] = piece
            rest = rest - piece

    return _pcall(
        body, name=name, out_shape=jax.ShapeDtypeStruct((N_SPLIT, h, r, 128), F32), grid=(h,),
        in_specs=[pl.BlockSpec((None, r, 128), lambda i: (i, 0, 0)), pl.BlockSpec((None, 1, 128), lambda i: (i, 0, 0))],
        out_specs=pl.BlockSpec((N_SPLIT, None, r, 128), lambda i: (0, i, 0, 0)),
        compiler_params=_cp("parallel"),
    )(f_t, b_f)


def _forget_bwd(d_cum, f_t, b_f, name):
    h, r, _ = f_t.shape

    def body(g_ref, f_ref, b_ref, df_ref, db_ref):
        df = _cumsum_rows(g_ref[...], reverse=True) * _sigmoid(-(f_ref[...] + b_ref[...]))
        df_ref[...] = df
        db_ref[...] = jnp.zeros_like(db_ref) + jnp.sum(df)

    blk = pl.BlockSpec((None, r, 128), lambda i: (i, 0, 0))
    one = pl.BlockSpec((None, 1, 128), lambda i: (i, 0, 0))
    return _pcall(
        body, name=name,
        out_shape=(jax.ShapeDtypeStruct((h, r, 128), F32), jax.ShapeDtypeStruct((h, 1, 128), F32)),
        grid=(h,), in_specs=[blk, blk, one], out_specs=(blk, one),
        compiler_params=_cp("parallel"),
    )(d_cum, f_t, b_f)


_NT = (((1,), (1,)), ((), ()))


def _causal_keep(tk, tq):
    return lax.broadcasted_iota(jnp.int32, (tk, tq), 0) <= lax.broadcasted_iota(jnp.int32, (tk, tq), 1)


_TN = (((0,), (0,)), ((), ()))
PAIR = 2
V_FOX = 2 * WIDTH // 128
Q_CHUNK, K_CHUNK, V_CHUNK = (3 * WIDTH) // 128, (4 * WIDTH) // 128, (5 * WIDTH) // 128


def _riding_exchange(exchange, n_in, n_out, grid):
    if exchange is None:
        return [], [], [], [], lambda refs: (refs, lambda: None)
    scatter, xs = exchange
    n = len(xs)
    hbm = pl.BlockSpec(memory_space=pltpu.HBM)

    def hook(refs):
        ins, x_refs = refs[:n_in], refs[n_in:n_in + n]
        outs, land_refs = refs[n_in + n:n_in + n + n_out], refs[n_in + n + n_out:n_in + 2 * n + n_out]
        rest = refs[n_in + 2 * n + n_out:]
        own_scratch, sems = rest[:len(rest) - 3], rest[len(rest) - 3:]
        start, wait = _direct_exchange(scatter, x_refs, land_refs, *sems)
        ids = [pl.program_id(a) for a in range(len(grid))]
        is_first = functools.reduce(jnp.logical_and, [i == 0 for i in ids])
        is_last = functools.reduce(jnp.logical_and, [i == g - 1 for i, g in zip(ids, grid)])
        pl.when(is_first)(start)
        return tuple(ins) + tuple(outs) + tuple(own_scratch), lambda: pl.when(is_last)(wait)

    return list(xs), _exchange_shapes(scatter, xs), [hbm] * n, _exchange_scratch(n), hook


def _fox_fwd(q_aug, k_aug, qkv, name, t=TQ, exchange=None):
    s = q_aug.shape[0]
    t = min(t, s)
    n = s // t
    dh = HEAD_DIM
    grid = (N_HEADS // PAIR, n)
    ex_in, ex_out, ex_specs, ex_scratch, hook = _riding_exchange(exchange, 3, 2, grid)

    def body(*refs):
        (q_ref, k_ref, v_ref, o_ref, lse_ref, m_s, l_s, acc_s), finish = hook(refs)
        qi = pl.program_id(1)
        m_s[...] = jnp.full_like(m_s, NEG)
        l_s[...] = jnp.zeros_like(l_s)
        acc_s[...] = jnp.zeros_like(acc_s)

        def step(kj, diag):
            rows = pl.ds(pl.multiple_of(kj * t, t), t)
            for hh in range(PAIR):
                st = lax.dot_general(k_ref[rows, AUG * hh:AUG * (hh + 1)], q_ref[:, AUG * hh:AUG * (hh + 1)], _NT,
                                     preferred_element_type=F32)
                if diag:
                    st = jnp.where(_causal_keep(t, t), st, NEG)
                m_prev = m_s[hh]
                m_new = jnp.maximum(m_prev, jnp.max(st, axis=0, keepdims=True))
                a = jnp.exp(m_prev - m_new)
                pt = jnp.exp(st - m_new)
                l_s[hh] = a * l_s[hh] + jnp.sum(pt, axis=0, keepdims=True)
                acc_s[hh] = a * acc_s[hh] + lax.dot_general(v_ref[rows, dh * hh:dh * (hh + 1)], pt.astype(BF16), _TN,
                                                            preferred_element_type=F32)
                m_s[hh] = m_new

        def off_diagonal(kj, carry):
            step(kj, False)
            return carry

        lax.fori_loop(0, qi, off_diagonal, 0)
        step(qi, True)
        for hh in range(PAIR):
            o_ref[hh] = acc_s[hh] / l_s[hh]
            lse_ref[hh, 0] = m_s[hh] + jnp.log(l_s[hh])
        finish()

    return _pcall(
        body, name=name,
        out_shape=[jax.ShapeDtypeStruct((N_HEADS, dh, s), F32), jax.ShapeDtypeStruct((N_HEADS, n, 1, t), F32)] + ex_out,
        grid=grid,
        in_specs=[pl.BlockSpec((t, PAIR * AUG), lambda hp, i: (i, hp)),
                  pl.BlockSpec((s, PAIR * AUG), lambda hp, i: (0, hp)),
                  pl.BlockSpec((s, PAIR * dh), lambda hp, i: (0, V_FOX + hp))] + ex_specs,
        out_specs=[pl.BlockSpec((PAIR, dh, t), lambda hp, i: (hp, 0, i)),
                   pl.BlockSpec((PAIR, 1, 1, t), lambda hp, i: (hp, i, 0, 0))] + ex_specs,
        scratch_shapes=[pltpu.VMEM((PAIR, 1, t), F32), pltpu.VMEM((PAIR, 1, t), F32),
                        pltpu.VMEM((PAIR, dh, t), F32)] + ex_scratch,
        compiler_params=_cp("arbitrary", "arbitrary"),
    )(q_aug, k_aug, qkv, *ex_in)


def _row_dot(a_t, b_t, name, t=2048):
    h, dh, s = a_t.shape
    t = min(t, s)

    def body(a_ref, b_ref, o_ref):
        o_ref[...] = jnp.sum(a_ref[...].astype(F32) * b_ref[...], axis=0, keepdims=True)

    blk = pl.BlockSpec((None, dh, t), lambda hh, i: (hh, 0, i))
    return _pcall(
        body, name=name, out_shape=jax.ShapeDtypeStruct((h, 1, s), F32), grid=(h, s // t),
        in_specs=[blk, blk], out_specs=pl.BlockSpec((None, 1, t), lambda hh, i: (hh, 0, i)),
        compiler_params=_cp("parallel", "parallel"),
    )(a_t, b_t)


def _fox_bwd(q_aug, k_aug, qkv, do, lse, delta, name, t=TQ, exchange=None):
    s = q_aug.shape[0]
    dh = HEAD_DIM
    t = min(t, s)
    n = s // t
    grid = (N_HEADS // PAIR, n)
    ex_in, ex_out, ex_specs, ex_scratch, hook = _riding_exchange(exchange, 6, 5, grid)

    def body(*refs):
        (q_ref, k_ref, v_ref, do_ref, lse_ref, dl_ref, dk_ref, dv_ref, dq_ref, dsum_ref, dqsum_ref,
         dk_s, dv_s, dsum_s), finish = hook(refs)
        kj = pl.program_id(1)

        @pl.when(kj == 0)
        def _():
            dq_ref[...] = jnp.zeros_like(dq_ref)
            dqsum_ref[...] = jnp.zeros_like(dqsum_ref)

        dk_s[...] = jnp.zeros_like(dk_s)
        dv_s[...] = jnp.zeros_like(dv_s)
        dsum_s[...] = jnp.zeros_like(dsum_s)

        def step(qi, diag):
            rows = pl.ds(pl.multiple_of(qi * t, t), t)
            for hh in range(PAIR):
                q = q_ref[rows, AUG * hh:AUG * (hh + 1)]
                k = k_ref[:, AUG * hh:AUG * (hh + 1)]
                st = lax.dot_general(k, q, _NT, preferred_element_type=F32)
                if diag:
                    st = jnp.where(_causal_keep(t, t), st, NEG)
                pt = jnp.exp(st - lse_ref[hh, qi])
                do_v = do_ref[rows, dh * hh:dh * (hh + 1)]
                dpt = lax.dot_general(v_ref[:, dh * hh:dh * (hh + 1)], do_v, _NT, preferred_element_type=F32)
                ds32 = pt * (dpt - dl_ref[hh, qi])
                dsum_s[hh] += sum(ds32[:, 128 * u:128 * (u + 1)] for u in range(t // 128))
                dqsum_ref[hh, qi] += jnp.sum(ds32, axis=0, keepdims=True)
                dst = ds32.astype(BF16)
                dv_s[hh] += jnp.dot(pt.astype(BF16), do_v, preferred_element_type=F32)
                dk_s[hh] += jnp.dot(dst, q[:, :dh], preferred_element_type=F32)
                dq_ref[hh, qi] += lax.dot_general(k[:, :dh], dst, _TN, preferred_element_type=F32)

        def off_diagonal(qi, carry):
            step(qi, False)
            return carry

        step(kj, True)
        lax.fori_loop(kj + 1, n, off_diagonal, 0)
        for hh in range(PAIR):
            dk_ref[:, dh * hh:dh * (hh + 1)] = dk_s[hh].astype(dk_ref.dtype)
            dv_ref[:, dh * hh:dh * (hh + 1)] = dv_s[hh].astype(dv_ref.dtype)
            dsum_ref[:, 128 * hh:128 * (hh + 1)] = dsum_s[hh]
        finish()

    stat = pl.BlockSpec((PAIR, n, 1, t), lambda hp, j: (hp, 0, 0, 0))
    return _pcall(
        body, name=name,
        out_shape=[jax.ShapeDtypeStruct((s, WIDTH), BF16), jax.ShapeDtypeStruct((s, WIDTH), BF16),
                   jax.ShapeDtypeStruct((N_HEADS, n, dh, t), F32), jax.ShapeDtypeStruct((s, N_HEADS * 128), F32),
                   jax.ShapeDtypeStruct((N_HEADS, n, 1, t), F32)] + ex_out,
        grid=grid,
        in_specs=[pl.BlockSpec((s, PAIR * AUG), lambda hp, j: (0, hp)),
                  pl.BlockSpec((t, PAIR * AUG), lambda hp, j: (j, hp)),
                  pl.BlockSpec((t, PAIR * dh), lambda hp, j: (j, V_FOX + hp)),
                  pl.BlockSpec((s, PAIR * dh), lambda hp, j: (0, hp)),
                  stat, stat] + ex_specs,
        out_specs=[pl.BlockSpec((t, PAIR * dh), lambda hp, j: (j, hp)),
                   pl.BlockSpec((t, PAIR * dh), lambda hp, j: (j, hp)),
                   pl.BlockSpec((PAIR, n, dh, t), lambda hp, j: (hp, 0, 0, 0)),
                   pl.BlockSpec((t, PAIR * 128), lambda hp, j: (j, hp)),
                   stat] + ex_specs,
        scratch_shapes=[pltpu.VMEM((PAIR, t, dh), F32), pltpu.VMEM((PAIR, t, dh), F32),
                        pltpu.VMEM((PAIR, t, 128), F32)] + ex_scratch,
        compiler_params=_cp("arbitrary", "arbitrary"),
    )(q_aug, k_aug, qkv, do, lse, delta, *ex_in)


def _chunk_logits(kw, qs, bias, first_tile, j):
    st = lax.dot_general(kw[SUB * j:SUB * j + SUBK], qs, _NT, preferred_element_type=F32) + bias
    row = lax.broadcasted_iota(jnp.int32, st.shape, 0)
    return jnp.where(jnp.logical_or(jnp.logical_not(first_tile), row >= TQ - SUB * j), st, NEG)


def _window(prev_ref, cur_ref, hh):
    cols = slice(HEAD_DIM * hh, HEAD_DIM * (hh + 1))
    return jnp.concatenate([prev_ref[:, cols], cur_ref[:, cols]], axis=0)


def _chunk_fwd(qkv, bias_t, name):
    s = qkv.shape[0]
    dh = HEAD_DIM
    n = s // TQ
    nsub = TQ // SUB
    prev = lambda i: jnp.maximum(i - 1, 0)

    def body(q_ref, kp_ref, kc_ref, vp_ref, vc_ref, b_ref, o_ref, lse_ref):
        first_tile = pl.program_id(1) == 0
        for hh in range(PAIR):
            kw, vw = _window(kp_ref, kc_ref, hh), _window(vp_ref, vc_ref, hh)
            bias = b_ref[hh]
            for j in range(nsub):
                qs = q_ref[SUB * j:SUB * (j + 1), dh * hh:dh * (hh + 1)] * QK_SCALE
                st = _chunk_logits(kw, qs, bias, first_tile, j)
                m = jnp.max(st, axis=0, keepdims=True)
                pt = jnp.exp(st - m)
                l = jnp.sum(pt, axis=0, keepdims=True)
                ot = lax.dot_general(vw[SUB * j:SUB * j + SUBK], pt.astype(BF16), _TN, preferred_element_type=F32)
                o_ref[hh, :, SUB * j:SUB * (j + 1)] = ot / l
                lse_ref[hh, 0, :, SUB * j:SUB * (j + 1)] = m + jnp.log(l)

    blk = lambda col, m: pl.BlockSpec((TQ, PAIR * dh), lambda hp, i: (m(i), col + hp))
    same = lambda i: i
    return _pcall(
        body, name=name,
        out_shape=(jax.ShapeDtypeStruct((N_HEADS, dh, s), F32), jax.ShapeDtypeStruct((N_HEADS, n, 1, TQ), F32)),
        grid=(N_HEADS // PAIR, n),
        in_specs=[blk(Q_CHUNK, same), blk(K_CHUNK, prev), blk(K_CHUNK, same), blk(V_CHUNK, prev), blk(V_CHUNK, same),
                  pl.BlockSpec((PAIR, SUBK, SUB), lambda hp, i: (hp, 0, 0))],
        out_specs=(pl.BlockSpec((PAIR, dh, TQ), lambda hp, i: (hp, 0, i)),
                   pl.BlockSpec((PAIR, 1, 1, TQ), lambda hp, i: (hp, i, 0, 0))),
        compiler_params=_cp("parallel", "parallel"),
    )(qkv, qkv, qkv, qkv, qkv, bias_t)


def _chunk_bwd(qkv, do, lse, delta, bias_t, name):
    s = qkv.shape[0]
    dh = HEAD_DIM
    n = s // TQ
    nsub = TQ // SUB
    cur = lambda i: jnp.minimum(i, n - 1)
    prev = lambda i: jnp.maximum(cur(i) - 1, 0)
    done = lambda i: jnp.maximum(i - 1, 0)

    def body(q_ref, kp_ref, kc_ref, vp_ref, vc_ref, do_ref, lse_ref, dl_ref, b_ref,
             dq_ref, dk_ref, dv_ref, db_ref, dkw_s, dvw_s, ck_s, cv_s):
        i = pl.program_id(1)

        @pl.when(i == 0)
        def _():
            db_ref[...] = jnp.zeros_like(db_ref)
            ck_s[...] = jnp.zeros_like(ck_s)
            cv_s[...] = jnp.zeros_like(cv_s)

        dkw_s[...] = jnp.zeros_like(dkw_s)
        dvw_s[...] = jnp.zeros_like(dvw_s)

        @pl.when(i < n)
        def _():
            first_tile = i == 0
            for hh in range(PAIR):
                kw, vw = _window(kp_ref, kc_ref, hh), _window(vp_ref, vc_ref, hh)
                bias = b_ref[hh]
                for j in range(nsub):
                    rows = pl.ds(SUB * j, SUB)
                    win = pl.ds(SUB * j, SUBK)
                    qs = q_ref[rows, dh * hh:dh * (hh + 1)] * QK_SCALE
                    st = _chunk_logits(kw, qs, bias, first_tile, j)
                    pt = jnp.exp(st - lse_ref[hh, 0, :, rows])
                    do_j = do_ref[rows, dh * hh:dh * (hh + 1)]
                    dpt = lax.dot_general(vw[SUB * j:SUB * j + SUBK], do_j, _NT, preferred_element_type=F32)
                    dst = pt * (dpt - dl_ref[hh, 0, :, rows])
                    db_ref[hh] += dst
                    dsb = dst.astype(BF16)
                    dvw_s[hh, win, :] += jnp.dot(pt.astype(BF16), do_j, preferred_element_type=F32)
                    dkw_s[hh, win, :] += jnp.dot(dsb, qs, preferred_element_type=F32)
                    dq_ref[hh, :, rows] = QK_SCALE * lax.dot_general(kw[SUB * j:SUB * j + SUBK], dsb, _TN,
                                                                     preferred_element_type=F32)

        for hh in range(PAIR):
            cols = slice(dh * hh, dh * (hh + 1))
            dk_ref[:, cols] = (ck_s[hh] + dkw_s[hh, 0:TQ, :]).astype(dk_ref.dtype)
            dv_ref[:, cols] = (cv_s[hh] + dvw_s[hh, 0:TQ, :]).astype(dv_ref.dtype)
        ck_s[...] = dkw_s[:, TQ:2 * TQ, :]
        cv_s[...] = dvw_s[:, TQ:2 * TQ, :]

    blk = lambda col, m: pl.BlockSpec((TQ, PAIR * dh), lambda hp, i: (m(i), col + hp))
    stat = pl.BlockSpec((PAIR, 1, 1, TQ), lambda hp, i: (hp, cur(i), 0, 0))
    bs = pl.BlockSpec((PAIR, SUBK, SUB), lambda hp, i: (hp, 0, 0))
    return _pcall(
        body, name=name,
        out_shape=(jax.ShapeDtypeStruct((N_HEADS, dh, s), F32), jax.ShapeDtypeStruct((s, WIDTH), BF16),
                   jax.ShapeDtypeStruct((s, WIDTH), BF16), jax.ShapeDtypeStruct((N_HEADS, SUBK, SUB), F32)),
        grid=(N_HEADS // PAIR, n + 1),
        in_specs=[blk(Q_CHUNK, cur), blk(K_CHUNK, prev), blk(K_CHUNK, cur), blk(V_CHUNK, prev), blk(V_CHUNK, cur),
                  blk(0, cur), stat, stat, bs],
        out_specs=(pl.BlockSpec((PAIR, dh, TQ), lambda hp, i: (hp, 0, cur(i))), blk(0, done), blk(0, done), bs),
        scratch_shapes=[pltpu.VMEM((PAIR, 2 * TQ, dh), F32), pltpu.VMEM((PAIR, 2 * TQ, dh), F32),
                        pltpu.VMEM((PAIR, TQ, dh), F32), pltpu.VMEM((PAIR, TQ, dh), F32)],
        compiler_params=_cp("parallel", "arbitrary"),
    )(qkv, qkv, qkv, qkv, qkv, do, lse, delta, bias_t)


def _mod_part(c_all, w_ada, b_ada, name):
    nl, d, n = w_ada.shape
    b = c_all.shape[0]

    def body(c_ref, w_ref, b_ref, o_ref, cond_ref):
        cv = c_ref[...]
        cond = cv * _sigmoid(cv)
        cond_ref[...] = cond
        o_ref[...] = jnp.dot(cond, w_ref[...], preferred_element_type=F32, precision=HI) + b_ref[...]

    return _pcall(
        body, name=name,
        out_shape=(jax.ShapeDtypeStruct((nl, b, n), F32), jax.ShapeDtypeStruct((b, d), F32)),
        grid=(nl,),
        in_specs=[pl.BlockSpec((b, d), lambda l: (0, 0)), pl.BlockSpec((None, d, n), lambda l: (l, 0, 0)),
                  pl.BlockSpec((None, 1, n), lambda l: (l, 0, 0))],
        out_specs=(pl.BlockSpec((None, b, n), lambda l: (l, 0, 0)), pl.BlockSpec((b, d), lambda l: (0, 0))),
        compiler_params=_cp("arbitrary"),
    )(c_all, w_ada, b_ada)


def _adamw(parts, w, m, v, name, tr=1024):
    p, rl, cp = parts[0].shape
    r, c = w.shape
    tr = _tile(rl, tr, 16)
    per = rl // tr
    c1 = 1.0 / (1.0 - ADAM_B1 ** ADAM_STEP)
    c2 = 1.0 / (1.0 - ADAM_B2 ** ADAM_STEP)

    def body(*refs):
        p_refs, (w_ref, m_ref, v_ref, g_ref, d_ref, nm_ref, nv_ref) = refs[:len(parts)], refs[len(parts):]
        for which, p_ref in enumerate(p_refs):
            pl.when(pl.program_id(0) // per == which)(
                functools.partial(update, p_ref, w_ref, m_ref, v_ref, g_ref, d_ref, nm_ref, nv_ref))

    def update(p_ref, w_ref, m_ref, v_ref, g_ref, d_ref, nm_ref, nv_ref):
        g = p_ref[0].astype(F32)
        for i in range(1, p):
            g = g + p_ref[i].astype(F32)
        g = g[:, :c]
        nm = ADAM_B1 * m_ref[...] + (1.0 - ADAM_B1) * g
        nv = ADAM_B2 * v_ref[...] + (1.0 - ADAM_B2) * (g * g)
        g_ref[...] = g
        nm_ref[...] = nm
        nv_ref[...] = nv
        d_ref[...] = -ADAM_LR * ((nm * c1) / (jnp.sqrt(nv * c2) + ADAM_EPS) + ADAM_WD * w_ref[...])

    blk = pl.BlockSpec((tr, c), lambda i: (i, 0))
    out = jax.ShapeDtypeStruct((r, c), F32)
    return _pcall(
        body, name=name, out_shape=(out, out, out, out), grid=(r // tr,),
        in_specs=[pl.BlockSpec((p, tr, cp), lambda i, _w=which: (0, jnp.clip(i - _w * per, 0, per - 1), 0))
                  for which in range(len(parts))] + [blk, blk, blk],
        out_specs=(blk, blk, blk, blk),
        compiler_params=_cp("parallel"),
    )(*parts, w, m, v)


def _pad_lanes(a):
    pad = (-a.shape[-1]) % 128
    return jnp.pad(a, [(0, 0)] * (a.ndim - 1) + [(0, pad)]) if pad else a


def _rows128(a, mult=16):
    flat = a.reshape(-1)
    n = flat.shape[0]
    per = 128 * mult
    pad = (-n) % per
    if pad:
        flat = jnp.concatenate([flat, jnp.zeros((pad,), a.dtype)])
    return flat.reshape(-1, 128)


def _pack(arrs, mult=16):
    pieces = [_rows128(a, mult) for a in arrs]
    return jnp.concatenate(pieces, axis=0), [p.shape[0] for p in pieces]


def _unpack(packed, rows, shapes, lead=()):
    out, at = [], 0
    for r, shp in zip(rows, shapes):
        n = int(np.prod(shp))
        piece = packed[..., at:at + r, :].reshape(lead + (r * 128,))[..., :n]
        out.append(piece.reshape(lead + tuple(shp)))
        at += r
    return out


def _rel_bias_tile(table):
    h = table.shape[0]
    lo = REL_CLIP - (CHUNK - 1)
    n_far = LEFT_CHUNKS * CHUNK + CHUNK - 1 - REL_CLIP
    vec = jnp.concatenate([table[:, lo:2 * REL_CLIP], jnp.repeat(table[:, 2 * REL_CLIP:], n_far + 1, axis=1)], axis=1)
    rev = vec[:, ::-1]
    n_vec = BAND + CHUNK - 1
    skew = jnp.tile(rev, (1, CHUNK + 1))[:, :CHUNK * (n_vec + 1)].reshape(h, CHUNK, n_vec + 1)
    bias = skew[:, ::-1, :BAND]
    neg = jnp.full((h, CHUNK, CHUNK), NEG, F32)
    two = jnp.concatenate([jnp.concatenate([bias, neg], axis=2), jnp.concatenate([neg, bias], axis=2)], axis=1)
    return two.transpose(0, 2, 1)


def _rel_bias_tile_grad(dbias_t):
    h = dbias_t.shape[0]
    two = dbias_t.transpose(0, 2, 1)
    dbias = two[:, :CHUNK, :BAND] + two[:, CHUNK:, CHUNK:]
    n_vec = BAND + CHUNK - 1
    dskew = jnp.pad(dbias[:, ::-1, :], ((0, 0), (0, 0), (0, n_vec + 1 - BAND))).reshape(h, CHUNK * (n_vec + 1))
    dskew = jnp.pad(dskew, ((0, 0), (0, (CHUNK + 1) * n_vec - CHUNK * (n_vec + 1))))
    drev = jnp.sum(dskew.reshape(h, CHUNK + 1, n_vec), axis=1)
    dvec = drev[:, ::-1]
    lo = REL_CLIP - (CHUNK - 1)
    n_near = 2 * REL_CLIP - lo
    return jnp.concatenate([jnp.zeros((h, lo), F32), dvec[:, :n_near],
                            jnp.sum(dvec[:, n_near:], axis=1, keepdims=True)], axis=1)


BIG = ("w_in", "w_br_fox", "w_br_chunk", "w_out", "w_up", "w_down")
SMALL = ("b_f", "rel_bias", "conv_b", "b_ada", "ln1_g", "ln1_b", "ln2_g", "ln2_b")
ORDER = ("w_in", "b_f", "rel_bias", "w_br_fox", "w_br_chunk", "w_out", "w_up", "conv_w", "conv_b", "w_down",
         "w_ada", "b_ada", "ln1_g", "ln1_b", "ln2_g", "ln2_b")


def _dest_major(g, axis):
    shp = g.shape
    g = g.reshape(shp[:axis] + (N_DEV, shp[axis] // N_DEV) + shp[axis + 1:])
    return jnp.moveaxis(g, axis, 0)


EARLY = ("w_down", "w_up")
LATE = ("w_in", "w_br_fox", "w_br_chunk", "w_out")


def _shard_axis(n):
    return 0 if n in ("w_out", "w_down") else 1


def _slab(g, n):
    return _pad_lanes(_dest_major(g, _shard_axis(n)).astype(BF16))


def _layer_fwd(x, h1, wl, mod, next_mod, alpha, gather=None):
    sh1, sc1, g1, sh2, sc2, g2 = mod
    s = x.shape[0]
    qkv = _mm(h1, wl["w_qkv"], out_dtype=BF16, name="proj_qkv", tk=1024)
    gates = _mm(h1, wl["w_gates"], out_dtype=F32, name="proj_gates", tk=1024)
    f_t = _mm(wl["w_f_t"], h1, out_dtype=F32, name="proj_f", tb=True, tk=1024)[:N_HEADS]
    f_t = f_t.reshape(N_HEADS, s // 128, 128)
    b_f = jnp.broadcast_to(wl["b_f"].reshape(N_HEADS, 1, 1), (N_HEADS, 1, 128))
    cum = _forget_fwd(f_t, b_f, "forget_fwd").reshape(N_SPLIT, N_HEADS, s)
    cum_cols = cum.transpose(2, 1, 0).astype(BF16)

    ones = jnp.ones((s, N_HEADS, N_SPLIT), BF16)
    zeros = jnp.zeros((s, N_HEADS, AUG - HEAD_DIM - 2 * N_SPLIT), BF16)
    q_a = qkv[:, :WIDTH].reshape(s, N_HEADS, HEAD_DIM) * QK_SCALE
    k_a = qkv[:, WIDTH:2 * WIDTH].reshape(s, N_HEADS, HEAD_DIM)
    q_aug = jnp.concatenate([q_a, cum_cols, ones, zeros], axis=-1).reshape(s, N_HEADS * AUG)
    k_aug = jnp.concatenate([k_a, ones, -cum_cols, zeros], axis=-1).reshape(s, N_HEADS * AUG)
    o_a_t, lse_a, *gathered = _fox_fwd(q_aug, k_aug, qkv, "fox_fwd_gather" if gather else "fox_fwd",
                                       exchange=(False, gather) if gather else None)
    bias_t = _rel_bias_tile(wl["rel_bias"])
    o_c_t, lse_c = _chunk_fwd(qkv, bias_t, "chunk_fwd")
    o_a = o_a_t.transpose(2, 0, 1).reshape(s, WIDTH).astype(BF16)
    o_c = o_c_t.transpose(2, 0, 1).reshape(s, WIDTH).astype(BF16)

    merged = _merge_fwd(o_a, o_c, wl["w_br_fox"], wl["w_br_chunk"], gates, "merge_fwd")
    mix, x1, h2 = _mm_res_ln(merged, wl["w_out"], x, g1, wl["ln1_g"], wl["ln1_b"], sc2, sh2, alpha, "out_ln1")
    u0 = _mm(h2, wl["w_up"], out_dtype=F32, name="ffn_up", tk=1024, tn=1408, o_halves=True)
    act = _conv_act_fwd(u0, wl["conv_w"], wl["conv_b"], "conv_act_fwd")
    y2, x2, h_next = _mm_res_ln(act, wl["w_down"], x1, g2, wl["ln2_g"], wl["ln2_b"], next_mod[0], next_mod[1],
                                alpha, "down_ln2", tk=1408)
    saved = dict(x=x, h1=h1, gates=gates, f_t=f_t, b_f=b_f, q_aug=q_aug, k_aug=k_aug, qkv=qkv,
                 bias_t=bias_t, o_a_t=o_a_t, o_c_t=o_c_t, lse_a=lse_a, lse_c=lse_c,
                 o_a=o_a, o_c=o_c, merged=merged, mix=mix, x1=x1, h2=h2, u0=u0, act=act, y2=y2)
    return x2, h_next, saved, gathered


def _layer_bwd(dx2, sv, wl, mod, alpha, send):
    sh1, sc1, g1, sh2, sc2, g2 = mod
    s = dx2.shape[0]
    g = {}
    dres2, dy2, sums = _ln_res_bwd(dx2, sv["x1"], sv["y2"], g2, wl["ln2_g"], alpha, "ln2_bwd")
    g["ln2_g"], g["ln2_b"], dg2 = sums[0], sums[1], sums[2]
    dact = _mm(dy2, wl["w_down_t"], out_dtype=F32, name="dact", tk=1024, tn=1408)
    g["w_down"] = _mm(sv["act"], dy2, out_dtype=F32, name="g_w_down", ta=True, tm=1408, tk=1024)
    du0, csum = _conv_act_bwd(dact, sv["u0"], wl["conv_w"], wl["conv_b"], "conv_act_bwd")
    g["conv_w"] = jnp.concatenate([csum[0, 0:3], csum[1, 0:3]], axis=1)
    g["conv_b"] = jnp.concatenate([csum[0, 3], csum[1, 3]])
    dh2 = _mm(du0, wl["w_up_t"], out_dtype=F32, name="dh2", tk=1408, a_halves=True)
    g["w_up"] = _mm(sv["h2"], du0, out_dtype=F32, name="g_w_up", ta=True, tk=1024, tn=1408, b_halves=True)
    dx1, sums = _ln_mod_bwd(dh2, sv["x1"], sc2, dres2, "ln_mod2_bwd")
    dsh2, dsc2 = sums[0], sums[1]

    dres1, dmix, sums = _ln_res_bwd(dx1, sv["x"], sv["mix"], g1, wl["ln1_g"], alpha, "ln1_bwd")
    g["ln1_g"], g["ln1_b"], dg1 = sums[0], sums[1], sums[2]
    g["w_out"] = _mm(sv["merged"], dmix, out_dtype=F32, name="g_w_out", ta=True, tk=1024)
    dba, dbc, do_a, do_c, dgates = _merge_bwd(
        dmix, wl["w_out_t"], sv["o_a"], sv["o_c"], wl["w_br_fox"], wl["w_br_chunk"],
        wl["w_br_fox_t"], wl["w_br_chunk_t"], sv["gates"], "merge_bwd")
    g["w_br_fox"] = _mm(sv["o_a"], dba, out_dtype=F32, name="g_w_br_fox", ta=True, tk=1024)
    g["w_br_chunk"] = _mm(sv["o_c"], dbc, out_dtype=F32, name="g_w_br_chunk", ta=True, tk=1024)
    n_t = s // min(TQ, s)
    stats = lambda a: a.reshape(N_HEADS, n_t, 1, s // n_t)

    delta_a = _row_dot(do_a.T.reshape(N_HEADS, HEAD_DIM, s), sv["o_a_t"], "delta_fox")
    send = list(send) + [_slab(g[n], n) for n in EARLY]
    dk_a, dv_a, dq_a_t, dsum, dqsum, *landed = _fox_bwd(sv["q_aug"], sv["k_aug"], sv["qkv"], do_a, sv["lse_a"],
                                                        stats(delta_a), "fox_bwd_scatter_%d" % len(send),
                                                        exchange=(True, send))
    dq_a = (dq_a_t * QK_SCALE).transpose(1, 3, 0, 2).reshape(s, WIDTH).astype(BF16)
    d_cum = dqsum.reshape(N_HEADS, s) - jnp.sum(dsum.reshape(s, N_HEADS, 128), axis=-1).T
    df_t, db_f = _forget_bwd(d_cum.reshape(N_HEADS, s // 128, 128), sv["f_t"], sv["b_f"], "forget_bwd")
    g["b_f"] = db_f[:, 0, 0]
    df_t = df_t.reshape(N_HEADS, s)

    delta_c = _row_dot(do_c.T.reshape(N_HEADS, HEAD_DIM, s), sv["o_c_t"], "delta_chunk")
    dq_c_t, dk_c, dv_c, dbias_t = _chunk_bwd(sv["qkv"], do_c, sv["lse_c"], stats(delta_c), sv["bias_t"], "chunk_bwd")
    g["rel_bias"] = _rel_bias_tile_grad(dbias_t)
    dq_c = dq_c_t.transpose(2, 0, 1).reshape(s, WIDTH).astype(BF16)

    dqkv = jnp.concatenate([dq_a, dk_a, dv_a, dq_c, dk_c, dv_c], axis=1)
    df_pad = jnp.zeros((16 - N_HEADS, s), F32)
    df16 = jnp.concatenate([df_t, df_pad], axis=0).astype(BF16)
    df_cols = jnp.concatenate([df16.T, jnp.zeros((s, AUG - 16), BF16)], axis=1)

    dh1 = _mm(dqkv, wl["w_qkv_t"], out_dtype=F32, name="dh1_qkv", tk=1024)
    dh1g = _mm(dgates, wl["w_gates_t"], out_dtype=F32, name="dh1_gates", tk=1024)
    g_qkv = _mm(sv["h1"], dqkv, out_dtype=F32, name="g_w_qkv", ta=True, tk=1024)
    g_gates = _mm(sv["h1"], dgates, out_dtype=F32, name="g_w_gates", ta=True, tk=1024)
    g_f_t = _mm(df16, sv["h1"], out_dtype=F32, name="g_w_f", tk=1024)[:N_HEADS]
    g["w_in"] = jnp.concatenate([g_qkv[:, :3 * WIDTH], g_f_t.T, g_qkv[:, 3 * WIDTH:], g_gates], axis=1)
    dx, sums = _ln_mod_bwd(dh1, sv["x"], sc1, dres1, "ln_mod1_bwd", extra=(dh1g, df_cols, wl["w_f_pad"]))
    dsh1, dsc1 = sums[0], sums[1]
    g["mod"] = jnp.concatenate([dsh1, dsc1, dg1, dsh2, dsc2, dg2])
    return dx, g, landed


def kernel(x, c, w_in, b_f, rel_bias, w_br_fox, w_br_chunk, w_out, w_up, conv_w, conv_b, w_down, w_ada, b_ada, ln1_g, ln1_b, ln2_g, ln2_b, loss_target, m_w_in, m_b_f, m_rel_bias, m_w_br_fox, m_w_br_chunk, m_w_out, m_w_up, m_conv_w, m_conv_b, m_w_down, m_w_ada, m_b_ada, m_ln1_g, m_ln1_b, m_ln2_g, m_ln2_b, v_w_in, v_b_f, v_rel_bias, v_w_br_fox, v_w_br_chunk, v_w_out, v_w_up, v_conv_w, v_conv_b, v_w_down, v_w_ada, v_b_ada, v_ln1_g, v_ln1_b, v_ln2_g, v_ln2_b):
    w = dict(w_in=w_in, b_f=b_f, rel_bias=rel_bias, w_br_fox=w_br_fox, w_br_chunk=w_br_chunk, w_out=w_out,
             w_up=w_up, conv_w=conv_w, conv_b=conv_b, w_down=w_down, w_ada=w_ada, b_ada=b_ada,
             ln1_g=ln1_g, ln1_b=ln1_b, ln2_g=ln2_g, ln2_b=ln2_b)
    m = dict(w_in=m_w_in, b_f=m_b_f, rel_bias=m_rel_bias, w_br_fox=m_w_br_fox, w_br_chunk=m_w_br_chunk,
             w_out=m_w_out, w_up=m_w_up, conv_w=m_conv_w, conv_b=m_conv_b, w_down=m_w_down, w_ada=m_w_ada,
             b_ada=m_b_ada, ln1_g=m_ln1_g, ln1_b=m_ln1_b, ln2_g=m_ln2_g, ln2_b=m_ln2_b)
    v = dict(w_in=v_w_in, b_f=v_b_f, rel_bias=v_rel_bias, w_br_fox=v_w_br_fox, w_br_chunk=v_w_br_chunk,
             w_out=v_w_out, w_up=v_w_up, conv_w=v_conv_w, conv_b=v_conv_b, w_down=v_w_down, w_ada=v_w_ada,
             b_ada=v_b_ada, ln1_g=v_ln1_g, ln1_b=v_ln1_b, ln2_g=v_ln2_g, ln2_b=v_ln2_b)
    depth, d, _ = w_in.shape
    s = x.shape[1]
    alpha = (2.0 * depth) ** 0.25
    me = 4 * lax.axis_index("x") + 2 * lax.axis_index("y") + lax.axis_index("c")
    x0 = x.reshape(s, d)
    target = loss_target.reshape(s, d)

    small_in, small_rows = _pack([c, conv_w], mult=8)
    small_all = _all_gather([small_in], "gather_c_conv")[0]
    c_all, conv_w_all = _unpack(small_all, small_rows, [c.shape, conv_w.shape], lead=(N_DEV,))
    c_all = c_all.reshape(N_DEV, d)
    conv_w_full = conv_w_all.transpose(1, 2, 0, 3).reshape(depth, conv_w.shape[1], -1)
    n_ada = w_ada.shape[2]
    b_ada_mine = lax.dynamic_slice_in_dim(b_ada, me * n_ada, n_ada, axis=1).reshape(depth, 1, n_ada)
    mod_part, cond_all = _mod_part(c_all, w_ada, b_ada_mine, "mod_part")
    mod_all = _all_gather([mod_part.reshape(depth * N_DEV, n_ada)], "gather_mod")[0]
    mod_all = mod_all.reshape(N_DEV, depth, N_DEV, n_ada)
    mod_mine = lax.dynamic_index_in_dim(mod_all, me, axis=2, keepdims=False)
    mod_mine = mod_mine.transpose(1, 0, 2).reshape(depth, N_MOD, 1, d)

    shards = lambda l: [_pad_lanes(w[n][l].astype(BF16)) for n in BIG]
    cols = np.cumsum([0, WIDTH, WIDTH, WIDTH, N_HEADS, WIDTH, WIDTH, WIDTH, d, d])

    def layer_weights(l, parts):
        full = {}
        for n, p in zip(BIG, parts):
            rows, ncol = w[n].shape[1:]
            p = jnp.moveaxis(p[..., :ncol], 0, _shard_axis(n))
            full[n] = p.reshape((rows * N_DEV, ncol) if _shard_axis(n) == 0 else (rows, ncol * N_DEV))
        wi = full["w_in"]
        w_qkv = jnp.concatenate([wi[:, :cols[3]], wi[:, cols[4]:cols[7]]], axis=1)
        w_gates = wi[:, cols[7]:]
        w_f_t = jnp.concatenate([wi[:, cols[3]:cols[4]].T, jnp.zeros((16 - N_HEADS, d), BF16)], axis=0)
        w_f_pad = jnp.concatenate([w_f_t, jnp.zeros((AUG - 16, d), BF16)], axis=0)
        row = lambda a: a[l].reshape(1, -1)
        return dict(
            w_qkv=w_qkv, w_gates=w_gates, w_f_t=w_f_t, w_f_pad=w_f_pad, w_qkv_t=w_qkv.T, w_gates_t=w_gates.T,
            w_br_fox=full["w_br_fox"], w_br_chunk=full["w_br_chunk"],
            w_br_fox_t=full["w_br_fox"].T, w_br_chunk_t=full["w_br_chunk"].T,
            w_out=full["w_out"], w_out_t=full["w_out"].T, w_up=full["w_up"], w_up_t=full["w_up"].T,
            w_down=full["w_down"], w_down_t=full["w_down"].T,
            conv_w=conv_w_full[l].reshape(3, 2, -1).transpose(1, 0, 2), conv_b=conv_b[l].reshape(2, 1, -1),
            b_f=b_f[l], rel_bias=rel_bias[l],
            ln1_g=row(ln1_g), ln1_b=row(ln1_b), ln2_g=row(ln2_g), ln2_b=row(ln2_b))

    xs, saved, layers = x0, [], []
    hs = _ln_mod(x0, mod_mine[0][1], mod_mine[0][0], "ln_mod1")
    parts = _all_gather(shards(0), "gather_weights")
    for l in range(depth):
        layers.append(layer_weights(l, parts))
        nxt = mod_mine[min(l + 1, depth - 1)]
        xs, hs, sv, parts = _layer_fwd(xs, hs, layers[l], list(mod_mine[l]), (nxt[1], nxt[0]), alpha,
                                       gather=shards(l + 1) if l + 1 < depth else None)
        saved.append(sv)
    dxs, loss_part = _loss_grad(xs, target, "loss_grad")
    loss = lax.psum(loss_part[0, 0], AXES)
    grads, big_parts, send = [None] * depth, {}, []
    for l in reversed(range(depth)):
        dxs, grads[l], landed = _layer_bwd(dxs, saved[l], layers[l], list(mod_mine[l]), alpha, send)
        for n, part in zip([(n, l + 1) for n in LATE if send] + [(n, l) for n in EARLY], landed):
            big_parts[n] = part
        send = [_slab(grads[l][n], n) for n in LATE]
    for n, part in zip(LATE, _all_to_all(send, "scatter_grads")):
        big_parts[n, 0] = part
    grad_x = dxs.reshape(1, s, d)
    stack = lambda n: jnp.stack([grads[l][n] for l in range(depth)])

    rep = ["b_ada"] + [n for n in SMALL if n != "b_ada"]
    small_g = [stack("mod")] + [stack(n) for n in rep[1:]] + [stack("conv_w")]
    small_pack, small_g_rows = _pack(small_g, mult=8)
    small_parts = _all_gather([small_pack], "gather_small_grads")[0]
    n_rep_rows = sum(small_g_rows[:-1])
    dmod_all = _unpack(small_parts, small_g_rows[:1], [(depth, N_MOD * d)], lead=(N_DEV,))[0]
    cw_all = _unpack(small_parts[:, n_rep_rows:], small_g_rows[-1:], [small_g[-1].shape], lead=(N_DEV,))[0]

    out = {}
    for n in BIG:
        rows2d = lambda a: a.reshape(-1, a.shape[-1])
        res = _adamw([big_parts[n, l] for l in range(depth)], rows2d(w[n]), rows2d(m[n]), rows2d(v[n]),
                     "adamw_" + n, tr=256)
        for kind, r in zip(("grad", "delta", "new_m", "new_v"), res):
            out[kind, n] = r.reshape(w[n].shape)

    dmod_mine = lax.dynamic_slice_in_dim(dmod_all, me * n_ada, n_ada, axis=2)
    g_ada = jnp.stack([_mm(cond_all.T, dmod_mine[:, l], out_dtype=F32, name="g_w_ada", tk=N_DEV, tn=n_ada,
                           precision=HI) for l in range(depth)])
    lp = [_rows128(t["w_ada"], 8) for t in (w, m, v)]
    res = _adamw([_rows128(g_ada, 8)[None]], *lp, "adamw_ada")
    for kind, r in zip(("grad", "delta", "new_m", "new_v"), res):
        out[kind, "w_ada"] = _unpack(r, [lp[0].shape[0]], [w_ada.shape])[0]

    rp = [_pack([t[n] for n in rep], mult=8)[0] for t in (w, m, v)]
    res = _adamw([small_parts[:, :n_rep_rows]], *rp, "adamw_small")
    for kind, r in zip(("grad", "delta", "new_m", "new_v"), res):
        for n, a in zip(rep, _unpack(r, small_g_rows[:-1], [w[n].shape for n in rep])):
            out[kind, n] = a

    n_cw = conv_w.shape[2]
    cw_mine = lax.dynamic_slice_in_dim(cw_all, me * n_cw, n_cw, axis=3)
    cp = [_rows128(t["conv_w"], 8) for t in (w, m, v)]
    cw_in = jnp.stack([_rows128(cw_mine[i], 8) for i in range(N_DEV)])
    res = _adamw([cw_in], *cp, "adamw_conv_w")
    for kind, r in zip(("grad", "delta", "new_m", "new_v"), res):
        out[kind, "conv_w"] = _unpack(r, [cp[0].shape[0]], [conv_w.shape])[0]

    result = [loss, grad_x]
    for kind in ("grad", "delta", "new_m", "new_v"):
        result += [out[kind, n] for n in ORDER]
    return tuple(result)
```

```python
import functools
import math

import jax
import jax.numpy as jnp
import numpy as np
from jax import lax
from jax.experimental import pallas as pl
from jax.experimental.pallas import tpu as pltpu

F32 = jnp.float32
BF16 = jnp.bfloat16
HI = lax.Precision.HIGHEST
MESH_ID = pl.DeviceIdType.MESH
AXES = ("x", "y", "c")
N_DEV = 8

N_HEADS = 8
HEAD_DIM = 64
WIDTH = N_HEADS * HEAD_DIM
CHUNK = 64
LEFT_CHUNKS = 8
BAND = (LEFT_CHUNKS + 1) * CHUNK
REL_CLIP = 128
LN_EPS = 1e-5
N_MOD = 6
QK_SCALE = 1.0 / math.sqrt(HEAD_DIM)
NEG = -1e30
AUG = 128
N_SPLIT = 3
TQ = 512
SUB = 2 * CHUNK
SUBK = BAND + CHUNK

ADAM_LR, ADAM_B1, ADAM_B2, ADAM_EPS, ADAM_WD, ADAM_STEP = 0.001, 0.9, 0.999, 1e-08, 0.01, 10

VMEM_LIMIT = 56 * 2 ** 20


def _pcall(body, **kw):
    return pl.pallas_call(body, **kw)


def _cp(*sem):
    return pltpu.CompilerParams(dimension_semantics=sem if sem else None, vmem_limit_bytes=VMEM_LIMIT)


def _my_place():
    return lax.axis_index("x"), lax.axis_index("y"), lax.axis_index("c")


N_PEER = N_DEV - 1


def _comm_call(body, xs, out_shapes, name):
    n = len(xs)
    hbm = pl.BlockSpec(memory_space=pltpu.HBM)
    return _pcall(
        functools.partial(body, n), name=name, out_shape=out_shapes, in_specs=[hbm] * n, out_specs=[hbm] * n,
        scratch_shapes=[pltpu.SemaphoreType.DMA((N_PEER * n,)), pltpu.SemaphoreType.DMA((N_PEER * n,)),
                        pltpu.SemaphoreType.DMA((n,))],
    )(*xs)


def _all_gather(xs, name):
    def body(n, *refs):
        x_refs, out_refs, (send_sems, recv_sems, local_sems) = refs[:n], refs[n:2 * n], refs[2 * n:]
        mx, my, mc = _my_place()
        me, sibling = (mx, my, mc), (mx, my, 1 - mc)
        chips = [(1 - mx, my), (mx, 1 - my), (1 - mx, 1 - my)]

        def slot(a, px, py, pc):
            return out_refs[a].at[4 * px + 2 * py + pc]

        def copy(a, k, block, to, own=False):
            return pltpu.make_async_remote_copy(
                src_ref=x_refs[a] if own else slot(a, *block), dst_ref=slot(a, *block),
                send_sem=send_sems.at[a * N_PEER + k], recv_sem=recv_sems.at[a * N_PEER + k],
                device_id=to, device_id_type=MESH_ID)

        arrays = range(n)
        mine = [pltpu.make_async_copy(x_refs[a], slot(a, *me), local_sems.at[a]) for a in arrays]
        first = [copy(a, 0, me, sibling, own=True) for a in arrays]
        first += [copy(a, 1 + j, me, (*chip, mc), own=True) for a in arrays for j, chip in enumerate(chips)]
        for cp in mine + first:
            cp.start()
        passed = []
        for a in arrays:
            for j, chip in enumerate(chips):
                copy(a, 1 + j, (*chip, mc), me).wait_recv()
                passed.append(copy(a, 4 + j, (*chip, mc), sibling))
                passed[-1].start()
        for a in arrays:
            copy(a, 0, sibling, me).wait_recv()
            for j, chip in enumerate(chips):
                copy(a, 4 + j, (*chip, 1 - mc), me).wait_recv()
        for cp in first + passed:
            cp.wait_send()
        for cp in mine:
            cp.wait()

    return _comm_call(body, xs, [jax.ShapeDtypeStruct((N_DEV,) + x.shape, x.dtype) for x in xs], name)


def _direct_exchange(scatter, x_refs, out_refs, send_sems, recv_sems, local_sems):
    mx, my, mc = _my_place()
    me = 4 * mx + 2 * my + mc
    mine, copies = [], []
    for a, (x_ref, out_ref) in enumerate(zip(x_refs, out_refs)):
        mine.append(pltpu.make_async_copy(x_ref.at[me] if scatter else x_ref, out_ref.at[me], local_sems.at[a]))
        for k in range(N_PEER):
            px, py, pc = mx ^ ((k + 1) >> 2), my ^ (((k + 1) >> 1) & 1), mc ^ ((k + 1) & 1)
            copies.append(pltpu.make_async_remote_copy(
                src_ref=x_ref.at[4 * px + 2 * py + pc] if scatter else x_ref, dst_ref=out_ref.at[me],
                send_sem=send_sems.at[a * N_PEER + k], recv_sem=recv_sems.at[a * N_PEER + k],
                device_id=(px, py, pc), device_id_type=MESH_ID))

    def start():
        for cp in mine + copies:
            cp.start()

    def wait():
        for cp in copies:
            cp.wait_recv()
        for cp in copies:
            cp.wait_send()
        for cp in mine:
            cp.wait()

    return start, wait


def _exchange_shapes(scatter, xs):
    return [jax.ShapeDtypeStruct(x.shape if scatter else (N_DEV,) + x.shape, x.dtype) for x in xs]


def _exchange_scratch(n):
    return [pltpu.SemaphoreType.DMA((N_PEER * n,)), pltpu.SemaphoreType.DMA((N_PEER * n,)),
            pltpu.SemaphoreType.DMA((n,))]


def _all_to_all(xs, name):
    def body(n, *refs):
        start, wait = _direct_exchange(True, refs[:n], refs[n:2 * n], *refs[2 * n:])
        start()
        wait()

    return _comm_call(body, xs, _exchange_shapes(True, xs), name)


def _tile(n, pref, unit=128):
    if n <= pref:
        return n
    t = pref - pref % unit
    while t > unit and n % t:
        t -= unit
    assert n % t == 0, (n, pref, unit)
    return t


def _mm(a, b, *, out_dtype, name, ta=False, tb=False, tm=1024, tn=1024, tk=512, precision=None,
        a_halves=False, b_halves=False, o_halves=False):
    a_shape = (a.shape[1], 2 * a.shape[2]) if a_halves else a.shape
    b_shape = (b.shape[1], 2 * b.shape[2]) if b_halves else b.shape
    (k_dim, m) = a_shape if ta else a_shape[::-1]
    n = b_shape[0] if tb else b_shape[1]
    tm, tn = _tile(m, tm, 8), _tile(n // 2 if (b_halves or o_halves) else n, tn)
    tk = _tile(k_dim // 2 if a_halves else k_dim, tk)
    nk = k_dim // tk
    dims = (((0 if ta else 1,), (1 if tb else 0,)), ((), ()))

    def body(a_ref, b_ref, o_ref, acc_ref):
        k = pl.program_id(2)
        part = lax.dot_general(a_ref[...], b_ref[...], dims, preferred_element_type=F32, precision=precision)
        if nk == 1:
            o_ref[...] = part.astype(o_ref.dtype)
            return

        @pl.when(k == 0)
        def _():
            acc_ref[...] = part

        @pl.when(jnp.logical_and(k > 0, k < nk - 1))
        def _():
            acc_ref[...] += part

        @pl.when(k == nk - 1)
        def _():
            o_ref[...] = (acc_ref[...] + part).astype(o_ref.dtype)

    def spec(shape2, pick, halves, per_half):
        if not halves:
            return pl.BlockSpec(shape2, pick)

        def index(i, j, k):
            r, c = pick(i, j, k)
            return (c // per_half, r, c % per_half)
        return pl.BlockSpec((None,) + shape2, index)

    assert not (ta and a_halves) and not (tb and b_halves)
    a_spec = spec((tk, tm), lambda i, j, k: (k, i), False, 0) if ta else \
        spec((tm, tk), lambda i, j, k: (i, k), a_halves, (k_dim // 2) // tk if a_halves else 0)
    b_spec = spec((tn, tk), lambda i, j, k: (j, k), False, 0) if tb else \
        spec((tk, tn), lambda i, j, k: (k, j), b_halves, (n // 2) // tn if b_halves else 0)
    o_spec = spec((tm, tn), lambda i, j, k: (i, j), o_halves, (n // 2) // tn if o_halves else 0)
    out_shape = (2, m, n // 2) if o_halves else (m, n)
    return _pcall(
        body, name=name,
        out_shape=jax.ShapeDtypeStruct(out_shape, out_dtype),
        grid=(m // tm, n // tn, nk),
        in_specs=[a_spec, b_spec],
        out_specs=o_spec,
        scratch_shapes=[pltpu.VMEM((tm, tn), F32)],
        compiler_params=_cp("parallel", "parallel", "arbitrary"),
    )(a, b)


def _ln_stats(x):
    mu = jnp.mean(x, axis=-1, keepdims=True)
    xc = x - mu
    var = jnp.mean(xc * xc, axis=-1, keepdims=True)
    rstd = lax.rsqrt(var + LN_EPS)
    return xc * rstd, rstd


def _ln_mod(x, sc, sh, name, tm=512):
    s, d = x.shape
    tm = min(tm, s)

    def body(x_ref, sc_ref, sh_ref, h_ref):
        xhat, _ = _ln_stats(x_ref[...])
        h_ref[...] = (xhat * (1.0 + sc_ref[...]) + sh_ref[...]).astype(h_ref.dtype)

    row = pl.BlockSpec((1, d), lambda i: (0, 0))
    return _pcall(
        body, name=name, out_shape=jax.ShapeDtypeStruct((s, d), BF16), grid=(s // tm,),
        in_specs=[pl.BlockSpec((tm, d), lambda i: (i, 0)), row, row],
        out_specs=pl.BlockSpec((tm, d), lambda i: (i, 0)),
        compiler_params=_cp("parallel"),
    )(x, sc, sh)


def _mm_res_ln(a, w, xres, gate, ln_g, ln_b, sc, sh, alpha, name, tm=512, tk=512):
    s, k_dim = a.shape
    d = w.shape[1]
    tm, tk = _tile(s, tm, 8), _tile(k_dim, tk)
    nk = k_dim // tk

    def body(a_ref, w_ref, x_ref, g_ref, lg_ref, lb_ref, sc_ref, sh_ref, y_ref, xn_ref, h_ref, acc_ref):
        k = pl.program_id(1)

        @pl.when(k == 0)
        def _():
            acc_ref[...] = jnp.zeros_like(acc_ref)

        acc_ref[...] += jnp.dot(a_ref[...], w_ref[...], preferred_element_type=F32)

        @pl.when(k == nk - 1)
        def _():
            y = acc_ref[...]
            y_ref[...] = y
            z = alpha * x_ref[...] + (1.0 + g_ref[...]) * y
            zhat, _ = _ln_stats(z)
            xn = zhat * lg_ref[...] + lb_ref[...]
            xn_ref[...] = xn
            xhat, _ = _ln_stats(xn)
            h_ref[...] = (xhat * (1.0 + sc_ref[...]) + sh_ref[...]).astype(h_ref.dtype)

    row = pl.BlockSpec((1, d), lambda i, k: (0, 0))
    tile = pl.BlockSpec((tm, d), lambda i, k: (i, 0))
    return _pcall(
        body, name=name,
        out_shape=(jax.ShapeDtypeStruct((s, d), F32), jax.ShapeDtypeStruct((s, d), F32),
                   jax.ShapeDtypeStruct((s, d), BF16)),
        grid=(s // tm, nk),
        in_specs=[pl.BlockSpec((tm, tk), lambda i, k: (i, k)), pl.BlockSpec((tk, d), lambda i, k: (k, 0)),
                  tile, row, row, row, row, row],
        out_specs=(tile, tile, tile),
        scratch_shapes=[pltpu.VMEM((tm, d), F32)],
        compiler_params=_cp("parallel", "arbitrary"),
    )(a, w, xres, gate, ln_g, ln_b, sc, sh)


def _colsum(v):
    return jnp.sum(v, axis=0, keepdims=True)


def _ln_res_bwd(dxn, xres, y, gate, ln_g, alpha, name, tm=256):
    s, d = dxn.shape
    tm = min(tm, s)

    def body(dxn_ref, x_ref, y_ref, g_ref, lg_ref, dres_ref, dy_ref, sums_ref):
        @pl.when(pl.program_id(0) == 0)
        def _():
            sums_ref[...] = jnp.zeros_like(sums_ref)

        dxn_v, y_v = dxn_ref[...], y_ref[...]
        one_g = 1.0 + g_ref[...]
        zhat, rstd = _ln_stats(alpha * x_ref[...] + one_g * y_v)
        dzh = dxn_v * lg_ref[...]
        dz = rstd * (dzh - jnp.mean(dzh, axis=-1, keepdims=True)
                     - zhat * jnp.mean(dzh * zhat, axis=-1, keepdims=True))
        dres_ref[...] = alpha * dz
        dy_ref[...] = (one_g * dz).astype(dy_ref.dtype)
        sums_ref[0:1, :] += _colsum(dxn_v * zhat)
        sums_ref[1:2, :] += _colsum(dxn_v)
        sums_ref[2:3, :] += _colsum(dz * y_v)

    row = pl.BlockSpec((1, d), lambda i: (0, 0))
    tile = pl.BlockSpec((tm, d), lambda i: (i, 0))
    return _pcall(
        body, name=name,
        out_shape=(jax.ShapeDtypeStruct((s, d), F32), jax.ShapeDtypeStruct((s, d), BF16),
                   jax.ShapeDtypeStruct((8, d), F32)),
        grid=(s // tm,),
        in_specs=[tile, tile, tile, row, row],
        out_specs=(tile, tile, pl.BlockSpec((8, d), lambda i: (0, 0))),
        compiler_params=_cp("arbitrary"),
    )(dxn, xres, y, gate, ln_g)


def _ln_mod_bwd(dh, x, sc, dres, name, extra=None, tm=256):
    s, d = dh.shape
    tm = min(tm, s)

    def body(*refs):
        if extra is None:
            dh_ref, x_ref, sc_ref, dres_ref, dx_ref, sums_ref = refs
            dh_v = dh_ref[...]
        else:
            dh_ref, x_ref, sc_ref, dres_ref, dh2_ref, df_ref, wf_ref, dx_ref, sums_ref = refs
            dh_v = dh_ref[...] + dh2_ref[...] + jnp.dot(df_ref[...], wf_ref[...], preferred_element_type=F32)

        @pl.when(pl.program_id(0) == 0)
        def _():
            sums_ref[...] = jnp.zeros_like(sums_ref)

        xhat, rstd = _ln_stats(x_ref[...])
        dxh = dh_v * (1.0 + sc_ref[...])
        dx_ref[...] = dres_ref[...] + rstd * (dxh - jnp.mean(dxh, axis=-1, keepdims=True)
                                             - xhat * jnp.mean(dxh * xhat, axis=-1, keepdims=True))
        sums_ref[0:1, :] += _colsum(dh_v)
        sums_ref[1:2, :] += _colsum(dh_v * xhat)

    row = pl.BlockSpec((1, d), lambda i: (0, 0))
    tile = pl.BlockSpec((tm, d), lambda i: (i, 0))
    in_specs = [tile, tile, row, tile]
    args = [dh, x, sc, dres]
    if extra is not None:
        in_specs += [tile, pl.BlockSpec((tm, AUG), lambda i: (i, 0)), pl.BlockSpec((AUG, d), lambda i: (0, 0))]
        args += list(extra)
    return _pcall(
        body, name=name,
        out_shape=(jax.ShapeDtypeStruct((s, d), F32), jax.ShapeDtypeStruct((8, d), F32)),
        grid=(s // tm,), in_specs=in_specs,
        out_specs=(tile, pl.BlockSpec((8, d), lambda i: (0, 0))),
        compiler_params=_cp("arbitrary"),
    )(*args)


def _loss_grad(xn, target, name, tm=512):
    s, d = xn.shape
    tm = min(tm, s)
    n = s // tm

    def body(x_ref, t_ref, dx_ref, loss_ref, acc_ref):
        i = pl.program_id(0)

        @pl.when(i == 0)
        def _():
            acc_ref[...] = jnp.zeros_like(acc_ref)

        err = x_ref[...] - t_ref[...]
        dx_ref[...] = err * (1.0 / d)
        acc_ref[...] += _colsum(err * err)

        @pl.when(i == n - 1)
        def _():
            loss_ref[...] = jnp.zeros_like(loss_ref) + (0.5 / d) * jnp.sum(acc_ref[...])

    tile = pl.BlockSpec((tm, d), lambda i: (i, 0))
    return _pcall(
        body, name=name,
        out_shape=(jax.ShapeDtypeStruct((s, d), F32), jax.ShapeDtypeStruct((8, 128), F32)),
        grid=(n,), in_specs=[tile, tile],
        out_specs=(tile, pl.BlockSpec((8, 128), lambda i: (0, 0))),
        scratch_shapes=[pltpu.VMEM((1, d), F32)],
        compiler_params=_cp("arbitrary"),
    )(xn, target)


def _sigmoid(v):
    return 0.5 * jnp.tanh(0.5 * v) + 0.5


def _merge_fwd(o_a, o_c, w_a, w_c, gates, name, tm=256):
    s, wd = o_a.shape
    d = w_a.shape[1]
    tm = min(tm, s)

    def body(oa_ref, oc_ref, wa_ref, wc_ref, ga_ref, gc_ref, m_ref):
        ba = jnp.dot(oa_ref[...], wa_ref[...], preferred_element_type=F32)
        bc = jnp.dot(oc_ref[...], wc_ref[...], preferred_element_type=F32)
        m_ref[...] = (_sigmoid(ga_ref[...]) * ba + _sigmoid(gc_ref[...]) * bc).astype(m_ref.dtype)

    o_spec = pl.BlockSpec((tm, wd), lambda i: (i, 0))
    w_spec = pl.BlockSpec((wd, d), lambda i: (0, 0))
    return _pcall(
        body, name=name, out_shape=jax.ShapeDtypeStruct((s, d), BF16), grid=(s // tm,),
        in_specs=[o_spec, o_spec, w_spec, w_spec,
                  pl.BlockSpec((tm, d), lambda i: (i, 0)), pl.BlockSpec((tm, d), lambda i: (i, 1))],
        out_specs=pl.BlockSpec((tm, d), lambda i: (i, 0)),
        compiler_params=_cp("parallel"),
    )(o_a, o_c, w_a, w_c, gates, gates)


def _merge_bwd(dmix, w_out_t, o_a, o_c, w_a, w_c, w_a_t, w_c_t, gates, name, tm=256):
    s, wd = o_a.shape
    d = w_a.shape[1]
    tm = min(tm, s)

    def body(dmix_ref, wot_ref, oa_ref, oc_ref, wa_ref, wc_ref, wat_ref, wct_ref, ga_ref, gc_ref,
             dba_ref, dbc_ref, doa_ref, doc_ref, dg_ref):
        dm = jnp.dot(dmix_ref[...], wot_ref[...], preferred_element_type=F32)
        for half, (o_ref, w_ref, wt_ref, g_ref, db_ref, do_ref) in enumerate((
                (oa_ref, wa_ref, wat_ref, ga_ref, dba_ref, doa_ref),
                (oc_ref, wc_ref, wct_ref, gc_ref, dbc_ref, doc_ref))):
            b = jnp.dot(o_ref[...], w_ref[...], preferred_element_type=F32)
            sg = _sigmoid(g_ref[...])
            db = (dm * sg).astype(BF16)
            db_ref[...] = db
            dg_ref[:, d * half:d * (half + 1)] = (dm * b * sg * (1.0 - sg)).astype(dg_ref.dtype)
            do_ref[...] = jnp.dot(db, wt_ref[...], preferred_element_type=F32).astype(do_ref.dtype)

    row_d = pl.BlockSpec((tm, d), lambda i: (i, 0))
    row_w = pl.BlockSpec((tm, wd), lambda i: (i, 0))
    full = lambda shp: pl.BlockSpec(shp, lambda i: (0, 0))
    return _pcall(
        body, name=name,
        out_shape=(jax.ShapeDtypeStruct((s, d), BF16), jax.ShapeDtypeStruct((s, d), BF16),
                   jax.ShapeDtypeStruct((s, wd), BF16), jax.ShapeDtypeStruct((s, wd), BF16),
                   jax.ShapeDtypeStruct((s, 2 * d), BF16)),
        grid=(s // tm,),
        in_specs=[row_d, full((d, d)), row_w, row_w, full((wd, d)), full((wd, d)), full((d, wd)), full((d, wd)),
                  row_d, pl.BlockSpec((tm, d), lambda i: (i, 1))],
        out_specs=(row_d, row_d, row_w, row_w, pl.BlockSpec((tm, 2 * d), lambda i: (i, 0))),
        compiler_params=_cp("parallel"),
    )(dmix, w_out_t, o_a, o_c, w_a, w_c, w_a_t, w_c_t, gates, gates)


def _shift_down(cur, k, fill_rows):
    out = pltpu.roll(cur, k, axis=0)
    rid = lax.broadcasted_iota(jnp.int32, cur.shape, 0)
    for r, fill in enumerate(fill_rows):
        out = jnp.where(rid == r, fill, out)
    return out


def _shift_up(cur, k, fill_rows):
    n = cur.shape[0]
    out = pltpu.roll(cur, n - k, axis=0)
    rid = lax.broadcasted_iota(jnp.int32, cur.shape, 0)
    for r, fill in enumerate(fill_rows):
        out = jnp.where(rid == n - k + r, fill, out)
    return out


def _conv_rows(cur, prev8, first, w, b):
    p6 = jnp.where(first, 0.0, prev8[6:7, :])
    p7 = jnp.where(first, 0.0, prev8[7:8, :])
    m1 = _shift_down(cur, 1, [p7])
    m2 = _shift_down(cur, 2, [p6, p7])
    u = b + w[0:1, :] * m2 + w[1:2, :] * m1 + w[2:3, :] * cur
    return u, m1, m2


def _conv_specs(tr, tc, order):
    r8 = tr // 8
    at = lambda f: (lambda *g: f(*order(*g)))
    return [pl.BlockSpec((2, tr, tc), at(lambda i, j: (0, i, j))),
            pl.BlockSpec((2, 8, tc), at(lambda i, j: (0, jnp.maximum(i * r8 - 1, 0), j))),
            pl.BlockSpec((2, 3, tc), at(lambda i, j: (0, 0, j))),
            pl.BlockSpec((2, 1, tc), at(lambda i, j: (0, 0, j)))]


def _conv_act_fwd(u0, conv_w, conv_b, name, tr=512, tc=256):
    _, s, f = u0.shape
    tr, tc = _tile(s, tr, 8), _tile(f, tc)

    def body(u_ref, p_ref, w_ref, b_ref, act_ref):
        first = pl.program_id(0) == 0
        a, _, _ = _conv_rows(u_ref[0], p_ref[0], first, w_ref[0], b_ref[0])
        v, _, _ = _conv_rows(u_ref[1], p_ref[1], first, w_ref[1], b_ref[1])
        act_ref[...] = (a * _sigmoid(a) * v).astype(act_ref.dtype)

    return _pcall(
        body, name=name, out_shape=jax.ShapeDtypeStruct((s, f), BF16), grid=(s // tr, f // tc),
        in_specs=_conv_specs(tr, tc, lambda i, j: (i, j)),
        out_specs=pl.BlockSpec((tr, tc), lambda i, j: (i, j)),
        compiler_params=_cp("parallel", "parallel"),
    )(u0, u0, conv_w, conv_b)


def _act_grads(dact, a, v):
    sg = _sigmoid(a)
    return dact * v * sg * (1.0 + a * (1.0 - sg)), dact * a * sg


def _conv_act_bwd(dact, u0, conv_w, conv_b, name, tr=512, tc=256):
    _, s, f = u0.shape
    tr, tc = _tile(s, tr, 8), _tile(f, tc)
    n = s // tr
    r8 = tr // 8

    def body(dact_ref, dnext_ref, u_ref, p_ref, unext_ref, w_ref, b_ref, du0_ref, sums_ref):
        i = pl.program_id(1)
        first, last = i == 0, i == n - 1

        @pl.when(first)
        def _():
            sums_ref[...] = jnp.zeros_like(sums_ref)

        cur = (u_ref[0], u_ref[1])
        (a, a1, a2), (v, v1, v2) = [_conv_rows(cur[hf], p_ref[hf], first, w_ref[hf], b_ref[hf]) for hf in range(2)]
        da, dv = _act_grads(dact_ref[...], a, v)
        (an, _, _), (vn, _, _) = [_conv_rows(unext_ref[hf], cur[hf][tr - 8:tr, :], False, w_ref[hf], b_ref[hf])
                                  for hf in range(2)]
        dan, dvn = _act_grads(dnext_ref[...], an, vn)
        for hf, g, gn, shifted in ((0, da, dan, (a2, a1, cur[0])), (1, dv, dvn, (v2, v1, cur[1]))):
            w = w_ref[hf]
            n0 = jnp.where(last, 0.0, gn[0:1, :])
            n1 = jnp.where(last, 0.0, gn[1:2, :])
            du0 = w[2:3, :] * g + w[1:2, :] * _shift_up(g, 1, [n0]) + w[0:1, :] * _shift_up(g, 2, [n0, n1])
            du0_ref[hf] = du0.astype(du0_ref.dtype)
            for r in range(3):
                sums_ref[hf, r:r + 1, :] += _colsum(g * shifted[r])
            sums_ref[hf, 3:4, :] += _colsum(g)

    nxt = lambda i: jnp.minimum((i + 1) * r8, s // 8 - 1)
    order = lambda j, i: (i, j)
    specs = _conv_specs(tr, tc, order)
    return _pcall(
        body, name=name,
        out_shape=(jax.ShapeDtypeStruct((2, s, f), BF16), jax.ShapeDtypeStruct((2, 8, f), F32)),
        grid=(f // tc, n),
        in_specs=[pl.BlockSpec((tr, tc), lambda j, i: (i, j)), pl.BlockSpec((8, tc), lambda j, i: (nxt(i), j)),
                  specs[0], specs[1], pl.BlockSpec((2, 8, tc), lambda j, i: (0, nxt(i), j)), specs[2], specs[3]],
        out_specs=(pl.BlockSpec((2, tr, tc), lambda j, i: (0, i, j)), pl.BlockSpec((2, 8, tc), lambda j, i: (0, 0, j))),
        compiler_params=_cp("parallel", "arbitrary"),
    )(dact, dact, u0, u0, u0, conv_w, conv_b)


def _cumsum_rows(v, reverse=False):
    r = v.shape[0]
    i128 = lax.broadcasted_iota(jnp.int32, (128, 128), 0), lax.broadcasted_iota(jnp.int32, (128, 128), 1)
    ir = lax.broadcasted_iota(jnp.int32, (r, r), 0), lax.broadcasted_iota(jnp.int32, (r, r), 1)
    in_row = (i128[0] >= i128[1] if reverse else i128[0] <= i128[1]).astype(F32)
    rows = (ir[1] > ir[0] if reverse else ir[1] < ir[0]).astype(F32)
    within = jnp.dot(v, in_row, preferred_element_type=F32, precision=HI)
    tot = jnp.broadcast_to(within[:, 0:1] if reverse else within[:, 127:128], (r, 128))
    return within + jnp.dot(rows, tot, preferred_element_type=F32, precision=HI)


def _forget_fwd(f_t, b_f, name):
    h, r, _ = f_t.shape

    def body(f_ref, b_ref, o_ref):
        z = f_ref[...] + b_ref[...]
        logf = jnp.minimum(z, 0.0) - jnp.log(1.0 + jnp.exp(-jnp.abs(z)))
        rest = _cumsum_rows(logf)
        for i in range(N_SPLIT):
            piece = rest.astype(BF16).astype(F32)
            o_ref[i] = piece
            rest = rest - piece

    return _pcall(
        body, name=name, out_shape=jax.ShapeDtypeStruct((N_SPLIT, h, r, 128), F32), grid=(h,),
        in_specs=[pl.BlockSpec((None, r, 128), lambda i: (i, 0, 0)), pl.BlockSpec((None, 1, 128), lambda i: (i, 0, 0))],
        out_specs=pl.BlockSpec((N_SPLIT, None, r, 128), lambda i: (0, i, 0, 0)),
        compiler_params=_cp("parallel"),
    )(f_t, b_f)


def _forget_bwd(d_cum, f_t, b_f, name):
    h, r, _ = f_t.shape

    def body(g_ref, f_ref, b_ref, df_ref, db_ref):
        df = _cumsum_rows(g_ref[...], reverse=True) * _sigmoid(-(f_ref[...] + b_ref[...]))
        df_ref[...] = df
        db_ref[...] = jnp.zeros_like(db_ref) + jnp.sum(df)

    blk = pl.BlockSpec((None, r, 128), lambda i: (i, 0, 0))
    one = pl.BlockSpec((None, 1, 128), lambda i: (i, 0, 0))
    return _pcall(
        body, name=name,
        out_shape=(jax.ShapeDtypeStruct((h, r, 128), F32), jax.ShapeDtypeStruct((h, 1, 128), F32)),
        grid=(h,), in_specs=[blk, blk, one], out_specs=(blk, one),
        compiler_params=_cp("parallel"),
    )(d_cum, f_t, b_f)


_NT = (((1,), (1,)), ((), ()))


def _causal_keep(tk, tq):
    return lax.broadcasted_iota(jnp.int32, (tk, tq), 0) <= lax.broadcasted_iota(jnp.int32, (tk, tq), 1)


_TN = (((0,), (0,)), ((), ()))
PAIR = 2
V_FOX = 2 * WIDTH // 128
Q_CHUNK, K_CHUNK, V_CHUNK = (3 * WIDTH) // 128, (4 * WIDTH) // 128, (5 * WIDTH) // 128


def _riding_exchange(exchange, n_in, n_out, grid):
    if exchange is None:
        return [], [], [], [], lambda refs: (refs, lambda: None)
    scatter, xs = exchange
    n = len(xs)
    hbm = pl.BlockSpec(memory_space=pltpu.HBM)

    def hook(refs):
        ins, x_refs = refs[:n_in], refs[n_in:n_in + n]
        outs, land_refs = refs[n_in + n:n_in + n + n_out], refs[n_in + n + n_out:n_in + 2 * n + n_out]
        rest = refs[n_in + 2 * n + n_out:]
        own_scratch, sems = rest[:len(rest) - 3], rest[len(rest) - 3:]
        start, wait = _direct_exchange(scatter, x_refs, land_refs, *sems)
        ids = [pl.program_id(a) for a in range(len(grid))]
        is_first = functools.reduce(jnp.logical_and, [i == 0 for i in ids])
        is_last = functools.reduce(jnp.logical_and, [i == g - 1 for i, g in zip(ids, grid)])
        pl.when(is_first)(start)
        return tuple(ins) + tuple(outs) + tuple(own_scratch), lambda: pl.when(is_last)(wait)

    return list(xs), _exchange_shapes(scatter, xs), [hbm] * n, _exchange_scratch(n), hook


def _fox_fwd(q_aug, k_aug, qkv, name, t=TQ, exchange=None):
    s = q_aug.shape[0]
    t = min(t, s)
    n = s // t
    dh = HEAD_DIM
    grid = (N_HEADS // PAIR, n)
    ex_in, ex_out, ex_specs, ex_scratch, hook = _riding_exchange(exchange, 3, 2, grid)

    def body(*refs):
        (q_ref, k_ref, v_ref, o_ref, lse_ref, m_s, l_s, acc_s), finish = hook(refs)
        qi = pl.program_id(1)
        m_s[...] = jnp.full_like(m_s, NEG)
        l_s[...] = jnp.zeros_like(l_s)
        acc_s[...] = jnp.zeros_like(acc_s)

        def step(kj, diag):
            rows = pl.ds(pl.multiple_of(kj * t, t), t)
            for hh in range(PAIR):
                st = lax.dot_general(k_ref[rows, AUG * hh:AUG * (hh + 1)], q_ref[:, AUG * hh:AUG * (hh + 1)], _NT,
                                     preferred_element_type=F32)
                if diag:
                    st = jnp.where(_causal_keep(t, t), st, NEG)
                m_prev = m_s[hh]
                m_new = jnp.maximum(m_prev, jnp.max(st, axis=0, keepdims=True))
                a = jnp.exp(m_prev - m_new)
                pt = jnp.exp(st - m_new)
                l_s[hh] = a * l_s[hh] + jnp.sum(pt, axis=0, keepdims=True)
                acc_s[hh] = a * acc_s[hh] + lax.dot_general(v_ref[rows, dh * hh:dh * (hh + 1)], pt.astype(BF16), _TN,
                                                            preferred_element_type=F32)
                m_s[hh] = m_new

        def off_diagonal(kj, carry):
            step(kj, False)
            return carry

        lax.fori_loop(0, qi, off_diagonal, 0)
        step(qi, True)
        for hh in range(PAIR):
            o_ref[hh] = acc_s[hh] / l_s[hh]
            lse_ref[hh, 0] = m_s[hh] + jnp.log(l_s[hh])
        finish()

    return _pcall(
        body, name=name,
        out_shape=[jax.ShapeDtypeStruct((N_HEADS, dh, s), F32), jax.ShapeDtypeStruct((N_HEADS, n, 1, t), F32)] + ex_out,
        grid=grid,
        in_specs=[pl.BlockSpec((t, PAIR * AUG), lambda hp, i: (i, hp)),
                  pl.BlockSpec((s, PAIR * AUG), lambda hp, i: (0, hp)),
                  pl.BlockSpec((s, PAIR * dh), lambda hp, i: (0, V_FOX + hp))] + ex_specs,
        out_specs=[pl.BlockSpec((PAIR, dh, t), lambda hp, i: (hp, 0, i)),
                   pl.BlockSpec((PAIR, 1, 1, t), lambda hp, i: (hp, i, 0, 0))] + ex_specs,
        scratch_shapes=[pltpu.VMEM((PAIR, 1, t), F32), pltpu.VMEM((PAIR, 1, t), F32),
                        pltpu.VMEM((PAIR, dh, t), F32)] + ex_scratch,
        compiler_params=_cp("arbitrary", "arbitrary"),
    )(q_aug, k_aug, qkv, *ex_in)


def _row_dot(a_t, b_t, name):
    h, n, dh, t = a_t.shape

    def body(a_ref, b_ref, o_ref):
        o_ref[...] = jnp.sum(a_ref[...].astype(F32) * b_ref[...], axis=0, keepdims=True)

    return _pcall(
        body, name=name, out_shape=jax.ShapeDtypeStruct((h, n, 1, t), F32), grid=(h, n),
        in_specs=[pl.BlockSpec((None, None, dh, t), lambda hh, i: (hh, i, 0, 0)),
                  pl.BlockSpec((None, dh, t), lambda hh, i: (hh, 0, i))],
        out_specs=pl.BlockSpec((None, None, 1, t), lambda hh, i: (hh, i, 0, 0)),
        compiler_params=_cp("parallel", "parallel"),
    )(a_t, b_t)


def _fox_bwd(q_aug, k_aug, qkv, do, q_t, do_t, lse, delta, name, t=TQ, exchange=None):
    s = q_aug.shape[0]
    dh = HEAD_DIM
    t = min(t, s)
    n = s // t
    grid = (N_HEADS // PAIR, n)
    ex_in, ex_out, ex_specs, ex_scratch, hook = _riding_exchange(exchange, 8, 5, grid)

    def body(*refs):
        (q_ref, k_ref, v_ref, do_ref, qt_ref, dot_ref, lse_ref, dl_ref, dk_ref, dv_ref, dq_ref, dsum_ref,
         dqsum_ref, dk_s, dv_s, dsum_s), finish = hook(refs)
        kj = pl.program_id(1)

        @pl.when(kj == 0)
        def _():
            dq_ref[...] = jnp.zeros_like(dq_ref)
            dqsum_ref[...] = jnp.zeros_like(dqsum_ref)

        dk_s[...] = jnp.zeros_like(dk_s)
        dv_s[...] = jnp.zeros_like(dv_s)
        dsum_s[...] = jnp.zeros_like(dsum_s)

        def step(qi, diag):
            rows = pl.ds(pl.multiple_of(qi * t, t), t)
            for hh in range(PAIR):
                q = q_ref[rows, AUG * hh:AUG * (hh + 1)]
                k = k_ref[:, AUG * hh:AUG * (hh + 1)]
                st = lax.dot_general(k, q, _NT, preferred_element_type=F32)
                if diag:
                    st = jnp.where(_causal_keep(t, t), st, NEG)
                pt = jnp.exp(st - lse_ref[hh, qi])
                do_v = do_ref[rows, dh * hh:dh * (hh + 1)]
                dpt = lax.dot_general(v_ref[:, dh * hh:dh * (hh + 1)], do_v, _NT, preferred_element_type=F32)
                ds32 = pt * (dpt - dl_ref[hh, qi])
                dsum_s[hh] += sum(ds32[:, 128 * u:128 * (u + 1)] for u in range(t // 128))
                dqsum_ref[hh, qi] += jnp.sum(ds32, axis=0, keepdims=True)
                dst = ds32.astype(BF16)
                dv_s[hh] += lax.dot_general(dot_ref[hh, qi], pt.astype(BF16), _NT, preferred_element_type=F32)
                dk_s[hh] += lax.dot_general(qt_ref[hh, qi], dst, _NT, preferred_element_type=F32)
                dq_ref[hh, qi] += lax.dot_general(k[:, :dh], dst, _TN, preferred_element_type=F32)

        def off_diagonal(qi, carry):
            step(qi, False)
            return carry

        step(kj, True)
        lax.fori_loop(kj + 1, n, off_diagonal, 0)
        dk_ref[...] = dk_s[...].astype(dk_ref.dtype)
        dv_ref[...] = dv_s[...].astype(dv_ref.dtype)
        for hh in range(PAIR):
            dsum_ref[:, 128 * hh:128 * (hh + 1)] = dsum_s[hh]
        finish()

    stat = pl.BlockSpec((PAIR, n, 1, t), lambda hp, j: (hp, 0, 0, 0))
    whole_t = pl.BlockSpec((PAIR, n, dh, t), lambda hp, j: (hp, 0, 0, 0))
    tile_t = pl.BlockSpec((PAIR, dh, t), lambda hp, j: (hp, 0, j))
    return _pcall(
        body, name=name,
        out_shape=[jax.ShapeDtypeStruct((N_HEADS, dh, s), BF16), jax.ShapeDtypeStruct((N_HEADS, dh, s), BF16),
                   jax.ShapeDtypeStruct((N_HEADS, n, dh, t), F32), jax.ShapeDtypeStruct((s, N_HEADS * 128), F32),
                   jax.ShapeDtypeStruct((N_HEADS, n, 1, t), F32)] + ex_out,
        grid=grid,
        in_specs=[pl.BlockSpec((s, PAIR * AUG), lambda hp, j: (0, hp)),
                  pl.BlockSpec((t, PAIR * AUG), lambda hp, j: (j, hp)),
                  pl.BlockSpec((t, PAIR * dh), lambda hp, j: (j, V_FOX + hp)),
                  pl.BlockSpec((s, PAIR * dh), lambda hp, j: (0, hp)),
                  whole_t, whole_t, stat, stat] + ex_specs,
        out_specs=[tile_t, tile_t, whole_t, pl.BlockSpec((t, PAIR * 128), lambda hp, j: (j, hp)), stat] + ex_specs,
        scratch_shapes=[pltpu.VMEM((PAIR, dh, t), F32), pltpu.VMEM((PAIR, dh, t), F32),
                        pltpu.VMEM((PAIR, t, 128), F32)] + ex_scratch,
        compiler_params=_cp("arbitrary", "arbitrary"),
    )(q_aug, k_aug, qkv, do, q_t, do_t, lse, delta, *ex_in)


def _chunk_logits(kw, qs, bias, first_tile, j):
    st = lax.dot_general(kw[SUB * j:SUB * j + SUBK], qs, _NT, preferred_element_type=F32) + bias
    row = lax.broadcasted_iota(jnp.int32, st.shape, 0)
    return jnp.where(jnp.logical_or(jnp.logical_not(first_tile), row >= TQ - SUB * j), st, NEG)


def _window(prev_ref, cur_ref, hh):
    cols = slice(HEAD_DIM * hh, HEAD_DIM * (hh + 1))
    return jnp.concatenate([prev_ref[:, cols], cur_ref[:, cols]], axis=0)


def _chunk_fwd(qkv, bias_t, name):
    s = qkv.shape[0]
    dh = HEAD_DIM
    n = s // TQ
    nsub = TQ // SUB
    prev = lambda i: jnp.maximum(i - 1, 0)

    def body(q_ref, kp_ref, kc_ref, vp_ref, vc_ref, b_ref, o_ref, lse_ref):
        first_tile = pl.program_id(1) == 0
        for hh in range(PAIR):
            kw, vw = _window(kp_ref, kc_ref, hh), _window(vp_ref, vc_ref, hh)
            bias = b_ref[hh]
            for j in range(nsub):
                qs = q_ref[SUB * j:SUB * (j + 1), dh * hh:dh * (hh + 1)] * QK_SCALE
                st = _chunk_logits(kw, qs, bias, first_tile, j)
                m = jnp.max(st, axis=0, keepdims=True)
                pt = jnp.exp(st - m)
                l = jnp.sum(pt, axis=0, keepdims=True)
                ot = lax.dot_general(vw[SUB * j:SUB * j + SUBK], pt.astype(BF16), _TN, preferred_element_type=F32)
                o_ref[hh, :, SUB * j:SUB * (j + 1)] = ot / l
                lse_ref[hh, 0, :, SUB * j:SUB * (j + 1)] = m + jnp.log(l)

    blk = lambda col, m: pl.BlockSpec((TQ, PAIR * dh), lambda hp, i: (m(i), col + hp))
    same = lambda i: i
    return _pcall(
        body, name=name,
        out_shape=(jax.ShapeDtypeStruct((N_HEADS, dh, s), F32), jax.ShapeDtypeStruct((N_HEADS, n, 1, TQ), F32)),
        grid=(N_HEADS // PAIR, n),
        in_specs=[blk(Q_CHUNK, same), blk(K_CHUNK, prev), blk(K_CHUNK, same), blk(V_CHUNK, prev), blk(V_CHUNK, same),
                  pl.BlockSpec((PAIR, SUBK, SUB), lambda hp, i: (hp, 0, 0))],
        out_specs=(pl.BlockSpec((PAIR, dh, TQ), lambda hp, i: (hp, 0, i)),
                   pl.BlockSpec((PAIR, 1, 1, TQ), lambda hp, i: (hp, i, 0, 0))),
        compiler_params=_cp("parallel", "parallel"),
    )(qkv, qkv, qkv, qkv, qkv, bias_t)


def _chunk_bwd(qkv, do, lse, delta, bias_t, name):
    s = qkv.shape[0]
    dh = HEAD_DIM
    n = s // TQ
    nsub = TQ // SUB
    cur = lambda i: jnp.minimum(i, n - 1)
    prev = lambda i: jnp.maximum(cur(i) - 1, 0)
    done = lambda i: jnp.maximum(i - 1, 0)

    def body(q_ref, kp_ref, kc_ref, vp_ref, vc_ref, do_ref, lse_ref, dl_ref, b_ref,
             dq_ref, dk_ref, dv_ref, db_ref, dkw_s, dvw_s, ck_s, cv_s):
        i = pl.program_id(1)

        @pl.when(i == 0)
        def _():
            db_ref[...] = jnp.zeros_like(db_ref)
            ck_s[...] = jnp.zeros_like(ck_s)
            cv_s[...] = jnp.zeros_like(cv_s)

        dkw_s[...] = jnp.zeros_like(dkw_s)
        dvw_s[...] = jnp.zeros_like(dvw_s)

        @pl.when(i < n)
        def _():
            first_tile = i == 0
            for hh in range(PAIR):
                kw, vw = _window(kp_ref, kc_ref, hh), _window(vp_ref, vc_ref, hh)
                bias = b_ref[hh]
                for j in range(nsub):
                    rows = pl.ds(SUB * j, SUB)
                    win = pl.ds(SUB * j, SUBK)
                    qs = q_ref[rows, dh * hh:dh * (hh + 1)] * QK_SCALE
                    st = _chunk_logits(kw, qs, bias, first_tile, j)
                    pt = jnp.exp(st - lse_ref[hh, 0, :, rows])
                    do_j = do_ref[rows, dh * hh:dh * (hh + 1)]
                    dpt = lax.dot_general(vw[SUB * j:SUB * j + SUBK], do_j, _NT, preferred_element_type=F32)
                    dst = pt * (dpt - dl_ref[hh, 0, :, rows])
                    db_ref[hh] += dst
                    dsb = dst.astype(BF16)
                    dvw_s[hh, win, :] += jnp.dot(pt.astype(BF16), do_j, preferred_element_type=F32)
                    dkw_s[hh, win, :] += jnp.dot(dsb, qs, preferred_element_type=F32)
                    dq_ref[hh, :, rows] = QK_SCALE * lax.dot_general(kw[SUB * j:SUB * j + SUBK], dsb, _TN,
                                                                     preferred_element_type=F32)

        for hh in range(PAIR):
            cols = slice(dh * hh, dh * (hh + 1))
            dk_ref[:, cols] = (ck_s[hh] + dkw_s[hh, 0:TQ, :]).astype(dk_ref.dtype)
            dv_ref[:, cols] = (cv_s[hh] + dvw_s[hh, 0:TQ, :]).astype(dv_ref.dtype)
        ck_s[...] = dkw_s[:, TQ:2 * TQ, :]
        cv_s[...] = dvw_s[:, TQ:2 * TQ, :]

    blk = lambda col, m: pl.BlockSpec((TQ, PAIR * dh), lambda hp, i: (m(i), col + hp))
    stat = pl.BlockSpec((PAIR, 1, 1, TQ), lambda hp, i: (hp, cur(i), 0, 0))
    bs = pl.BlockSpec((PAIR, SUBK, SUB), lambda hp, i: (hp, 0, 0))
    return _pcall(
        body, name=name,
        out_shape=(jax.ShapeDtypeStruct((N_HEADS, dh, s), F32), jax.ShapeDtypeStruct((s, WIDTH), BF16),
                   jax.ShapeDtypeStruct((s, WIDTH), BF16), jax.ShapeDtypeStruct((N_HEADS, SUBK, SUB), F32)),
        grid=(N_HEADS // PAIR, n + 1),
        in_specs=[blk(Q_CHUNK, cur), blk(K_CHUNK, prev), blk(K_CHUNK, cur), blk(V_CHUNK, prev), blk(V_CHUNK, cur),
                  blk(0, cur), stat, stat, bs],
        out_specs=(pl.BlockSpec((PAIR, dh, TQ), lambda hp, i: (hp, 0, cur(i))), blk(0, done), blk(0, done), bs),
        scratch_shapes=[pltpu.VMEM((PAIR, 2 * TQ, dh), F32), pltpu.VMEM((PAIR, 2 * TQ, dh), F32),
                        pltpu.VMEM((PAIR, TQ, dh), F32), pltpu.VMEM((PAIR, TQ, dh), F32)],
        compiler_params=_cp("parallel", "arbitrary"),
    )(qkv, qkv, qkv, qkv, qkv, do, lse, delta, bias_t)


def _mod_part(c_all, w_ada, b_ada, name):
    nl, d, n = w_ada.shape
    b = c_all.shape[0]

    def body(c_ref, w_ref, b_ref, o_ref, cond_ref):
        cv = c_ref[...]
        cond = cv * _sigmoid(cv)
        cond_ref[...] = cond
        o_ref[...] = jnp.dot(cond, w_ref[...], preferred_element_type=F32, precision=HI) + b_ref[...]

    return _pcall(
        body, name=name,
        out_shape=(jax.ShapeDtypeStruct((nl, b, n), F32), jax.ShapeDtypeStruct((b, d), F32)),
        grid=(nl,),
        in_specs=[pl.BlockSpec((b, d), lambda l: (0, 0)), pl.BlockSpec((None, d, n), lambda l: (l, 0, 0)),
                  pl.BlockSpec((None, 1, n), lambda l: (l, 0, 0))],
        out_specs=(pl.BlockSpec((None, b, n), lambda l: (l, 0, 0)), pl.BlockSpec((b, d), lambda l: (0, 0))),
        compiler_params=_cp("arbitrary"),
    )(c_all, w_ada, b_ada)


def _adamw(parts, w, m, v, name, tr=1024):
    p, rl, cp = parts[0].shape
    r, c = w.shape
    tr = _tile(rl, tr, 16)
    per = rl // tr
    c1 = 1.0 / (1.0 - ADAM_B1 ** ADAM_STEP)
    c2 = 1.0 / (1.0 - ADAM_B2 ** ADAM_STEP)

    def body(*refs):
        p_refs, (w_ref, m_ref, v_ref, g_ref, d_ref, nm_ref, nv_ref) = refs[:len(parts)], refs[len(parts):]
        for which, p_ref in enumerate(p_refs):
            pl.when(pl.program_id(0) // per == which)(
                functools.partial(update, p_ref, w_ref, m_ref, v_ref, g_ref, d_ref, nm_ref, nv_ref))

    def update(p_ref, w_ref, m_ref, v_ref, g_ref, d_ref, nm_ref, nv_ref):
        g = p_ref[0].astype(F32)
        for i in range(1, p):
            g = g + p_ref[i].astype(F32)
        g = g[:, :c]
        nm = ADAM_B1 * m_ref[...] + (1.0 - ADAM_B1) * g
        nv = ADAM_B2 * v_ref[...] + (1.0 - ADAM_B2) * (g * g)
        g_ref[...] = g
        nm_ref[...] = nm
        nv_ref[...] = nv
        d_ref[...] = -ADAM_LR * ((nm * c1) / (jnp.sqrt(nv * c2) + ADAM_EPS) + ADAM_WD * w_ref[...])

    blk = pl.BlockSpec((tr, c), lambda i: (i, 0))
    out = jax.ShapeDtypeStruct((r, c), F32)
    return _pcall(
        body, name=name, out_shape=(out, out, out, out), grid=(r // tr,),
        in_specs=[pl.BlockSpec((p, tr, cp), lambda i, _w=which: (0, jnp.clip(i - _w * per, 0, per - 1), 0))
                  for which in range(len(parts))] + [blk, blk, blk],
        out_specs=(blk, blk, blk, blk),
        compiler_params=_cp("parallel"),
    )(*parts, w, m, v)


def _pad_lanes(a):
    pad = (-a.shape[-1]) % 128
    return jnp.pad(a, [(0, 0)] * (a.ndim - 1) + [(0, pad)]) if pad else a


def _rows128(a, mult=16):
    flat = a.reshape(-1)
    n = flat.shape[0]
    per = 128 * mult
    pad = (-n) % per
    if pad:
        flat = jnp.concatenate([flat, jnp.zeros((pad,), a.dtype)])
    return flat.reshape(-1, 128)


def _pack(arrs, mult=16):
    pieces = [_rows128(a, mult) for a in arrs]
    return jnp.concatenate(pieces, axis=0), [p.shape[0] for p in pieces]


def _unpack(packed, rows, shapes, lead=()):
    out, at = [], 0
    for r, shp in zip(rows, shapes):
        n = int(np.prod(shp))
        piece = packed[..., at:at + r, :].reshape(lead + (r * 128,))[..., :n]
        out.append(piece.reshape(lead + tuple(shp)))
        at += r
    return out


def _rel_bias_tile(table):
    h = table.shape[0]
    lo = REL_CLIP - (CHUNK - 1)
    n_far = LEFT_CHUNKS * CHUNK + CHUNK - 1 - REL_CLIP
    vec = jnp.concatenate([table[:, lo:2 * REL_CLIP], jnp.repeat(table[:, 2 * REL_CLIP:], n_far + 1, axis=1)], axis=1)
    rev = vec[:, ::-1]
    n_vec = BAND + CHUNK - 1
    skew = jnp.tile(rev, (1, CHUNK + 1))[:, :CHUNK * (n_vec + 1)].reshape(h, CHUNK, n_vec + 1)
    bias = skew[:, ::-1, :BAND]
    neg = jnp.full((h, CHUNK, CHUNK), NEG, F32)
    two = jnp.concatenate([jnp.concatenate([bias, neg], axis=2), jnp.concatenate([neg, bias], axis=2)], axis=1)
    return two.transpose(0, 2, 1)


def _rel_bias_tile_grad(dbias_t):
    h = dbias_t.shape[0]
    two = dbias_t.transpose(0, 2, 1)
    dbias = two[:, :CHUNK, :BAND] + two[:, CHUNK:, CHUNK:]
    n_vec = BAND + CHUNK - 1
    dskew = jnp.pad(dbias[:, ::-1, :], ((0, 0), (0, 0), (0, n_vec + 1 - BAND))).reshape(h, CHUNK * (n_vec + 1))
    dskew = jnp.pad(dskew, ((0, 0), (0, (CHUNK + 1) * n_vec - CHUNK * (n_vec + 1))))
    drev = jnp.sum(dskew.reshape(h, CHUNK + 1, n_vec), axis=1)
    dvec = drev[:, ::-1]
    lo = REL_CLIP - (CHUNK - 1)
    n_near = 2 * REL_CLIP - lo
    return jnp.concatenate([jnp.zeros((h, lo), F32), dvec[:, :n_near],
                            jnp.sum(dvec[:, n_near:], axis=1, keepdims=True)], axis=1)


BIG = ("w_in", "w_br_fox", "w_br_chunk", "w_out", "w_up", "w_down")
SMALL = ("b_f", "rel_bias", "conv_b", "b_ada", "ln1_g", "ln1_b", "ln2_g", "ln2_b")
ORDER = ("w_in", "b_f", "rel_bias", "w_br_fox", "w_br_chunk", "w_out", "w_up", "conv_w", "conv_b", "w_down",
         "w_ada", "b_ada", "ln1_g", "ln1_b", "ln2_g", "ln2_b")


def _dest_major(g, axis):
    shp = g.shape
    g = g.reshape(shp[:axis] + (N_DEV, shp[axis] // N_DEV) + shp[axis + 1:])
    return jnp.moveaxis(g, axis, 0)


EARLY = ("w_down", "w_up")
LATE = ("w_in", "w_br_fox", "w_br_chunk", "w_out")


def _shard_axis(n):
    return 0 if n in ("w_out", "w_down") else 1


def _slab(g, n):
    return _pad_lanes(_dest_major(g, _shard_axis(n)).astype(BF16))


def _layer_fwd(x, h1, wl, mod, next_mod, alpha, gather=None):
    sh1, sc1, g1, sh2, sc2, g2 = mod
    s = x.shape[0]
    qkv = _mm(h1, wl["w_qkv"], out_dtype=BF16, name="proj_qkv", tk=1024)
    gates = _mm(h1, wl["w_gates"], out_dtype=F32, name="proj_gates", tk=1024)
    f_t = _mm(wl["w_f_t"], h1, out_dtype=F32, name="proj_f", tb=True, tk=1024)[:N_HEADS]
    f_t = f_t.reshape(N_HEADS, s // 128, 128)
    b_f = jnp.broadcast_to(wl["b_f"].reshape(N_HEADS, 1, 1), (N_HEADS, 1, 128))
    cum = _forget_fwd(f_t, b_f, "forget_fwd").reshape(N_SPLIT, N_HEADS, s)
    cum_cols = cum.transpose(2, 1, 0).astype(BF16)

    ones = jnp.ones((s, N_HEADS, N_SPLIT), BF16)
    zeros = jnp.zeros((s, N_HEADS, AUG - HEAD_DIM - 2 * N_SPLIT), BF16)
    q_a = qkv[:, :WIDTH].reshape(s, N_HEADS, HEAD_DIM) * QK_SCALE
    k_a = qkv[:, WIDTH:2 * WIDTH].reshape(s, N_HEADS, HEAD_DIM)
    q_aug = jnp.concatenate([q_a, cum_cols, ones, zeros], axis=-1).reshape(s, N_HEADS * AUG)
    k_aug = jnp.concatenate([k_a, ones, -cum_cols, zeros], axis=-1).reshape(s, N_HEADS * AUG)
    o_a_t, lse_a, *gathered = _fox_fwd(q_aug, k_aug, qkv, "fox_fwd_gather" if gather else "fox_fwd",
                                       exchange=(False, gather) if gather else None)
    bias_t = _rel_bias_tile(wl["rel_bias"])
    o_c_t, lse_c = _chunk_fwd(qkv, bias_t, "chunk_fwd")
    o_a = o_a_t.transpose(2, 0, 1).reshape(s, WIDTH).astype(BF16)
    o_c = o_c_t.transpose(2, 0, 1).reshape(s, WIDTH).astype(BF16)

    merged = _merge_fwd(o_a, o_c, wl["w_br_fox"], wl["w_br_chunk"], gates, "merge_fwd")
    mix, x1, h2 = _mm_res_ln(merged, wl["w_out"], x, g1, wl["ln1_g"], wl["ln1_b"], sc2, sh2, alpha, "out_ln1")
    u0 = _mm(h2, wl["w_up"], out_dtype=F32, name="ffn_up", tk=1024, tn=1408, o_halves=True)
    act = _conv_act_fwd(u0, wl["conv_w"], wl["conv_b"], "conv_act_fwd")
    y2, x2, h_next = _mm_res_ln(act, wl["w_down"], x1, g2, wl["ln2_g"], wl["ln2_b"], next_mod[0], next_mod[1],
                                alpha, "down_ln2", tk=1408)
    saved = dict(x=x, h1=h1, gates=gates, f_t=f_t, b_f=b_f, q_aug=q_aug, k_aug=k_aug, qkv=qkv,
                 bias_t=bias_t, o_a_t=o_a_t, o_c_t=o_c_t, lse_a=lse_a, lse_c=lse_c,
                 o_a=o_a, o_c=o_c, merged=merged, mix=mix, x1=x1, h2=h2, u0=u0, act=act, y2=y2)
    return x2, h_next, saved, gathered


def _layer_bwd(dx2, sv, wl, mod, alpha, send):
    sh1, sc1, g1, sh2, sc2, g2 = mod
    s = dx2.shape[0]
    g = {}
    dres2, dy2, sums = _ln_res_bwd(dx2, sv["x1"], sv["y2"], g2, wl["ln2_g"], alpha, "ln2_bwd")
    g["ln2_g"], g["ln2_b"], dg2 = sums[0], sums[1], sums[2]
    dact = _mm(dy2, wl["w_down_t"], out_dtype=F32, name="dact", tk=1024, tn=1408)
    g["w_down"] = _mm(sv["act"], dy2, out_dtype=F32, name="g_w_down", ta=True, tm=1408, tk=1024)
    du0, csum = _conv_act_bwd(dact, sv["u0"], wl["conv_w"], wl["conv_b"], "conv_act_bwd")
    g["conv_w"] = jnp.concatenate([csum[0, 0:3], csum[1, 0:3]], axis=1)
    g["conv_b"] = jnp.concatenate([csum[0, 3], csum[1, 3]])
    dh2 = _mm(du0, wl["w_up_t"], out_dtype=F32, name="dh2", tk=1408, a_halves=True)
    g["w_up"] = _mm(sv["h2"], du0, out_dtype=F32, name="g_w_up", ta=True, tk=1024, tn=1408, b_halves=True)
    dx1, sums = _ln_mod_bwd(dh2, sv["x1"], sc2, dres2, "ln_mod2_bwd")
    dsh2, dsc2 = sums[0], sums[1]

    dres1, dmix, sums = _ln_res_bwd(dx1, sv["x"], sv["mix"], g1, wl["ln1_g"], alpha, "ln1_bwd")
    g["ln1_g"], g["ln1_b"], dg1 = sums[0], sums[1], sums[2]
    g["w_out"] = _mm(sv["merged"], dmix, out_dtype=F32, name="g_w_out", ta=True, tk=1024)
    dba, dbc, do_a, do_c, dgates = _merge_bwd(
        dmix, wl["w_out_t"], sv["o_a"], sv["o_c"], wl["w_br_fox"], wl["w_br_chunk"],
        wl["w_br_fox_t"], wl["w_br_chunk_t"], sv["gates"], "merge_bwd")
    g["w_br_fox"] = _mm(sv["o_a"], dba, out_dtype=F32, name="g_w_br_fox", ta=True, tk=1024)
    g["w_br_chunk"] = _mm(sv["o_c"], dbc, out_dtype=F32, name="g_w_br_chunk", ta=True, tk=1024)
    n_t = s // min(TQ, s)
    tiles_t = lambda a: a.reshape(n_t, s // n_t, N_HEADS, HEAD_DIM).transpose(2, 0, 3, 1)
    tokens = lambda a_t: a_t.transpose(2, 0, 1).reshape(s, WIDTH)

    do_a_t = tiles_t(do_a)
    delta_a = _row_dot(do_a_t, sv["o_a_t"], "delta_fox")
    q_a_t = tiles_t(sv["q_aug"].reshape(s, N_HEADS, AUG)[:, :, :HEAD_DIM].reshape(s, WIDTH))
    send = list(send) + [_slab(g[n], n) for n in EARLY]
    dk_a_t, dv_a_t, dq_a_t, dsum, dqsum, *landed = _fox_bwd(
        sv["q_aug"], sv["k_aug"], sv["qkv"], do_a, q_a_t, do_a_t, sv["lse_a"], delta_a,
        "fox_bwd_scatter_%d" % len(send), exchange=(True, send))
    dk_a, dv_a = tokens(dk_a_t), tokens(dv_a_t)
    dq_a = (dq_a_t * QK_SCALE).transpose(1, 3, 0, 2).reshape(s, WIDTH).astype(BF16)
    d_cum = dqsum.reshape(N_HEADS, s) - jnp.sum(dsum.reshape(s, N_HEADS, 128), axis=-1).T
    df_t, db_f = _forget_bwd(d_cum.reshape(N_HEADS, s // 128, 128), sv["f_t"], sv["b_f"], "forget_bwd")
    g["b_f"] = db_f[:, 0, 0]
    df_t = df_t.reshape(N_HEADS, s)

    delta_c = _row_dot(tiles_t(do_c), sv["o_c_t"], "delta_chunk")
    dq_c_t, dk_c, dv_c, dbias_t = _chunk_bwd(sv["qkv"], do_c, sv["lse_c"], delta_c, sv["bias_t"], "chunk_bwd")
    g["rel_bias"] = _rel_bias_tile_grad(dbias_t)
    dq_c = tokens(dq_c_t).astype(BF16)

    dqkv = jnp.concatenate([dq_a, dk_a, dv_a, dq_c, dk_c, dv_c], axis=1)
    df_pad = jnp.zeros((16 - N_HEADS, s), F32)
    df16 = jnp.concatenate([df_t, df_pad], axis=0).astype(BF16)
    df_cols = jnp.concatenate([df16.T, jnp.zeros((s, AUG - 16), BF16)], axis=1)

    dh1 = _mm(dqkv, wl["w_qkv_t"], out_dtype=F32, name="dh1_qkv", tk=1024)
    dh1g = _mm(dgates, wl["w_gates_t"], out_dtype=F32, name="dh1_gates", tk=1024)
    g_qkv = _mm(sv["h1"], dqkv, out_dtype=F32, name="g_w_qkv", ta=True, tk=1024)
    g_gates = _mm(sv["h1"], dgates, out_dtype=F32, name="g_w_gates", ta=True, tk=1024)
    g_f_t = _mm(df16, sv["h1"], out_dtype=F32, name="g_w_f", tk=1024)[:N_HEADS]
    g["w_in"] = jnp.concatenate([g_qkv[:, :3 * WIDTH], g_f_t.T, g_qkv[:, 3 * WIDTH:], g_gates], axis=1)
    dx, sums = _ln_mod_bwd(dh1, sv["x"], sc1, dres1, "ln_mod1_bwd", extra=(dh1g, df_cols, wl["w_f_pad"]))
    dsh1, dsc1 = sums[0], sums[1]
    g["mod"] = jnp.concatenate([dsh1, dsc1, dg1, dsh2, dsc2, dg2])
    return dx, g, landed


def kernel(x, c, w_in, b_f, rel_bias, w_br_fox, w_br_chunk, w_out, w_up, conv_w, conv_b, w_down, w_ada, b_ada, ln1_g, ln1_b, ln2_g, ln2_b, loss_target, m_w_in, m_b_f, m_rel_bias, m_w_br_fox, m_w_br_chunk, m_w_out, m_w_up, m_conv_w, m_conv_b, m_w_down, m_w_ada, m_b_ada, m_ln1_g, m_ln1_b, m_ln2_g, m_ln2_b, v_w_in, v_b_f, v_rel_bias, v_w_br_fox, v_w_br_chunk, v_w_out, v_w_up, v_conv_w, v_conv_b, v_w_down, v_w_ada, v_b_ada, v_ln1_g, v_ln1_b, v_ln2_g, v_ln2_b):
    w = dict(w_in=w_in, b_f=b_f, rel_bias=rel_bias, w_br_fox=w_br_fox, w_br_chunk=w_br_chunk, w_out=w_out,
             w_up=w_up, conv_w=conv_w, conv_b=conv_b, w_down=w_down, w_ada=w_ada, b_ada=b_ada,
             ln1_g=ln1_g, ln1_b=ln1_b, ln2_g=ln2_g, ln2_b=ln2_b)
    m = dict(w_in=m_w_in, b_f=m_b_f, rel_bias=m_rel_bias, w_br_fox=m_w_br_fox, w_br_chunk=m_w_br_chunk,
             w_out=m_w_out, w_up=m_w_up, conv_w=m_conv_w, conv_b=m_conv_b, w_down=m_w_down, w_ada=m_w_ada,
             b_ada=m_b_ada, ln1_g=m_ln1_g, ln1_b=m_ln1_b, ln2_g=m_ln2_g, ln2_b=m_ln2_b)
    v = dict(w_in=v_w_in, b_f=v_b_f, rel_bias=v_rel_bias, w_br_fox=v_w_br_fox, w_br_chunk=v_w_br_chunk,
             w_out=v_w_out, w_up=v_w_up, conv_w=v_conv_w, conv_b=v_conv_b, w_down=v_w_down, w_ada=v_w_ada,
             b_ada=v_b_ada, ln1_g=v_ln1_g, ln1_b=v_ln1_b, ln2_g=v_ln2_g, ln2_b=v_ln2_b)
    depth, d, _ = w_in.shape
    s = x.shape[1]
    alpha = (2.0 * depth) ** 0.25
    me = 4 * lax.axis_index("x") + 2 * lax.axis_index("y") + lax.axis_index("c")
    x0 = x.reshape(s, d)
    target = loss_target.reshape(s, d)

    small_in, small_rows = _pack([c, conv_w], mult=8)
    small_all = _all_gather([small_in], "gather_c_conv")[0]
    c_all, conv_w_all = _unpack(small_all, small_rows, [c.shape, conv_w.shape], lead=(N_DEV,))
    c_all = c_all.reshape(N_DEV, d)
    conv_w_full = conv_w_all.transpose(1, 2, 0, 3).reshape(depth, conv_w.shape[1], -1)
    n_ada = w_ada.shape[2]
    b_ada_mine = lax.dynamic_slice_in_dim(b_ada, me * n_ada, n_ada, axis=1).reshape(depth, 1, n_ada)
    mod_part, cond_all = _mod_part(c_all, w_ada, b_ada_mine, "mod_part")
    mod_all = _all_gather([mod_part.reshape(depth * N_DEV, n_ada)], "gather_mod")[0]
    mod_all = mod_all.reshape(N_DEV, depth, N_DEV, n_ada)
    mod_mine = lax.dynamic_index_in_dim(mod_all, me, axis=2, keepdims=False)
    mod_mine = mod_mine.transpose(1, 0, 2).reshape(depth, N_MOD, 1, d)

    shards = lambda l: [_pad_lanes(w[n][l].astype(BF16)) for n in BIG]
    cols = np.cumsum([0, WIDTH, WIDTH, WIDTH, N_HEADS, WIDTH, WIDTH, WIDTH, d, d])

    def layer_weights(l, parts):
        full = {}
        for n, p in zip(BIG, parts):
            rows, ncol = w[n].shape[1:]
            p = jnp.moveaxis(p[..., :ncol], 0, _shard_axis(n))
            full[n] = p.reshape((rows * N_DEV, ncol) if _shard_axis(n) == 0 else (rows, ncol * N_DEV))
        wi = full["w_in"]
        w_qkv = jnp.concatenate([wi[:, :cols[3]], wi[:, cols[4]:cols[7]]], axis=1)
        w_gates = wi[:, cols[7]:]
        w_f_t = jnp.concatenate([wi[:, cols[3]:cols[4]].T, jnp.zeros((16 - N_HEADS, d), BF16)], axis=0)
        w_f_pad = jnp.concatenate([w_f_t, jnp.zeros((AUG - 16, d), BF16)], axis=0)
        row = lambda a: a[l].reshape(1, -1)
        return dict(
            w_qkv=w_qkv, w_gates=w_gates, w_f_t=w_f_t, w_f_pad=w_f_pad, w_qkv_t=w_qkv.T, w_gates_t=w_gates.T,
            w_br_fox=full["w_br_fox"], w_br_chunk=full["w_br_chunk"],
            w_br_fox_t=full["w_br_fox"].T, w_br_chunk_t=full["w_br_chunk"].T,
            w_out=full["w_out"], w_out_t=full["w_out"].T, w_up=full["w_up"], w_up_t=full["w_up"].T,
            w_down=full["w_down"], w_down_t=full["w_down"].T,
            conv_w=conv_w_full[l].reshape(3, 2, -1).transpose(1, 0, 2), conv_b=conv_b[l].reshape(2, 1, -1),
            b_f=b_f[l], rel_bias=rel_bias[l],
            ln1_g=row(ln1_g), ln1_b=row(ln1_b), ln2_g=row(ln2_g), ln2_b=row(ln2_b))

    xs, saved, layers = x0, [], []
    hs = _ln_mod(x0, mod_mine[0][1], mod_mine[0][0], "ln_mod1")
    parts = _all_gather(shards(0), "gather_weights")
    for l in range(depth):
        layers.append(layer_weights(l, parts))
        nxt = mod_mine[min(l + 1, depth - 1)]
        xs, hs, sv, parts = _layer_fwd(xs, hs, layers[l], list(mod_mine[l]), (nxt[1], nxt[0]), alpha,
                                       gather=shards(l + 1) if l + 1 < depth else None)
        saved.append(sv)
    dxs, loss_part = _loss_grad(xs, target, "loss_grad")
    loss = lax.psum(loss_part[0, 0], AXES)
    grads, big_parts, send = [None] * depth, {}, []
    for l in reversed(range(depth)):
        dxs, grads[l], landed = _layer_bwd(dxs, saved[l], layers[l], list(mod_mine[l]), alpha, send)
        for n, part in zip([(n, l + 1) for n in LATE if send] + [(n, l) for n in EARLY], landed):
            big_parts[n] = part
        send = [_slab(grads[l][n], n) for n in LATE]
    for n, part in zip(LATE, _all_to_all(send, "scatter_grads")):
        big_parts[n, 0] = part
    grad_x = dxs.reshape(1, s, d)
    stack = lambda n: jnp.stack([grads[l][n] for l in range(depth)])

    rep = ["b_ada"] + [n for n in SMALL if n != "b_ada"]
    small_g = [stack("mod")] + [stack(n) for n in rep[1:]] + [stack("conv_w")]
    small_pack, small_g_rows = _pack(small_g, mult=8)
    small_parts = _all_gather([small_pack], "gather_small_grads")[0]
    n_rep_rows = sum(small_g_rows[:-1])
    dmod_all = _unpack(small_parts, small_g_rows[:1], [(depth, N_MOD * d)], lead=(N_DEV,))[0]
    cw_all = _unpack(small_parts[:, n_rep_rows:], small_g_rows[-1:], [small_g[-1].shape], lead=(N_DEV,))[0]

    out = {}
    for n in BIG:
        rows2d = lambda a: a.reshape(-1, a.shape[-1])
        res = _adamw([big_parts[n, l] for l in range(depth)], rows2d(w[n]), rows2d(m[n]), rows2d(v[n]),
                     "adamw_" + n, tr=256)
        for kind, r in zip(("grad", "delta", "new_m", "new_v"), res):
            out[kind, n] = r.reshape(w[n].shape)

    dmod_mine = lax.dynamic_slice_in_dim(dmod_all, me * n_ada, n_ada, axis=2)
    g_ada = jnp.stack([_mm(cond_all.T, dmod_mine[:, l], out_dtype=F32, name="g_w_ada", tk=N_DEV, tn=n_ada,
                           precision=HI) for l in range(depth)])
    lp = [_rows128(t["w_ada"], 8) for t in (w, m, v)]
    res = _adamw([_rows128(g_ada, 8)[None]], *lp, "adamw_ada")
    for kind, r in zip(("grad", "delta", "new_m", "new_v"), res):
        out[kind, "w_ada"] = _unpack(r, [lp[0].shape[0]], [w_ada.shape])[0]

    rp = [_pack([t[n] for n in rep], mult=8)[0] for t in (w, m, v)]
    res = _adamw([small_parts[:, :n_rep_rows]], *rp, "adamw_small")
    for kind, r in zip(("grad", "delta", "new_m", "new_v"), res):
        for n, a in zip(rep, _unpack(r, small_g_rows[:-1], [w[n].shape for n in rep])):
            out[kind, n] = a

    n_cw = conv_w.shape[2]
    cw_mine = lax.dynamic_slice_in_dim(cw_all, me * n_cw, n_cw, axis=3)
    cp = [_rows128(t["conv_w"], 8) for t in (w, m, v)]
    cw_in = jnp.stack([_rows128(cw_mine[i], 8) for i in range(N_DEV)])
    res = _adamw([cw_in], *cp, "adamw_conv_w")
    for kind, r in zip(("grad", "delta", "new_m", "new_v"), res):
        out[kind, "conv_w"] = _unpack(r, [cp[0].shape[0]], [conv_w.shape])[0]

    result = [loss, grad_x]
    for kind in ("grad", "delta", "new_m", "new_v"):
        result += [out[kind, n] for n in ORDER]
    return tuple(result)
```

```python
import functools
import math

import jax
import jax.numpy as jnp
import numpy as np
from jax import lax
from jax.experimental import pallas as pl
from jax.experimental.pallas import tpu as pltpu

F32 = jnp.float32
BF16 = jnp.bfloat16
HI = lax.Precision.HIGHEST
MESH_ID = pl.DeviceIdType.MESH
AXES = ("x", "y", "c")
N_DEV = 8

N_HEADS = 8
HEAD_DIM = 64
WIDTH = N_HEADS * HEAD_DIM
CHUNK = 64
LEFT_CHUNKS = 8
BAND = (LEFT_CHUNKS + 1) * CHUNK
REL_CLIP = 128
LN_EPS = 1e-5
N_MOD = 6
QK_SCALE = 1.0 / math.sqrt(HEAD_DIM)
NEG = -1e30
AUG = 128
N_SPLIT = 3
TQ = 512
SUB = 2 * CHUNK
SUBK = BAND + CHUNK

ADAM_LR, ADAM_B1, ADAM_B2, ADAM_EPS, ADAM_WD, ADAM_STEP = 0.001, 0.9, 0.999, 1e-08, 0.01, 10

VMEM_LIMIT = 56 * 2 ** 20


def _pcall(body, **kw):
    return pl.pallas_call(body, **kw)


def _cp(*sem):
    return pltpu.CompilerParams(dimension_semantics=sem if sem else None, vmem_limit_bytes=VMEM_LIMIT)


def _my_place():
    return lax.axis_index("x"), lax.axis_index("y"), lax.axis_index("c")


N_PEER = N_DEV - 1


def _comm_call(body, xs, out_shapes, name):
    n = len(xs)
    hbm = pl.BlockSpec(memory_space=pltpu.HBM)
    return _pcall(
        functools.partial(body, n), name=name, out_shape=out_shapes, in_specs=[hbm] * n, out_specs=[hbm] * n,
        scratch_shapes=[pltpu.SemaphoreType.DMA((N_PEER * n,)), pltpu.SemaphoreType.DMA((N_PEER * n,)),
                        pltpu.SemaphoreType.DMA((n,))],
    )(*xs)


def _all_gather(xs, name):
    def body(n, *refs):
        x_refs, out_refs, (send_sems, recv_sems, local_sems) = refs[:n], refs[n:2 * n], refs[2 * n:]
        mx, my, mc = _my_place()
        me, sibling = (mx, my, mc), (mx, my, 1 - mc)
        chips = [(1 - mx, my), (mx, 1 - my), (1 - mx, 1 - my)]

        def slot(a, px, py, pc):
            return out_refs[a].at[4 * px + 2 * py + pc]

        def copy(a, k, block, to, own=False):
            return pltpu.make_async_remote_copy(
                src_ref=x_refs[a] if own else slot(a, *block), dst_ref=slot(a, *block),
                send_sem=send_sems.at[a * N_PEER + k], recv_sem=recv_sems.at[a * N_PEER + k],
                device_id=to, device_id_type=MESH_ID)

        arrays = range(n)
        mine = [pltpu.make_async_copy(x_refs[a], slot(a, *me), local_sems.at[a]) for a in arrays]
        first = [copy(a, 0, me, sibling, own=True) for a in arrays]
        first += [copy(a, 1 + j, me, (*chip, mc), own=True) for a in arrays for j, chip in enumerate(chips)]
        for cp in mine + first:
            cp.start()
        passed = []
        for a in arrays:
            for j, chip in enumerate(chips):
                copy(a, 1 + j, (*chip, mc), me).wait_recv()
                passed.append(copy(a, 4 + j, (*chip, mc), sibling))
                passed[-1].start()
        for a in arrays:
            copy(a, 0, sibling, me).wait_recv()
            for j, chip in enumerate(chips):
                copy(a, 4 + j, (*chip, 1 - mc), me).wait_recv()
        for cp in first + passed:
            cp.wait_send()
        for cp in mine:
            cp.wait()

    return _comm_call(body, xs, [jax.ShapeDtypeStruct((N_DEV,) + x.shape, x.dtype) for x in xs], name)


def _direct_exchange(scatter, x_refs, out_refs, send_sems, recv_sems, local_sems):
    mx, my, mc = _my_place()
    me = 4 * mx + 2 * my + mc
    mine, copies = [], []
    for a, (x_ref, out_ref) in enumerate(zip(x_refs, out_refs)):
        mine.append(pltpu.make_async_copy(x_ref.at[me] if scatter else x_ref, out_ref.at[me], local_sems.at[a]))
        for k in range(N_PEER):
            px, py, pc = mx ^ ((k + 1) >> 2), my ^ (((k + 1) >> 1) & 1), mc ^ ((k + 1) & 1)
            copies.append(pltpu.make_async_remote_copy(
                src_ref=x_ref.at[4 * px + 2 * py + pc] if scatter else x_ref, dst_ref=out_ref.at[me],
                send_sem=send_sems.at[a * N_PEER + k], recv_sem=recv_sems.at[a * N_PEER + k],
                device_id=(px, py, pc), device_id_type=MESH_ID))

    def start():
        for cp in mine + copies:
            cp.start()

    def wait():
        for cp in copies:
            cp.wait_recv()
        for cp in copies:
            cp.wait_send()
        for cp in mine:
            cp.wait()

    return start, wait


def _exchange_shapes(scatter, xs):
    return [jax.ShapeDtypeStruct(x.shape if scatter else (N_DEV,) + x.shape, x.dtype) for x in xs]


def _exchange_scratch(n):
    return [pltpu.SemaphoreType.DMA((N_PEER * n,)), pltpu.SemaphoreType.DMA((N_PEER * n,)),
            pltpu.SemaphoreType.DMA((n,))]


def _all_to_all(xs, name):
    def body(n, *refs):
        start, wait = _direct_exchange(True, refs[:n], refs[n:2 * n], *refs[2 * n:])
        start()
        wait()

    return _comm_call(body, xs, _exchange_shapes(True, xs), name)


def _tile(n, pref, unit=128):
    if n <= pref:
        return n
    t = pref - pref % unit
    while t > unit and n % t:
        t -= unit
    assert n % t == 0, (n, pref, unit)
    return t


def _mm(a, b, *, out_dtype, name, ta=False, tb=False, tm=1024, tn=1024, tk=512, precision=None,
        a_halves=False, b_halves=False, o_halves=False):
    a_shape = (a.shape[1], 2 * a.shape[2]) if a_halves else a.shape
    b_shape = (b.shape[1], 2 * b.shape[2]) if b_halves else b.shape
    (k_dim, m) = a_shape if ta else a_shape[::-1]
    n = b_shape[0] if tb else b_shape[1]
    tm, tn = _tile(m, tm, 8), _tile(n // 2 if (b_halves or o_halves) else n, tn)
    tk = _tile(k_dim // 2 if a_halves else k_dim, tk)
    nk = k_dim // tk
    dims = (((0 if ta else 1,), (1 if tb else 0,)), ((), ()))

    def body(a_ref, b_ref, o_ref, acc_ref):
        k = pl.program_id(2)
        if nk == 1:
            o_ref[...] = lax.dot_general(a_ref[...], b_ref[...], dims, preferred_element_type=F32,
                                         precision=precision).astype(o_ref.dtype)
            return

        @pl.when(k == 0)
        def _():
            acc_ref[...] = jnp.zeros_like(acc_ref)

        acc_ref[...] += lax.dot_general(a_ref[...], b_ref[...], dims, preferred_element_type=F32,
                                        precision=precision)

        @pl.when(k == nk - 1)
        def _():
            o_ref[...] = acc_ref[...].astype(o_ref.dtype)

    def spec(shape2, pick, halves, per_half):
        if not halves:
            return pl.BlockSpec(shape2, pick)

        def index(i, j, k):
            r, c = pick(i, j, k)
            return (c // per_half, r, c % per_half)
        return pl.BlockSpec((None,) + shape2, index)

    assert not (ta and a_halves) and not (tb and b_halves)
    a_spec = spec((tk, tm), lambda i, j, k: (k, i), False, 0) if ta else \
        spec((tm, tk), lambda i, j, k: (i, k), a_halves, (k_dim // 2) // tk if a_halves else 0)
    b_spec = spec((tn, tk), lambda i, j, k: (j, k), False, 0) if tb else \
        spec((tk, tn), lambda i, j, k: (k, j), b_halves, (n // 2) // tn if b_halves else 0)
    o_spec = spec((tm, tn), lambda i, j, k: (i, j), o_halves, (n // 2) // tn if o_halves else 0)
    out_shape = (2, m, n // 2) if o_halves else (m, n)
    return _pcall(
        body, name=name,
        out_shape=jax.ShapeDtypeStruct(out_shape, out_dtype),
        grid=(m // tm, n // tn, nk),
        in_specs=[a_spec, b_spec],
        out_specs=o_spec,
        scratch_shapes=[pltpu.VMEM((tm, tn), F32)],
        compiler_params=_cp("parallel", "parallel", "arbitrary"),
    )(a, b)


def _ln_stats(x):
    mu = jnp.mean(x, axis=-1, keepdims=True)
    xc = x - mu
    var = jnp.mean(xc * xc, axis=-1, keepdims=True)
    rstd = lax.rsqrt(var + LN_EPS)
    return xc * rstd, rstd


def _ln_mod(x, sc, sh, name, tm=512):
    s, d = x.shape
    tm = min(tm, s)

    def body(x_ref, sc_ref, sh_ref, h_ref):
        xhat, _ = _ln_stats(x_ref[...])
        h_ref[...] = (xhat * (1.0 + sc_ref[...]) + sh_ref[...]).astype(h_ref.dtype)

    row = pl.BlockSpec((1, d), lambda i: (0, 0))
    return _pcall(
        body, name=name, out_shape=jax.ShapeDtypeStruct((s, d), BF16), grid=(s // tm,),
        in_specs=[pl.BlockSpec((tm, d), lambda i: (i, 0)), row, row],
        out_specs=pl.BlockSpec((tm, d), lambda i: (i, 0)),
        compiler_params=_cp("parallel"),
    )(x, sc, sh)


def _mm_res_ln(a, w, xres, gate, ln_g, ln_b, sc, sh, alpha, name, tm=512, tk=512):
    s, k_dim = a.shape
    d = w.shape[1]
    tm, tk = _tile(s, tm, 8), _tile(k_dim, tk)
    nk = k_dim // tk

    def body(a_ref, w_ref, x_ref, g_ref, lg_ref, lb_ref, sc_ref, sh_ref, y_ref, xn_ref, h_ref, acc_ref):
        k = pl.program_id(1)

        @pl.when(k == 0)
        def _():
            acc_ref[...] = jnp.zeros_like(acc_ref)

        acc_ref[...] += jnp.dot(a_ref[...], w_ref[...], preferred_element_type=F32)

        @pl.when(k == nk - 1)
        def _():
            y = acc_ref[...]
            y_ref[...] = y
            z = alpha * x_ref[...] + (1.0 + g_ref[...]) * y
            zhat, _ = _ln_stats(z)
            xn = zhat * lg_ref[...] + lb_ref[...]
            xn_ref[...] = xn
            xhat, _ = _ln_stats(xn)
            h_ref[...] = (xhat * (1.0 + sc_ref[...]) + sh_ref[...]).astype(h_ref.dtype)

    row = pl.BlockSpec((1, d), lambda i, k: (0, 0))
    tile = pl.BlockSpec((tm, d), lambda i, k: (i, 0))
    return _pcall(
        body, name=name,
        out_shape=(jax.ShapeDtypeStruct((s, d), F32), jax.ShapeDtypeStruct((s, d), F32),
                   jax.ShapeDtypeStruct((s, d), BF16)),
        grid=(s // tm, nk),
        in_specs=[pl.BlockSpec((tm, tk), lambda i, k: (i, k)), pl.BlockSpec((tk, d), lambda i, k: (k, 0)),
                  tile, row, row, row, row, row],
        out_specs=(tile, tile, tile),
        scratch_shapes=[pltpu.VMEM((tm, d), F32)],
        compiler_params=_cp("parallel", "arbitrary"),
    )(a, w, xres, gate, ln_g, ln_b, sc, sh)


def _colsum(v):
    return jnp.sum(v, axis=0, keepdims=True)


def _ln_res_bwd(dxn, xres, y, gate, ln_g, alpha, name, tm=256):
    s, d = dxn.shape
    tm = min(tm, s)

    def body(dxn_ref, x_ref, y_ref, g_ref, lg_ref, dres_ref, dy_ref, sums_ref):
        @pl.when(pl.program_id(0) == 0)
        def _():
            sums_ref[...] = jnp.zeros_like(sums_ref)

        dxn_v, y_v = dxn_ref[...], y_ref[...]
        one_g = 1.0 + g_ref[...]
        zhat, rstd = _ln_stats(alpha * x_ref[...] + one_g * y_v)
        dzh = dxn_v * lg_ref[...]
        dz = rstd * (dzh - jnp.mean(dzh, axis=-1, keepdims=True)
                     - zhat * jnp.mean(dzh * zhat, axis=-1, keepdims=True))
        dres_ref[...] = alpha * dz
        dy_ref[...] = (one_g * dz).astype(dy_ref.dtype)
        sums_ref[0:1, :] += _colsum(dxn_v * zhat)
        sums_ref[1:2, :] += _colsum(dxn_v)
        sums_ref[2:3, :] += _colsum(dz * y_v)

    row = pl.BlockSpec((1, d), lambda i: (0, 0))
    tile = pl.BlockSpec((tm, d), lambda i: (i, 0))
    return _pcall(
        body, name=name,
        out_shape=(jax.ShapeDtypeStruct((s, d), F32), jax.ShapeDtypeStruct((s, d), BF16),
                   jax.ShapeDtypeStruct((8, d), F32)),
        grid=(s // tm,),
        in_specs=[tile, tile, tile, row, row],
        out_specs=(tile, tile, pl.BlockSpec((8, d), lambda i: (0, 0))),
        compiler_params=_cp("arbitrary"),
    )(dxn, xres, y, gate, ln_g)


def _ln_mod_bwd(dh, x, sc, dres, name, extra=None, tm=256):
    s, d = dh.shape
    tm = min(tm, s)

    def body(*refs):
        if extra is None:
            dh_ref, x_ref, sc_ref, dres_ref, dx_ref, sums_ref = refs
            dh_v = dh_ref[...]
        else:
            dh_ref, x_ref, sc_ref, dres_ref, dh2_ref, df_ref, wf_ref, dx_ref, sums_ref = refs
            dh_v = dh_ref[...] + dh2_ref[...] + jnp.dot(df_ref[...], wf_ref[...], preferred_element_type=F32)

        @pl.when(pl.program_id(0) == 0)
        def _():
            sums_ref[...] = jnp.zeros_like(sums_ref)

        xhat, rstd = _ln_stats(x_ref[...])
        dxh = dh_v * (1.0 + sc_ref[...])
        dx_ref[...] = dres_ref[...] + rstd * (dxh - jnp.mean(dxh, axis=-1, keepdims=True)
                                             - xhat * jnp.mean(dxh * xhat, axis=-1, keepdims=True))
        sums_ref[0:1, :] += _colsum(dh_v)
        sums_ref[1:2, :] += _colsum(dh_v * xhat)

    row = pl.BlockSpec((1, d), lambda i: (0, 0))
    tile = pl.BlockSpec((tm, d), lambda i: (i, 0))
    in_specs = [tile, tile, row, tile]
    args = [dh, x, sc, dres]
    if extra is not None:
        in_specs += [tile, pl.BlockSpec((tm, AUG), lambda i: (i, 0)), pl.BlockSpec((AUG, d), lambda i: (0, 0))]
        args += list(extra)
    return _pcall(
        body, name=name,
        out_shape=(jax.ShapeDtypeStruct((s, d), F32), jax.ShapeDtypeStruct((8, d), F32)),
        grid=(s // tm,), in_specs=in_specs,
        out_specs=(tile, pl.BlockSpec((8, d), lambda i: (0, 0))),
        compiler_params=_cp("arbitrary"),
    )(*args)


def _loss_grad(xn, target, name, tm=512):
    s, d = xn.shape
    tm = min(tm, s)
    n = s // tm

    def body(x_ref, t_ref, dx_ref, loss_ref, acc_ref):
        i = pl.program_id(0)

        @pl.when(i == 0)
        def _():
            acc_ref[...] = jnp.zeros_like(acc_ref)

        err = x_ref[...] - t_ref[...]
        dx_ref[...] = err * (1.0 / d)
        acc_ref[...] += _colsum(err * err)

        @pl.when(i == n - 1)
        def _():
            loss_ref[...] = jnp.zeros_like(loss_ref) + (0.5 / d) * jnp.sum(acc_ref[...])

    tile = pl.BlockSpec((tm, d), lambda i: (i, 0))
    return _pcall(
        body, name=name,
        out_shape=(jax.ShapeDtypeStruct((s, d), F32), jax.ShapeDtypeStruct((8, 128), F32)),
        grid=(n,), in_specs=[tile, tile],
        out_specs=(tile, pl.BlockSpec((8, 128), lambda i: (0, 0))),
        scratch_shapes=[pltpu.VMEM((1, d), F32)],
        compiler_params=_cp("arbitrary"),
    )(xn, target)


def _sigmoid(v):
    return 0.5 * jnp.tanh(0.5 * v) + 0.5


def _merge_fwd(o_a, o_c, w_a, w_c, gates, name, tm=256):
    s, wd = o_a.shape
    d = w_a.shape[1]
    tm = min(tm, s)

    def body(oa_ref, oc_ref, wa_ref, wc_ref, ga_ref, gc_ref, m_ref):
        ba = jnp.dot(oa_ref[...], wa_ref[...], preferred_element_type=F32)
        bc = jnp.dot(oc_ref[...], wc_ref[...], preferred_element_type=F32)
        m_ref[...] = (_sigmoid(ga_ref[...]) * ba + _sigmoid(gc_ref[...]) * bc).astype(m_ref.dtype)

    o_spec = pl.BlockSpec((tm, wd), lambda i: (i, 0))
    w_spec = pl.BlockSpec((wd, d), lambda i: (0, 0))
    return _pcall(
        body, name=name, out_shape=jax.ShapeDtypeStruct((s, d), BF16), grid=(s // tm,),
        in_specs=[o_spec, o_spec, w_spec, w_spec,
                  pl.BlockSpec((tm, d), lambda i: (i, 0)), pl.BlockSpec((tm, d), lambda i: (i, 1))],
        out_specs=pl.BlockSpec((tm, d), lambda i: (i, 0)),
        compiler_params=_cp("parallel"),
    )(o_a, o_c, w_a, w_c, gates, gates)


def _merge_bwd(dmix, w_out_t, o_a, o_c, w_a, w_c, w_a_t, w_c_t, gates, name, tm=256):
    s, wd = o_a.shape
    d = w_a.shape[1]
    tm = min(tm, s)

    def body(dmix_ref, wot_ref, oa_ref, oc_ref, wa_ref, wc_ref, wat_ref, wct_ref, ga_ref, gc_ref,
             dba_ref, dbc_ref, doa_ref, doc_ref, dg_ref):
        dm = jnp.dot(dmix_ref[...], wot_ref[...], preferred_element_type=F32)
        for half, (o_ref, w_ref, wt_ref, g_ref, db_ref, do_ref) in enumerate((
                (oa_ref, wa_ref, wat_ref, ga_ref, dba_ref, doa_ref),
                (oc_ref, wc_ref, wct_ref, gc_ref, dbc_ref, doc_ref))):
            b = jnp.dot(o_ref[...], w_ref[...], preferred_element_type=F32)
            sg = _sigmoid(g_ref[...])
            db = (dm * sg).astype(BF16)
            db_ref[...] = db
            dg_ref[:, d * half:d * (half + 1)] = (dm * b * sg * (1.0 - sg)).astype(dg_ref.dtype)
            do_ref[...] = jnp.dot(db, wt_ref[...], preferred_element_type=F32).astype(do_ref.dtype)

    row_d = pl.BlockSpec((tm, d), lambda i: (i, 0))
    row_w = pl.BlockSpec((tm, wd), lambda i: (i, 0))
    full = lambda shp: pl.BlockSpec(shp, lambda i: (0, 0))
    return _pcall(
        body, name=name,
        out_shape=(jax.ShapeDtypeStruct((s, d), BF16), jax.ShapeDtypeStruct((s, d), BF16),
                   jax.ShapeDtypeStruct((s, wd), BF16), jax.ShapeDtypeStruct((s, wd), BF16),
                   jax.ShapeDtypeStruct((s, 2 * d), BF16)),
        grid=(s // tm,),
        in_specs=[row_d, full((d, d)), row_w, row_w, full((wd, d)), full((wd, d)), full((d, wd)), full((d, wd)),
                  row_d, pl.BlockSpec((tm, d), lambda i: (i, 1))],
        out_specs=(row_d, row_d, row_w, row_w, pl.BlockSpec((tm, 2 * d), lambda i: (i, 0))),
        compiler_params=_cp("parallel"),
    )(dmix, w_out_t, o_a, o_c, w_a, w_c, w_a_t, w_c_t, gates, gates)


def _shift_down(cur, k, fill_rows):
    out = pltpu.roll(cur, k, axis=0)
    rid = lax.broadcasted_iota(jnp.int32, cur.shape, 0)
    for r, fill in enumerate(fill_rows):
        out = jnp.where(rid == r, fill, out)
    return out


def _shift_up(cur, k, fill_rows):
    n = cur.shape[0]
    out = pltpu.roll(cur, n - k, axis=0)
    rid = lax.broadcasted_iota(jnp.int32, cur.shape, 0)
    for r, fill in enumerate(fill_rows):
        out = jnp.where(rid == n - k + r, fill, out)
    return out


def _conv_rows(cur, prev8, first, w, b):
    p6 = jnp.where(first, 0.0, prev8[6:7, :])
    p7 = jnp.where(first, 0.0, prev8[7:8, :])
    m1 = _shift_down(cur, 1, [p7])
    m2 = _shift_down(cur, 2, [p6, p7])
    u = b + w[0:1, :] * m2 + w[1:2, :] * m1 + w[2:3, :] * cur
    return u, m1, m2


def _conv_specs(tr, tc, order):
    r8 = tr // 8
    at = lambda f: (lambda *g: f(*order(*g)))
    return [pl.BlockSpec((2, tr, tc), at(lambda i, j: (0, i, j))),
            pl.BlockSpec((2, 8, tc), at(lambda i, j: (0, jnp.maximum(i * r8 - 1, 0), j))),
            pl.BlockSpec((2, 3, tc), at(lambda i, j: (0, 0, j))),
            pl.BlockSpec((2, 1, tc), at(lambda i, j: (0, 0, j)))]


def _conv_act_fwd(u0, conv_w, conv_b, name, tr=512, tc=256):
    _, s, f = u0.shape
    tr, tc = _tile(s, tr, 8), _tile(f, tc)

    def body(u_ref, p_ref, w_ref, b_ref, act_ref):
        first = pl.program_id(0) == 0
        a, _, _ = _conv_rows(u_ref[0], p_ref[0], first, w_ref[0], b_ref[0])
        v, _, _ = _conv_rows(u_ref[1], p_ref[1], first, w_ref[1], b_ref[1])
        act_ref[...] = (a * _sigmoid(a) * v).astype(act_ref.dtype)

    return _pcall(
        body, name=name, out_shape=jax.ShapeDtypeStruct((s, f), BF16), grid=(s // tr, f // tc),
        in_specs=_conv_specs(tr, tc, lambda i, j: (i, j)),
        out_specs=pl.BlockSpec((tr, tc), lambda i, j: (i, j)),
        compiler_params=_cp("parallel", "parallel"),
    )(u0, u0, conv_w, conv_b)


def _act_grads(dact, a, v):
    sg = _sigmoid(a)
    return dact * v * sg * (1.0 + a * (1.0 - sg)), dact * a * sg


def _conv_act_bwd(dact, u0, conv_w, conv_b, name, tr=512, tc=256):
    _, s, f = u0.shape
    tr, tc = _tile(s, tr, 8), _tile(f, tc)
    n = s // tr
    r8 = tr // 8

    def body(dact_ref, dnext_ref, u_ref, p_ref, unext_ref, w_ref, b_ref, du0_ref, sums_ref):
        i = pl.program_id(1)
        first, last = i == 0, i == n - 1

        @pl.when(first)
        def _():
            sums_ref[...] = jnp.zeros_like(sums_ref)

        cur = (u_ref[0], u_ref[1])
        (a, a1, a2), (v, v1, v2) = [_conv_rows(cur[hf], p_ref[hf], first, w_ref[hf], b_ref[hf]) for hf in range(2)]
        da, dv = _act_grads(dact_ref[...], a, v)
        (an, _, _), (vn, _, _) = [_conv_rows(unext_ref[hf], cur[hf][tr - 8:tr, :], False, w_ref[hf], b_ref[hf])
                                  for hf in range(2)]
        dan, dvn = _act_grads(dnext_ref[...], an, vn)
        for hf, g, gn, shifted in ((0, da, dan, (a2, a1, cur[0])), (1, dv, dvn, (v2, v1, cur[1]))):
            w = w_ref[hf]
            n0 = jnp.where(last, 0.0, gn[0:1, :])
            n1 = jnp.where(last, 0.0, gn[1:2, :])
            du0 = w[2:3, :] * g + w[1:2, :] * _shift_up(g, 1, [n0]) + w[0:1, :] * _shift_up(g, 2, [n0, n1])
            du0_ref[hf] = du0.astype(du0_ref.dtype)
            for r in range(3):
                sums_ref[hf, r:r + 1, :] += _colsum(g * shifted[r])
            sums_ref[hf, 3:4, :] += _colsum(g)

    nxt = lambda i: jnp.minimum((i + 1) * r8, s // 8 - 1)
    order = lambda j, i: (i, j)
    specs = _conv_specs(tr, tc, order)
    return _pcall(
        body, name=name,
        out_shape=(jax.ShapeDtypeStruct((2, s, f), BF16), jax.ShapeDtypeStruct((2, 8, f), F32)),
        grid=(f // tc, n),
        in_specs=[pl.BlockSpec((tr, tc), lambda j, i: (i, j)), pl.BlockSpec((8, tc), lambda j, i: (nxt(i), j)),
                  specs[0], specs[1], pl.BlockSpec((2, 8, tc), lambda j, i: (0, nxt(i), j)), specs[2], specs[3]],
        out_specs=(pl.BlockSpec((2, tr, tc), lambda j, i: (0, i, j)), pl.BlockSpec((2, 8, tc), lambda j, i: (0, 0, j))),
        compiler_params=_cp("parallel", "arbitrary"),
    )(dact, dact, u0, u0, u0, conv_w, conv_b)


def _cumsum_rows(v, reverse=False):
    r = v.shape[0]
    i128 = lax.broadcasted_iota(jnp.int32, (128, 128), 0), lax.broadcasted_iota(jnp.int32, (128, 128), 1)
    ir = lax.broadcasted_iota(jnp.int32, (r, r), 0), lax.broadcasted_iota(jnp.int32, (r, r), 1)
    in_row = (i128[0] >= i128[1] if reverse else i128[0] <= i128[1]).astype(F32)
    rows = (ir[1] > ir[0] if reverse else ir[1] < ir[0]).astype(F32)
    within = jnp.dot(v, in_row, preferred_element_type=F32, precision=HI)
    tot = jnp.broadcast_to(within[:, 0:1] if reverse else within[:, 127:128], (r, 128))
    return within + jnp.dot(rows, tot, preferred_element_type=F32, precision=HI)


def _forget_fwd(f_t, b_f, name):
    h, r, _ = f_t.shape

    def body(f_ref, b_ref, o_ref):
        z = f_ref[...] + b_ref[...]
        logf = jnp.minimum(z, 0.0) - jnp.log(1.0 + jnp.exp(-jnp.abs(z)))
        rest = _cumsum_rows(logf)
        for i in range(N_SPLIT):
            piece = rest.astype(BF16).astype(F32)
            o_ref[i] = piece
            rest = rest - piece

    return _pcall(
        body, name=name, out_shape=jax.ShapeDtypeStruct((N_SPLIT, h, r, 128), F32), grid=(h,),
        in_specs=[pl.BlockSpec((None, r, 128), lambda i: (i, 0, 0)), pl.BlockSpec((None, 1, 128), lambda i: (i, 0, 0))],
        out_specs=pl.BlockSpec((N_SPLIT, None, r, 128), lambda i: (0, i, 0, 0)),
        compiler_params=_cp("parallel"),
    )(f_t, b_f)


def _forget_bwd(d_cum, f_t, b_f, name):
    h, r, _ = f_t.shape

    def body(g_ref, f_ref, b_ref, df_ref, db_ref):
        df = _cumsum_rows(g_ref[...], reverse=True) * _sigmoid(-(f_ref[...] + b_ref[...]))
        df_ref[...] = df
        db_ref[...] = jnp.zeros_like(db_ref) + jnp.sum(df)

    blk = pl.BlockSpec((None, r, 128), lambda i: (i, 0, 0))
    one = pl.BlockSpec((None, 1, 128), lambda i: (i, 0, 0))
    return _pcall(
        body, name=name,
        out_shape=(jax.ShapeDtypeStruct((h, r, 128), F32), jax.ShapeDtypeStruct((h, 1, 128), F32)),
        grid=(h,), in_specs=[blk, blk, one], out_specs=(blk, one),
        compiler_params=_cp("parallel"),
    )(d_cum, f_t, b_f)


_NT = (((1,), (1,)), ((), ()))


def _causal_keep(tk, tq):
    return lax.broadcasted_iota(jnp.int32, (tk, tq), 0) <= lax.broadcasted_iota(jnp.int32, (tk, tq), 1)


_TN = (((0,), (0,)), ((), ()))
PAIR = 2
V_FOX = 2 * WIDTH // 128
Q_CHUNK, K_CHUNK, V_CHUNK = (3 * WIDTH) // 128, (4 * WIDTH) // 128, (5 * WIDTH) // 128


def _riding_exchange(exchange, n_in, n_out, grid):
    if exchange is None:
        return [], [], [], [], lambda refs: (refs, lambda: None)
    scatter, xs = exchange
    n = len(xs)
    hbm = pl.BlockSpec(memory_space=pltpu.HBM)

    def hook(refs):
        ins, x_refs = refs[:n_in], refs[n_in:n_in + n]
        outs, land_refs = refs[n_in + n:n_in + n + n_out], refs[n_in + n + n_out:n_in + 2 * n + n_out]
        rest = refs[n_in + 2 * n + n_out:]
        own_scratch, sems = rest[:len(rest) - 3], rest[len(rest) - 3:]
        start, wait = _direct_exchange(scatter, x_refs, land_refs, *sems)
        ids = [pl.program_id(a) for a in range(len(grid))]
        is_first = functools.reduce(jnp.logical_and, [i == 0 for i in ids])
        is_last = functools.reduce(jnp.logical_and, [i == g - 1 for i, g in zip(ids, grid)])
        pl.when(is_first)(start)
        return tuple(ins) + tuple(outs) + tuple(own_scratch), lambda: pl.when(is_last)(wait)

    return list(xs), _exchange_shapes(scatter, xs), [hbm] * n, _exchange_scratch(n), hook


def _fox_fwd(q_aug, k_aug, qkv, name, t=TQ, exchange=None):
    s = q_aug.shape[0]
    t = min(t, s)
    n = s // t
    dh = HEAD_DIM
    grid = (N_HEADS // PAIR, n)
    ex_in, ex_out, ex_specs, ex_scratch, hook = _riding_exchange(exchange, 3, 2, grid)

    def body(*refs):
        (q_ref, k_ref, v_ref, o_ref, lse_ref, m_s, l_s, acc_s), finish = hook(refs)
        qi = pl.program_id(1)
        m_s[...] = jnp.full_like(m_s, NEG)
        l_s[...] = jnp.zeros_like(l_s)
        acc_s[...] = jnp.zeros_like(acc_s)

        def step(kj, diag):
            rows = pl.ds(pl.multiple_of(kj * t, t), t)
            for hh in range(PAIR):
                st = lax.dot_general(k_ref[rows, AUG * hh:AUG * (hh + 1)], q_ref[:, AUG * hh:AUG * (hh + 1)], _NT,
                                     preferred_element_type=F32)
                if diag:
                    st = jnp.where(_causal_keep(t, t), st, NEG)
                m_prev = m_s[hh]
                m_new = jnp.maximum(m_prev, jnp.max(st, axis=0, keepdims=True))
                a = jnp.exp(m_prev - m_new)
                pt = jnp.exp(st - m_new)
                l_s[hh] = a * l_s[hh] + jnp.sum(pt, axis=0, keepdims=True)
                acc_s[hh] = a * acc_s[hh] + lax.dot_general(v_ref[rows, dh * hh:dh * (hh + 1)], pt.astype(BF16), _TN,
                                                            preferred_element_type=F32)
                m_s[hh] = m_new

        def off_diagonal(kj, carry):
            step(kj, False)
            return carry

        lax.fori_loop(0, qi, off_diagonal, 0)
        step(qi, True)
        for hh in range(PAIR):
            o_ref[hh] = acc_s[hh] / l_s[hh]
            lse_ref[hh, 0] = m_s[hh] + jnp.log(l_s[hh])
        finish()

    return _pcall(
        body, name=name,
        out_shape=[jax.ShapeDtypeStruct((N_HEADS, dh, s), F32), jax.ShapeDtypeStruct((N_HEADS, n, 1, t), F32)] + ex_out,
        grid=grid,
        in_specs=[pl.BlockSpec((t, PAIR * AUG), lambda hp, i: (i, hp)),
                  pl.BlockSpec((s, PAIR * AUG), lambda hp, i: (0, hp)),
                  pl.BlockSpec((s, PAIR * dh), lambda hp, i: (0, V_FOX + hp))] + ex_specs,
        out_specs=[pl.BlockSpec((PAIR, dh, t), lambda hp, i: (hp, 0, i)),
                   pl.BlockSpec((PAIR, 1, 1, t), lambda hp, i: (hp, i, 0, 0))] + ex_specs,
        scratch_shapes=[pltpu.VMEM((PAIR, 1, t), F32), pltpu.VMEM((PAIR, 1, t), F32),
                        pltpu.VMEM((PAIR, dh, t), F32)] + ex_scratch,
        compiler_params=_cp("arbitrary", "arbitrary"),
    )(q_aug, k_aug, qkv, *ex_in)


def _row_dot(a_t, b_t, name, t=2048):
    h, dh, s = a_t.shape
    t = min(t, s)

    def body(a_ref, b_ref, o_ref):
        o_ref[...] = jnp.sum(a_ref[...].astype(F32) * b_ref[...], axis=0, keepdims=True)

    blk = pl.BlockSpec((None, dh, t), lambda hh, i: (hh, 0, i))
    return _pcall(
        body, name=name, out_shape=jax.ShapeDtypeStruct((h, 1, s), F32), grid=(h, s // t),
        in_specs=[blk, blk], out_specs=pl.BlockSpec((None, 1, t), lambda hh, i: (hh, 0, i)),
        compiler_params=_cp("parallel", "parallel"),
    )(a_t, b_t)


def _fox_bwd(q_aug, k_aug, qkv, do, lse, delta, name, t=TQ, exchange=None):
    s = q_aug.shape[0]
    dh = HEAD_DIM
    t = min(t, s)
    n = s // t
    grid = (N_HEADS // PAIR, n)
    ex_in, ex_out, ex_specs, ex_scratch, hook = _riding_exchange(exchange, 6, 5, grid)

    def body(*refs):
        (q_ref, k_ref, v_ref, do_ref, lse_ref, dl_ref, dk_ref, dv_ref, dq_ref, dsum_ref, dqsum_ref,
         dk_s, dv_s, dsum_s), finish = hook(refs)
        kj = pl.program_id(1)

        @pl.when(kj == 0)
        def _():
            dq_ref[...] = jnp.zeros_like(dq_ref)
            dqsum_ref[...] = jnp.zeros_like(dqsum_ref)

        dk_s[...] = jnp.zeros_like(dk_s)
        dv_s[...] = jnp.zeros_like(dv_s)
        dsum_s[...] = jnp.zeros_like(dsum_s)

        def step(qi, diag):
            rows = pl.ds(pl.multiple_of(qi * t, t), t)
            for hh in range(PAIR):
                q = q_ref[rows, AUG * hh:AUG * (hh + 1)]
                k = k_ref[:, AUG * hh:AUG * (hh + 1)]
                st = lax.dot_general(k, q, _NT, preferred_element_type=F32)
                if diag:
                    st = jnp.where(_causal_keep(t, t), st, NEG)
                pt = jnp.exp(st - lse_ref[hh, qi])
                do_v = do_ref[rows, dh * hh:dh * (hh + 1)]
                dpt = lax.dot_general(v_ref[:, dh * hh:dh * (hh + 1)], do_v, _NT, preferred_element_type=F32)
                ds32 = pt * (dpt - dl_ref[hh, qi])
                dsum_s[hh] += sum(ds32[:, 128 * u:128 * (u + 1)] for u in range(t // 128))
                dqsum_ref[hh, qi] += jnp.sum(ds32, axis=0, keepdims=True)
                dst = ds32.astype(BF16)
                dv_s[hh] += jnp.dot(pt.astype(BF16), do_v, preferred_element_type=F32)
                dk_s[hh] += jnp.dot(dst, q[:, :dh], preferred_element_type=F32)
                dq_ref[hh, qi] += lax.dot_general(k[:, :dh], dst, _TN, preferred_element_type=F32)

        def off_diagonal(qi, carry):
            step(qi, False)
            return carry

        step(kj, True)
        lax.fori_loop(kj + 1, n, off_diagonal, 0)
        for hh in range(PAIR):
            dk_ref[:, dh * hh:dh * (hh + 1)] = dk_s[hh].astype(dk_ref.dtype)
            dv_ref[:, dh * hh:dh * (hh + 1)] = dv_s[hh].astype(dv_ref.dtype)
            dsum_ref[:, 128 * hh:128 * (hh + 1)] = dsum_s[hh]
        finish()

    stat = pl.BlockSpec((PAIR, n, 1, t), lambda hp, j: (hp, 0, 0, 0))
    return _pcall(
        body, name=name,
        out_shape=[jax.ShapeDtypeStruct((s, WIDTH), BF16), jax.ShapeDtypeStruct((s, WIDTH), BF16),
                   jax.ShapeDtypeStruct((N_HEADS, n, dh, t), F32), jax.ShapeDtypeStruct((s, N_HEADS * 128), F32),
                   jax.ShapeDtypeStruct((N_HEADS, n, 1, t), F32)] + ex_out,
        grid=grid,
        in_specs=[pl.BlockSpec((s, PAIR * AUG), lambda hp, j: (0, hp)),
                  pl.BlockSpec((t, PAIR * AUG), lambda hp, j: (j, hp)),
                  pl.BlockSpec((t, PAIR * dh), lambda hp, j: (j, V_FOX + hp)),
                  pl.BlockSpec((s, PAIR * dh), lambda hp, j: (0, hp)),
                  stat, stat] + ex_specs,
        out_specs=[pl.BlockSpec((t, PAIR * dh), lambda hp, j: (j, hp)),
                   pl.BlockSpec((t, PAIR * dh), lambda hp, j: (j, hp)),
                   pl.BlockSpec((PAIR, n, dh, t), lambda hp, j: (hp, 0, 0, 0)),
                   pl.BlockSpec((t, PAIR * 128), lambda hp, j: (j, hp)),
                   stat] + ex_specs,
        scratch_shapes=[pltpu.VMEM((PAIR, t, dh), F32), pltpu.VMEM((PAIR, t, dh), F32),
                        pltpu.VMEM((PAIR, t, 128), F32)] + ex_scratch,
        compiler_params=_cp("arbitrary", "arbitrary"),
    )(q_aug, k_aug, qkv, do, lse, delta, *ex_in)


N_SUB = TQ // SUB


def _chunk_logits(kw, qs, b_ref, hh, first_tile, j):
    bias = b_ref[hh, jnp.where(first_tile, 1 + j, 0)]
    return lax.dot_general(kw[SUB * j:SUB * j + SUBK], qs, _NT, preferred_element_type=F32) + bias


def _window(prev_ref, cur_ref, hh):
    cols = slice(HEAD_DIM * hh, HEAD_DIM * (hh + 1))
    return jnp.concatenate([prev_ref[:, cols], cur_ref[:, cols]], axis=0)


def _chunk_fwd(qkv, bias_t, name):
    s = qkv.shape[0]
    dh = HEAD_DIM
    n = s // TQ
    nsub = TQ // SUB
    prev = lambda i: jnp.maximum(i - 1, 0)

    def body(q_ref, kp_ref, kc_ref, vp_ref, vc_ref, b_ref, o_ref, lse_ref):
        first_tile = pl.program_id(1) == 0
        for hh in range(PAIR):
            kw, vw = _window(kp_ref, kc_ref, hh), _window(vp_ref, vc_ref, hh)
            for j in range(nsub):
                qs = q_ref[SUB * j:SUB * (j + 1), dh * hh:dh * (hh + 1)] * QK_SCALE
                st = _chunk_logits(kw, qs, b_ref, hh, first_tile, j)
                m = jnp.max(st, axis=0, keepdims=True)
                pt = jnp.exp(st - m)
                l = jnp.sum(pt, axis=0, keepdims=True)
                ot = lax.dot_general(vw[SUB * j:SUB * j + SUBK], pt.astype(BF16), _TN, preferred_element_type=F32)
                o_ref[hh, :, SUB * j:SUB * (j + 1)] = ot / l
                lse_ref[hh, 0, :, SUB * j:SUB * (j + 1)] = m + jnp.log(l)

    blk = lambda col, m: pl.BlockSpec((TQ, PAIR * dh), lambda hp, i: (m(i), col + hp))
    same = lambda i: i
    return _pcall(
        body, name=name,
        out_shape=(jax.ShapeDtypeStruct((N_HEADS, dh, s), F32), jax.ShapeDtypeStruct((N_HEADS, n, 1, TQ), F32)),
        grid=(N_HEADS // PAIR, n),
        in_specs=[blk(Q_CHUNK, same), blk(K_CHUNK, prev), blk(K_CHUNK, same), blk(V_CHUNK, prev), blk(V_CHUNK, same),
                  pl.BlockSpec((PAIR, 1 + N_SUB, SUBK, SUB), lambda hp, i: (hp, 0, 0, 0))],
        out_specs=(pl.BlockSpec((PAIR, dh, TQ), lambda hp, i: (hp, 0, i)),
                   pl.BlockSpec((PAIR, 1, 1, TQ), lambda hp, i: (hp, i, 0, 0))),
        compiler_params=_cp("parallel", "parallel"),
    )(qkv, qkv, qkv, qkv, qkv, bias_t)


def _chunk_bwd(qkv, do, lse, delta, bias_t, name):
    s = qkv.shape[0]
    dh = HEAD_DIM
    n = s // TQ
    nsub = TQ // SUB
    cur = lambda i: jnp.minimum(i, n - 1)
    prev = lambda i: jnp.maximum(cur(i) - 1, 0)
    done = lambda i: jnp.maximum(i - 1, 0)

    def body(q_ref, kp_ref, kc_ref, vp_ref, vc_ref, do_ref, lse_ref, dl_ref, b_ref,
             dq_ref, dk_ref, dv_ref, db_ref, dkw_s, dvw_s, ck_s, cv_s):
        i = pl.program_id(1)

        @pl.when(i == 0)
        def _():
            db_ref[...] = jnp.zeros_like(db_ref)
            ck_s[...] = jnp.zeros_like(ck_s)
            cv_s[...] = jnp.zeros_like(cv_s)

        dkw_s[...] = jnp.zeros_like(dkw_s)
        dvw_s[...] = jnp.zeros_like(dvw_s)

        @pl.when(i < n)
        def _():
            first_tile = i == 0
            for hh in range(PAIR):
                kw, vw = _window(kp_ref, kc_ref, hh), _window(vp_ref, vc_ref, hh)
                for j in range(nsub):
                    rows = pl.ds(SUB * j, SUB)
                    win = pl.ds(SUB * j, SUBK)
                    qs = q_ref[rows, dh * hh:dh * (hh + 1)] * QK_SCALE
                    st = _chunk_logits(kw, qs, b_ref, hh, first_tile, j)
                    pt = jnp.exp(st - lse_ref[hh, 0, :, rows])
                    do_j = do_ref[rows, dh * hh:dh * (hh + 1)]
                    dpt = lax.dot_general(vw[SUB * j:SUB * j + SUBK], do_j, _NT, preferred_element_type=F32)
                    dst = pt * (dpt - dl_ref[hh, 0, :, rows])
                    db_ref[hh] += dst
                    dsb = dst.astype(BF16)
                    dvw_s[hh, win, :] += jnp.dot(pt.astype(BF16), do_j, preferred_element_type=F32)
                    dkw_s[hh, win, :] += jnp.dot(dsb, qs, preferred_element_type=F32)
                    dq_ref[hh, :, rows] = QK_SCALE * lax.dot_general(kw[SUB * j:SUB * j + SUBK], dsb, _TN,
                                                                     preferred_element_type=F32)

        for hh in range(PAIR):
            cols = slice(dh * hh, dh * (hh + 1))
            dk_ref[:, cols] = (ck_s[hh] + dkw_s[hh, 0:TQ, :]).astype(dk_ref.dtype)
            dv_ref[:, cols] = (cv_s[hh] + dvw_s[hh, 0:TQ, :]).astype(dv_ref.dtype)
        ck_s[...] = dkw_s[:, TQ:2 * TQ, :]
        cv_s[...] = dvw_s[:, TQ:2 * TQ, :]

    blk = lambda col, m: pl.BlockSpec((TQ, PAIR * dh), lambda hp, i: (m(i), col + hp))
    stat = pl.BlockSpec((PAIR, 1, 1, TQ), lambda hp, i: (hp, cur(i), 0, 0))
    bs = pl.BlockSpec((PAIR, SUBK, SUB), lambda hp, i: (hp, 0, 0))
    return _pcall(
        body, name=name,
        out_shape=(jax.ShapeDtypeStruct((N_HEADS, dh, s), F32), jax.ShapeDtypeStruct((s, WIDTH), BF16),
                   jax.ShapeDtypeStruct((s, WIDTH), BF16), jax.ShapeDtypeStruct((N_HEADS, SUBK, SUB), F32)),
        grid=(N_HEADS // PAIR, n + 1),
        in_specs=[blk(Q_CHUNK, cur), blk(K_CHUNK, prev), blk(K_CHUNK, cur), blk(V_CHUNK, prev), blk(V_CHUNK, cur),
                  blk(0, cur), stat, stat, pl.BlockSpec((PAIR, 1 + N_SUB, SUBK, SUB), lambda hp, i: (hp, 0, 0, 0))],
        out_specs=(pl.BlockSpec((PAIR, dh, TQ), lambda hp, i: (hp, 0, cur(i))), blk(0, done), blk(0, done), bs),
        scratch_shapes=[pltpu.VMEM((PAIR, 2 * TQ, dh), F32), pltpu.VMEM((PAIR, 2 * TQ, dh), F32),
                        pltpu.VMEM((PAIR, TQ, dh), F32), pltpu.VMEM((PAIR, TQ, dh), F32)],
        compiler_params=_cp("parallel", "arbitrary"),
    )(qkv, qkv, qkv, qkv, qkv, do, lse, delta, bias_t)


def _mod_part(c_all, w_ada, b_ada, name):
    nl, d, n = w_ada.shape
    b = c_all.shape[0]

    def body(c_ref, w_ref, b_ref, o_ref, cond_ref):
        cv = c_ref[...]
        cond = cv * _sigmoid(cv)
        cond_ref[...] = cond
        o_ref[...] = jnp.dot(cond, w_ref[...], preferred_element_type=F32, precision=HI) + b_ref[...]

    return _pcall(
        body, name=name,
        out_shape=(jax.ShapeDtypeStruct((nl, b, n), F32), jax.ShapeDtypeStruct((b, d), F32)),
        grid=(nl,),
        in_specs=[pl.BlockSpec((b, d), lambda l: (0, 0)), pl.BlockSpec((None, d, n), lambda l: (l, 0, 0)),
                  pl.BlockSpec((None, 1, n), lambda l: (l, 0, 0))],
        out_specs=(pl.BlockSpec((None, b, n), lambda l: (l, 0, 0)), pl.BlockSpec((b, d), lambda l: (0, 0))),
        compiler_params=_cp("arbitrary"),
    )(c_all, w_ada, b_ada)


def _adamw(parts, w, m, v, name, tr=1024):
    p, rl, cp = parts[0].shape
    r, c = w.shape
    tr = _tile(rl, tr, 16)
    per = rl // tr
    c1 = 1.0 / (1.0 - ADAM_B1 ** ADAM_STEP)
    c2 = 1.0 / (1.0 - ADAM_B2 ** ADAM_STEP)

    def body(*refs):
        p_refs, (w_ref, m_ref, v_ref, g_ref, d_ref, nm_ref, nv_ref) = refs[:len(parts)], refs[len(parts):]
        for which, p_ref in enumerate(p_refs):
            pl.when(pl.program_id(0) // per == which)(
                functools.partial(update, p_ref, w_ref, m_ref, v_ref, g_ref, d_ref, nm_ref, nv_ref))

    def update(p_ref, w_ref, m_ref, v_ref, g_ref, d_ref, nm_ref, nv_ref):
        g = p_ref[0].astype(F32)
        for i in range(1, p):
            g = g + p_ref[i].astype(F32)
        g = g[:, :c]
        nm = ADAM_B1 * m_ref[...] + (1.0 - ADAM_B1) * g
        nv = ADAM_B2 * v_ref[...] + (1.0 - ADAM_B2) * (g * g)
        g_ref[...] = g
        nm_ref[...] = nm
        nv_ref[...] = nv
        d_ref[...] = -ADAM_LR * ((nm * c1) / (jnp.sqrt(nv * c2) + ADAM_EPS) + ADAM_WD * w_ref[...])

    blk = pl.BlockSpec((tr, c), lambda i: (i, 0))
    out = jax.ShapeDtypeStruct((r, c), F32)
    return _pcall(
        body, name=name, out_shape=(out, out, out, out), grid=(r // tr,),
        in_specs=[pl.BlockSpec((p, tr, cp), lambda i, _w=which: (0, jnp.clip(i - _w * per, 0, per - 1), 0))
                  for which in range(len(parts))] + [blk, blk, blk],
        out_specs=(blk, blk, blk, blk),
        compiler_params=_cp("parallel"),
    )(*parts, w, m, v)


def _pad_lanes(a):
    pad = (-a.shape[-1]) % 128
    return jnp.pad(a, [(0, 0)] * (a.ndim - 1) + [(0, pad)]) if pad else a


def _rows128(a, mult=16):
    flat = a.reshape(-1)
    n = flat.shape[0]
    per = 128 * mult
    pad = (-n) % per
    if pad:
        flat = jnp.concatenate([flat, jnp.zeros((pad,), a.dtype)])
    return flat.reshape(-1, 128)


def _pack(arrs, mult=16):
    pieces = [_rows128(a, mult) for a in arrs]
    return jnp.concatenate(pieces, axis=0), [p.shape[0] for p in pieces]


def _unpack(packed, rows, shapes, lead=()):
    out, at = [], 0
    for r, shp in zip(rows, shapes):
        n = int(np.prod(shp))
        piece = packed[..., at:at + r, :].reshape(lead + (r * 128,))[..., :n]
        out.append(piece.reshape(lead + tuple(shp)))
        at += r
    return out


def _rel_bias_tile(table):
    h = table.shape[0]
    lo = REL_CLIP - (CHUNK - 1)
    n_far = LEFT_CHUNKS * CHUNK + CHUNK - 1 - REL_CLIP
    vec = jnp.concatenate([table[:, lo:2 * REL_CLIP], jnp.repeat(table[:, 2 * REL_CLIP:], n_far + 1, axis=1)], axis=1)
    rev = vec[:, ::-1]
    n_vec = BAND + CHUNK - 1
    skew = jnp.tile(rev, (1, CHUNK + 1))[:, :CHUNK * (n_vec + 1)].reshape(h, CHUNK, n_vec + 1)
    bias = skew[:, ::-1, :BAND]
    neg = jnp.full((h, CHUNK, CHUNK), NEG, F32)
    two = jnp.concatenate([jnp.concatenate([bias, neg], axis=2), jnp.concatenate([neg, bias], axis=2)], axis=1)
    plain = two.transpose(0, 2, 1)
    key = np.arange(SUBK)[:, None]
    return jnp.stack([plain] + [jnp.where(key >= TQ - SUB * j, plain, NEG) for j in range(N_SUB)], axis=1)


def _rel_bias_tile_grad(dbias_t):
    h = dbias_t.shape[0]
    two = dbias_t.transpose(0, 2, 1)
    dbias = two[:, :CHUNK, :BAND] + two[:, CHUNK:, CHUNK:]
    n_vec = BAND + CHUNK - 1
    dskew = jnp.pad(dbias[:, ::-1, :], ((0, 0), (0, 0), (0, n_vec + 1 - BAND))).reshape(h, CHUNK * (n_vec + 1))
    dskew = jnp.pad(dskew, ((0, 0), (0, (CHUNK + 1) * n_vec - CHUNK * (n_vec + 1))))
    drev = jnp.sum(dskew.reshape(h, CHUNK + 1, n_vec), axis=1)
    dvec = drev[:, ::-1]
    lo = REL_CLIP - (CHUNK - 1)
    n_near = 2 * REL_CLIP - lo
    return jnp.concatenate([jnp.zeros((h, lo), F32), dvec[:, :n_near],
                            jnp.sum(dvec[:, n_near:], axis=1, keepdims=True)], axis=1)


BIG = ("w_in", "w_br_fox", "w_br_chunk", "w_out", "w_up", "w_down")
SMALL = ("b_f", "rel_bias", "conv_b", "b_ada", "ln1_g", "ln1_b", "ln2_g", "ln2_b")
ORDER = ("w_in", "b_f", "rel_bias", "w_br_fox", "w_br_chunk", "w_out", "w_up", "conv_w", "conv_b", "w_down",
         "w_ada", "b_ada", "ln1_g", "ln1_b", "ln2_g", "ln2_b")


def _dest_major(g, axis):
    shp = g.shape
    g = g.reshape(shp[:axis] + (N_DEV, shp[axis] // N_DEV) + shp[axis + 1:])
    return jnp.moveaxis(g, axis, 0)


EARLY = ("w_down", "w_up", "w_out", "w_br_fox", "w_br_chunk")
LATE = ("w_in",)


def _shard_axis(n):
    return 0 if n in ("w_out", "w_down") else 1


def _slab(g, n):
    return _pad_lanes(_dest_major(g, _shard_axis(n)).astype(BF16))


def _layer_fwd(x, h1, wl, mod, next_mod, alpha, gather=None):
    sh1, sc1, g1, sh2, sc2, g2 = mod
    s = x.shape[0]
    qkv = _mm(h1, wl["w_qkv"], out_dtype=BF16, name="proj_qkv", tk=1024)
    gates = _mm(h1, wl["w_gates"], out_dtype=F32, name="proj_gates", tk=1024)
    f_t = _mm(wl["w_f_t"], h1, out_dtype=F32, name="proj_f", tb=True, tk=1024)[:N_HEADS]
    f_t = f_t.reshape(N_HEADS, s // 128, 128)
    b_f = jnp.broadcast_to(wl["b_f"].reshape(N_HEADS, 1, 1), (N_HEADS, 1, 128))
    cum = _forget_fwd(f_t, b_f, "forget_fwd").reshape(N_SPLIT, N_HEADS, s)
    cum_cols = cum.transpose(2, 1, 0).astype(BF16)

    ones = jnp.ones((s, N_HEADS, N_SPLIT), BF16)
    zeros = jnp.zeros((s, N_HEADS, AUG - HEAD_DIM - 2 * N_SPLIT), BF16)
    q_a = qkv[:, :WIDTH].reshape(s, N_HEADS, HEAD_DIM) * QK_SCALE
    k_a = qkv[:, WIDTH:2 * WIDTH].reshape(s, N_HEADS, HEAD_DIM)
    q_aug = jnp.concatenate([q_a, cum_cols, ones, zeros], axis=-1).reshape(s, N_HEADS * AUG)
    k_aug = jnp.concatenate([k_a, ones, -cum_cols, zeros], axis=-1).reshape(s, N_HEADS * AUG)
    o_a_t, lse_a, *gathered = _fox_fwd(q_aug, k_aug, qkv, "fox_fwd_gather" if gather else "fox_fwd",
                                       exchange=(False, gather) if gather else None)
    bias_t = _rel_bias_tile(wl["rel_bias"])
    o_c_t, lse_c = _chunk_fwd(qkv, bias_t, "chunk_fwd")
    o_a = o_a_t.transpose(2, 0, 1).reshape(s, WIDTH).astype(BF16)
    o_c = o_c_t.transpose(2, 0, 1).reshape(s, WIDTH).astype(BF16)

    merged = _merge_fwd(o_a, o_c, wl["w_br_fox"], wl["w_br_chunk"], gates, "merge_fwd")
    mix, x1, h2 = _mm_res_ln(merged, wl["w_out"], x, g1, wl["ln1_g"], wl["ln1_b"], sc2, sh2, alpha, "out_ln1")
    u0 = _mm(h2, wl["w_up"], out_dtype=F32, name="ffn_up", tk=1024, tn=1408, o_halves=True)
    act = _conv_act_fwd(u0, wl["conv_w"], wl["conv_b"], "conv_act_fwd")
    y2, x2, h_next = _mm_res_ln(act, wl["w_down"], x1, g2, wl["ln2_g"], wl["ln2_b"], next_mod[0], next_mod[1],
                                alpha, "down_ln2", tk=1408)
    saved = dict(x=x, h1=h1, gates=gates, f_t=f_t, b_f=b_f, q_aug=q_aug, k_aug=k_aug, qkv=qkv,
                 bias_t=bias_t, o_a_t=o_a_t, o_c_t=o_c_t, lse_a=lse_a, lse_c=lse_c,
                 o_a=o_a, o_c=o_c, merged=merged, mix=mix, x1=x1, h2=h2, u0=u0, act=act, y2=y2)
    return x2, h_next, saved, gathered


def _layer_bwd(dx2, sv, wl, mod, alpha, send):
    sh1, sc1, g1, sh2, sc2, g2 = mod
    s = dx2.shape[0]
    g = {}
    dres2, dy2, sums = _ln_res_bwd(dx2, sv["x1"], sv["y2"], g2, wl["ln2_g"], alpha, "ln2_bwd")
    g["ln2_g"], g["ln2_b"], dg2 = sums[0], sums[1], sums[2]
    dact = _mm(dy2, wl["w_down_t"], out_dtype=F32, name="dact", tk=1024, tn=1408)
    g["w_down"] = _mm(sv["act"], dy2, out_dtype=F32, name="g_w_down", ta=True, tm=1408, tk=2048)
    du0, csum = _conv_act_bwd(dact, sv["u0"], wl["conv_w"], wl["conv_b"], "conv_act_bwd")
    g["conv_w"] = jnp.concatenate([csum[0, 0:3], csum[1, 0:3]], axis=1)
    g["conv_b"] = jnp.concatenate([csum[0, 3], csum[1, 3]])
    dh2 = _mm(du0, wl["w_up_t"], out_dtype=F32, name="dh2", tk=2816, a_halves=True)
    g["w_up"] = _mm(sv["h2"], du0, out_dtype=F32, name="g_w_up", ta=True, tk=2048, tn=1408, b_halves=True)
    dx1, sums = _ln_mod_bwd(dh2, sv["x1"], sc2, dres2, "ln_mod2_bwd")
    dsh2, dsc2 = sums[0], sums[1]

    dres1, dmix, sums = _ln_res_bwd(dx1, sv["x"], sv["mix"], g1, wl["ln1_g"], alpha, "ln1_bwd")
    g["ln1_g"], g["ln1_b"], dg1 = sums[0], sums[1], sums[2]
    g["w_out"] = _mm(sv["merged"], dmix, out_dtype=F32, name="g_w_out", ta=True, tk=2048)
    dba, dbc, do_a, do_c, dgates = _merge_bwd(
        dmix, wl["w_out_t"], sv["o_a"], sv["o_c"], wl["w_br_fox"], wl["w_br_chunk"],
        wl["w_br_fox_t"], wl["w_br_chunk_t"], sv["gates"], "merge_bwd")
    g["w_br_fox"] = _mm(sv["o_a"], dba, out_dtype=F32, name="g_w_br_fox", ta=True, tk=2048)
    g["w_br_chunk"] = _mm(sv["o_c"], dbc, out_dtype=F32, name="g_w_br_chunk", ta=True, tk=2048)
    n_t = s // min(TQ, s)
    stats = lambda a: a.reshape(N_HEADS, n_t, 1, s // n_t)

    delta_a = _row_dot(do_a.T.reshape(N_HEADS, HEAD_DIM, s), sv["o_a_t"], "delta_fox")
    send = list(send) + [_slab(g[n], n) for n in EARLY]
    dk_a, dv_a, dq_a_t, dsum, dqsum, *landed = _fox_bwd(sv["q_aug"], sv["k_aug"], sv["qkv"], do_a, sv["lse_a"],
                                                        stats(delta_a), "fox_bwd_scatter_%d" % len(send),
                                                        exchange=(True, send))
    dq_a = (dq_a_t * QK_SCALE).transpose(1, 3, 0, 2).reshape(s, WIDTH).astype(BF16)
    d_cum = dqsum.reshape(N_HEADS, s) - jnp.sum(dsum.reshape(s, N_HEADS, 128), axis=-1).T
    df_t, db_f = _forget_bwd(d_cum.reshape(N_HEADS, s // 128, 128), sv["f_t"], sv["b_f"], "forget_bwd")
    g["b_f"] = db_f[:, 0, 0]
    df_t = df_t.reshape(N_HEADS, s)

    delta_c = _row_dot(do_c.T.reshape(N_HEADS, HEAD_DIM, s), sv["o_c_t"], "delta_chunk")
    dq_c_t, dk_c, dv_c, dbias_t = _chunk_bwd(sv["qkv"], do_c, sv["lse_c"], stats(delta_c), sv["bias_t"], "chunk_bwd")
    g["rel_bias"] = _rel_bias_tile_grad(dbias_t)
    dq_c = dq_c_t.transpose(2, 0, 1).reshape(s, WIDTH).astype(BF16)

    dqkv = jnp.concatenate([dq_a, dk_a, dv_a, dq_c, dk_c, dv_c], axis=1)
    df_pad = jnp.zeros((16 - N_HEADS, s), F32)
    df16 = jnp.concatenate([df_t, df_pad], axis=0).astype(BF16)
    df_cols = jnp.concatenate([df16.T, jnp.zeros((s, AUG - 16), BF16)], axis=1)

    dh1 = _mm(dqkv, wl["w_qkv_t"], out_dtype=F32, name="dh1_qkv", tk=1536)
    dh1g = _mm(dgates, wl["w_gates_t"], out_dtype=F32, name="dh1_gates", tk=2048)
    g_qkv = _mm(sv["h1"], dqkv, out_dtype=F32, name="g_w_qkv", ta=True, tk=2048)
    g_gates = _mm(sv["h1"], dgates, out_dtype=F32, name="g_w_gates", ta=True, tk=2048)
    g_f_t = _mm(df16, sv["h1"], out_dtype=F32, name="g_w_f", tk=1024)[:N_HEADS]
    g["w_in"] = jnp.concatenate([g_qkv[:, :3 * WIDTH], g_f_t.T, g_qkv[:, 3 * WIDTH:], g_gates], axis=1)
    dx, sums = _ln_mod_bwd(dh1, sv["x"], sc1, dres1, "ln_mod1_bwd", extra=(dh1g, df_cols, wl["w_f_pad"]))
    dsh1, dsc1 = sums[0], sums[1]
    g["mod"] = jnp.concatenate([dsh1, dsc1, dg1, dsh2, dsc2, dg2])
    return dx, g, landed


def kernel(x, c, w_in, b_f, rel_bias, w_br_fox, w_br_chunk, w_out, w_up, conv_w, conv_b, w_down, w_ada, b_ada, ln1_g, ln1_b, ln2_g, ln2_b, loss_target, m_w_in, m_b_f, m_rel_bias, m_w_br_fox, m_w_br_chunk, m_w_out, m_w_up, m_conv_w, m_conv_b, m_w_down, m_w_ada, m_b_ada, m_ln1_g, m_ln1_b, m_ln2_g, m_ln2_b, v_w_in, v_b_f, v_rel_bias, v_w_br_fox, v_w_br_chunk, v_w_out, v_w_up, v_conv_w, v_conv_b, v_w_down, v_w_ada, v_b_ada, v_ln1_g, v_ln1_b, v_ln2_g, v_ln2_b):
    w = dict(w_in=w_in, b_f=b_f, rel_bias=rel_bias, w_br_fox=w_br_fox, w_br_chunk=w_br_chunk, w_out=w_out,
             w_up=w_up, conv_w=conv_w, conv_b=conv_b, w_down=w_down, w_ada=w_ada, b_ada=b_ada,
             ln1_g=ln1_g, ln1_b=ln1_b, ln2_g=ln2_g, ln2_b=ln2_b)
    m = dict(w_in=m_w_in, b_f=m_b_f, rel_bias=m_rel_bias, w_br_fox=m_w_br_fox, w_br_chunk=m_w_br_chunk,
             w_out=m_w_out, w_up=m_w_up, conv_w=m_conv_w, conv_b=m_conv_b, w_down=m_w_down, w_ada=m_w_ada,
             b_ada=m_b_ada, ln1_g=m_ln1_g, ln1_b=m_ln1_b, ln2_g=m_ln2_g, ln2_b=m_ln2_b)
    v = dict(w_in=v_w_in, b_f=v_b_f, rel_bias=v_rel_bias, w_br_fox=v_w_br_fox, w_br_chunk=v_w_br_chunk,
             w_out=v_w_out, w_up=v_w_up, conv_w=v_conv_w, conv_b=v_conv_b, w_down=v_w_down, w_ada=v_w_ada,
             b_ada=v_b_ada, ln1_g=v_ln1_g, ln1_b=v_ln1_b, ln2_g=v_ln2_g, ln2_b=v_ln2_b)
    depth, d, _ = w_in.shape
    s = x.shape[1]
    alpha = (2.0 * depth) ** 0.25
    me = 4 * lax.axis_index("x") + 2 * lax.axis_index("y") + lax.axis_index("c")
    x0 = x.reshape(s, d)
    target = loss_target.reshape(s, d)

    small_in, small_rows = _pack([c, conv_w], mult=8)
    small_all = _all_gather([small_in], "gather_c_conv")[0]
    c_all, conv_w_all = _unpack(small_all, small_rows, [c.shape, conv_w.shape], lead=(N_DEV,))
    c_all = c_all.reshape(N_DEV, d)
    conv_w_full = conv_w_all.transpose(1, 2, 0, 3).reshape(depth, conv_w.shape[1], -1)
    n_ada = w_ada.shape[2]
    b_ada_mine = lax.dynamic_slice_in_dim(b_ada, me * n_ada, n_ada, axis=1).reshape(depth, 1, n_ada)
    mod_part, cond_all = _mod_part(c_all, w_ada, b_ada_mine, "mod_part")
    mod_all = _all_gather([mod_part.reshape(depth * N_DEV, n_ada)], "gather_mod")[0]
    mod_all = mod_all.reshape(N_DEV, depth, N_DEV, n_ada)
    mod_mine = lax.dynamic_index_in_dim(mod_all, me, axis=2, keepdims=False)
    mod_mine = mod_mine.transpose(1, 0, 2).reshape(depth, N_MOD, 1, d)

    shards = lambda l: [_pad_lanes(w[n][l].astype(BF16)) for n in BIG]
    cols = np.cumsum([0, WIDTH, WIDTH, WIDTH, N_HEADS, WIDTH, WIDTH, WIDTH, d, d])

    def layer_weights(l, parts):
        full = {}
        for n, p in zip(BIG, parts):
            rows, ncol = w[n].shape[1:]
            p = jnp.moveaxis(p[..., :ncol], 0, _shard_axis(n))
            full[n] = p.reshape((rows * N_DEV, ncol) if _shard_axis(n) == 0 else (rows, ncol * N_DEV))
        wi = full["w_in"]
        w_qkv = jnp.concatenate([wi[:, :cols[3]], wi[:, cols[4]:cols[7]]], axis=1)
        w_gates = wi[:, cols[7]:]
        w_f_t = jnp.concatenate([wi[:, cols[3]:cols[4]].T, jnp.zeros((16 - N_HEADS, d), BF16)], axis=0)
        w_f_pad = jnp.concatenate([w_f_t, jnp.zeros((AUG - 16, d), BF16)], axis=0)
        row = lambda a: a[l].reshape(1, -1)
        return dict(
            w_qkv=w_qkv, w_gates=w_gates, w_f_t=w_f_t, w_f_pad=w_f_pad, w_qkv_t=w_qkv.T, w_gates_t=w_gates.T,
            w_br_fox=full["w_br_fox"], w_br_chunk=full["w_br_chunk"],
            w_br_fox_t=full["w_br_fox"].T, w_br_chunk_t=full["w_br_chunk"].T,
            w_out=full["w_out"], w_out_t=full["w_out"].T, w_up=full["w_up"], w_up_t=full["w_up"].T,
            w_down=full["w_down"], w_down_t=full["w_down"].T,
            conv_w=conv_w_full[l].reshape(3, 2, -1).transpose(1, 0, 2), conv_b=conv_b[l].reshape(2, 1, -1),
            b_f=b_f[l], rel_bias=rel_bias[l],
            ln1_g=row(ln1_g), ln1_b=row(ln1_b), ln2_g=row(ln2_g), ln2_b=row(ln2_b))

    xs, saved, layers = x0, [], []
    hs = _ln_mod(x0, mod_mine[0][1], mod_mine[0][0], "ln_mod1")
    parts = _all_gather(shards(0), "gather_weights")
    for l in range(depth):
        layers.append(layer_weights(l, parts))
        nxt = mod_mine[min(l + 1, depth - 1)]
        xs, hs, sv, parts = _layer_fwd(xs, hs, layers[l], list(mod_mine[l]), (nxt[1], nxt[0]), alpha,
                                       gather=shards(l + 1) if l + 1 < depth else None)
        saved.append(sv)
    dxs, loss_part = _loss_grad(xs, target, "loss_grad")
    loss = lax.psum(loss_part[0, 0], AXES)
    grads, big_parts, send = [None] * depth, {}, []
    for l in reversed(range(depth)):
        dxs, grads[l], landed = _layer_bwd(dxs, saved[l], layers[l], list(mod_mine[l]), alpha, send)
        for n, part in zip([(n, l + 1) for n in LATE if send] + [(n, l) for n in EARLY], landed):
            big_parts[n] = part
        send = [_slab(grads[l][n], n) for n in LATE]
    for n, part in zip(LATE, _all_to_all(send, "scatter_grads")):
        big_parts[n, 0] = part
    grad_x = dxs.reshape(1, s, d)
    stack = lambda n: jnp.stack([grads[l][n] for l in range(depth)])

    rep = ["b_ada"] + [n for n in SMALL if n != "b_ada"]
    small_g = [stack("mod")] + [stack(n) for n in rep[1:]] + [stack("conv_w")]
    small_pack, small_g_rows = _pack(small_g, mult=8)
    small_parts = _all_gather([small_pack], "gather_small_grads")[0]
    n_rep_rows = sum(small_g_rows[:-1])
    dmod_all = _unpack(small_parts, small_g_rows[:1], [(depth, N_MOD * d)], lead=(N_DEV,))[0]
    cw_all = _unpack(small_parts[:, n_rep_rows:], small_g_rows[-1:], [small_g[-1].shape], lead=(N_DEV,))[0]

    out = {}
    for n in BIG:
        rows2d = lambda a: a.reshape(-1, a.shape[-1])
        res = _adamw([big_parts[n, l] for l in range(depth)], rows2d(w[n]), rows2d(m[n]), rows2d(v[n]),
                     "adamw_" + n, tr=256)
        for kind, r in zip(("grad", "delta", "new_m", "new_v"), res):
            out[kind, n] = r.reshape(w[n].shape)

    dmod_mine = lax.dynamic_slice_in_dim(dmod_all, me * n_ada, n_ada, axis=2)
    g_ada = jnp.stack([_mm(cond_all.T, dmod_mine[:, l], out_dtype=F32, name="g_w_ada", tk=N_DEV, tn=n_ada,
                           precision=HI) for l in range(depth)])
    lp = [_rows128(t["w_ada"], 8) for t in (w, m, v)]
    res = _adamw([_rows128(g_ada, 8)[None]], *lp, "adamw_ada")
    for kind, r in zip(("grad", "delta", "new_m", "new_v"), res):
        out[kind, "w_ada"] = _unpack(r, [lp[0].shape[0]], [w_ada.shape])[0]

    rp = [_pack([t[n] for n in rep], mult=8)[0] for t in (w, m, v)]
    res = _adamw([small_parts[:, :n_rep_rows]], *rp, "adamw_small")
    for kind, r in zip(("grad", "delta", "new_m", "new_v"), res):
        for n, a in zip(rep, _unpack(r, small_g_rows[:-1], [w[n].shape for n in rep])):
            out[kind, n] = a

    n_cw = conv_w.shape[2]
    cw_mine = lax.dynamic_slice_in_dim(cw_all, me * n_cw, n_cw, axis=3)
    cp = [_rows128(t["conv_w"], 8) for t in (w, m, v)]
    cw_in = jnp.stack([_rows128(cw_mine[i], 8) for i in range(N_DEV)])
    res = _adamw([cw_in], *cp, "adamw_conv_w")
    for kind, r in zip(("grad", "delta", "new_m", "new_v"), res):
        out[kind, "conv_w"] = _unpack(r, [cp[0].shape[0]], [conv_w.shape])[0]

    result = [loss, grad_x]
    for kind in ("grad", "delta", "new_m", "new_v"):
        result += [out[kind, n] for n in ORDER]
    return tuple(result)
```

```python
import functools
import math

import jax
import jax.numpy as jnp
import numpy as np
from jax import lax
from jax.experimental import pallas as pl
from jax.experimental.pallas import tpu as pltpu

F32 = jnp.float32
BF16 = jnp.bfloat16
HI = lax.Precision.HIGHEST
MESH_ID = pl.DeviceIdType.MESH
AXES = ("x", "y", "c")
N_DEV = 8

N_HEADS = 8
HEAD_DIM = 64
WIDTH = N_HEADS * HEAD_DIM
CHUNK = 64
LEFT_CHUNKS = 8
BAND = (LEFT_CHUNKS + 1) * CHUNK
REL_CLIP = 128
LN_EPS = 1e-5
N_MOD = 6
QK_SCALE = 1.0 / math.sqrt(HEAD_DIM)
NEG = -1e30
AUG = 128
N_SPLIT = 3
TQ = 512
SUB = 2 * CHUNK
SUBK = BAND + CHUNK

ADAM_LR, ADAM_B1, ADAM_B2, ADAM_EPS, ADAM_WD, ADAM_STEP = 0.001, 0.9, 0.999, 1e-08, 0.01, 10

VMEM_LIMIT = 56 * 2 ** 20


def _pcall(body, **kw):
    return pl.pallas_call(body, **kw)


def _cp(*sem):
    return pltpu.CompilerParams(dimension_semantics=sem if sem else None, vmem_limit_bytes=VMEM_LIMIT)


def _my_place():
    return lax.axis_index("x"), lax.axis_index("y"), lax.axis_index("c")


N_PEER = N_DEV - 1


def _comm_call(body, xs, out_shapes, name):
    n = len(xs)
    hbm = pl.BlockSpec(memory_space=pltpu.HBM)
    return _pcall(
        functools.partial(body, n), name=name, out_shape=out_shapes, in_specs=[hbm] * n, out_specs=[hbm] * n,
        scratch_shapes=[pltpu.SemaphoreType.DMA((N_PEER * n,)), pltpu.SemaphoreType.DMA((N_PEER * n,)),
                        pltpu.SemaphoreType.DMA((n,))],
    )(*xs)


def _all_gather(xs, name):
    def body(n, *refs):
        x_refs, out_refs, (send_sems, recv_sems, local_sems) = refs[:n], refs[n:2 * n], refs[2 * n:]
        mx, my, mc = _my_place()
        me, sibling = (mx, my, mc), (mx, my, 1 - mc)
        chips = [(1 - mx, my), (mx, 1 - my), (1 - mx, 1 - my)]

        def slot(a, px, py, pc):
            return out_refs[a].at[4 * px + 2 * py + pc]

        def copy(a, k, block, to, own=False):
            return pltpu.make_async_remote_copy(
                src_ref=x_refs[a] if own else slot(a, *block), dst_ref=slot(a, *block),
                send_sem=send_sems.at[a * N_PEER + k], recv_sem=recv_sems.at[a * N_PEER + k],
                device_id=to, device_id_type=MESH_ID)

        arrays = range(n)
        mine = [pltpu.make_async_copy(x_refs[a], slot(a, *me), local_sems.at[a]) for a in arrays]
        first = [copy(a, 0, me, sibling, own=True) for a in arrays]
        first += [copy(a, 1 + j, me, (*chip, mc), own=True) for a in arrays for j, chip in enumerate(chips)]
        for cp in mine + first:
            cp.start()
        passed = []
        for a in arrays:
            for j, chip in enumerate(chips):
                copy(a, 1 + j, (*chip, mc), me).wait_recv()
                passed.append(copy(a, 4 + j, (*chip, mc), sibling))
                passed[-1].start()
        for a in arrays:
            copy(a, 0, sibling, me).wait_recv()
            for j, chip in enumerate(chips):
                copy(a, 4 + j, (*chip, 1 - mc), me).wait_recv()
        for cp in first + passed:
            cp.wait_send()
        for cp in mine:
            cp.wait()

    return _comm_call(body, xs, [jax.ShapeDtypeStruct((N_DEV,) + x.shape, x.dtype) for x in xs], name)


def _direct_exchange(scatter, x_refs, out_refs, send_sems, recv_sems, local_sems):
    mx, my, mc = _my_place()
    me = 4 * mx + 2 * my + mc
    mine, copies = [], []
    for a, (x_ref, out_ref) in enumerate(zip(x_refs, out_refs)):
        mine.append(pltpu.make_async_copy(x_ref.at[me] if scatter else x_ref, out_ref.at[me], local_sems.at[a]))
        for k in range(N_PEER):
            px, py, pc = mx ^ ((k + 1) >> 2), my ^ (((k + 1) >> 1) & 1), mc ^ ((k + 1) & 1)
            copies.append(pltpu.make_async_remote_copy(
                src_ref=x_ref.at[4 * px + 2 * py + pc] if scatter else x_ref, dst_ref=out_ref.at[me],
                send_sem=send_sems.at[a * N_PEER + k], recv_sem=recv_sems.at[a * N_PEER + k],
                device_id=(px, py, pc), device_id_type=MESH_ID))

    def start():
        for cp in mine + copies:
            cp.start()

    def wait():
        for cp in copies:
            cp.wait_recv()
        for cp in copies:
            cp.wait_send()
        for cp in mine:
            cp.wait()

    return start, wait


def _exchange_shapes(scatter, xs):
    return [jax.ShapeDtypeStruct(x.shape if scatter else (N_DEV,) + x.shape, x.dtype) for x in xs]


def _exchange_scratch(n):
    return [pltpu.SemaphoreType.DMA((N_PEER * n,)), pltpu.SemaphoreType.DMA((N_PEER * n,)),
            pltpu.SemaphoreType.DMA((n,))]


def _tile(n, pref, unit=128):
    if n <= pref:
        return n
    t = pref - pref % unit
    while t > unit and n % t:
        t -= unit
    assert n % t == 0, (n, pref, unit)
    return t


def _mm(a, b, *, out_dtype, name, ta=False, tb=False, tm=1024, tn=1024, tk=512, precision=None,
        a_halves=False, b_halves=False, o_halves=False, exchange=None):
    a_shape = (a.shape[1], 2 * a.shape[2]) if a_halves else a.shape
    b_shape = (b.shape[1], 2 * b.shape[2]) if b_halves else b.shape
    (k_dim, m) = a_shape if ta else a_shape[::-1]
    n = b_shape[0] if tb else b_shape[1]
    tm, tn = _tile(m, tm, 8), _tile(n // 2 if (b_halves or o_halves) else n, tn)
    tk = _tile(k_dim // 2 if a_halves else k_dim, tk)
    nk = k_dim // tk
    dims = (((0 if ta else 1,), (1 if tb else 0,)), ((), ()))
    grid = (m // tm, n // tn, nk)
    ex_in, ex_out, ex_specs, ex_scratch, hook = _riding_exchange(exchange, 2, 1, grid)

    def body(*refs):
        (a_ref, b_ref, o_ref, acc_ref), finish = hook(refs)
        k = pl.program_id(2)
        if nk == 1:
            o_ref[...] = lax.dot_general(a_ref[...], b_ref[...], dims, preferred_element_type=F32,
                                         precision=precision).astype(o_ref.dtype)
            finish()
            return

        @pl.when(k == 0)
        def _():
            acc_ref[...] = jnp.zeros_like(acc_ref)

        acc_ref[...] += lax.dot_general(a_ref[...], b_ref[...], dims, preferred_element_type=F32,
                                        precision=precision)

        @pl.when(k == nk - 1)
        def _():
            o_ref[...] = acc_ref[...].astype(o_ref.dtype)

        finish()

    def spec(shape2, pick, halves, per_half):
        if not halves:
            return pl.BlockSpec(shape2, pick)

        def index(i, j, k):
            r, c = pick(i, j, k)
            return (c // per_half, r, c % per_half)
        return pl.BlockSpec((None,) + shape2, index)

    assert not (ta and a_halves) and not (tb and b_halves)
    a_spec = spec((tk, tm), lambda i, j, k: (k, i), False, 0) if ta else \
        spec((tm, tk), lambda i, j, k: (i, k), a_halves, (k_dim // 2) // tk if a_halves else 0)
    b_spec = spec((tn, tk), lambda i, j, k: (j, k), False, 0) if tb else \
        spec((tk, tn), lambda i, j, k: (k, j), b_halves, (n // 2) // tn if b_halves else 0)
    o_spec = spec((tm, tn), lambda i, j, k: (i, j), o_halves, (n // 2) // tn if o_halves else 0)
    out_shape = (2, m, n // 2) if o_halves else (m, n)
    res = _pcall(
        body, name=name,
        out_shape=[jax.ShapeDtypeStruct(out_shape, out_dtype)] + ex_out,
        grid=grid,
        in_specs=[a_spec, b_spec] + ex_specs,
        out_specs=[o_spec] + ex_specs,
        scratch_shapes=[pltpu.VMEM((tm, tn), F32)] + ex_scratch,
        compiler_params=_cp(*(["arbitrary"] * 3 if exchange else ["parallel", "parallel", "arbitrary"])),
    )(a, b, *ex_in)
    return res if exchange else res[0]


def _ln_stats(x):
    mu = jnp.mean(x, axis=-1, keepdims=True)
    xc = x - mu
    var = jnp.mean(xc * xc, axis=-1, keepdims=True)
    rstd = lax.rsqrt(var + LN_EPS)
    return xc * rstd, rstd


def _ln_mod(x, sc, sh, name, tm=512):
    s, d = x.shape
    tm = min(tm, s)

    def body(x_ref, sc_ref, sh_ref, h_ref):
        xhat, _ = _ln_stats(x_ref[...])
        h_ref[...] = (xhat * (1.0 + sc_ref[...]) + sh_ref[...]).astype(h_ref.dtype)

    row = pl.BlockSpec((1, d), lambda i: (0, 0))
    return _pcall(
        body, name=name, out_shape=jax.ShapeDtypeStruct((s, d), BF16), grid=(s // tm,),
        in_specs=[pl.BlockSpec((tm, d), lambda i: (i, 0)), row, row],
        out_specs=pl.BlockSpec((tm, d), lambda i: (i, 0)),
        compiler_params=_cp("parallel"),
    )(x, sc, sh)


def _mm_res_ln(a, w, xres, gate, ln_g, ln_b, sc, sh, alpha, name, tm=512, tk=512):
    s, k_dim = a.shape
    d = w.shape[1]
    tm, tk = _tile(s, tm, 8), _tile(k_dim, tk)
    nk = k_dim // tk

    def body(a_ref, w_ref, x_ref, g_ref, lg_ref, lb_ref, sc_ref, sh_ref, y_ref, xn_ref, h_ref, acc_ref):
        k = pl.program_id(1)

        @pl.when(k == 0)
        def _():
            acc_ref[...] = jnp.zeros_like(acc_ref)

        acc_ref[...] += jnp.dot(a_ref[...], w_ref[...], preferred_element_type=F32)

        @pl.when(k == nk - 1)
        def _():
            y = acc_ref[...]
            y_ref[...] = y
            z = alpha * x_ref[...] + (1.0 + g_ref[...]) * y
            zhat, _ = _ln_stats(z)
            xn = zhat * lg_ref[...] + lb_ref[...]
            xn_ref[...] = xn
            xhat, _ = _ln_stats(xn)
            h_ref[...] = (xhat * (1.0 + sc_ref[...]) + sh_ref[...]).astype(h_ref.dtype)

    row = pl.BlockSpec((1, d), lambda i, k: (0, 0))
    tile = pl.BlockSpec((tm, d), lambda i, k: (i, 0))
    return _pcall(
        body, name=name,
        out_shape=(jax.ShapeDtypeStruct((s, d), F32), jax.ShapeDtypeStruct((s, d), F32),
                   jax.ShapeDtypeStruct((s, d), BF16)),
        grid=(s // tm, nk),
        in_specs=[pl.BlockSpec((tm, tk), lambda i, k: (i, k)), pl.BlockSpec((tk, d), lambda i, k: (k, 0)),
                  tile, row, row, row, row, row],
        out_specs=(tile, tile, tile),
        scratch_shapes=[pltpu.VMEM((tm, d), F32)],
        compiler_params=_cp("parallel", "arbitrary"),
    )(a, w, xres, gate, ln_g, ln_b, sc, sh)


def _colsum(v):
    return jnp.sum(v, axis=0, keepdims=True)


def _ln_res_bwd(dxn, xres, y, gate, ln_g, alpha, name, tm=256):
    s, d = dxn.shape
    tm = min(tm, s)

    def body(dxn_ref, x_ref, y_ref, g_ref, lg_ref, dres_ref, dy_ref, sums_ref):
        @pl.when(pl.program_id(0) == 0)
        def _():
            sums_ref[...] = jnp.zeros_like(sums_ref)

        dxn_v, y_v = dxn_ref[...], y_ref[...]
        one_g = 1.0 + g_ref[...]
        zhat, rstd = _ln_stats(alpha * x_ref[...] + one_g * y_v)
        dzh = dxn_v * lg_ref[...]
        dz = rstd * (dzh - jnp.mean(dzh, axis=-1, keepdims=True)
                     - zhat * jnp.mean(dzh * zhat, axis=-1, keepdims=True))
        dres_ref[...] = alpha * dz
        dy_ref[...] = (one_g * dz).astype(dy_ref.dtype)
        sums_ref[0:1, :] += _colsum(dxn_v * zhat)
        sums_ref[1:2, :] += _colsum(dxn_v)
        sums_ref[2:3, :] += _colsum(dz * y_v)

    row = pl.BlockSpec((1, d), lambda i: (0, 0))
    tile = pl.BlockSpec((tm, d), lambda i: (i, 0))
    return _pcall(
        body, name=name,
        out_shape=(jax.ShapeDtypeStruct((s, d), F32), jax.ShapeDtypeStruct((s, d), BF16),
                   jax.ShapeDtypeStruct((8, d), F32)),
        grid=(s // tm,),
        in_specs=[tile, tile, tile, row, row],
        out_specs=(tile, tile, pl.BlockSpec((8, d), lambda i: (0, 0))),
        compiler_params=_cp("arbitrary"),
    )(dxn, xres, y, gate, ln_g)


def _ln_mod_bwd(dh, x, sc, dres, name, extra=None, tm=256):
    s, d = dh.shape
    tm = min(tm, s)

    def body(*refs):
        if extra is None:
            dh_ref, x_ref, sc_ref, dres_ref, dx_ref, sums_ref = refs
            dh_v = dh_ref[...]
        else:
            dh_ref, x_ref, sc_ref, dres_ref, dh2_ref, df_ref, wf_ref, dx_ref, sums_ref = refs
            dh_v = dh_ref[...] + dh2_ref[...] + jnp.dot(df_ref[...], wf_ref[...], preferred_element_type=F32)

        @pl.when(pl.program_id(0) == 0)
        def _():
            sums_ref[...] = jnp.zeros_like(sums_ref)

        xhat, rstd = _ln_stats(x_ref[...])
        dxh = dh_v * (1.0 + sc_ref[...])
        dx_ref[...] = dres_ref[...] + rstd * (dxh - jnp.mean(dxh, axis=-1, keepdims=True)
                                             - xhat * jnp.mean(dxh * xhat, axis=-1, keepdims=True))
        sums_ref[0:1, :] += _colsum(dh_v)
        sums_ref[1:2, :] += _colsum(dh_v * xhat)

    row = pl.BlockSpec((1, d), lambda i: (0, 0))
    tile = pl.BlockSpec((tm, d), lambda i: (i, 0))
    in_specs = [tile, tile, row, tile]
    args = [dh, x, sc, dres]
    if extra is not None:
        in_specs += [tile, pl.BlockSpec((tm, AUG), lambda i: (i, 0)), pl.BlockSpec((AUG, d), lambda i: (0, 0))]
        args += list(extra)
    return _pcall(
        body, name=name,
        out_shape=(jax.ShapeDtypeStruct((s, d), F32), jax.ShapeDtypeStruct((8, d), F32)),
        grid=(s // tm,), in_specs=in_specs,
        out_specs=(tile, pl.BlockSpec((8, d), lambda i: (0, 0))),
        compiler_params=_cp("arbitrary"),
    )(*args)


def _loss_grad(xn, target, name, tm=512):
    s, d = xn.shape
    tm = min(tm, s)
    n = s // tm

    def body(x_ref, t_ref, dx_ref, loss_ref, acc_ref):
        i = pl.program_id(0)

        @pl.when(i == 0)
        def _():
            acc_ref[...] = jnp.zeros_like(acc_ref)

        err = x_ref[...] - t_ref[...]
        dx_ref[...] = err * (1.0 / d)
        acc_ref[...] += _colsum(err * err)

        @pl.when(i == n - 1)
        def _():
            loss_ref[...] = jnp.zeros_like(loss_ref) + (0.5 / d) * jnp.sum(acc_ref[...])

    tile = pl.BlockSpec((tm, d), lambda i: (i, 0))
    return _pcall(
        body, name=name,
        out_shape=(jax.ShapeDtypeStruct((s, d), F32), jax.ShapeDtypeStruct((8, 128), F32)),
        grid=(n,), in_specs=[tile, tile],
        out_specs=(tile, pl.BlockSpec((8, 128), lambda i: (0, 0))),
        scratch_shapes=[pltpu.VMEM((1, d), F32)],
        compiler_params=_cp("arbitrary"),
    )(xn, target)


def _sigmoid(v):
    return 0.5 * jnp.tanh(0.5 * v) + 0.5


def _merge_fwd(o_a, o_c, w_a, w_c, gates, name, tm=256):
    s, wd = o_a.shape
    d = w_a.shape[1]
    tm = min(tm, s)

    def body(oa_ref, oc_ref, wa_ref, wc_ref, ga_ref, gc_ref, m_ref):
        ba = jnp.dot(oa_ref[...], wa_ref[...], preferred_element_type=F32)
        bc = jnp.dot(oc_ref[...], wc_ref[...], preferred_element_type=F32)
        m_ref[...] = (_sigmoid(ga_ref[...]) * ba + _sigmoid(gc_ref[...]) * bc).astype(m_ref.dtype)

    o_spec = pl.BlockSpec((tm, wd), lambda i: (i, 0))
    w_spec = pl.BlockSpec((wd, d), lambda i: (0, 0))
    return _pcall(
        body, name=name, out_shape=jax.ShapeDtypeStruct((s, d), BF16), grid=(s // tm,),
        in_specs=[o_spec, o_spec, w_spec, w_spec,
                  pl.BlockSpec((tm, d), lambda i: (i, 0)), pl.BlockSpec((tm, d), lambda i: (i, 1))],
        out_specs=pl.BlockSpec((tm, d), lambda i: (i, 0)),
        compiler_params=_cp("parallel"),
    )(o_a, o_c, w_a, w_c, gates, gates)


def _merge_bwd(dmix, w_out_t, o_a, o_c, w_a, w_c, w_a_t, w_c_t, gates, name, tm=256):
    s, wd = o_a.shape
    d = w_a.shape[1]
    tm = min(tm, s)

    def body(dmix_ref, wot_ref, oa_ref, oc_ref, wa_ref, wc_ref, wat_ref, wct_ref, ga_ref, gc_ref,
             dba_ref, dbc_ref, doa_ref, doc_ref, dg_ref):
        dm = jnp.dot(dmix_ref[...], wot_ref[...], preferred_element_type=F32)
        for half, (o_ref, w_ref, wt_ref, g_ref, db_ref, do_ref) in enumerate((
                (oa_ref, wa_ref, wat_ref, ga_ref, dba_ref, doa_ref),
                (oc_ref, wc_ref, wct_ref, gc_ref, dbc_ref, doc_ref))):
            b = jnp.dot(o_ref[...], w_ref[...], preferred_element_type=F32)
            sg = _sigmoid(g_ref[...])
            db = (dm * sg).astype(BF16)
            db_ref[...] = db
            dg_ref[:, d * half:d * (half + 1)] = (dm * b * sg * (1.0 - sg)).astype(dg_ref.dtype)
            do_ref[...] = jnp.dot(db, wt_ref[...], preferred_element_type=F32).astype(do_ref.dtype)

    row_d = pl.BlockSpec((tm, d), lambda i: (i, 0))
    row_w = pl.BlockSpec((tm, wd), lambda i: (i, 0))
    full = lambda shp: pl.BlockSpec(shp, lambda i: (0, 0))
    return _pcall(
        body, name=name,
        out_shape=(jax.ShapeDtypeStruct((s, d), BF16), jax.ShapeDtypeStruct((s, d), BF16),
                   jax.ShapeDtypeStruct((s, wd), BF16), jax.ShapeDtypeStruct((s, wd), BF16),
                   jax.ShapeDtypeStruct((s, 2 * d), BF16)),
        grid=(s // tm,),
        in_specs=[row_d, full((d, d)), row_w, row_w, full((wd, d)), full((wd, d)), full((d, wd)), full((d, wd)),
                  row_d, pl.BlockSpec((tm, d), lambda i: (i, 1))],
        out_specs=(row_d, row_d, row_w, row_w, pl.BlockSpec((tm, 2 * d), lambda i: (i, 0))),
        compiler_params=_cp("parallel"),
    )(dmix, w_out_t, o_a, o_c, w_a, w_c, w_a_t, w_c_t, gates, gates)


def _shift_down(cur, k, fill_rows):
    out = pltpu.roll(cur, k, axis=0)
    rid = lax.broadcasted_iota(jnp.int32, cur.shape, 0)
    for r, fill in enumerate(fill_rows):
        out = jnp.where(rid == r, fill, out)
    return out


def _shift_up(cur, k, fill_rows):
    n = cur.shape[0]
    out = pltpu.roll(cur, n - k, axis=0)
    rid = lax.broadcasted_iota(jnp.int32, cur.shape, 0)
    for r, fill in enumerate(fill_rows):
        out = jnp.where(rid == n - k + r, fill, out)
    return out


def _conv_rows(cur, prev8, first, w, b):
    p6 = jnp.where(first, 0.0, prev8[6:7, :])
    p7 = jnp.where(first, 0.0, prev8[7:8, :])
    m1 = _shift_down(cur, 1, [p7])
    m2 = _shift_down(cur, 2, [p6, p7])
    u = b + w[0:1, :] * m2 + w[1:2, :] * m1 + w[2:3, :] * cur
    return u, m1, m2


def _conv_specs(tr, tc, order):
    r8 = tr // 8
    at = lambda f: (lambda *g: f(*order(*g)))
    return [pl.BlockSpec((2, tr, tc), at(lambda i, j: (0, i, j))),
            pl.BlockSpec((2, 8, tc), at(lambda i, j: (0, jnp.maximum(i * r8 - 1, 0), j))),
            pl.BlockSpec((2, 3, tc), at(lambda i, j: (0, 0, j))),
            pl.BlockSpec((2, 1, tc), at(lambda i, j: (0, 0, j)))]


def _conv_act_fwd(u0, conv_w, conv_b, name, tr=512, tc=256):
    _, s, f = u0.shape
    tr, tc = _tile(s, tr, 8), _tile(f, tc)

    def body(u_ref, p_ref, w_ref, b_ref, act_ref):
        first = pl.program_id(0) == 0
        a, _, _ = _conv_rows(u_ref[0], p_ref[0], first, w_ref[0], b_ref[0])
        v, _, _ = _conv_rows(u_ref[1], p_ref[1], first, w_ref[1], b_ref[1])
        act_ref[...] = (a * _sigmoid(a) * v).astype(act_ref.dtype)

    return _pcall(
        body, name=name, out_shape=jax.ShapeDtypeStruct((s, f), BF16), grid=(s // tr, f // tc),
        in_specs=_conv_specs(tr, tc, lambda i, j: (i, j)),
        out_specs=pl.BlockSpec((tr, tc), lambda i, j: (i, j)),
        compiler_params=_cp("parallel", "parallel"),
    )(u0, u0, conv_w, conv_b)


def _act_grads(dact, a, v):
    sg = _sigmoid(a)
    return dact * v * sg * (1.0 + a * (1.0 - sg)), dact * a * sg


def _conv_act_bwd(dact, u0, conv_w, conv_b, name, tr=512, tc=256):
    _, s, f = u0.shape
    tr, tc = _tile(s, tr, 8), _tile(f, tc)
    n = s // tr
    r8 = tr // 8

    def body(dact_ref, dnext_ref, u_ref, p_ref, unext_ref, w_ref, b_ref, du0_ref, sums_ref):
        i = pl.program_id(1)
        first, last = i == 0, i == n - 1

        @pl.when(first)
        def _():
            sums_ref[...] = jnp.zeros_like(sums_ref)

        cur = (u_ref[0], u_ref[1])
        (a, a1, a2), (v, v1, v2) = [_conv_rows(cur[hf], p_ref[hf], first, w_ref[hf], b_ref[hf]) for hf in range(2)]
        da, dv = _act_grads(dact_ref[...], a, v)
        (an, _, _), (vn, _, _) = [_conv_rows(unext_ref[hf], cur[hf][tr - 8:tr, :], False, w_ref[hf], b_ref[hf])
                                  for hf in range(2)]
        dan, dvn = _act_grads(dnext_ref[...], an, vn)
        for hf, g, gn, shifted in ((0, da, dan, (a2, a1, cur[0])), (1, dv, dvn, (v2, v1, cur[1]))):
            w = w_ref[hf]
            n0 = jnp.where(last, 0.0, gn[0:1, :])
            n1 = jnp.where(last, 0.0, gn[1:2, :])
            du0 = w[2:3, :] * g + w[1:2, :] * _shift_up(g, 1, [n0]) + w[0:1, :] * _shift_up(g, 2, [n0, n1])
            du0_ref[hf] = du0.astype(du0_ref.dtype)
            for r in range(3):
                sums_ref[hf, r:r + 1, :] += _colsum(g * shifted[r])
            sums_ref[hf, 3:4, :] += _colsum(g)

    nxt = lambda i: jnp.minimum((i + 1) * r8, s // 8 - 1)
    order = lambda j, i: (i, j)
    specs = _conv_specs(tr, tc, order)
    return _pcall(
        body, name=name,
        out_shape=(jax.ShapeDtypeStruct((2, s, f), BF16), jax.ShapeDtypeStruct((2, 8, f), F32)),
        grid=(f // tc, n),
        in_specs=[pl.BlockSpec((tr, tc), lambda j, i: (i, j)), pl.BlockSpec((8, tc), lambda j, i: (nxt(i), j)),
                  specs[0], specs[1], pl.BlockSpec((2, 8, tc), lambda j, i: (0, nxt(i), j)), specs[2], specs[3]],
        out_specs=(pl.BlockSpec((2, tr, tc), lambda j, i: (0, i, j)), pl.BlockSpec((2, 8, tc), lambda j, i: (0, 0, j))),
        compiler_params=_cp("parallel", "arbitrary"),
    )(dact, dact, u0, u0, u0, conv_w, conv_b)


def _cumsum_rows(v, reverse=False):
    r = v.shape[0]
    i128 = lax.broadcasted_iota(jnp.int32, (128, 128), 0), lax.broadcasted_iota(jnp.int32, (128, 128), 1)
    ir = lax.broadcasted_iota(jnp.int32, (r, r), 0), lax.broadcasted_iota(jnp.int32, (r, r), 1)
    in_row = (i128[0] >= i128[1] if reverse else i128[0] <= i128[1]).astype(F32)
    rows = (ir[1] > ir[0] if reverse else ir[1] < ir[0]).astype(F32)
    within = jnp.dot(v, in_row, preferred_element_type=F32, precision=HI)
    tot = jnp.broadcast_to(within[:, 0:1] if reverse else within[:, 127:128], (r, 128))
    return within + jnp.dot(rows, tot, preferred_element_type=F32, precision=HI)


def _forget_fwd(f_t, b_f, name):
    h, r, _ = f_t.shape

    def body(f_ref, b_ref, o_ref):
        z = f_ref[...] + b_ref[...]
        logf = jnp.minimum(z, 0.0) - jnp.log(1.0 + jnp.exp(-jnp.abs(z)))
        rest = _cumsum_rows(logf)
        for i in range(N_SPLIT):
            piece = rest.astype(BF16).astype(F32)
            o_ref[i] = piece
            rest = rest - piece

    return _pcall(
        body, name=name, out_shape=jax.ShapeDtypeStruct((N_SPLIT, h, r, 128), F32), grid=(h,),
        in_specs=[pl.BlockSpec((None, r, 128), lambda i: (i, 0, 0)), pl.BlockSpec((None, 1, 128), lambda i: (i, 0, 0))],
        out_specs=pl.BlockSpec((N_SPLIT, None, r, 128), lambda i: (0, i, 0, 0)),
        compiler_params=_cp("parallel"),
    )(f_t, b_f)


def _forget_bwd(d_cum, f_t, b_f, name):
    h, r, _ = f_t.shape

    def body(g_ref, f_ref, b_ref, df_ref, db_ref):
        df = _cumsum_rows(g_ref[...], reverse=True) * _sigmoid(-(f_ref[...] + b_ref[...]))
        df_ref[...] = df
        db_ref[...] = jnp.zeros_like(db_ref) + jnp.sum(df)

    blk = pl.BlockSpec((None, r, 128), lambda i: (i, 0, 0))
    one = pl.BlockSpec((None, 1, 128), lambda i: (i, 0, 0))
    return _pcall(
        body, name=name,
        out_shape=(jax.ShapeDtypeStruct((h, r, 128), F32), jax.ShapeDtypeStruct((h, 1, 128), F32)),
        grid=(h,), in_specs=[blk, blk, one], out_specs=(blk, one),
        compiler_params=_cp("parallel"),
    )(d_cum, f_t, b_f)


_NT = (((1,), (1,)), ((), ()))


def _causal_keep(tk, tq):
    return lax.broadcasted_iota(jnp.int32, (tk, tq), 0) <= lax.broadcasted_iota(jnp.int32, (tk, tq), 1)


_TN = (((0,), (0,)), ((), ()))
PAIR = 2
V_FOX = 2 * WIDTH // 128
Q_CHUNK, K_CHUNK, V_CHUNK = (3 * WIDTH) // 128, (4 * WIDTH) // 128, (5 * WIDTH) // 128


def _riding_exchange(exchange, n_in, n_out, grid):
    if exchange is None:
        return [], [], [], [], lambda refs: (refs, lambda: None)
    scatter, xs = exchange
    n = len(xs)
    hbm = pl.BlockSpec(memory_space=pltpu.HBM)

    def hook(refs):
        ins, x_refs = refs[:n_in], refs[n_in:n_in + n]
        outs, land_refs = refs[n_in + n:n_in + n + n_out], refs[n_in + n + n_out:n_in + 2 * n + n_out]
        rest = refs[n_in + 2 * n + n_out:]
        own_scratch, sems = rest[:len(rest) - 3], rest[len(rest) - 3:]
        start, wait = _direct_exchange(scatter, x_refs, land_refs, *sems)
        ids = [pl.program_id(a) for a in range(len(grid))]
        is_first = functools.reduce(jnp.logical_and, [i == 0 for i in ids])
        is_last = functools.reduce(jnp.logical_and, [i == g - 1 for i, g in zip(ids, grid)])
        pl.when(is_first)(start)
        return tuple(ins) + tuple(outs) + tuple(own_scratch), lambda: pl.when(is_last)(wait)

    return list(xs), _exchange_shapes(scatter, xs), [hbm] * n, _exchange_scratch(n), hook


def _fox_fwd(q_aug, k_aug, qkv, name, t=TQ, exchange=None):
    s = q_aug.shape[0]
    t = min(t, s)
    n = s // t
    dh = HEAD_DIM
    grid = (N_HEADS // PAIR, n)
    ex_in, ex_out, ex_specs, ex_scratch, hook = _riding_exchange(exchange, 3, 2, grid)

    def body(*refs):
        (q_ref, k_ref, v_ref, o_ref, lse_ref, m_s, l_s, acc_s), finish = hook(refs)
        qi = pl.program_id(1)
        m_s[...] = jnp.full_like(m_s, NEG)
        l_s[...] = jnp.zeros_like(l_s)
        acc_s[...] = jnp.zeros_like(acc_s)

        def step(kj, diag):
            rows = pl.ds(pl.multiple_of(kj * t, t), t)
            for hh in range(PAIR):
                st = lax.dot_general(k_ref[rows, AUG * hh:AUG * (hh + 1)], q_ref[:, AUG * hh:AUG * (hh + 1)], _NT,
                                     preferred_element_type=F32)
                if diag:
                    st = jnp.where(_causal_keep(t, t), st, NEG)
                m_prev = m_s[hh]
                m_new = jnp.maximum(m_prev, jnp.max(st, axis=0, keepdims=True))
                a = jnp.exp(m_prev - m_new)
                pt = jnp.exp(st - m_new)
                l_s[hh] = a * l_s[hh] + jnp.sum(pt, axis=0, keepdims=True)
                acc_s[hh] = a * acc_s[hh] + lax.dot_general(v_ref[rows, dh * hh:dh * (hh + 1)], pt.astype(BF16), _TN,
                                                            preferred_element_type=F32)
                m_s[hh] = m_new

        def off_diagonal(kj, carry):
            step(kj, False)
            return carry

        lax.fori_loop(0, qi, off_diagonal, 0)
        step(qi, True)
        for hh in range(PAIR):
            o_ref[hh] = acc_s[hh] / l_s[hh]
            lse_ref[hh, 0] = m_s[hh] + jnp.log(l_s[hh])
        finish()

    return _pcall(
        body, name=name,
        out_shape=[jax.ShapeDtypeStruct((N_HEADS, dh, s), F32), jax.ShapeDtypeStruct((N_HEADS, n, 1, t), F32)] + ex_out,
        grid=grid,
        in_specs=[pl.BlockSpec((t, PAIR * AUG), lambda hp, i: (i, hp)),
                  pl.BlockSpec((s, PAIR * AUG), lambda hp, i: (0, hp)),
                  pl.BlockSpec((s, PAIR * dh), lambda hp, i: (0, V_FOX + hp))] + ex_specs,
        out_specs=[pl.BlockSpec((PAIR, dh, t), lambda hp, i: (hp, 0, i)),
                   pl.BlockSpec((PAIR, 1, 1, t), lambda hp, i: (hp, i, 0, 0))] + ex_specs,
        scratch_shapes=[pltpu.VMEM((PAIR, 1, t), F32), pltpu.VMEM((PAIR, 1, t), F32),
                        pltpu.VMEM((PAIR, dh, t), F32)] + ex_scratch,
        compiler_params=_cp("arbitrary", "arbitrary"),
    )(q_aug, k_aug, qkv, *ex_in)


def _row_dot(a_t, b_t, name, t=2048):
    h, dh, s = a_t.shape
    t = min(t, s)

    def body(a_ref, b_ref, o_ref):
        o_ref[...] = jnp.sum(a_ref[...].astype(F32) * b_ref[...], axis=0, keepdims=True)

    blk = pl.BlockSpec((None, dh, t), lambda hh, i: (hh, 0, i))
    return _pcall(
        body, name=name, out_shape=jax.ShapeDtypeStruct((h, 1, s), F32), grid=(h, s // t),
        in_specs=[blk, blk], out_specs=pl.BlockSpec((None, 1, t), lambda hh, i: (hh, 0, i)),
        compiler_params=_cp("parallel", "parallel"),
    )(a_t, b_t)


def _fox_bwd(q_aug, k_aug, qkv, do, lse, delta, name, t=TQ, exchange=None):
    s = q_aug.shape[0]
    dh = HEAD_DIM
    t = min(t, s)
    n = s // t
    grid = (N_HEADS // PAIR, n)
    ex_in, ex_out, ex_specs, ex_scratch, hook = _riding_exchange(exchange, 6, 5, grid)

    def body(*refs):
        (q_ref, k_ref, v_ref, do_ref, lse_ref, dl_ref, dk_ref, dv_ref, dq_ref, dsum_ref, dqsum_ref,
         dk_s, dv_s, dsum_s), finish = hook(refs)
        kj = pl.program_id(1)

        @pl.when(kj == 0)
        def _():
            dq_ref[...] = jnp.zeros_like(dq_ref)
            dqsum_ref[...] = jnp.zeros_like(dqsum_ref)

        dk_s[...] = jnp.zeros_like(dk_s)
        dv_s[...] = jnp.zeros_like(dv_s)
        dsum_s[...] = jnp.zeros_like(dsum_s)

        def step(qi, diag):
            rows = pl.ds(pl.multiple_of(qi * t, t), t)
            for hh in range(PAIR):
                q = q_ref[rows, AUG * hh:AUG * (hh + 1)]
                k = k_ref[:, AUG * hh:AUG * (hh + 1)]
                st = lax.dot_general(k, q, _NT, preferred_element_type=F32)
                if diag:
                    st = jnp.where(_causal_keep(t, t), st, NEG)
                pt = jnp.exp(st - lse_ref[hh, qi])
                do_v = do_ref[rows, dh * hh:dh * (hh + 1)]
                dpt = lax.dot_general(v_ref[:, dh * hh:dh * (hh + 1)], do_v, _NT, preferred_element_type=F32)
                ds32 = pt * (dpt - dl_ref[hh, qi])
                dsum_s[hh] += sum(ds32[:, 128 * u:128 * (u + 1)] for u in range(t // 128))
                dqsum_ref[hh, qi] += jnp.sum(ds32, axis=0, keepdims=True)
                dst = ds32.astype(BF16)
                dv_s[hh] += jnp.dot(pt.astype(BF16), do_v, preferred_element_type=F32)
                dk_s[hh] += jnp.dot(dst, q[:, :dh], preferred_element_type=F32)
                dq_ref[hh, qi] += lax.dot_general(k[:, :dh], dst, _TN, preferred_element_type=F32)

        def off_diagonal(qi, carry):
            step(qi, False)
            return carry

        step(kj, True)
        lax.fori_loop(kj + 1, n, off_diagonal, 0)
        for hh in range(PAIR):
            dk_ref[:, dh * hh:dh * (hh + 1)] = dk_s[hh].astype(dk_ref.dtype)
            dv_ref[:, dh * hh:dh * (hh + 1)] = dv_s[hh].astype(dv_ref.dtype)
            dsum_ref[:, 128 * hh:128 * (hh + 1)] = dsum_s[hh]
        finish()

    stat = pl.BlockSpec((PAIR, n, 1, t), lambda hp, j: (hp, 0, 0, 0))
    return _pcall(
        body, name=name,
        out_shape=[jax.ShapeDtypeStruct((s, WIDTH), BF16), jax.ShapeDtypeStruct((s, WIDTH), BF16),
                   jax.ShapeDtypeStruct((N_HEADS, n, dh, t), F32), jax.ShapeDtypeStruct((s, N_HEADS * 128), F32),
                   jax.ShapeDtypeStruct((N_HEADS, n, 1, t), F32)] + ex_out,
        grid=grid,
        in_specs=[pl.BlockSpec((s, PAIR * AUG), lambda hp, j: (0, hp)),
                  pl.BlockSpec((t, PAIR * AUG), lambda hp, j: (j, hp)),
                  pl.BlockSpec((t, PAIR * dh), lambda hp, j: (j, V_FOX + hp)),
                  pl.BlockSpec((s, PAIR * dh), lambda hp, j: (0, hp)),
                  stat, stat] + ex_specs,
        out_specs=[pl.BlockSpec((t, PAIR * dh), lambda hp, j: (j, hp)),
                   pl.BlockSpec((t, PAIR * dh), lambda hp, j: (j, hp)),
                   pl.BlockSpec((PAIR, n, dh, t), lambda hp, j: (hp, 0, 0, 0)),
                   pl.BlockSpec((t, PAIR * 128), lambda hp, j: (j, hp)),
                   stat] + ex_specs,
        scratch_shapes=[pltpu.VMEM((PAIR, t, dh), F32), pltpu.VMEM((PAIR, t, dh), F32),
                        pltpu.VMEM((PAIR, t, 128), F32)] + ex_scratch,
        compiler_params=_cp("arbitrary", "arbitrary"),
    )(q_aug, k_aug, qkv, do, lse, delta, *ex_in)


N_SUB = TQ // SUB


def _chunk_logits(kw, qs, b_ref, hh, first_tile, j):
    bias = b_ref[hh, jnp.where(first_tile, 1 + j, 0)]
    return lax.dot_general(kw[SUB * j:SUB * j + SUBK], qs, _NT, preferred_element_type=F32) + bias


def _window(prev_ref, cur_ref, hh):
    cols = slice(HEAD_DIM * hh, HEAD_DIM * (hh + 1))
    return jnp.concatenate([prev_ref[:, cols], cur_ref[:, cols]], axis=0)


def _chunk_fwd(qkv, bias_t, name):
    s = qkv.shape[0]
    dh = HEAD_DIM
    n = s // TQ
    nsub = TQ // SUB
    prev = lambda i: jnp.maximum(i - 1, 0)

    def body(q_ref, kp_ref, kc_ref, vp_ref, vc_ref, b_ref, o_ref, lse_ref):
        first_tile = pl.program_id(1) == 0
        for hh in range(PAIR):
            kw, vw = _window(kp_ref, kc_ref, hh), _window(vp_ref, vc_ref, hh)
            for j in range(nsub):
                qs = q_ref[SUB * j:SUB * (j + 1), dh * hh:dh * (hh + 1)] * QK_SCALE
                st = _chunk_logits(kw, qs, b_ref, hh, first_tile, j)
                m = jnp.max(st, axis=0, keepdims=True)
                pt = jnp.exp(st - m)
                l = jnp.sum(pt, axis=0, keepdims=True)
                ot = lax.dot_general(vw[SUB * j:SUB * j + SUBK], pt.astype(BF16), _TN, preferred_element_type=F32)
                o_ref[hh, :, SUB * j:SUB * (j + 1)] = ot / l
                lse_ref[hh, 0, :, SUB * j:SUB * (j + 1)] = m + jnp.log(l)

    blk = lambda col, m: pl.BlockSpec((TQ, PAIR * dh), lambda hp, i: (m(i), col + hp))
    same = lambda i: i
    return _pcall(
        body, name=name,
        out_shape=(jax.ShapeDtypeStruct((N_HEADS, dh, s), F32), jax.ShapeDtypeStruct((N_HEADS, n, 1, TQ), F32)),
        grid=(N_HEADS // PAIR, n),
        in_specs=[blk(Q_CHUNK, same), blk(K_CHUNK, prev), blk(K_CHUNK, same), blk(V_CHUNK, prev), blk(V_CHUNK, same),
                  pl.BlockSpec((PAIR, 1 + N_SUB, SUBK, SUB), lambda hp, i: (hp, 0, 0, 0))],
        out_specs=(pl.BlockSpec((PAIR, dh, TQ), lambda hp, i: (hp, 0, i)),
                   pl.BlockSpec((PAIR, 1, 1, TQ), lambda hp, i: (hp, i, 0, 0))),
        compiler_params=_cp("parallel", "parallel"),
    )(qkv, qkv, qkv, qkv, qkv, bias_t)


def _chunk_bwd(qkv, do, lse, delta, bias_t, name):
    s = qkv.shape[0]
    dh = HEAD_DIM
    n = s // TQ
    nsub = TQ // SUB
    cur = lambda i: jnp.minimum(i, n - 1)
    prev = lambda i: jnp.maximum(cur(i) - 1, 0)
    done = lambda i: jnp.maximum(i - 1, 0)

    def body(q_ref, kp_ref, kc_ref, vp_ref, vc_ref, do_ref, lse_ref, dl_ref, b_ref,
             dq_ref, dk_ref, dv_ref, db_ref, dkw_s, dvw_s, ck_s, cv_s):
        i = pl.program_id(1)

        @pl.when(i == 0)
        def _():
            db_ref[...] = jnp.zeros_like(db_ref)
            ck_s[...] = jnp.zeros_like(ck_s)
            cv_s[...] = jnp.zeros_like(cv_s)

        dkw_s[...] = jnp.zeros_like(dkw_s)
        dvw_s[...] = jnp.zeros_like(dvw_s)

        @pl.when(i < n)
        def _():
            first_tile = i == 0
            for hh in range(PAIR):
                kw, vw = _window(kp_ref, kc_ref, hh), _window(vp_ref, vc_ref, hh)
                for j in range(nsub):
                    rows = pl.ds(SUB * j, SUB)
                    win = pl.ds(SUB * j, SUBK)
                    qs = q_ref[rows, dh * hh:dh * (hh + 1)] * QK_SCALE
                    st = _chunk_logits(kw, qs, b_ref, hh, first_tile, j)
                    pt = jnp.exp(st - lse_ref[hh, 0, :, rows])
                    do_j = do_ref[rows, dh * hh:dh * (hh + 1)]
                    dpt = lax.dot_general(vw[SUB * j:SUB * j + SUBK], do_j, _NT, preferred_element_type=F32)
                    dst = pt * (dpt - dl_ref[hh, 0, :, rows])
                    db_ref[hh] += dst
                    dsb = dst.astype(BF16)
                    dvw_s[hh, win, :] += jnp.dot(pt.astype(BF16), do_j, preferred_element_type=F32)
                    dkw_s[hh, win, :] += jnp.dot(dsb, qs, preferred_element_type=F32)
                    dq_ref[hh, :, rows] = QK_SCALE * lax.dot_general(kw[SUB * j:SUB * j + SUBK], dsb, _TN,
                                                                     preferred_element_type=F32)

        for hh in range(PAIR):
            cols = slice(dh * hh, dh * (hh + 1))
            dk_ref[:, cols] = (ck_s[hh] + dkw_s[hh, 0:TQ, :]).astype(dk_ref.dtype)
            dv_ref[:, cols] = (cv_s[hh] + dvw_s[hh, 0:TQ, :]).astype(dv_ref.dtype)
        ck_s[...] = dkw_s[:, TQ:2 * TQ, :]
        cv_s[...] = dvw_s[:, TQ:2 * TQ, :]

    blk = lambda col, m: pl.BlockSpec((TQ, PAIR * dh), lambda hp, i: (m(i), col + hp))
    stat = pl.BlockSpec((PAIR, 1, 1, TQ), lambda hp, i: (hp, cur(i), 0, 0))
    bs = pl.BlockSpec((PAIR, SUBK, SUB), lambda hp, i: (hp, 0, 0))
    return _pcall(
        body, name=name,
        out_shape=(jax.ShapeDtypeStruct((N_HEADS, dh, s), F32), jax.ShapeDtypeStruct((s, WIDTH), BF16),
                   jax.ShapeDtypeStruct((s, WIDTH), BF16), jax.ShapeDtypeStruct((N_HEADS, SUBK, SUB), F32)),
        grid=(N_HEADS // PAIR, n + 1),
        in_specs=[blk(Q_CHUNK, cur), blk(K_CHUNK, prev), blk(K_CHUNK, cur), blk(V_CHUNK, prev), blk(V_CHUNK, cur),
                  blk(0, cur), stat, stat, pl.BlockSpec((PAIR, 1 + N_SUB, SUBK, SUB), lambda hp, i: (hp, 0, 0, 0))],
        out_specs=(pl.BlockSpec((PAIR, dh, TQ), lambda hp, i: (hp, 0, cur(i))), blk(0, done), blk(0, done), bs),
        scratch_shapes=[pltpu.VMEM((PAIR, 2 * TQ, dh), F32), pltpu.VMEM((PAIR, 2 * TQ, dh), F32),
                        pltpu.VMEM((PAIR, TQ, dh), F32), pltpu.VMEM((PAIR, TQ, dh), F32)],
        compiler_params=_cp("parallel", "arbitrary"),
    )(qkv, qkv, qkv, qkv, qkv, do, lse, delta, bias_t)


def _mod_part(c_all, w_ada, b_ada, name):
    nl, d, n = w_ada.shape
    b = c_all.shape[0]

    def body(c_ref, w_ref, b_ref, o_ref, cond_ref):
        cv = c_ref[...]
        cond = cv * _sigmoid(cv)
        cond_ref[...] = cond
        o_ref[...] = jnp.dot(cond, w_ref[...], preferred_element_type=F32, precision=HI) + b_ref[...]

    return _pcall(
        body, name=name,
        out_shape=(jax.ShapeDtypeStruct((nl, b, n), F32), jax.ShapeDtypeStruct((b, d), F32)),
        grid=(nl,),
        in_specs=[pl.BlockSpec((b, d), lambda l: (0, 0)), pl.BlockSpec((None, d, n), lambda l: (l, 0, 0)),
                  pl.BlockSpec((None, 1, n), lambda l: (l, 0, 0))],
        out_specs=(pl.BlockSpec((None, b, n), lambda l: (l, 0, 0)), pl.BlockSpec((b, d), lambda l: (0, 0))),
        compiler_params=_cp("arbitrary"),
    )(c_all, w_ada, b_ada)


def _adamw(parts, w, m, v, name, tr=1024):
    p, rl, cp = parts[0].shape
    r, c = w.shape
    tr = _tile(rl, tr, 16)
    per = rl // tr
    c1 = 1.0 / (1.0 - ADAM_B1 ** ADAM_STEP)
    c2 = 1.0 / (1.0 - ADAM_B2 ** ADAM_STEP)

    def body(*refs):
        p_refs, (w_ref, m_ref, v_ref, g_ref, d_ref, nm_ref, nv_ref) = refs[:len(parts)], refs[len(parts):]
        for which, p_ref in enumerate(p_refs):
            pl.when(pl.program_id(0) // per == which)(
                functools.partial(update, p_ref, w_ref, m_ref, v_ref, g_ref, d_ref, nm_ref, nv_ref))

    def update(p_ref, w_ref, m_ref, v_ref, g_ref, d_ref, nm_ref, nv_ref):
        g = p_ref[0].astype(F32)
        for i in range(1, p):
            g = g + p_ref[i].astype(F32)
        g = g[:, :c]
        nm = ADAM_B1 * m_ref[...] + (1.0 - ADAM_B1) * g
        nv = ADAM_B2 * v_ref[...] + (1.0 - ADAM_B2) * (g * g)
        g_ref[...] = g
        nm_ref[...] = nm
        nv_ref[...] = nv
        d_ref[...] = -ADAM_LR * ((nm * c1) / (jnp.sqrt(nv * c2) + ADAM_EPS) + ADAM_WD * w_ref[...])

    blk = pl.BlockSpec((tr, c), lambda i: (i, 0))
    out = jax.ShapeDtypeStruct((r, c), F32)
    return _pcall(
        body, name=name, out_shape=(out, out, out, out), grid=(r // tr,),
        in_specs=[pl.BlockSpec((p, tr, cp), lambda i, _w=which: (0, jnp.clip(i - _w * per, 0, per - 1), 0))
                  for which in range(len(parts))] + [blk, blk, blk],
        out_specs=(blk, blk, blk, blk),
        compiler_params=_cp("parallel"),
    )(*parts, w, m, v)


def _pad_lanes(a):
    pad = (-a.shape[-1]) % 128
    return jnp.pad(a, [(0, 0)] * (a.ndim - 1) + [(0, pad)]) if pad else a


def _rows128(a, mult=16):
    flat = a.reshape(-1)
    n = flat.shape[0]
    per = 128 * mult
    pad = (-n) % per
    if pad:
        flat = jnp.concatenate([flat, jnp.zeros((pad,), a.dtype)])
    return flat.reshape(-1, 128)


def _pack(arrs, mult=16):
    pieces = [_rows128(a, mult) for a in arrs]
    return jnp.concatenate(pieces, axis=0), [p.shape[0] for p in pieces]


def _unpack(packed, rows, shapes, lead=()):
    out, at = [], 0
    for r, shp in zip(rows, shapes):
        n = int(np.prod(shp))
        piece = packed[..., at:at + r, :].reshape(lead + (r * 128,))[..., :n]
        out.append(piece.reshape(lead + tuple(shp)))
        at += r
    return out


def _rel_bias_tile(table):
    h = table.shape[0]
    lo = REL_CLIP - (CHUNK - 1)
    n_far = LEFT_CHUNKS * CHUNK + CHUNK - 1 - REL_CLIP
    vec = jnp.concatenate([table[:, lo:2 * REL_CLIP], jnp.repeat(table[:, 2 * REL_CLIP:], n_far + 1, axis=1)], axis=1)
    rev = vec[:, ::-1]
    n_vec = BAND + CHUNK - 1
    skew = jnp.tile(rev, (1, CHUNK + 1))[:, :CHUNK * (n_vec + 1)].reshape(h, CHUNK, n_vec + 1)
    bias = skew[:, ::-1, :BAND]
    neg = jnp.full((h, CHUNK, CHUNK), NEG, F32)
    two = jnp.concatenate([jnp.concatenate([bias, neg], axis=2), jnp.concatenate([neg, bias], axis=2)], axis=1)
    plain = two.transpose(0, 2, 1)
    key = np.arange(SUBK)[:, None]
    return jnp.stack([plain] + [jnp.where(key >= TQ - SUB * j, plain, NEG) for j in range(N_SUB)], axis=1)


def _rel_bias_tile_grad(dbias_t):
    h = dbias_t.shape[0]
    two = dbias_t.transpose(0, 2, 1)
    dbias = two[:, :CHUNK, :BAND] + two[:, CHUNK:, CHUNK:]
    n_vec = BAND + CHUNK - 1
    dskew = jnp.pad(dbias[:, ::-1, :], ((0, 0), (0, 0), (0, n_vec + 1 - BAND))).reshape(h, CHUNK * (n_vec + 1))
    dskew = jnp.pad(dskew, ((0, 0), (0, (CHUNK + 1) * n_vec - CHUNK * (n_vec + 1))))
    drev = jnp.sum(dskew.reshape(h, CHUNK + 1, n_vec), axis=1)
    dvec = drev[:, ::-1]
    lo = REL_CLIP - (CHUNK - 1)
    n_near = 2 * REL_CLIP - lo
    return jnp.concatenate([jnp.zeros((h, lo), F32), dvec[:, :n_near],
                            jnp.sum(dvec[:, n_near:], axis=1, keepdims=True)], axis=1)


BIG = ("w_in", "w_br_fox", "w_br_chunk", "w_out", "w_up", "w_down")
SMALL = ("b_f", "rel_bias", "conv_b", "b_ada", "ln1_g", "ln1_b", "ln2_g", "ln2_b")
ORDER = ("w_in", "b_f", "rel_bias", "w_br_fox", "w_br_chunk", "w_out", "w_up", "conv_w", "conv_b", "w_down",
         "w_ada", "b_ada", "ln1_g", "ln1_b", "ln2_g", "ln2_b")


def _dest_major(g, axis):
    shp = g.shape
    g = g.reshape(shp[:axis] + (N_DEV, shp[axis] // N_DEV) + shp[axis + 1:])
    return jnp.moveaxis(g, axis, 0)


EARLY = ("w_down", "w_up", "w_out", "w_br_fox", "w_br_chunk")
LATE = ("w_in",)
REST = ("w_br_fox", "w_br_chunk", "w_out", "w_up", "w_down")


def _shard_axis(n):
    return 0 if n in ("w_out", "w_down") else 1


def _slab(g, n):
    return _pad_lanes(_dest_major(g, _shard_axis(n)).astype(BF16))


def _layer_fwd(x, h1, wl, rest_of, mod, next_mod, alpha, gather):
    sh1, sc1, g1, sh2, sc2, g2 = mod
    s = x.shape[0]
    qkv = _mm(h1, wl["w_qkv"], out_dtype=BF16, name="proj_qkv", tk=1024)
    gates = _mm(h1, wl["w_gates"], out_dtype=F32, name="proj_gates", tk=1024)
    f_t = _mm(wl["w_f_t"], h1, out_dtype=F32, name="proj_f", tb=True, tk=1024)[:N_HEADS]
    f_t = f_t.reshape(N_HEADS, s // 128, 128)
    b_f = jnp.broadcast_to(wl["b_f"].reshape(N_HEADS, 1, 1), (N_HEADS, 1, 128))
    cum = _forget_fwd(f_t, b_f, "forget_fwd").reshape(N_SPLIT, N_HEADS, s)
    cum_cols = cum.transpose(2, 1, 0).astype(BF16)

    ones = jnp.ones((s, N_HEADS, N_SPLIT), BF16)
    zeros = jnp.zeros((s, N_HEADS, AUG - HEAD_DIM - 2 * N_SPLIT), BF16)
    q_a = qkv[:, :WIDTH].reshape(s, N_HEADS, HEAD_DIM) * QK_SCALE
    k_a = qkv[:, WIDTH:2 * WIDTH].reshape(s, N_HEADS, HEAD_DIM)
    q_aug = jnp.concatenate([q_a, cum_cols, ones, zeros], axis=-1).reshape(s, N_HEADS * AUG)
    k_aug = jnp.concatenate([k_a, ones, -cum_cols, zeros], axis=-1).reshape(s, N_HEADS * AUG)
    o_a_t, lse_a, *gathered = _fox_fwd(q_aug, k_aug, qkv, "fox_fwd_gather_%d" % len(gather), exchange=(False, gather))
    wl.update(rest_of(gathered[:len(REST)]))
    gathered = gathered[len(REST):]
    bias_t = _rel_bias_tile(wl["rel_bias"])
    o_c_t, lse_c = _chunk_fwd(qkv, bias_t, "chunk_fwd")
    o_a = o_a_t.transpose(2, 0, 1).reshape(s, WIDTH).astype(BF16)
    o_c = o_c_t.transpose(2, 0, 1).reshape(s, WIDTH).astype(BF16)

    merged = _merge_fwd(o_a, o_c, wl["w_br_fox"], wl["w_br_chunk"], gates, "merge_fwd")
    mix, x1, h2 = _mm_res_ln(merged, wl["w_out"], x, g1, wl["ln1_g"], wl["ln1_b"], sc2, sh2, alpha, "out_ln1")
    u0 = _mm(h2, wl["w_up"], out_dtype=F32, name="ffn_up", tk=1024, tn=1408, o_halves=True)
    act = _conv_act_fwd(u0, wl["conv_w"], wl["conv_b"], "conv_act_fwd")
    y2, x2, h_next = _mm_res_ln(act, wl["w_down"], x1, g2, wl["ln2_g"], wl["ln2_b"], next_mod[0], next_mod[1],
                                alpha, "down_ln2", tk=1408)
    saved = dict(x=x, h1=h1, gates=gates, f_t=f_t, b_f=b_f, q_aug=q_aug, k_aug=k_aug, qkv=qkv,
                 bias_t=bias_t, o_a_t=o_a_t, o_c_t=o_c_t, lse_a=lse_a, lse_c=lse_c,
                 o_a=o_a, o_c=o_c, merged=merged, mix=mix, x1=x1, h2=h2, u0=u0, act=act, y2=y2)
    return x2, h_next, saved, gathered


def _layer_bwd(dx2, sv, wl, mod, alpha, send, ride_late):
    sh1, sc1, g1, sh2, sc2, g2 = mod
    s = dx2.shape[0]
    g = {}
    dres2, dy2, sums = _ln_res_bwd(dx2, sv["x1"], sv["y2"], g2, wl["ln2_g"], alpha, "ln2_bwd")
    g["ln2_g"], g["ln2_b"], dg2 = sums[0], sums[1], sums[2]
    dact = _mm(dy2, wl["w_down_t"], out_dtype=F32, name="dact", tk=1024, tn=1408)
    g["w_down"] = _mm(sv["act"], dy2, out_dtype=F32, name="g_w_down", ta=True, tm=1408, tk=2048)
    du0, csum = _conv_act_bwd(dact, sv["u0"], wl["conv_w"], wl["conv_b"], "conv_act_bwd")
    g["conv_w"] = jnp.concatenate([csum[0, 0:3], csum[1, 0:3]], axis=1)
    g["conv_b"] = jnp.concatenate([csum[0, 3], csum[1, 3]])
    dh2 = _mm(du0, wl["w_up_t"], out_dtype=F32, name="dh2", tk=2816, a_halves=True)
    g["w_up"] = _mm(sv["h2"], du0, out_dtype=F32, name="g_w_up", ta=True, tk=2048, tn=1408, b_halves=True)
    dx1, sums = _ln_mod_bwd(dh2, sv["x1"], sc2, dres2, "ln_mod2_bwd")
    dsh2, dsc2 = sums[0], sums[1]

    dres1, dmix, sums = _ln_res_bwd(dx1, sv["x"], sv["mix"], g1, wl["ln1_g"], alpha, "ln1_bwd")
    g["ln1_g"], g["ln1_b"], dg1 = sums[0], sums[1], sums[2]
    g["w_out"] = _mm(sv["merged"], dmix, out_dtype=F32, name="g_w_out", ta=True, tk=2048)
    dba, dbc, do_a, do_c, dgates = _merge_bwd(
        dmix, wl["w_out_t"], sv["o_a"], sv["o_c"], wl["w_br_fox"], wl["w_br_chunk"],
        wl["w_br_fox_t"], wl["w_br_chunk_t"], sv["gates"], "merge_bwd")
    g["w_br_fox"] = _mm(sv["o_a"], dba, out_dtype=F32, name="g_w_br_fox", ta=True, tk=2048)
    g["w_br_chunk"] = _mm(sv["o_c"], dbc, out_dtype=F32, name="g_w_br_chunk", ta=True, tk=2048)
    n_t = s // min(TQ, s)
    stats = lambda a: a.reshape(N_HEADS, n_t, 1, s // n_t)

    delta_a = _row_dot(do_a.T.reshape(N_HEADS, HEAD_DIM, s), sv["o_a_t"], "delta_fox")
    send = list(send) + [_slab(g[n], n) for n in EARLY]
    dk_a, dv_a, dq_a_t, dsum, dqsum, *landed = _fox_bwd(sv["q_aug"], sv["k_aug"], sv["qkv"], do_a, sv["lse_a"],
                                                        stats(delta_a), "fox_bwd_scatter_%d" % len(send),
                                                        exchange=(True, send))
    dq_a = (dq_a_t * QK_SCALE).transpose(1, 3, 0, 2).reshape(s, WIDTH).astype(BF16)
    d_cum = dqsum.reshape(N_HEADS, s) - jnp.sum(dsum.reshape(s, N_HEADS, 128), axis=-1).T
    df_t, db_f = _forget_bwd(d_cum.reshape(N_HEADS, s // 128, 128), sv["f_t"], sv["b_f"], "forget_bwd")
    g["b_f"] = db_f[:, 0, 0]
    df_t = df_t.reshape(N_HEADS, s)

    delta_c = _row_dot(do_c.T.reshape(N_HEADS, HEAD_DIM, s), sv["o_c_t"], "delta_chunk")
    dq_c_t, dk_c, dv_c, dbias_t = _chunk_bwd(sv["qkv"], do_c, sv["lse_c"], stats(delta_c), sv["bias_t"], "chunk_bwd")
    g["rel_bias"] = _rel_bias_tile_grad(dbias_t)
    dq_c = dq_c_t.transpose(2, 0, 1).reshape(s, WIDTH).astype(BF16)

    dqkv = jnp.concatenate([dq_a, dk_a, dv_a, dq_c, dk_c, dv_c], axis=1)
    df_pad = jnp.zeros((16 - N_HEADS, s), F32)
    df16 = jnp.concatenate([df_t, df_pad], axis=0).astype(BF16)
    df_cols = jnp.concatenate([df16.T, jnp.zeros((s, AUG - 16), BF16)], axis=1)

    g_qkv = _mm(sv["h1"], dqkv, out_dtype=F32, name="g_w_qkv", ta=True, tk=2048)
    g_gates = _mm(sv["h1"], dgates, out_dtype=F32, name="g_w_gates", ta=True, tk=2048)
    g_f_t = _mm(df16, sv["h1"], out_dtype=F32, name="g_w_f", tk=1024)[:N_HEADS]
    g["w_in"] = jnp.concatenate([g_qkv[:, :3 * WIDTH], g_f_t.T, g_qkv[:, 3 * WIDTH:], g_gates], axis=1)
    late = [_slab(g[n], n) for n in LATE] if ride_late else None
    dh1, *landed_late = _mm(dqkv, wl["w_qkv_t"], out_dtype=F32, name="dh1_qkv_scatter", tk=1536,
                            exchange=(True, late)) if ride_late else \
        [_mm(dqkv, wl["w_qkv_t"], out_dtype=F32, name="dh1_qkv", tk=1536)]
    dh1g = _mm(dgates, wl["w_gates_t"], out_dtype=F32, name="dh1_gates", tk=2048)
    dx, sums = _ln_mod_bwd(dh1, sv["x"], sc1, dres1, "ln_mod1_bwd", extra=(dh1g, df_cols, wl["w_f_pad"]))
    dsh1, dsc1 = sums[0], sums[1]
    g["mod"] = jnp.concatenate([dsh1, dsc1, dg1, dsh2, dsc2, dg2])
    return dx, g, landed, landed_late


def kernel(x, c, w_in, b_f, rel_bias, w_br_fox, w_br_chunk, w_out, w_up, conv_w, conv_b, w_down, w_ada, b_ada, ln1_g, ln1_b, ln2_g, ln2_b, loss_target, m_w_in, m_b_f, m_rel_bias, m_w_br_fox, m_w_br_chunk, m_w_out, m_w_up, m_conv_w, m_conv_b, m_w_down, m_w_ada, m_b_ada, m_ln1_g, m_ln1_b, m_ln2_g, m_ln2_b, v_w_in, v_b_f, v_rel_bias, v_w_br_fox, v_w_br_chunk, v_w_out, v_w_up, v_conv_w, v_conv_b, v_w_down, v_w_ada, v_b_ada, v_ln1_g, v_ln1_b, v_ln2_g, v_ln2_b):
    w = dict(w_in=w_in, b_f=b_f, rel_bias=rel_bias, w_br_fox=w_br_fox, w_br_chunk=w_br_chunk, w_out=w_out,
             w_up=w_up, conv_w=conv_w, conv_b=conv_b, w_down=w_down, w_ada=w_ada, b_ada=b_ada,
             ln1_g=ln1_g, ln1_b=ln1_b, ln2_g=ln2_g, ln2_b=ln2_b)
    m = dict(w_in=m_w_in, b_f=m_b_f, rel_bias=m_rel_bias, w_br_fox=m_w_br_fox, w_br_chunk=m_w_br_chunk,
             w_out=m_w_out, w_up=m_w_up, conv_w=m_conv_w, conv_b=m_conv_b, w_down=m_w_down, w_ada=m_w_ada,
             b_ada=m_b_ada, ln1_g=m_ln1_g, ln1_b=m_ln1_b, ln2_g=m_ln2_g, ln2_b=m_ln2_b)
    v = dict(w_in=v_w_in, b_f=v_b_f, rel_bias=v_rel_bias, w_br_fox=v_w_br_fox, w_br_chunk=v_w_br_chunk,
             w_out=v_w_out, w_up=v_w_up, conv_w=v_conv_w, conv_b=v_conv_b, w_down=v_w_down, w_ada=v_w_ada,
             b_ada=v_b_ada, ln1_g=v_ln1_g, ln1_b=v_ln1_b, ln2_g=v_ln2_g, ln2_b=v_ln2_b)
    depth, d, _ = w_in.shape
    s = x.shape[1]
    alpha = (2.0 * depth) ** 0.25
    me = 4 * lax.axis_index("x") + 2 * lax.axis_index("y") + lax.axis_index("c")
    x0 = x.reshape(s, d)
    target = loss_target.reshape(s, d)

    small_in, small_rows = _pack([c, conv_w], mult=8)
    small_all = _all_gather([small_in], "gather_c_conv")[0]
    c_all, conv_w_all = _unpack(small_all, small_rows, [c.shape, conv_w.shape], lead=(N_DEV,))
    c_all = c_all.reshape(N_DEV, d)
    conv_w_full = conv_w_all.transpose(1, 2, 0, 3).reshape(depth, conv_w.shape[1], -1)
    n_ada = w_ada.shape[2]
    b_ada_mine = lax.dynamic_slice_in_dim(b_ada, me * n_ada, n_ada, axis=1).reshape(depth, 1, n_ada)
    mod_part, cond_all = _mod_part(c_all, w_ada, b_ada_mine, "mod_part")
    mod_all = _all_gather([mod_part.reshape(depth * N_DEV, n_ada)], "gather_mod")[0]
    mod_all = mod_all.reshape(N_DEV, depth, N_DEV, n_ada)
    mod_mine = lax.dynamic_index_in_dim(mod_all, me, axis=2, keepdims=False)
    mod_mine = mod_mine.transpose(1, 0, 2).reshape(depth, N_MOD, 1, d)

    shard = lambda n, l: _pad_lanes(w[n][l].astype(BF16))
    cols = np.cumsum([0, WIDTH, WIDTH, WIDTH, N_HEADS, WIDTH, WIDTH, WIDTH, d, d])

    def whole(n, p):
        rows, ncol = w[n].shape[1:]
        p = jnp.moveaxis(p[..., :ncol], 0, _shard_axis(n))
        return p.reshape((rows * N_DEV, ncol) if _shard_axis(n) == 0 else (rows, ncol * N_DEV))

    def first_weights(l, w_in_parts):
        wi = whole("w_in", w_in_parts)
        w_qkv = jnp.concatenate([wi[:, :cols[3]], wi[:, cols[4]:cols[7]]], axis=1)
        w_gates = wi[:, cols[7]:]
        w_f_t = jnp.concatenate([wi[:, cols[3]:cols[4]].T, jnp.zeros((16 - N_HEADS, d), BF16)], axis=0)
        w_f_pad = jnp.concatenate([w_f_t, jnp.zeros((AUG - 16, d), BF16)], axis=0)
        row = lambda a: a[l].reshape(1, -1)
        return dict(
            w_qkv=w_qkv, w_gates=w_gates, w_f_t=w_f_t, w_f_pad=w_f_pad, w_qkv_t=w_qkv.T, w_gates_t=w_gates.T,
            conv_w=conv_w_full[l].reshape(3, 2, -1).transpose(1, 0, 2), conv_b=conv_b[l].reshape(2, 1, -1),
            b_f=b_f[l], rel_bias=rel_bias[l],
            ln1_g=row(ln1_g), ln1_b=row(ln1_b), ln2_g=row(ln2_g), ln2_b=row(ln2_b))

    def rest_weights(parts):
        out = {}
        for n, p in zip(REST, parts):
            out[n] = whole(n, p)
            out[n + "_t"] = out[n].T
        return out

    xs, saved, layers = x0, [], []
    hs = _ln_mod(x0, mod_mine[0][1], mod_mine[0][0], "ln_mod1")
    w_in_parts = _all_gather([shard("w_in", 0)], "gather_weights")[0]
    for l in range(depth):
        layers.append(first_weights(l, w_in_parts))
        nxt = mod_mine[min(l + 1, depth - 1)]
        gather = [shard(n, l) for n in REST] + ([shard("w_in", l + 1)] if l + 1 < depth else [])
        xs, hs, sv, extra = _layer_fwd(xs, hs, layers[l], rest_weights, list(mod_mine[l]), (nxt[1], nxt[0]), alpha,
                                       gather)
        w_in_parts = extra[0] if extra else None
        saved.append(sv)
    dxs, loss_part = _loss_grad(xs, target, "loss_grad")
    loss = lax.psum(loss_part[0, 0], AXES)
    grads, big_parts, send = [None] * depth, {}, []
    for l in reversed(range(depth)):
        dxs, grads[l], landed, landed_late = _layer_bwd(dxs, saved[l], layers[l], list(mod_mine[l]), alpha, send,
                                                        ride_late=(l == 0))
        for n, part in zip([(n, l + 1) for n in LATE if send] + [(n, l) for n in EARLY], landed):
            big_parts[n] = part
        for n, part in zip(LATE, landed_late):
            big_parts[n, l] = part
        send = [_slab(grads[l][n], n) for n in LATE] if l else []
    grad_x = dxs.reshape(1, s, d)
    stack = lambda n: jnp.stack([grads[l][n] for l in range(depth)])

    rep = ["b_ada"] + [n for n in SMALL if n != "b_ada"]
    small_g = [stack("mod")] + [stack(n) for n in rep[1:]] + [stack("conv_w")]
    small_pack, small_g_rows = _pack(small_g, mult=8)
    small_parts = _all_gather([small_pack], "gather_small_grads")[0]
    n_rep_rows = sum(small_g_rows[:-1])
    dmod_all = _unpack(small_parts, small_g_rows[:1], [(depth, N_MOD * d)], lead=(N_DEV,))[0]
    cw_all = _unpack(small_parts[:, n_rep_rows:], small_g_rows[-1:], [small_g[-1].shape], lead=(N_DEV,))[0]

    out = {}
    for n in BIG:
        rows2d = lambda a: a.reshape(-1, a.shape[-1])
        res = _adamw([big_parts[n, l] for l in range(depth)], rows2d(w[n]), rows2d(m[n]), rows2d(v[n]),
                     "adamw_" + n, tr=256)
        for kind, r in zip(("grad", "delta", "new_m", "new_v"), res):
            out[kind, n] = r.reshape(w[n].shape)

    dmod_mine = lax.dynamic_slice_in_dim(dmod_all, me * n_ada, n_ada, axis=2)
    g_ada = jnp.stack([_mm(cond_all.T, dmod_mine[:, l], out_dtype=F32, name="g_w_ada", tk=N_DEV, tn=n_ada,
                           precision=HI) for l in range(depth)])
    lp = [_rows128(t["w_ada"], 8) for t in (w, m, v)]
    res = _adamw([_rows128(g_ada, 8)[None]], *lp, "adamw_ada")
    for kind, r in zip(("grad", "delta", "new_m", "new_v"), res):
        out[kind, "w_ada"] = _unpack(r, [lp[0].shape[0]], [w_ada.shape])[0]

    rp = [_pack([t[n] for n in rep], mult=8)[0] for t in (w, m, v)]
    res = _adamw([small_parts[:, :n_rep_rows]], *rp, "adamw_small")
    for kind, r in zip(("grad", "delta", "new_m", "new_v"), res):
        for n, a in zip(rep, _unpack(r, small_g_rows[:-1], [w[n].shape for n in rep])):
            out[kind, n] = a

    n_cw = conv_w.shape[2]
    cw_mine = lax.dynamic_slice_in_dim(cw_all, me * n_cw, n_cw, axis=3)
    cp = [_rows128(t["conv_w"], 8) for t in (w, m, v)]
    cw_in = jnp.stack([_rows128(cw_mine[i], 8) for i in range(N_DEV)])
    res = _adamw([cw_in], *cp, "adamw_conv_w")
    for kind, r in zip(("grad", "delta", "new_m", "new_v"), res):
        out[kind, "conv_w"] = _unpack(r, [cp[0].shape[0]], [conv_w.shape])[0]

    result = [loss, grad_x]
    for kind in ("grad", "delta", "new_m", "new_v"):
        result += [out[kind, n] for n in ORDER]
    return tuple(result)
```

```python
import functools
import math

import jax
import jax.numpy as jnp
import numpy as np
from jax import lax
from jax.experimental import pallas as pl
from jax.experimental.pallas import tpu as pltpu

F32 = jnp.float32
BF16 = jnp.bfloat16
HI = lax.Precision.HIGHEST
MESH_ID = pl.DeviceIdType.MESH
AXES = ("x", "y", "c")
N_DEV = 8

N_HEADS = 8
HEAD_DIM = 64
WIDTH = N_HEADS * HEAD_DIM
CHUNK = 64
LEFT_CHUNKS = 8
BAND = (LEFT_CHUNKS + 1) * CHUNK
REL_CLIP = 128
LN_EPS = 1e-5
N_MOD = 6
QK_SCALE = 1.0 / math.sqrt(HEAD_DIM)
NEG = -1e30
AUG = 128
N_SPLIT = 3
TQ = 512
SUB = 2 * CHUNK
SUBK = BAND + CHUNK

ADAM_LR, ADAM_B1, ADAM_B2, ADAM_EPS, ADAM_WD, ADAM_STEP = 0.001, 0.9, 0.999, 1e-08, 0.01, 10

VMEM_LIMIT = 56 * 2 ** 20


def _pcall(body, **kw):
    return pl.pallas_call(body, **kw)


def _cp(*sem):
    return pltpu.CompilerParams(dimension_semantics=sem if sem else None, vmem_limit_bytes=VMEM_LIMIT)


def _my_place():
    return lax.axis_index("x"), lax.axis_index("y"), lax.axis_index("c")


N_PEER = N_DEV - 1


def _comm_call(body, xs, out_shapes, name):
    n = len(xs)
    hbm = pl.BlockSpec(memory_space=pltpu.HBM)
    return _pcall(
        functools.partial(body, n), name=name, out_shape=out_shapes, in_specs=[hbm] * n, out_specs=[hbm] * n,
        scratch_shapes=[pltpu.SemaphoreType.DMA((N_PEER * n,)), pltpu.SemaphoreType.DMA((N_PEER * n,)),
                        pltpu.SemaphoreType.DMA((n,))],
    )(*xs)


def _all_gather(xs, name):
    def body(n, *refs):
        x_refs, out_refs, (send_sems, recv_sems, local_sems) = refs[:n], refs[n:2 * n], refs[2 * n:]
        mx, my, mc = _my_place()
        me, sibling = (mx, my, mc), (mx, my, 1 - mc)
        chips = [(1 - mx, my), (mx, 1 - my), (1 - mx, 1 - my)]

        def slot(a, px, py, pc):
            return out_refs[a].at[4 * px + 2 * py + pc]

        def copy(a, k, block, to, own=False):
            return pltpu.make_async_remote_copy(
                src_ref=x_refs[a] if own else slot(a, *block), dst_ref=slot(a, *block),
                send_sem=send_sems.at[a * N_PEER + k], recv_sem=recv_sems.at[a * N_PEER + k],
                device_id=to, device_id_type=MESH_ID)

        arrays = range(n)
        mine = [pltpu.make_async_copy(x_refs[a], slot(a, *me), local_sems.at[a]) for a in arrays]
        first = [copy(a, 0, me, sibling, own=True) for a in arrays]
        first += [copy(a, 1 + j, me, (*chip, mc), own=True) for a in arrays for j, chip in enumerate(chips)]
        for cp in mine + first:
            cp.start()
        passed = []
        for a in arrays:
            for j, chip in enumerate(chips):
                copy(a, 1 + j, (*chip, mc), me).wait_recv()
                passed.append(copy(a, 4 + j, (*chip, mc), sibling))
                passed[-1].start()
        for a in arrays:
            copy(a, 0, sibling, me).wait_recv()
            for j, chip in enumerate(chips):
                copy(a, 4 + j, (*chip, 1 - mc), me).wait_recv()
        for cp in first + passed:
            cp.wait_send()
        for cp in mine:
            cp.wait()

    return _comm_call(body, xs, [jax.ShapeDtypeStruct((N_DEV,) + x.shape, x.dtype) for x in xs], name)


def _direct_exchange(scatter, x_refs, out_refs, send_sems, recv_sems, local_sems):
    mx, my, mc = _my_place()
    me = 4 * mx + 2 * my + mc
    mine, copies = [], []
    for a, (x_ref, out_ref) in enumerate(zip(x_refs, out_refs)):
        mine.append(pltpu.make_async_copy(x_ref.at[me] if scatter else x_ref, out_ref.at[me], local_sems.at[a]))
        for k in range(N_PEER):
            px, py, pc = mx ^ ((k + 1) >> 2), my ^ (((k + 1) >> 1) & 1), mc ^ ((k + 1) & 1)
            copies.append(pltpu.make_async_remote_copy(
                src_ref=x_ref.at[4 * px + 2 * py + pc] if scatter else x_ref, dst_ref=out_ref.at[me],
                send_sem=send_sems.at[a * N_PEER + k], recv_sem=recv_sems.at[a * N_PEER + k],
                device_id=(px, py, pc), device_id_type=MESH_ID))

    def start():
        for cp in mine + copies:
            cp.start()

    def wait():
        for cp in copies:
            cp.wait_recv()
        for cp in copies:
            cp.wait_send()
        for cp in mine:
            cp.wait()

    return start, wait


def _exchange_shapes(scatter, xs):
    return [jax.ShapeDtypeStruct(x.shape if scatter else (N_DEV,) + x.shape, x.dtype) for x in xs]


def _exchange_scratch(n):
    return [pltpu.SemaphoreType.DMA((N_PEER * n,)), pltpu.SemaphoreType.DMA((N_PEER * n,)),
            pltpu.SemaphoreType.DMA((n,))]


def _tile(n, pref, unit=128):
    if n <= pref:
        return n
    t = pref - pref % unit
    while t > unit and n % t:
        t -= unit
    assert n % t == 0, (n, pref, unit)
    return t


def _mm(a, b, *, out_dtype, name, ta=False, tb=False, tm=1024, tn=1024, tk=512, precision=None,
        a_halves=False, b_halves=False, o_halves=False, exchange=None):
    a_shape = (a.shape[1], 2 * a.shape[2]) if a_halves else a.shape
    b_shape = (b.shape[1], 2 * b.shape[2]) if b_halves else b.shape
    (k_dim, m) = a_shape if ta else a_shape[::-1]
    n = b_shape[0] if tb else b_shape[1]
    tm, tn = _tile(m, tm, 8), _tile(n // 2 if (b_halves or o_halves) else n, tn)
    tk = _tile(k_dim // 2 if a_halves else k_dim, tk)
    nk = k_dim // tk
    dims = (((0 if ta else 1,), (1 if tb else 0,)), ((), ()))
    grid = (m // tm, n // tn, nk)
    ex_in, ex_out, ex_specs, ex_scratch, hook = _riding_exchange(exchange, 2, 1, grid)

    def body(*refs):
        (a_ref, b_ref, o_ref, acc_ref), finish = hook(refs)
        k = pl.program_id(2)
        if nk == 1:
            o_ref[...] = lax.dot_general(a_ref[...], b_ref[...], dims, preferred_element_type=F32,
                                         precision=precision).astype(o_ref.dtype)
            finish()
            return

        @pl.when(k == 0)
        def _():
            acc_ref[...] = jnp.zeros_like(acc_ref)

        acc_ref[...] += lax.dot_general(a_ref[...], b_ref[...], dims, preferred_element_type=F32,
                                        precision=precision)

        @pl.when(k == nk - 1)
        def _():
            o_ref[...] = acc_ref[...].astype(o_ref.dtype)

        finish()

    def spec(shape2, pick, halves, per_half):
        if not halves:
            return pl.BlockSpec(shape2, pick)

        def index(i, j, k):
            r, c = pick(i, j, k)
            return (c // per_half, r, c % per_half)
        return pl.BlockSpec((None,) + shape2, index)

    assert not (ta and a_halves) and not (tb and b_halves)
    a_spec = spec((tk, tm), lambda i, j, k: (k, i), False, 0) if ta else \
        spec((tm, tk), lambda i, j, k: (i, k), a_halves, (k_dim // 2) // tk if a_halves else 0)
    b_spec = spec((tn, tk), lambda i, j, k: (j, k), False, 0) if tb else \
        spec((tk, tn), lambda i, j, k: (k, j), b_halves, (n // 2) // tn if b_halves else 0)
    o_spec = spec((tm, tn), lambda i, j, k: (i, j), o_halves, (n // 2) // tn if o_halves else 0)
    out_shape = (2, m, n // 2) if o_halves else (m, n)
    res = _pcall(
        body, name=name,
        out_shape=[jax.ShapeDtypeStruct(out_shape, out_dtype)] + ex_out,
        grid=grid,
        in_specs=[a_spec, b_spec] + ex_specs,
        out_specs=[o_spec] + ex_specs,
        scratch_shapes=[pltpu.VMEM((tm, tn), F32)] + ex_scratch,
        compiler_params=_cp(*(["arbitrary"] * 3 if exchange else ["parallel", "parallel", "arbitrary"])),
    )(a, b, *ex_in)
    return res if exchange else res[0]


def _ln_stats(x):
    mu = jnp.mean(x, axis=-1, keepdims=True)
    xc = x - mu
    var = jnp.mean(xc * xc, axis=-1, keepdims=True)
    rstd = lax.rsqrt(var + LN_EPS)
    return xc * rstd, rstd


def _ln_mod(x, sc, sh, name, tm=512):
    s, d = x.shape
    tm = min(tm, s)

    def body(x_ref, sc_ref, sh_ref, h_ref):
        xhat, _ = _ln_stats(x_ref[...])
        h_ref[...] = (xhat * (1.0 + sc_ref[...]) + sh_ref[...]).astype(h_ref.dtype)

    row = pl.BlockSpec((1, d), lambda i: (0, 0))
    return _pcall(
        body, name=name, out_shape=jax.ShapeDtypeStruct((s, d), BF16), grid=(s // tm,),
        in_specs=[pl.BlockSpec((tm, d), lambda i: (i, 0)), row, row],
        out_specs=pl.BlockSpec((tm, d), lambda i: (i, 0)),
        compiler_params=_cp("parallel"),
    )(x, sc, sh)


def _mm_res_ln(a, w, xres, gate, ln_g, ln_b, sc, sh, alpha, name, tm=512, tk=512):
    s, k_dim = a.shape
    d = w.shape[1]
    tm, tk = _tile(s, tm, 8), _tile(k_dim, tk)
    nk = k_dim // tk

    def body(a_ref, w_ref, x_ref, g_ref, lg_ref, lb_ref, sc_ref, sh_ref, y_ref, xn_ref, h_ref, acc_ref):
        k = pl.program_id(1)

        @pl.when(k == 0)
        def _():
            acc_ref[...] = jnp.zeros_like(acc_ref)

        acc_ref[...] += jnp.dot(a_ref[...], w_ref[...], preferred_element_type=F32)

        @pl.when(k == nk - 1)
        def _():
            y = acc_ref[...]
            y_ref[...] = y
            z = alpha * x_ref[...] + (1.0 + g_ref[...]) * y
            zhat, _ = _ln_stats(z)
            xn = zhat * lg_ref[...] + lb_ref[...]
            xn_ref[...] = xn
            xhat, _ = _ln_stats(xn)
            h_ref[...] = (xhat * (1.0 + sc_ref[...]) + sh_ref[...]).astype(h_ref.dtype)

    row = pl.BlockSpec((1, d), lambda i, k: (0, 0))
    tile = pl.BlockSpec((tm, d), lambda i, k: (i, 0))
    return _pcall(
        body, name=name,
        out_shape=(jax.ShapeDtypeStruct((s, d), F32), jax.ShapeDtypeStruct((s, d), F32),
                   jax.ShapeDtypeStruct((s, d), BF16)),
        grid=(s // tm, nk),
        in_specs=[pl.BlockSpec((tm, tk), lambda i, k: (i, k)), pl.BlockSpec((tk, d), lambda i, k: (k, 0)),
                  tile, row, row, row, row, row],
        out_specs=(tile, tile, tile),
        scratch_shapes=[pltpu.VMEM((tm, d), F32)],
        compiler_params=_cp("parallel", "arbitrary"),
    )(a, w, xres, gate, ln_g, ln_b, sc, sh)


def _colsum(v):
    return jnp.sum(v, axis=0, keepdims=True)


def _ln_res_bwd(dxn, xres, y, gate, ln_g, alpha, name, tm=256):
    s, d = dxn.shape
    tm = min(tm, s)

    def body(dxn_ref, x_ref, y_ref, g_ref, lg_ref, dres_ref, dy_ref, sums_ref):
        @pl.when(pl.program_id(0) == 0)
        def _():
            sums_ref[...] = jnp.zeros_like(sums_ref)

        dxn_v, y_v = dxn_ref[...], y_ref[...]
        one_g = 1.0 + g_ref[...]
        zhat, rstd = _ln_stats(alpha * x_ref[...] + one_g * y_v)
        dzh = dxn_v * lg_ref[...]
        dz = rstd * (dzh - jnp.mean(dzh, axis=-1, keepdims=True)
                     - zhat * jnp.mean(dzh * zhat, axis=-1, keepdims=True))
        dres_ref[...] = alpha * dz
        dy_ref[...] = (one_g * dz).astype(dy_ref.dtype)
        sums_ref[0:1, :] += _colsum(dxn_v * zhat)
        sums_ref[1:2, :] += _colsum(dxn_v)
        sums_ref[2:3, :] += _colsum(dz * y_v)

    row = pl.BlockSpec((1, d), lambda i: (0, 0))
    tile = pl.BlockSpec((tm, d), lambda i: (i, 0))
    return _pcall(
        body, name=name,
        out_shape=(jax.ShapeDtypeStruct((s, d), F32), jax.ShapeDtypeStruct((s, d), BF16),
                   jax.ShapeDtypeStruct((8, d), F32)),
        grid=(s // tm,),
        in_specs=[tile, tile, tile, row, row],
        out_specs=(tile, tile, pl.BlockSpec((8, d), lambda i: (0, 0))),
        compiler_params=_cp("arbitrary"),
    )(dxn, xres, y, gate, ln_g)


def _ln_mod_bwd(dh, x, sc, dres, name, extra=None, tm=256):
    s, d = dh.shape
    tm = min(tm, s)

    def body(*refs):
        if extra is None:
            dh_ref, x_ref, sc_ref, dres_ref, dx_ref, sums_ref = refs
            dh_v = dh_ref[...]
        else:
            dh_ref, x_ref, sc_ref, dres_ref, dh2_ref, df_ref, wf_ref, dx_ref, sums_ref = refs
            dh_v = dh_ref[...] + dh2_ref[...] + jnp.dot(df_ref[...], wf_ref[...], preferred_element_type=F32)

        @pl.when(pl.program_id(0) == 0)
        def _():
            sums_ref[...] = jnp.zeros_like(sums_ref)

        xhat, rstd = _ln_stats(x_ref[...])
        dxh = dh_v * (1.0 + sc_ref[...])
        dx_ref[...] = dres_ref[...] + rstd * (dxh - jnp.mean(dxh, axis=-1, keepdims=True)
                                             - xhat * jnp.mean(dxh * xhat, axis=-1, keepdims=True))
        sums_ref[0:1, :] += _colsum(dh_v)
        sums_ref[1:2, :] += _colsum(dh_v * xhat)

    row = pl.BlockSpec((1, d), lambda i: (0, 0))
    tile = pl.BlockSpec((tm, d), lambda i: (i, 0))
    in_specs = [tile, tile, row, tile]
    args = [dh, x, sc, dres]
    if extra is not None:
        in_specs += [tile, pl.BlockSpec((tm, AUG), lambda i: (i, 0)), pl.BlockSpec((AUG, d), lambda i: (0, 0))]
        args += list(extra)
    return _pcall(
        body, name=name,
        out_shape=(jax.ShapeDtypeStruct((s, d), F32), jax.ShapeDtypeStruct((8, d), F32)),
        grid=(s // tm,), in_specs=in_specs,
        out_specs=(tile, pl.BlockSpec((8, d), lambda i: (0, 0))),
        compiler_params=_cp("arbitrary"),
    )(*args)


def _loss_grad(xn, target, name, tm=512):
    s, d = xn.shape
    tm = min(tm, s)
    n = s // tm

    def body(x_ref, t_ref, dx_ref, loss_ref, acc_ref):
        i = pl.program_id(0)

        @pl.when(i == 0)
        def _():
            acc_ref[...] = jnp.zeros_like(acc_ref)

        err = x_ref[...] - t_ref[...]
        dx_ref[...] = err * (1.0 / d)
        acc_ref[...] += _colsum(err * err)

        @pl.when(i == n - 1)
        def _():
            loss_ref[...] = jnp.zeros_like(loss_ref) + (0.5 / d) * jnp.sum(acc_ref[...])

    tile = pl.BlockSpec((tm, d), lambda i: (i, 0))
    return _pcall(
        body, name=name,
        out_shape=(jax.ShapeDtypeStruct((s, d), F32), jax.ShapeDtypeStruct((8, 128), F32)),
        grid=(n,), in_specs=[tile, tile],
        out_specs=(tile, pl.BlockSpec((8, 128), lambda i: (0, 0))),
        scratch_shapes=[pltpu.VMEM((1, d), F32)],
        compiler_params=_cp("arbitrary"),
    )(xn, target)


def _sigmoid(v):
    return 0.5 * jnp.tanh(0.5 * v) + 0.5


def _merge_fwd(o_a, o_c, w_a, w_c, gates, name, tm=256):
    s, wd = o_a.shape
    d = w_a.shape[1]
    tm = min(tm, s)

    def body(oa_ref, oc_ref, wa_ref, wc_ref, ga_ref, gc_ref, m_ref):
        ba = jnp.dot(oa_ref[...], wa_ref[...], preferred_element_type=F32)
        bc = jnp.dot(oc_ref[...], wc_ref[...], preferred_element_type=F32)
        m_ref[...] = (_sigmoid(ga_ref[...]) * ba + _sigmoid(gc_ref[...]) * bc).astype(m_ref.dtype)

    o_spec = pl.BlockSpec((tm, wd), lambda i: (i, 0))
    w_spec = pl.BlockSpec((wd, d), lambda i: (0, 0))
    return _pcall(
        body, name=name, out_shape=jax.ShapeDtypeStruct((s, d), BF16), grid=(s // tm,),
        in_specs=[o_spec, o_spec, w_spec, w_spec,
                  pl.BlockSpec((tm, d), lambda i: (i, 0)), pl.BlockSpec((tm, d), lambda i: (i, 1))],
        out_specs=pl.BlockSpec((tm, d), lambda i: (i, 0)),
        compiler_params=_cp("parallel"),
    )(o_a, o_c, w_a, w_c, gates, gates)


def _merge_bwd(dmix, w_out_t, o_a, o_c, w_a, w_c, w_a_t, w_c_t, gates, name, tm=256):
    s, wd = o_a.shape
    d = w_a.shape[1]
    tm = min(tm, s)

    def body(dmix_ref, wot_ref, oa_ref, oc_ref, wa_ref, wc_ref, wat_ref, wct_ref, ga_ref, gc_ref,
             dba_ref, dbc_ref, doa_ref, doc_ref, dg_ref):
        dm = jnp.dot(dmix_ref[...], wot_ref[...], preferred_element_type=F32)
        for half, (o_ref, w_ref, wt_ref, g_ref, db_ref, do_ref) in enumerate((
                (oa_ref, wa_ref, wat_ref, ga_ref, dba_ref, doa_ref),
                (oc_ref, wc_ref, wct_ref, gc_ref, dbc_ref, doc_ref))):
            b = jnp.dot(o_ref[...], w_ref[...], preferred_element_type=F32)
            sg = _sigmoid(g_ref[...])
            db = (dm * sg).astype(BF16)
            db_ref[...] = db
            dg_ref[:, d * half:d * (half + 1)] = (dm * b * sg * (1.0 - sg)).astype(dg_ref.dtype)
            do_ref[...] = jnp.dot(db, wt_ref[...], preferred_element_type=F32).astype(do_ref.dtype)

    row_d = pl.BlockSpec((tm, d), lambda i: (i, 0))
    row_w = pl.BlockSpec((tm, wd), lambda i: (i, 0))
    full = lambda shp: pl.BlockSpec(shp, lambda i: (0, 0))
    return _pcall(
        body, name=name,
        out_shape=(jax.ShapeDtypeStruct((s, d), BF16), jax.ShapeDtypeStruct((s, d), BF16),
                   jax.ShapeDtypeStruct((s, wd), BF16), jax.ShapeDtypeStruct((s, wd), BF16),
                   jax.ShapeDtypeStruct((s, 2 * d), BF16)),
        grid=(s // tm,),
        in_specs=[row_d, full((d, d)), row_w, row_w, full((wd, d)), full((wd, d)), full((d, wd)), full((d, wd)),
                  row_d, pl.BlockSpec((tm, d), lambda i: (i, 1))],
        out_specs=(row_d, row_d, row_w, row_w, pl.BlockSpec((tm, 2 * d), lambda i: (i, 0))),
        compiler_params=_cp("parallel"),
    )(dmix, w_out_t, o_a, o_c, w_a, w_c, w_a_t, w_c_t, gates, gates)


def _shift_down(cur, k, fill_rows):
    out = pltpu.roll(cur, k, axis=0)
    rid = lax.broadcasted_iota(jnp.int32, cur.shape, 0)
    for r, fill in enumerate(fill_rows):
        out = jnp.where(rid == r, fill, out)
    return out


def _shift_up(cur, k, fill_rows):
    n = cur.shape[0]
    out = pltpu.roll(cur, n - k, axis=0)
    rid = lax.broadcasted_iota(jnp.int32, cur.shape, 0)
    for r, fill in enumerate(fill_rows):
        out = jnp.where(rid == n - k + r, fill, out)
    return out


def _conv_rows(cur, prev8, first, w, b):
    p6 = jnp.where(first, 0.0, prev8[6:7, :])
    p7 = jnp.where(first, 0.0, prev8[7:8, :])
    m1 = _shift_down(cur, 1, [p7])
    m2 = _shift_down(cur, 2, [p6, p7])
    u = b + w[0:1, :] * m2 + w[1:2, :] * m1 + w[2:3, :] * cur
    return u, m1, m2


def _conv_specs(tr, tc, order):
    r8 = tr // 8
    at = lambda f: (lambda *g: f(*order(*g)))
    return [pl.BlockSpec((2, tr, tc), at(lambda i, j: (0, i, j))),
            pl.BlockSpec((2, 8, tc), at(lambda i, j: (0, jnp.maximum(i * r8 - 1, 0), j))),
            pl.BlockSpec((2, 3, tc), at(lambda i, j: (0, 0, j))),
            pl.BlockSpec((2, 1, tc), at(lambda i, j: (0, 0, j)))]


def _conv_act_fwd(u0, conv_w, conv_b, name, tr=512, tc=256):
    _, s, f = u0.shape
    tr, tc = _tile(s, tr, 8), _tile(f, tc)

    def body(u_ref, p_ref, w_ref, b_ref, act_ref):
        first = pl.program_id(0) == 0
        a, _, _ = _conv_rows(u_ref[0], p_ref[0], first, w_ref[0], b_ref[0])
        v, _, _ = _conv_rows(u_ref[1], p_ref[1], first, w_ref[1], b_ref[1])
        act_ref[...] = (a * _sigmoid(a) * v).astype(act_ref.dtype)

    return _pcall(
        body, name=name, out_shape=jax.ShapeDtypeStruct((s, f), BF16), grid=(s // tr, f // tc),
        in_specs=_conv_specs(tr, tc, lambda i, j: (i, j)),
        out_specs=pl.BlockSpec((tr, tc), lambda i, j: (i, j)),
        compiler_params=_cp("parallel", "parallel"),
    )(u0, u0, conv_w, conv_b)


def _act_grads(dact, a, v):
    sg = _sigmoid(a)
    return dact * v * sg * (1.0 + a * (1.0 - sg)), dact * a * sg


def _conv_act_bwd(dact, u0, conv_w, conv_b, name, tr=512, tc=256):
    _, s, f = u0.shape
    tr, tc = _tile(s, tr, 8), _tile(f, tc)
    n = s // tr
    r8 = tr // 8

    def body(dact_ref, dnext_ref, u_ref, p_ref, unext_ref, w_ref, b_ref, du0_ref, sums_ref):
        i = pl.program_id(1)
        first, last = i == 0, i == n - 1

        @pl.when(first)
        def _():
            sums_ref[...] = jnp.zeros_like(sums_ref)

        cur = (u_ref[0], u_ref[1])
        (a, a1, a2), (v, v1, v2) = [_conv_rows(cur[hf], p_ref[hf], first, w_ref[hf], b_ref[hf]) for hf in range(2)]
        da, dv = _act_grads(dact_ref[...], a, v)
        (an, _, _), (vn, _, _) = [_conv_rows(unext_ref[hf], cur[hf][tr - 8:tr, :], False, w_ref[hf], b_ref[hf])
                                  for hf in range(2)]
        dan, dvn = _act_grads(dnext_ref[...], an, vn)
        for hf, g, gn, shifted in ((0, da, dan, (a2, a1, cur[0])), (1, dv, dvn, (v2, v1, cur[1]))):
            w = w_ref[hf]
            n0 = jnp.where(last, 0.0, gn[0:1, :])
            n1 = jnp.where(last, 0.0, gn[1:2, :])
            du0 = w[2:3, :] * g + w[1:2, :] * _shift_up(g, 1, [n0]) + w[0:1, :] * _shift_up(g, 2, [n0, n1])
            du0_ref[hf] = du0.astype(du0_ref.dtype)
            for r in range(3):
                sums_ref[hf, r:r + 1, :] += _colsum(g * shifted[r])
            sums_ref[hf, 3:4, :] += _colsum(g)

    nxt = lambda i: jnp.minimum((i + 1) * r8, s // 8 - 1)
    order = lambda j, i: (i, j)
    specs = _conv_specs(tr, tc, order)
    return _pcall(
        body, name=name,
        out_shape=(jax.ShapeDtypeStruct((2, s, f), BF16), jax.ShapeDtypeStruct((2, 8, f), F32)),
        grid=(f // tc, n),
        in_specs=[pl.BlockSpec((tr, tc), lambda j, i: (i, j)), pl.BlockSpec((8, tc), lambda j, i: (nxt(i), j)),
                  specs[0], specs[1], pl.BlockSpec((2, 8, tc), lambda j, i: (0, nxt(i), j)), specs[2], specs[3]],
        out_specs=(pl.BlockSpec((2, tr, tc), lambda j, i: (0, i, j)), pl.BlockSpec((2, 8, tc), lambda j, i: (0, 0, j))),
        compiler_params=_cp("parallel", "arbitrary"),
    )(dact, dact, u0, u0, u0, conv_w, conv_b)


def _cumsum_rows(v, reverse=False):
    r = v.shape[0]
    i128 = lax.broadcasted_iota(jnp.int32, (128, 128), 0), lax.broadcasted_iota(jnp.int32, (128, 128), 1)
    ir = lax.broadcasted_iota(jnp.int32, (r, r), 0), lax.broadcasted_iota(jnp.int32, (r, r), 1)
    in_row = (i128[0] >= i128[1] if reverse else i128[0] <= i128[1]).astype(F32)
    rows = (ir[1] > ir[0] if reverse else ir[1] < ir[0]).astype(F32)
    within = jnp.dot(v, in_row, preferred_element_type=F32, precision=HI)
    tot = jnp.broadcast_to(within[:, 0:1] if reverse else within[:, 127:128], (r, 128))
    return within + jnp.dot(rows, tot, preferred_element_type=F32, precision=HI)


def _forget_fwd(f_t, b_f, name):
    h, r, _ = f_t.shape

    def body(f_ref, b_ref, o_ref):
        z = f_ref[...] + b_ref[...]
        logf = jnp.minimum(z, 0.0) - jnp.log(1.0 + jnp.exp(-jnp.abs(z)))
        rest = _cumsum_rows(logf)
        for i in range(N_SPLIT):
            piece = rest.astype(BF16).astype(F32)
            o_ref[i] = piece
            rest = rest - piece

    return _pcall(
        body, name=name, out_shape=jax.ShapeDtypeStruct((N_SPLIT, h, r, 128), F32), grid=(h,),
        in_specs=[pl.BlockSpec((None, r, 128), lambda i: (i, 0, 0)), pl.BlockSpec((None, 1, 128), lambda i: (i, 0, 0))],
        out_specs=pl.BlockSpec((N_SPLIT, None, r, 128), lambda i: (0, i, 0, 0)),
        compiler_params=_cp("parallel"),
    )(f_t, b_f)


def _forget_bwd(d_cum, f_t, b_f, name):
    h, r, _ = f_t.shape

    def body(g_ref, f_ref, b_ref, df_ref, db_ref):
        df = _cumsum_rows(g_ref[...], reverse=True) * _sigmoid(-(f_ref[...] + b_ref[...]))
        df_ref[...] = df
        db_ref[...] = jnp.zeros_like(db_ref) + jnp.sum(df)

    blk = pl.BlockSpec((None, r, 128), lambda i: (i, 0, 0))
    one = pl.BlockSpec((None, 1, 128), lambda i: (i, 0, 0))
    return _pcall(
        body, name=name,
        out_shape=(jax.ShapeDtypeStruct((h, r, 128), F32), jax.ShapeDtypeStruct((h, 1, 128), F32)),
        grid=(h,), in_specs=[blk, blk, one], out_specs=(blk, one),
        compiler_params=_cp("parallel"),
    )(d_cum, f_t, b_f)


_NT = (((1,), (1,)), ((), ()))


def _causal_keep(tk, tq):
    return lax.broadcasted_iota(jnp.int32, (tk, tq), 0) <= lax.broadcasted_iota(jnp.int32, (tk, tq), 1)


_TN = (((0,), (0,)), ((), ()))
PAIR = 2
V_FOX = 2 * WIDTH // 128
Q_CHUNK, K_CHUNK, V_CHUNK = (3 * WIDTH) // 128, (4 * WIDTH) // 128, (5 * WIDTH) // 128


def _riding_exchange(exchange, n_in, n_out, grid):
    if exchange is None:
        return [], [], [], [], lambda refs: (refs, lambda: None)
    scatter, xs = exchange
    n = len(xs)
    hbm = pl.BlockSpec(memory_space=pltpu.HBM)

    def hook(refs):
        ins, x_refs = refs[:n_in], refs[n_in:n_in + n]
        outs, land_refs = refs[n_in + n:n_in + n + n_out], refs[n_in + n + n_out:n_in + 2 * n + n_out]
        rest = refs[n_in + 2 * n + n_out:]
        own_scratch, sems = rest[:len(rest) - 3], rest[len(rest) - 3:]
        start, wait = _direct_exchange(scatter, x_refs, land_refs, *sems)
        ids = [pl.program_id(a) for a in range(len(grid))]
        is_first = functools.reduce(jnp.logical_and, [i == 0 for i in ids])
        is_last = functools.reduce(jnp.logical_and, [i == g - 1 for i, g in zip(ids, grid)])
        pl.when(is_first)(start)
        return tuple(ins) + tuple(outs) + tuple(own_scratch), lambda: pl.when(is_last)(wait)

    return list(xs), _exchange_shapes(scatter, xs), [hbm] * n, _exchange_scratch(n), hook


def _fox_fwd(q_aug, k_aug, qkv, name, t=TQ, exchange=None):
    s = q_aug.shape[0]
    t = min(t, s)
    n = s // t
    dh = HEAD_DIM
    grid = (N_HEADS // PAIR, n)
    ex_in, ex_out, ex_specs, ex_scratch, hook = _riding_exchange(exchange, 3, 2, grid)

    def body(*refs):
        (q_ref, k_ref, v_ref, o_ref, lse_ref, m_s, l_s, acc_s), finish = hook(refs)
        qi = pl.program_id(1)
        m_s[...] = jnp.full_like(m_s, NEG)
        l_s[...] = jnp.zeros_like(l_s)
        acc_s[...] = jnp.zeros_like(acc_s)

        def step(kj, diag):
            rows = pl.ds(pl.multiple_of(kj * t, t), t)
            for hh in range(PAIR):
                st = lax.dot_general(k_ref[rows, AUG * hh:AUG * (hh + 1)], q_ref[:, AUG * hh:AUG * (hh + 1)], _NT,
                                     preferred_element_type=F32)
                if diag:
                    st = jnp.where(_causal_keep(t, t), st, NEG)
                m_prev = m_s[hh]
                m_new = jnp.maximum(m_prev, jnp.max(st, axis=0, keepdims=True))
                a = jnp.exp(m_prev - m_new)
                pt = jnp.exp(st - m_new)
                l_s[hh] = a * l_s[hh] + jnp.sum(pt, axis=0, keepdims=True)
                acc_s[hh] = a * acc_s[hh] + lax.dot_general(v_ref[rows, dh * hh:dh * (hh + 1)], pt.astype(BF16), _TN,
                                                            preferred_element_type=F32)
                m_s[hh] = m_new

        def off_diagonal(kj, carry):
            step(kj, False)
            return carry

        lax.fori_loop(0, qi, off_diagonal, 0)
        step(qi, True)
        for hh in range(PAIR):
            o_ref[hh] = acc_s[hh] / l_s[hh]
            lse_ref[hh, 0] = m_s[hh] + jnp.log(l_s[hh])
        finish()

    return _pcall(
        body, name=name,
        out_shape=[jax.ShapeDtypeStruct((N_HEADS, dh, s), F32), jax.ShapeDtypeStruct((N_HEADS, n, 1, t), F32)] + ex_out,
        grid=grid,
        in_specs=[pl.BlockSpec((t, PAIR * AUG), lambda hp, i: (i, hp)),
                  pl.BlockSpec((s, PAIR * AUG), lambda hp, i: (0, hp)),
                  pl.BlockSpec((s, PAIR * dh), lambda hp, i: (0, V_FOX + hp))] + ex_specs,
        out_specs=[pl.BlockSpec((PAIR, dh, t), lambda hp, i: (hp, 0, i)),
                   pl.BlockSpec((PAIR, 1, 1, t), lambda hp, i: (hp, i, 0, 0))] + ex_specs,
        scratch_shapes=[pltpu.VMEM((PAIR, 1, t), F32), pltpu.VMEM((PAIR, 1, t), F32),
                        pltpu.VMEM((PAIR, dh, t), F32)] + ex_scratch,
        compiler_params=_cp("arbitrary", "arbitrary"),
    )(q_aug, k_aug, qkv, *ex_in)


def _row_dot(a_t, b_t, name, t=2048):
    h, dh, s = a_t.shape
    t = min(t, s)

    def body(a_ref, b_ref, o_ref):
        o_ref[...] = jnp.sum(a_ref[...].astype(F32) * b_ref[...], axis=0, keepdims=True)

    blk = pl.BlockSpec((None, dh, t), lambda hh, i: (hh, 0, i))
    return _pcall(
        body, name=name, out_shape=jax.ShapeDtypeStruct((h, 1, s), F32), grid=(h, s // t),
        in_specs=[blk, blk], out_specs=pl.BlockSpec((None, 1, t), lambda hh, i: (hh, 0, i)),
        compiler_params=_cp("parallel", "parallel"),
    )(a_t, b_t)


def _fox_bwd(q_aug, k_aug, qkv, do, lse, delta, name, t=TQ, exchange=None):
    s = q_aug.shape[0]
    dh = HEAD_DIM
    t = min(t, s)
    n = s // t
    grid = (N_HEADS // PAIR, n)
    ex_in, ex_out, ex_specs, ex_scratch, hook = _riding_exchange(exchange, 6, 5, grid)

    def body(*refs):
        (q_ref, k_ref, v_ref, do_ref, lse_ref, dl_ref, dk_ref, dv_ref, dq_ref, dsum_ref, dqsum_ref,
         dk_s, dv_s, dsum_s), finish = hook(refs)
        kj = pl.program_id(1)

        @pl.when(kj == 0)
        def _():
            dq_ref[...] = jnp.zeros_like(dq_ref)
            dqsum_ref[...] = jnp.zeros_like(dqsum_ref)

        dk_s[...] = jnp.zeros_like(dk_s)
        dv_s[...] = jnp.zeros_like(dv_s)
        dsum_s[...] = jnp.zeros_like(dsum_s)

        def step(qi, diag):
            rows = pl.ds(pl.multiple_of(qi * t, t), t)
            for hh in range(PAIR):
                q = q_ref[rows, AUG * hh:AUG * (hh + 1)]
                k = k_ref[:, AUG * hh:AUG * (hh + 1)]
                st = lax.dot_general(k, q, _NT, preferred_element_type=F32)
                if diag:
                    st = jnp.where(_causal_keep(t, t), st, NEG)
                pt = jnp.exp(st - lse_ref[hh, qi])
                do_v = do_ref[rows, dh * hh:dh * (hh + 1)]
                dpt = lax.dot_general(v_ref[:, dh * hh:dh * (hh + 1)], do_v, _NT, preferred_element_type=F32)
                ds32 = pt * (dpt - dl_ref[hh, qi])
                dsum_s[hh] += sum(ds32[:, 128 * u:128 * (u + 1)] for u in range(t // 128))
                dqsum_ref[hh, qi] += jnp.sum(ds32, axis=0, keepdims=True)
                dst = ds32.astype(BF16)
                dv_s[hh] += jnp.dot(pt.astype(BF16), do_v, preferred_element_type=F32)
                dk_s[hh] += jnp.dot(dst, q[:, :dh], preferred_element_type=F32)
                dq_ref[hh, qi] += lax.dot_general(k[:, :dh], dst, _TN, preferred_element_type=F32)

        def off_diagonal(qi, carry):
            step(qi, False)
            return carry

        step(kj, True)
        lax.fori_loop(kj + 1, n, off_diagonal, 0)
        for hh in range(PAIR):
            dk_ref[:, dh * hh:dh * (hh + 1)] = dk_s[hh].astype(dk_ref.dtype)
            dv_ref[:, dh * hh:dh * (hh + 1)] = dv_s[hh].astype(dv_ref.dtype)
            dsum_ref[:, 128 * hh:128 * (hh + 1)] = dsum_s[hh]
        finish()

    stat = pl.BlockSpec((PAIR, n, 1, t), lambda hp, j: (hp, 0, 0, 0))
    return _pcall(
        body, name=name,
        out_shape=[jax.ShapeDtypeStruct((s, WIDTH), BF16), jax.ShapeDtypeStruct((s, WIDTH), BF16),
                   jax.ShapeDtypeStruct((N_HEADS, n, dh, t), F32), jax.ShapeDtypeStruct((s, N_HEADS * 128), F32),
                   jax.ShapeDtypeStruct((N_HEADS, n, 1, t), F32)] + ex_out,
        grid=grid,
        in_specs=[pl.BlockSpec((s, PAIR * AUG), lambda hp, j: (0, hp)),
                  pl.BlockSpec((t, PAIR * AUG), lambda hp, j: (j, hp)),
                  pl.BlockSpec((t, PAIR * dh), lambda hp, j: (j, V_FOX + hp)),
                  pl.BlockSpec((s, PAIR * dh), lambda hp, j: (0, hp)),
                  stat, stat] + ex_specs,
        out_specs=[pl.BlockSpec((t, PAIR * dh), lambda hp, j: (j, hp)),
                   pl.BlockSpec((t, PAIR * dh), lambda hp, j: (j, hp)),
                   pl.BlockSpec((PAIR, n, dh, t), lambda hp, j: (hp, 0, 0, 0)),
                   pl.BlockSpec((t, PAIR * 128), lambda hp, j: (j, hp)),
                   stat] + ex_specs,
        scratch_shapes=[pltpu.VMEM((PAIR, t, dh), F32), pltpu.VMEM((PAIR, t, dh), F32),
                        pltpu.VMEM((PAIR, t, 128), F32)] + ex_scratch,
        compiler_params=_cp("arbitrary", "arbitrary"),
    )(q_aug, k_aug, qkv, do, lse, delta, *ex_in)


N_SUB = TQ // SUB


def _pair_blockdiag(x2):
    lane = lax.broadcasted_iota(jnp.int32, x2.shape, 1)
    zero = jnp.zeros_like(x2)
    return jnp.concatenate([jnp.where(lane < HEAD_DIM, x2, zero), jnp.where(lane >= HEAD_DIM, x2, zero)], axis=0)


def _pair_lanes(ref, j):
    return jnp.concatenate([ref[hh, 0, :, SUB * j:SUB * (j + 1)] for hh in range(PAIR)], axis=1)


def _chunk_logits(kwin, qbd, b_ref, first_tile, j):
    which = jnp.where(first_tile, 1 + j, 0)
    bias = jnp.concatenate([b_ref[hh, which] for hh in range(PAIR)], axis=1)
    return lax.dot_general(kwin, qbd, _NT, preferred_element_type=F32) + bias


def _chunk_fwd(qkv, bias_t, name):
    s = qkv.shape[0]
    dh = HEAD_DIM
    n = s // TQ
    nsub = TQ // SUB
    prev = lambda i: jnp.maximum(i - 1, 0)

    def body(q_ref, kp_ref, kc_ref, vp_ref, vc_ref, b_ref, o_ref, lse_ref):
        first_tile = pl.program_id(1) == 0
        kw = jnp.concatenate([kp_ref[...], kc_ref[...]], axis=0)
        vw = jnp.concatenate([vp_ref[...], vc_ref[...]], axis=0)
        for j in range(nsub):
            rows, win = slice(SUB * j, SUB * (j + 1)), slice(SUB * j, SUB * j + SUBK)
            st = _chunk_logits(kw[win], _pair_blockdiag(q_ref[rows, :] * QK_SCALE), b_ref, first_tile, j)
            m = jnp.max(st, axis=0, keepdims=True)
            pt = jnp.exp(st - m)
            l = jnp.sum(pt, axis=0, keepdims=True)
            ot = lax.dot_general(vw[win], pt.astype(BF16), _TN, preferred_element_type=F32)
            stat = m + jnp.log(l)
            for hh in range(PAIR):
                lanes = slice(SUB * hh, SUB * (hh + 1))
                o_ref[hh, :, rows] = ot[dh * hh:dh * (hh + 1), lanes] / l[:, lanes]
                lse_ref[hh, 0, :, rows] = stat[:, lanes]

    blk = lambda col, m: pl.BlockSpec((TQ, PAIR * dh), lambda hp, i: (m(i), col + hp))
    same = lambda i: i
    return _pcall(
        body, name=name,
        out_shape=(jax.ShapeDtypeStruct((N_HEADS, dh, s), F32), jax.ShapeDtypeStruct((N_HEADS, n, 1, TQ), F32)),
        grid=(N_HEADS // PAIR, n),
        in_specs=[blk(Q_CHUNK, same), blk(K_CHUNK, prev), blk(K_CHUNK, same), blk(V_CHUNK, prev), blk(V_CHUNK, same),
                  pl.BlockSpec((PAIR, 1 + N_SUB, SUBK, SUB), lambda hp, i: (hp, 0, 0, 0))],
        out_specs=(pl.BlockSpec((PAIR, dh, TQ), lambda hp, i: (hp, 0, i)),
                   pl.BlockSpec((PAIR, 1, 1, TQ), lambda hp, i: (hp, i, 0, 0))),
        compiler_params=_cp("parallel", "parallel"),
    )(qkv, qkv, qkv, qkv, qkv, bias_t)


def _chunk_bwd(qkv, do, lse, delta, bias_t, name):
    s = qkv.shape[0]
    dh = HEAD_DIM
    n = s // TQ
    nsub = TQ // SUB
    cur = lambda i: jnp.minimum(i, n - 1)
    prev = lambda i: jnp.maximum(cur(i) - 1, 0)
    done = lambda i: jnp.maximum(i - 1, 0)

    def body(q_ref, kp_ref, kc_ref, vp_ref, vc_ref, do_ref, lse_ref, dl_ref, b_ref,
             dq_ref, dk_ref, dv_ref, db_ref, dkw_s, dvw_s, ck_s, cv_s):
        i = pl.program_id(1)

        @pl.when(i == 0)
        def _():
            db_ref[...] = jnp.zeros_like(db_ref)
            ck_s[...] = jnp.zeros_like(ck_s)
            cv_s[...] = jnp.zeros_like(cv_s)

        dkw_s[...] = jnp.zeros_like(dkw_s)
        dvw_s[...] = jnp.zeros_like(dvw_s)

        @pl.when(i < n)
        def _():
            first_tile = i == 0
            kw = jnp.concatenate([kp_ref[...], kc_ref[...]], axis=0)
            vw = jnp.concatenate([vp_ref[...], vc_ref[...]], axis=0)
            for j in range(nsub):
                rows, win = slice(SUB * j, SUB * (j + 1)), slice(SUB * j, SUB * j + SUBK)
                qbd = _pair_blockdiag(q_ref[rows, :] * QK_SCALE)
                dobd = _pair_blockdiag(do_ref[rows, :])
                st = _chunk_logits(kw[win], qbd, b_ref, first_tile, j)
                pt = jnp.exp(st - _pair_lanes(lse_ref, j))
                dpt = lax.dot_general(vw[win], dobd, _NT, preferred_element_type=F32)
                dst = pt * (dpt - _pair_lanes(dl_ref, j))
                dsb = dst.astype(BF16)
                dvw_s[win, :] += jnp.dot(pt.astype(BF16), dobd, preferred_element_type=F32)
                dkw_s[win, :] += jnp.dot(dsb, qbd, preferred_element_type=F32)
                dqt = QK_SCALE * lax.dot_general(kw[win], dsb, _TN, preferred_element_type=F32)
                for hh in range(PAIR):
                    lanes = slice(SUB * hh, SUB * (hh + 1))
                    db_ref[hh] += dst[:, lanes]
                    dq_ref[hh, :, rows] = dqt[dh * hh:dh * (hh + 1), lanes]

        dk_ref[...] = (ck_s[...] + dkw_s[0:TQ, :]).astype(dk_ref.dtype)
        dv_ref[...] = (cv_s[...] + dvw_s[0:TQ, :]).astype(dv_ref.dtype)
        ck_s[...] = dkw_s[TQ:2 * TQ, :]
        cv_s[...] = dvw_s[TQ:2 * TQ, :]

    blk = lambda col, m: pl.BlockSpec((TQ, PAIR * dh), lambda hp, i: (m(i), col + hp))
    stat = pl.BlockSpec((PAIR, 1, 1, TQ), lambda hp, i: (hp, cur(i), 0, 0))
    bs = pl.BlockSpec((PAIR, SUBK, SUB), lambda hp, i: (hp, 0, 0))
    return _pcall(
        body, name=name,
        out_shape=(jax.ShapeDtypeStruct((N_HEADS, dh, s), F32), jax.ShapeDtypeStruct((s, WIDTH), BF16),
                   jax.ShapeDtypeStruct((s, WIDTH), BF16), jax.ShapeDtypeStruct((N_HEADS, SUBK, SUB), F32)),
        grid=(N_HEADS // PAIR, n + 1),
        in_specs=[blk(Q_CHUNK, cur), blk(K_CHUNK, prev), blk(K_CHUNK, cur), blk(V_CHUNK, prev), blk(V_CHUNK, cur),
                  blk(0, cur), stat, stat, pl.BlockSpec((PAIR, 1 + N_SUB, SUBK, SUB), lambda hp, i: (hp, 0, 0, 0))],
        out_specs=(pl.BlockSpec((PAIR, dh, TQ), lambda hp, i: (hp, 0, cur(i))), blk(0, done), blk(0, done), bs),
        scratch_shapes=[pltpu.VMEM((2 * TQ, PAIR * dh), F32), pltpu.VMEM((2 * TQ, PAIR * dh), F32),
                        pltpu.VMEM((TQ, PAIR * dh), F32), pltpu.VMEM((TQ, PAIR * dh), F32)],
        compiler_params=_cp("parallel", "arbitrary"),
    )(qkv, qkv, qkv, qkv, qkv, do, lse, delta, bias_t)


def _mod_part(c_all, w_ada, b_ada, name):
    nl, d, n = w_ada.shape
    b = c_all.shape[0]

    def body(c_ref, w_ref, b_ref, o_ref, cond_ref):
        cv = c_ref[...]
        cond = cv * _sigmoid(cv)
        cond_ref[...] = cond
        o_ref[...] = jnp.dot(cond, w_ref[...], preferred_element_type=F32, precision=HI) + b_ref[...]

    return _pcall(
        body, name=name,
        out_shape=(jax.ShapeDtypeStruct((nl, b, n), F32), jax.ShapeDtypeStruct((b, d), F32)),
        grid=(nl,),
        in_specs=[pl.BlockSpec((b, d), lambda l: (0, 0)), pl.BlockSpec((None, d, n), lambda l: (l, 0, 0)),
                  pl.BlockSpec((None, 1, n), lambda l: (l, 0, 0))],
        out_specs=(pl.BlockSpec((None, b, n), lambda l: (l, 0, 0)), pl.BlockSpec((b, d), lambda l: (0, 0))),
        compiler_params=_cp("arbitrary"),
    )(c_all, w_ada, b_ada)


def _adamw(parts, w, m, v, name, tr=1024):
    p, rl, cp = parts[0].shape
    r, c = w.shape
    tr = _tile(rl, tr, 16)
    per = rl // tr
    c1 = 1.0 / (1.0 - ADAM_B1 ** ADAM_STEP)
    c2 = 1.0 / (1.0 - ADAM_B2 ** ADAM_STEP)

    def body(*refs):
        p_refs, (w_ref, m_ref, v_ref, g_ref, d_ref, nm_ref, nv_ref) = refs[:len(parts)], refs[len(parts):]
        for which, p_ref in enumerate(p_refs):
            pl.when(pl.program_id(0) // per == which)(
                functools.partial(update, p_ref, w_ref, m_ref, v_ref, g_ref, d_ref, nm_ref, nv_ref))

    def update(p_ref, w_ref, m_ref, v_ref, g_ref, d_ref, nm_ref, nv_ref):
        g = p_ref[0].astype(F32)
        for i in range(1, p):
            g = g + p_ref[i].astype(F32)
        g = g[:, :c]
        nm = ADAM_B1 * m_ref[...] + (1.0 - ADAM_B1) * g
        nv = ADAM_B2 * v_ref[...] + (1.0 - ADAM_B2) * (g * g)
        g_ref[...] = g
        nm_ref[...] = nm
        nv_ref[...] = nv
        d_ref[...] = -ADAM_LR * ((nm * c1) / (jnp.sqrt(nv * c2) + ADAM_EPS) + ADAM_WD * w_ref[...])

    blk = pl.BlockSpec((tr, c), lambda i: (i, 0))
    out = jax.ShapeDtypeStruct((r, c), F32)
    return _pcall(
        body, name=name, out_shape=(out, out, out, out), grid=(r // tr,),
        in_specs=[pl.BlockSpec((p, tr, cp), lambda i, _w=which: (0, jnp.clip(i - _w * per, 0, per - 1), 0))
                  for which in range(len(parts))] + [blk, blk, blk],
        out_specs=(blk, blk, blk, blk),
        compiler_params=_cp("parallel"),
    )(*parts, w, m, v)


def _pad_lanes(a):
    pad = (-a.shape[-1]) % 128
    return jnp.pad(a, [(0, 0)] * (a.ndim - 1) + [(0, pad)]) if pad else a


def _rows128(a, mult=16):
    flat = a.reshape(-1)
    n = flat.shape[0]
    per = 128 * mult
    pad = (-n) % per
    if pad:
        flat = jnp.concatenate([flat, jnp.zeros((pad,), a.dtype)])
    return flat.reshape(-1, 128)


def _pack(arrs, mult=16):
    pieces = [_rows128(a, mult) for a in arrs]
    return jnp.concatenate(pieces, axis=0), [p.shape[0] for p in pieces]


def _unpack(packed, rows, shapes, lead=()):
    out, at = [], 0
    for r, shp in zip(rows, shapes):
        n = int(np.prod(shp))
        piece = packed[..., at:at + r, :].reshape(lead + (r * 128,))[..., :n]
        out.append(piece.reshape(lead + tuple(shp)))
        at += r
    return out


def _rel_bias_tile(table):
    h = table.shape[0]
    lo = REL_CLIP - (CHUNK - 1)
    n_far = LEFT_CHUNKS * CHUNK + CHUNK - 1 - REL_CLIP
    vec = jnp.concatenate([table[:, lo:2 * REL_CLIP], jnp.repeat(table[:, 2 * REL_CLIP:], n_far + 1, axis=1)], axis=1)
    rev = vec[:, ::-1]
    n_vec = BAND + CHUNK - 1
    skew = jnp.tile(rev, (1, CHUNK + 1))[:, :CHUNK * (n_vec + 1)].reshape(h, CHUNK, n_vec + 1)
    bias = skew[:, ::-1, :BAND]
    neg = jnp.full((h, CHUNK, CHUNK), NEG, F32)
    two = jnp.concatenate([jnp.concatenate([bias, neg], axis=2), jnp.concatenate([neg, bias], axis=2)], axis=1)
    plain = two.transpose(0, 2, 1)
    key = np.arange(SUBK)[:, None]
    return jnp.stack([plain] + [jnp.where(key >= TQ - SUB * j, plain, NEG) for j in range(N_SUB)], axis=1)


def _rel_bias_tile_grad(dbias_t):
    h = dbias_t.shape[0]
    two = dbias_t.transpose(0, 2, 1)
    dbias = two[:, :CHUNK, :BAND] + two[:, CHUNK:, CHUNK:]
    n_vec = BAND + CHUNK - 1
    dskew = jnp.pad(dbias[:, ::-1, :], ((0, 0), (0, 0), (0, n_vec + 1 - BAND))).reshape(h, CHUNK * (n_vec + 1))
    dskew = jnp.pad(dskew, ((0, 0), (0, (CHUNK + 1) * n_vec - CHUNK * (n_vec + 1))))
    drev = jnp.sum(dskew.reshape(h, CHUNK + 1, n_vec), axis=1)
    dvec = drev[:, ::-1]
    lo = REL_CLIP - (CHUNK - 1)
    n_near = 2 * REL_CLIP - lo
    return jnp.concatenate([jnp.zeros((h, lo), F32), dvec[:, :n_near],
                            jnp.sum(dvec[:, n_near:], axis=1, keepdims=True)], axis=1)


BIG = ("w_in", "w_br_fox", "w_br_chunk", "w_out", "w_up", "w_down")
SMALL = ("b_f", "rel_bias", "conv_b", "b_ada", "ln1_g", "ln1_b", "ln2_g", "ln2_b")
ORDER = ("w_in", "b_f", "rel_bias", "w_br_fox", "w_br_chunk", "w_out", "w_up", "conv_w", "conv_b", "w_down",
         "w_ada", "b_ada", "ln1_g", "ln1_b", "ln2_g", "ln2_b")


def _dest_major(g, axis):
    shp = g.shape
    g = g.reshape(shp[:axis] + (N_DEV, shp[axis] // N_DEV) + shp[axis + 1:])
    return jnp.moveaxis(g, axis, 0)


EARLY = ("w_down", "w_up", "w_out", "w_br_fox", "w_br_chunk")
LATE = ("w_in",)
REST = ("w_br_fox", "w_br_chunk", "w_out", "w_up", "w_down")


def _shard_axis(n):
    return 0 if n in ("w_out", "w_down") else 1


def _slab(g, n):
    return _pad_lanes(_dest_major(g, _shard_axis(n)).astype(BF16))


def _layer_fwd(x, h1, wl, rest_of, mod, next_mod, alpha, gather):
    sh1, sc1, g1, sh2, sc2, g2 = mod
    s = x.shape[0]
    qkv = _mm(h1, wl["w_qkv"], out_dtype=BF16, name="proj_qkv", tk=1024)
    gates = _mm(h1, wl["w_gates"], out_dtype=F32, name="proj_gates", tk=1024)
    f_t = _mm(wl["w_f_t"], h1, out_dtype=F32, name="proj_f", tb=True, tk=1024)[:N_HEADS]
    f_t = f_t.reshape(N_HEADS, s // 128, 128)
    b_f = jnp.broadcast_to(wl["b_f"].reshape(N_HEADS, 1, 1), (N_HEADS, 1, 128))
    cum = _forget_fwd(f_t, b_f, "forget_fwd").reshape(N_SPLIT, N_HEADS, s)
    cum_cols = cum.transpose(2, 1, 0).astype(BF16)

    ones = jnp.ones((s, N_HEADS, N_SPLIT), BF16)
    zeros = jnp.zeros((s, N_HEADS, AUG - HEAD_DIM - 2 * N_SPLIT), BF16)
    q_a = qkv[:, :WIDTH].reshape(s, N_HEADS, HEAD_DIM) * QK_SCALE
    k_a = qkv[:, WIDTH:2 * WIDTH].reshape(s, N_HEADS, HEAD_DIM)
    q_aug = jnp.concatenate([q_a, cum_cols, ones, zeros], axis=-1).reshape(s, N_HEADS * AUG)
    k_aug = jnp.concatenate([k_a, ones, -cum_cols, zeros], axis=-1).reshape(s, N_HEADS * AUG)
    o_a_t, lse_a, *gathered = _fox_fwd(q_aug, k_aug, qkv, "fox_fwd_gather_%d" % len(gather), exchange=(False, gather))
    wl.update(rest_of(gathered[:len(REST)]))
    gathered = gathered[len(REST):]
    bias_t = _rel_bias_tile(wl["rel_bias"])
    o_c_t, lse_c = _chunk_fwd(qkv, bias_t, "chunk_fwd")
    o_a = o_a_t.transpose(2, 0, 1).reshape(s, WIDTH).astype(BF16)
    o_c = o_c_t.transpose(2, 0, 1).reshape(s, WIDTH).astype(BF16)

    merged = _merge_fwd(o_a, o_c, wl["w_br_fox"], wl["w_br_chunk"], gates, "merge_fwd")
    mix, x1, h2 = _mm_res_ln(merged, wl["w_out"], x, g1, wl["ln1_g"], wl["ln1_b"], sc2, sh2, alpha, "out_ln1")
    u0 = _mm(h2, wl["w_up"], out_dtype=F32, name="ffn_up", tk=1024, tn=1408, o_halves=True)
    act = _conv_act_fwd(u0, wl["conv_w"], wl["conv_b"], "conv_act_fwd")
    y2, x2, h_next = _mm_res_ln(act, wl["w_down"], x1, g2, wl["ln2_g"], wl["ln2_b"], next_mod[0], next_mod[1],
                                alpha, "down_ln2", tk=1408)
    saved = dict(x=x, h1=h1, gates=gates, f_t=f_t, b_f=b_f, q_aug=q_aug, k_aug=k_aug, qkv=qkv,
                 bias_t=bias_t, o_a_t=o_a_t, o_c_t=o_c_t, lse_a=lse_a, lse_c=lse_c,
                 o_a=o_a, o_c=o_c, merged=merged, mix=mix, x1=x1, h2=h2, u0=u0, act=act, y2=y2)
    return x2, h_next, saved, gathered


def _layer_bwd(dx2, sv, wl, mod, alpha, send, ride_late):
    sh1, sc1, g1, sh2, sc2, g2 = mod
    s = dx2.shape[0]
    g = {}
    dres2, dy2, sums = _ln_res_bwd(dx2, sv["x1"], sv["y2"], g2, wl["ln2_g"], alpha, "ln2_bwd")
    g["ln2_g"], g["ln2_b"], dg2 = sums[0], sums[1], sums[2]
    dact = _mm(dy2, wl["w_down_t"], out_dtype=F32, name="dact", tk=1024, tn=1408)
    g["w_down"] = _mm(sv["act"], dy2, out_dtype=F32, name="g_w_down", ta=True, tm=1408, tk=2048)
    du0, csum = _conv_act_bwd(dact, sv["u0"], wl["conv_w"], wl["conv_b"], "conv_act_bwd")
    g["conv_w"] = jnp.concatenate([csum[0, 0:3], csum[1, 0:3]], axis=1)
    g["conv_b"] = jnp.concatenate([csum[0, 3], csum[1, 3]])
    dh2 = _mm(du0, wl["w_up_t"], out_dtype=F32, name="dh2", tk=2816, a_halves=True)
    g["w_up"] = _mm(sv["h2"], du0, out_dtype=F32, name="g_w_up", ta=True, tk=2048, tn=1408, b_halves=True)
    dx1, sums = _ln_mod_bwd(dh2, sv["x1"], sc2, dres2, "ln_mod2_bwd")
    dsh2, dsc2 = sums[0], sums[1]

    dres1, dmix, sums = _ln_res_bwd(dx1, sv["x"], sv["mix"], g1, wl["ln1_g"], alpha, "ln1_bwd")
    g["ln1_g"], g["ln1_b"], dg1 = sums[0], sums[1], sums[2]
    g["w_out"] = _mm(sv["merged"], dmix, out_dtype=F32, name="g_w_out", ta=True, tk=2048)
    dba, dbc, do_a, do_c, dgates = _merge_bwd(
        dmix, wl["w_out_t"], sv["o_a"], sv["o_c"], wl["w_br_fox"], wl["w_br_chunk"],
        wl["w_br_fox_t"], wl["w_br_chunk_t"], sv["gates"], "merge_bwd")
    g["w_br_fox"] = _mm(sv["o_a"], dba, out_dtype=F32, name="g_w_br_fox", ta=True, tk=2048)
    g["w_br_chunk"] = _mm(sv["o_c"], dbc, out_dtype=F32, name="g_w_br_chunk", ta=True, tk=2048)
    n_t = s // min(TQ, s)
    stats = lambda a: a.reshape(N_HEADS, n_t, 1, s // n_t)

    delta_a = _row_dot(do_a.T.reshape(N_HEADS, HEAD_DIM, s), sv["o_a_t"], "delta_fox")
    send = list(send) + [_slab(g[n], n) for n in EARLY]
    dk_a, dv_a, dq_a_t, dsum, dqsum, *landed = _fox_bwd(sv["q_aug"], sv["k_aug"], sv["qkv"], do_a, sv["lse_a"],
                                                        stats(delta_a), "fox_bwd_scatter_%d" % len(send),
                                                        exchange=(True, send))
    dq_a = (dq_a_t * QK_SCALE).transpose(1, 3, 0, 2).reshape(s, WIDTH).astype(BF16)
    d_cum = dqsum.reshape(N_HEADS, s) - jnp.sum(dsum.reshape(s, N_HEADS, 128), axis=-1).T
    df_t, db_f = _forget_bwd(d_cum.reshape(N_HEADS, s // 128, 128), sv["f_t"], sv["b_f"], "forget_bwd")
    g["b_f"] = db_f[:, 0, 0]
    df_t = df_t.reshape(N_HEADS, s)

    delta_c = _row_dot(do_c.T.reshape(N_HEADS, HEAD_DIM, s), sv["o_c_t"], "delta_chunk")
    dq_c_t, dk_c, dv_c, dbias_t = _chunk_bwd(sv["qkv"], do_c, sv["lse_c"], stats(delta_c), sv["bias_t"], "chunk_bwd")
    g["rel_bias"] = _rel_bias_tile_grad(dbias_t)
    dq_c = dq_c_t.transpose(2, 0, 1).reshape(s, WIDTH).astype(BF16)

    dqkv = jnp.concatenate([dq_a, dk_a, dv_a, dq_c, dk_c, dv_c], axis=1)
    df_pad = jnp.zeros((16 - N_HEADS, s), F32)
    df16 = jnp.concatenate([df_t, df_pad], axis=0).astype(BF16)
    df_cols = jnp.concatenate([df16.T, jnp.zeros((s, AUG - 16), BF16)], axis=1)

    g_qkv = _mm(sv["h1"], dqkv, out_dtype=F32, name="g_w_qkv", ta=True, tk=2048)
    g_gates = _mm(sv["h1"], dgates, out_dtype=F32, name="g_w_gates", ta=True, tk=2048)
    g_f_t = _mm(df16, sv["h1"], out_dtype=F32, name="g_w_f", tk=1024)[:N_HEADS]
    g["w_in"] = jnp.concatenate([g_qkv[:, :3 * WIDTH], g_f_t.T, g_qkv[:, 3 * WIDTH:], g_gates], axis=1)
    late = [_slab(g[n], n) for n in LATE] if ride_late else None
    dh1, *landed_late = _mm(dqkv, wl["w_qkv_t"], out_dtype=F32, name="dh1_qkv_scatter", tk=1536,
                            exchange=(True, late)) if ride_late else \
        [_mm(dqkv, wl["w_qkv_t"], out_dtype=F32, name="dh1_qkv", tk=1536)]
    dh1g = _mm(dgates, wl["w_gates_t"], out_dtype=F32, name="dh1_gates", tk=2048)
    dx, sums = _ln_mod_bwd(dh1, sv["x"], sc1, dres1, "ln_mod1_bwd", extra=(dh1g, df_cols, wl["w_f_pad"]))
    dsh1, dsc1 = sums[0], sums[1]
    g["mod"] = jnp.concatenate([dsh1, dsc1, dg1, dsh2, dsc2, dg2])
    return dx, g, landed, landed_late


def kernel(x, c, w_in, b_f, rel_bias, w_br_fox, w_br_chunk, w_out, w_up, conv_w, conv_b, w_down, w_ada, b_ada, ln1_g, ln1_b, ln2_g, ln2_b, loss_target, m_w_in, m_b_f, m_rel_bias, m_w_br_fox, m_w_br_chunk, m_w_out, m_w_up, m_conv_w, m_conv_b, m_w_down, m_w_ada, m_b_ada, m_ln1_g, m_ln1_b, m_ln2_g, m_ln2_b, v_w_in, v_b_f, v_rel_bias, v_w_br_fox, v_w_br_chunk, v_w_out, v_w_up, v_conv_w, v_conv_b, v_w_down, v_w_ada, v_b_ada, v_ln1_g, v_ln1_b, v_ln2_g, v_ln2_b):
    w = dict(w_in=w_in, b_f=b_f, rel_bias=rel_bias, w_br_fox=w_br_fox, w_br_chunk=w_br_chunk, w_out=w_out,
             w_up=w_up, conv_w=conv_w, conv_b=conv_b, w_down=w_down, w_ada=w_ada, b_ada=b_ada,
             ln1_g=ln1_g, ln1_b=ln1_b, ln2_g=ln2_g, ln2_b=ln2_b)
    m = dict(w_in=m_w_in, b_f=m_b_f, rel_bias=m_rel_bias, w_br_fox=m_w_br_fox, w_br_chunk=m_w_br_chunk,
             w_out=m_w_out, w_up=m_w_up, conv_w=m_conv_w, conv_b=m_conv_b, w_down=m_w_down, w_ada=m_w_ada,
             b_ada=m_b_ada, ln1_g=m_ln1_g, ln1_b=m_ln1_b, ln2_g=m_ln2_g, ln2_b=m_ln2_b)
    v = dict(w_in=v_w_in, b_f=v_b_f, rel_bias=v_rel_bias, w_br_fox=v_w_br_fox, w_br_chunk=v_w_br_chunk,
             w_out=v_w_out, w_up=v_w_up, conv_w=v_conv_w, conv_b=v_conv_b, w_down=v_w_down, w_ada=v_w_ada,
             b_ada=v_b_ada, ln1_g=v_ln1_g, ln1_b=v_ln1_b, ln2_g=v_ln2_g, ln2_b=v_ln2_b)
    depth, d, _ = w_in.shape
    s = x.shape[1]
    alpha = (2.0 * depth) ** 0.25
    me = 4 * lax.axis_index("x") + 2 * lax.axis_index("y") + lax.axis_index("c")
    x0 = x.reshape(s, d)
    target = loss_target.reshape(s, d)

    small_in, small_rows = _pack([c, conv_w], mult=8)
    small_all = _all_gather([small_in], "gather_c_conv")[0]
    c_all, conv_w_all = _unpack(small_all, small_rows, [c.shape, conv_w.shape], lead=(N_DEV,))
    c_all = c_all.reshape(N_DEV, d)
    conv_w_full = conv_w_all.transpose(1, 2, 0, 3).reshape(depth, conv_w.shape[1], -1)
    n_ada = w_ada.shape[2]
    b_ada_mine = lax.dynamic_slice_in_dim(b_ada, me * n_ada, n_ada, axis=1).reshape(depth, 1, n_ada)
    mod_part, cond_all = _mod_part(c_all, w_ada, b_ada_mine, "mod_part")
    mod_all = _all_gather([mod_part.reshape(depth * N_DEV, n_ada)], "gather_mod")[0]
    mod_all = mod_all.reshape(N_DEV, depth, N_DEV, n_ada)
    mod_mine = lax.dynamic_index_in_dim(mod_all, me, axis=2, keepdims=False)
    mod_mine = mod_mine.transpose(1, 0, 2).reshape(depth, N_MOD, 1, d)

    shard = lambda n, l: _pad_lanes(w[n][l].astype(BF16))
    cols = np.cumsum([0, WIDTH, WIDTH, WIDTH, N_HEADS, WIDTH, WIDTH, WIDTH, d, d])

    def whole(n, p):
        rows, ncol = w[n].shape[1:]
        p = jnp.moveaxis(p[..., :ncol], 0, _shard_axis(n))
        return p.reshape((rows * N_DEV, ncol) if _shard_axis(n) == 0 else (rows, ncol * N_DEV))

    def first_weights(l, w_in_parts):
        wi = whole("w_in", w_in_parts)
        w_qkv = jnp.concatenate([wi[:, :cols[3]], wi[:, cols[4]:cols[7]]], axis=1)
        w_gates = wi[:, cols[7]:]
        w_f_t = jnp.concatenate([wi[:, cols[3]:cols[4]].T, jnp.zeros((16 - N_HEADS, d), BF16)], axis=0)
        w_f_pad = jnp.concatenate([w_f_t, jnp.zeros((AUG - 16, d), BF16)], axis=0)
        row = lambda a: a[l].reshape(1, -1)
        return dict(
            w_qkv=w_qkv, w_gates=w_gates, w_f_t=w_f_t, w_f_pad=w_f_pad, w_qkv_t=w_qkv.T, w_gates_t=w_gates.T,
            conv_w=conv_w_full[l].reshape(3, 2, -1).transpose(1, 0, 2), conv_b=conv_b[l].reshape(2, 1, -1),
            b_f=b_f[l], rel_bias=rel_bias[l],
            ln1_g=row(ln1_g), ln1_b=row(ln1_b), ln2_g=row(ln2_g), ln2_b=row(ln2_b))

    def rest_weights(parts):
        out = {}
        for n, p in zip(REST, parts):
            out[n] = whole(n, p)
            out[n + "_t"] = out[n].T
        return out

    xs, saved, layers = x0, [], []
    hs = _ln_mod(x0, mod_mine[0][1], mod_mine[0][0], "ln_mod1")
    w_in_parts = _all_gather([shard("w_in", 0)], "gather_weights")[0]
    for l in range(depth):
        layers.append(first_weights(l, w_in_parts))
        nxt = mod_mine[min(l + 1, depth - 1)]
        gather = [shard(n, l) for n in REST] + ([shard("w_in", l + 1)] if l + 1 < depth else [])
        xs, hs, sv, extra = _layer_fwd(xs, hs, layers[l], rest_weights, list(mod_mine[l]), (nxt[1], nxt[0]), alpha,
                                       gather)
        w_in_parts = extra[0] if extra else None
        saved.append(sv)
    dxs, loss_part = _loss_grad(xs, target, "loss_grad")
    loss = lax.psum(loss_part[0, 0], AXES)
    grads, big_parts, send = [None] * depth, {}, []
    for l in reversed(range(depth)):
        dxs, grads[l], landed, landed_late = _layer_bwd(dxs, saved[l], layers[l], list(mod_mine[l]), alpha, send,
                                                        ride_late=(l == 0))
        for n, part in zip([(n, l + 1) for n in LATE if send] + [(n, l) for n in EARLY], landed):
            big_parts[n] = part
        for n, part in zip(LATE, landed_late):
            big_parts[n, l] = part
        send = [_slab(grads[l][n], n) for n in LATE] if l else []
    grad_x = dxs.reshape(1, s, d)
    stack = lambda n: jnp.stack([grads[l][n] for l in range(depth)])

    rep = ["b_ada"] + [n for n in SMALL if n != "b_ada"]
    small_g = [stack("mod")] + [stack(n) for n in rep[1:]] + [stack("conv_w")]
    small_pack, small_g_rows = _pack(small_g, mult=8)
    small_parts = _all_gather([small_pack], "gather_small_grads")[0]
    n_rep_rows = sum(small_g_rows[:-1])
    dmod_all = _unpack(small_parts, small_g_rows[:1], [(depth, N_MOD * d)], lead=(N_DEV,))[0]
    cw_all = _unpack(small_parts[:, n_rep_rows:], small_g_rows[-1:], [small_g[-1].shape], lead=(N_DEV,))[0]

    out = {}
    for n in BIG:
        rows2d = lambda a: a.reshape(-1, a.shape[-1])
        res = _adamw([big_parts[n, l] for l in range(depth)], rows2d(w[n]), rows2d(m[n]), rows2d(v[n]),
                     "adamw_" + n, tr=256)
        for kind, r in zip(("grad", "delta", "new_m", "new_v"), res):
            out[kind, n] = r.reshape(w[n].shape)

    dmod_mine = lax.dynamic_slice_in_dim(dmod_all, me * n_ada, n_ada, axis=2)
    g_ada = jnp.stack([_mm(cond_all.T, dmod_mine[:, l], out_dtype=F32, name="g_w_ada", tk=N_DEV, tn=n_ada,
                           precision=HI) for l in range(depth)])
    lp = [_rows128(t["w_ada"], 8) for t in (w, m, v)]
    res = _adamw([_rows128(g_ada, 8)[None]], *lp, "adamw_ada")
    for kind, r in zip(("grad", "delta", "new_m", "new_v"), res):
        out[kind, "w_ada"] = _unpack(r, [lp[0].shape[0]], [w_ada.shape])[0]

    rp = [_pack([t[n] for n in rep], mult=8)[0] for t in (w, m, v)]
    res = _adamw([small_parts[:, :n_rep_rows]], *rp, "adamw_small")
    for kind, r in zip(("grad", "delta", "new_m", "new_v"), res):
        for n, a in zip(rep, _unpack(r, small_g_rows[:-1], [w[n].shape for n in rep])):
            out[kind, n] = a

    n_cw = conv_w.shape[2]
    cw_mine = lax.dynamic_slice_in_dim(cw_all, me * n_cw, n_cw, axis=3)
    cp = [_rows128(t["conv_w"], 8) for t in (w, m, v)]
    cw_in = jnp.stack([_rows128(cw_mine[i], 8) for i in range(N_DEV)])
    res = _adamw([cw_in], *cp, "adamw_conv_w")
    for kind, r in zip(("grad", "delta", "new_m", "new_v"), res):
        out[kind, "conv_w"] = _unpack(r, [cp[0].shape[0]], [conv_w.shape])[0]

    result = [loss, grad_x]
    for kind in ("grad", "delta", "new_m", "new_v"):
        result += [out[kind, n] for n in ORDER]
    return tuple(result)
```

```python
import functools
import math

import jax
import jax.numpy as jnp
import numpy as np
from jax import lax
from jax.experimental import pallas as pl
from jax.experimental.pallas import tpu as pltpu

F32 = jnp.float32
BF16 = jnp.bfloat16
HI = lax.Precision.HIGHEST
MESH_ID = pl.DeviceIdType.MESH
AXES = ("x", "y", "c")
N_DEV = 8

N_HEADS = 8
HEAD_DIM = 64
WIDTH = N_HEADS * HEAD_DIM
CHUNK = 64
LEFT_CHUNKS = 8
BAND = (LEFT_CHUNKS + 1) * CHUNK
REL_CLIP = 128
LN_EPS = 1e-5
N_MOD = 6
QK_SCALE = 1.0 / math.sqrt(HEAD_DIM)
LOG2E = math.log2(math.e)
LN2 = math.log(2.0)
QK_SCALE2 = QK_SCALE * LOG2E
NEG = -1e30
AUG = 128
N_SPLIT = 3
TQ = 512
SUB = 2 * CHUNK
SUBK = BAND + CHUNK

ADAM_LR, ADAM_B1, ADAM_B2, ADAM_EPS, ADAM_WD, ADAM_STEP = 0.001, 0.9, 0.999, 1e-08, 0.01, 10

VMEM_LIMIT = 56 * 2 ** 20


def _pcall(body, **kw):
    return pl.pallas_call(body, **kw)


def _cp(*sem):
    return pltpu.CompilerParams(dimension_semantics=sem if sem else None, vmem_limit_bytes=VMEM_LIMIT)


def _my_place():
    return lax.axis_index("x"), lax.axis_index("y"), lax.axis_index("c")


N_PEER = N_DEV - 1


def _comm_call(body, xs, out_shapes, name):
    n = len(xs)
    hbm = pl.BlockSpec(memory_space=pltpu.HBM)
    return _pcall(
        functools.partial(body, n), name=name, out_shape=out_shapes, in_specs=[hbm] * n, out_specs=[hbm] * n,
        scratch_shapes=[pltpu.SemaphoreType.DMA((N_PEER * n,)), pltpu.SemaphoreType.DMA((N_PEER * n,)),
                        pltpu.SemaphoreType.DMA((n,))],
    )(*xs)


def _all_gather(xs, name):
    def body(n, *refs):
        x_refs, out_refs, (send_sems, recv_sems, local_sems) = refs[:n], refs[n:2 * n], refs[2 * n:]
        mx, my, mc = _my_place()
        me, sibling = (mx, my, mc), (mx, my, 1 - mc)
        chips = [(1 - mx, my), (mx, 1 - my), (1 - mx, 1 - my)]

        def slot(a, px, py, pc):
            return out_refs[a].at[4 * px + 2 * py + pc]

        def copy(a, k, block, to, own=False):
            return pltpu.make_async_remote_copy(
                src_ref=x_refs[a] if own else slot(a, *block), dst_ref=slot(a, *block),
                send_sem=send_sems.at[a * N_PEER + k], recv_sem=recv_sems.at[a * N_PEER + k],
                device_id=to, device_id_type=MESH_ID)

        arrays = range(n)
        mine = [pltpu.make_async_copy(x_refs[a], slot(a, *me), local_sems.at[a]) for a in arrays]
        first = [copy(a, 0, me, sibling, own=True) for a in arrays]
        first += [copy(a, 1 + j, me, (*chip, mc), own=True) for a in arrays for j, chip in enumerate(chips)]
        for cp in mine + first:
            cp.start()
        passed = []
        for a in arrays:
            for j, chip in enumerate(chips):
                copy(a, 1 + j, (*chip, mc), me).wait_recv()
                passed.append(copy(a, 4 + j, (*chip, mc), sibling))
                passed[-1].start()
        for a in arrays:
            copy(a, 0, sibling, me).wait_recv()
            for j, chip in enumerate(chips):
                copy(a, 4 + j, (*chip, 1 - mc), me).wait_recv()
        for cp in first + passed:
            cp.wait_send()
        for cp in mine:
            cp.wait()

    return _comm_call(body, xs, [jax.ShapeDtypeStruct((N_DEV,) + x.shape, x.dtype) for x in xs], name)


def _direct_exchange(scatter, x_refs, out_refs, send_sems, recv_sems, local_sems):
    mx, my, mc = _my_place()
    me = 4 * mx + 2 * my + mc
    mine, copies = [], []
    for a, (x_ref, out_ref) in enumerate(zip(x_refs, out_refs)):
        mine.append(pltpu.make_async_copy(x_ref.at[me] if scatter else x_ref, out_ref.at[me], local_sems.at[a]))
        for k in range(N_PEER):
            px, py, pc = mx ^ ((k + 1) >> 2), my ^ (((k + 1) >> 1) & 1), mc ^ ((k + 1) & 1)
            copies.append(pltpu.make_async_remote_copy(
                src_ref=x_ref.at[4 * px + 2 * py + pc] if scatter else x_ref, dst_ref=out_ref.at[me],
                send_sem=send_sems.at[a * N_PEER + k], recv_sem=recv_sems.at[a * N_PEER + k],
                device_id=(px, py, pc), device_id_type=MESH_ID))

    def start():
        for cp in mine + copies:
            cp.start()

    def wait():
        for cp in copies:
            cp.wait_recv()
        for cp in copies:
            cp.wait_send()
        for cp in mine:
            cp.wait()

    return start, wait


def _exchange_shapes(scatter, xs):
    return [jax.ShapeDtypeStruct(x.shape if scatter else (N_DEV,) + x.shape, x.dtype) for x in xs]


def _exchange_scratch(n):
    return [pltpu.SemaphoreType.DMA((N_PEER * n,)), pltpu.SemaphoreType.DMA((N_PEER * n,)),
            pltpu.SemaphoreType.DMA((n,))]


def _tile(n, pref, unit=128):
    if n <= pref:
        return n
    t = pref - pref % unit
    while t > unit and n % t:
        t -= unit
    assert n % t == 0, (n, pref, unit)
    return t


def _mm(a, b, *, out_dtype, name, ta=False, tb=False, tm=1024, tn=1024, tk=512, precision=None,
        a_halves=False, b_halves=False, o_halves=False, exchange=None):
    a_shape = (a.shape[1], 2 * a.shape[2]) if a_halves else a.shape
    b_shape = (b.shape[1], 2 * b.shape[2]) if b_halves else b.shape
    (k_dim, m) = a_shape if ta else a_shape[::-1]
    n = b_shape[0] if tb else b_shape[1]
    tm, tn = _tile(m, tm, 8), _tile(n // 2 if (b_halves or o_halves) else n, tn)
    tk = _tile(k_dim // 2 if a_halves else k_dim, tk)
    nk = k_dim // tk
    dims = (((0 if ta else 1,), (1 if tb else 0,)), ((), ()))
    grid = (m // tm, n // tn, nk)
    ex_in, ex_out, ex_specs, ex_scratch, hook = _riding_exchange(exchange, 2, 1, grid)

    def body(*refs):
        (a_ref, b_ref, o_ref, acc_ref), finish = hook(refs)
        k = pl.program_id(2)
        if nk == 1:
            o_ref[...] = lax.dot_general(a_ref[...], b_ref[...], dims, preferred_element_type=F32,
                                         precision=precision).astype(o_ref.dtype)
            finish()
            return

        @pl.when(k == 0)
        def _():
            acc_ref[...] = jnp.zeros_like(acc_ref)

        acc_ref[...] += lax.dot_general(a_ref[...], b_ref[...], dims, preferred_element_type=F32,
                                        precision=precision)

        @pl.when(k == nk - 1)
        def _():
            o_ref[...] = acc_ref[...].astype(o_ref.dtype)

        finish()

    def spec(shape2, pick, halves, per_half):
        if not halves:
            return pl.BlockSpec(shape2, pick)

        def index(i, j, k):
            r, c = pick(i, j, k)
            return (c // per_half, r, c % per_half)
        return pl.BlockSpec((None,) + shape2, index)

    assert not (ta and a_halves) and not (tb and b_halves)
    a_spec = spec((tk, tm), lambda i, j, k: (k, i), False, 0) if ta else \
        spec((tm, tk), lambda i, j, k: (i, k), a_halves, (k_dim // 2) // tk if a_halves else 0)
    b_spec = spec((tn, tk), lambda i, j, k: (j, k), False, 0) if tb else \
        spec((tk, tn), lambda i, j, k: (k, j), b_halves, (n // 2) // tn if b_halves else 0)
    o_spec = spec((tm, tn), lambda i, j, k: (i, j), o_halves, (n // 2) // tn if o_halves else 0)
    out_shape = (2, m, n // 2) if o_halves else (m, n)
    res = _pcall(
        body, name=name,
        out_shape=[jax.ShapeDtypeStruct(out_shape, out_dtype)] + ex_out,
        grid=grid,
        in_specs=[a_spec, b_spec] + ex_specs,
        out_specs=[o_spec] + ex_specs,
        scratch_shapes=[pltpu.VMEM((tm, tn), F32)] + ex_scratch,
        compiler_params=_cp(*(["arbitrary"] * 3 if exchange else ["parallel", "parallel", "arbitrary"])),
    )(a, b, *ex_in)
    return res if exchange else res[0]


def _ln_stats(x):
    mu = jnp.mean(x, axis=-1, keepdims=True)
    xc = x - mu
    var = jnp.mean(xc * xc, axis=-1, keepdims=True)
    rstd = lax.rsqrt(var + LN_EPS)
    return xc * rstd, rstd


def _ln_mod(x, sc, sh, name, tm=512):
    s, d = x.shape
    tm = min(tm, s)

    def body(x_ref, sc_ref, sh_ref, h_ref):
        xhat, _ = _ln_stats(x_ref[...])
        h_ref[...] = (xhat * (1.0 + sc_ref[...]) + sh_ref[...]).astype(h_ref.dtype)

    row = pl.BlockSpec((1, d), lambda i: (0, 0))
    return _pcall(
        body, name=name, out_shape=jax.ShapeDtypeStruct((s, d), BF16), grid=(s // tm,),
        in_specs=[pl.BlockSpec((tm, d), lambda i: (i, 0)), row, row],
        out_specs=pl.BlockSpec((tm, d), lambda i: (i, 0)),
        compiler_params=_cp("parallel"),
    )(x, sc, sh)


def _mm_res_ln(a, w, xres, gate, ln_g, ln_b, sc, sh, alpha, name, tm=512, tk=512):
    s, k_dim = a.shape
    d = w.shape[1]
    tm, tk = _tile(s, tm, 8), _tile(k_dim, tk)
    nk = k_dim // tk

    def body(a_ref, w_ref, x_ref, g_ref, lg_ref, lb_ref, sc_ref, sh_ref, y_ref, xn_ref, h_ref, acc_ref):
        k = pl.program_id(1)

        @pl.when(k == 0)
        def _():
            acc_ref[...] = jnp.zeros_like(acc_ref)

        acc_ref[...] += jnp.dot(a_ref[...], w_ref[...], preferred_element_type=F32)

        @pl.when(k == nk - 1)
        def _():
            y = acc_ref[...]
            y_ref[...] = y
            z = alpha * x_ref[...] + (1.0 + g_ref[...]) * y
            zhat, _ = _ln_stats(z)
            xn = zhat * lg_ref[...] + lb_ref[...]
            xn_ref[...] = xn
            xhat, _ = _ln_stats(xn)
            h_ref[...] = (xhat * (1.0 + sc_ref[...]) + sh_ref[...]).astype(h_ref.dtype)

    row = pl.BlockSpec((1, d), lambda i, k: (0, 0))
    tile = pl.BlockSpec((tm, d), lambda i, k: (i, 0))
    return _pcall(
        body, name=name,
        out_shape=(jax.ShapeDtypeStruct((s, d), F32), jax.ShapeDtypeStruct((s, d), F32),
                   jax.ShapeDtypeStruct((s, d), BF16)),
        grid=(s // tm, nk),
        in_specs=[pl.BlockSpec((tm, tk), lambda i, k: (i, k)), pl.BlockSpec((tk, d), lambda i, k: (k, 0)),
                  tile, row, row, row, row, row],
        out_specs=(tile, tile, tile),
        scratch_shapes=[pltpu.VMEM((tm, d), F32)],
        compiler_params=_cp("parallel", "arbitrary"),
    )(a, w, xres, gate, ln_g, ln_b, sc, sh)


def _colsum(v):
    return jnp.sum(v, axis=0, keepdims=True)


def _ln_res_bwd(dxn, xres, y, gate, ln_g, alpha, name, tm=256):
    s, d = dxn.shape
    tm = min(tm, s)

    def body(dxn_ref, x_ref, y_ref, g_ref, lg_ref, dres_ref, dy_ref, sums_ref):
        @pl.when(pl.program_id(0) == 0)
        def _():
            sums_ref[...] = jnp.zeros_like(sums_ref)

        dxn_v, y_v = dxn_ref[...], y_ref[...]
        one_g = 1.0 + g_ref[...]
        zhat, rstd = _ln_stats(alpha * x_ref[...] + one_g * y_v)
        dzh = dxn_v * lg_ref[...]
        dz = rstd * (dzh - jnp.mean(dzh, axis=-1, keepdims=True)
                     - zhat * jnp.mean(dzh * zhat, axis=-1, keepdims=True))
        dres_ref[...] = alpha * dz
        dy_ref[...] = (one_g * dz).astype(dy_ref.dtype)
        sums_ref[0:1, :] += _colsum(dxn_v * zhat)
        sums_ref[1:2, :] += _colsum(dxn_v)
        sums_ref[2:3, :] += _colsum(dz * y_v)

    row = pl.BlockSpec((1, d), lambda i: (0, 0))
    tile = pl.BlockSpec((tm, d), lambda i: (i, 0))
    return _pcall(
        body, name=name,
        out_shape=(jax.ShapeDtypeStruct((s, d), F32), jax.ShapeDtypeStruct((s, d), BF16),
                   jax.ShapeDtypeStruct((8, d), F32)),
        grid=(s // tm,),
        in_specs=[tile, tile, tile, row, row],
        out_specs=(tile, tile, pl.BlockSpec((8, d), lambda i: (0, 0))),
        compiler_params=_cp("arbitrary"),
    )(dxn, xres, y, gate, ln_g)


def _ln_mod_bwd(dh, x, sc, dres, name, extra=None, tm=256):
    s, d = dh.shape
    tm = min(tm, s)

    def body(*refs):
        if extra is None:
            dh_ref, x_ref, sc_ref, dres_ref, dx_ref, sums_ref = refs
            dh_v = dh_ref[...]
        else:
            dh_ref, x_ref, sc_ref, dres_ref, dh2_ref, df_ref, wf_ref, dx_ref, sums_ref = refs
            dh_v = dh_ref[...] + dh2_ref[...] + jnp.dot(df_ref[...], wf_ref[...], preferred_element_type=F32)

        @pl.when(pl.program_id(0) == 0)
        def _():
            sums_ref[...] = jnp.zeros_like(sums_ref)

        xhat, rstd = _ln_stats(x_ref[...])
        dxh = dh_v * (1.0 + sc_ref[...])
        dx_ref[...] = dres_ref[...] + rstd * (dxh - jnp.mean(dxh, axis=-1, keepdims=True)
                                             - xhat * jnp.mean(dxh * xhat, axis=-1, keepdims=True))
        sums_ref[0:1, :] += _colsum(dh_v)
        sums_ref[1:2, :] += _colsum(dh_v * xhat)

    row = pl.BlockSpec((1, d), lambda i: (0, 0))
    tile = pl.BlockSpec((tm, d), lambda i: (i, 0))
    in_specs = [tile, tile, row, tile]
    args = [dh, x, sc, dres]
    if extra is not None:
        in_specs += [tile, pl.BlockSpec((tm, AUG), lambda i: (i, 0)), pl.BlockSpec((AUG, d), lambda i: (0, 0))]
        args += list(extra)
    return _pcall(
        body, name=name,
        out_shape=(jax.ShapeDtypeStruct((s, d), F32), jax.ShapeDtypeStruct((8, d), F32)),
        grid=(s // tm,), in_specs=in_specs,
        out_specs=(tile, pl.BlockSpec((8, d), lambda i: (0, 0))),
        compiler_params=_cp("arbitrary"),
    )(*args)


def _loss_grad(xn, target, name, tm=512):
    s, d = xn.shape
    tm = min(tm, s)
    n = s // tm

    def body(x_ref, t_ref, dx_ref, loss_ref, acc_ref):
        i = pl.program_id(0)

        @pl.when(i == 0)
        def _():
            acc_ref[...] = jnp.zeros_like(acc_ref)

        err = x_ref[...] - t_ref[...]
        dx_ref[...] = err * (1.0 / d)
        acc_ref[...] += _colsum(err * err)

        @pl.when(i == n - 1)
        def _():
            loss_ref[...] = jnp.zeros_like(loss_ref) + (0.5 / d) * jnp.sum(acc_ref[...])

    tile = pl.BlockSpec((tm, d), lambda i: (i, 0))
    return _pcall(
        body, name=name,
        out_shape=(jax.ShapeDtypeStruct((s, d), F32), jax.ShapeDtypeStruct((8, 128), F32)),
        grid=(n,), in_specs=[tile, tile],
        out_specs=(tile, pl.BlockSpec((8, 128), lambda i: (0, 0))),
        scratch_shapes=[pltpu.VMEM((1, d), F32)],
        compiler_params=_cp("arbitrary"),
    )(xn, target)


def _sigmoid(v):
    return 0.5 * jnp.tanh(0.5 * v) + 0.5


def _merge_fwd(o_a, o_c, w_a, w_c, gates, name, tm=256):
    s, wd = o_a.shape
    d = w_a.shape[1]
    tm = min(tm, s)

    def body(oa_ref, oc_ref, wa_ref, wc_ref, ga_ref, gc_ref, m_ref):
        ba = jnp.dot(oa_ref[...], wa_ref[...], preferred_element_type=F32)
        bc = jnp.dot(oc_ref[...], wc_ref[...], preferred_element_type=F32)
        m_ref[...] = (_sigmoid(ga_ref[...]) * ba + _sigmoid(gc_ref[...]) * bc).astype(m_ref.dtype)

    o_spec = pl.BlockSpec((tm, wd), lambda i: (i, 0))
    w_spec = pl.BlockSpec((wd, d), lambda i: (0, 0))
    return _pcall(
        body, name=name, out_shape=jax.ShapeDtypeStruct((s, d), BF16), grid=(s // tm,),
        in_specs=[o_spec, o_spec, w_spec, w_spec,
                  pl.BlockSpec((tm, d), lambda i: (i, 0)), pl.BlockSpec((tm, d), lambda i: (i, 1))],
        out_specs=pl.BlockSpec((tm, d), lambda i: (i, 0)),
        compiler_params=_cp("parallel"),
    )(o_a, o_c, w_a, w_c, gates, gates)


def _merge_bwd(dmix, w_out_t, o_a, o_c, w_a, w_c, w_a_t, w_c_t, gates, name, tm=256):
    s, wd = o_a.shape
    d = w_a.shape[1]
    tm = min(tm, s)

    def body(dmix_ref, wot_ref, oa_ref, oc_ref, wa_ref, wc_ref, wat_ref, wct_ref, ga_ref, gc_ref,
             dba_ref, dbc_ref, doa_ref, doc_ref, dg_ref):
        dm = jnp.dot(dmix_ref[...], wot_ref[...], preferred_element_type=F32)
        for half, (o_ref, w_ref, wt_ref, g_ref, db_ref, do_ref) in enumerate((
                (oa_ref, wa_ref, wat_ref, ga_ref, dba_ref, doa_ref),
                (oc_ref, wc_ref, wct_ref, gc_ref, dbc_ref, doc_ref))):
            b = jnp.dot(o_ref[...], w_ref[...], preferred_element_type=F32)
            sg = _sigmoid(g_ref[...])
            db = (dm * sg).astype(BF16)
            db_ref[...] = db
            dg_ref[:, d * half:d * (half + 1)] = (dm * b * sg * (1.0 - sg)).astype(dg_ref.dtype)
            do_ref[...] = jnp.dot(db, wt_ref[...], preferred_element_type=F32).astype(do_ref.dtype)

    row_d = pl.BlockSpec((tm, d), lambda i: (i, 0))
    row_w = pl.BlockSpec((tm, wd), lambda i: (i, 0))
    full = lambda shp: pl.BlockSpec(shp, lambda i: (0, 0))
    return _pcall(
        body, name=name,
        out_shape=(jax.ShapeDtypeStruct((s, d), BF16), jax.ShapeDtypeStruct((s, d), BF16),
                   jax.ShapeDtypeStruct((s, wd), BF16), jax.ShapeDtypeStruct((s, wd), BF16),
                   jax.ShapeDtypeStruct((s, 2 * d), BF16)),
        grid=(s // tm,),
        in_specs=[row_d, full((d, d)), row_w, row_w, full((wd, d)), full((wd, d)), full((d, wd)), full((d, wd)),
                  row_d, pl.BlockSpec((tm, d), lambda i: (i, 1))],
        out_specs=(row_d, row_d, row_w, row_w, pl.BlockSpec((tm, 2 * d), lambda i: (i, 0))),
        compiler_params=_cp("parallel"),
    )(dmix, w_out_t, o_a, o_c, w_a, w_c, w_a_t, w_c_t, gates, gates)


def _shift_down(cur, k, fill_rows):
    out = pltpu.roll(cur, k, axis=0)
    rid = lax.broadcasted_iota(jnp.int32, cur.shape, 0)
    for r, fill in enumerate(fill_rows):
        out = jnp.where(rid == r, fill, out)
    return out


def _shift_up(cur, k, fill_rows):
    n = cur.shape[0]
    out = pltpu.roll(cur, n - k, axis=0)
    rid = lax.broadcasted_iota(jnp.int32, cur.shape, 0)
    for r, fill in enumerate(fill_rows):
        out = jnp.where(rid == n - k + r, fill, out)
    return out


def _conv_rows(cur, prev8, first, w, b):
    p6 = jnp.where(first, 0.0, prev8[6:7, :])
    p7 = jnp.where(first, 0.0, prev8[7:8, :])
    m1 = _shift_down(cur, 1, [p7])
    m2 = _shift_down(cur, 2, [p6, p7])
    u = b + w[0:1, :] * m2 + w[1:2, :] * m1 + w[2:3, :] * cur
    return u, m1, m2


def _conv_specs(tr, tc, order):
    r8 = tr // 8
    at = lambda f: (lambda *g: f(*order(*g)))
    return [pl.BlockSpec((2, tr, tc), at(lambda i, j: (0, i, j))),
            pl.BlockSpec((2, 8, tc), at(lambda i, j: (0, jnp.maximum(i * r8 - 1, 0), j))),
            pl.BlockSpec((2, 3, tc), at(lambda i, j: (0, 0, j))),
            pl.BlockSpec((2, 1, tc), at(lambda i, j: (0, 0, j)))]


def _conv_act_fwd(u0, conv_w, conv_b, name, tr=512, tc=256):
    _, s, f = u0.shape
    tr, tc = _tile(s, tr, 8), _tile(f, tc)

    def body(u_ref, p_ref, w_ref, b_ref, act_ref):
        first = pl.program_id(0) == 0
        a, _, _ = _conv_rows(u_ref[0], p_ref[0], first, w_ref[0], b_ref[0])
        v, _, _ = _conv_rows(u_ref[1], p_ref[1], first, w_ref[1], b_ref[1])
        act_ref[...] = (a * _sigmoid(a) * v).astype(act_ref.dtype)

    return _pcall(
        body, name=name, out_shape=jax.ShapeDtypeStruct((s, f), BF16), grid=(s // tr, f // tc),
        in_specs=_conv_specs(tr, tc, lambda i, j: (i, j)),
        out_specs=pl.BlockSpec((tr, tc), lambda i, j: (i, j)),
        compiler_params=_cp("parallel", "parallel"),
    )(u0, u0, conv_w, conv_b)


def _act_grads(dact, a, v):
    sg = _sigmoid(a)
    return dact * v * sg * (1.0 + a * (1.0 - sg)), dact * a * sg


def _conv_act_bwd(dact, u0, conv_w, conv_b, name, tr=512, tc=256):
    _, s, f = u0.shape
    tr, tc = _tile(s, tr, 8), _tile(f, tc)
    n = s // tr
    r8 = tr // 8

    def body(dact_ref, dnext_ref, u_ref, p_ref, unext_ref, w_ref, b_ref, du0_ref, sums_ref):
        i = pl.program_id(1)
        first, last = i == 0, i == n - 1

        @pl.when(first)
        def _():
            sums_ref[...] = jnp.zeros_like(sums_ref)

        cur = (u_ref[0], u_ref[1])
        (a, a1, a2), (v, v1, v2) = [_conv_rows(cur[hf], p_ref[hf], first, w_ref[hf], b_ref[hf]) for hf in range(2)]
        da, dv = _act_grads(dact_ref[...], a, v)
        (an, _, _), (vn, _, _) = [_conv_rows(unext_ref[hf], cur[hf][tr - 8:tr, :], False, w_ref[hf], b_ref[hf])
                                  for hf in range(2)]
        dan, dvn = _act_grads(dnext_ref[...], an, vn)
        for hf, g, gn, shifted in ((0, da, dan, (a2, a1, cur[0])), (1, dv, dvn, (v2, v1, cur[1]))):
            w = w_ref[hf]
            n0 = jnp.where(last, 0.0, gn[0:1, :])
            n1 = jnp.where(last, 0.0, gn[1:2, :])
            du0 = w[2:3, :] * g + w[1:2, :] * _shift_up(g, 1, [n0]) + w[0:1, :] * _shift_up(g, 2, [n0, n1])
            du0_ref[hf] = du0.astype(du0_ref.dtype)
            for r in range(3):
                sums_ref[hf, r:r + 1, :] += _colsum(g * shifted[r])
            sums_ref[hf, 3:4, :] += _colsum(g)

    nxt = lambda i: jnp.minimum((i + 1) * r8, s // 8 - 1)
    order = lambda j, i: (i, j)
    specs = _conv_specs(tr, tc, order)
    return _pcall(
        body, name=name,
        out_shape=(jax.ShapeDtypeStruct((2, s, f), BF16), jax.ShapeDtypeStruct((2, 8, f), F32)),
        grid=(f // tc, n),
        in_specs=[pl.BlockSpec((tr, tc), lambda j, i: (i, j)), pl.BlockSpec((8, tc), lambda j, i: (nxt(i), j)),
                  specs[0], specs[1], pl.BlockSpec((2, 8, tc), lambda j, i: (0, nxt(i), j)), specs[2], specs[3]],
        out_specs=(pl.BlockSpec((2, tr, tc), lambda j, i: (0, i, j)), pl.BlockSpec((2, 8, tc), lambda j, i: (0, 0, j))),
        compiler_params=_cp("parallel", "arbitrary"),
    )(dact, dact, u0, u0, u0, conv_w, conv_b)


def _cumsum_rows(v, reverse=False):
    r = v.shape[0]
    i128 = lax.broadcasted_iota(jnp.int32, (128, 128), 0), lax.broadcasted_iota(jnp.int32, (128, 128), 1)
    ir = lax.broadcasted_iota(jnp.int32, (r, r), 0), lax.broadcasted_iota(jnp.int32, (r, r), 1)
    in_row = (i128[0] >= i128[1] if reverse else i128[0] <= i128[1]).astype(F32)
    rows = (ir[1] > ir[0] if reverse else ir[1] < ir[0]).astype(F32)
    within = jnp.dot(v, in_row, preferred_element_type=F32, precision=HI)
    tot = jnp.broadcast_to(within[:, 0:1] if reverse else within[:, 127:128], (r, 128))
    return within + jnp.dot(rows, tot, preferred_element_type=F32, precision=HI)


def _forget_fwd(f_t, b_f, name):
    h, r, _ = f_t.shape

    def body(f_ref, b_ref, o_ref):
        z = f_ref[...] + b_ref[...]
        logf = jnp.minimum(z, 0.0) - jnp.log(1.0 + jnp.exp(-jnp.abs(z)))
        rest = _cumsum_rows(logf) * LOG2E
        for i in range(N_SPLIT):
            piece = rest.astype(BF16).astype(F32)
            o_ref[i] = piece
            rest = rest - piece

    return _pcall(
        body, name=name, out_shape=jax.ShapeDtypeStruct((N_SPLIT, h, r, 128), F32), grid=(h,),
        in_specs=[pl.BlockSpec((None, r, 128), lambda i: (i, 0, 0)), pl.BlockSpec((None, 1, 128), lambda i: (i, 0, 0))],
        out_specs=pl.BlockSpec((N_SPLIT, None, r, 128), lambda i: (0, i, 0, 0)),
        compiler_params=_cp("parallel"),
    )(f_t, b_f)


def _forget_bwd(d_cum, f_t, b_f, name):
    h, r, _ = f_t.shape

    def body(g_ref, f_ref, b_ref, df_ref, db_ref):
        df = _cumsum_rows(g_ref[...], reverse=True) * _sigmoid(-(f_ref[...] + b_ref[...]))
        df_ref[...] = df
        db_ref[...] = jnp.zeros_like(db_ref) + jnp.sum(df)

    blk = pl.BlockSpec((None, r, 128), lambda i: (i, 0, 0))
    one = pl.BlockSpec((None, 1, 128), lambda i: (i, 0, 0))
    return _pcall(
        body, name=name,
        out_shape=(jax.ShapeDtypeStruct((h, r, 128), F32), jax.ShapeDtypeStruct((h, 1, 128), F32)),
        grid=(h,), in_specs=[blk, blk, one], out_specs=(blk, one),
        compiler_params=_cp("parallel"),
    )(d_cum, f_t, b_f)


_NT = (((1,), (1,)), ((), ()))


def _causal_keep(tk, tq):
    return lax.broadcasted_iota(jnp.int32, (tk, tq), 0) <= lax.broadcasted_iota(jnp.int32, (tk, tq), 1)


_TN = (((0,), (0,)), ((), ()))
PAIR = 2
V_FOX = 2 * WIDTH // 128
Q_CHUNK, K_CHUNK, V_CHUNK = (3 * WIDTH) // 128, (4 * WIDTH) // 128, (5 * WIDTH) // 128


def _riding_exchange(exchange, n_in, n_out, grid):
    if exchange is None:
        return [], [], [], [], lambda refs: (refs, lambda: None)
    scatter, xs = exchange
    n = len(xs)
    hbm = pl.BlockSpec(memory_space=pltpu.HBM)

    def hook(refs):
        ins, x_refs = refs[:n_in], refs[n_in:n_in + n]
        outs, land_refs = refs[n_in + n:n_in + n + n_out], refs[n_in + n + n_out:n_in + 2 * n + n_out]
        rest = refs[n_in + 2 * n + n_out:]
        own_scratch, sems = rest[:len(rest) - 3], rest[len(rest) - 3:]
        start, wait = _direct_exchange(scatter, x_refs, land_refs, *sems)
        ids = [pl.program_id(a) for a in range(len(grid))]
        is_first = functools.reduce(jnp.logical_and, [i == 0 for i in ids])
        is_last = functools.reduce(jnp.logical_and, [i == g - 1 for i, g in zip(ids, grid)])
        pl.when(is_first)(start)
        return tuple(ins) + tuple(outs) + tuple(own_scratch), lambda: pl.when(is_last)(wait)

    return list(xs), _exchange_shapes(scatter, xs), [hbm] * n, _exchange_scratch(n), hook


def _fox_fwd(q_aug, k_aug, qkv, name, t=TQ, exchange=None):
    s = q_aug.shape[0]
    t = min(t, s)
    n = s // t
    dh = HEAD_DIM
    grid = (N_HEADS // PAIR, n)
    ex_in, ex_out, ex_specs, ex_scratch, hook = _riding_exchange(exchange, 3, 2, grid)

    def body(*refs):
        (q_ref, k_ref, v_ref, o_ref, lse_ref, m_s, l_s, acc_s), finish = hook(refs)
        qi = pl.program_id(1)
        m_s[...] = jnp.full_like(m_s, NEG)
        l_s[...] = jnp.zeros_like(l_s)
        acc_s[...] = jnp.zeros_like(acc_s)

        def step(kj, diag):
            rows = pl.ds(pl.multiple_of(kj * t, t), t)
            for hh in range(PAIR):
                st = lax.dot_general(k_ref[rows, AUG * hh:AUG * (hh + 1)], q_ref[:, AUG * hh:AUG * (hh + 1)], _NT,
                                     preferred_element_type=F32)
                if diag:
                    st = jnp.where(_causal_keep(t, t), st, NEG)
                m_prev = m_s[hh]
                m_new = jnp.maximum(m_prev, jnp.max(st, axis=0, keepdims=True))
                a = jnp.exp2(m_prev - m_new)
                pt = jnp.exp2(st - m_new)
                l_s[hh] = a * l_s[hh] + jnp.sum(pt, axis=0, keepdims=True)
                acc_s[hh] = a * acc_s[hh] + lax.dot_general(v_ref[rows, dh * hh:dh * (hh + 1)], pt.astype(BF16), _TN,
                                                            preferred_element_type=F32)
                m_s[hh] = m_new

        def off_diagonal(kj, carry):
            step(kj, False)
            return carry

        lax.fori_loop(0, qi, off_diagonal, 0)
        step(qi, True)
        for hh in range(PAIR):
            o_ref[hh] = acc_s[hh] / l_s[hh]
            lse_ref[hh, 0] = m_s[hh] + jnp.log2(l_s[hh])
        finish()

    return _pcall(
        body, name=name,
        out_shape=[jax.ShapeDtypeStruct((N_HEADS, dh, s), F32), jax.ShapeDtypeStruct((N_HEADS, n, 1, t), F32)] + ex_out,
        grid=grid,
        in_specs=[pl.BlockSpec((t, PAIR * AUG), lambda hp, i: (i, hp)),
                  pl.BlockSpec((s, PAIR * AUG), lambda hp, i: (0, hp)),
                  pl.BlockSpec((s, PAIR * dh), lambda hp, i: (0, V_FOX + hp))] + ex_specs,
        out_specs=[pl.BlockSpec((PAIR, dh, t), lambda hp, i: (hp, 0, i)),
                   pl.BlockSpec((PAIR, 1, 1, t), lambda hp, i: (hp, i, 0, 0))] + ex_specs,
        scratch_shapes=[pltpu.VMEM((PAIR, 1, t), F32), pltpu.VMEM((PAIR, 1, t), F32),
                        pltpu.VMEM((PAIR, dh, t), F32)] + ex_scratch,
        compiler_params=_cp("arbitrary", "arbitrary"),
    )(q_aug, k_aug, qkv, *ex_in)


def _row_dot(a_t, b_t, name, t=2048):
    h, dh, s = a_t.shape
    t = min(t, s)

    def body(a_ref, b_ref, o_ref):
        o_ref[...] = jnp.sum(a_ref[...].astype(F32) * b_ref[...], axis=0, keepdims=True)

    blk = pl.BlockSpec((None, dh, t), lambda hh, i: (hh, 0, i))
    return _pcall(
        body, name=name, out_shape=jax.ShapeDtypeStruct((h, 1, s), F32), grid=(h, s // t),
        in_specs=[blk, blk], out_specs=pl.BlockSpec((None, 1, t), lambda hh, i: (hh, 0, i)),
        compiler_params=_cp("parallel", "parallel"),
    )(a_t, b_t)


def _fox_bwd(q_aug, k_aug, qkv, do, lse, delta, name, t=TQ, exchange=None):
    s = q_aug.shape[0]
    dh = HEAD_DIM
    t = min(t, s)
    n = s // t
    grid = (N_HEADS // PAIR, n)
    ex_in, ex_out, ex_specs, ex_scratch, hook = _riding_exchange(exchange, 6, 5, grid)

    def body(*refs):
        (q_ref, k_ref, v_ref, do_ref, lse_ref, dl_ref, dk_ref, dv_ref, dq_ref, dsum_ref, dqsum_ref,
         dk_s, dv_s, dsum_s), finish = hook(refs)
        kj = pl.program_id(1)

        @pl.when(kj == 0)
        def _():
            dq_ref[...] = jnp.zeros_like(dq_ref)
            dqsum_ref[...] = jnp.zeros_like(dqsum_ref)

        dk_s[...] = jnp.zeros_like(dk_s)
        dv_s[...] = jnp.zeros_like(dv_s)
        dsum_s[...] = jnp.zeros_like(dsum_s)

        def step(qi, diag):
            rows = pl.ds(pl.multiple_of(qi * t, t), t)
            for hh in range(PAIR):
                q = q_ref[rows, AUG * hh:AUG * (hh + 1)]
                k = k_ref[:, AUG * hh:AUG * (hh + 1)]
                st = lax.dot_general(k, q, _NT, preferred_element_type=F32)
                if diag:
                    st = jnp.where(_causal_keep(t, t), st, NEG)
                pt = jnp.exp2(st - lse_ref[hh, qi])
                do_v = do_ref[rows, dh * hh:dh * (hh + 1)]
                dpt = lax.dot_general(v_ref[:, dh * hh:dh * (hh + 1)], do_v, _NT, preferred_element_type=F32)
                ds32 = pt * (dpt - dl_ref[hh, qi])
                dsum_s[hh] += sum(ds32[:, 128 * u:128 * (u + 1)] for u in range(t // 128))
                dqsum_ref[hh, qi] += jnp.sum(ds32, axis=0, keepdims=True)
                dst = ds32.astype(BF16)
                dv_s[hh] += jnp.dot(pt.astype(BF16), do_v, preferred_element_type=F32)
                dk_s[hh] += jnp.dot(dst, q[:, :dh], preferred_element_type=F32)
                dq_ref[hh, qi] += lax.dot_general(k[:, :dh], dst, _TN, preferred_element_type=F32)

        def off_diagonal(qi, carry):
            step(qi, False)
            return carry

        step(kj, True)
        lax.fori_loop(kj + 1, n, off_diagonal, 0)
        for hh in range(PAIR):
            dk_ref[:, dh * hh:dh * (hh + 1)] = (LN2 * dk_s[hh]).astype(dk_ref.dtype)
            dv_ref[:, dh * hh:dh * (hh + 1)] = dv_s[hh].astype(dv_ref.dtype)
            dsum_ref[:, 128 * hh:128 * (hh + 1)] = dsum_s[hh]
        finish()

    stat = pl.BlockSpec((PAIR, n, 1, t), lambda hp, j: (hp, 0, 0, 0))
    return _pcall(
        body, name=name,
        out_shape=[jax.ShapeDtypeStruct((s, WIDTH), BF16), jax.ShapeDtypeStruct((s, WIDTH), BF16),
                   jax.ShapeDtypeStruct((N_HEADS, n, dh, t), F32), jax.ShapeDtypeStruct((s, N_HEADS * 128), F32),
                   jax.ShapeDtypeStruct((N_HEADS, n, 1, t), F32)] + ex_out,
        grid=grid,
        in_specs=[pl.BlockSpec((s, PAIR * AUG), lambda hp, j: (0, hp)),
                  pl.BlockSpec((t, PAIR * AUG), lambda hp, j: (j, hp)),
                  pl.BlockSpec((t, PAIR * dh), lambda hp, j: (j, V_FOX + hp)),
                  pl.BlockSpec((s, PAIR * dh), lambda hp, j: (0, hp)),
                  stat, stat] + ex_specs,
        out_specs=[pl.BlockSpec((t, PAIR * dh), lambda hp, j: (j, hp)),
                   pl.BlockSpec((t, PAIR * dh), lambda hp, j: (j, hp)),
                   pl.BlockSpec((PAIR, n, dh, t), lambda hp, j: (hp, 0, 0, 0)),
                   pl.BlockSpec((t, PAIR * 128), lambda hp, j: (j, hp)),
                   stat] + ex_specs,
        scratch_shapes=[pltpu.VMEM((PAIR, t, dh), F32), pltpu.VMEM((PAIR, t, dh), F32),
                        pltpu.VMEM((PAIR, t, 128), F32)] + ex_scratch,
        compiler_params=_cp("arbitrary", "arbitrary"),
    )(q_aug, k_aug, qkv, do, lse, delta, *ex_in)


N_SUB = TQ // SUB


def _pair_blockdiag(x2):
    lane = lax.broadcasted_iota(jnp.int32, x2.shape, 1)
    zero = jnp.zeros_like(x2)
    return jnp.concatenate([jnp.where(lane < HEAD_DIM, x2, zero), jnp.where(lane >= HEAD_DIM, x2, zero)], axis=0)


def _pair_lanes(ref, j):
    return jnp.concatenate([ref[hh, 0, :, SUB * j:SUB * (j + 1)] for hh in range(PAIR)], axis=1)


def _chunk_logits(kwin, qbd, b_ref, first_tile, j):
    which = jnp.where(first_tile, 1 + j, 0)
    bias = jnp.concatenate([b_ref[hh, which] for hh in range(PAIR)], axis=1)
    return lax.dot_general(kwin, qbd, _NT, preferred_element_type=F32) + bias


def _chunk_fwd(qkv, bias_t, name):
    s = qkv.shape[0]
    dh = HEAD_DIM
    n = s // TQ
    nsub = TQ // SUB
    prev = lambda i: jnp.maximum(i - 1, 0)

    def body(q_ref, kp_ref, kc_ref, vp_ref, vc_ref, b_ref, o_ref, lse_ref):
        first_tile = pl.program_id(1) == 0
        kw = jnp.concatenate([kp_ref[...], kc_ref[...]], axis=0)
        vw = jnp.concatenate([vp_ref[...], vc_ref[...]], axis=0)
        for j in range(nsub):
            rows, win = slice(SUB * j, SUB * (j + 1)), slice(SUB * j, SUB * j + SUBK)
            st = _chunk_logits(kw[win], _pair_blockdiag(q_ref[rows, :] * QK_SCALE2), b_ref, first_tile, j)
            m = jnp.max(st, axis=0, keepdims=True)
            pt = jnp.exp2(st - m)
            l = jnp.sum(pt, axis=0, keepdims=True)
            ot = lax.dot_general(vw[win], pt.astype(BF16), _TN, preferred_element_type=F32)
            stat = m + jnp.log2(l)
            for hh in range(PAIR):
                lanes = slice(SUB * hh, SUB * (hh + 1))
                o_ref[hh, :, rows] = ot[dh * hh:dh * (hh + 1), lanes] / l[:, lanes]
                lse_ref[hh, 0, :, rows] = stat[:, lanes]

    blk = lambda col, m: pl.BlockSpec((TQ, PAIR * dh), lambda hp, i: (m(i), col + hp))
    same = lambda i: i
    return _pcall(
        body, name=name,
        out_shape=(jax.ShapeDtypeStruct((N_HEADS, dh, s), F32), jax.ShapeDtypeStruct((N_HEADS, n, 1, TQ), F32)),
        grid=(N_HEADS // PAIR, n),
        in_specs=[blk(Q_CHUNK, same), blk(K_CHUNK, prev), blk(K_CHUNK, same), blk(V_CHUNK, prev), blk(V_CHUNK, same),
                  pl.BlockSpec((PAIR, 1 + N_SUB, SUBK, SUB), lambda hp, i: (hp, 0, 0, 0))],
        out_specs=(pl.BlockSpec((PAIR, dh, TQ), lambda hp, i: (hp, 0, i)),
                   pl.BlockSpec((PAIR, 1, 1, TQ), lambda hp, i: (hp, i, 0, 0))),
        compiler_params=_cp("parallel", "parallel"),
    )(qkv, qkv, qkv, qkv, qkv, bias_t)


def _chunk_bwd(qkv, do, lse, delta, bias_t, name):
    s = qkv.shape[0]
    dh = HEAD_DIM
    n = s // TQ
    nsub = TQ // SUB
    cur = lambda i: jnp.minimum(i, n - 1)
    prev = lambda i: jnp.maximum(cur(i) - 1, 0)
    done = lambda i: jnp.maximum(i - 1, 0)

    def body(q_ref, kp_ref, kc_ref, vp_ref, vc_ref, do_ref, lse_ref, dl_ref, b_ref,
             dq_ref, dk_ref, dv_ref, db_ref, dkw_s, dvw_s, ck_s, cv_s):
        i = pl.program_id(1)

        @pl.when(i == 0)
        def _():
            db_ref[...] = jnp.zeros_like(db_ref)
            ck_s[...] = jnp.zeros_like(ck_s)
            cv_s[...] = jnp.zeros_like(cv_s)

        dkw_s[...] = jnp.zeros_like(dkw_s)
        dvw_s[...] = jnp.zeros_like(dvw_s)

        @pl.when(i < n)
        def _():
            first_tile = i == 0
            kw = jnp.concatenate([kp_ref[...], kc_ref[...]], axis=0)
            vw = jnp.concatenate([vp_ref[...], vc_ref[...]], axis=0)
            for j in range(nsub):
                rows, win = slice(SUB * j, SUB * (j + 1)), slice(SUB * j, SUB * j + SUBK)
                qbd = _pair_blockdiag(q_ref[rows, :] * QK_SCALE2)
                dobd = _pair_blockdiag(do_ref[rows, :])
                st = _chunk_logits(kw[win], qbd, b_ref, first_tile, j)
                pt = jnp.exp2(st - _pair_lanes(lse_ref, j))
                dpt = lax.dot_general(vw[win], dobd, _NT, preferred_element_type=F32)
                dst = pt * (dpt - _pair_lanes(dl_ref, j))
                dsb = dst.astype(BF16)
                dvw_s[win, :] += jnp.dot(pt.astype(BF16), dobd, preferred_element_type=F32)
                dkw_s[win, :] += jnp.dot(dsb, qbd, preferred_element_type=F32)
                dqt = QK_SCALE * lax.dot_general(kw[win], dsb, _TN, preferred_element_type=F32)
                for hh in range(PAIR):
                    lanes = slice(SUB * hh, SUB * (hh + 1))
                    db_ref[hh] += dst[:, lanes]
                    dq_ref[hh, :, rows] = dqt[dh * hh:dh * (hh + 1), lanes]

        dk_ref[...] = (LN2 * (ck_s[...] + dkw_s[0:TQ, :])).astype(dk_ref.dtype)
        dv_ref[...] = (cv_s[...] + dvw_s[0:TQ, :]).astype(dv_ref.dtype)
        ck_s[...] = dkw_s[TQ:2 * TQ, :]
        cv_s[...] = dvw_s[TQ:2 * TQ, :]

    blk = lambda col, m: pl.BlockSpec((TQ, PAIR * dh), lambda hp, i: (m(i), col + hp))
    stat = pl.BlockSpec((PAIR, 1, 1, TQ), lambda hp, i: (hp, cur(i), 0, 0))
    bs = pl.BlockSpec((PAIR, SUBK, SUB), lambda hp, i: (hp, 0, 0))
    return _pcall(
        body, name=name,
        out_shape=(jax.ShapeDtypeStruct((N_HEADS, dh, s), F32), jax.ShapeDtypeStruct((s, WIDTH), BF16),
                   jax.ShapeDtypeStruct((s, WIDTH), BF16), jax.ShapeDtypeStruct((N_HEADS, SUBK, SUB), F32)),
        grid=(N_HEADS // PAIR, n + 1),
        in_specs=[blk(Q_CHUNK, cur), blk(K_CHUNK, prev), blk(K_CHUNK, cur), blk(V_CHUNK, prev), blk(V_CHUNK, cur),
                  blk(0, cur), stat, stat, pl.BlockSpec((PAIR, 1 + N_SUB, SUBK, SUB), lambda hp, i: (hp, 0, 0, 0))],
        out_specs=(pl.BlockSpec((PAIR, dh, TQ), lambda hp, i: (hp, 0, cur(i))), blk(0, done), blk(0, done), bs),
        scratch_shapes=[pltpu.VMEM((2 * TQ, PAIR * dh), F32), pltpu.VMEM((2 * TQ, PAIR * dh), F32),
                        pltpu.VMEM((TQ, PAIR * dh), F32), pltpu.VMEM((TQ, PAIR * dh), F32)],
        compiler_params=_cp("parallel", "arbitrary"),
    )(qkv, qkv, qkv, qkv, qkv, do, lse, delta, bias_t)


def _mod_part(c_all, w_ada, b_ada, name):
    nl, d, n = w_ada.shape
    b = c_all.shape[0]

    def body(c_ref, w_ref, b_ref, o_ref, cond_ref):
        cv = c_ref[...]
        cond = cv * _sigmoid(cv)
        cond_ref[...] = cond
        o_ref[...] = jnp.dot(cond, w_ref[...], preferred_element_type=F32, precision=HI) + b_ref[...]

    return _pcall(
        body, name=name,
        out_shape=(jax.ShapeDtypeStruct((nl, b, n), F32), jax.ShapeDtypeStruct((b, d), F32)),
        grid=(nl,),
        in_specs=[pl.BlockSpec((b, d), lambda l: (0, 0)), pl.BlockSpec((None, d, n), lambda l: (l, 0, 0)),
                  pl.BlockSpec((None, 1, n), lambda l: (l, 0, 0))],
        out_specs=(pl.BlockSpec((None, b, n), lambda l: (l, 0, 0)), pl.BlockSpec((b, d), lambda l: (0, 0))),
        compiler_params=_cp("arbitrary"),
    )(c_all, w_ada, b_ada)


def _adamw(parts, w, m, v, name, tr=1024):
    p, rl, cp = parts[0].shape
    r, c = w.shape
    tr = _tile(rl, tr, 16)
    per = rl // tr
    c1 = 1.0 / (1.0 - ADAM_B1 ** ADAM_STEP)
    c2 = 1.0 / (1.0 - ADAM_B2 ** ADAM_STEP)

    def body(*refs):
        p_refs, (w_ref, m_ref, v_ref, g_ref, d_ref, nm_ref, nv_ref) = refs[:len(parts)], refs[len(parts):]
        for which, p_ref in enumerate(p_refs):
            pl.when(pl.program_id(0) // per == which)(
                functools.partial(update, p_ref, w_ref, m_ref, v_ref, g_ref, d_ref, nm_ref, nv_ref))

    def update(p_ref, w_ref, m_ref, v_ref, g_ref, d_ref, nm_ref, nv_ref):
        g = p_ref[0].astype(F32)
        for i in range(1, p):
            g = g + p_ref[i].astype(F32)
        g = g[:, :c]
        nm = ADAM_B1 * m_ref[...] + (1.0 - ADAM_B1) * g
        nv = ADAM_B2 * v_ref[...] + (1.0 - ADAM_B2) * (g * g)
        g_ref[...] = g
        nm_ref[...] = nm
        nv_ref[...] = nv
        d_ref[...] = -ADAM_LR * ((nm * c1) / (jnp.sqrt(nv * c2) + ADAM_EPS) + ADAM_WD * w_ref[...])

    blk = pl.BlockSpec((tr, c), lambda i: (i, 0))
    out = jax.ShapeDtypeStruct((r, c), F32)
    return _pcall(
        body, name=name, out_shape=(out, out, out, out), grid=(r // tr,),
        in_specs=[pl.BlockSpec((p, tr, cp), lambda i, _w=which: (0, jnp.clip(i - _w * per, 0, per - 1), 0))
                  for which in range(len(parts))] + [blk, blk, blk],
        out_specs=(blk, blk, blk, blk),
        compiler_params=_cp("parallel"),
    )(*parts, w, m, v)


def _pad_lanes(a):
    pad = (-a.shape[-1]) % 128
    return jnp.pad(a, [(0, 0)] * (a.ndim - 1) + [(0, pad)]) if pad else a


def _rows128(a, mult=16):
    flat = a.reshape(-1)
    n = flat.shape[0]
    per = 128 * mult
    pad = (-n) % per
    if pad:
        flat = jnp.concatenate([flat, jnp.zeros((pad,), a.dtype)])
    return flat.reshape(-1, 128)


def _pack(arrs, mult=16):
    pieces = [_rows128(a, mult) for a in arrs]
    return jnp.concatenate(pieces, axis=0), [p.shape[0] for p in pieces]


def _unpack(packed, rows, shapes, lead=()):
    out, at = [], 0
    for r, shp in zip(rows, shapes):
        n = int(np.prod(shp))
        piece = packed[..., at:at + r, :].reshape(lead + (r * 128,))[..., :n]
        out.append(piece.reshape(lead + tuple(shp)))
        at += r
    return out


def _rel_bias_tile(table):
    h = table.shape[0]
    lo = REL_CLIP - (CHUNK - 1)
    n_far = LEFT_CHUNKS * CHUNK + CHUNK - 1 - REL_CLIP
    vec = jnp.concatenate([table[:, lo:2 * REL_CLIP], jnp.repeat(table[:, 2 * REL_CLIP:], n_far + 1, axis=1)], axis=1)
    rev = vec[:, ::-1]
    n_vec = BAND + CHUNK - 1
    skew = jnp.tile(rev, (1, CHUNK + 1))[:, :CHUNK * (n_vec + 1)].reshape(h, CHUNK, n_vec + 1)
    bias = skew[:, ::-1, :BAND]
    neg = jnp.full((h, CHUNK, CHUNK), NEG, F32)
    two = jnp.concatenate([jnp.concatenate([bias, neg], axis=2), jnp.concatenate([neg, bias], axis=2)], axis=1)
    plain = two.transpose(0, 2, 1) * LOG2E
    key = np.arange(SUBK)[:, None]
    return jnp.stack([plain] + [jnp.where(key >= TQ - SUB * j, plain, NEG) for j in range(N_SUB)], axis=1)


def _rel_bias_tile_grad(dbias_t):
    h = dbias_t.shape[0]
    two = dbias_t.transpose(0, 2, 1)
    dbias = two[:, :CHUNK, :BAND] + two[:, CHUNK:, CHUNK:]
    n_vec = BAND + CHUNK - 1
    dskew = jnp.pad(dbias[:, ::-1, :], ((0, 0), (0, 0), (0, n_vec + 1 - BAND))).reshape(h, CHUNK * (n_vec + 1))
    dskew = jnp.pad(dskew, ((0, 0), (0, (CHUNK + 1) * n_vec - CHUNK * (n_vec + 1))))
    drev = jnp.sum(dskew.reshape(h, CHUNK + 1, n_vec), axis=1)
    dvec = drev[:, ::-1]
    lo = REL_CLIP - (CHUNK - 1)
    n_near = 2 * REL_CLIP - lo
    return jnp.concatenate([jnp.zeros((h, lo), F32), dvec[:, :n_near],
                            jnp.sum(dvec[:, n_near:], axis=1, keepdims=True)], axis=1)


BIG = ("w_in", "w_br_fox", "w_br_chunk", "w_out", "w_up", "w_down")
SMALL = ("b_f", "rel_bias", "conv_b", "b_ada", "ln1_g", "ln1_b", "ln2_g", "ln2_b")
ORDER = ("w_in", "b_f", "rel_bias", "w_br_fox", "w_br_chunk", "w_out", "w_up", "conv_w", "conv_b", "w_down",
         "w_ada", "b_ada", "ln1_g", "ln1_b", "ln2_g", "ln2_b")


def _dest_major(g, axis):
    shp = g.shape
    g = g.reshape(shp[:axis] + (N_DEV, shp[axis] // N_DEV) + shp[axis + 1:])
    return jnp.moveaxis(g, axis, 0)


EARLY = ("w_down", "w_up", "w_out", "w_br_fox", "w_br_chunk")
LATE = ("w_in",)
REST = ("w_br_fox", "w_br_chunk", "w_out", "w_up", "w_down")


def _shard_axis(n):
    return 0 if n in ("w_out", "w_down") else 1


def _slab(g, n):
    return _pad_lanes(_dest_major(g, _shard_axis(n)).astype(BF16))


def _layer_fwd(x, h1, wl, rest_of, mod, next_mod, alpha, gather):
    sh1, sc1, g1, sh2, sc2, g2 = mod
    s = x.shape[0]
    qkv = _mm(h1, wl["w_qkv"], out_dtype=BF16, name="proj_qkv", tk=1024)
    gates = _mm(h1, wl["w_gates"], out_dtype=F32, name="proj_gates", tk=1024)
    f_t = _mm(wl["w_f_t"], h1, out_dtype=F32, name="proj_f", tb=True, tk=1024)[:N_HEADS]
    f_t = f_t.reshape(N_HEADS, s // 128, 128)
    b_f = jnp.broadcast_to(wl["b_f"].reshape(N_HEADS, 1, 1), (N_HEADS, 1, 128))
    cum = _forget_fwd(f_t, b_f, "forget_fwd").reshape(N_SPLIT, N_HEADS, s)
    cum_cols = cum.transpose(2, 1, 0).astype(BF16)

    ones = jnp.ones((s, N_HEADS, N_SPLIT), BF16)
    zeros = jnp.zeros((s, N_HEADS, AUG - HEAD_DIM - 2 * N_SPLIT), BF16)
    q_a = qkv[:, :WIDTH].reshape(s, N_HEADS, HEAD_DIM) * QK_SCALE2
    k_a = qkv[:, WIDTH:2 * WIDTH].reshape(s, N_HEADS, HEAD_DIM)
    q_aug = jnp.concatenate([q_a, cum_cols, ones, zeros], axis=-1).reshape(s, N_HEADS * AUG)
    k_aug = jnp.concatenate([k_a, ones, -cum_cols, zeros], axis=-1).reshape(s, N_HEADS * AUG)
    o_a_t, lse_a, *gathered = _fox_fwd(q_aug, k_aug, qkv, "fox_fwd_gather_%d" % len(gather), exchange=(False, gather))
    wl.update(rest_of(gathered[:len(REST)]))
    gathered = gathered[len(REST):]
    bias_t = _rel_bias_tile(wl["rel_bias"])
    o_c_t, lse_c = _chunk_fwd(qkv, bias_t, "chunk_fwd")
    o_a = o_a_t.transpose(2, 0, 1).reshape(s, WIDTH).astype(BF16)
    o_c = o_c_t.transpose(2, 0, 1).reshape(s, WIDTH).astype(BF16)

    merged = _merge_fwd(o_a, o_c, wl["w_br_fox"], wl["w_br_chunk"], gates, "merge_fwd")
    mix, x1, h2 = _mm_res_ln(merged, wl["w_out"], x, g1, wl["ln1_g"], wl["ln1_b"], sc2, sh2, alpha, "out_ln1")
    u0 = _mm(h2, wl["w_up"], out_dtype=F32, name="ffn_up", tk=1024, tn=1408, o_halves=True)
    act = _conv_act_fwd(u0, wl["conv_w"], wl["conv_b"], "conv_act_fwd")
    y2, x2, h_next = _mm_res_ln(act, wl["w_down"], x1, g2, wl["ln2_g"], wl["ln2_b"], next_mod[0], next_mod[1],
                                alpha, "down_ln2", tk=1408)
    saved = dict(x=x, h1=h1, gates=gates, f_t=f_t, b_f=b_f, q_aug=q_aug, k_aug=k_aug, qkv=qkv,
                 bias_t=bias_t, o_a_t=o_a_t, o_c_t=o_c_t, lse_a=lse_a, lse_c=lse_c,
                 o_a=o_a, o_c=o_c, merged=merged, mix=mix, x1=x1, h2=h2, u0=u0, act=act, y2=y2)
    return x2, h_next, saved, gathered


def _layer_bwd(dx2, sv, wl, mod, alpha, send, ride_late):
    sh1, sc1, g1, sh2, sc2, g2 = mod
    s = dx2.shape[0]
    g = {}
    dres2, dy2, sums = _ln_res_bwd(dx2, sv["x1"], sv["y2"], g2, wl["ln2_g"], alpha, "ln2_bwd")
    g["ln2_g"], g["ln2_b"], dg2 = sums[0], sums[1], sums[2]
    dact = _mm(dy2, wl["w_down_t"], out_dtype=F32, name="dact", tk=1024, tn=1408)
    g["w_down"] = _mm(sv["act"], dy2, out_dtype=F32, name="g_w_down", ta=True, tm=1408, tk=2048)
    du0, csum = _conv_act_bwd(dact, sv["u0"], wl["conv_w"], wl["conv_b"], "conv_act_bwd")
    g["conv_w"] = jnp.concatenate([csum[0, 0:3], csum[1, 0:3]], axis=1)
    g["conv_b"] = jnp.concatenate([csum[0, 3], csum[1, 3]])
    dh2 = _mm(du0, wl["w_up_t"], out_dtype=F32, name="dh2", tk=2816, a_halves=True)
    g["w_up"] = _mm(sv["h2"], du0, out_dtype=F32, name="g_w_up", ta=True, tk=2048, tn=1408, b_halves=True)
    dx1, sums = _ln_mod_bwd(dh2, sv["x1"], sc2, dres2, "ln_mod2_bwd")
    dsh2, dsc2 = sums[0], sums[1]

    dres1, dmix, sums = _ln_res_bwd(dx1, sv["x"], sv["mix"], g1, wl["ln1_g"], alpha, "ln1_bwd")
    g["ln1_g"], g["ln1_b"], dg1 = sums[0], sums[1], sums[2]
    g["w_out"] = _mm(sv["merged"], dmix, out_dtype=F32, name="g_w_out", ta=True, tk=2048)
    dba, dbc, do_a, do_c, dgates = _merge_bwd(
        dmix, wl["w_out_t"], sv["o_a"], sv["o_c"], wl["w_br_fox"], wl["w_br_chunk"],
        wl["w_br_fox_t"], wl["w_br_chunk_t"], sv["gates"], "merge_bwd")
    g["w_br_fox"] = _mm(sv["o_a"], dba, out_dtype=F32, name="g_w_br_fox", ta=True, tk=2048)
    g["w_br_chunk"] = _mm(sv["o_c"], dbc, out_dtype=F32, name="g_w_br_chunk", ta=True, tk=2048)
    n_t = s // min(TQ, s)
    stats = lambda a: a.reshape(N_HEADS, n_t, 1, s // n_t)

    delta_a = _row_dot(do_a.T.reshape(N_HEADS, HEAD_DIM, s), sv["o_a_t"], "delta_fox")
    send = list(send) + [_slab(g[n], n) for n in EARLY]
    dk_a, dv_a, dq_a_t, dsum, dqsum, *landed = _fox_bwd(sv["q_aug"], sv["k_aug"], sv["qkv"], do_a, sv["lse_a"],
                                                        stats(delta_a), "fox_bwd_scatter_%d" % len(send),
                                                        exchange=(True, send))
    dq_a = (dq_a_t * QK_SCALE).transpose(1, 3, 0, 2).reshape(s, WIDTH).astype(BF16)
    d_cum = dqsum.reshape(N_HEADS, s) - jnp.sum(dsum.reshape(s, N_HEADS, 128), axis=-1).T
    df_t, db_f = _forget_bwd(d_cum.reshape(N_HEADS, s // 128, 128), sv["f_t"], sv["b_f"], "forget_bwd")
    g["b_f"] = db_f[:, 0, 0]
    df_t = df_t.reshape(N_HEADS, s)

    delta_c = _row_dot(do_c.T.reshape(N_HEADS, HEAD_DIM, s), sv["o_c_t"], "delta_chunk")
    dq_c_t, dk_c, dv_c, dbias_t = _chunk_bwd(sv["qkv"], do_c, sv["lse_c"], stats(delta_c), sv["bias_t"], "chunk_bwd")
    g["rel_bias"] = _rel_bias_tile_grad(dbias_t)
    dq_c = dq_c_t.transpose(2, 0, 1).reshape(s, WIDTH).astype(BF16)

    dqkv = jnp.concatenate([dq_a, dk_a, dv_a, dq_c, dk_c, dv_c], axis=1)
    df_pad = jnp.zeros((16 - N_HEADS, s), F32)
    df16 = jnp.concatenate([df_t, df_pad], axis=0).astype(BF16)
    df_cols = jnp.concatenate([df16.T, jnp.zeros((s, AUG - 16), BF16)], axis=1)

    g_qkv = _mm(sv["h1"], dqkv, out_dtype=F32, name="g_w_qkv", ta=True, tk=2048)
    g_gates = _mm(sv["h1"], dgates, out_dtype=F32, name="g_w_gates", ta=True, tk=2048)
    g_f_t = _mm(df16, sv["h1"], out_dtype=F32, name="g_w_f", tk=1024)[:N_HEADS]
    g["w_in"] = jnp.concatenate([g_qkv[:, :3 * WIDTH], g_f_t.T, g_qkv[:, 3 * WIDTH:], g_gates], axis=1)
    late = [_slab(g[n], n) for n in LATE] if ride_late else None
    dh1, *landed_late = _mm(dqkv, wl["w_qkv_t"], out_dtype=F32, name="dh1_qkv_scatter", tk=1536,
                            exchange=(True, late)) if ride_late else \
        [_mm(dqkv, wl["w_qkv_t"], out_dtype=F32, name="dh1_qkv", tk=1536)]
    dh1g = _mm(dgates, wl["w_gates_t"], out_dtype=F32, name="dh1_gates", tk=2048)
    dx, sums = _ln_mod_bwd(dh1, sv["x"], sc1, dres1, "ln_mod1_bwd", extra=(dh1g, df_cols, wl["w_f_pad"]))
    dsh1, dsc1 = sums[0], sums[1]
    g["mod"] = jnp.concatenate([dsh1, dsc1, dg1, dsh2, dsc2, dg2])
    return dx, g, landed, landed_late


def kernel(x, c, w_in, b_f, rel_bias, w_br_fox, w_br_chunk, w_out, w_up, conv_w, conv_b, w_down, w_ada, b_ada, ln1_g, ln1_b, ln2_g, ln2_b, loss_target, m_w_in, m_b_f, m_rel_bias, m_w_br_fox, m_w_br_chunk, m_w_out, m_w_up, m_conv_w, m_conv_b, m_w_down, m_w_ada, m_b_ada, m_ln1_g, m_ln1_b, m_ln2_g, m_ln2_b, v_w_in, v_b_f, v_rel_bias, v_w_br_fox, v_w_br_chunk, v_w_out, v_w_up, v_conv_w, v_conv_b, v_w_down, v_w_ada, v_b_ada, v_ln1_g, v_ln1_b, v_ln2_g, v_ln2_b):
    w = dict(w_in=w_in, b_f=b_f, rel_bias=rel_bias, w_br_fox=w_br_fox, w_br_chunk=w_br_chunk, w_out=w_out,
             w_up=w_up, conv_w=conv_w, conv_b=conv_b, w_down=w_down, w_ada=w_ada, b_ada=b_ada,
             ln1_g=ln1_g, ln1_b=ln1_b, ln2_g=ln2_g, ln2_b=ln2_b)
    m = dict(w_in=m_w_in, b_f=m_b_f, rel_bias=m_rel_bias, w_br_fox=m_w_br_fox, w_br_chunk=m_w_br_chunk,
             w_out=m_w_out, w_up=m_w_up, conv_w=m_conv_w, conv_b=m_conv_b, w_down=m_w_down, w_ada=m_w_ada,
             b_ada=m_b_ada, ln1_g=m_ln1_g, ln1_b=m_ln1_b, ln2_g=m_ln2_g, ln2_b=m_ln2_b)
    v = dict(w_in=v_w_in, b_f=v_b_f, rel_bias=v_rel_bias, w_br_fox=v_w_br_fox, w_br_chunk=v_w_br_chunk,
             w_out=v_w_out, w_up=v_w_up, conv_w=v_conv_w, conv_b=v_conv_b, w_down=v_w_down, w_ada=v_w_ada,
             b_ada=v_b_ada, ln1_g=v_ln1_g, ln1_b=v_ln1_b, ln2_g=v_ln2_g, ln2_b=v_ln2_b)
    depth, d, _ = w_in.shape
    s = x.shape[1]
    alpha = (2.0 * depth) ** 0.25
    me = 4 * lax.axis_index("x") + 2 * lax.axis_index("y") + lax.axis_index("c")
    x0 = x.reshape(s, d)
    target = loss_target.reshape(s, d)

    small_in, small_rows = _pack([c, conv_w], mult=8)
    small_all = _all_gather([small_in], "gather_c_conv")[0]
    c_all, conv_w_all = _unpack(small_all, small_rows, [c.shape, conv_w.shape], lead=(N_DEV,))
    c_all = c_all.reshape(N_DEV, d)
    conv_w_full = conv_w_all.transpose(1, 2, 0, 3).reshape(depth, conv_w.shape[1], -1)
    n_ada = w_ada.shape[2]
    b_ada_mine = lax.dynamic_slice_in_dim(b_ada, me * n_ada, n_ada, axis=1).reshape(depth, 1, n_ada)
    mod_part, cond_all = _mod_part(c_all, w_ada, b_ada_mine, "mod_part")
    mod_all = _all_gather([mod_part.reshape(depth * N_DEV, n_ada)], "gather_mod")[0]
    mod_all = mod_all.reshape(N_DEV, depth, N_DEV, n_ada)
    mod_mine = lax.dynamic_index_in_dim(mod_all, me, axis=2, keepdims=False)
    mod_mine = mod_mine.transpose(1, 0, 2).reshape(depth, N_MOD, 1, d)

    shard = lambda n, l: _pad_lanes(w[n][l].astype(BF16))
    cols = np.cumsum([0, WIDTH, WIDTH, WIDTH, N_HEADS, WIDTH, WIDTH, WIDTH, d, d])

    def whole(n, p):
        rows, ncol = w[n].shape[1:]
        p = jnp.moveaxis(p[..., :ncol], 0, _shard_axis(n))
        return p.reshape((rows * N_DEV, ncol) if _shard_axis(n) == 0 else (rows, ncol * N_DEV))

    def first_weights(l, w_in_parts):
        wi = whole("w_in", w_in_parts)
        w_qkv = jnp.concatenate([wi[:, :cols[3]], wi[:, cols[4]:cols[7]]], axis=1)
        w_gates = wi[:, cols[7]:]
        w_f_t = jnp.concatenate([wi[:, cols[3]:cols[4]].T, jnp.zeros((16 - N_HEADS, d), BF16)], axis=0)
        w_f_pad = jnp.concatenate([w_f_t, jnp.zeros((AUG - 16, d), BF16)], axis=0)
        row = lambda a: a[l].reshape(1, -1)
        return dict(
            w_qkv=w_qkv, w_gates=w_gates, w_f_t=w_f_t, w_f_pad=w_f_pad, w_qkv_t=w_qkv.T, w_gates_t=w_gates.T,
            conv_w=conv_w_full[l].reshape(3, 2, -1).transpose(1, 0, 2), conv_b=conv_b[l].reshape(2, 1, -1),
            b_f=b_f[l], rel_bias=rel_bias[l],
            ln1_g=row(ln1_g), ln1_b=row(ln1_b), ln2_g=row(ln2_g), ln2_b=row(ln2_b))

    def rest_weights(parts):
        out = {}
        for n, p in zip(REST, parts):
            out[n] = whole(n, p)
            out[n + "_t"] = out[n].T
        return out

    xs, saved, layers = x0, [], []
    hs = _ln_mod(x0, mod_mine[0][1], mod_mine[0][0], "ln_mod1")
    w_in_parts = _all_gather([shard("w_in", 0)], "gather_weights")[0]
    for l in range(depth):
        layers.append(first_weights(l, w_in_parts))
        nxt = mod_mine[min(l + 1, depth - 1)]
        gather = [shard(n, l) for n in REST] + ([shard("w_in", l + 1)] if l + 1 < depth else [])
        xs, hs, sv, extra = _layer_fwd(xs, hs, layers[l], rest_weights, list(mod_mine[l]), (nxt[1], nxt[0]), alpha,
                                       gather)
        w_in_parts = extra[0] if extra else None
        saved.append(sv)
    dxs, loss_part = _loss_grad(xs, target, "loss_grad")
    loss = lax.psum(loss_part[0, 0], AXES)
    grads, big_parts, send = [None] * depth, {}, []
    for l in reversed(range(depth)):
        dxs, grads[l], landed, landed_late = _layer_bwd(dxs, saved[l], layers[l], list(mod_mine[l]), alpha, send,
                                                        ride_late=(l == 0))
        for n, part in zip([(n, l + 1) for n in LATE if send] + [(n, l) for n in EARLY], landed):
            big_parts[n] = part
        for n, part in zip(LATE, landed_late):
            big_parts[n, l] = part
        send = [_slab(grads[l][n], n) for n in LATE] if l else []
    grad_x = dxs.reshape(1, s, d)
    stack = lambda n: jnp.stack([grads[l][n] for l in range(depth)])

    rep = ["b_ada"] + [n for n in SMALL if n != "b_ada"]
    small_g = [stack("mod")] + [stack(n) for n in rep[1:]] + [stack("conv_w")]
    small_pack, small_g_rows = _pack(small_g, mult=8)
    small_parts = _all_gather([small_pack], "gather_small_grads")[0]
    n_rep_rows = sum(small_g_rows[:-1])
    dmod_all = _unpack(small_parts, small_g_rows[:1], [(depth, N_MOD * d)], lead=(N_DEV,))[0]
    cw_all = _unpack(small_parts[:, n_rep_rows:], small_g_rows[-1:], [small_g[-1].shape], lead=(N_DEV,))[0]

    out = {}
    for n in BIG:
        rows2d = lambda a: a.reshape(-1, a.shape[-1])
        res = _adamw([big_parts[n, l] for l in range(depth)], rows2d(w[n]), rows2d(m[n]), rows2d(v[n]),
                     "adamw_" + n, tr=256)
        for kind, r in zip(("grad", "delta", "new_m", "new_v"), res):
            out[kind, n] = r.reshape(w[n].shape)

    dmod_mine = lax.dynamic_slice_in_dim(dmod_all, me * n_ada, n_ada, axis=2)
    g_ada = jnp.stack([_mm(cond_all.T, dmod_mine[:, l], out_dtype=F32, name="g_w_ada", tk=N_DEV, tn=n_ada,
                           precision=HI) for l in range(depth)])
    lp = [_rows128(t["w_ada"], 8) for t in (w, m, v)]
    res = _adamw([_rows128(g_ada, 8)[None]], *lp, "adamw_ada")
    for kind, r in zip(("grad", "delta", "new_m", "new_v"), res):
        out[kind, "w_ada"] = _unpack(r, [lp[0].shape[0]], [w_ada.shape])[0]

    rp = [_pack([t[n] for n in rep], mult=8)[0] for t in (w, m, v)]
    res = _adamw([small_parts[:, :n_rep_rows]], *rp, "adamw_small")
    for kind, r in zip(("grad", "delta", "new_m", "new_v"), res):
        for n, a in zip(rep, _unpack(r, small_g_rows[:-1], [w[n].shape for n in rep])):
            out[kind, n] = a

    n_cw = conv_w.shape[2]
    cw_mine = lax.dynamic_slice_in_dim(cw_all, me * n_cw, n_cw, axis=3)
    cp = [_rows128(t["conv_w"], 8) for t in (w, m, v)]
    cw_in = jnp.stack([_rows128(cw_mine[i], 8) for i in range(N_DEV)])
    res = _adamw([cw_in], *cp, "adamw_conv_w")
    for kind, r in zip(("grad", "delta", "new_m", "new_v"), res):
        out[kind, "conv_w"] = _unpack(r, [cp[0].shape[0]], [conv_w.shape])[0]

    result = [loss, grad_x]
    for kind in ("grad", "delta", "new_m", "new_v"):
        result += [out[kind, n] for n in ORDER]
    return tuple(result)
```

```python
import functools
import math

import jax
import jax.numpy as jnp
import numpy as np
from jax import lax
from jax.experimental import pallas as pl
from jax.experimental.pallas import tpu as pltpu

F32 = jnp.float32
BF16 = jnp.bfloat16
HI = lax.Precision.HIGHEST
MESH_ID = pl.DeviceIdType.MESH
AXES = ("x", "y", "c")
N_DEV = 8

N_HEADS = 8
HEAD_DIM = 64
WIDTH = N_HEADS * HEAD_DIM
CHUNK = 64
LEFT_CHUNKS = 8
BAND = (LEFT_CHUNKS + 1) * CHUNK
REL_CLIP = 128
LN_EPS = 1e-5
N_MOD = 6
QK_SCALE = 1.0 / math.sqrt(HEAD_DIM)
LOG2E = math.log2(math.e)
LN2 = math.log(2.0)
QK_SCALE2 = QK_SCALE * LOG2E
NEG = -1e30
AUG = 128
N_SPLIT = 3
TQ = 512
FOX_FWD_T = 1024
SUB = 2 * CHUNK
SUBK = BAND + CHUNK

ADAM_LR, ADAM_B1, ADAM_B2, ADAM_EPS, ADAM_WD, ADAM_STEP = 0.001, 0.9, 0.999, 1e-08, 0.01, 10

VMEM_LIMIT = 56 * 2 ** 20


def _pcall(body, **kw):
    return pl.pallas_call(body, **kw)


def _cp(*sem):
    return pltpu.CompilerParams(dimension_semantics=sem if sem else None, vmem_limit_bytes=VMEM_LIMIT)


def _my_place():
    return lax.axis_index("x"), lax.axis_index("y"), lax.axis_index("c")


N_PEER = N_DEV - 1


def _comm_call(body, xs, out_shapes, name):
    n = len(xs)
    hbm = pl.BlockSpec(memory_space=pltpu.HBM)
    return _pcall(
        functools.partial(body, n), name=name, out_shape=out_shapes, in_specs=[hbm] * n, out_specs=[hbm] * n,
        scratch_shapes=[pltpu.SemaphoreType.DMA((N_PEER * n,)), pltpu.SemaphoreType.DMA((N_PEER * n,)),
                        pltpu.SemaphoreType.DMA((n,))],
    )(*xs)


def _all_gather(xs, name):
    def body(n, *refs):
        x_refs, out_refs, (send_sems, recv_sems, local_sems) = refs[:n], refs[n:2 * n], refs[2 * n:]
        mx, my, mc = _my_place()
        me, sibling = (mx, my, mc), (mx, my, 1 - mc)
        chips = [(1 - mx, my), (mx, 1 - my), (1 - mx, 1 - my)]

        def slot(a, px, py, pc):
            return out_refs[a].at[4 * px + 2 * py + pc]

        def copy(a, k, block, to, own=False):
            return pltpu.make_async_remote_copy(
                src_ref=x_refs[a] if own else slot(a, *block), dst_ref=slot(a, *block),
                send_sem=send_sems.at[a * N_PEER + k], recv_sem=recv_sems.at[a * N_PEER + k],
                device_id=to, device_id_type=MESH_ID)

        arrays = range(n)
        mine = [pltpu.make_async_copy(x_refs[a], slot(a, *me), local_sems.at[a]) for a in arrays]
        first = [copy(a, 0, me, sibling, own=True) for a in arrays]
        first += [copy(a, 1 + j, me, (*chip, mc), own=True) for a in arrays for j, chip in enumerate(chips)]
        for cp in mine + first:
            cp.start()
        passed = []
        for a in arrays:
            for j, chip in enumerate(chips):
                copy(a, 1 + j, (*chip, mc), me).wait_recv()
                passed.append(copy(a, 4 + j, (*chip, mc), sibling))
                passed[-1].start()
        for a in arrays:
            copy(a, 0, sibling, me).wait_recv()
            for j, chip in enumerate(chips):
                copy(a, 4 + j, (*chip, 1 - mc), me).wait_recv()
        for cp in first + passed:
            cp.wait_send()
        for cp in mine:
            cp.wait()

    return _comm_call(body, xs, [jax.ShapeDtypeStruct((N_DEV,) + x.shape, x.dtype) for x in xs], name)


def _direct_exchange(scatter, x_refs, out_refs, send_sems, recv_sems, local_sems):
    mx, my, mc = _my_place()
    me = 4 * mx + 2 * my + mc
    mine, copies = [], []
    for a, (x_ref, out_ref) in enumerate(zip(x_refs, out_refs)):
        mine.append(pltpu.make_async_copy(x_ref.at[me] if scatter else x_ref, out_ref.at[me], local_sems.at[a]))
        for k in range(N_PEER):
            px, py, pc = mx ^ ((k + 1) >> 2), my ^ (((k + 1) >> 1) & 1), mc ^ ((k + 1) & 1)
            copies.append(pltpu.make_async_remote_copy(
                src_ref=x_ref.at[4 * px + 2 * py + pc] if scatter else x_ref, dst_ref=out_ref.at[me],
                send_sem=send_sems.at[a * N_PEER + k], recv_sem=recv_sems.at[a * N_PEER + k],
                device_id=(px, py, pc), device_id_type=MESH_ID))

    def start():
        for cp in mine + copies:
            cp.start()

    def wait():
        for cp in copies:
            cp.wait_recv()
        for cp in copies:
            cp.wait_send()
        for cp in mine:
            cp.wait()

    return start, wait


def _exchange_shapes(scatter, xs):
    return [jax.ShapeDtypeStruct(x.shape if scatter else (N_DEV,) + x.shape, x.dtype) for x in xs]


def _exchange_scratch(n):
    return [pltpu.SemaphoreType.DMA((N_PEER * n,)), pltpu.SemaphoreType.DMA((N_PEER * n,)),
            pltpu.SemaphoreType.DMA((n,))]


def _tile(n, pref, unit=128):
    if n <= pref:
        return n
    t = pref - pref % unit
    while t > unit and n % t:
        t -= unit
    assert n % t == 0, (n, pref, unit)
    return t


def _mm(a, b, *, out_dtype, name, ta=False, tb=False, tm=1024, tn=1024, tk=512, precision=None,
        a_halves=False, b_halves=False, o_halves=False, exchange=None):
    a_shape = (a.shape[1], 2 * a.shape[2]) if a_halves else a.shape
    b_shape = (b.shape[1], 2 * b.shape[2]) if b_halves else b.shape
    (k_dim, m) = a_shape if ta else a_shape[::-1]
    n = b_shape[0] if tb else b_shape[1]
    tm, tn = _tile(m, tm, 8), _tile(n // 2 if (b_halves or o_halves) else n, tn)
    tk = _tile(k_dim // 2 if a_halves else k_dim, tk)
    nk = k_dim // tk
    dims = (((0 if ta else 1,), (1 if tb else 0,)), ((), ()))
    grid = (m // tm, n // tn, nk)
    ex_in, ex_out, ex_specs, ex_scratch, hook = _riding_exchange(exchange, 2, 1, grid)

    def body(*refs):
        (a_ref, b_ref, o_ref, acc_ref), finish = hook(refs)
        k = pl.program_id(2)
        if nk == 1:
            o_ref[...] = lax.dot_general(a_ref[...], b_ref[...], dims, preferred_element_type=F32,
                                         precision=precision).astype(o_ref.dtype)
            finish()
            return

        @pl.when(k == 0)
        def _():
            acc_ref[...] = jnp.zeros_like(acc_ref)

        acc_ref[...] += lax.dot_general(a_ref[...], b_ref[...], dims, preferred_element_type=F32,
                                        precision=precision)

        @pl.when(k == nk - 1)
        def _():
            o_ref[...] = acc_ref[...].astype(o_ref.dtype)

        finish()

    def spec(shape2, pick, halves, per_half):
        if not halves:
            return pl.BlockSpec(shape2, pick)

        def index(i, j, k):
            r, c = pick(i, j, k)
            return (c // per_half, r, c % per_half)
        return pl.BlockSpec((None,) + shape2, index)

    assert not (ta and a_halves) and not (tb and b_halves)
    a_spec = spec((tk, tm), lambda i, j, k: (k, i), False, 0) if ta else \
        spec((tm, tk), lambda i, j, k: (i, k), a_halves, (k_dim // 2) // tk if a_halves else 0)
    b_spec = spec((tn, tk), lambda i, j, k: (j, k), False, 0) if tb else \
        spec((tk, tn), lambda i, j, k: (k, j), b_halves, (n // 2) // tn if b_halves else 0)
    o_spec = spec((tm, tn), lambda i, j, k: (i, j), o_halves, (n // 2) // tn if o_halves else 0)
    out_shape = (2, m, n // 2) if o_halves else (m, n)
    res = _pcall(
        body, name=name,
        out_shape=[jax.ShapeDtypeStruct(out_shape, out_dtype)] + ex_out,
        grid=grid,
        in_specs=[a_spec, b_spec] + ex_specs,
        out_specs=[o_spec] + ex_specs,
        scratch_shapes=[pltpu.VMEM((tm, tn), F32)] + ex_scratch,
        compiler_params=_cp(*(["arbitrary"] * 3 if exchange else ["parallel", "parallel", "arbitrary"])),
    )(a, b, *ex_in)
    return res if exchange else res[0]


def _ln_stats(x):
    mu = jnp.mean(x, axis=-1, keepdims=True)
    xc = x - mu
    var = jnp.mean(xc * xc, axis=-1, keepdims=True)
    rstd = lax.rsqrt(var + LN_EPS)
    return xc * rstd, rstd


def _ln_mod(x, sc, sh, name, tm=512):
    s, d = x.shape
    tm = min(tm, s)

    def body(x_ref, sc_ref, sh_ref, h_ref):
        xhat, _ = _ln_stats(x_ref[...])
        h_ref[...] = (xhat * (1.0 + sc_ref[...]) + sh_ref[...]).astype(h_ref.dtype)

    row = pl.BlockSpec((1, d), lambda i: (0, 0))
    return _pcall(
        body, name=name, out_shape=jax.ShapeDtypeStruct((s, d), BF16), grid=(s // tm,),
        in_specs=[pl.BlockSpec((tm, d), lambda i: (i, 0)), row, row],
        out_specs=pl.BlockSpec((tm, d), lambda i: (i, 0)),
        compiler_params=_cp("parallel"),
    )(x, sc, sh)


def _mm_res_ln(a, w, xres, gate, ln_g, ln_b, sc, sh, alpha, name, tm=512, tk=512):
    s, k_dim = a.shape
    d = w.shape[1]
    tm, tk = _tile(s, tm, 8), _tile(k_dim, tk)
    nk = k_dim // tk

    def body(a_ref, w_ref, x_ref, g_ref, lg_ref, lb_ref, sc_ref, sh_ref, y_ref, xn_ref, h_ref, acc_ref):
        k = pl.program_id(1)

        @pl.when(k == 0)
        def _():
            acc_ref[...] = jnp.zeros_like(acc_ref)

        acc_ref[...] += jnp.dot(a_ref[...], w_ref[...], preferred_element_type=F32)

        @pl.when(k == nk - 1)
        def _():
            y = acc_ref[...]
            y_ref[...] = y
            z = alpha * x_ref[...] + (1.0 + g_ref[...]) * y
            zhat, _ = _ln_stats(z)
            xn = zhat * lg_ref[...] + lb_ref[...]
            xn_ref[...] = xn
            xhat, _ = _ln_stats(xn)
            h_ref[...] = (xhat * (1.0 + sc_ref[...]) + sh_ref[...]).astype(h_ref.dtype)

    row = pl.BlockSpec((1, d), lambda i, k: (0, 0))
    tile = pl.BlockSpec((tm, d), lambda i, k: (i, 0))
    return _pcall(
        body, name=name,
        out_shape=(jax.ShapeDtypeStruct((s, d), F32), jax.ShapeDtypeStruct((s, d), F32),
                   jax.ShapeDtypeStruct((s, d), BF16)),
        grid=(s // tm, nk),
        in_specs=[pl.BlockSpec((tm, tk), lambda i, k: (i, k)), pl.BlockSpec((tk, d), lambda i, k: (k, 0)),
                  tile, row, row, row, row, row],
        out_specs=(tile, tile, tile),
        scratch_shapes=[pltpu.VMEM((tm, d), F32)],
        compiler_params=_cp("parallel", "arbitrary"),
    )(a, w, xres, gate, ln_g, ln_b, sc, sh)


def _colsum(v):
    return jnp.sum(v, axis=0, keepdims=True)


def _ln_res_bwd(dxn, xres, y, gate, ln_g, alpha, name, tm=256):
    s, d = dxn.shape
    tm = min(tm, s)

    def body(dxn_ref, x_ref, y_ref, g_ref, lg_ref, dres_ref, dy_ref, sums_ref):
        @pl.when(pl.program_id(0) == 0)
        def _():
            sums_ref[...] = jnp.zeros_like(sums_ref)

        dxn_v, y_v = dxn_ref[...], y_ref[...]
        one_g = 1.0 + g_ref[...]
        zhat, rstd = _ln_stats(alpha * x_ref[...] + one_g * y_v)
        dzh = dxn_v * lg_ref[...]
        dz = rstd * (dzh - jnp.mean(dzh, axis=-1, keepdims=True)
                     - zhat * jnp.mean(dzh * zhat, axis=-1, keepdims=True))
        dres_ref[...] = alpha * dz
        dy_ref[...] = (one_g * dz).astype(dy_ref.dtype)
        sums_ref[0:1, :] += _colsum(dxn_v * zhat)
        sums_ref[1:2, :] += _colsum(dxn_v)
        sums_ref[2:3, :] += _colsum(dz * y_v)

    row = pl.BlockSpec((1, d), lambda i: (0, 0))
    tile = pl.BlockSpec((tm, d), lambda i: (i, 0))
    return _pcall(
        body, name=name,
        out_shape=(jax.ShapeDtypeStruct((s, d), F32), jax.ShapeDtypeStruct((s, d), BF16),
                   jax.ShapeDtypeStruct((8, d), F32)),
        grid=(s // tm,),
        in_specs=[tile, tile, tile, row, row],
        out_specs=(tile, tile, pl.BlockSpec((8, d), lambda i: (0, 0))),
        compiler_params=_cp("arbitrary"),
    )(dxn, xres, y, gate, ln_g)


def _ln_mod_bwd(dh, x, sc, dres, name, extra=None, tm=256):
    s, d = dh.shape
    tm = min(tm, s)

    def body(*refs):
        if extra is None:
            dh_ref, x_ref, sc_ref, dres_ref, dx_ref, sums_ref = refs
            dh_v = dh_ref[...]
        else:
            dh_ref, x_ref, sc_ref, dres_ref, dh2_ref, df_ref, wf_ref, dx_ref, sums_ref = refs
            dh_v = dh_ref[...] + dh2_ref[...] + jnp.dot(df_ref[...], wf_ref[...], preferred_element_type=F32)

        @pl.when(pl.program_id(0) == 0)
        def _():
            sums_ref[...] = jnp.zeros_like(sums_ref)

        xhat, rstd = _ln_stats(x_ref[...])
        dxh = dh_v * (1.0 + sc_ref[...])
        dx_ref[...] = dres_ref[...] + rstd * (dxh - jnp.mean(dxh, axis=-1, keepdims=True)
                                             - xhat * jnp.mean(dxh * xhat, axis=-1, keepdims=True))
        sums_ref[0:1, :] += _colsum(dh_v)
        sums_ref[1:2, :] += _colsum(dh_v * xhat)

    row = pl.BlockSpec((1, d), lambda i: (0, 0))
    tile = pl.BlockSpec((tm, d), lambda i: (i, 0))
    in_specs = [tile, tile, row, tile]
    args = [dh, x, sc, dres]
    if extra is not None:
        in_specs += [tile, pl.BlockSpec((tm, AUG), lambda i: (i, 0)), pl.BlockSpec((AUG, d), lambda i: (0, 0))]
        args += list(extra)
    return _pcall(
        body, name=name,
        out_shape=(jax.ShapeDtypeStruct((s, d), F32), jax.ShapeDtypeStruct((8, d), F32)),
        grid=(s // tm,), in_specs=in_specs,
        out_specs=(tile, pl.BlockSpec((8, d), lambda i: (0, 0))),
        compiler_params=_cp("arbitrary"),
    )(*args)


def _loss_grad(xn, target, name, tm=512):
    s, d = xn.shape
    tm = min(tm, s)
    n = s // tm

    def body(x_ref, t_ref, dx_ref, loss_ref, acc_ref):
        i = pl.program_id(0)

        @pl.when(i == 0)
        def _():
            acc_ref[...] = jnp.zeros_like(acc_ref)

        err = x_ref[...] - t_ref[...]
        dx_ref[...] = err * (1.0 / d)
        acc_ref[...] += _colsum(err * err)

        @pl.when(i == n - 1)
        def _():
            loss_ref[...] = jnp.zeros_like(loss_ref) + (0.5 / d) * jnp.sum(acc_ref[...])

    tile = pl.BlockSpec((tm, d), lambda i: (i, 0))
    return _pcall(
        body, name=name,
        out_shape=(jax.ShapeDtypeStruct((s, d), F32), jax.ShapeDtypeStruct((8, 128), F32)),
        grid=(n,), in_specs=[tile, tile],
        out_specs=(tile, pl.BlockSpec((8, 128), lambda i: (0, 0))),
        scratch_shapes=[pltpu.VMEM((1, d), F32)],
        compiler_params=_cp("arbitrary"),
    )(xn, target)


def _sigmoid(v):
    return 0.5 * jnp.tanh(0.5 * v) + 0.5


def _merge_fwd(o_a, o_c, w_a, w_c, gates, name, tm=256):
    s, wd = o_a.shape
    d = w_a.shape[1]
    tm = min(tm, s)

    def body(oa_ref, oc_ref, wa_ref, wc_ref, ga_ref, gc_ref, m_ref):
        ba = jnp.dot(oa_ref[...], wa_ref[...], preferred_element_type=F32)
        bc = jnp.dot(oc_ref[...], wc_ref[...], preferred_element_type=F32)
        m_ref[...] = (_sigmoid(ga_ref[...]) * ba + _sigmoid(gc_ref[...]) * bc).astype(m_ref.dtype)

    o_spec = pl.BlockSpec((tm, wd), lambda i: (i, 0))
    w_spec = pl.BlockSpec((wd, d), lambda i: (0, 0))
    return _pcall(
        body, name=name, out_shape=jax.ShapeDtypeStruct((s, d), BF16), grid=(s // tm,),
        in_specs=[o_spec, o_spec, w_spec, w_spec,
                  pl.BlockSpec((tm, d), lambda i: (i, 0)), pl.BlockSpec((tm, d), lambda i: (i, 1))],
        out_specs=pl.BlockSpec((tm, d), lambda i: (i, 0)),
        compiler_params=_cp("parallel"),
    )(o_a, o_c, w_a, w_c, gates, gates)


def _merge_bwd(dmix, w_out_t, o_a, o_c, w_a, w_c, w_a_t, w_c_t, gates, name, tm=256):
    s, wd = o_a.shape
    d = w_a.shape[1]
    tm = min(tm, s)

    def body(dmix_ref, wot_ref, oa_ref, oc_ref, wa_ref, wc_ref, wat_ref, wct_ref, ga_ref, gc_ref,
             dba_ref, dbc_ref, doa_ref, doc_ref, dg_ref):
        dm = jnp.dot(dmix_ref[...], wot_ref[...], preferred_element_type=F32)
        for half, (o_ref, w_ref, wt_ref, g_ref, db_ref, do_ref) in enumerate((
                (oa_ref, wa_ref, wat_ref, ga_ref, dba_ref, doa_ref),
                (oc_ref, wc_ref, wct_ref, gc_ref, dbc_ref, doc_ref))):
            b = jnp.dot(o_ref[...], w_ref[...], preferred_element_type=F32)
            sg = _sigmoid(g_ref[...])
            db = (dm * sg).astype(BF16)
            db_ref[...] = db
            dg_ref[:, d * half:d * (half + 1)] = (dm * b * sg * (1.0 - sg)).astype(dg_ref.dtype)
            do_ref[...] = jnp.dot(db, wt_ref[...], preferred_element_type=F32).astype(do_ref.dtype)

    row_d = pl.BlockSpec((tm, d), lambda i: (i, 0))
    row_w = pl.BlockSpec((tm, wd), lambda i: (i, 0))
    full = lambda shp: pl.BlockSpec(shp, lambda i: (0, 0))
    return _pcall(
        body, name=name,
        out_shape=(jax.ShapeDtypeStruct((s, d), BF16), jax.ShapeDtypeStruct((s, d), BF16),
                   jax.ShapeDtypeStruct((s, wd), BF16), jax.ShapeDtypeStruct((s, wd), BF16),
                   jax.ShapeDtypeStruct((s, 2 * d), BF16)),
        grid=(s // tm,),
        in_specs=[row_d, full((d, d)), row_w, row_w, full((wd, d)), full((wd, d)), full((d, wd)), full((d, wd)),
                  row_d, pl.BlockSpec((tm, d), lambda i: (i, 1))],
        out_specs=(row_d, row_d, row_w, row_w, pl.BlockSpec((tm, 2 * d), lambda i: (i, 0))),
        compiler_params=_cp("parallel"),
    )(dmix, w_out_t, o_a, o_c, w_a, w_c, w_a_t, w_c_t, gates, gates)


def _shift_down(cur, k, fill_rows):
    out = pltpu.roll(cur, k, axis=0)
    rid = lax.broadcasted_iota(jnp.int32, cur.shape, 0)
    for r, fill in enumerate(fill_rows):
        out = jnp.where(rid == r, fill, out)
    return out


def _shift_up(cur, k, fill_rows):
    n = cur.shape[0]
    out = pltpu.roll(cur, n - k, axis=0)
    rid = lax.broadcasted_iota(jnp.int32, cur.shape, 0)
    for r, fill in enumerate(fill_rows):
        out = jnp.where(rid == n - k + r, fill, out)
    return out


def _conv_rows(cur, prev8, first, w, b):
    p6 = jnp.where(first, 0.0, prev8[6:7, :])
    p7 = jnp.where(first, 0.0, prev8[7:8, :])
    m1 = _shift_down(cur, 1, [p7])
    m2 = _shift_down(cur, 2, [p6, p7])
    u = b + w[0:1, :] * m2 + w[1:2, :] * m1 + w[2:3, :] * cur
    return u, m1, m2


def _conv_specs(tr, tc, order):
    r8 = tr // 8
    at = lambda f: (lambda *g: f(*order(*g)))
    return [pl.BlockSpec((2, tr, tc), at(lambda i, j: (0, i, j))),
            pl.BlockSpec((2, 8, tc), at(lambda i, j: (0, jnp.maximum(i * r8 - 1, 0), j))),
            pl.BlockSpec((2, 3, tc), at(lambda i, j: (0, 0, j))),
            pl.BlockSpec((2, 1, tc), at(lambda i, j: (0, 0, j)))]


def _conv_act_fwd(u0, conv_w, conv_b, name, tr=512, tc=256):
    _, s, f = u0.shape
    tr, tc = _tile(s, tr, 8), _tile(f, tc)

    def body(u_ref, p_ref, w_ref, b_ref, act_ref):
        first = pl.program_id(0) == 0
        a, _, _ = _conv_rows(u_ref[0], p_ref[0], first, w_ref[0], b_ref[0])
        v, _, _ = _conv_rows(u_ref[1], p_ref[1], first, w_ref[1], b_ref[1])
        act_ref[...] = (a * _sigmoid(a) * v).astype(act_ref.dtype)

    return _pcall(
        body, name=name, out_shape=jax.ShapeDtypeStruct((s, f), BF16), grid=(s // tr, f // tc),
        in_specs=_conv_specs(tr, tc, lambda i, j: (i, j)),
        out_specs=pl.BlockSpec((tr, tc), lambda i, j: (i, j)),
        compiler_params=_cp("parallel", "parallel"),
    )(u0, u0, conv_w, conv_b)


def _act_grads(dact, a, v):
    sg = _sigmoid(a)
    return dact * v * sg * (1.0 + a * (1.0 - sg)), dact * a * sg


def _conv_act_bwd(dact, u0, conv_w, conv_b, name, tr=512, tc=256):
    _, s, f = u0.shape
    tr, tc = _tile(s, tr, 8), _tile(f, tc)
    n = s // tr
    r8 = tr // 8

    def body(dact_ref, dnext_ref, u_ref, p_ref, unext_ref, w_ref, b_ref, du0_ref, sums_ref):
        i = pl.program_id(1)
        first, last = i == 0, i == n - 1

        @pl.when(first)
        def _():
            sums_ref[...] = jnp.zeros_like(sums_ref)

        cur = (u_ref[0], u_ref[1])
        (a, a1, a2), (v, v1, v2) = [_conv_rows(cur[hf], p_ref[hf], first, w_ref[hf], b_ref[hf]) for hf in range(2)]
        da, dv = _act_grads(dact_ref[...], a, v)
        (an, _, _), (vn, _, _) = [_conv_rows(unext_ref[hf], cur[hf][tr - 8:tr, :], False, w_ref[hf], b_ref[hf])
                                  for hf in range(2)]
        dan, dvn = _act_grads(dnext_ref[...], an, vn)
        for hf, g, gn, shifted in ((0, da, dan, (a2, a1, cur[0])), (1, dv, dvn, (v2, v1, cur[1]))):
            w = w_ref[hf]
            n0 = jnp.where(last, 0.0, gn[0:1, :])
            n1 = jnp.where(last, 0.0, gn[1:2, :])
            du0 = w[2:3, :] * g + w[1:2, :] * _shift_up(g, 1, [n0]) + w[0:1, :] * _shift_up(g, 2, [n0, n1])
            du0_ref[hf] = du0.astype(du0_ref.dtype)
            for r in range(3):
                sums_ref[hf, r:r + 1, :] += _colsum(g * shifted[r])
            sums_ref[hf, 3:4, :] += _colsum(g)

    nxt = lambda i: jnp.minimum((i + 1) * r8, s // 8 - 1)
    order = lambda j, i: (i, j)
    specs = _conv_specs(tr, tc, order)
    return _pcall(
        body, name=name,
        out_shape=(jax.ShapeDtypeStruct((2, s, f), BF16), jax.ShapeDtypeStruct((2, 8, f), F32)),
        grid=(f // tc, n),
        in_specs=[pl.BlockSpec((tr, tc), lambda j, i: (i, j)), pl.BlockSpec((8, tc), lambda j, i: (nxt(i), j)),
                  specs[0], specs[1], pl.BlockSpec((2, 8, tc), lambda j, i: (0, nxt(i), j)), specs[2], specs[3]],
        out_specs=(pl.BlockSpec((2, tr, tc), lambda j, i: (0, i, j)), pl.BlockSpec((2, 8, tc), lambda j, i: (0, 0, j))),
        compiler_params=_cp("parallel", "arbitrary"),
    )(dact, dact, u0, u0, u0, conv_w, conv_b)


def _cumsum_rows(v, reverse=False):
    r = v.shape[0]
    i128 = lax.broadcasted_iota(jnp.int32, (128, 128), 0), lax.broadcasted_iota(jnp.int32, (128, 128), 1)
    ir = lax.broadcasted_iota(jnp.int32, (r, r), 0), lax.broadcasted_iota(jnp.int32, (r, r), 1)
    in_row = (i128[0] >= i128[1] if reverse else i128[0] <= i128[1]).astype(F32)
    rows = (ir[1] > ir[0] if reverse else ir[1] < ir[0]).astype(F32)
    within = jnp.dot(v, in_row, preferred_element_type=F32, precision=HI)
    tot = jnp.broadcast_to(within[:, 0:1] if reverse else within[:, 127:128], (r, 128))
    return within + jnp.dot(rows, tot, preferred_element_type=F32, precision=HI)


def _forget_fwd(f_t, b_f, name):
    h, r, _ = f_t.shape

    def body(f_ref, b_ref, o_ref):
        z = f_ref[...] + b_ref[...]
        logf = jnp.minimum(z, 0.0) - jnp.log(1.0 + jnp.exp(-jnp.abs(z)))
        rest = _cumsum_rows(logf) * LOG2E
        for i in range(N_SPLIT):
            piece = rest.astype(BF16).astype(F32)
            o_ref[i] = piece
            rest = rest - piece

    return _pcall(
        body, name=name, out_shape=jax.ShapeDtypeStruct((N_SPLIT, h, r, 128), F32), grid=(h,),
        in_specs=[pl.BlockSpec((None, r, 128), lambda i: (i, 0, 0)), pl.BlockSpec((None, 1, 128), lambda i: (i, 0, 0))],
        out_specs=pl.BlockSpec((N_SPLIT, None, r, 128), lambda i: (0, i, 0, 0)),
        compiler_params=_cp("parallel"),
    )(f_t, b_f)


def _forget_bwd(d_cum, f_t, b_f, name):
    h, r, _ = f_t.shape

    def body(g_ref, f_ref, b_ref, df_ref, db_ref):
        df = _cumsum_rows(g_ref[...], reverse=True) * _sigmoid(-(f_ref[...] + b_ref[...]))
        df_ref[...] = df
        db_ref[...] = jnp.zeros_like(db_ref) + jnp.sum(df)

    blk = pl.BlockSpec((None, r, 128), lambda i: (i, 0, 0))
    one = pl.BlockSpec((None, 1, 128), lambda i: (i, 0, 0))
    return _pcall(
        body, name=name,
        out_shape=(jax.ShapeDtypeStruct((h, r, 128), F32), jax.ShapeDtypeStruct((h, 1, 128), F32)),
        grid=(h,), in_specs=[blk, blk, one], out_specs=(blk, one),
        compiler_params=_cp("parallel"),
    )(d_cum, f_t, b_f)


_NT = (((1,), (1,)), ((), ()))


def _causal_keep(tk, tq):
    return lax.broadcasted_iota(jnp.int32, (tk, tq), 0) <= lax.broadcasted_iota(jnp.int32, (tk, tq), 1)


_TN = (((0,), (0,)), ((), ()))
PAIR = 2
V_FOX = 2 * WIDTH // 128
Q_CHUNK, K_CHUNK, V_CHUNK = (3 * WIDTH) // 128, (4 * WIDTH) // 128, (5 * WIDTH) // 128


def _riding_exchange(exchange, n_in, n_out, grid):
    if exchange is None:
        return [], [], [], [], lambda refs: (refs, lambda: None)
    scatter, xs = exchange
    n = len(xs)
    hbm = pl.BlockSpec(memory_space=pltpu.HBM)

    def hook(refs):
        ins, x_refs = refs[:n_in], refs[n_in:n_in + n]
        outs, land_refs = refs[n_in + n:n_in + n + n_out], refs[n_in + n + n_out:n_in + 2 * n + n_out]
        rest = refs[n_in + 2 * n + n_out:]
        own_scratch, sems = rest[:len(rest) - 3], rest[len(rest) - 3:]
        start, wait = _direct_exchange(scatter, x_refs, land_refs, *sems)
        ids = [pl.program_id(a) for a in range(len(grid))]
        is_first = functools.reduce(jnp.logical_and, [i == 0 for i in ids])
        is_last = functools.reduce(jnp.logical_and, [i == g - 1 for i, g in zip(ids, grid)])
        pl.when(is_first)(start)
        return tuple(ins) + tuple(outs) + tuple(own_scratch), lambda: pl.when(is_last)(wait)

    return list(xs), _exchange_shapes(scatter, xs), [hbm] * n, _exchange_scratch(n), hook


def _fox_fwd(q_aug, k_aug, qkv, name, t=TQ, exchange=None):
    s = q_aug.shape[0]
    t = min(t, s)
    n = s // t
    dh = HEAD_DIM
    grid = (N_HEADS // PAIR, n)
    ex_in, ex_out, ex_specs, ex_scratch, hook = _riding_exchange(exchange, 3, 2, grid)

    def body(*refs):
        (q_ref, k_ref, v_ref, o_ref, lse_ref, m_s, l_s, acc_s), finish = hook(refs)
        qi = pl.program_id(1)
        m_s[...] = jnp.full_like(m_s, NEG)
        l_s[...] = jnp.zeros_like(l_s)
        acc_s[...] = jnp.zeros_like(acc_s)

        def step(kj, diag):
            rows = pl.ds(pl.multiple_of(kj * t, t), t)
            for hh in range(PAIR):
                st = lax.dot_general(k_ref[rows, AUG * hh:AUG * (hh + 1)], q_ref[:, AUG * hh:AUG * (hh + 1)], _NT,
                                     preferred_element_type=F32)
                if diag:
                    st = jnp.where(_causal_keep(t, t), st, NEG)
                m_prev = m_s[hh]
                m_new = jnp.maximum(m_prev, jnp.max(st, axis=0, keepdims=True))
                a = jnp.exp2(m_prev - m_new)
                pt = jnp.exp2(st - m_new)
                l_s[hh] = a * l_s[hh] + jnp.sum(pt, axis=0, keepdims=True)
                acc_s[hh] = a * acc_s[hh] + lax.dot_general(v_ref[rows, dh * hh:dh * (hh + 1)], pt.astype(BF16), _TN,
                                                            preferred_element_type=F32)
                m_s[hh] = m_new

        def off_diagonal(kj, carry):
            step(kj, False)
            return carry

        lax.fori_loop(0, qi, off_diagonal, 0)
        step(qi, True)
        for hh in range(PAIR):
            o_ref[hh] = acc_s[hh] / l_s[hh]
            lse_ref[hh, 0] = m_s[hh] + jnp.log2(l_s[hh])
        finish()

    return _pcall(
        body, name=name,
        out_shape=[jax.ShapeDtypeStruct((N_HEADS, dh, s), F32), jax.ShapeDtypeStruct((N_HEADS, n, 1, t), F32)] + ex_out,
        grid=grid,
        in_specs=[pl.BlockSpec((t, PAIR * AUG), lambda hp, i: (i, hp)),
                  pl.BlockSpec((s, PAIR * AUG), lambda hp, i: (0, hp)),
                  pl.BlockSpec((s, PAIR * dh), lambda hp, i: (0, V_FOX + hp))] + ex_specs,
        out_specs=[pl.BlockSpec((PAIR, dh, t), lambda hp, i: (hp, 0, i)),
                   pl.BlockSpec((PAIR, 1, 1, t), lambda hp, i: (hp, i, 0, 0))] + ex_specs,
        scratch_shapes=[pltpu.VMEM((PAIR, 1, t), F32), pltpu.VMEM((PAIR, 1, t), F32),
                        pltpu.VMEM((PAIR, dh, t), F32)] + ex_scratch,
        compiler_params=_cp("arbitrary", "arbitrary"),
    )(q_aug, k_aug, qkv, *ex_in)


def _row_dot(a_t, b_t, name, t=2048):
    h, dh, s = a_t.shape
    t = min(t, s)

    def body(a_ref, b_ref, o_ref):
        o_ref[...] = jnp.sum(a_ref[...].astype(F32) * b_ref[...], axis=0, keepdims=True)

    blk = pl.BlockSpec((None, dh, t), lambda hh, i: (hh, 0, i))
    return _pcall(
        body, name=name, out_shape=jax.ShapeDtypeStruct((h, 1, s), F32), grid=(h, s // t),
        in_specs=[blk, blk], out_specs=pl.BlockSpec((None, 1, t), lambda hh, i: (hh, 0, i)),
        compiler_params=_cp("parallel", "parallel"),
    )(a_t, b_t)


def _fox_bwd(q_aug, k_aug, qkv, do, lse, delta, name, t=TQ, exchange=None):
    s = q_aug.shape[0]
    dh = HEAD_DIM
    t = min(t, s)
    n = s // t
    grid = (N_HEADS // PAIR, n)
    ex_in, ex_out, ex_specs, ex_scratch, hook = _riding_exchange(exchange, 6, 5, grid)

    def body(*refs):
        (q_ref, k_ref, v_ref, do_ref, lse_ref, dl_ref, dk_ref, dv_ref, dq_ref, dsum_ref, dqsum_ref,
         dk_s, dv_s, dsum_s), finish = hook(refs)
        kj = pl.program_id(1)

        @pl.when(kj == 0)
        def _():
            dq_ref[...] = jnp.zeros_like(dq_ref)
            dqsum_ref[...] = jnp.zeros_like(dqsum_ref)

        dk_s[...] = jnp.zeros_like(dk_s)
        dv_s[...] = jnp.zeros_like(dv_s)
        dsum_s[...] = jnp.zeros_like(dsum_s)

        def step(qi, diag):
            rows = pl.ds(pl.multiple_of(qi * t, t), t)
            for hh in range(PAIR):
                q = q_ref[rows, AUG * hh:AUG * (hh + 1)]
                k = k_ref[:, AUG * hh:AUG * (hh + 1)]
                st = lax.dot_general(k, q, _NT, preferred_element_type=F32)
                if diag:
                    st = jnp.where(_causal_keep(t, t), st, NEG)
                pt = jnp.exp2(st - lse_ref[hh, qi])
                do_v = do_ref[rows, dh * hh:dh * (hh + 1)]
                dpt = lax.dot_general(v_ref[:, dh * hh:dh * (hh + 1)], do_v, _NT, preferred_element_type=F32)
                ds32 = pt * (dpt - dl_ref[hh, qi])
                dsum_s[hh] += sum(ds32[:, 128 * u:128 * (u + 1)] for u in range(t // 128))
                dqsum_ref[hh, qi] += jnp.sum(ds32, axis=0, keepdims=True)
                dst = ds32.astype(BF16)
                dv_s[hh] += jnp.dot(pt.astype(BF16), do_v, preferred_element_type=F32)
                dk_s[hh] += jnp.dot(dst, q[:, :dh], preferred_element_type=F32)
                dq_ref[hh, qi] += lax.dot_general(k[:, :dh], dst, _TN, preferred_element_type=F32)

        def off_diagonal(qi, carry):
            step(qi, False)
            return carry

        step(kj, True)
        lax.fori_loop(kj + 1, n, off_diagonal, 0)
        for hh in range(PAIR):
            dk_ref[:, dh * hh:dh * (hh + 1)] = (LN2 * dk_s[hh]).astype(dk_ref.dtype)
            dv_ref[:, dh * hh:dh * (hh + 1)] = dv_s[hh].astype(dv_ref.dtype)
            dsum_ref[:, 128 * hh:128 * (hh + 1)] = dsum_s[hh]
        finish()

    stat = pl.BlockSpec((PAIR, n, 1, t), lambda hp, j: (hp, 0, 0, 0))
    return _pcall(
        body, name=name,
        out_shape=[jax.ShapeDtypeStruct((s, WIDTH), BF16), jax.ShapeDtypeStruct((s, WIDTH), BF16),
                   jax.ShapeDtypeStruct((N_HEADS, n, dh, t), F32), jax.ShapeDtypeStruct((s, N_HEADS * 128), F32),
                   jax.ShapeDtypeStruct((N_HEADS, n, 1, t), F32)] + ex_out,
        grid=grid,
        in_specs=[pl.BlockSpec((s, PAIR * AUG), lambda hp, j: (0, hp)),
                  pl.BlockSpec((t, PAIR * AUG), lambda hp, j: (j, hp)),
                  pl.BlockSpec((t, PAIR * dh), lambda hp, j: (j, V_FOX + hp)),
                  pl.BlockSpec((s, PAIR * dh), lambda hp, j: (0, hp)),
                  stat, stat] + ex_specs,
        out_specs=[pl.BlockSpec((t, PAIR * dh), lambda hp, j: (j, hp)),
                   pl.BlockSpec((t, PAIR * dh), lambda hp, j: (j, hp)),
                   pl.BlockSpec((PAIR, n, dh, t), lambda hp, j: (hp, 0, 0, 0)),
                   pl.BlockSpec((t, PAIR * 128), lambda hp, j: (j, hp)),
                   stat] + ex_specs,
        scratch_shapes=[pltpu.VMEM((PAIR, t, dh), F32), pltpu.VMEM((PAIR, t, dh), F32),
                        pltpu.VMEM((PAIR, t, 128), F32)] + ex_scratch,
        compiler_params=_cp("arbitrary", "arbitrary"),
    )(q_aug, k_aug, qkv, do, lse, delta, *ex_in)


N_SUB = TQ // SUB


def _pair_blockdiag(x2):
    lane = lax.broadcasted_iota(jnp.int32, x2.shape, 1)
    zero = jnp.zeros_like(x2)
    return jnp.concatenate([jnp.where(lane < HEAD_DIM, x2, zero), jnp.where(lane >= HEAD_DIM, x2, zero)], axis=0)


def _pair_lanes(ref, j):
    return jnp.concatenate([ref[hh, 0, :, SUB * j:SUB * (j + 1)] for hh in range(PAIR)], axis=1)


def _chunk_logits(kwin, qbd, b_ref, first_tile, j):
    which = jnp.where(first_tile, 1 + j, 0)
    bias = jnp.concatenate([b_ref[hh, which] for hh in range(PAIR)], axis=1)
    return lax.dot_general(kwin, qbd, _NT, preferred_element_type=F32) + bias


def _chunk_fwd(qkv, bias_t, name):
    s = qkv.shape[0]
    dh = HEAD_DIM
    n = s // TQ
    nsub = TQ // SUB
    prev = lambda i: jnp.maximum(i - 1, 0)

    def body(q_ref, kp_ref, kc_ref, vp_ref, vc_ref, b_ref, o_ref, lse_ref):
        first_tile = pl.program_id(1) == 0
        kw = jnp.concatenate([kp_ref[...], kc_ref[...]], axis=0)
        vw = jnp.concatenate([vp_ref[...], vc_ref[...]], axis=0)
        for j in range(nsub):
            rows, win = slice(SUB * j, SUB * (j + 1)), slice(SUB * j, SUB * j + SUBK)
            st = _chunk_logits(kw[win], _pair_blockdiag(q_ref[rows, :] * QK_SCALE2), b_ref, first_tile, j)
            m = jnp.max(st, axis=0, keepdims=True)
            pt = jnp.exp2(st - m)
            l = jnp.sum(pt, axis=0, keepdims=True)
            ot = lax.dot_general(vw[win], pt.astype(BF16), _TN, preferred_element_type=F32)
            stat = m + jnp.log2(l)
            for hh in range(PAIR):
                lanes = slice(SUB * hh, SUB * (hh + 1))
                o_ref[hh, :, rows] = ot[dh * hh:dh * (hh + 1), lanes] / l[:, lanes]
                lse_ref[hh, 0, :, rows] = stat[:, lanes]

    blk = lambda col, m: pl.BlockSpec((TQ, PAIR * dh), lambda hp, i: (m(i), col + hp))
    same = lambda i: i
    return _pcall(
        body, name=name,
        out_shape=(jax.ShapeDtypeStruct((N_HEADS, dh, s), F32), jax.ShapeDtypeStruct((N_HEADS, n, 1, TQ), F32)),
        grid=(N_HEADS // PAIR, n),
        in_specs=[blk(Q_CHUNK, same), blk(K_CHUNK, prev), blk(K_CHUNK, same), blk(V_CHUNK, prev), blk(V_CHUNK, same),
                  pl.BlockSpec((PAIR, 1 + N_SUB, SUBK, SUB), lambda hp, i: (hp, 0, 0, 0))],
        out_specs=(pl.BlockSpec((PAIR, dh, TQ), lambda hp, i: (hp, 0, i)),
                   pl.BlockSpec((PAIR, 1, 1, TQ), lambda hp, i: (hp, i, 0, 0))),
        compiler_params=_cp("parallel", "parallel"),
    )(qkv, qkv, qkv, qkv, qkv, bias_t)


def _chunk_bwd(qkv, do, lse, delta, bias_t, name):
    s = qkv.shape[0]
    dh = HEAD_DIM
    n = s // TQ
    nsub = TQ // SUB
    cur = lambda i: jnp.minimum(i, n - 1)
    prev = lambda i: jnp.maximum(cur(i) - 1, 0)
    done = lambda i: jnp.maximum(i - 1, 0)

    def body(q_ref, kp_ref, kc_ref, vp_ref, vc_ref, do_ref, lse_ref, dl_ref, b_ref,
             dq_ref, dk_ref, dv_ref, db_ref, dkw_s, dvw_s, ck_s, cv_s):
        i = pl.program_id(1)

        @pl.when(i == 0)
        def _():
            db_ref[...] = jnp.zeros_like(db_ref)
            ck_s[...] = jnp.zeros_like(ck_s)
            cv_s[...] = jnp.zeros_like(cv_s)

        dkw_s[...] = jnp.zeros_like(dkw_s)
        dvw_s[...] = jnp.zeros_like(dvw_s)

        @pl.when(i < n)
        def _():
            first_tile = i == 0
            kw = jnp.concatenate([kp_ref[...], kc_ref[...]], axis=0)
            vw = jnp.concatenate([vp_ref[...], vc_ref[...]], axis=0)
            for j in range(nsub):
                rows, win = slice(SUB * j, SUB * (j + 1)), slice(SUB * j, SUB * j + SUBK)
                qbd = _pair_blockdiag(q_ref[rows, :] * QK_SCALE2)
                dobd = _pair_blockdiag(do_ref[rows, :])
                st = _chunk_logits(kw[win], qbd, b_ref, first_tile, j)
                pt = jnp.exp2(st - _pair_lanes(lse_ref, j))
                dpt = lax.dot_general(vw[win], dobd, _NT, preferred_element_type=F32)
                dst = pt * (dpt - _pair_lanes(dl_ref, j))
                dsb = dst.astype(BF16)
                dvw_s[win, :] += jnp.dot(pt.astype(BF16), dobd, preferred_element_type=F32)
                dkw_s[win, :] += jnp.dot(dsb, qbd, preferred_element_type=F32)
                dqt = QK_SCALE * lax.dot_general(kw[win], dsb, _TN, preferred_element_type=F32)
                for hh in range(PAIR):
                    lanes = slice(SUB * hh, SUB * (hh + 1))
                    db_ref[hh] += dst[:, lanes]
                    dq_ref[hh, :, rows] = dqt[dh * hh:dh * (hh + 1), lanes]

        dk_ref[...] = (LN2 * (ck_s[...] + dkw_s[0:TQ, :])).astype(dk_ref.dtype)
        dv_ref[...] = (cv_s[...] + dvw_s[0:TQ, :]).astype(dv_ref.dtype)
        ck_s[...] = dkw_s[TQ:2 * TQ, :]
        cv_s[...] = dvw_s[TQ:2 * TQ, :]

    blk = lambda col, m: pl.BlockSpec((TQ, PAIR * dh), lambda hp, i: (m(i), col + hp))
    stat = pl.BlockSpec((PAIR, 1, 1, TQ), lambda hp, i: (hp, cur(i), 0, 0))
    bs = pl.BlockSpec((PAIR, SUBK, SUB), lambda hp, i: (hp, 0, 0))
    return _pcall(
        body, name=name,
        out_shape=(jax.ShapeDtypeStruct((N_HEADS, dh, s), F32), jax.ShapeDtypeStruct((s, WIDTH), BF16),
                   jax.ShapeDtypeStruct((s, WIDTH), BF16), jax.ShapeDtypeStruct((N_HEADS, SUBK, SUB), F32)),
        grid=(N_HEADS // PAIR, n + 1),
        in_specs=[blk(Q_CHUNK, cur), blk(K_CHUNK, prev), blk(K_CHUNK, cur), blk(V_CHUNK, prev), blk(V_CHUNK, cur),
                  blk(0, cur), stat, stat, pl.BlockSpec((PAIR, 1 + N_SUB, SUBK, SUB), lambda hp, i: (hp, 0, 0, 0))],
        out_specs=(pl.BlockSpec((PAIR, dh, TQ), lambda hp, i: (hp, 0, cur(i))), blk(0, done), blk(0, done), bs),
        scratch_shapes=[pltpu.VMEM((2 * TQ, PAIR * dh), F32), pltpu.VMEM((2 * TQ, PAIR * dh), F32),
                        pltpu.VMEM((TQ, PAIR * dh), F32), pltpu.VMEM((TQ, PAIR * dh), F32)],
        compiler_params=_cp("parallel", "arbitrary"),
    )(qkv, qkv, qkv, qkv, qkv, do, lse, delta, bias_t)


def _mod_part(c_all, w_ada, b_ada, name):
    nl, d, n = w_ada.shape
    b = c_all.shape[0]

    def body(c_ref, w_ref, b_ref, o_ref, cond_ref):
        cv = c_ref[...]
        cond = cv * _sigmoid(cv)
        cond_ref[...] = cond
        o_ref[...] = jnp.dot(cond, w_ref[...], preferred_element_type=F32, precision=HI) + b_ref[...]

    return _pcall(
        body, name=name,
        out_shape=(jax.ShapeDtypeStruct((nl, b, n), F32), jax.ShapeDtypeStruct((b, d), F32)),
        grid=(nl,),
        in_specs=[pl.BlockSpec((b, d), lambda l: (0, 0)), pl.BlockSpec((None, d, n), lambda l: (l, 0, 0)),
                  pl.BlockSpec((None, 1, n), lambda l: (l, 0, 0))],
        out_specs=(pl.BlockSpec((None, b, n), lambda l: (l, 0, 0)), pl.BlockSpec((b, d), lambda l: (0, 0))),
        compiler_params=_cp("arbitrary"),
    )(c_all, w_ada, b_ada)


def _adamw(parts, w, m, v, name, tr=1024):
    p, rl, cp = parts[0].shape
    r, c = w.shape
    tr = _tile(rl, tr, 16)
    per = rl // tr
    c1 = 1.0 / (1.0 - ADAM_B1 ** ADAM_STEP)
    c2 = 1.0 / (1.0 - ADAM_B2 ** ADAM_STEP)

    def body(*refs):
        p_refs, (w_ref, m_ref, v_ref, g_ref, d_ref, nm_ref, nv_ref) = refs[:len(parts)], refs[len(parts):]
        for which, p_ref in enumerate(p_refs):
            pl.when(pl.program_id(0) // per == which)(
                functools.partial(update, p_ref, w_ref, m_ref, v_ref, g_ref, d_ref, nm_ref, nv_ref))

    def update(p_ref, w_ref, m_ref, v_ref, g_ref, d_ref, nm_ref, nv_ref):
        g = p_ref[0].astype(F32)
        for i in range(1, p):
            g = g + p_ref[i].astype(F32)
        g = g[:, :c]
        nm = ADAM_B1 * m_ref[...] + (1.0 - ADAM_B1) * g
        nv = ADAM_B2 * v_ref[...] + (1.0 - ADAM_B2) * (g * g)
        g_ref[...] = g
        nm_ref[...] = nm
        nv_ref[...] = nv
        d_ref[...] = -ADAM_LR * ((nm * c1) / (jnp.sqrt(nv * c2) + ADAM_EPS) + ADAM_WD * w_ref[...])

    blk = pl.BlockSpec((tr, c), lambda i: (i, 0))
    out = jax.ShapeDtypeStruct((r, c), F32)
    return _pcall(
        body, name=name, out_shape=(out, out, out, out), grid=(r // tr,),
        in_specs=[pl.BlockSpec((p, tr, cp), lambda i, _w=which: (0, jnp.clip(i - _w * per, 0, per - 1), 0))
                  for which in range(len(parts))] + [blk, blk, blk],
        out_specs=(blk, blk, blk, blk),
        compiler_params=_cp("parallel"),
    )(*parts, w, m, v)


def _pad_lanes(a):
    pad = (-a.shape[-1]) % 128
    return jnp.pad(a, [(0, 0)] * (a.ndim - 1) + [(0, pad)]) if pad else a


def _rows128(a, mult=16):
    flat = a.reshape(-1)
    n = flat.shape[0]
    per = 128 * mult
    pad = (-n) % per
    if pad:
        flat = jnp.concatenate([flat, jnp.zeros((pad,), a.dtype)])
    return flat.reshape(-1, 128)


def _pack(arrs, mult=16):
    pieces = [_rows128(a, mult) for a in arrs]
    return jnp.concatenate(pieces, axis=0), [p.shape[0] for p in pieces]


def _unpack(packed, rows, shapes, lead=()):
    out, at = [], 0
    for r, shp in zip(rows, shapes):
        n = int(np.prod(shp))
        piece = packed[..., at:at + r, :].reshape(lead + (r * 128,))[..., :n]
        out.append(piece.reshape(lead + tuple(shp)))
        at += r
    return out


def _rel_bias_tile(table):
    h = table.shape[0]
    lo = REL_CLIP - (CHUNK - 1)
    n_far = LEFT_CHUNKS * CHUNK + CHUNK - 1 - REL_CLIP
    vec = jnp.concatenate([table[:, lo:2 * REL_CLIP], jnp.repeat(table[:, 2 * REL_CLIP:], n_far + 1, axis=1)], axis=1)
    rev = vec[:, ::-1]
    n_vec = BAND + CHUNK - 1
    skew = jnp.tile(rev, (1, CHUNK + 1))[:, :CHUNK * (n_vec + 1)].reshape(h, CHUNK, n_vec + 1)
    bias = skew[:, ::-1, :BAND]
    neg = jnp.full((h, CHUNK, CHUNK), NEG, F32)
    two = jnp.concatenate([jnp.concatenate([bias, neg], axis=2), jnp.concatenate([neg, bias], axis=2)], axis=1)
    plain = two.transpose(0, 2, 1) * LOG2E
    key = np.arange(SUBK)[:, None]
    return jnp.stack([plain] + [jnp.where(key >= TQ - SUB * j, plain, NEG) for j in range(N_SUB)], axis=1)


def _rel_bias_tile_grad(dbias_t):
    h = dbias_t.shape[0]
    two = dbias_t.transpose(0, 2, 1)
    dbias = two[:, :CHUNK, :BAND] + two[:, CHUNK:, CHUNK:]
    n_vec = BAND + CHUNK - 1
    dskew = jnp.pad(dbias[:, ::-1, :], ((0, 0), (0, 0), (0, n_vec + 1 - BAND))).reshape(h, CHUNK * (n_vec + 1))
    dskew = jnp.pad(dskew, ((0, 0), (0, (CHUNK + 1) * n_vec - CHUNK * (n_vec + 1))))
    drev = jnp.sum(dskew.reshape(h, CHUNK + 1, n_vec), axis=1)
    dvec = drev[:, ::-1]
    lo = REL_CLIP - (CHUNK - 1)
    n_near = 2 * REL_CLIP - lo
    return jnp.concatenate([jnp.zeros((h, lo), F32), dvec[:, :n_near],
                            jnp.sum(dvec[:, n_near:], axis=1, keepdims=True)], axis=1)


BIG = ("w_in", "w_br_fox", "w_br_chunk", "w_out", "w_up", "w_down")
SMALL = ("b_f", "rel_bias", "conv_b", "b_ada", "ln1_g", "ln1_b", "ln2_g", "ln2_b")
ORDER = ("w_in", "b_f", "rel_bias", "w_br_fox", "w_br_chunk", "w_out", "w_up", "conv_w", "conv_b", "w_down",
         "w_ada", "b_ada", "ln1_g", "ln1_b", "ln2_g", "ln2_b")


def _dest_major(g, axis):
    shp = g.shape
    g = g.reshape(shp[:axis] + (N_DEV, shp[axis] // N_DEV) + shp[axis + 1:])
    return jnp.moveaxis(g, axis, 0)


EARLY = ("w_down", "w_up", "w_out", "w_br_fox", "w_br_chunk")
LATE = ("w_in",)
REST = ("w_br_fox", "w_br_chunk", "w_out", "w_up", "w_down")


def _shard_axis(n):
    return 0 if n in ("w_out", "w_down") else 1


def _slab(g, n):
    return _pad_lanes(_dest_major(g, _shard_axis(n)).astype(BF16))


def _layer_fwd(x, h1, wl, rest_of, mod, next_mod, alpha, gather):
    sh1, sc1, g1, sh2, sc2, g2 = mod
    s = x.shape[0]
    qkv = _mm(h1, wl["w_qkv"], out_dtype=BF16, name="proj_qkv", tk=1024)
    gates = _mm(h1, wl["w_gates"], out_dtype=F32, name="proj_gates", tk=1024)
    f_t = _mm(wl["w_f_t"], h1, out_dtype=F32, name="proj_f", tb=True, tk=1024)[:N_HEADS]
    f_t = f_t.reshape(N_HEADS, s // 128, 128)
    b_f = jnp.broadcast_to(wl["b_f"].reshape(N_HEADS, 1, 1), (N_HEADS, 1, 128))
    cum = _forget_fwd(f_t, b_f, "forget_fwd").reshape(N_SPLIT, N_HEADS, s)
    cum_cols = cum.transpose(2, 1, 0).astype(BF16)

    ones = jnp.ones((s, N_HEADS, N_SPLIT), BF16)
    zeros = jnp.zeros((s, N_HEADS, AUG - HEAD_DIM - 2 * N_SPLIT), BF16)
    q_a = qkv[:, :WIDTH].reshape(s, N_HEADS, HEAD_DIM) * QK_SCALE2
    k_a = qkv[:, WIDTH:2 * WIDTH].reshape(s, N_HEADS, HEAD_DIM)
    q_aug = jnp.concatenate([q_a, cum_cols, ones, zeros], axis=-1).reshape(s, N_HEADS * AUG)
    k_aug = jnp.concatenate([k_a, ones, -cum_cols, zeros], axis=-1).reshape(s, N_HEADS * AUG)
    o_a_t, lse_a, *gathered = _fox_fwd(q_aug, k_aug, qkv, "fox_fwd_gather_%d" % len(gather), t=FOX_FWD_T,
                                       exchange=(False, gather))
    wl.update(rest_of(gathered[:len(REST)]))
    gathered = gathered[len(REST):]
    bias_t = _rel_bias_tile(wl["rel_bias"])
    o_c_t, lse_c = _chunk_fwd(qkv, bias_t, "chunk_fwd")
    o_a = o_a_t.transpose(2, 0, 1).reshape(s, WIDTH).astype(BF16)
    o_c = o_c_t.transpose(2, 0, 1).reshape(s, WIDTH).astype(BF16)

    merged = _merge_fwd(o_a, o_c, wl["w_br_fox"], wl["w_br_chunk"], gates, "merge_fwd")
    mix, x1, h2 = _mm_res_ln(merged, wl["w_out"], x, g1, wl["ln1_g"], wl["ln1_b"], sc2, sh2, alpha, "out_ln1")
    u0 = _mm(h2, wl["w_up"], out_dtype=F32, name="ffn_up", tk=1024, tn=1408, o_halves=True)
    act = _conv_act_fwd(u0, wl["conv_w"], wl["conv_b"], "conv_act_fwd")
    y2, x2, h_next = _mm_res_ln(act, wl["w_down"], x1, g2, wl["ln2_g"], wl["ln2_b"], next_mod[0], next_mod[1],
                                alpha, "down_ln2", tk=1408)
    saved = dict(x=x, h1=h1, gates=gates, f_t=f_t, b_f=b_f, q_aug=q_aug, k_aug=k_aug, qkv=qkv,
                 bias_t=bias_t, o_a_t=o_a_t, o_c_t=o_c_t, lse_a=lse_a, lse_c=lse_c,
                 o_a=o_a, o_c=o_c, merged=merged, mix=mix, x1=x1, h2=h2, u0=u0, act=act, y2=y2)
    return x2, h_next, saved, gathered


def _layer_bwd(dx2, sv, wl, mod, alpha, send, ride_late):
    sh1, sc1, g1, sh2, sc2, g2 = mod
    s = dx2.shape[0]
    g = {}
    dres2, dy2, sums = _ln_res_bwd(dx2, sv["x1"], sv["y2"], g2, wl["ln2_g"], alpha, "ln2_bwd")
    g["ln2_g"], g["ln2_b"], dg2 = sums[0], sums[1], sums[2]
    dact = _mm(dy2, wl["w_down_t"], out_dtype=F32, name="dact", tk=1024, tn=1408)
    g["w_down"] = _mm(sv["act"], dy2, out_dtype=F32, name="g_w_down", ta=True, tm=1408, tk=2048)
    du0, csum = _conv_act_bwd(dact, sv["u0"], wl["conv_w"], wl["conv_b"], "conv_act_bwd")
    g["conv_w"] = jnp.concatenate([csum[0, 0:3], csum[1, 0:3]], axis=1)
    g["conv_b"] = jnp.concatenate([csum[0, 3], csum[1, 3]])
    dh2 = _mm(du0, wl["w_up_t"], out_dtype=F32, name="dh2", tk=2816, a_halves=True)
    g["w_up"] = _mm(sv["h2"], du0, out_dtype=F32, name="g_w_up", ta=True, tk=2048, tn=1408, b_halves=True)
    dx1, sums = _ln_mod_bwd(dh2, sv["x1"], sc2, dres2, "ln_mod2_bwd")
    dsh2, dsc2 = sums[0], sums[1]

    dres1, dmix, sums = _ln_res_bwd(dx1, sv["x"], sv["mix"], g1, wl["ln1_g"], alpha, "ln1_bwd")
    g["ln1_g"], g["ln1_b"], dg1 = sums[0], sums[1], sums[2]
    g["w_out"] = _mm(sv["merged"], dmix, out_dtype=F32, name="g_w_out", ta=True, tk=2048)
    dba, dbc, do_a, do_c, dgates = _merge_bwd(
        dmix, wl["w_out_t"], sv["o_a"], sv["o_c"], wl["w_br_fox"], wl["w_br_chunk"],
        wl["w_br_fox_t"], wl["w_br_chunk_t"], sv["gates"], "merge_bwd")
    g["w_br_fox"] = _mm(sv["o_a"], dba, out_dtype=F32, name="g_w_br_fox", ta=True, tk=2048)
    g["w_br_chunk"] = _mm(sv["o_c"], dbc, out_dtype=F32, name="g_w_br_chunk", ta=True, tk=2048)
    n_t = s // min(TQ, s)
    stats = lambda a: a.reshape(N_HEADS, n_t, 1, s // n_t)

    delta_a = _row_dot(do_a.T.reshape(N_HEADS, HEAD_DIM, s), sv["o_a_t"], "delta_fox")
    send = list(send) + [_slab(g[n], n) for n in EARLY]
    dk_a, dv_a, dq_a_t, dsum, dqsum, *landed = _fox_bwd(sv["q_aug"], sv["k_aug"], sv["qkv"], do_a, stats(sv["lse_a"]),
                                                        stats(delta_a), "fox_bwd_scatter_%d" % len(send),
                                                        exchange=(True, send))
    dq_a = (dq_a_t * QK_SCALE).transpose(1, 3, 0, 2).reshape(s, WIDTH).astype(BF16)
    d_cum = dqsum.reshape(N_HEADS, s) - jnp.sum(dsum.reshape(s, N_HEADS, 128), axis=-1).T
    df_t, db_f = _forget_bwd(d_cum.reshape(N_HEADS, s // 128, 128), sv["f_t"], sv["b_f"], "forget_bwd")
    g["b_f"] = db_f[:, 0, 0]
    df_t = df_t.reshape(N_HEADS, s)

    delta_c = _row_dot(do_c.T.reshape(N_HEADS, HEAD_DIM, s), sv["o_c_t"], "delta_chunk")
    dq_c_t, dk_c, dv_c, dbias_t = _chunk_bwd(sv["qkv"], do_c, sv["lse_c"], stats(delta_c), sv["bias_t"], "chunk_bwd")
    g["rel_bias"] = _rel_bias_tile_grad(dbias_t)
    dq_c = dq_c_t.transpose(2, 0, 1).reshape(s, WIDTH).astype(BF16)

    dqkv = jnp.concatenate([dq_a, dk_a, dv_a, dq_c, dk_c, dv_c], axis=1)
    df_pad = jnp.zeros((16 - N_HEADS, s), F32)
    df16 = jnp.concatenate([df_t, df_pad], axis=0).astype(BF16)
    df_cols = jnp.concatenate([df16.T, jnp.zeros((s, AUG - 16), BF16)], axis=1)

    g_qkv = _mm(sv["h1"], dqkv, out_dtype=F32, name="g_w_qkv", ta=True, tk=2048)
    g_gates = _mm(sv["h1"], dgates, out_dtype=F32, name="g_w_gates", ta=True, tk=2048)
    g_f_t = _mm(df16, sv["h1"], out_dtype=F32, name="g_w_f", tk=1024)[:N_HEADS]
    g["w_in"] = jnp.concatenate([g_qkv[:, :3 * WIDTH], g_f_t.T, g_qkv[:, 3 * WIDTH:], g_gates], axis=1)
    late = [_slab(g[n], n) for n in LATE] if ride_late else None
    dh1, *landed_late = _mm(dqkv, wl["w_qkv_t"], out_dtype=F32, name="dh1_qkv_scatter", tk=1536,
                            exchange=(True, late)) if ride_late else \
        [_mm(dqkv, wl["w_qkv_t"], out_dtype=F32, name="dh1_qkv", tk=1536)]
    dh1g = _mm(dgates, wl["w_gates_t"], out_dtype=F32, name="dh1_gates", tk=2048)
    dx, sums = _ln_mod_bwd(dh1, sv["x"], sc1, dres1, "ln_mod1_bwd", extra=(dh1g, df_cols, wl["w_f_pad"]))
    dsh1, dsc1 = sums[0], sums[1]
    g["mod"] = jnp.concatenate([dsh1, dsc1, dg1, dsh2, dsc2, dg2])
    return dx, g, landed, landed_late


def kernel(x, c, w_in, b_f, rel_bias, w_br_fox, w_br_chunk, w_out, w_up, conv_w, conv_b, w_down, w_ada, b_ada, ln1_g, ln1_b, ln2_g, ln2_b, loss_target, m_w_in, m_b_f, m_rel_bias, m_w_br_fox, m_w_br_chunk, m_w_out, m_w_up, m_conv_w, m_conv_b, m_w_down, m_w_ada, m_b_ada, m_ln1_g, m_ln1_b, m_ln2_g, m_ln2_b, v_w_in, v_b_f, v_rel_bias, v_w_br_fox, v_w_br_chunk, v_w_out, v_w_up, v_conv_w, v_conv_b, v_w_down, v_w_ada, v_b_ada, v_ln1_g, v_ln1_b, v_ln2_g, v_ln2_b):
    w = dict(w_in=w_in, b_f=b_f, rel_bias=rel_bias, w_br_fox=w_br_fox, w_br_chunk=w_br_chunk, w_out=w_out,
             w_up=w_up, conv_w=conv_w, conv_b=conv_b, w_down=w_down, w_ada=w_ada, b_ada=b_ada,
             ln1_g=ln1_g, ln1_b=ln1_b, ln2_g=ln2_g, ln2_b=ln2_b)
    m = dict(w_in=m_w_in, b_f=m_b_f, rel_bias=m_rel_bias, w_br_fox=m_w_br_fox, w_br_chunk=m_w_br_chunk,
             w_out=m_w_out, w_up=m_w_up, conv_w=m_conv_w, conv_b=m_conv_b, w_down=m_w_down, w_ada=m_w_ada,
             b_ada=m_b_ada, ln1_g=m_ln1_g, ln1_b=m_ln1_b, ln2_g=m_ln2_g, ln2_b=m_ln2_b)
    v = dict(w_in=v_w_in, b_f=v_b_f, rel_bias=v_rel_bias, w_br_fox=v_w_br_fox, w_br_chunk=v_w_br_chunk,
             w_out=v_w_out, w_up=v_w_up, conv_w=v_conv_w, conv_b=v_conv_b, w_down=v_w_down, w_ada=v_w_ada,
             b_ada=v_b_ada, ln1_g=v_ln1_g, ln1_b=v_ln1_b, ln2_g=v_ln2_g, ln2_b=v_ln2_b)
    depth, d, _ = w_in.shape
    s = x.shape[1]
    alpha = (2.0 * depth) ** 0.25
    me = 4 * lax.axis_index("x") + 2 * lax.axis_index("y") + lax.axis_index("c")
    x0 = x.reshape(s, d)
    target = loss_target.reshape(s, d)

    small_in, small_rows = _pack([c, conv_w], mult=8)
    small_all = _all_gather([small_in], "gather_c_conv")[0]
    c_all, conv_w_all = _unpack(small_all, small_rows, [c.shape, conv_w.shape], lead=(N_DEV,))
    c_all = c_all.reshape(N_DEV, d)
    conv_w_full = conv_w_all.transpose(1, 2, 0, 3).reshape(depth, conv_w.shape[1], -1)
    n_ada = w_ada.shape[2]
    b_ada_mine = lax.dynamic_slice_in_dim(b_ada, me * n_ada, n_ada, axis=1).reshape(depth, 1, n_ada)
    mod_part, cond_all = _mod_part(c_all, w_ada, b_ada_mine, "mod_part")
    mod_all = _all_gather([mod_part.reshape(depth * N_DEV, n_ada)], "gather_mod")[0]
    mod_all = mod_all.reshape(N_DEV, depth, N_DEV, n_ada)
    mod_mine = lax.dynamic_index_in_dim(mod_all, me, axis=2, keepdims=False)
    mod_mine = mod_mine.transpose(1, 0, 2).reshape(depth, N_MOD, 1, d)

    shard = lambda n, l: _pad_lanes(w[n][l].astype(BF16))
    cols = np.cumsum([0, WIDTH, WIDTH, WIDTH, N_HEADS, WIDTH, WIDTH, WIDTH, d, d])

    def whole(n, p):
        rows, ncol = w[n].shape[1:]
        p = jnp.moveaxis(p[..., :ncol], 0, _shard_axis(n))
        return p.reshape((rows * N_DEV, ncol) if _shard_axis(n) == 0 else (rows, ncol * N_DEV))

    def first_weights(l, w_in_parts):
        wi = whole("w_in", w_in_parts)
        w_qkv = jnp.concatenate([wi[:, :cols[3]], wi[:, cols[4]:cols[7]]], axis=1)
        w_gates = wi[:, cols[7]:]
        w_f_t = jnp.concatenate([wi[:, cols[3]:cols[4]].T, jnp.zeros((16 - N_HEADS, d), BF16)], axis=0)
        w_f_pad = jnp.concatenate([w_f_t, jnp.zeros((AUG - 16, d), BF16)], axis=0)
        row = lambda a: a[l].reshape(1, -1)
        return dict(
            w_qkv=w_qkv, w_gates=w_gates, w_f_t=w_f_t, w_f_pad=w_f_pad, w_qkv_t=w_qkv.T, w_gates_t=w_gates.T,
            conv_w=conv_w_full[l].reshape(3, 2, -1).transpose(1, 0, 2), conv_b=conv_b[l].reshape(2, 1, -1),
            b_f=b_f[l], rel_bias=rel_bias[l],
            ln1_g=row(ln1_g), ln1_b=row(ln1_b), ln2_g=row(ln2_g), ln2_b=row(ln2_b))

    def rest_weights(parts):
        out = {}
        for n, p in zip(REST, parts):
            out[n] = whole(n, p)
            out[n + "_t"] = out[n].T
        return out

    xs, saved, layers = x0, [], []
    hs = _ln_mod(x0, mod_mine[0][1], mod_mine[0][0], "ln_mod1")
    w_in_parts = _all_gather([shard("w_in", 0)], "gather_weights")[0]
    for l in range(depth):
        layers.append(first_weights(l, w_in_parts))
        nxt = mod_mine[min(l + 1, depth - 1)]
        gather = [shard(n, l) for n in REST] + ([shard("w_in", l + 1)] if l + 1 < depth else [])
        xs, hs, sv, extra = _layer_fwd(xs, hs, layers[l], rest_weights, list(mod_mine[l]), (nxt[1], nxt[0]), alpha,
                                       gather)
        w_in_parts = extra[0] if extra else None
        saved.append(sv)
    dxs, loss_part = _loss_grad(xs, target, "loss_grad")
    loss = lax.psum(loss_part[0, 0], AXES)
    grads, big_parts, send = [None] * depth, {}, []
    for l in reversed(range(depth)):
        dxs, grads[l], landed, landed_late = _layer_bwd(dxs, saved[l], layers[l], list(mod_mine[l]), alpha, send,
                                                        ride_late=(l == 0))
        for n, part in zip([(n, l + 1) for n in LATE if send] + [(n, l) for n in EARLY], landed):
            big_parts[n] = part
        for n, part in zip(LATE, landed_late):
            big_parts[n, l] = part
        send = [_slab(grads[l][n], n) for n in LATE] if l else []
    grad_x = dxs.reshape(1, s, d)
    stack = lambda n: jnp.stack([grads[l][n] for l in range(depth)])

    rep = ["b_ada"] + [n for n in SMALL if n != "b_ada"]
    small_g = [stack("mod")] + [stack(n) for n in rep[1:]] + [stack("conv_w")]
    small_pack, small_g_rows = _pack(small_g, mult=8)
    small_parts = _all_gather([small_pack], "gather_small_grads")[0]
    n_rep_rows = sum(small_g_rows[:-1])
    dmod_all = _unpack(small_parts, small_g_rows[:1], [(depth, N_MOD * d)], lead=(N_DEV,))[0]
    cw_all = _unpack(small_parts[:, n_rep_rows:], small_g_rows[-1:], [small_g[-1].shape], lead=(N_DEV,))[0]

    out = {}
    for n in BIG:
        rows2d = lambda a: a.reshape(-1, a.shape[-1])
        res = _adamw([big_parts[n, l] for l in range(depth)], rows2d(w[n]), rows2d(m[n]), rows2d(v[n]),
                     "adamw_" + n, tr=256)
        for kind, r in zip(("grad", "delta", "new_m", "new_v"), res):
            out[kind, n] = r.reshape(w[n].shape)

    dmod_mine = lax.dynamic_slice_in_dim(dmod_all, me * n_ada, n_ada, axis=2)
    g_ada = jnp.stack([_mm(cond_all.T, dmod_mine[:, l], out_dtype=F32, name="g_w_ada", tk=N_DEV, tn=n_ada,
                           precision=HI) for l in range(depth)])
    lp = [_rows128(t["w_ada"], 8) for t in (w, m, v)]
    res = _adamw([_rows128(g_ada, 8)[None]], *lp, "adamw_ada")
    for kind, r in zip(("grad", "delta", "new_m", "new_v"), res):
        out[kind, "w_ada"] = _unpack(r, [lp[0].shape[0]], [w_ada.shape])[0]

    rp = [_pack([t[n] for n in rep], mult=8)[0] for t in (w, m, v)]
    res = _adamw([small_parts[:, :n_rep_rows]], *rp, "adamw_small")
    for kind, r in zip(("grad", "delta", "new_m", "new_v"), res):
        for n, a in zip(rep, _unpack(r, small_g_rows[:-1], [w[n].shape for n in rep])):
            out[kind, n] = a

    n_cw = conv_w.shape[2]
    cw_mine = lax.dynamic_slice_in_dim(cw_all, me * n_cw, n_cw, axis=3)
    cp = [_rows128(t["conv_w"], 8) for t in (w, m, v)]
    cw_in = jnp.stack([_rows128(cw_mine[i], 8) for i in range(N_DEV)])
    res = _adamw([cw_in], *cp, "adamw_conv_w")
    for kind, r in zip(("grad", "delta", "new_m", "new_v"), res):
        out[kind, "conv_w"] = _unpack(r, [cp[0].shape[0]], [conv_w.shape])[0]

    result = [loss, grad_x]
    for kind in ("grad", "delta", "new_m", "new_v"):
        result += [out[kind, n] for n in ORDER]
    return tuple(result)
```

```python
import functools
import math

import jax
import jax.numpy as jnp
import numpy as np
from jax import lax
from jax.experimental import pallas as pl
from jax.experimental.pallas import tpu as pltpu

F32 = jnp.float32
BF16 = jnp.bfloat16
HI = lax.Precision.HIGHEST
MESH_ID = pl.DeviceIdType.MESH
AXES = ("x", "y", "c")
N_DEV = 8

N_HEADS = 8
HEAD_DIM = 64
WIDTH = N_HEADS * HEAD_DIM
CHUNK = 64
LEFT_CHUNKS = 8
BAND = (LEFT_CHUNKS + 1) * CHUNK
REL_CLIP = 128
LN_EPS = 1e-5
N_MOD = 6
QK_SCALE = 1.0 / math.sqrt(HEAD_DIM)
LOG2E = math.log2(math.e)
LN2 = math.log(2.0)
QK_SCALE2 = QK_SCALE * LOG2E
NEG = -1e30
AUG = 128
N_SPLIT = 3
TQ = 512
FOX_FWD_T = 1024
FOX_BWD_T = 1024
SUB = 2 * CHUNK
SUBK = BAND + CHUNK

ADAM_LR, ADAM_B1, ADAM_B2, ADAM_EPS, ADAM_WD, ADAM_STEP = 0.001, 0.9, 0.999, 1e-08, 0.01, 10

VMEM_LIMIT = 56 * 2 ** 20


def _pcall(body, **kw):
    return pl.pallas_call(body, **kw)


def _cp(*sem):
    return pltpu.CompilerParams(dimension_semantics=sem if sem else None, vmem_limit_bytes=VMEM_LIMIT)


def _my_place():
    return lax.axis_index("x"), lax.axis_index("y"), lax.axis_index("c")


N_PEER = N_DEV - 1


def _comm_call(body, xs, out_shapes, name):
    n = len(xs)
    hbm = pl.BlockSpec(memory_space=pltpu.HBM)
    return _pcall(
        functools.partial(body, n), name=name, out_shape=out_shapes, in_specs=[hbm] * n, out_specs=[hbm] * n,
        scratch_shapes=[pltpu.SemaphoreType.DMA((N_PEER * n,)), pltpu.SemaphoreType.DMA((N_PEER * n,)),
                        pltpu.SemaphoreType.DMA((n,))],
    )(*xs)


def _all_gather(xs, name):
    def body(n, *refs):
        x_refs, out_refs, (send_sems, recv_sems, local_sems) = refs[:n], refs[n:2 * n], refs[2 * n:]
        mx, my, mc = _my_place()
        me, sibling = (mx, my, mc), (mx, my, 1 - mc)
        chips = [(1 - mx, my), (mx, 1 - my), (1 - mx, 1 - my)]

        def slot(a, px, py, pc):
            return out_refs[a].at[4 * px + 2 * py + pc]

        def copy(a, k, block, to, own=False):
            return pltpu.make_async_remote_copy(
                src_ref=x_refs[a] if own else slot(a, *block), dst_ref=slot(a, *block),
                send_sem=send_sems.at[a * N_PEER + k], recv_sem=recv_sems.at[a * N_PEER + k],
                device_id=to, device_id_type=MESH_ID)

        arrays = range(n)
        mine = [pltpu.make_async_copy(x_refs[a], slot(a, *me), local_sems.at[a]) for a in arrays]
        first = [copy(a, 0, me, sibling, own=True) for a in arrays]
        first += [copy(a, 1 + j, me, (*chip, mc), own=True) for a in arrays for j, chip in enumerate(chips)]
        for cp in mine + first:
            cp.start()
        passed = []
        for a in arrays:
            for j, chip in enumerate(chips):
                copy(a, 1 + j, (*chip, mc), me).wait_recv()
                passed.append(copy(a, 4 + j, (*chip, mc), sibling))
                passed[-1].start()
        for a in arrays:
            copy(a, 0, sibling, me).wait_recv()
            for j, chip in enumerate(chips):
                copy(a, 4 + j, (*chip, 1 - mc), me).wait_recv()
        for cp in first + passed:
            cp.wait_send()
        for cp in mine:
            cp.wait()

    return _comm_call(body, xs, [jax.ShapeDtypeStruct((N_DEV,) + x.shape, x.dtype) for x in xs], name)


def _direct_exchange(scatter, x_refs, out_refs, send_sems, recv_sems, local_sems):
    mx, my, mc = _my_place()
    me = 4 * mx + 2 * my + mc
    mine, copies = [], []
    for a, (x_ref, out_ref) in enumerate(zip(x_refs, out_refs)):
        mine.append(pltpu.make_async_copy(x_ref.at[me] if scatter else x_ref, out_ref.at[me], local_sems.at[a]))
        for k in range(N_PEER):
            px, py, pc = mx ^ ((k + 1) >> 2), my ^ (((k + 1) >> 1) & 1), mc ^ ((k + 1) & 1)
            copies.append(pltpu.make_async_remote_copy(
                src_ref=x_ref.at[4 * px + 2 * py + pc] if scatter else x_ref, dst_ref=out_ref.at[me],
                send_sem=send_sems.at[a * N_PEER + k], recv_sem=recv_sems.at[a * N_PEER + k],
                device_id=(px, py, pc), device_id_type=MESH_ID))

    def start():
        for cp in mine + copies:
            cp.start()

    def wait():
        for cp in copies:
            cp.wait_recv()
        for cp in copies:
            cp.wait_send()
        for cp in mine:
            cp.wait()

    return start, wait


def _exchange_shapes(scatter, xs):
    return [jax.ShapeDtypeStruct(x.shape if scatter else (N_DEV,) + x.shape, x.dtype) for x in xs]


def _exchange_scratch(n):
    return [pltpu.SemaphoreType.DMA((N_PEER * n,)), pltpu.SemaphoreType.DMA((N_PEER * n,)),
            pltpu.SemaphoreType.DMA((n,))]


def _tile(n, pref, unit=128):
    if n <= pref:
        return n
    t = pref - pref % unit
    while t > unit and n % t:
        t -= unit
    assert n % t == 0, (n, pref, unit)
    return t


def _mm(a, b, *, out_dtype, name, ta=False, tb=False, tm=1024, tn=1024, tk=512, precision=None,
        a_halves=False, b_halves=False, o_halves=False, exchange=None):
    a_shape = (a.shape[1], 2 * a.shape[2]) if a_halves else a.shape
    b_shape = (b.shape[1], 2 * b.shape[2]) if b_halves else b.shape
    (k_dim, m) = a_shape if ta else a_shape[::-1]
    n = b_shape[0] if tb else b_shape[1]
    tm, tn = _tile(m, tm, 8), _tile(n // 2 if (b_halves or o_halves) else n, tn)
    tk = _tile(k_dim // 2 if a_halves else k_dim, tk)
    nk = k_dim // tk
    dims = (((0 if ta else 1,), (1 if tb else 0,)), ((), ()))
    grid = (m // tm, n // tn, nk)
    ex_in, ex_out, ex_specs, ex_scratch, hook = _riding_exchange(exchange, 2, 1, grid)

    def body(*refs):
        (a_ref, b_ref, o_ref, acc_ref), finish = hook(refs)
        k = pl.program_id(2)
        if nk == 1:
            o_ref[...] = lax.dot_general(a_ref[...], b_ref[...], dims, preferred_element_type=F32,
                                         precision=precision).astype(o_ref.dtype)
            finish()
            return

        @pl.when(k == 0)
        def _():
            acc_ref[...] = jnp.zeros_like(acc_ref)

        acc_ref[...] += lax.dot_general(a_ref[...], b_ref[...], dims, preferred_element_type=F32,
                                        precision=precision)

        @pl.when(k == nk - 1)
        def _():
            o_ref[...] = acc_ref[...].astype(o_ref.dtype)

        finish()

    def spec(shape2, pick, halves, per_half):
        if not halves:
            return pl.BlockSpec(shape2, pick)

        def index(i, j, k):
            r, c = pick(i, j, k)
            return (c // per_half, r, c % per_half)
        return pl.BlockSpec((None,) + shape2, index)

    assert not (ta and a_halves) and not (tb and b_halves)
    a_spec = spec((tk, tm), lambda i, j, k: (k, i), False, 0) if ta else \
        spec((tm, tk), lambda i, j, k: (i, k), a_halves, (k_dim // 2) // tk if a_halves else 0)
    b_spec = spec((tn, tk), lambda i, j, k: (j, k), False, 0) if tb else \
        spec((tk, tn), lambda i, j, k: (k, j), b_halves, (n // 2) // tn if b_halves else 0)
    o_spec = spec((tm, tn), lambda i, j, k: (i, j), o_halves, (n // 2) // tn if o_halves else 0)
    out_shape = (2, m, n // 2) if o_halves else (m, n)
    res = _pcall(
        body, name=name,
        out_shape=[jax.ShapeDtypeStruct(out_shape, out_dtype)] + ex_out,
        grid=grid,
        in_specs=[a_spec, b_spec] + ex_specs,
        out_specs=[o_spec] + ex_specs,
        scratch_shapes=[pltpu.VMEM((tm, tn), F32)] + ex_scratch,
        compiler_params=_cp(*(["arbitrary"] * 3 if exchange else ["parallel", "parallel", "arbitrary"])),
    )(a, b, *ex_in)
    return res if exchange else res[0]


def _ln_stats(x):
    mu = jnp.mean(x, axis=-1, keepdims=True)
    xc = x - mu
    var = jnp.mean(xc * xc, axis=-1, keepdims=True)
    rstd = lax.rsqrt(var + LN_EPS)
    return xc * rstd, rstd


def _ln_mod(x, sc, sh, name, tm=512):
    s, d = x.shape
    tm = min(tm, s)

    def body(x_ref, sc_ref, sh_ref, h_ref):
        xhat, _ = _ln_stats(x_ref[...])
        h_ref[...] = (xhat * (1.0 + sc_ref[...]) + sh_ref[...]).astype(h_ref.dtype)

    row = pl.BlockSpec((1, d), lambda i: (0, 0))
    return _pcall(
        body, name=name, out_shape=jax.ShapeDtypeStruct((s, d), BF16), grid=(s // tm,),
        in_specs=[pl.BlockSpec((tm, d), lambda i: (i, 0)), row, row],
        out_specs=pl.BlockSpec((tm, d), lambda i: (i, 0)),
        compiler_params=_cp("parallel"),
    )(x, sc, sh)


def _mm_res_ln(a, w, xres, gate, ln_g, ln_b, sc, sh, alpha, name, tm=512, tk=512):
    s, k_dim = a.shape
    d = w.shape[1]
    tm, tk = _tile(s, tm, 8), _tile(k_dim, tk)
    nk = k_dim // tk

    def body(a_ref, w_ref, x_ref, g_ref, lg_ref, lb_ref, sc_ref, sh_ref, y_ref, xn_ref, h_ref, acc_ref):
        k = pl.program_id(1)

        @pl.when(k == 0)
        def _():
            acc_ref[...] = jnp.zeros_like(acc_ref)

        acc_ref[...] += jnp.dot(a_ref[...], w_ref[...], preferred_element_type=F32)

        @pl.when(k == nk - 1)
        def _():
            y = acc_ref[...]
            y_ref[...] = y
            z = alpha * x_ref[...] + (1.0 + g_ref[...]) * y
            zhat, _ = _ln_stats(z)
            xn = zhat * lg_ref[...] + lb_ref[...]
            xn_ref[...] = xn
            xhat, _ = _ln_stats(xn)
            h_ref[...] = (xhat * (1.0 + sc_ref[...]) + sh_ref[...]).astype(h_ref.dtype)

    row = pl.BlockSpec((1, d), lambda i, k: (0, 0))
    tile = pl.BlockSpec((tm, d), lambda i, k: (i, 0))
    return _pcall(
        body, name=name,
        out_shape=(jax.ShapeDtypeStruct((s, d), F32), jax.ShapeDtypeStruct((s, d), F32),
                   jax.ShapeDtypeStruct((s, d), BF16)),
        grid=(s // tm, nk),
        in_specs=[pl.BlockSpec((tm, tk), lambda i, k: (i, k)), pl.BlockSpec((tk, d), lambda i, k: (k, 0)),
                  tile, row, row, row, row, row],
        out_specs=(tile, tile, tile),
        scratch_shapes=[pltpu.VMEM((tm, d), F32)],
        compiler_params=_cp("parallel", "arbitrary"),
    )(a, w, xres, gate, ln_g, ln_b, sc, sh)


def _colsum(v):
    return jnp.sum(v, axis=0, keepdims=True)


def _ln_res_bwd(dxn, xres, y, gate, ln_g, alpha, name, tm=256):
    s, d = dxn.shape
    tm = min(tm, s)

    def body(dxn_ref, x_ref, y_ref, g_ref, lg_ref, dres_ref, dy_ref, sums_ref):
        @pl.when(pl.program_id(0) == 0)
        def _():
            sums_ref[...] = jnp.zeros_like(sums_ref)

        dxn_v, y_v = dxn_ref[...], y_ref[...]
        one_g = 1.0 + g_ref[...]
        zhat, rstd = _ln_stats(alpha * x_ref[...] + one_g * y_v)
        dzh = dxn_v * lg_ref[...]
        dz = rstd * (dzh - jnp.mean(dzh, axis=-1, keepdims=True)
                     - zhat * jnp.mean(dzh * zhat, axis=-1, keepdims=True))
        dres_ref[...] = alpha * dz
        dy_ref[...] = (one_g * dz).astype(dy_ref.dtype)
        sums_ref[0:1, :] += _colsum(dxn_v * zhat)
        sums_ref[1:2, :] += _colsum(dxn_v)
        sums_ref[2:3, :] += _colsum(dz * y_v)

    row = pl.BlockSpec((1, d), lambda i: (0, 0))
    tile = pl.BlockSpec((tm, d), lambda i: (i, 0))
    return _pcall(
        body, name=name,
        out_shape=(jax.ShapeDtypeStruct((s, d), F32), jax.ShapeDtypeStruct((s, d), BF16),
                   jax.ShapeDtypeStruct((8, d), F32)),
        grid=(s // tm,),
        in_specs=[tile, tile, tile, row, row],
        out_specs=(tile, tile, pl.BlockSpec((8, d), lambda i: (0, 0))),
        compiler_params=_cp("arbitrary"),
    )(dxn, xres, y, gate, ln_g)


def _ln_mod_bwd(dh, x, sc, dres, name, extra=None, tm=256):
    s, d = dh.shape
    tm = min(tm, s)

    def body(*refs):
        if extra is None:
            dh_ref, x_ref, sc_ref, dres_ref, dx_ref, sums_ref = refs
            dh_v = dh_ref[...]
        else:
            dh_ref, x_ref, sc_ref, dres_ref, dh2_ref, df_ref, wf_ref, dx_ref, sums_ref = refs
            dh_v = dh_ref[...] + dh2_ref[...] + jnp.dot(df_ref[...], wf_ref[...], preferred_element_type=F32)

        @pl.when(pl.program_id(0) == 0)
        def _():
            sums_ref[...] = jnp.zeros_like(sums_ref)

        xhat, rstd = _ln_stats(x_ref[...])
        dxh = dh_v * (1.0 + sc_ref[...])
        dx_ref[...] = dres_ref[...] + rstd * (dxh - jnp.mean(dxh, axis=-1, keepdims=True)
                                             - xhat * jnp.mean(dxh * xhat, axis=-1, keepdims=True))
        sums_ref[0:1, :] += _colsum(dh_v)
        sums_ref[1:2, :] += _colsum(dh_v * xhat)

    row = pl.BlockSpec((1, d), lambda i: (0, 0))
    tile = pl.BlockSpec((tm, d), lambda i: (i, 0))
    in_specs = [tile, tile, row, tile]
    args = [dh, x, sc, dres]
    if extra is not None:
        in_specs += [tile, pl.BlockSpec((tm, AUG), lambda i: (i, 0)), pl.BlockSpec((AUG, d), lambda i: (0, 0))]
        args += list(extra)
    return _pcall(
        body, name=name,
        out_shape=(jax.ShapeDtypeStruct((s, d), F32), jax.ShapeDtypeStruct((8, d), F32)),
        grid=(s // tm,), in_specs=in_specs,
        out_specs=(tile, pl.BlockSpec((8, d), lambda i: (0, 0))),
        compiler_params=_cp("arbitrary"),
    )(*args)


def _loss_grad(xn, target, name, tm=512):
    s, d = xn.shape
    tm = min(tm, s)
    n = s // tm

    def body(x_ref, t_ref, dx_ref, loss_ref, acc_ref):
        i = pl.program_id(0)

        @pl.when(i == 0)
        def _():
            acc_ref[...] = jnp.zeros_like(acc_ref)

        err = x_ref[...] - t_ref[...]
        dx_ref[...] = err * (1.0 / d)
        acc_ref[...] += _colsum(err * err)

        @pl.when(i == n - 1)
        def _():
            loss_ref[...] = jnp.zeros_like(loss_ref) + (0.5 / d) * jnp.sum(acc_ref[...])

    tile = pl.BlockSpec((tm, d), lambda i: (i, 0))
    return _pcall(
        body, name=name,
        out_shape=(jax.ShapeDtypeStruct((s, d), F32), jax.ShapeDtypeStruct((8, 128), F32)),
        grid=(n,), in_specs=[tile, tile],
        out_specs=(tile, pl.BlockSpec((8, 128), lambda i: (0, 0))),
        scratch_shapes=[pltpu.VMEM((1, d), F32)],
        compiler_params=_cp("arbitrary"),
    )(xn, target)


def _sigmoid(v):
    return 0.5 * jnp.tanh(0.5 * v) + 0.5


def _merge_fwd(o_a, o_c, w_a, w_c, gates, name, tm=256):
    s, wd = o_a.shape
    d = w_a.shape[1]
    tm = min(tm, s)

    def body(oa_ref, oc_ref, wa_ref, wc_ref, ga_ref, gc_ref, m_ref):
        ba = jnp.dot(oa_ref[...], wa_ref[...], preferred_element_type=F32)
        bc = jnp.dot(oc_ref[...], wc_ref[...], preferred_element_type=F32)
        m_ref[...] = (_sigmoid(ga_ref[...]) * ba + _sigmoid(gc_ref[...]) * bc).astype(m_ref.dtype)

    o_spec = pl.BlockSpec((tm, wd), lambda i: (i, 0))
    w_spec = pl.BlockSpec((wd, d), lambda i: (0, 0))
    return _pcall(
        body, name=name, out_shape=jax.ShapeDtypeStruct((s, d), BF16), grid=(s // tm,),
        in_specs=[o_spec, o_spec, w_spec, w_spec,
                  pl.BlockSpec((tm, d), lambda i: (i, 0)), pl.BlockSpec((tm, d), lambda i: (i, 1))],
        out_specs=pl.BlockSpec((tm, d), lambda i: (i, 0)),
        compiler_params=_cp("parallel"),
    )(o_a, o_c, w_a, w_c, gates, gates)


def _merge_bwd(dmix, w_out_t, o_a, o_c, w_a, w_c, w_a_t, w_c_t, gates, name, tm=256):
    s, wd = o_a.shape
    d = w_a.shape[1]
    tm = min(tm, s)

    def body(dmix_ref, wot_ref, oa_ref, oc_ref, wa_ref, wc_ref, wat_ref, wct_ref, ga_ref, gc_ref,
             dba_ref, dbc_ref, doa_ref, doc_ref, dg_ref):
        dm = jnp.dot(dmix_ref[...], wot_ref[...], preferred_element_type=F32)
        for half, (o_ref, w_ref, wt_ref, g_ref, db_ref, do_ref) in enumerate((
                (oa_ref, wa_ref, wat_ref, ga_ref, dba_ref, doa_ref),
                (oc_ref, wc_ref, wct_ref, gc_ref, dbc_ref, doc_ref))):
            b = jnp.dot(o_ref[...], w_ref[...], preferred_element_type=F32)
            sg = _sigmoid(g_ref[...])
            db = (dm * sg).astype(BF16)
            db_ref[...] = db
            dg_ref[:, d * half:d * (half + 1)] = (dm * b * sg * (1.0 - sg)).astype(dg_ref.dtype)
            do_ref[...] = jnp.dot(db, wt_ref[...], preferred_element_type=F32).astype(do_ref.dtype)

    row_d = pl.BlockSpec((tm, d), lambda i: (i, 0))
    row_w = pl.BlockSpec((tm, wd), lambda i: (i, 0))
    full = lambda shp: pl.BlockSpec(shp, lambda i: (0, 0))
    return _pcall(
        body, name=name,
        out_shape=(jax.ShapeDtypeStruct((s, d), BF16), jax.ShapeDtypeStruct((s, d), BF16),
                   jax.ShapeDtypeStruct((s, wd), BF16), jax.ShapeDtypeStruct((s, wd), BF16),
                   jax.ShapeDtypeStruct((s, 2 * d), BF16)),
        grid=(s // tm,),
        in_specs=[row_d, full((d, d)), row_w, row_w, full((wd, d)), full((wd, d)), full((d, wd)), full((d, wd)),
                  row_d, pl.BlockSpec((tm, d), lambda i: (i, 1))],
        out_specs=(row_d, row_d, row_w, row_w, pl.BlockSpec((tm, 2 * d), lambda i: (i, 0))),
        compiler_params=_cp("parallel"),
    )(dmix, w_out_t, o_a, o_c, w_a, w_c, w_a_t, w_c_t, gates, gates)


def _shift_down(cur, k, fill_rows):
    out = pltpu.roll(cur, k, axis=0)
    rid = lax.broadcasted_iota(jnp.int32, cur.shape, 0)
    for r, fill in enumerate(fill_rows):
        out = jnp.where(rid == r, fill, out)
    return out


def _shift_up(cur, k, fill_rows):
    n = cur.shape[0]
    out = pltpu.roll(cur, n - k, axis=0)
    rid = lax.broadcasted_iota(jnp.int32, cur.shape, 0)
    for r, fill in enumerate(fill_rows):
        out = jnp.where(rid == n - k + r, fill, out)
    return out


def _conv_rows(cur, prev8, first, w, b):
    p6 = jnp.where(first, 0.0, prev8[6:7, :])
    p7 = jnp.where(first, 0.0, prev8[7:8, :])
    m1 = _shift_down(cur, 1, [p7])
    m2 = _shift_down(cur, 2, [p6, p7])
    u = b + w[0:1, :] * m2 + w[1:2, :] * m1 + w[2:3, :] * cur
    return u, m1, m2


def _conv_specs(tr, tc, order):
    r8 = tr // 8
    at = lambda f: (lambda *g: f(*order(*g)))
    return [pl.BlockSpec((2, tr, tc), at(lambda i, j: (0, i, j))),
            pl.BlockSpec((2, 8, tc), at(lambda i, j: (0, jnp.maximum(i * r8 - 1, 0), j))),
            pl.BlockSpec((2, 3, tc), at(lambda i, j: (0, 0, j))),
            pl.BlockSpec((2, 1, tc), at(lambda i, j: (0, 0, j)))]


def _conv_act_fwd(u0, conv_w, conv_b, name, tr=512, tc=256):
    _, s, f = u0.shape
    tr, tc = _tile(s, tr, 8), _tile(f, tc)

    def body(u_ref, p_ref, w_ref, b_ref, act_ref):
        first = pl.program_id(0) == 0
        a, _, _ = _conv_rows(u_ref[0], p_ref[0], first, w_ref[0], b_ref[0])
        v, _, _ = _conv_rows(u_ref[1], p_ref[1], first, w_ref[1], b_ref[1])
        act_ref[...] = (a * _sigmoid(a) * v).astype(act_ref.dtype)

    return _pcall(
        body, name=name, out_shape=jax.ShapeDtypeStruct((s, f), BF16), grid=(s // tr, f // tc),
        in_specs=_conv_specs(tr, tc, lambda i, j: (i, j)),
        out_specs=pl.BlockSpec((tr, tc), lambda i, j: (i, j)),
        compiler_params=_cp("parallel", "parallel"),
    )(u0, u0, conv_w, conv_b)


def _act_grads(dact, a, v):
    sg = _sigmoid(a)
    return dact * v * sg * (1.0 + a * (1.0 - sg)), dact * a * sg


def _conv_act_bwd(dact, u0, conv_w, conv_b, name, tr=512, tc=256):
    _, s, f = u0.shape
    tr, tc = _tile(s, tr, 8), _tile(f, tc)
    n = s // tr
    r8 = tr // 8

    def body(dact_ref, dnext_ref, u_ref, p_ref, unext_ref, w_ref, b_ref, du0_ref, sums_ref):
        i = pl.program_id(1)
        first, last = i == 0, i == n - 1

        @pl.when(first)
        def _():
            sums_ref[...] = jnp.zeros_like(sums_ref)

        cur = (u_ref[0], u_ref[1])
        (a, a1, a2), (v, v1, v2) = [_conv_rows(cur[hf], p_ref[hf], first, w_ref[hf], b_ref[hf]) for hf in range(2)]
        da, dv = _act_grads(dact_ref[...], a, v)
        (an, _, _), (vn, _, _) = [_conv_rows(unext_ref[hf], cur[hf][tr - 8:tr, :], False, w_ref[hf], b_ref[hf])
                                  for hf in range(2)]
        dan, dvn = _act_grads(dnext_ref[...], an, vn)
        for hf, g, gn, shifted in ((0, da, dan, (a2, a1, cur[0])), (1, dv, dvn, (v2, v1, cur[1]))):
            w = w_ref[hf]
            n0 = jnp.where(last, 0.0, gn[0:1, :])
            n1 = jnp.where(last, 0.0, gn[1:2, :])
            du0 = w[2:3, :] * g + w[1:2, :] * _shift_up(g, 1, [n0]) + w[0:1, :] * _shift_up(g, 2, [n0, n1])
            du0_ref[hf] = du0.astype(du0_ref.dtype)
            for r in range(3):
                sums_ref[hf, r:r + 1, :] += _colsum(g * shifted[r])
            sums_ref[hf, 3:4, :] += _colsum(g)

    nxt = lambda i: jnp.minimum((i + 1) * r8, s // 8 - 1)
    order = lambda j, i: (i, j)
    specs = _conv_specs(tr, tc, order)
    return _pcall(
        body, name=name,
        out_shape=(jax.ShapeDtypeStruct((2, s, f), BF16), jax.ShapeDtypeStruct((2, 8, f), F32)),
        grid=(f // tc, n),
        in_specs=[pl.BlockSpec((tr, tc), lambda j, i: (i, j)), pl.BlockSpec((8, tc), lambda j, i: (nxt(i), j)),
                  specs[0], specs[1], pl.BlockSpec((2, 8, tc), lambda j, i: (0, nxt(i), j)), specs[2], specs[3]],
        out_specs=(pl.BlockSpec((2, tr, tc), lambda j, i: (0, i, j)), pl.BlockSpec((2, 8, tc), lambda j, i: (0, 0, j))),
        compiler_params=_cp("parallel", "arbitrary"),
    )(dact, dact, u0, u0, u0, conv_w, conv_b)


def _cumsum_rows(v, reverse=False):
    r = v.shape[0]
    i128 = lax.broadcasted_iota(jnp.int32, (128, 128), 0), lax.broadcasted_iota(jnp.int32, (128, 128), 1)
    ir = lax.broadcasted_iota(jnp.int32, (r, r), 0), lax.broadcasted_iota(jnp.int32, (r, r), 1)
    in_row = (i128[0] >= i128[1] if reverse else i128[0] <= i128[1]).astype(F32)
    rows = (ir[1] > ir[0] if reverse else ir[1] < ir[0]).astype(F32)
    within = jnp.dot(v, in_row, preferred_element_type=F32, precision=HI)
    tot = jnp.broadcast_to(within[:, 0:1] if reverse else within[:, 127:128], (r, 128))
    return within + jnp.dot(rows, tot, preferred_element_type=F32, precision=HI)


def _forget_fwd(f_t, b_f, name):
    h, r, _ = f_t.shape

    def body(f_ref, b_ref, o_ref):
        z = f_ref[...] + b_ref[...]
        logf = jnp.minimum(z, 0.0) - jnp.log(1.0 + jnp.exp(-jnp.abs(z)))
        rest = _cumsum_rows(logf) * LOG2E
        for i in range(N_SPLIT):
            piece = rest.astype(BF16).astype(F32)
            o_ref[i] = piece
            rest = rest - piece

    return _pcall(
        body, name=name, out_shape=jax.ShapeDtypeStruct((N_SPLIT, h, r, 128), F32), grid=(h,),
        in_specs=[pl.BlockSpec((None, r, 128), lambda i: (i, 0, 0)), pl.BlockSpec((None, 1, 128), lambda i: (i, 0, 0))],
        out_specs=pl.BlockSpec((N_SPLIT, None, r, 128), lambda i: (0, i, 0, 0)),
        compiler_params=_cp("parallel"),
    )(f_t, b_f)


def _forget_bwd(d_cum, f_t, b_f, name):
    h, r, _ = f_t.shape

    def body(g_ref, f_ref, b_ref, df_ref, db_ref):
        df = _cumsum_rows(g_ref[...], reverse=True) * _sigmoid(-(f_ref[...] + b_ref[...]))
        df_ref[...] = df
        db_ref[...] = jnp.zeros_like(db_ref) + jnp.sum(df)

    blk = pl.BlockSpec((None, r, 128), lambda i: (i, 0, 0))
    one = pl.BlockSpec((None, 1, 128), lambda i: (i, 0, 0))
    return _pcall(
        body, name=name,
        out_shape=(jax.ShapeDtypeStruct((h, r, 128), F32), jax.ShapeDtypeStruct((h, 1, 128), F32)),
        grid=(h,), in_specs=[blk, blk, one], out_specs=(blk, one),
        compiler_params=_cp("parallel"),
    )(d_cum, f_t, b_f)


_NT = (((1,), (1,)), ((), ()))


def _causal_keep(tk, tq):
    return lax.broadcasted_iota(jnp.int32, (tk, tq), 0) <= lax.broadcasted_iota(jnp.int32, (tk, tq), 1)


_TN = (((0,), (0,)), ((), ()))
PAIR = 2
V_FOX = 2 * WIDTH // 128
Q_CHUNK, K_CHUNK, V_CHUNK = (3 * WIDTH) // 128, (4 * WIDTH) // 128, (5 * WIDTH) // 128


def _riding_exchange(exchange, n_in, n_out, grid):
    if exchange is None:
        return [], [], [], [], lambda refs: (refs, lambda: None)
    scatter, xs = exchange
    n = len(xs)
    hbm = pl.BlockSpec(memory_space=pltpu.HBM)

    def hook(refs):
        ins, x_refs = refs[:n_in], refs[n_in:n_in + n]
        outs, land_refs = refs[n_in + n:n_in + n + n_out], refs[n_in + n + n_out:n_in + 2 * n + n_out]
        rest = refs[n_in + 2 * n + n_out:]
        own_scratch, sems = rest[:len(rest) - 3], rest[len(rest) - 3:]
        start, wait = _direct_exchange(scatter, x_refs, land_refs, *sems)
        ids = [pl.program_id(a) for a in range(len(grid))]
        is_first = functools.reduce(jnp.logical_and, [i == 0 for i in ids])
        is_last = functools.reduce(jnp.logical_and, [i == g - 1 for i, g in zip(ids, grid)])
        pl.when(is_first)(start)
        return tuple(ins) + tuple(outs) + tuple(own_scratch), lambda: pl.when(is_last)(wait)

    return list(xs), _exchange_shapes(scatter, xs), [hbm] * n, _exchange_scratch(n), hook


def _fox_fwd(q_aug, k_aug, qkv, name, t=TQ, exchange=None):
    s = q_aug.shape[0]
    t = min(t, s)
    n = s // t
    dh = HEAD_DIM
    grid = (N_HEADS // PAIR, n)
    ex_in, ex_out, ex_specs, ex_scratch, hook = _riding_exchange(exchange, 3, 2, grid)

    def body(*refs):
        (q_ref, k_ref, v_ref, o_ref, lse_ref, m_s, l_s, acc_s), finish = hook(refs)
        qi = pl.program_id(1)
        m_s[...] = jnp.full_like(m_s, NEG)
        l_s[...] = jnp.zeros_like(l_s)
        acc_s[...] = jnp.zeros_like(acc_s)

        def step(kj, diag):
            rows = pl.ds(pl.multiple_of(kj * t, t), t)
            for hh in range(PAIR):
                st = lax.dot_general(k_ref[rows, AUG * hh:AUG * (hh + 1)], q_ref[:, AUG * hh:AUG * (hh + 1)], _NT,
                                     preferred_element_type=F32)
                if diag:
                    st = jnp.where(_causal_keep(t, t), st, NEG)
                m_prev = m_s[hh]
                m_new = jnp.maximum(m_prev, jnp.max(st, axis=0, keepdims=True))
                a = jnp.exp2(m_prev - m_new)
                pt = jnp.exp2(st - m_new)
                l_s[hh] = a * l_s[hh] + jnp.sum(pt, axis=0, keepdims=True)
                acc_s[hh] = a * acc_s[hh] + lax.dot_general(v_ref[rows, dh * hh:dh * (hh + 1)], pt.astype(BF16), _TN,
                                                            preferred_element_type=F32)
                m_s[hh] = m_new

        def off_diagonal(kj, carry):
            step(kj, False)
            return carry

        lax.fori_loop(0, qi, off_diagonal, 0)
        step(qi, True)
        for hh in range(PAIR):
            o_ref[hh] = acc_s[hh] / l_s[hh]
            lse_ref[hh, 0] = m_s[hh] + jnp.log2(l_s[hh])
        finish()

    return _pcall(
        body, name=name,
        out_shape=[jax.ShapeDtypeStruct((N_HEADS, dh, s), F32), jax.ShapeDtypeStruct((N_HEADS, n, 1, t), F32)] + ex_out,
        grid=grid,
        in_specs=[pl.BlockSpec((t, PAIR * AUG), lambda hp, i: (i, hp)),
                  pl.BlockSpec((s, PAIR * AUG), lambda hp, i: (0, hp)),
                  pl.BlockSpec((s, PAIR * dh), lambda hp, i: (0, V_FOX + hp))] + ex_specs,
        out_specs=[pl.BlockSpec((PAIR, dh, t), lambda hp, i: (hp, 0, i)),
                   pl.BlockSpec((PAIR, 1, 1, t), lambda hp, i: (hp, i, 0, 0))] + ex_specs,
        scratch_shapes=[pltpu.VMEM((PAIR, 1, t), F32), pltpu.VMEM((PAIR, 1, t), F32),
                        pltpu.VMEM((PAIR, dh, t), F32)] + ex_scratch,
        compiler_params=_cp("arbitrary", "arbitrary"),
    )(q_aug, k_aug, qkv, *ex_in)


def _row_dot(a_t, b_t, name, t=2048):
    h, dh, s = a_t.shape
    t = min(t, s)

    def body(a_ref, b_ref, o_ref):
        o_ref[...] = jnp.sum(a_ref[...].astype(F32) * b_ref[...], axis=0, keepdims=True)

    blk = pl.BlockSpec((None, dh, t), lambda hh, i: (hh, 0, i))
    return _pcall(
        body, name=name, out_shape=jax.ShapeDtypeStruct((h, 1, s), F32), grid=(h, s // t),
        in_specs=[blk, blk], out_specs=pl.BlockSpec((None, 1, t), lambda hh, i: (hh, 0, i)),
        compiler_params=_cp("parallel", "parallel"),
    )(a_t, b_t)


def _fox_bwd(q_aug, k_aug, qkv, do, lse, delta, name, t=TQ, exchange=None):
    s = q_aug.shape[0]
    dh = HEAD_DIM
    t = min(t, s)
    n = s // t
    grid = (N_HEADS // PAIR, n)
    ex_in, ex_out, ex_specs, ex_scratch, hook = _riding_exchange(exchange, 6, 5, grid)

    def body(*refs):
        (q_ref, k_ref, v_ref, do_ref, lse_ref, dl_ref, dk_ref, dv_ref, dq_ref, dsum_ref, dqsum_ref,
         dk_s, dv_s, dsum_s), finish = hook(refs)
        kj = pl.program_id(1)

        @pl.when(kj == 0)
        def _():
            dq_ref[...] = jnp.zeros_like(dq_ref)
            dqsum_ref[...] = jnp.zeros_like(dqsum_ref)

        dk_s[...] = jnp.zeros_like(dk_s)
        dv_s[...] = jnp.zeros_like(dv_s)
        dsum_s[...] = jnp.zeros_like(dsum_s)

        def step(qi, diag):
            rows = pl.ds(pl.multiple_of(qi * t, t), t)
            for hh in range(PAIR):
                q = q_ref[rows, AUG * hh:AUG * (hh + 1)]
                k = k_ref[:, AUG * hh:AUG * (hh + 1)]
                st = lax.dot_general(k, q, _NT, preferred_element_type=F32)
                if diag:
                    st = jnp.where(_causal_keep(t, t), st, NEG)
                pt = jnp.exp2(st - lse_ref[hh, qi])
                do_v = do_ref[rows, dh * hh:dh * (hh + 1)]
                dpt = lax.dot_general(v_ref[:, dh * hh:dh * (hh + 1)], do_v, _NT, preferred_element_type=F32)
                ds32 = pt * (dpt - dl_ref[hh, qi])
                dsum_s[hh] += sum(ds32[:, 128 * u:128 * (u + 1)] for u in range(t // 128))
                dqsum_ref[hh, qi] += jnp.sum(ds32, axis=0, keepdims=True)
                dst = ds32.astype(BF16)
                dv_s[hh] += jnp.dot(pt.astype(BF16), do_v, preferred_element_type=F32)
                dk_s[hh] += jnp.dot(dst, q[:, :dh], preferred_element_type=F32)
                dq_ref[hh, qi] += lax.dot_general(k[:, :dh], dst, _TN, preferred_element_type=F32)

        def off_diagonal(qi, carry):
            step(qi, False)
            return carry

        step(kj, True)
        lax.fori_loop(kj + 1, n, off_diagonal, 0)
        for hh in range(PAIR):
            dk_ref[:, dh * hh:dh * (hh + 1)] = (LN2 * dk_s[hh]).astype(dk_ref.dtype)
            dv_ref[:, dh * hh:dh * (hh + 1)] = dv_s[hh].astype(dv_ref.dtype)
            dsum_ref[:, 128 * hh:128 * (hh + 1)] = dsum_s[hh]
        finish()

    stat = pl.BlockSpec((PAIR, n, 1, t), lambda hp, j: (hp, 0, 0, 0))
    return _pcall(
        body, name=name,
        out_shape=[jax.ShapeDtypeStruct((s, WIDTH), BF16), jax.ShapeDtypeStruct((s, WIDTH), BF16),
                   jax.ShapeDtypeStruct((N_HEADS, n, dh, t), F32), jax.ShapeDtypeStruct((s, N_HEADS * 128), F32),
                   jax.ShapeDtypeStruct((N_HEADS, n, 1, t), F32)] + ex_out,
        grid=grid,
        in_specs=[pl.BlockSpec((s, PAIR * AUG), lambda hp, j: (0, hp)),
                  pl.BlockSpec((t, PAIR * AUG), lambda hp, j: (j, hp)),
                  pl.BlockSpec((t, PAIR * dh), lambda hp, j: (j, V_FOX + hp)),
                  pl.BlockSpec((s, PAIR * dh), lambda hp, j: (0, hp)),
                  stat, stat] + ex_specs,
        out_specs=[pl.BlockSpec((t, PAIR * dh), lambda hp, j: (j, hp)),
                   pl.BlockSpec((t, PAIR * dh), lambda hp, j: (j, hp)),
                   pl.BlockSpec((PAIR, n, dh, t), lambda hp, j: (hp, 0, 0, 0)),
                   pl.BlockSpec((t, PAIR * 128), lambda hp, j: (j, hp)),
                   stat] + ex_specs,
        scratch_shapes=[pltpu.VMEM((PAIR, t, dh), F32), pltpu.VMEM((PAIR, t, dh), F32),
                        pltpu.VMEM((PAIR, t, 128), F32)] + ex_scratch,
        compiler_params=_cp("arbitrary", "arbitrary"),
    )(q_aug, k_aug, qkv, do, lse, delta, *ex_in)


N_SUB = TQ // SUB


def _pair_blockdiag(x2):
    lane = lax.broadcasted_iota(jnp.int32, x2.shape, 1)
    zero = jnp.zeros_like(x2)
    return jnp.concatenate([jnp.where(lane < HEAD_DIM, x2, zero), jnp.where(lane >= HEAD_DIM, x2, zero)], axis=0)


def _pair_lanes(ref, j):
    return jnp.concatenate([ref[hh, 0, :, SUB * j:SUB * (j + 1)] for hh in range(PAIR)], axis=1)


def _chunk_logits(kwin, qbd, b_ref, first_tile, j):
    which = jnp.where(first_tile, 1 + j, 0)
    bias = jnp.concatenate([b_ref[hh, which] for hh in range(PAIR)], axis=1)
    return lax.dot_general(kwin, qbd, _NT, preferred_element_type=F32) + bias


def _chunk_fwd(qkv, bias_t, name):
    s = qkv.shape[0]
    dh = HEAD_DIM
    n = s // TQ
    nsub = TQ // SUB
    prev = lambda i: jnp.maximum(i - 1, 0)

    def body(q_ref, kp_ref, kc_ref, vp_ref, vc_ref, b_ref, o_ref, lse_ref):
        first_tile = pl.program_id(1) == 0
        kw = jnp.concatenate([kp_ref[...], kc_ref[...]], axis=0)
        vw = jnp.concatenate([vp_ref[...], vc_ref[...]], axis=0)
        for j in range(nsub):
            rows, win = slice(SUB * j, SUB * (j + 1)), slice(SUB * j, SUB * j + SUBK)
            st = _chunk_logits(kw[win], _pair_blockdiag(q_ref[rows, :] * QK_SCALE2), b_ref, first_tile, j)
            m = jnp.max(st, axis=0, keepdims=True)
            pt = jnp.exp2(st - m)
            l = jnp.sum(pt, axis=0, keepdims=True)
            ot = lax.dot_general(vw[win], pt.astype(BF16), _TN, preferred_element_type=F32)
            stat = m + jnp.log2(l)
            for hh in range(PAIR):
                lanes = slice(SUB * hh, SUB * (hh + 1))
                o_ref[hh, :, rows] = ot[dh * hh:dh * (hh + 1), lanes] / l[:, lanes]
                lse_ref[hh, 0, :, rows] = stat[:, lanes]

    blk = lambda col, m: pl.BlockSpec((TQ, PAIR * dh), lambda hp, i: (m(i), col + hp))
    same = lambda i: i
    return _pcall(
        body, name=name,
        out_shape=(jax.ShapeDtypeStruct((N_HEADS, dh, s), F32), jax.ShapeDtypeStruct((N_HEADS, n, 1, TQ), F32)),
        grid=(N_HEADS // PAIR, n),
        in_specs=[blk(Q_CHUNK, same), blk(K_CHUNK, prev), blk(K_CHUNK, same), blk(V_CHUNK, prev), blk(V_CHUNK, same),
                  pl.BlockSpec((PAIR, 1 + N_SUB, SUBK, SUB), lambda hp, i: (hp, 0, 0, 0))],
        out_specs=(pl.BlockSpec((PAIR, dh, TQ), lambda hp, i: (hp, 0, i)),
                   pl.BlockSpec((PAIR, 1, 1, TQ), lambda hp, i: (hp, i, 0, 0))),
        compiler_params=_cp("parallel", "parallel"),
    )(qkv, qkv, qkv, qkv, qkv, bias_t)


def _chunk_bwd(qkv, do, lse, delta, bias_t, name):
    s = qkv.shape[0]
    dh = HEAD_DIM
    n = s // TQ
    nsub = TQ // SUB
    cur = lambda i: jnp.minimum(i, n - 1)
    prev = lambda i: jnp.maximum(cur(i) - 1, 0)
    done = lambda i: jnp.maximum(i - 1, 0)

    def body(q_ref, kp_ref, kc_ref, vp_ref, vc_ref, do_ref, lse_ref, dl_ref, b_ref,
             dq_ref, dk_ref, dv_ref, db_ref, dkw_s, dvw_s, ck_s, cv_s):
        i = pl.program_id(1)

        @pl.when(i == 0)
        def _():
            db_ref[...] = jnp.zeros_like(db_ref)
            ck_s[...] = jnp.zeros_like(ck_s)
            cv_s[...] = jnp.zeros_like(cv_s)

        dkw_s[...] = jnp.zeros_like(dkw_s)
        dvw_s[...] = jnp.zeros_like(dvw_s)

        @pl.when(i < n)
        def _():
            first_tile = i == 0
            kw = jnp.concatenate([kp_ref[...], kc_ref[...]], axis=0)
            vw = jnp.concatenate([vp_ref[...], vc_ref[...]], axis=0)
            for j in range(nsub):
                rows, win = slice(SUB * j, SUB * (j + 1)), slice(SUB * j, SUB * j + SUBK)
                qbd = _pair_blockdiag(q_ref[rows, :] * QK_SCALE2)
                dobd = _pair_blockdiag(do_ref[rows, :])
                st = _chunk_logits(kw[win], qbd, b_ref, first_tile, j)
                pt = jnp.exp2(st - _pair_lanes(lse_ref, j))
                dpt = lax.dot_general(vw[win], dobd, _NT, preferred_element_type=F32)
                dst = pt * (dpt - _pair_lanes(dl_ref, j))
                dsb = dst.astype(BF16)
                dvw_s[win, :] += jnp.dot(pt.astype(BF16), dobd, preferred_element_type=F32)
                dkw_s[win, :] += jnp.dot(dsb, qbd, preferred_element_type=F32)
                dqt = QK_SCALE * lax.dot_general(kw[win], dsb, _TN, preferred_element_type=F32)
                for hh in range(PAIR):
                    lanes = slice(SUB * hh, SUB * (hh + 1))
                    db_ref[hh] += dst[:, lanes]
                    dq_ref[hh, :, rows] = dqt[dh * hh:dh * (hh + 1), lanes]

        dk_ref[...] = (LN2 * (ck_s[...] + dkw_s[0:TQ, :])).astype(dk_ref.dtype)
        dv_ref[...] = (cv_s[...] + dvw_s[0:TQ, :]).astype(dv_ref.dtype)
        ck_s[...] = dkw_s[TQ:2 * TQ, :]
        cv_s[...] = dvw_s[TQ:2 * TQ, :]

    blk = lambda col, m: pl.BlockSpec((TQ, PAIR * dh), lambda hp, i: (m(i), col + hp))
    stat = pl.BlockSpec((PAIR, 1, 1, TQ), lambda hp, i: (hp, cur(i), 0, 0))
    bs = pl.BlockSpec((PAIR, SUBK, SUB), lambda hp, i: (hp, 0, 0))
    return _pcall(
        body, name=name,
        out_shape=(jax.ShapeDtypeStruct((N_HEADS, dh, s), F32), jax.ShapeDtypeStruct((s, WIDTH), BF16),
                   jax.ShapeDtypeStruct((s, WIDTH), BF16), jax.ShapeDtypeStruct((N_HEADS, SUBK, SUB), F32)),
        grid=(N_HEADS // PAIR, n + 1),
        in_specs=[blk(Q_CHUNK, cur), blk(K_CHUNK, prev), blk(K_CHUNK, cur), blk(V_CHUNK, prev), blk(V_CHUNK, cur),
                  blk(0, cur), stat, stat, pl.BlockSpec((PAIR, 1 + N_SUB, SUBK, SUB), lambda hp, i: (hp, 0, 0, 0))],
        out_specs=(pl.BlockSpec((PAIR, dh, TQ), lambda hp, i: (hp, 0, cur(i))), blk(0, done), blk(0, done), bs),
        scratch_shapes=[pltpu.VMEM((2 * TQ, PAIR * dh), F32), pltpu.VMEM((2 * TQ, PAIR * dh), F32),
                        pltpu.VMEM((TQ, PAIR * dh), F32), pltpu.VMEM((TQ, PAIR * dh), F32)],
        compiler_params=_cp("parallel", "arbitrary"),
    )(qkv, qkv, qkv, qkv, qkv, do, lse, delta, bias_t)


def _mod_part(c_all, w_ada, b_ada, name):
    nl, d, n = w_ada.shape
    b = c_all.shape[0]

    def body(c_ref, w_ref, b_ref, o_ref, cond_ref):
        cv = c_ref[...]
        cond = cv * _sigmoid(cv)
        cond_ref[...] = cond
        o_ref[...] = jnp.dot(cond, w_ref[...], preferred_element_type=F32, precision=HI) + b_ref[...]

    return _pcall(
        body, name=name,
        out_shape=(jax.ShapeDtypeStruct((nl, b, n), F32), jax.ShapeDtypeStruct((b, d), F32)),
        grid=(nl,),
        in_specs=[pl.BlockSpec((b, d), lambda l: (0, 0)), pl.BlockSpec((None, d, n), lambda l: (l, 0, 0)),
                  pl.BlockSpec((None, 1, n), lambda l: (l, 0, 0))],
        out_specs=(pl.BlockSpec((None, b, n), lambda l: (l, 0, 0)), pl.BlockSpec((b, d), lambda l: (0, 0))),
        compiler_params=_cp("arbitrary"),
    )(c_all, w_ada, b_ada)


def _adamw(parts, w, m, v, name, tr=1024):
    p, rl, cp = parts[0].shape
    r, c = w.shape
    tr = _tile(rl, tr, 16)
    per = rl // tr
    c1 = 1.0 / (1.0 - ADAM_B1 ** ADAM_STEP)
    c2 = 1.0 / (1.0 - ADAM_B2 ** ADAM_STEP)

    def body(*refs):
        p_refs, (w_ref, m_ref, v_ref, g_ref, d_ref, nm_ref, nv_ref) = refs[:len(parts)], refs[len(parts):]
        for which, p_ref in enumerate(p_refs):
            pl.when(pl.program_id(0) // per == which)(
                functools.partial(update, p_ref, w_ref, m_ref, v_ref, g_ref, d_ref, nm_ref, nv_ref))

    def update(p_ref, w_ref, m_ref, v_ref, g_ref, d_ref, nm_ref, nv_ref):
        g = p_ref[0].astype(F32)
        for i in range(1, p):
            g = g + p_ref[i].astype(F32)
        g = g[:, :c]
        nm = ADAM_B1 * m_ref[...] + (1.0 - ADAM_B1) * g
        nv = ADAM_B2 * v_ref[...] + (1.0 - ADAM_B2) * (g * g)
        g_ref[...] = g
        nm_ref[...] = nm
        nv_ref[...] = nv
        d_ref[...] = -ADAM_LR * ((nm * c1) / (jnp.sqrt(nv * c2) + ADAM_EPS) + ADAM_WD * w_ref[...])

    blk = pl.BlockSpec((tr, c), lambda i: (i, 0))
    out = jax.ShapeDtypeStruct((r, c), F32)
    return _pcall(
        body, name=name, out_shape=(out, out, out, out), grid=(r // tr,),
        in_specs=[pl.BlockSpec((p, tr, cp), lambda i, _w=which: (0, jnp.clip(i - _w * per, 0, per - 1), 0))
                  for which in range(len(parts))] + [blk, blk, blk],
        out_specs=(blk, blk, blk, blk),
        compiler_params=_cp("parallel"),
    )(*parts, w, m, v)


def _pad_lanes(a):
    pad = (-a.shape[-1]) % 128
    return jnp.pad(a, [(0, 0)] * (a.ndim - 1) + [(0, pad)]) if pad else a


def _rows128(a, mult=16):
    flat = a.reshape(-1)
    n = flat.shape[0]
    per = 128 * mult
    pad = (-n) % per
    if pad:
        flat = jnp.concatenate([flat, jnp.zeros((pad,), a.dtype)])
    return flat.reshape(-1, 128)


def _pack(arrs, mult=16):
    pieces = [_rows128(a, mult) for a in arrs]
    return jnp.concatenate(pieces, axis=0), [p.shape[0] for p in pieces]


def _unpack(packed, rows, shapes, lead=()):
    out, at = [], 0
    for r, shp in zip(rows, shapes):
        n = int(np.prod(shp))
        piece = packed[..., at:at + r, :].reshape(lead + (r * 128,))[..., :n]
        out.append(piece.reshape(lead + tuple(shp)))
        at += r
    return out


def _rel_bias_tile(table):
    h = table.shape[0]
    lo = REL_CLIP - (CHUNK - 1)
    n_far = LEFT_CHUNKS * CHUNK + CHUNK - 1 - REL_CLIP
    vec = jnp.concatenate([table[:, lo:2 * REL_CLIP], jnp.repeat(table[:, 2 * REL_CLIP:], n_far + 1, axis=1)], axis=1)
    rev = vec[:, ::-1]
    n_vec = BAND + CHUNK - 1
    skew = jnp.tile(rev, (1, CHUNK + 1))[:, :CHUNK * (n_vec + 1)].reshape(h, CHUNK, n_vec + 1)
    bias = skew[:, ::-1, :BAND]
    neg = jnp.full((h, CHUNK, CHUNK), NEG, F32)
    two = jnp.concatenate([jnp.concatenate([bias, neg], axis=2), jnp.concatenate([neg, bias], axis=2)], axis=1)
    plain = two.transpose(0, 2, 1) * LOG2E
    key = np.arange(SUBK)[:, None]
    return jnp.stack([plain] + [jnp.where(key >= TQ - SUB * j, plain, NEG) for j in range(N_SUB)], axis=1)


def _rel_bias_tile_grad(dbias_t):
    h = dbias_t.shape[0]
    two = dbias_t.transpose(0, 2, 1)
    dbias = two[:, :CHUNK, :BAND] + two[:, CHUNK:, CHUNK:]
    n_vec = BAND + CHUNK - 1
    dskew = jnp.pad(dbias[:, ::-1, :], ((0, 0), (0, 0), (0, n_vec + 1 - BAND))).reshape(h, CHUNK * (n_vec + 1))
    dskew = jnp.pad(dskew, ((0, 0), (0, (CHUNK + 1) * n_vec - CHUNK * (n_vec + 1))))
    drev = jnp.sum(dskew.reshape(h, CHUNK + 1, n_vec), axis=1)
    dvec = drev[:, ::-1]
    lo = REL_CLIP - (CHUNK - 1)
    n_near = 2 * REL_CLIP - lo
    return jnp.concatenate([jnp.zeros((h, lo), F32), dvec[:, :n_near],
                            jnp.sum(dvec[:, n_near:], axis=1, keepdims=True)], axis=1)


BIG = ("w_in", "w_br_fox", "w_br_chunk", "w_out", "w_up", "w_down")
SMALL = ("b_f", "rel_bias", "conv_b", "b_ada", "ln1_g", "ln1_b", "ln2_g", "ln2_b")
ORDER = ("w_in", "b_f", "rel_bias", "w_br_fox", "w_br_chunk", "w_out", "w_up", "conv_w", "conv_b", "w_down",
         "w_ada", "b_ada", "ln1_g", "ln1_b", "ln2_g", "ln2_b")


def _dest_major(g, axis):
    shp = g.shape
    g = g.reshape(shp[:axis] + (N_DEV, shp[axis] // N_DEV) + shp[axis + 1:])
    return jnp.moveaxis(g, axis, 0)


EARLY = ("w_down", "w_up", "w_out", "w_br_fox", "w_br_chunk")
LATE = ("w_in",)
REST = ("w_br_fox", "w_br_chunk", "w_out", "w_up", "w_down")


def _shard_axis(n):
    return 0 if n in ("w_out", "w_down") else 1


def _slab(g, n):
    return _pad_lanes(_dest_major(g, _shard_axis(n)).astype(BF16))


def _layer_fwd(x, h1, wl, rest_of, mod, next_mod, alpha, gather):
    sh1, sc1, g1, sh2, sc2, g2 = mod
    s = x.shape[0]
    qkv = _mm(h1, wl["w_qkv"], out_dtype=BF16, name="proj_qkv", tk=1024)
    gates = _mm(h1, wl["w_gates"], out_dtype=F32, name="proj_gates", tk=1024)
    f_t = _mm(wl["w_f_t"], h1, out_dtype=F32, name="proj_f", tb=True, tk=1024)[:N_HEADS]
    f_t = f_t.reshape(N_HEADS, s // 128, 128)
    b_f = jnp.broadcast_to(wl["b_f"].reshape(N_HEADS, 1, 1), (N_HEADS, 1, 128))
    cum = _forget_fwd(f_t, b_f, "forget_fwd").reshape(N_SPLIT, N_HEADS, s)
    cum_cols = cum.transpose(2, 1, 0).astype(BF16)

    ones = jnp.ones((s, N_HEADS, N_SPLIT), BF16)
    zeros = jnp.zeros((s, N_HEADS, AUG - HEAD_DIM - 2 * N_SPLIT), BF16)
    q_a = qkv[:, :WIDTH].reshape(s, N_HEADS, HEAD_DIM) * QK_SCALE2
    k_a = qkv[:, WIDTH:2 * WIDTH].reshape(s, N_HEADS, HEAD_DIM)
    q_aug = jnp.concatenate([q_a, cum_cols, ones, zeros], axis=-1).reshape(s, N_HEADS * AUG)
    k_aug = jnp.concatenate([k_a, ones, -cum_cols, zeros], axis=-1).reshape(s, N_HEADS * AUG)
    o_a_t, lse_a, *gathered = _fox_fwd(q_aug, k_aug, qkv, "fox_fwd_gather_%d" % len(gather), t=FOX_FWD_T,
                                       exchange=(False, gather))
    wl.update(rest_of(gathered[:len(REST)]))
    gathered = gathered[len(REST):]
    bias_t = _rel_bias_tile(wl["rel_bias"])
    o_c_t, lse_c = _chunk_fwd(qkv, bias_t, "chunk_fwd")
    o_a = o_a_t.transpose(2, 0, 1).reshape(s, WIDTH).astype(BF16)
    o_c = o_c_t.transpose(2, 0, 1).reshape(s, WIDTH).astype(BF16)

    merged = _merge_fwd(o_a, o_c, wl["w_br_fox"], wl["w_br_chunk"], gates, "merge_fwd")
    mix, x1, h2 = _mm_res_ln(merged, wl["w_out"], x, g1, wl["ln1_g"], wl["ln1_b"], sc2, sh2, alpha, "out_ln1")
    u0 = _mm(h2, wl["w_up"], out_dtype=F32, name="ffn_up", tk=1024, tn=1408, o_halves=True)
    act = _conv_act_fwd(u0, wl["conv_w"], wl["conv_b"], "conv_act_fwd")
    y2, x2, h_next = _mm_res_ln(act, wl["w_down"], x1, g2, wl["ln2_g"], wl["ln2_b"], next_mod[0], next_mod[1],
                                alpha, "down_ln2", tk=1408)
    saved = dict(x=x, h1=h1, gates=gates, f_t=f_t, b_f=b_f, q_aug=q_aug, k_aug=k_aug, qkv=qkv,
                 bias_t=bias_t, o_a_t=o_a_t, o_c_t=o_c_t, lse_a=lse_a, lse_c=lse_c,
                 o_a=o_a, o_c=o_c, merged=merged, mix=mix, x1=x1, h2=h2, u0=u0, act=act, y2=y2)
    return x2, h_next, saved, gathered


def _layer_bwd(dx2, sv, wl, mod, alpha, send, ride_late):
    sh1, sc1, g1, sh2, sc2, g2 = mod
    s = dx2.shape[0]
    g = {}
    dres2, dy2, sums = _ln_res_bwd(dx2, sv["x1"], sv["y2"], g2, wl["ln2_g"], alpha, "ln2_bwd")
    g["ln2_g"], g["ln2_b"], dg2 = sums[0], sums[1], sums[2]
    dact = _mm(dy2, wl["w_down_t"], out_dtype=F32, name="dact", tk=1024, tn=1408)
    g["w_down"] = _mm(sv["act"], dy2, out_dtype=F32, name="g_w_down", ta=True, tm=1408, tk=2048)
    du0, csum = _conv_act_bwd(dact, sv["u0"], wl["conv_w"], wl["conv_b"], "conv_act_bwd")
    g["conv_w"] = jnp.concatenate([csum[0, 0:3], csum[1, 0:3]], axis=1)
    g["conv_b"] = jnp.concatenate([csum[0, 3], csum[1, 3]])
    dh2 = _mm(du0, wl["w_up_t"], out_dtype=F32, name="dh2", tk=2816, a_halves=True)
    g["w_up"] = _mm(sv["h2"], du0, out_dtype=F32, name="g_w_up", ta=True, tk=2048, tn=1408, b_halves=True)
    dx1, sums = _ln_mod_bwd(dh2, sv["x1"], sc2, dres2, "ln_mod2_bwd")
    dsh2, dsc2 = sums[0], sums[1]

    dres1, dmix, sums = _ln_res_bwd(dx1, sv["x"], sv["mix"], g1, wl["ln1_g"], alpha, "ln1_bwd")
    g["ln1_g"], g["ln1_b"], dg1 = sums[0], sums[1], sums[2]
    g["w_out"] = _mm(sv["merged"], dmix, out_dtype=F32, name="g_w_out", ta=True, tk=2048)
    dba, dbc, do_a, do_c, dgates = _merge_bwd(
        dmix, wl["w_out_t"], sv["o_a"], sv["o_c"], wl["w_br_fox"], wl["w_br_chunk"],
        wl["w_br_fox_t"], wl["w_br_chunk_t"], sv["gates"], "merge_bwd")
    g["w_br_fox"] = _mm(sv["o_a"], dba, out_dtype=F32, name="g_w_br_fox", ta=True, tk=2048)
    g["w_br_chunk"] = _mm(sv["o_c"], dbc, out_dtype=F32, name="g_w_br_chunk", ta=True, tk=2048)
    tiles = lambda a, t: a.reshape(N_HEADS, s // min(t, s), 1, min(t, s))
    stats = lambda a: tiles(a, TQ)

    delta_a = _row_dot(do_a.T.reshape(N_HEADS, HEAD_DIM, s), sv["o_a_t"], "delta_fox")
    send = list(send) + [_slab(g[n], n) for n in EARLY]
    dk_a, dv_a, dq_a_t, dsum, dqsum, *landed = _fox_bwd(
        sv["q_aug"], sv["k_aug"], sv["qkv"], do_a, tiles(sv["lse_a"], FOX_BWD_T), tiles(delta_a, FOX_BWD_T),
        "fox_bwd_scatter_%d" % len(send), t=FOX_BWD_T, exchange=(True, send))
    dq_a = (dq_a_t * QK_SCALE).transpose(1, 3, 0, 2).reshape(s, WIDTH).astype(BF16)
    d_cum = dqsum.reshape(N_HEADS, s) - jnp.sum(dsum.reshape(s, N_HEADS, 128), axis=-1).T
    df_t, db_f = _forget_bwd(d_cum.reshape(N_HEADS, s // 128, 128), sv["f_t"], sv["b_f"], "forget_bwd")
    g["b_f"] = db_f[:, 0, 0]
    df_t = df_t.reshape(N_HEADS, s)

    delta_c = _row_dot(do_c.T.reshape(N_HEADS, HEAD_DIM, s), sv["o_c_t"], "delta_chunk")
    dq_c_t, dk_c, dv_c, dbias_t = _chunk_bwd(sv["qkv"], do_c, sv["lse_c"], stats(delta_c), sv["bias_t"], "chunk_bwd")
    g["rel_bias"] = _rel_bias_tile_grad(dbias_t)
    dq_c = dq_c_t.transpose(2, 0, 1).reshape(s, WIDTH).astype(BF16)

    dqkv = jnp.concatenate([dq_a, dk_a, dv_a, dq_c, dk_c, dv_c], axis=1)
    df_pad = jnp.zeros((16 - N_HEADS, s), F32)
    df16 = jnp.concatenate([df_t, df_pad], axis=0).astype(BF16)
    df_cols = jnp.concatenate([df16.T, jnp.zeros((s, AUG - 16), BF16)], axis=1)

    g_qkv = _mm(sv["h1"], dqkv, out_dtype=F32, name="g_w_qkv", ta=True, tk=2048)
    g_gates = _mm(sv["h1"], dgates, out_dtype=F32, name="g_w_gates", ta=True, tk=2048)
    g_f_t = _mm(df16, sv["h1"], out_dtype=F32, name="g_w_f", tk=1024)[:N_HEADS]
    g["w_in"] = jnp.concatenate([g_qkv[:, :3 * WIDTH], g_f_t.T, g_qkv[:, 3 * WIDTH:], g_gates], axis=1)
    late = [_slab(g[n], n) for n in LATE] if ride_late else None
    dh1, *landed_late = _mm(dqkv, wl["w_qkv_t"], out_dtype=F32, name="dh1_qkv_scatter", tk=1536,
                            exchange=(True, late)) if ride_late else \
        [_mm(dqkv, wl["w_qkv_t"], out_dtype=F32, name="dh1_qkv", tk=1536)]
    dh1g = _mm(dgates, wl["w_gates_t"], out_dtype=F32, name="dh1_gates", tk=2048)
    dx, sums = _ln_mod_bwd(dh1, sv["x"], sc1, dres1, "ln_mod1_bwd", extra=(dh1g, df_cols, wl["w_f_pad"]))
    dsh1, dsc1 = sums[0], sums[1]
    g["mod"] = jnp.concatenate([dsh1, dsc1, dg1, dsh2, dsc2, dg2])
    return dx, g, landed, landed_late


def kernel(x, c, w_in, b_f, rel_bias, w_br_fox, w_br_chunk, w_out, w_up, conv_w, conv_b, w_down, w_ada, b_ada, ln1_g, ln1_b, ln2_g, ln2_b, loss_target, m_w_in, m_b_f, m_rel_bias, m_w_br_fox, m_w_br_chunk, m_w_out, m_w_up, m_conv_w, m_conv_b, m_w_down, m_w_ada, m_b_ada, m_ln1_g, m_ln1_b, m_ln2_g, m_ln2_b, v_w_in, v_b_f, v_rel_bias, v_w_br_fox, v_w_br_chunk, v_w_out, v_w_up, v_conv_w, v_conv_b, v_w_down, v_w_ada, v_b_ada, v_ln1_g, v_ln1_b, v_ln2_g, v_ln2_b):
    w = dict(w_in=w_in, b_f=b_f, rel_bias=rel_bias, w_br_fox=w_br_fox, w_br_chunk=w_br_chunk, w_out=w_out,
             w_up=w_up, conv_w=conv_w, conv_b=conv_b, w_down=w_down, w_ada=w_ada, b_ada=b_ada,
             ln1_g=ln1_g, ln1_b=ln1_b, ln2_g=ln2_g, ln2_b=ln2_b)
    m = dict(w_in=m_w_in, b_f=m_b_f, rel_bias=m_rel_bias, w_br_fox=m_w_br_fox, w_br_chunk=m_w_br_chunk,
             w_out=m_w_out, w_up=m_w_up, conv_w=m_conv_w, conv_b=m_conv_b, w_down=m_w_down, w_ada=m_w_ada,
             b_ada=m_b_ada, ln1_g=m_ln1_g, ln1_b=m_ln1_b, ln2_g=m_ln2_g, ln2_b=m_ln2_b)
    v = dict(w_in=v_w_in, b_f=v_b_f, rel_bias=v_rel_bias, w_br_fox=v_w_br_fox, w_br_chunk=v_w_br_chunk,
             w_out=v_w_out, w_up=v_w_up, conv_w=v_conv_w, conv_b=v_conv_b, w_down=v_w_down, w_ada=v_w_ada,
             b_ada=v_b_ada, ln1_g=v_ln1_g, ln1_b=v_ln1_b, ln2_g=v_ln2_g, ln2_b=v_ln2_b)
    depth, d, _ = w_in.shape
    s = x.shape[1]
    alpha = (2.0 * depth) ** 0.25
    me = 4 * lax.axis_index("x") + 2 * lax.axis_index("y") + lax.axis_index("c")
    x0 = x.reshape(s, d)
    target = loss_target.reshape(s, d)

    small_in, small_rows = _pack([c, conv_w], mult=8)
    small_all = _all_gather([small_in], "gather_c_conv")[0]
    c_all, conv_w_all = _unpack(small_all, small_rows, [c.shape, conv_w.shape], lead=(N_DEV,))
    c_all = c_all.reshape(N_DEV, d)
    conv_w_full = conv_w_all.transpose(1, 2, 0, 3).reshape(depth, conv_w.shape[1], -1)
    n_ada = w_ada.shape[2]
    b_ada_mine = lax.dynamic_slice_in_dim(b_ada, me * n_ada, n_ada, axis=1).reshape(depth, 1, n_ada)
    mod_part, cond_all = _mod_part(c_all, w_ada, b_ada_mine, "mod_part")
    mod_all = _all_gather([mod_part.reshape(depth * N_DEV, n_ada)], "gather_mod")[0]
    mod_all = mod_all.reshape(N_DEV, depth, N_DEV, n_ada)
    mod_mine = lax.dynamic_index_in_dim(mod_all, me, axis=2, keepdims=False)
    mod_mine = mod_mine.transpose(1, 0, 2).reshape(depth, N_MOD, 1, d)

    shard = lambda n, l: _pad_lanes(w[n][l].astype(BF16))
    cols = np.cumsum([0, WIDTH, WIDTH, WIDTH, N_HEADS, WIDTH, WIDTH, WIDTH, d, d])

    def whole(n, p):
        rows, ncol = w[n].shape[1:]
        p = jnp.moveaxis(p[..., :ncol], 0, _shard_axis(n))
        return p.reshape((rows * N_DEV, ncol) if _shard_axis(n) == 0 else (rows, ncol * N_DEV))

    def first_weights(l, w_in_parts):
        wi = whole("w_in", w_in_parts)
        w_qkv = jnp.concatenate([wi[:, :cols[3]], wi[:, cols[4]:cols[7]]], axis=1)
        w_gates = wi[:, cols[7]:]
        w_f_t = jnp.concatenate([wi[:, cols[3]:cols[4]].T, jnp.zeros((16 - N_HEADS, d), BF16)], axis=0)
        w_f_pad = jnp.concatenate([w_f_t, jnp.zeros((AUG - 16, d), BF16)], axis=0)
        row = lambda a: a[l].reshape(1, -1)
        return dict(
            w_qkv=w_qkv, w_gates=w_gates, w_f_t=w_f_t, w_f_pad=w_f_pad, w_qkv_t=w_qkv.T, w_gates_t=w_gates.T,
            conv_w=conv_w_full[l].reshape(3, 2, -1).transpose(1, 0, 2), conv_b=conv_b[l].reshape(2, 1, -1),
            b_f=b_f[l], rel_bias=rel_bias[l],
            ln1_g=row(ln1_g), ln1_b=row(ln1_b), ln2_g=row(ln2_g), ln2_b=row(ln2_b))

    def rest_weights(parts):
        out = {}
        for n, p in zip(REST, parts):
            out[n] = whole(n, p)
            out[n + "_t"] = out[n].T
        return out

    xs, saved, layers = x0, [], []
    hs = _ln_mod(x0, mod_mine[0][1], mod_mine[0][0], "ln_mod1")
    w_in_parts = _all_gather([shard("w_in", 0)], "gather_weights")[0]
    for l in range(depth):
        layers.append(first_weights(l, w_in_parts))
        nxt = mod_mine[min(l + 1, depth - 1)]
        gather = [shard(n, l) for n in REST] + ([shard("w_in", l + 1)] if l + 1 < depth else [])
        xs, hs, sv, extra = _layer_fwd(xs, hs, layers[l], rest_weights, list(mod_mine[l]), (nxt[1], nxt[0]), alpha,
                                       gather)
        w_in_parts = extra[0] if extra else None
        saved.append(sv)
    dxs, loss_part = _loss_grad(xs, target, "loss_grad")
    loss = lax.psum(loss_part[0, 0], AXES)
    grads, big_parts, send = [None] * depth, {}, []
    for l in reversed(range(depth)):
        dxs, grads[l], landed, landed_late = _layer_bwd(dxs, saved[l], layers[l], list(mod_mine[l]), alpha, send,
                                                        ride_late=(l == 0))
        for n, part in zip([(n, l + 1) for n in LATE if send] + [(n, l) for n in EARLY], landed):
            big_parts[n] = part
        for n, part in zip(LATE, landed_late):
            big_parts[n, l] = part
        send = [_slab(grads[l][n], n) for n in LATE] if l else []
    grad_x = dxs.reshape(1, s, d)
    stack = lambda n: jnp.stack([grads[l][n] for l in range(depth)])

    rep = ["b_ada"] + [n for n in SMALL if n != "b_ada"]
    small_g = [stack("mod")] + [stack(n) for n in rep[1:]] + [stack("conv_w")]
    small_pack, small_g_rows = _pack(small_g, mult=8)
    small_parts = _all_gather([small_pack], "gather_small_grads")[0]
    n_rep_rows = sum(small_g_rows[:-1])
    dmod_all = _unpack(small_parts, small_g_rows[:1], [(depth, N_MOD * d)], lead=(N_DEV,))[0]
    cw_all = _unpack(small_parts[:, n_rep_rows:], small_g_rows[-1:], [small_g[-1].shape], lead=(N_DEV,))[0]

    out = {}
    for n in BIG:
        rows2d = lambda a: a.reshape(-1, a.shape[-1])
        res = _adamw([big_parts[n, l] for l in range(depth)], rows2d(w[n]), rows2d(m[n]), rows2d(v[n]),
                     "adamw_" + n, tr=256)
        for kind, r in zip(("grad", "delta", "new_m", "new_v"), res):
            out[kind, n] = r.reshape(w[n].shape)

    dmod_mine = lax.dynamic_slice_in_dim(dmod_all, me * n_ada, n_ada, axis=2)
    g_ada = jnp.stack([_mm(cond_all.T, dmod_mine[:, l], out_dtype=F32, name="g_w_ada", tk=N_DEV, tn=n_ada,
                           precision=HI) for l in range(depth)])
    lp = [_rows128(t["w_ada"], 8) for t in (w, m, v)]
    res = _adamw([_rows128(g_ada, 8)[None]], *lp, "adamw_ada")
    for kind, r in zip(("grad", "delta", "new_m", "new_v"), res):
        out[kind, "w_ada"] = _unpack(r, [lp[0].shape[0]], [w_ada.shape])[0]

    rp = [_pack([t[n] for n in rep], mult=8)[0] for t in (w, m, v)]
    res = _adamw([small_parts[:, :n_rep_rows]], *rp, "adamw_small")
    for kind, r in zip(("grad", "delta", "new_m", "new_v"), res):
        for n, a in zip(rep, _unpack(r, small_g_rows[:-1], [w[n].shape for n in rep])):
            out[kind, n] = a

    n_cw = conv_w.shape[2]
    cw_mine = lax.dynamic_slice_in_dim(cw_all, me * n_cw, n_cw, axis=3)
    cp = [_rows128(t["conv_w"], 8) for t in (w, m, v)]
    cw_in = jnp.stack([_rows128(cw_mine[i], 8) for i in range(N_DEV)])
    res = _adamw([cw_in], *cp, "adamw_conv_w")
    for kind, r in zip(("grad", "delta", "new_m", "new_v"), res):
        out[kind, "conv_w"] = _unpack(r, [cp[0].shape[0]], [conv_w.shape])[0]

    result = [loss, grad_x]
    for kind in ("grad", "delta", "new_m", "new_v"):
        result += [out[kind, n] for n in ORDER]
    return tuple(result)
```

```python
import functools
import math

import jax
import jax.numpy as jnp
import numpy as np
from jax import lax
from jax.experimental import pallas as pl
from jax.experimental.pallas import tpu as pltpu

F32 = jnp.float32
BF16 = jnp.bfloat16
HI = lax.Precision.HIGHEST
MESH_ID = pl.DeviceIdType.MESH
AXES = ("x", "y", "c")
N_DEV = 8

N_HEADS = 8
HEAD_DIM = 64
WIDTH = N_HEADS * HEAD_DIM
CHUNK = 64
LEFT_CHUNKS = 8
BAND = (LEFT_CHUNKS + 1) * CHUNK
REL_CLIP = 128
LN_EPS = 1e-5
N_MOD = 6
QK_SCALE = 1.0 / math.sqrt(HEAD_DIM)
LOG2E = math.log2(math.e)
LN2 = math.log(2.0)
QK_SCALE2 = QK_SCALE * LOG2E
NEG = -1e30
AUG = 128
N_SPLIT = 3
TQ = 512
FOX_FWD_T = 1024
FOX_BWD_T = 1024
SUB = 2 * CHUNK
SUBK = BAND + CHUNK

ADAM_LR, ADAM_B1, ADAM_B2, ADAM_EPS, ADAM_WD, ADAM_STEP = 0.001, 0.9, 0.999, 1e-08, 0.01, 10

VMEM_LIMIT = 56 * 2 ** 20


def _pcall(body, **kw):
    return pl.pallas_call(body, **kw)


def _cp(*sem):
    return pltpu.CompilerParams(dimension_semantics=sem if sem else None, vmem_limit_bytes=VMEM_LIMIT)


def _my_place():
    return lax.axis_index("x"), lax.axis_index("y"), lax.axis_index("c")


N_PEER = N_DEV - 1


def _comm_call(body, xs, out_shapes, name):
    n = len(xs)
    hbm = pl.BlockSpec(memory_space=pltpu.HBM)
    return _pcall(
        functools.partial(body, n), name=name, out_shape=out_shapes, in_specs=[hbm] * n, out_specs=[hbm] * n,
        scratch_shapes=[pltpu.SemaphoreType.DMA((N_PEER * n,)), pltpu.SemaphoreType.DMA((N_PEER * n,)),
                        pltpu.SemaphoreType.DMA((n,))],
    )(*xs)


def _all_gather(xs, name):
    def body(n, *refs):
        x_refs, out_refs, (send_sems, recv_sems, local_sems) = refs[:n], refs[n:2 * n], refs[2 * n:]
        mx, my, mc = _my_place()
        me, sibling = (mx, my, mc), (mx, my, 1 - mc)
        chips = [(1 - mx, my), (mx, 1 - my), (1 - mx, 1 - my)]

        def slot(a, px, py, pc):
            return out_refs[a].at[4 * px + 2 * py + pc]

        def copy(a, k, block, to, own=False):
            return pltpu.make_async_remote_copy(
                src_ref=x_refs[a] if own else slot(a, *block), dst_ref=slot(a, *block),
                send_sem=send_sems.at[a * N_PEER + k], recv_sem=recv_sems.at[a * N_PEER + k],
                device_id=to, device_id_type=MESH_ID)

        arrays = range(n)
        mine = [pltpu.make_async_copy(x_refs[a], slot(a, *me), local_sems.at[a]) for a in arrays]
        first = [copy(a, 0, me, sibling, own=True) for a in arrays]
        first += [copy(a, 1 + j, me, (*chip, mc), own=True) for a in arrays for j, chip in enumerate(chips)]
        for cp in mine + first:
            cp.start()
        passed = []
        for a in arrays:
            for j, chip in enumerate(chips):
                copy(a, 1 + j, (*chip, mc), me).wait_recv()
                passed.append(copy(a, 4 + j, (*chip, mc), sibling))
                passed[-1].start()
        for a in arrays:
            copy(a, 0, sibling, me).wait_recv()
            for j, chip in enumerate(chips):
                copy(a, 4 + j, (*chip, 1 - mc), me).wait_recv()
        for cp in first + passed:
            cp.wait_send()
        for cp in mine:
            cp.wait()

    return _comm_call(body, xs, [jax.ShapeDtypeStruct((N_DEV,) + x.shape, x.dtype) for x in xs], name)


def _direct_exchange(scatter, x_refs, out_refs, send_sems, recv_sems, local_sems):
    mx, my, mc = _my_place()
    me = 4 * mx + 2 * my + mc
    mine, copies = [], []
    for a, (x_ref, out_ref) in enumerate(zip(x_refs, out_refs)):
        mine.append(pltpu.make_async_copy(x_ref.at[me] if scatter else x_ref, out_ref.at[me], local_sems.at[a]))
        for k in range(N_PEER):
            px, py, pc = mx ^ ((k + 1) >> 2), my ^ (((k + 1) >> 1) & 1), mc ^ ((k + 1) & 1)
            copies.append(pltpu.make_async_remote_copy(
                src_ref=x_ref.at[4 * px + 2 * py + pc] if scatter else x_ref, dst_ref=out_ref.at[me],
                send_sem=send_sems.at[a * N_PEER + k], recv_sem=recv_sems.at[a * N_PEER + k],
                device_id=(px, py, pc), device_id_type=MESH_ID))

    def start():
        for cp in mine + copies:
            cp.start()

    def wait():
        for cp in copies:
            cp.wait_recv()
        for cp in copies:
            cp.wait_send()
        for cp in mine:
            cp.wait()

    return start, wait


def _exchange_shapes(scatter, xs):
    return [jax.ShapeDtypeStruct(x.shape if scatter else (N_DEV,) + x.shape, x.dtype) for x in xs]


def _exchange_scratch(n):
    return [pltpu.SemaphoreType.DMA((N_PEER * n,)), pltpu.SemaphoreType.DMA((N_PEER * n,)),
            pltpu.SemaphoreType.DMA((n,))]


def _tile(n, pref, unit=128):
    if n <= pref:
        return n
    t = pref - pref % unit
    while t > unit and n % t:
        t -= unit
    assert n % t == 0, (n, pref, unit)
    return t


def _mm(a, b, *, out_dtype, name, ta=False, tb=False, tm=1024, tn=1024, tk=512, precision=None,
        a_halves=False, b_halves=False, o_halves=False, exchange=None):
    a_shape = (a.shape[1], 2 * a.shape[2]) if a_halves else a.shape
    b_shape = (b.shape[1], 2 * b.shape[2]) if b_halves else b.shape
    (k_dim, m) = a_shape if ta else a_shape[::-1]
    n = b_shape[0] if tb else b_shape[1]
    tm, tn = _tile(m, tm, 8), _tile(n // 2 if (b_halves or o_halves) else n, tn)
    tk = _tile(k_dim // 2 if a_halves else k_dim, tk)
    nk = k_dim // tk
    dims = (((0 if ta else 1,), (1 if tb else 0,)), ((), ()))
    grid = (m // tm, n // tn, nk)
    ex_in, ex_out, ex_specs, ex_scratch, hook = _riding_exchange(exchange, 2, 1, grid)

    def body(*refs):
        (a_ref, b_ref, o_ref, acc_ref), finish = hook(refs)
        k = pl.program_id(2)
        if nk == 1:
            o_ref[...] = lax.dot_general(a_ref[...], b_ref[...], dims, preferred_element_type=F32,
                                         precision=precision).astype(o_ref.dtype)
            finish()
            return

        @pl.when(k == 0)
        def _():
            acc_ref[...] = jnp.zeros_like(acc_ref)

        acc_ref[...] += lax.dot_general(a_ref[...], b_ref[...], dims, preferred_element_type=F32,
                                        precision=precision)

        @pl.when(k == nk - 1)
        def _():
            o_ref[...] = acc_ref[...].astype(o_ref.dtype)

        finish()

    def spec(shape2, pick, halves, per_half):
        if not halves:
            return pl.BlockSpec(shape2, pick)

        def index(i, j, k):
            r, c = pick(i, j, k)
            return (c // per_half, r, c % per_half)
        return pl.BlockSpec((None,) + shape2, index)

    assert not (ta and a_halves) and not (tb and b_halves)
    a_spec = spec((tk, tm), lambda i, j, k: (k, i), False, 0) if ta else \
        spec((tm, tk), lambda i, j, k: (i, k), a_halves, (k_dim // 2) // tk if a_halves else 0)
    b_spec = spec((tn, tk), lambda i, j, k: (j, k), False, 0) if tb else \
        spec((tk, tn), lambda i, j, k: (k, j), b_halves, (n // 2) // tn if b_halves else 0)
    o_spec = spec((tm, tn), lambda i, j, k: (i, j), o_halves, (n // 2) // tn if o_halves else 0)
    out_shape = (2, m, n // 2) if o_halves else (m, n)
    res = _pcall(
        body, name=name,
        out_shape=[jax.ShapeDtypeStruct(out_shape, out_dtype)] + ex_out,
        grid=grid,
        in_specs=[a_spec, b_spec] + ex_specs,
        out_specs=[o_spec] + ex_specs,
        scratch_shapes=[pltpu.VMEM((tm, tn), F32)] + ex_scratch,
        compiler_params=_cp(*(["arbitrary"] * 3 if exchange else ["parallel", "parallel", "arbitrary"])),
    )(a, b, *ex_in)
    return res if exchange else res[0]


def _ln_stats(x):
    mu = jnp.mean(x, axis=-1, keepdims=True)
    xc = x - mu
    var = jnp.mean(xc * xc, axis=-1, keepdims=True)
    rstd = lax.rsqrt(var + LN_EPS)
    return xc * rstd, rstd


def _ln_mod(x, sc, sh, name, tm=512):
    s, d = x.shape
    tm = min(tm, s)

    def body(x_ref, sc_ref, sh_ref, h_ref):
        xhat, _ = _ln_stats(x_ref[...])
        h_ref[...] = (xhat * (1.0 + sc_ref[...]) + sh_ref[...]).astype(h_ref.dtype)

    row = pl.BlockSpec((1, d), lambda i: (0, 0))
    return _pcall(
        body, name=name, out_shape=jax.ShapeDtypeStruct((s, d), BF16), grid=(s // tm,),
        in_specs=[pl.BlockSpec((tm, d), lambda i: (i, 0)), row, row],
        out_specs=pl.BlockSpec((tm, d), lambda i: (i, 0)),
        compiler_params=_cp("parallel"),
    )(x, sc, sh)


def _mm_res_ln(a, w, xres, gate, ln_g, ln_b, sc, sh, alpha, name, tm=512, tk=512):
    s, k_dim = a.shape
    d = w.shape[1]
    tm, tk = _tile(s, tm, 8), _tile(k_dim, tk)
    nk = k_dim // tk

    def body(a_ref, w_ref, x_ref, g_ref, lg_ref, lb_ref, sc_ref, sh_ref, y_ref, xn_ref, h_ref, acc_ref):
        k = pl.program_id(1)

        @pl.when(k == 0)
        def _():
            acc_ref[...] = jnp.zeros_like(acc_ref)

        acc_ref[...] += jnp.dot(a_ref[...], w_ref[...], preferred_element_type=F32)

        @pl.when(k == nk - 1)
        def _():
            y = acc_ref[...]
            y_ref[...] = y
            z = alpha * x_ref[...] + (1.0 + g_ref[...]) * y
            zhat, _ = _ln_stats(z)
            xn = zhat * lg_ref[...] + lb_ref[...]
            xn_ref[...] = xn
            xhat, _ = _ln_stats(xn)
            h_ref[...] = (xhat * (1.0 + sc_ref[...]) + sh_ref[...]).astype(h_ref.dtype)

    row = pl.BlockSpec((1, d), lambda i, k: (0, 0))
    tile = pl.BlockSpec((tm, d), lambda i, k: (i, 0))
    return _pcall(
        body, name=name,
        out_shape=(jax.ShapeDtypeStruct((s, d), F32), jax.ShapeDtypeStruct((s, d), F32),
                   jax.ShapeDtypeStruct((s, d), BF16)),
        grid=(s // tm, nk),
        in_specs=[pl.BlockSpec((tm, tk), lambda i, k: (i, k)), pl.BlockSpec((tk, d), lambda i, k: (k, 0)),
                  tile, row, row, row, row, row],
        out_specs=(tile, tile, tile),
        scratch_shapes=[pltpu.VMEM((tm, d), F32)],
        compiler_params=_cp("parallel", "arbitrary"),
    )(a, w, xres, gate, ln_g, ln_b, sc, sh)


def _colsum(v):
    return jnp.sum(v, axis=0, keepdims=True)


def _ln_res_bwd(dxn, xres, y, gate, ln_g, alpha, name, tm=512):
    s, d = dxn.shape
    tm = min(tm, s)

    def body(dxn_ref, x_ref, y_ref, g_ref, lg_ref, dres_ref, dy_ref, sums_ref):
        @pl.when(pl.program_id(0) == 0)
        def _():
            sums_ref[...] = jnp.zeros_like(sums_ref)

        dxn_v, y_v = dxn_ref[...], y_ref[...]
        one_g = 1.0 + g_ref[...]
        zhat, rstd = _ln_stats(alpha * x_ref[...] + one_g * y_v)
        dzh = dxn_v * lg_ref[...]
        dz = rstd * (dzh - jnp.mean(dzh, axis=-1, keepdims=True)
                     - zhat * jnp.mean(dzh * zhat, axis=-1, keepdims=True))
        dres_ref[...] = alpha * dz
        dy_ref[...] = (one_g * dz).astype(dy_ref.dtype)
        sums_ref[0:1, :] += _colsum(dxn_v * zhat)
        sums_ref[1:2, :] += _colsum(dxn_v)
        sums_ref[2:3, :] += _colsum(dz * y_v)

    row = pl.BlockSpec((1, d), lambda i: (0, 0))
    tile = pl.BlockSpec((tm, d), lambda i: (i, 0))
    return _pcall(
        body, name=name,
        out_shape=(jax.ShapeDtypeStruct((s, d), F32), jax.ShapeDtypeStruct((s, d), BF16),
                   jax.ShapeDtypeStruct((8, d), F32)),
        grid=(s // tm,),
        in_specs=[tile, tile, tile, row, row],
        out_specs=(tile, tile, pl.BlockSpec((8, d), lambda i: (0, 0))),
        compiler_params=_cp("arbitrary"),
    )(dxn, xres, y, gate, ln_g)


def _ln_mod_bwd(dh, x, sc, dres, name, extra=None, tm=512):
    s, d = dh.shape
    tm = min(tm, s)

    def body(*refs):
        if extra is None:
            dh_ref, x_ref, sc_ref, dres_ref, dx_ref, sums_ref = refs
            dh_v = dh_ref[...]
        else:
            dh_ref, x_ref, sc_ref, dres_ref, dh2_ref, df_ref, wf_ref, dx_ref, sums_ref = refs
            dh_v = dh_ref[...] + dh2_ref[...] + jnp.dot(df_ref[...], wf_ref[...], preferred_element_type=F32)

        @pl.when(pl.program_id(0) == 0)
        def _():
            sums_ref[...] = jnp.zeros_like(sums_ref)

        xhat, rstd = _ln_stats(x_ref[...])
        dxh = dh_v * (1.0 + sc_ref[...])
        dx_ref[...] = dres_ref[...] + rstd * (dxh - jnp.mean(dxh, axis=-1, keepdims=True)
                                             - xhat * jnp.mean(dxh * xhat, axis=-1, keepdims=True))
        sums_ref[0:1, :] += _colsum(dh_v)
        sums_ref[1:2, :] += _colsum(dh_v * xhat)

    row = pl.BlockSpec((1, d), lambda i: (0, 0))
    tile = pl.BlockSpec((tm, d), lambda i: (i, 0))
    in_specs = [tile, tile, row, tile]
    args = [dh, x, sc, dres]
    if extra is not None:
        in_specs += [tile, pl.BlockSpec((tm, AUG), lambda i: (i, 0)), pl.BlockSpec((AUG, d), lambda i: (0, 0))]
        args += list(extra)
    return _pcall(
        body, name=name,
        out_shape=(jax.ShapeDtypeStruct((s, d), F32), jax.ShapeDtypeStruct((8, d), F32)),
        grid=(s // tm,), in_specs=in_specs,
        out_specs=(tile, pl.BlockSpec((8, d), lambda i: (0, 0))),
        compiler_params=_cp("arbitrary"),
    )(*args)


def _loss_grad(xn, target, name, tm=512):
    s, d = xn.shape
    tm = min(tm, s)
    n = s // tm

    def body(x_ref, t_ref, dx_ref, loss_ref, acc_ref):
        i = pl.program_id(0)

        @pl.when(i == 0)
        def _():
            acc_ref[...] = jnp.zeros_like(acc_ref)

        err = x_ref[...] - t_ref[...]
        dx_ref[...] = err * (1.0 / d)
        acc_ref[...] += _colsum(err * err)

        @pl.when(i == n - 1)
        def _():
            loss_ref[...] = jnp.zeros_like(loss_ref) + (0.5 / d) * jnp.sum(acc_ref[...])

    tile = pl.BlockSpec((tm, d), lambda i: (i, 0))
    return _pcall(
        body, name=name,
        out_shape=(jax.ShapeDtypeStruct((s, d), F32), jax.ShapeDtypeStruct((8, 128), F32)),
        grid=(n,), in_specs=[tile, tile],
        out_specs=(tile, pl.BlockSpec((8, 128), lambda i: (0, 0))),
        scratch_shapes=[pltpu.VMEM((1, d), F32)],
        compiler_params=_cp("arbitrary"),
    )(xn, target)


def _sigmoid(v):
    return 0.5 * jnp.tanh(0.5 * v) + 0.5


def _merge_fwd(o_a, o_c, w_a, w_c, gates, name, tm=512):
    s, wd = o_a.shape
    d = w_a.shape[1]
    tm = min(tm, s)

    def body(oa_ref, oc_ref, wa_ref, wc_ref, ga_ref, gc_ref, m_ref):
        ba = jnp.dot(oa_ref[...], wa_ref[...], preferred_element_type=F32)
        bc = jnp.dot(oc_ref[...], wc_ref[...], preferred_element_type=F32)
        m_ref[...] = (_sigmoid(ga_ref[...]) * ba + _sigmoid(gc_ref[...]) * bc).astype(m_ref.dtype)

    o_spec = pl.BlockSpec((tm, wd), lambda i: (i, 0))
    w_spec = pl.BlockSpec((wd, d), lambda i: (0, 0))
    return _pcall(
        body, name=name, out_shape=jax.ShapeDtypeStruct((s, d), BF16), grid=(s // tm,),
        in_specs=[o_spec, o_spec, w_spec, w_spec,
                  pl.BlockSpec((tm, d), lambda i: (i, 0)), pl.BlockSpec((tm, d), lambda i: (i, 1))],
        out_specs=pl.BlockSpec((tm, d), lambda i: (i, 0)),
        compiler_params=_cp("parallel"),
    )(o_a, o_c, w_a, w_c, gates, gates)


def _merge_bwd(dmix, w_out_t, o_a, o_c, w_a, w_c, w_a_t, w_c_t, gates, name, tm=512):
    s, wd = o_a.shape
    d = w_a.shape[1]
    tm = min(tm, s)

    def body(dmix_ref, wot_ref, oa_ref, oc_ref, wa_ref, wc_ref, wat_ref, wct_ref, ga_ref, gc_ref,
             dba_ref, dbc_ref, doa_ref, doc_ref, dg_ref):
        dm = jnp.dot(dmix_ref[...], wot_ref[...], preferred_element_type=F32)
        for half, (o_ref, w_ref, wt_ref, g_ref, db_ref, do_ref) in enumerate((
                (oa_ref, wa_ref, wat_ref, ga_ref, dba_ref, doa_ref),
                (oc_ref, wc_ref, wct_ref, gc_ref, dbc_ref, doc_ref))):
            b = jnp.dot(o_ref[...], w_ref[...], preferred_element_type=F32)
            sg = _sigmoid(g_ref[...])
            db = (dm * sg).astype(BF16)
            db_ref[...] = db
            dg_ref[:, d * half:d * (half + 1)] = (dm * b * sg * (1.0 - sg)).astype(dg_ref.dtype)
            do_ref[...] = jnp.dot(db, wt_ref[...], preferred_element_type=F32).astype(do_ref.dtype)

    row_d = pl.BlockSpec((tm, d), lambda i: (i, 0))
    row_w = pl.BlockSpec((tm, wd), lambda i: (i, 0))
    full = lambda shp: pl.BlockSpec(shp, lambda i: (0, 0))
    return _pcall(
        body, name=name,
        out_shape=(jax.ShapeDtypeStruct((s, d), BF16), jax.ShapeDtypeStruct((s, d), BF16),
                   jax.ShapeDtypeStruct((s, wd), BF16), jax.ShapeDtypeStruct((s, wd), BF16),
                   jax.ShapeDtypeStruct((s, 2 * d), BF16)),
        grid=(s // tm,),
        in_specs=[row_d, full((d, d)), row_w, row_w, full((wd, d)), full((wd, d)), full((d, wd)), full((d, wd)),
                  row_d, pl.BlockSpec((tm, d), lambda i: (i, 1))],
        out_specs=(row_d, row_d, row_w, row_w, pl.BlockSpec((tm, 2 * d), lambda i: (i, 0))),
        compiler_params=_cp("parallel"),
    )(dmix, w_out_t, o_a, o_c, w_a, w_c, w_a_t, w_c_t, gates, gates)


def _shift_down(cur, k, fill_rows):
    out = pltpu.roll(cur, k, axis=0)
    rid = lax.broadcasted_iota(jnp.int32, cur.shape, 0)
    for r, fill in enumerate(fill_rows):
        out = jnp.where(rid == r, fill, out)
    return out


def _shift_up(cur, k, fill_rows):
    n = cur.shape[0]
    out = pltpu.roll(cur, n - k, axis=0)
    rid = lax.broadcasted_iota(jnp.int32, cur.shape, 0)
    for r, fill in enumerate(fill_rows):
        out = jnp.where(rid == n - k + r, fill, out)
    return out


def _conv_rows(cur, prev8, first, w, b):
    p6 = jnp.where(first, 0.0, prev8[6:7, :])
    p7 = jnp.where(first, 0.0, prev8[7:8, :])
    m1 = _shift_down(cur, 1, [p7])
    m2 = _shift_down(cur, 2, [p6, p7])
    u = b + w[0:1, :] * m2 + w[1:2, :] * m1 + w[2:3, :] * cur
    return u, m1, m2


def _conv_specs(tr, tc, order):
    r8 = tr // 8
    at = lambda f: (lambda *g: f(*order(*g)))
    return [pl.BlockSpec((2, tr, tc), at(lambda i, j: (0, i, j))),
            pl.BlockSpec((2, 8, tc), at(lambda i, j: (0, jnp.maximum(i * r8 - 1, 0), j))),
            pl.BlockSpec((2, 3, tc), at(lambda i, j: (0, 0, j))),
            pl.BlockSpec((2, 1, tc), at(lambda i, j: (0, 0, j)))]


def _conv_act_fwd(u0, conv_w, conv_b, name, tr=512, tc=256):
    _, s, f = u0.shape
    tr, tc = _tile(s, tr, 8), _tile(f, tc)

    def body(u_ref, p_ref, w_ref, b_ref, act_ref):
        first = pl.program_id(0) == 0
        a, _, _ = _conv_rows(u_ref[0], p_ref[0], first, w_ref[0], b_ref[0])
        v, _, _ = _conv_rows(u_ref[1], p_ref[1], first, w_ref[1], b_ref[1])
        act_ref[...] = (a * _sigmoid(a) * v).astype(act_ref.dtype)

    return _pcall(
        body, name=name, out_shape=jax.ShapeDtypeStruct((s, f), BF16), grid=(s // tr, f // tc),
        in_specs=_conv_specs(tr, tc, lambda i, j: (i, j)),
        out_specs=pl.BlockSpec((tr, tc), lambda i, j: (i, j)),
        compiler_params=_cp("parallel", "parallel"),
    )(u0, u0, conv_w, conv_b)


def _act_grads(dact, a, v):
    sg = _sigmoid(a)
    return dact * v * sg * (1.0 + a * (1.0 - sg)), dact * a * sg


def _conv_act_bwd(dact, u0, conv_w, conv_b, name, tr=512, tc=256):
    _, s, f = u0.shape
    tr, tc = _tile(s, tr, 8), _tile(f, tc)
    n = s // tr
    r8 = tr // 8

    def body(dact_ref, dnext_ref, u_ref, p_ref, unext_ref, w_ref, b_ref, du0_ref, sums_ref):
        i = pl.program_id(1)
        first, last = i == 0, i == n - 1

        @pl.when(first)
        def _():
            sums_ref[...] = jnp.zeros_like(sums_ref)

        cur = (u_ref[0], u_ref[1])
        (a, a1, a2), (v, v1, v2) = [_conv_rows(cur[hf], p_ref[hf], first, w_ref[hf], b_ref[hf]) for hf in range(2)]
        da, dv = _act_grads(dact_ref[...], a, v)
        (an, _, _), (vn, _, _) = [_conv_rows(unext_ref[hf], cur[hf][tr - 8:tr, :], False, w_ref[hf], b_ref[hf])
                                  for hf in range(2)]
        dan, dvn = _act_grads(dnext_ref[...], an, vn)
        for hf, g, gn, shifted in ((0, da, dan, (a2, a1, cur[0])), (1, dv, dvn, (v2, v1, cur[1]))):
            w = w_ref[hf]
            n0 = jnp.where(last, 0.0, gn[0:1, :])
            n1 = jnp.where(last, 0.0, gn[1:2, :])
            du0 = w[2:3, :] * g + w[1:2, :] * _shift_up(g, 1, [n0]) + w[0:1, :] * _shift_up(g, 2, [n0, n1])
            du0_ref[hf] = du0.astype(du0_ref.dtype)
            for r in range(3):
                sums_ref[hf, r:r + 1, :] += _colsum(g * shifted[r])
            sums_ref[hf, 3:4, :] += _colsum(g)

    nxt = lambda i: jnp.minimum((i + 1) * r8, s // 8 - 1)
    order = lambda j, i: (i, j)
    specs = _conv_specs(tr, tc, order)
    return _pcall(
        body, name=name,
        out_shape=(jax.ShapeDtypeStruct((2, s, f), BF16), jax.ShapeDtypeStruct((2, 8, f), F32)),
        grid=(f // tc, n),
        in_specs=[pl.BlockSpec((tr, tc), lambda j, i: (i, j)), pl.BlockSpec((8, tc), lambda j, i: (nxt(i), j)),
                  specs[0], specs[1], pl.BlockSpec((2, 8, tc), lambda j, i: (0, nxt(i), j)), specs[2], specs[3]],
        out_specs=(pl.BlockSpec((2, tr, tc), lambda j, i: (0, i, j)), pl.BlockSpec((2, 8, tc), lambda j, i: (0, 0, j))),
        compiler_params=_cp("parallel", "arbitrary"),
    )(dact, dact, u0, u0, u0, conv_w, conv_b)


def _cumsum_rows(v, reverse=False):
    r = v.shape[0]
    i128 = lax.broadcasted_iota(jnp.int32, (128, 128), 0), lax.broadcasted_iota(jnp.int32, (128, 128), 1)
    ir = lax.broadcasted_iota(jnp.int32, (r, r), 0), lax.broadcasted_iota(jnp.int32, (r, r), 1)
    in_row = (i128[0] >= i128[1] if reverse else i128[0] <= i128[1]).astype(F32)
    rows = (ir[1] > ir[0] if reverse else ir[1] < ir[0]).astype(F32)
    within = jnp.dot(v, in_row, preferred_element_type=F32, precision=HI)
    tot = jnp.broadcast_to(within[:, 0:1] if reverse else within[:, 127:128], (r, 128))
    return within + jnp.dot(rows, tot, preferred_element_type=F32, precision=HI)


def _forget_fwd(f_t, b_f, name):
    h, r, _ = f_t.shape

    def body(f_ref, b_ref, o_ref):
        z = f_ref[...] + b_ref[...]
        logf = jnp.minimum(z, 0.0) - jnp.log(1.0 + jnp.exp(-jnp.abs(z)))
        rest = _cumsum_rows(logf) * LOG2E
        for i in range(N_SPLIT):
            piece = rest.astype(BF16).astype(F32)
            o_ref[i] = piece
            rest = rest - piece

    return _pcall(
        body, name=name, out_shape=jax.ShapeDtypeStruct((N_SPLIT, h, r, 128), F32), grid=(h,),
        in_specs=[pl.BlockSpec((None, r, 128), lambda i: (i, 0, 0)), pl.BlockSpec((None, 1, 128), lambda i: (i, 0, 0))],
        out_specs=pl.BlockSpec((N_SPLIT, None, r, 128), lambda i: (0, i, 0, 0)),
        compiler_params=_cp("parallel"),
    )(f_t, b_f)


def _forget_bwd(d_cum, f_t, b_f, name):
    h, r, _ = f_t.shape

    def body(g_ref, f_ref, b_ref, df_ref, db_ref):
        df = _cumsum_rows(g_ref[...], reverse=True) * _sigmoid(-(f_ref[...] + b_ref[...]))
        df_ref[...] = df
        db_ref[...] = jnp.zeros_like(db_ref) + jnp.sum(df)

    blk = pl.BlockSpec((None, r, 128), lambda i: (i, 0, 0))
    one = pl.BlockSpec((None, 1, 128), lambda i: (i, 0, 0))
    return _pcall(
        body, name=name,
        out_shape=(jax.ShapeDtypeStruct((h, r, 128), F32), jax.ShapeDtypeStruct((h, 1, 128), F32)),
        grid=(h,), in_specs=[blk, blk, one], out_specs=(blk, one),
        compiler_params=_cp("parallel"),
    )(d_cum, f_t, b_f)


_NT = (((1,), (1,)), ((), ()))


def _causal_keep(tk, tq):
    return lax.broadcasted_iota(jnp.int32, (tk, tq), 0) <= lax.broadcasted_iota(jnp.int32, (tk, tq), 1)


_TN = (((0,), (0,)), ((), ()))
PAIR = 2
V_FOX = 2 * WIDTH // 128
Q_CHUNK, K_CHUNK, V_CHUNK = (3 * WIDTH) // 128, (4 * WIDTH) // 128, (5 * WIDTH) // 128


def _riding_exchange(exchange, n_in, n_out, grid):
    if exchange is None:
        return [], [], [], [], lambda refs: (refs, lambda: None)
    scatter, xs = exchange
    n = len(xs)
    hbm = pl.BlockSpec(memory_space=pltpu.HBM)

    def hook(refs):
        ins, x_refs = refs[:n_in], refs[n_in:n_in + n]
        outs, land_refs = refs[n_in + n:n_in + n + n_out], refs[n_in + n + n_out:n_in + 2 * n + n_out]
        rest = refs[n_in + 2 * n + n_out:]
        own_scratch, sems = rest[:len(rest) - 3], rest[len(rest) - 3:]
        start, wait = _direct_exchange(scatter, x_refs, land_refs, *sems)
        ids = [pl.program_id(a) for a in range(len(grid))]
        is_first = functools.reduce(jnp.logical_and, [i == 0 for i in ids])
        is_last = functools.reduce(jnp.logical_and, [i == g - 1 for i, g in zip(ids, grid)])
        pl.when(is_first)(start)
        return tuple(ins) + tuple(outs) + tuple(own_scratch), lambda: pl.when(is_last)(wait)

    return list(xs), _exchange_shapes(scatter, xs), [hbm] * n, _exchange_scratch(n), hook


def _fox_fwd(q_aug, k_aug, qkv, name, t=TQ, exchange=None):
    s = q_aug.shape[0]
    t = min(t, s)
    n = s // t
    dh = HEAD_DIM
    grid = (N_HEADS // PAIR, n)
    ex_in, ex_out, ex_specs, ex_scratch, hook = _riding_exchange(exchange, 3, 2, grid)

    def body(*refs):
        (q_ref, k_ref, v_ref, o_ref, lse_ref, m_s, l_s, acc_s), finish = hook(refs)
        qi = pl.program_id(1)
        m_s[...] = jnp.full_like(m_s, NEG)
        l_s[...] = jnp.zeros_like(l_s)
        acc_s[...] = jnp.zeros_like(acc_s)

        def step(kj, diag):
            rows = pl.ds(pl.multiple_of(kj * t, t), t)
            for hh in range(PAIR):
                st = lax.dot_general(k_ref[rows, AUG * hh:AUG * (hh + 1)], q_ref[:, AUG * hh:AUG * (hh + 1)], _NT,
                                     preferred_element_type=F32)
                if diag:
                    st = jnp.where(_causal_keep(t, t), st, NEG)
                m_prev = m_s[hh]
                m_new = jnp.maximum(m_prev, jnp.max(st, axis=0, keepdims=True))
                a = jnp.exp2(m_prev - m_new)
                pt = jnp.exp2(st - m_new)
                l_s[hh] = a * l_s[hh] + jnp.sum(pt, axis=0, keepdims=True)
                acc_s[hh] = a * acc_s[hh] + lax.dot_general(v_ref[rows, dh * hh:dh * (hh + 1)], pt.astype(BF16), _TN,
                                                            preferred_element_type=F32)
                m_s[hh] = m_new

        def off_diagonal(kj, carry):
            step(kj, False)
            return carry

        lax.fori_loop(0, qi, off_diagonal, 0)
        step(qi, True)
        for hh in range(PAIR):
            o_ref[hh] = acc_s[hh] / l_s[hh]
            lse_ref[hh, 0] = m_s[hh] + jnp.log2(l_s[hh])
        finish()

    return _pcall(
        body, name=name,
        out_shape=[jax.ShapeDtypeStruct((N_HEADS, dh, s), F32), jax.ShapeDtypeStruct((N_HEADS, n, 1, t), F32)] + ex_out,
        grid=grid,
        in_specs=[pl.BlockSpec((t, PAIR * AUG), lambda hp, i: (i, hp)),
                  pl.BlockSpec((s, PAIR * AUG), lambda hp, i: (0, hp)),
                  pl.BlockSpec((s, PAIR * dh), lambda hp, i: (0, V_FOX + hp))] + ex_specs,
        out_specs=[pl.BlockSpec((PAIR, dh, t), lambda hp, i: (hp, 0, i)),
                   pl.BlockSpec((PAIR, 1, 1, t), lambda hp, i: (hp, i, 0, 0))] + ex_specs,
        scratch_shapes=[pltpu.VMEM((PAIR, 1, t), F32), pltpu.VMEM((PAIR, 1, t), F32),
                        pltpu.VMEM((PAIR, dh, t), F32)] + ex_scratch,
        compiler_params=_cp("arbitrary", "arbitrary"),
    )(q_aug, k_aug, qkv, *ex_in)


def _row_dot(a_t, b_t, name, t=2048):
    h, dh, s = a_t.shape
    t = min(t, s)

    def body(a_ref, b_ref, o_ref):
        o_ref[...] = jnp.sum(a_ref[...].astype(F32) * b_ref[...], axis=0, keepdims=True)

    blk = pl.BlockSpec((None, dh, t), lambda hh, i: (hh, 0, i))
    return _pcall(
        body, name=name, out_shape=jax.ShapeDtypeStruct((h, 1, s), F32), grid=(h, s // t),
        in_specs=[blk, blk], out_specs=pl.BlockSpec((None, 1, t), lambda hh, i: (hh, 0, i)),
        compiler_params=_cp("parallel", "parallel"),
    )(a_t, b_t)


def _fox_bwd(q_aug, k_aug, qkv, do, lse, delta, name, t=TQ, exchange=None):
    s = q_aug.shape[0]
    dh = HEAD_DIM
    t = min(t, s)
    n = s // t
    grid = (N_HEADS // PAIR, n)
    ex_in, ex_out, ex_specs, ex_scratch, hook = _riding_exchange(exchange, 6, 5, grid)

    def body(*refs):
        (q_ref, k_ref, v_ref, do_ref, lse_ref, dl_ref, dk_ref, dv_ref, dq_ref, dsum_ref, dqsum_ref,
         dk_s, dv_s, dsum_s), finish = hook(refs)
        kj = pl.program_id(1)

        @pl.when(kj == 0)
        def _():
            dq_ref[...] = jnp.zeros_like(dq_ref)
            dqsum_ref[...] = jnp.zeros_like(dqsum_ref)

        dk_s[...] = jnp.zeros_like(dk_s)
        dv_s[...] = jnp.zeros_like(dv_s)
        dsum_s[...] = jnp.zeros_like(dsum_s)

        def step(qi, diag):
            rows = pl.ds(pl.multiple_of(qi * t, t), t)
            for hh in range(PAIR):
                q = q_ref[rows, AUG * hh:AUG * (hh + 1)]
                k = k_ref[:, AUG * hh:AUG * (hh + 1)]
                st = lax.dot_general(k, q, _NT, preferred_element_type=F32)
                if diag:
                    st = jnp.where(_causal_keep(t, t), st, NEG)
                pt = jnp.exp2(st - lse_ref[hh, qi])
                do_v = do_ref[rows, dh * hh:dh * (hh + 1)]
                dpt = lax.dot_general(v_ref[:, dh * hh:dh * (hh + 1)], do_v, _NT, preferred_element_type=F32)
                ds32 = pt * (dpt - dl_ref[hh, qi])
                dsum_s[hh] += sum(ds32[:, 128 * u:128 * (u + 1)] for u in range(t // 128))
                dqsum_ref[hh, qi] += jnp.sum(ds32, axis=0, keepdims=True)
                dst = ds32.astype(BF16)
                dv_s[hh] += jnp.dot(pt.astype(BF16), do_v, preferred_element_type=F32)
                dk_s[hh] += jnp.dot(dst, q[:, :dh], preferred_element_type=F32)
                dq_ref[hh, qi] += lax.dot_general(k[:, :dh], dst, _TN, preferred_element_type=F32)

        def off_diagonal(qi, carry):
            step(qi, False)
            return carry

        step(kj, True)
        lax.fori_loop(kj + 1, n, off_diagonal, 0)
        for hh in range(PAIR):
            dk_ref[:, dh * hh:dh * (hh + 1)] = (LN2 * dk_s[hh]).astype(dk_ref.dtype)
            dv_ref[:, dh * hh:dh * (hh + 1)] = dv_s[hh].astype(dv_ref.dtype)
            dsum_ref[:, 128 * hh:128 * (hh + 1)] = dsum_s[hh]
        finish()

    stat = pl.BlockSpec((PAIR, n, 1, t), lambda hp, j: (hp, 0, 0, 0))
    return _pcall(
        body, name=name,
        out_shape=[jax.ShapeDtypeStruct((s, WIDTH), BF16), jax.ShapeDtypeStruct((s, WIDTH), BF16),
                   jax.ShapeDtypeStruct((N_HEADS, n, dh, t), F32), jax.ShapeDtypeStruct((s, N_HEADS * 128), F32),
                   jax.ShapeDtypeStruct((N_HEADS, n, 1, t), F32)] + ex_out,
        grid=grid,
        in_specs=[pl.BlockSpec((s, PAIR * AUG), lambda hp, j: (0, hp)),
                  pl.BlockSpec((t, PAIR * AUG), lambda hp, j: (j, hp)),
                  pl.BlockSpec((t, PAIR * dh), lambda hp, j: (j, V_FOX + hp)),
                  pl.BlockSpec((s, PAIR * dh), lambda hp, j: (0, hp)),
                  stat, stat] + ex_specs,
        out_specs=[pl.BlockSpec((t, PAIR * dh), lambda hp, j: (j, hp)),
                   pl.BlockSpec((t, PAIR * dh), lambda hp, j: (j, hp)),
                   pl.BlockSpec((PAIR, n, dh, t), lambda hp, j: (hp, 0, 0, 0)),
                   pl.BlockSpec((t, PAIR * 128), lambda hp, j: (j, hp)),
                   stat] + ex_specs,
        scratch_shapes=[pltpu.VMEM((PAIR, t, dh), F32), pltpu.VMEM((PAIR, t, dh), F32),
                        pltpu.VMEM((PAIR, t, 128), F32)] + ex_scratch,
        compiler_params=_cp("arbitrary", "arbitrary"),
    )(q_aug, k_aug, qkv, do, lse, delta, *ex_in)


N_SUB = TQ // SUB


def _pair_blockdiag(x2):
    lane = lax.broadcasted_iota(jnp.int32, x2.shape, 1)
    zero = jnp.zeros_like(x2)
    return jnp.concatenate([jnp.where(lane < HEAD_DIM, x2, zero), jnp.where(lane >= HEAD_DIM, x2, zero)], axis=0)


def _pair_lanes(ref, j):
    return jnp.concatenate([ref[hh, 0, :, SUB * j:SUB * (j + 1)] for hh in range(PAIR)], axis=1)


def _chunk_logits(kwin, qbd, b_ref, first_tile, j):
    which = jnp.where(first_tile, 1 + j, 0)
    bias = jnp.concatenate([b_ref[hh, which] for hh in range(PAIR)], axis=1)
    return lax.dot_general(kwin, qbd, _NT, preferred_element_type=F32) + bias


def _chunk_fwd(qkv, bias_t, name):
    s = qkv.shape[0]
    dh = HEAD_DIM
    n = s // TQ
    nsub = TQ // SUB
    prev = lambda i: jnp.maximum(i - 1, 0)

    def body(q_ref, kp_ref, kc_ref, vp_ref, vc_ref, b_ref, o_ref, lse_ref):
        first_tile = pl.program_id(1) == 0
        kw = jnp.concatenate([kp_ref[...], kc_ref[...]], axis=0)
        vw = jnp.concatenate([vp_ref[...], vc_ref[...]], axis=0)
        for j in range(nsub):
            rows, win = slice(SUB * j, SUB * (j + 1)), slice(SUB * j, SUB * j + SUBK)
            st = _chunk_logits(kw[win], _pair_blockdiag(q_ref[rows, :] * QK_SCALE2), b_ref, first_tile, j)
            m = jnp.max(st, axis=0, keepdims=True)
            pt = jnp.exp2(st - m)
            l = jnp.sum(pt, axis=0, keepdims=True)
            ot = lax.dot_general(vw[win], pt.astype(BF16), _TN, preferred_element_type=F32)
            stat = m + jnp.log2(l)
            for hh in range(PAIR):
                lanes = slice(SUB * hh, SUB * (hh + 1))
                o_ref[hh, :, rows] = ot[dh * hh:dh * (hh + 1), lanes] / l[:, lanes]
                lse_ref[hh, 0, :, rows] = stat[:, lanes]

    blk = lambda col, m: pl.BlockSpec((TQ, PAIR * dh), lambda hp, i: (m(i), col + hp))
    same = lambda i: i
    return _pcall(
        body, name=name,
        out_shape=(jax.ShapeDtypeStruct((N_HEADS, dh, s), F32), jax.ShapeDtypeStruct((N_HEADS, n, 1, TQ), F32)),
        grid=(N_HEADS // PAIR, n),
        in_specs=[blk(Q_CHUNK, same), blk(K_CHUNK, prev), blk(K_CHUNK, same), blk(V_CHUNK, prev), blk(V_CHUNK, same),
                  pl.BlockSpec((PAIR, 1 + N_SUB, SUBK, SUB), lambda hp, i: (hp, 0, 0, 0))],
        out_specs=(pl.BlockSpec((PAIR, dh, TQ), lambda hp, i: (hp, 0, i)),
                   pl.BlockSpec((PAIR, 1, 1, TQ), lambda hp, i: (hp, i, 0, 0))),
        compiler_params=_cp("parallel", "parallel"),
    )(qkv, qkv, qkv, qkv, qkv, bias_t)


def _chunk_bwd(qkv, do, lse, delta, bias_t, name):
    s = qkv.shape[0]
    dh = HEAD_DIM
    n = s // TQ
    nsub = TQ // SUB
    cur = lambda i: jnp.minimum(i, n - 1)
    prev = lambda i: jnp.maximum(cur(i) - 1, 0)
    done = lambda i: jnp.maximum(i - 1, 0)

    def body(q_ref, kp_ref, kc_ref, vp_ref, vc_ref, do_ref, lse_ref, dl_ref, b_ref,
             dq_ref, dk_ref, dv_ref, db_ref, dkw_s, dvw_s, ck_s, cv_s):
        i = pl.program_id(1)

        @pl.when(i == 0)
        def _():
            db_ref[...] = jnp.zeros_like(db_ref)
            ck_s[...] = jnp.zeros_like(ck_s)
            cv_s[...] = jnp.zeros_like(cv_s)

        dkw_s[...] = jnp.zeros_like(dkw_s)
        dvw_s[...] = jnp.zeros_like(dvw_s)

        @pl.when(i < n)
        def _():
            first_tile = i == 0
            kw = jnp.concatenate([kp_ref[...], kc_ref[...]], axis=0)
            vw = jnp.concatenate([vp_ref[...], vc_ref[...]], axis=0)
            for j in range(nsub):
                rows, win = slice(SUB * j, SUB * (j + 1)), slice(SUB * j, SUB * j + SUBK)
                qbd = _pair_blockdiag(q_ref[rows, :] * QK_SCALE2)
                dobd = _pair_blockdiag(do_ref[rows, :])
                st = _chunk_logits(kw[win], qbd, b_ref, first_tile, j)
                pt = jnp.exp2(st - _pair_lanes(lse_ref, j))
                dpt = lax.dot_general(vw[win], dobd, _NT, preferred_element_type=F32)
                dst = pt * (dpt - _pair_lanes(dl_ref, j))
                dsb = dst.astype(BF16)
                dvw_s[win, :] += jnp.dot(pt.astype(BF16), dobd, preferred_element_type=F32)
                dkw_s[win, :] += jnp.dot(dsb, qbd, preferred_element_type=F32)
                dqt = QK_SCALE * lax.dot_general(kw[win], dsb, _TN, preferred_element_type=F32)
                for hh in range(PAIR):
                    lanes = slice(SUB * hh, SUB * (hh + 1))
                    db_ref[hh] += dst[:, lanes]
                    dq_ref[hh, :, rows] = dqt[dh * hh:dh * (hh + 1), lanes]

        dk_ref[...] = (LN2 * (ck_s[...] + dkw_s[0:TQ, :])).astype(dk_ref.dtype)
        dv_ref[...] = (cv_s[...] + dvw_s[0:TQ, :]).astype(dv_ref.dtype)
        ck_s[...] = dkw_s[TQ:2 * TQ, :]
        cv_s[...] = dvw_s[TQ:2 * TQ, :]

    blk = lambda col, m: pl.BlockSpec((TQ, PAIR * dh), lambda hp, i: (m(i), col + hp))
    stat = pl.BlockSpec((PAIR, 1, 1, TQ), lambda hp, i: (hp, cur(i), 0, 0))
    bs = pl.BlockSpec((PAIR, SUBK, SUB), lambda hp, i: (hp, 0, 0))
    return _pcall(
        body, name=name,
        out_shape=(jax.ShapeDtypeStruct((N_HEADS, dh, s), F32), jax.ShapeDtypeStruct((s, WIDTH), BF16),
                   jax.ShapeDtypeStruct((s, WIDTH), BF16), jax.ShapeDtypeStruct((N_HEADS, SUBK, SUB), F32)),
        grid=(N_HEADS // PAIR, n + 1),
        in_specs=[blk(Q_CHUNK, cur), blk(K_CHUNK, prev), blk(K_CHUNK, cur), blk(V_CHUNK, prev), blk(V_CHUNK, cur),
                  blk(0, cur), stat, stat, pl.BlockSpec((PAIR, 1 + N_SUB, SUBK, SUB), lambda hp, i: (hp, 0, 0, 0))],
        out_specs=(pl.BlockSpec((PAIR, dh, TQ), lambda hp, i: (hp, 0, cur(i))), blk(0, done), blk(0, done), bs),
        scratch_shapes=[pltpu.VMEM((2 * TQ, PAIR * dh), F32), pltpu.VMEM((2 * TQ, PAIR * dh), F32),
                        pltpu.VMEM((TQ, PAIR * dh), F32), pltpu.VMEM((TQ, PAIR * dh), F32)],
        compiler_params=_cp("parallel", "arbitrary"),
    )(qkv, qkv, qkv, qkv, qkv, do, lse, delta, bias_t)


def _mod_part(c_all, w_ada, b_ada, name):
    nl, d, n = w_ada.shape
    b = c_all.shape[0]

    def body(c_ref, w_ref, b_ref, o_ref, cond_ref):
        cv = c_ref[...]
        cond = cv * _sigmoid(cv)
        cond_ref[...] = cond
        o_ref[...] = jnp.dot(cond, w_ref[...], preferred_element_type=F32, precision=HI) + b_ref[...]

    return _pcall(
        body, name=name,
        out_shape=(jax.ShapeDtypeStruct((nl, b, n), F32), jax.ShapeDtypeStruct((b, d), F32)),
        grid=(nl,),
        in_specs=[pl.BlockSpec((b, d), lambda l: (0, 0)), pl.BlockSpec((None, d, n), lambda l: (l, 0, 0)),
                  pl.BlockSpec((None, 1, n), lambda l: (l, 0, 0))],
        out_specs=(pl.BlockSpec((None, b, n), lambda l: (l, 0, 0)), pl.BlockSpec((b, d), lambda l: (0, 0))),
        compiler_params=_cp("arbitrary"),
    )(c_all, w_ada, b_ada)


def _adamw(parts, w, m, v, name, tr=1024):
    p, rl, cp = parts[0].shape
    r, c = w.shape
    tr = _tile(rl, tr, 16)
    per = rl // tr
    c1 = 1.0 / (1.0 - ADAM_B1 ** ADAM_STEP)
    c2 = 1.0 / (1.0 - ADAM_B2 ** ADAM_STEP)

    def body(*refs):
        p_refs, (w_ref, m_ref, v_ref, g_ref, d_ref, nm_ref, nv_ref) = refs[:len(parts)], refs[len(parts):]
        for which, p_ref in enumerate(p_refs):
            pl.when(pl.program_id(0) // per == which)(
                functools.partial(update, p_ref, w_ref, m_ref, v_ref, g_ref, d_ref, nm_ref, nv_ref))

    def update(p_ref, w_ref, m_ref, v_ref, g_ref, d_ref, nm_ref, nv_ref):
        g = p_ref[0].astype(F32)
        for i in range(1, p):
            g = g + p_ref[i].astype(F32)
        g = g[:, :c]
        nm = ADAM_B1 * m_ref[...] + (1.0 - ADAM_B1) * g
        nv = ADAM_B2 * v_ref[...] + (1.0 - ADAM_B2) * (g * g)
        g_ref[...] = g
        nm_ref[...] = nm
        nv_ref[...] = nv
        d_ref[...] = -ADAM_LR * ((nm * c1) / (jnp.sqrt(nv * c2) + ADAM_EPS) + ADAM_WD * w_ref[...])

    blk = pl.BlockSpec((tr, c), lambda i: (i, 0))
    out = jax.ShapeDtypeStruct((r, c), F32)
    return _pcall(
        body, name=name, out_shape=(out, out, out, out), grid=(r // tr,),
        in_specs=[pl.BlockSpec((p, tr, cp), lambda i, _w=which: (0, jnp.clip(i - _w * per, 0, per - 1), 0))
                  for which in range(len(parts))] + [blk, blk, blk],
        out_specs=(blk, blk, blk, blk),
        compiler_params=_cp("parallel"),
    )(*parts, w, m, v)


def _pad_lanes(a):
    pad = (-a.shape[-1]) % 128
    return jnp.pad(a, [(0, 0)] * (a.ndim - 1) + [(0, pad)]) if pad else a


def _rows128(a, mult=16):
    flat = a.reshape(-1)
    n = flat.shape[0]
    per = 128 * mult
    pad = (-n) % per
    if pad:
        flat = jnp.concatenate([flat, jnp.zeros((pad,), a.dtype)])
    return flat.reshape(-1, 128)


def _pack(arrs, mult=16):
    pieces = [_rows128(a, mult) for a in arrs]
    return jnp.concatenate(pieces, axis=0), [p.shape[0] for p in pieces]


def _unpack(packed, rows, shapes, lead=()):
    out, at = [], 0
    for r, shp in zip(rows, shapes):
        n = int(np.prod(shp))
        piece = packed[..., at:at + r, :].reshape(lead + (r * 128,))[..., :n]
        out.append(piece.reshape(lead + tuple(shp)))
        at += r
    return out


def _rel_bias_tile(table):
    h = table.shape[0]
    lo = REL_CLIP - (CHUNK - 1)
    n_far = LEFT_CHUNKS * CHUNK + CHUNK - 1 - REL_CLIP
    vec = jnp.concatenate([table[:, lo:2 * REL_CLIP], jnp.repeat(table[:, 2 * REL_CLIP:], n_far + 1, axis=1)], axis=1)
    rev = vec[:, ::-1]
    n_vec = BAND + CHUNK - 1
    skew = jnp.tile(rev, (1, CHUNK + 1))[:, :CHUNK * (n_vec + 1)].reshape(h, CHUNK, n_vec + 1)
    bias = skew[:, ::-1, :BAND]
    neg = jnp.full((h, CHUNK, CHUNK), NEG, F32)
    two = jnp.concatenate([jnp.concatenate([bias, neg], axis=2), jnp.concatenate([neg, bias], axis=2)], axis=1)
    plain = two.transpose(0, 2, 1) * LOG2E
    key = np.arange(SUBK)[:, None]
    return jnp.stack([plain] + [jnp.where(key >= TQ - SUB * j, plain, NEG) for j in range(N_SUB)], axis=1)


def _rel_bias_tile_grad(dbias_t):
    h = dbias_t.shape[0]
    two = dbias_t.transpose(0, 2, 1)
    dbias = two[:, :CHUNK, :BAND] + two[:, CHUNK:, CHUNK:]
    n_vec = BAND + CHUNK - 1
    dskew = jnp.pad(dbias[:, ::-1, :], ((0, 0), (0, 0), (0, n_vec + 1 - BAND))).reshape(h, CHUNK * (n_vec + 1))
    dskew = jnp.pad(dskew, ((0, 0), (0, (CHUNK + 1) * n_vec - CHUNK * (n_vec + 1))))
    drev = jnp.sum(dskew.reshape(h, CHUNK + 1, n_vec), axis=1)
    dvec = drev[:, ::-1]
    lo = REL_CLIP - (CHUNK - 1)
    n_near = 2 * REL_CLIP - lo
    return jnp.concatenate([jnp.zeros((h, lo), F32), dvec[:, :n_near],
                            jnp.sum(dvec[:, n_near:], axis=1, keepdims=True)], axis=1)


BIG = ("w_in", "w_br_fox", "w_br_chunk", "w_out", "w_up", "w_down")
SMALL = ("b_f", "rel_bias", "conv_b", "b_ada", "ln1_g", "ln1_b", "ln2_g", "ln2_b")
ORDER = ("w_in", "b_f", "rel_bias", "w_br_fox", "w_br_chunk", "w_out", "w_up", "conv_w", "conv_b", "w_down",
         "w_ada", "b_ada", "ln1_g", "ln1_b", "ln2_g", "ln2_b")


def _dest_major(g, axis):
    shp = g.shape
    g = g.reshape(shp[:axis] + (N_DEV, shp[axis] // N_DEV) + shp[axis + 1:])
    return jnp.moveaxis(g, axis, 0)


EARLY = ("w_down", "w_up", "w_out", "w_br_fox", "w_br_chunk")
LATE = ("w_in",)
REST = ("w_br_fox", "w_br_chunk", "w_out", "w_up", "w_down")


def _shard_axis(n):
    return 0 if n in ("w_out", "w_down") else 1


def _slab(g, n):
    return _pad_lanes(_dest_major(g, _shard_axis(n)).astype(BF16))


def _layer_fwd(x, h1, wl, rest_of, mod, next_mod, alpha, gather):
    sh1, sc1, g1, sh2, sc2, g2 = mod
    s = x.shape[0]
    qkv = _mm(h1, wl["w_qkv"], out_dtype=BF16, name="proj_qkv", tk=1024)
    gates = _mm(h1, wl["w_gates"], out_dtype=F32, name="proj_gates", tk=1024)
    f_t = _mm(wl["w_f_t"], h1, out_dtype=F32, name="proj_f", tb=True, tk=1024)[:N_HEADS]
    f_t = f_t.reshape(N_HEADS, s // 128, 128)
    b_f = jnp.broadcast_to(wl["b_f"].reshape(N_HEADS, 1, 1), (N_HEADS, 1, 128))
    cum = _forget_fwd(f_t, b_f, "forget_fwd").reshape(N_SPLIT, N_HEADS, s)
    cum_cols = cum.transpose(2, 1, 0).astype(BF16)

    ones = jnp.ones((s, N_HEADS, N_SPLIT), BF16)
    zeros = jnp.zeros((s, N_HEADS, AUG - HEAD_DIM - 2 * N_SPLIT), BF16)
    q_a = qkv[:, :WIDTH].reshape(s, N_HEADS, HEAD_DIM) * QK_SCALE2
    k_a = qkv[:, WIDTH:2 * WIDTH].reshape(s, N_HEADS, HEAD_DIM)
    q_aug = jnp.concatenate([q_a, cum_cols, ones, zeros], axis=-1).reshape(s, N_HEADS * AUG)
    k_aug = jnp.concatenate([k_a, ones, -cum_cols, zeros], axis=-1).reshape(s, N_HEADS * AUG)
    o_a_t, lse_a, *gathered = _fox_fwd(q_aug, k_aug, qkv, "fox_fwd_gather_%d" % len(gather), t=FOX_FWD_T,
                                       exchange=(False, gather))
    wl.update(rest_of(gathered[:len(REST)]))
    gathered = gathered[len(REST):]
    bias_t = _rel_bias_tile(wl["rel_bias"])
    o_c_t, lse_c = _chunk_fwd(qkv, bias_t, "chunk_fwd")
    o_a = o_a_t.transpose(2, 0, 1).reshape(s, WIDTH).astype(BF16)
    o_c = o_c_t.transpose(2, 0, 1).reshape(s, WIDTH).astype(BF16)

    merged = _merge_fwd(o_a, o_c, wl["w_br_fox"], wl["w_br_chunk"], gates, "merge_fwd")
    mix, x1, h2 = _mm_res_ln(merged, wl["w_out"], x, g1, wl["ln1_g"], wl["ln1_b"], sc2, sh2, alpha, "out_ln1")
    u0 = _mm(h2, wl["w_up"], out_dtype=F32, name="ffn_up", tk=1024, tn=1408, o_halves=True)
    act = _conv_act_fwd(u0, wl["conv_w"], wl["conv_b"], "conv_act_fwd")
    y2, x2, h_next = _mm_res_ln(act, wl["w_down"], x1, g2, wl["ln2_g"], wl["ln2_b"], next_mod[0], next_mod[1],
                                alpha, "down_ln2", tk=1408)
    saved = dict(x=x, h1=h1, gates=gates, f_t=f_t, b_f=b_f, q_aug=q_aug, k_aug=k_aug, qkv=qkv,
                 bias_t=bias_t, o_a_t=o_a_t, o_c_t=o_c_t, lse_a=lse_a, lse_c=lse_c,
                 o_a=o_a, o_c=o_c, merged=merged, mix=mix, x1=x1, h2=h2, u0=u0, act=act, y2=y2)
    return x2, h_next, saved, gathered


def _layer_bwd(dx2, sv, wl, mod, alpha, send, ride_late):
    sh1, sc1, g1, sh2, sc2, g2 = mod
    s = dx2.shape[0]
    g = {}
    dres2, dy2, sums = _ln_res_bwd(dx2, sv["x1"], sv["y2"], g2, wl["ln2_g"], alpha, "ln2_bwd")
    g["ln2_g"], g["ln2_b"], dg2 = sums[0], sums[1], sums[2]
    dact = _mm(dy2, wl["w_down_t"], out_dtype=F32, name="dact", tk=1024, tn=1408)
    g["w_down"] = _mm(sv["act"], dy2, out_dtype=F32, name="g_w_down", ta=True, tm=1408, tk=2048)
    du0, csum = _conv_act_bwd(dact, sv["u0"], wl["conv_w"], wl["conv_b"], "conv_act_bwd")
    g["conv_w"] = jnp.concatenate([csum[0, 0:3], csum[1, 0:3]], axis=1)
    g["conv_b"] = jnp.concatenate([csum[0, 3], csum[1, 3]])
    dh2 = _mm(du0, wl["w_up_t"], out_dtype=F32, name="dh2", tk=2816, a_halves=True)
    g["w_up"] = _mm(sv["h2"], du0, out_dtype=F32, name="g_w_up", ta=True, tk=2048, tn=1408, b_halves=True)
    dx1, sums = _ln_mod_bwd(dh2, sv["x1"], sc2, dres2, "ln_mod2_bwd")
    dsh2, dsc2 = sums[0], sums[1]

    dres1, dmix, sums = _ln_res_bwd(dx1, sv["x"], sv["mix"], g1, wl["ln1_g"], alpha, "ln1_bwd")
    g["ln1_g"], g["ln1_b"], dg1 = sums[0], sums[1], sums[2]
    g["w_out"] = _mm(sv["merged"], dmix, out_dtype=F32, name="g_w_out", ta=True, tk=2048)
    dba, dbc, do_a, do_c, dgates = _merge_bwd(
        dmix, wl["w_out_t"], sv["o_a"], sv["o_c"], wl["w_br_fox"], wl["w_br_chunk"],
        wl["w_br_fox_t"], wl["w_br_chunk_t"], sv["gates"], "merge_bwd")
    g["w_br_fox"] = _mm(sv["o_a"], dba, out_dtype=F32, name="g_w_br_fox", ta=True, tk=2048)
    g["w_br_chunk"] = _mm(sv["o_c"], dbc, out_dtype=F32, name="g_w_br_chunk", ta=True, tk=2048)
    tiles = lambda a, t: a.reshape(N_HEADS, s // min(t, s), 1, min(t, s))
    stats = lambda a: tiles(a, TQ)

    delta_a = _row_dot(do_a.T.reshape(N_HEADS, HEAD_DIM, s), sv["o_a_t"], "delta_fox")
    send = list(send) + [_slab(g[n], n) for n in EARLY]
    dk_a, dv_a, dq_a_t, dsum, dqsum, *landed = _fox_bwd(
        sv["q_aug"], sv["k_aug"], sv["qkv"], do_a, tiles(sv["lse_a"], FOX_BWD_T), tiles(delta_a, FOX_BWD_T),
        "fox_bwd_scatter_%d" % len(send), t=FOX_BWD_T, exchange=(True, send))
    dq_a = (dq_a_t * QK_SCALE).transpose(1, 3, 0, 2).reshape(s, WIDTH).astype(BF16)
    d_cum = dqsum.reshape(N_HEADS, s) - jnp.sum(dsum.reshape(s, N_HEADS, 128), axis=-1).T
    df_t, db_f = _forget_bwd(d_cum.reshape(N_HEADS, s // 128, 128), sv["f_t"], sv["b_f"], "forget_bwd")
    g["b_f"] = db_f[:, 0, 0]
    df_t = df_t.reshape(N_HEADS, s)

    delta_c = _row_dot(do_c.T.reshape(N_HEADS, HEAD_DIM, s), sv["o_c_t"], "delta_chunk")
    dq_c_t, dk_c, dv_c, dbias_t = _chunk_bwd(sv["qkv"], do_c, sv["lse_c"], stats(delta_c), sv["bias_t"], "chunk_bwd")
    g["rel_bias"] = _rel_bias_tile_grad(dbias_t)
    dq_c = dq_c_t.transpose(2, 0, 1).reshape(s, WIDTH).astype(BF16)

    dqkv = jnp.concatenate([dq_a, dk_a, dv_a, dq_c, dk_c, dv_c], axis=1)
    df_pad = jnp.zeros((16 - N_HEADS, s), F32)
    df16 = jnp.concatenate([df_t, df_pad], axis=0).astype(BF16)
    df_cols = jnp.concatenate([df16.T, jnp.zeros((s, AUG - 16), BF16)], axis=1)

    g_qkv = _mm(sv["h1"], dqkv, out_dtype=F32, name="g_w_qkv", ta=True, tk=2048)
    g_gates = _mm(sv["h1"], dgates, out_dtype=F32, name="g_w_gates", ta=True, tk=2048)
    g_f_t = _mm(df16, sv["h1"], out_dtype=F32, name="g_w_f", tk=1024)[:N_HEADS]
    g["w_in"] = jnp.concatenate([g_qkv[:, :3 * WIDTH], g_f_t.T, g_qkv[:, 3 * WIDTH:], g_gates], axis=1)
    late = [_slab(g[n], n) for n in LATE] if ride_late else None
    dh1, *landed_late = _mm(dqkv, wl["w_qkv_t"], out_dtype=F32, name="dh1_qkv_scatter", tk=1536,
                            exchange=(True, late)) if ride_late else \
        [_mm(dqkv, wl["w_qkv_t"], out_dtype=F32, name="dh1_qkv", tk=1536)]
    dh1g = _mm(dgates, wl["w_gates_t"], out_dtype=F32, name="dh1_gates", tk=2048)
    dx, sums = _ln_mod_bwd(dh1, sv["x"], sc1, dres1, "ln_mod1_bwd", extra=(dh1g, df_cols, wl["w_f_pad"]))
    dsh1, dsc1 = sums[0], sums[1]
    g["mod"] = jnp.concatenate([dsh1, dsc1, dg1, dsh2, dsc2, dg2])
    return dx, g, landed, landed_late


def kernel(x, c, w_in, b_f, rel_bias, w_br_fox, w_br_chunk, w_out, w_up, conv_w, conv_b, w_down, w_ada, b_ada, ln1_g, ln1_b, ln2_g, ln2_b, loss_target, m_w_in, m_b_f, m_rel_bias, m_w_br_fox, m_w_br_chunk, m_w_out, m_w_up, m_conv_w, m_conv_b, m_w_down, m_w_ada, m_b_ada, m_ln1_g, m_ln1_b, m_ln2_g, m_ln2_b, v_w_in, v_b_f, v_rel_bias, v_w_br_fox, v_w_br_chunk, v_w_out, v_w_up, v_conv_w, v_conv_b, v_w_down, v_w_ada, v_b_ada, v_ln1_g, v_ln1_b, v_ln2_g, v_ln2_b):
    w = dict(w_in=w_in, b_f=b_f, rel_bias=rel_bias, w_br_fox=w_br_fox, w_br_chunk=w_br_chunk, w_out=w_out,
             w_up=w_up, conv_w=conv_w, conv_b=conv_b, w_down=w_down, w_ada=w_ada, b_ada=b_ada,
             ln1_g=ln1_g, ln1_b=ln1_b, ln2_g=ln2_g, ln2_b=ln2_b)
    m = dict(w_in=m_w_in, b_f=m_b_f, rel_bias=m_rel_bias, w_br_fox=m_w_br_fox, w_br_chunk=m_w_br_chunk,
             w_out=m_w_out, w_up=m_w_up, conv_w=m_conv_w, conv_b=m_conv_b, w_down=m_w_down, w_ada=m_w_ada,
             b_ada=m_b_ada, ln1_g=m_ln1_g, ln1_b=m_ln1_b, ln2_g=m_ln2_g, ln2_b=m_ln2_b)
    v = dict(w_in=v_w_in, b_f=v_b_f, rel_bias=v_rel_bias, w_br_fox=v_w_br_fox, w_br_chunk=v_w_br_chunk,
             w_out=v_w_out, w_up=v_w_up, conv_w=v_conv_w, conv_b=v_conv_b, w_down=v_w_down, w_ada=v_w_ada,
             b_ada=v_b_ada, ln1_g=v_ln1_g, ln1_b=v_ln1_b, ln2_g=v_ln2_g, ln2_b=v_ln2_b)
    depth, d, _ = w_in.shape
    s = x.shape[1]
    alpha = (2.0 * depth) ** 0.25
    me = 4 * lax.axis_index("x") + 2 * lax.axis_index("y") + lax.axis_index("c")
    x0 = x.reshape(s, d)
    target = loss_target.reshape(s, d)

    small_in, small_rows = _pack([c, conv_w], mult=8)
    small_all = _all_gather([small_in], "gather_c_conv")[0]
    c_all, conv_w_all = _unpack(small_all, small_rows, [c.shape, conv_w.shape], lead=(N_DEV,))
    c_all = c_all.reshape(N_DEV, d)
    conv_w_full = conv_w_all.transpose(1, 2, 0, 3).reshape(depth, conv_w.shape[1], -1)
    n_ada = w_ada.shape[2]
    b_ada_mine = lax.dynamic_slice_in_dim(b_ada, me * n_ada, n_ada, axis=1).reshape(depth, 1, n_ada)
    mod_part, cond_all = _mod_part(c_all, w_ada, b_ada_mine, "mod_part")
    mod_all = _all_gather([mod_part.reshape(depth * N_DEV, n_ada)], "gather_mod")[0]
    mod_all = mod_all.reshape(N_DEV, depth, N_DEV, n_ada)
    mod_mine = lax.dynamic_index_in_dim(mod_all, me, axis=2, keepdims=False)
    mod_mine = mod_mine.transpose(1, 0, 2).reshape(depth, N_MOD, 1, d)

    shard = lambda n, l: _pad_lanes(w[n][l].astype(BF16))
    cols = np.cumsum([0, WIDTH, WIDTH, WIDTH, N_HEADS, WIDTH, WIDTH, WIDTH, d, d])

    def whole(n, p):
        rows, ncol = w[n].shape[1:]
        p = jnp.moveaxis(p[..., :ncol], 0, _shard_axis(n))
        return p.reshape((rows * N_DEV, ncol) if _shard_axis(n) == 0 else (rows, ncol * N_DEV))

    def first_weights(l, w_in_parts):
        wi = whole("w_in", w_in_parts)
        w_qkv = jnp.concatenate([wi[:, :cols[3]], wi[:, cols[4]:cols[7]]], axis=1)
        w_gates = wi[:, cols[7]:]
        w_f_t = jnp.concatenate([wi[:, cols[3]:cols[4]].T, jnp.zeros((16 - N_HEADS, d), BF16)], axis=0)
        w_f_pad = jnp.concatenate([w_f_t, jnp.zeros((AUG - 16, d), BF16)], axis=0)
        row = lambda a: a[l].reshape(1, -1)
        return dict(
            w_qkv=w_qkv, w_gates=w_gates, w_f_t=w_f_t, w_f_pad=w_f_pad, w_qkv_t=w_qkv.T, w_gates_t=w_gates.T,
            conv_w=conv_w_full[l].reshape(3, 2, -1).transpose(1, 0, 2), conv_b=conv_b[l].reshape(2, 1, -1),
            b_f=b_f[l], rel_bias=rel_bias[l],
            ln1_g=row(ln1_g), ln1_b=row(ln1_b), ln2_g=row(ln2_g), ln2_b=row(ln2_b))

    def rest_weights(parts):
        out = {}
        for n, p in zip(REST, parts):
            out[n] = whole(n, p)
            out[n + "_t"] = out[n].T
        return out

    xs, saved, layers = x0, [], []
    hs = _ln_mod(x0, mod_mine[0][1], mod_mine[0][0], "ln_mod1")
    w_in_parts = _all_gather([shard("w_in", 0)], "gather_weights")[0]
    for l in range(depth):
        layers.append(first_weights(l, w_in_parts))
        nxt = mod_mine[min(l + 1, depth - 1)]
        gather = [shard(n, l) for n in REST] + ([shard("w_in", l + 1)] if l + 1 < depth else [])
        xs, hs, sv, extra = _layer_fwd(xs, hs, layers[l], rest_weights, list(mod_mine[l]), (nxt[1], nxt[0]), alpha,
                                       gather)
        w_in_parts = extra[0] if extra else None
        saved.append(sv)
    dxs, loss_part = _loss_grad(xs, target, "loss_grad")
    loss = lax.psum(loss_part[0, 0], AXES)
    grads, big_parts, send = [None] * depth, {}, []
    for l in reversed(range(depth)):
        dxs, grads[l], landed, landed_late = _layer_bwd(dxs, saved[l], layers[l], list(mod_mine[l]), alpha, send,
                                                        ride_late=(l == 0))
        for n, part in zip([(n, l + 1) for n in LATE if send] + [(n, l) for n in EARLY], landed):
            big_parts[n] = part
        for n, part in zip(LATE, landed_late):
            big_parts[n, l] = part
        send = [_slab(grads[l][n], n) for n in LATE] if l else []
    grad_x = dxs.reshape(1, s, d)
    stack = lambda n: jnp.stack([grads[l][n] for l in range(depth)])

    rep = ["b_ada"] + [n for n in SMALL if n != "b_ada"]
    small_g = [stack("mod")] + [stack(n) for n in rep[1:]] + [stack("conv_w")]
    small_pack, small_g_rows = _pack(small_g, mult=8)
    small_parts = _all_gather([small_pack], "gather_small_grads")[0]
    n_rep_rows = sum(small_g_rows[:-1])
    dmod_all = _unpack(small_parts, small_g_rows[:1], [(depth, N_MOD * d)], lead=(N_DEV,))[0]
    cw_all = _unpack(small_parts[:, n_rep_rows:], small_g_rows[-1:], [small_g[-1].shape], lead=(N_DEV,))[0]

    out = {}
    for n in BIG:
        rows2d = lambda a: a.reshape(-1, a.shape[-1])
        res = _adamw([big_parts[n, l] for l in range(depth)], rows2d(w[n]), rows2d(m[n]), rows2d(v[n]),
                     "adamw_" + n, tr=256)
        for kind, r in zip(("grad", "delta", "new_m", "new_v"), res):
            out[kind, n] = r.reshape(w[n].shape)

    dmod_mine = lax.dynamic_slice_in_dim(dmod_all, me * n_ada, n_ada, axis=2)
    g_ada = jnp.stack([_mm(cond_all.T, dmod_mine[:, l], out_dtype=F32, name="g_w_ada", tk=N_DEV, tn=n_ada,
                           precision=HI) for l in range(depth)])
    lp = [_rows128(t["w_ada"], 8) for t in (w, m, v)]
    res = _adamw([_rows128(g_ada, 8)[None]], *lp, "adamw_ada")
    for kind, r in zip(("grad", "delta", "new_m", "new_v"), res):
        out[kind, "w_ada"] = _unpack(r, [lp[0].shape[0]], [w_ada.shape])[0]

    rp = [_pack([t[n] for n in rep], mult=8)[0] for t in (w, m, v)]
    res = _adamw([small_parts[:, :n_rep_rows]], *rp, "adamw_small")
    for kind, r in zip(("grad", "delta", "new_m", "new_v"), res):
        for n, a in zip(rep, _unpack(r, small_g_rows[:-1], [w[n].shape for n in rep])):
            out[kind, n] = a

    n_cw = conv_w.shape[2]
    cw_mine = lax.dynamic_slice_in_dim(cw_all, me * n_cw, n_cw, axis=3)
    cp = [_rows128(t["conv_w"], 8) for t in (w, m, v)]
    cw_in = jnp.stack([_rows128(cw_mine[i], 8) for i in range(N_DEV)])
    res = _adamw([cw_in], *cp, "adamw_conv_w")
    for kind, r in zip(("grad", "delta", "new_m", "new_v"), res):
        out[kind, "conv_w"] = _unpack(r, [cp[0].shape[0]], [conv_w.shape])[0]

    result = [loss, grad_x]
    for kind in ("grad", "delta", "new_m", "new_v"):
        result += [out[kind, n] for n in ORDER]
    return tuple(result)
```

```python
import functools
import math

import jax
import jax.numpy as jnp
import numpy as np
from jax import lax
from jax.experimental import pallas as pl
from jax.experimental.pallas import tpu as pltpu

F32 = jnp.float32
BF16 = jnp.bfloat16
HI = lax.Precision.HIGHEST
MESH_ID = pl.DeviceIdType.MESH
AXES = ("x", "y", "c")
N_DEV = 8

N_HEADS = 8
HEAD_DIM = 64
WIDTH = N_HEADS * HEAD_DIM
CHUNK = 64
LEFT_CHUNKS = 8
BAND = (LEFT_CHUNKS + 1) * CHUNK
REL_CLIP = 128
LN_EPS = 1e-5
N_MOD = 6
QK_SCALE = 1.0 / math.sqrt(HEAD_DIM)
LOG2E = math.log2(math.e)
LN2 = math.log(2.0)
QK_SCALE2 = QK_SCALE * LOG2E
NEG = -1e30
AUG = 128
N_SPLIT = 3
TQ = 512
FOX_FWD_T = 1024
FOX_BWD_T = 1024
SUB = 2 * CHUNK
SUBK = BAND + CHUNK

ADAM_LR, ADAM_B1, ADAM_B2, ADAM_EPS, ADAM_WD, ADAM_STEP = 0.001, 0.9, 0.999, 1e-08, 0.01, 10

VMEM_LIMIT = 56 * 2 ** 20


def _pcall(body, **kw):
    return pl.pallas_call(body, **kw)


def _cp(*sem):
    return pltpu.CompilerParams(dimension_semantics=sem if sem else None, vmem_limit_bytes=VMEM_LIMIT)


def _my_place():
    return lax.axis_index("x"), lax.axis_index("y"), lax.axis_index("c")


N_PEER = N_DEV - 1


def _comm_call(body, xs, out_shapes, name):
    n = len(xs)
    hbm = pl.BlockSpec(memory_space=pltpu.HBM)
    return _pcall(
        functools.partial(body, n), name=name, out_shape=out_shapes, in_specs=[hbm] * n, out_specs=[hbm] * n,
        scratch_shapes=[pltpu.SemaphoreType.DMA((N_PEER * n,)), pltpu.SemaphoreType.DMA((N_PEER * n,)),
                        pltpu.SemaphoreType.DMA((n,))],
    )(*xs)


def _all_gather(xs, name):
    def body(n, *refs):
        x_refs, out_refs, (send_sems, recv_sems, local_sems) = refs[:n], refs[n:2 * n], refs[2 * n:]
        mx, my, mc = _my_place()
        me, sibling = (mx, my, mc), (mx, my, 1 - mc)
        chips = [(1 - mx, my), (mx, 1 - my), (1 - mx, 1 - my)]

        def slot(a, px, py, pc):
            return out_refs[a].at[4 * px + 2 * py + pc]

        def copy(a, k, block, to, own=False):
            return pltpu.make_async_remote_copy(
                src_ref=x_refs[a] if own else slot(a, *block), dst_ref=slot(a, *block),
                send_sem=send_sems.at[a * N_PEER + k], recv_sem=recv_sems.at[a * N_PEER + k],
                device_id=to, device_id_type=MESH_ID)

        arrays = range(n)
        mine = [pltpu.make_async_copy(x_refs[a], slot(a, *me), local_sems.at[a]) for a in arrays]
        first = [copy(a, 0, me, sibling, own=True) for a in arrays]
        first += [copy(a, 1 + j, me, (*chip, mc), own=True) for a in arrays for j, chip in enumerate(chips)]
        for cp in mine + first:
            cp.start()
        passed = []
        for a in arrays:
            for j, chip in enumerate(chips):
                copy(a, 1 + j, (*chip, mc), me).wait_recv()
                passed.append(copy(a, 4 + j, (*chip, mc), sibling))
                passed[-1].start()
        for a in arrays:
            copy(a, 0, sibling, me).wait_recv()
            for j, chip in enumerate(chips):
                copy(a, 4 + j, (*chip, 1 - mc), me).wait_recv()
        for cp in first + passed:
            cp.wait_send()
        for cp in mine:
            cp.wait()

    return _comm_call(body, xs, [jax.ShapeDtypeStruct((N_DEV,) + x.shape, x.dtype) for x in xs], name)


def _direct_exchange(scatter, x_refs, out_refs, send_sems, recv_sems, local_sems):
    mx, my, mc = _my_place()
    me = 4 * mx + 2 * my + mc
    mine, copies = [], []
    for a, (x_ref, out_ref) in enumerate(zip(x_refs, out_refs)):
        mine.append(pltpu.make_async_copy(x_ref.at[me] if scatter else x_ref, out_ref.at[me], local_sems.at[a]))
        for k in range(N_PEER):
            px, py, pc = mx ^ ((k + 1) >> 2), my ^ (((k + 1) >> 1) & 1), mc ^ ((k + 1) & 1)
            copies.append(pltpu.make_async_remote_copy(
                src_ref=x_ref.at[4 * px + 2 * py + pc] if scatter else x_ref, dst_ref=out_ref.at[me],
                send_sem=send_sems.at[a * N_PEER + k], recv_sem=recv_sems.at[a * N_PEER + k],
                device_id=(px, py, pc), device_id_type=MESH_ID))

    def start():
        for cp in mine + copies:
            cp.start()

    def wait():
        for cp in copies:
            cp.wait_recv()
        for cp in copies:
            cp.wait_send()
        for cp in mine:
            cp.wait()

    return start, wait


def _exchange_shapes(scatter, xs):
    return [jax.ShapeDtypeStruct(x.shape if scatter else (N_DEV,) + x.shape, x.dtype) for x in xs]


def _exchange_scratch(n):
    return [pltpu.SemaphoreType.DMA((N_PEER * n,)), pltpu.SemaphoreType.DMA((N_PEER * n,)),
            pltpu.SemaphoreType.DMA((n,))]


def _tile(n, pref, unit=128):
    if n <= pref:
        return n
    t = pref - pref % unit
    while t > unit and n % t:
        t -= unit
    assert n % t == 0, (n, pref, unit)
    return t


def _mm(a, b, *, out_dtype, name, ta=False, tb=False, tm=1024, tn=1024, tk=512, precision=None,
        a_halves=False, b_halves=False, o_halves=False, exchange=None):
    a_shape = (a.shape[1], 2 * a.shape[2]) if a_halves else a.shape
    b_shape = (b.shape[1], 2 * b.shape[2]) if b_halves else b.shape
    (k_dim, m) = a_shape if ta else a_shape[::-1]
    n = b_shape[0] if tb else b_shape[1]
    tm, tn = _tile(m, tm, 8), _tile(n // 2 if (b_halves or o_halves) else n, tn)
    tk = _tile(k_dim // 2 if a_halves else k_dim, tk)
    nk = k_dim // tk
    dims = (((0 if ta else 1,), (1 if tb else 0,)), ((), ()))
    grid = (m // tm, n // tn, nk)
    ex_in, ex_out, ex_specs, ex_scratch, hook = _riding_exchange(exchange, 2, 1, grid)

    def body(*refs):
        (a_ref, b_ref, o_ref, acc_ref), finish = hook(refs)
        k = pl.program_id(2)
        if nk == 1:
            o_ref[...] = lax.dot_general(a_ref[...], b_ref[...], dims, preferred_element_type=F32,
                                         precision=precision).astype(o_ref.dtype)
            finish()
            return

        @pl.when(k == 0)
        def _():
            acc_ref[...] = jnp.zeros_like(acc_ref)

        acc_ref[...] += lax.dot_general(a_ref[...], b_ref[...], dims, preferred_element_type=F32,
                                        precision=precision)

        @pl.when(k == nk - 1)
        def _():
            o_ref[...] = acc_ref[...].astype(o_ref.dtype)

        finish()

    def spec(shape2, pick, halves, per_half):
        if not halves:
            return pl.BlockSpec(shape2, pick)

        def index(i, j, k):
            r, c = pick(i, j, k)
            return (c // per_half, r, c % per_half)
        return pl.BlockSpec((None,) + shape2, index)

    assert not (ta and a_halves) and not (tb and b_halves)
    a_spec = spec((tk, tm), lambda i, j, k: (k, i), False, 0) if ta else \
        spec((tm, tk), lambda i, j, k: (i, k), a_halves, (k_dim // 2) // tk if a_halves else 0)
    b_spec = spec((tn, tk), lambda i, j, k: (j, k), False, 0) if tb else \
        spec((tk, tn), lambda i, j, k: (k, j), b_halves, (n // 2) // tn if b_halves else 0)
    o_spec = spec((tm, tn), lambda i, j, k: (i, j), o_halves, (n // 2) // tn if o_halves else 0)
    out_shape = (2, m, n // 2) if o_halves else (m, n)
    res = _pcall(
        body, name=name,
        out_shape=[jax.ShapeDtypeStruct(out_shape, out_dtype)] + ex_out,
        grid=grid,
        in_specs=[a_spec, b_spec] + ex_specs,
        out_specs=[o_spec] + ex_specs,
        scratch_shapes=[pltpu.VMEM((tm, tn), F32)] + ex_scratch,
        compiler_params=_cp(*(["arbitrary"] * 3 if exchange else ["parallel", "parallel", "arbitrary"])),
    )(a, b, *ex_in)
    return res if exchange else res[0]


def _ln_stats(x):
    mu = jnp.mean(x, axis=-1, keepdims=True)
    xc = x - mu
    var = jnp.mean(xc * xc, axis=-1, keepdims=True)
    rstd = lax.rsqrt(var + LN_EPS)
    return xc * rstd, rstd


def _ln_mod(x, sc, sh, name, tm=512):
    s, d = x.shape
    tm = min(tm, s)

    def body(x_ref, sc_ref, sh_ref, h_ref):
        xhat, _ = _ln_stats(x_ref[...])
        h_ref[...] = (xhat * (1.0 + sc_ref[...]) + sh_ref[...]).astype(h_ref.dtype)

    row = pl.BlockSpec((1, d), lambda i: (0, 0))
    return _pcall(
        body, name=name, out_shape=jax.ShapeDtypeStruct((s, d), BF16), grid=(s // tm,),
        in_specs=[pl.BlockSpec((tm, d), lambda i: (i, 0)), row, row],
        out_specs=pl.BlockSpec((tm, d), lambda i: (i, 0)),
        compiler_params=_cp("parallel"),
    )(x, sc, sh)


def _mm_res_ln(a, w, xres, gate, ln_g, ln_b, sc, sh, alpha, name, tm=512, tk=512):
    s, k_dim = a.shape
    d = w.shape[1]
    tm, tk = _tile(s, tm, 8), _tile(k_dim, tk)
    nk = k_dim // tk

    def body(a_ref, w_ref, x_ref, g_ref, lg_ref, lb_ref, sc_ref, sh_ref, y_ref, xn_ref, h_ref, acc_ref):
        k = pl.program_id(1)

        @pl.when(k == 0)
        def _():
            acc_ref[...] = jnp.zeros_like(acc_ref)

        acc_ref[...] += jnp.dot(a_ref[...], w_ref[...], preferred_element_type=F32)

        @pl.when(k == nk - 1)
        def _():
            y = acc_ref[...]
            y_ref[...] = y
            z = alpha * x_ref[...] + (1.0 + g_ref[...]) * y
            zhat, _ = _ln_stats(z)
            xn = zhat * lg_ref[...] + lb_ref[...]
            xn_ref[...] = xn
            xhat, _ = _ln_stats(xn)
            h_ref[...] = (xhat * (1.0 + sc_ref[...]) + sh_ref[...]).astype(h_ref.dtype)

    row = pl.BlockSpec((1, d), lambda i, k: (0, 0))
    tile = pl.BlockSpec((tm, d), lambda i, k: (i, 0))
    return _pcall(
        body, name=name,
        out_shape=(jax.ShapeDtypeStruct((s, d), F32), jax.ShapeDtypeStruct((s, d), F32),
                   jax.ShapeDtypeStruct((s, d), BF16)),
        grid=(s // tm, nk),
        in_specs=[pl.BlockSpec((tm, tk), lambda i, k: (i, k)), pl.BlockSpec((tk, d), lambda i, k: (k, 0)),
                  tile, row, row, row, row, row],
        out_specs=(tile, tile, tile),
        scratch_shapes=[pltpu.VMEM((tm, d), F32)],
        compiler_params=_cp("parallel", "arbitrary"),
    )(a, w, xres, gate, ln_g, ln_b, sc, sh)


def _colsum(v):
    return jnp.sum(v, axis=0, keepdims=True)


def _ln_res_bwd(dxn, xres, y, gate, ln_g, alpha, name, tm=512):
    s, d = dxn.shape
    tm = min(tm, s)

    def body(dxn_ref, x_ref, y_ref, g_ref, lg_ref, dres_ref, dy_ref, sums_ref):
        @pl.when(pl.program_id(0) == 0)
        def _():
            sums_ref[...] = jnp.zeros_like(sums_ref)

        dxn_v, y_v = dxn_ref[...], y_ref[...]
        one_g = 1.0 + g_ref[...]
        zhat, rstd = _ln_stats(alpha * x_ref[...] + one_g * y_v)
        dzh = dxn_v * lg_ref[...]
        dz = rstd * (dzh - jnp.mean(dzh, axis=-1, keepdims=True)
                     - zhat * jnp.mean(dzh * zhat, axis=-1, keepdims=True))
        dres_ref[...] = alpha * dz
        dy_ref[...] = (one_g * dz).astype(dy_ref.dtype)
        sums_ref[0:1, :] += _colsum(dxn_v * zhat)
        sums_ref[1:2, :] += _colsum(dxn_v)
        sums_ref[2:3, :] += _colsum(dz * y_v)

    row = pl.BlockSpec((1, d), lambda i: (0, 0))
    tile = pl.BlockSpec((tm, d), lambda i: (i, 0))
    return _pcall(
        body, name=name,
        out_shape=(jax.ShapeDtypeStruct((s, d), F32), jax.ShapeDtypeStruct((s, d), BF16),
                   jax.ShapeDtypeStruct((8, d), F32)),
        grid=(s // tm,),
        in_specs=[tile, tile, tile, row, row],
        out_specs=(tile, tile, pl.BlockSpec((8, d), lambda i: (0, 0))),
        compiler_params=_cp("arbitrary"),
    )(dxn, xres, y, gate, ln_g)


def _ln_mod_bwd(dh, x, sc, dres, name, extra=None, tm=512):
    s, d = dh.shape
    tm = min(tm, s)

    def body(*refs):
        if extra is None:
            dh_ref, x_ref, sc_ref, dres_ref, dx_ref, sums_ref = refs
            dh_v = dh_ref[...]
        else:
            dh_ref, x_ref, sc_ref, dres_ref, dh2_ref, df_ref, wf_ref, dx_ref, sums_ref = refs
            dh_v = dh_ref[...] + dh2_ref[...] + jnp.dot(df_ref[...], wf_ref[...], preferred_element_type=F32)

        @pl.when(pl.program_id(0) == 0)
        def _():
            sums_ref[...] = jnp.zeros_like(sums_ref)

        xhat, rstd = _ln_stats(x_ref[...])
        dxh = dh_v * (1.0 + sc_ref[...])
        dx_ref[...] = dres_ref[...] + rstd * (dxh - jnp.mean(dxh, axis=-1, keepdims=True)
                                             - xhat * jnp.mean(dxh * xhat, axis=-1, keepdims=True))
        sums_ref[0:1, :] += _colsum(dh_v)
        sums_ref[1:2, :] += _colsum(dh_v * xhat)

    row = pl.BlockSpec((1, d), lambda i: (0, 0))
    tile = pl.BlockSpec((tm, d), lambda i: (i, 0))
    in_specs = [tile, tile, row, tile]
    args = [dh, x, sc, dres]
    if extra is not None:
        in_specs += [tile, pl.BlockSpec((tm, AUG), lambda i: (i, 0)), pl.BlockSpec((AUG, d), lambda i: (0, 0))]
        args += list(extra)
    return _pcall(
        body, name=name,
        out_shape=(jax.ShapeDtypeStruct((s, d), F32), jax.ShapeDtypeStruct((8, d), F32)),
        grid=(s // tm,), in_specs=in_specs,
        out_specs=(tile, pl.BlockSpec((8, d), lambda i: (0, 0))),
        compiler_params=_cp("arbitrary"),
    )(*args)


def _loss_grad(xn, target, name, tm=512):
    s, d = xn.shape
    tm = min(tm, s)
    n = s // tm

    def body(x_ref, t_ref, dx_ref, loss_ref, acc_ref):
        i = pl.program_id(0)

        @pl.when(i == 0)
        def _():
            acc_ref[...] = jnp.zeros_like(acc_ref)

        err = x_ref[...] - t_ref[...]
        dx_ref[...] = err * (1.0 / d)
        acc_ref[...] += _colsum(err * err)

        @pl.when(i == n - 1)
        def _():
            loss_ref[...] = jnp.zeros_like(loss_ref) + (0.5 / d) * jnp.sum(acc_ref[...])

    tile = pl.BlockSpec((tm, d), lambda i: (i, 0))
    return _pcall(
        body, name=name,
        out_shape=(jax.ShapeDtypeStruct((s, d), F32), jax.ShapeDtypeStruct((8, 128), F32)),
        grid=(n,), in_specs=[tile, tile],
        out_specs=(tile, pl.BlockSpec((8, 128), lambda i: (0, 0))),
        scratch_shapes=[pltpu.VMEM((1, d), F32)],
        compiler_params=_cp("arbitrary"),
    )(xn, target)


def _sigmoid(v):
    return 0.5 * jnp.tanh(0.5 * v) + 0.5


def _merge_fwd(o_a, o_c, w_a, w_c, gates, name, tm=512):
    s, wd = o_a.shape
    d = w_a.shape[1]
    tm = min(tm, s)

    def body(oa_ref, oc_ref, wa_ref, wc_ref, ga_ref, gc_ref, m_ref):
        ba = jnp.dot(oa_ref[...], wa_ref[...], preferred_element_type=F32)
        bc = jnp.dot(oc_ref[...], wc_ref[...], preferred_element_type=F32)
        m_ref[...] = (_sigmoid(ga_ref[...]) * ba + _sigmoid(gc_ref[...]) * bc).astype(m_ref.dtype)

    o_spec = pl.BlockSpec((tm, wd), lambda i: (i, 0))
    w_spec = pl.BlockSpec((wd, d), lambda i: (0, 0))
    return _pcall(
        body, name=name, out_shape=jax.ShapeDtypeStruct((s, d), BF16), grid=(s // tm,),
        in_specs=[o_spec, o_spec, w_spec, w_spec,
                  pl.BlockSpec((tm, d), lambda i: (i, 0)), pl.BlockSpec((tm, d), lambda i: (i, 1))],
        out_specs=pl.BlockSpec((tm, d), lambda i: (i, 0)),
        compiler_params=_cp("parallel"),
    )(o_a, o_c, w_a, w_c, gates, gates)


def _merge_bwd(dmix, w_out_t, o_a, o_c, w_a, w_c, w_a_t, w_c_t, gates, name, tm=512):
    s, wd = o_a.shape
    d = w_a.shape[1]
    tm = min(tm, s)

    def body(dmix_ref, wot_ref, oa_ref, oc_ref, wa_ref, wc_ref, wat_ref, wct_ref, ga_ref, gc_ref,
             dba_ref, dbc_ref, doa_ref, doc_ref, dg_ref):
        dm = jnp.dot(dmix_ref[...], wot_ref[...], preferred_element_type=F32)
        for half, (o_ref, w_ref, wt_ref, g_ref, db_ref, do_ref) in enumerate((
                (oa_ref, wa_ref, wat_ref, ga_ref, dba_ref, doa_ref),
                (oc_ref, wc_ref, wct_ref, gc_ref, dbc_ref, doc_ref))):
            b = jnp.dot(o_ref[...], w_ref[...], preferred_element_type=F32)
            sg = _sigmoid(g_ref[...])
            db = (dm * sg).astype(BF16)
            db_ref[...] = db
            dg_ref[:, d * half:d * (half + 1)] = (dm * b * sg * (1.0 - sg)).astype(dg_ref.dtype)
            do_ref[...] = jnp.dot(db, wt_ref[...], preferred_element_type=F32).astype(do_ref.dtype)

    row_d = pl.BlockSpec((tm, d), lambda i: (i, 0))
    row_w = pl.BlockSpec((tm, wd), lambda i: (i, 0))
    full = lambda shp: pl.BlockSpec(shp, lambda i: (0, 0))
    return _pcall(
        body, name=name,
        out_shape=(jax.ShapeDtypeStruct((s, d), BF16), jax.ShapeDtypeStruct((s, d), BF16),
                   jax.ShapeDtypeStruct((s, wd), BF16), jax.ShapeDtypeStruct((s, wd), BF16),
                   jax.ShapeDtypeStruct((s, 2 * d), BF16)),
        grid=(s // tm,),
        in_specs=[row_d, full((d, d)), row_w, row_w, full((wd, d)), full((wd, d)), full((d, wd)), full((d, wd)),
                  row_d, pl.BlockSpec((tm, d), lambda i: (i, 1))],
        out_specs=(row_d, row_d, row_w, row_w, pl.BlockSpec((tm, 2 * d), lambda i: (i, 0))),
        compiler_params=_cp("parallel"),
    )(dmix, w_out_t, o_a, o_c, w_a, w_c, w_a_t, w_c_t, gates, gates)


def _shift_down(cur, k, fill_rows):
    out = pltpu.roll(cur, k, axis=0)
    rid = lax.broadcasted_iota(jnp.int32, cur.shape, 0)
    for r, fill in enumerate(fill_rows):
        out = jnp.where(rid == r, fill, out)
    return out


def _shift_up(cur, k, fill_rows):
    n = cur.shape[0]
    out = pltpu.roll(cur, n - k, axis=0)
    rid = lax.broadcasted_iota(jnp.int32, cur.shape, 0)
    for r, fill in enumerate(fill_rows):
        out = jnp.where(rid == n - k + r, fill, out)
    return out


def _conv_rows(cur, prev8, first, w, b):
    p6 = jnp.where(first, 0.0, prev8[6:7, :])
    p7 = jnp.where(first, 0.0, prev8[7:8, :])
    m1 = _shift_down(cur, 1, [p7])
    m2 = _shift_down(cur, 2, [p6, p7])
    u = b + w[0:1, :] * m2 + w[1:2, :] * m1 + w[2:3, :] * cur
    return u, m1, m2


def _conv_specs(tr, tc, order):
    r8 = tr // 8
    at = lambda f: (lambda *g: f(*order(*g)))
    return [pl.BlockSpec((2, tr, tc), at(lambda i, j: (0, i, j))),
            pl.BlockSpec((2, 8, tc), at(lambda i, j: (0, jnp.maximum(i * r8 - 1, 0), j))),
            pl.BlockSpec((2, 3, tc), at(lambda i, j: (0, 0, j))),
            pl.BlockSpec((2, 1, tc), at(lambda i, j: (0, 0, j)))]


def _conv_act_fwd(u0, conv_w, conv_b, name, tr=1024, tc=256):
    _, s, f = u0.shape
    tr, tc = _tile(s, tr, 8), _tile(f, tc)

    def body(u_ref, p_ref, w_ref, b_ref, act_ref):
        first = pl.program_id(0) == 0
        a, _, _ = _conv_rows(u_ref[0], p_ref[0], first, w_ref[0], b_ref[0])
        v, _, _ = _conv_rows(u_ref[1], p_ref[1], first, w_ref[1], b_ref[1])
        act_ref[...] = (a * _sigmoid(a) * v).astype(act_ref.dtype)

    return _pcall(
        body, name=name, out_shape=jax.ShapeDtypeStruct((s, f), BF16), grid=(s // tr, f // tc),
        in_specs=_conv_specs(tr, tc, lambda i, j: (i, j)),
        out_specs=pl.BlockSpec((tr, tc), lambda i, j: (i, j)),
        compiler_params=_cp("parallel", "parallel"),
    )(u0, u0, conv_w, conv_b)


def _act_grads(dact, a, v):
    sg = _sigmoid(a)
    return dact * v * sg * (1.0 + a * (1.0 - sg)), dact * a * sg


def _conv_act_bwd(dact, u0, conv_w, conv_b, name, tr=1024, tc=256):
    _, s, f = u0.shape
    tr, tc = _tile(s, tr, 8), _tile(f, tc)
    n = s // tr
    r8 = tr // 8

    def body(dact_ref, dnext_ref, u_ref, p_ref, unext_ref, w_ref, b_ref, du0_ref, sums_ref):
        i = pl.program_id(1)
        first, last = i == 0, i == n - 1

        @pl.when(first)
        def _():
            sums_ref[...] = jnp.zeros_like(sums_ref)

        cur = (u_ref[0], u_ref[1])
        (a, a1, a2), (v, v1, v2) = [_conv_rows(cur[hf], p_ref[hf], first, w_ref[hf], b_ref[hf]) for hf in range(2)]
        da, dv = _act_grads(dact_ref[...], a, v)
        (an, _, _), (vn, _, _) = [_conv_rows(unext_ref[hf], cur[hf][tr - 8:tr, :], False, w_ref[hf], b_ref[hf])
                                  for hf in range(2)]
        dan, dvn = _act_grads(dnext_ref[...], an, vn)
        for hf, g, gn, shifted in ((0, da, dan, (a2, a1, cur[0])), (1, dv, dvn, (v2, v1, cur[1]))):
            w = w_ref[hf]
            n0 = jnp.where(last, 0.0, gn[0:1, :])
            n1 = jnp.where(last, 0.0, gn[1:2, :])
            du0 = w[2:3, :] * g + w[1:2, :] * _shift_up(g, 1, [n0]) + w[0:1, :] * _shift_up(g, 2, [n0, n1])
            du0_ref[hf] = du0.astype(du0_ref.dtype)
            for r in range(3):
                sums_ref[hf, r:r + 1, :] += _colsum(g * shifted[r])
            sums_ref[hf, 3:4, :] += _colsum(g)

    nxt = lambda i: jnp.minimum((i + 1) * r8, s // 8 - 1)
    order = lambda j, i: (i, j)
    specs = _conv_specs(tr, tc, order)
    return _pcall(
        body, name=name,
        out_shape=(jax.ShapeDtypeStruct((2, s, f), BF16), jax.ShapeDtypeStruct((2, 8, f), F32)),
        grid=(f // tc, n),
        in_specs=[pl.BlockSpec((tr, tc), lambda j, i: (i, j)), pl.BlockSpec((8, tc), lambda j, i: (nxt(i), j)),
                  specs[0], specs[1], pl.BlockSpec((2, 8, tc), lambda j, i: (0, nxt(i), j)), specs[2], specs[3]],
        out_specs=(pl.BlockSpec((2, tr, tc), lambda j, i: (0, i, j)), pl.BlockSpec((2, 8, tc), lambda j, i: (0, 0, j))),
        compiler_params=_cp("parallel", "arbitrary"),
    )(dact, dact, u0, u0, u0, conv_w, conv_b)


def _cumsum_rows(v, reverse=False):
    r = v.shape[0]
    i128 = lax.broadcasted_iota(jnp.int32, (128, 128), 0), lax.broadcasted_iota(jnp.int32, (128, 128), 1)
    ir = lax.broadcasted_iota(jnp.int32, (r, r), 0), lax.broadcasted_iota(jnp.int32, (r, r), 1)
    in_row = (i128[0] >= i128[1] if reverse else i128[0] <= i128[1]).astype(F32)
    rows = (ir[1] > ir[0] if reverse else ir[1] < ir[0]).astype(F32)
    within = jnp.dot(v, in_row, preferred_element_type=F32, precision=HI)
    tot = jnp.broadcast_to(within[:, 0:1] if reverse else within[:, 127:128], (r, 128))
    return within + jnp.dot(rows, tot, preferred_element_type=F32, precision=HI)


def _forget_fwd(f_t, b_f, name):
    h, r, _ = f_t.shape

    def body(f_ref, b_ref, o_ref):
        z = f_ref[...] + b_ref[...]
        logf = jnp.minimum(z, 0.0) - jnp.log(1.0 + jnp.exp(-jnp.abs(z)))
        rest = _cumsum_rows(logf) * LOG2E
        for i in range(N_SPLIT):
            piece = rest.astype(BF16).astype(F32)
            o_ref[i] = piece
            rest = rest - piece

    return _pcall(
        body, name=name, out_shape=jax.ShapeDtypeStruct((N_SPLIT, h, r, 128), F32), grid=(h,),
        in_specs=[pl.BlockSpec((None, r, 128), lambda i: (i, 0, 0)), pl.BlockSpec((None, 1, 128), lambda i: (i, 0, 0))],
        out_specs=pl.BlockSpec((N_SPLIT, None, r, 128), lambda i: (0, i, 0, 0)),
        compiler_params=_cp("parallel"),
    )(f_t, b_f)


def _forget_bwd(d_cum, f_t, b_f, name):
    h, r, _ = f_t.shape

    def body(g_ref, f_ref, b_ref, df_ref, db_ref):
        df = _cumsum_rows(g_ref[...], reverse=True) * _sigmoid(-(f_ref[...] + b_ref[...]))
        df_ref[...] = df
        db_ref[...] = jnp.zeros_like(db_ref) + jnp.sum(df)

    blk = pl.BlockSpec((None, r, 128), lambda i: (i, 0, 0))
    one = pl.BlockSpec((None, 1, 128), lambda i: (i, 0, 0))
    return _pcall(
        body, name=name,
        out_shape=(jax.ShapeDtypeStruct((h, r, 128), F32), jax.ShapeDtypeStruct((h, 1, 128), F32)),
        grid=(h,), in_specs=[blk, blk, one], out_specs=(blk, one),
        compiler_params=_cp("parallel"),
    )(d_cum, f_t, b_f)


_NT = (((1,), (1,)), ((), ()))


def _causal_keep(tk, tq):
    return lax.broadcasted_iota(jnp.int32, (tk, tq), 0) <= lax.broadcasted_iota(jnp.int32, (tk, tq), 1)


_TN = (((0,), (0,)), ((), ()))
PAIR = 2
V_FOX = 2 * WIDTH // 128
Q_CHUNK, K_CHUNK, V_CHUNK = (3 * WIDTH) // 128, (4 * WIDTH) // 128, (5 * WIDTH) // 128


def _riding_exchange(exchange, n_in, n_out, grid):
    if exchange is None:
        return [], [], [], [], lambda refs: (refs, lambda: None)
    scatter, xs = exchange
    n = len(xs)
    hbm = pl.BlockSpec(memory_space=pltpu.HBM)

    def hook(refs):
        ins, x_refs = refs[:n_in], refs[n_in:n_in + n]
        outs, land_refs = refs[n_in + n:n_in + n + n_out], refs[n_in + n + n_out:n_in + 2 * n + n_out]
        rest = refs[n_in + 2 * n + n_out:]
        own_scratch, sems = rest[:len(rest) - 3], rest[len(rest) - 3:]
        start, wait = _direct_exchange(scatter, x_refs, land_refs, *sems)
        ids = [pl.program_id(a) for a in range(len(grid))]
        is_first = functools.reduce(jnp.logical_and, [i == 0 for i in ids])
        is_last = functools.reduce(jnp.logical_and, [i == g - 1 for i, g in zip(ids, grid)])
        pl.when(is_first)(start)
        return tuple(ins) + tuple(outs) + tuple(own_scratch), lambda: pl.when(is_last)(wait)

    return list(xs), _exchange_shapes(scatter, xs), [hbm] * n, _exchange_scratch(n), hook


def _fox_fwd(q_aug, k_aug, qkv, name, t=TQ, exchange=None):
    s = q_aug.shape[0]
    t = min(t, s)
    n = s // t
    dh = HEAD_DIM
    grid = (N_HEADS // PAIR, n)
    ex_in, ex_out, ex_specs, ex_scratch, hook = _riding_exchange(exchange, 3, 2, grid)

    def body(*refs):
        (q_ref, k_ref, v_ref, o_ref, lse_ref, m_s, l_s, acc_s), finish = hook(refs)
        qi = pl.program_id(1)
        m_s[...] = jnp.full_like(m_s, NEG)
        l_s[...] = jnp.zeros_like(l_s)
        acc_s[...] = jnp.zeros_like(acc_s)

        def step(kj, diag):
            rows = pl.ds(pl.multiple_of(kj * t, t), t)
            for hh in range(PAIR):
                st = lax.dot_general(k_ref[rows, AUG * hh:AUG * (hh + 1)], q_ref[:, AUG * hh:AUG * (hh + 1)], _NT,
                                     preferred_element_type=F32)
                if diag:
                    st = jnp.where(_causal_keep(t, t), st, NEG)
                m_prev = m_s[hh]
                m_new = jnp.maximum(m_prev, jnp.max(st, axis=0, keepdims=True))
                a = jnp.exp2(m_prev - m_new)
                pt = jnp.exp2(st - m_new)
                l_s[hh] = a * l_s[hh] + jnp.sum(pt, axis=0, keepdims=True)
                acc_s[hh] = a * acc_s[hh] + lax.dot_general(v_ref[rows, dh * hh:dh * (hh + 1)], pt.astype(BF16), _TN,
                                                            preferred_element_type=F32)
                m_s[hh] = m_new

        def off_diagonal(kj, carry):
            step(kj, False)
            return carry

        lax.fori_loop(0, qi, off_diagonal, 0)
        step(qi, True)
        for hh in range(PAIR):
            o_ref[hh] = acc_s[hh] / l_s[hh]
            lse_ref[hh, 0] = m_s[hh] + jnp.log2(l_s[hh])
        finish()

    return _pcall(
        body, name=name,
        out_shape=[jax.ShapeDtypeStruct((N_HEADS, dh, s), F32), jax.ShapeDtypeStruct((N_HEADS, n, 1, t), F32)] + ex_out,
        grid=grid,
        in_specs=[pl.BlockSpec((t, PAIR * AUG), lambda hp, i: (i, hp)),
                  pl.BlockSpec((s, PAIR * AUG), lambda hp, i: (0, hp)),
                  pl.BlockSpec((s, PAIR * dh), lambda hp, i: (0, V_FOX + hp))] + ex_specs,
        out_specs=[pl.BlockSpec((PAIR, dh, t), lambda hp, i: (hp, 0, i)),
                   pl.BlockSpec((PAIR, 1, 1, t), lambda hp, i: (hp, i, 0, 0))] + ex_specs,
        scratch_shapes=[pltpu.VMEM((PAIR, 1, t), F32), pltpu.VMEM((PAIR, 1, t), F32),
                        pltpu.VMEM((PAIR, dh, t), F32)] + ex_scratch,
        compiler_params=_cp("arbitrary", "arbitrary"),
    )(q_aug, k_aug, qkv, *ex_in)


def _row_dot(a_t, b_t, name, t=2048):
    h, dh, s = a_t.shape
    t = min(t, s)

    def body(a_ref, b_ref, o_ref):
        o_ref[...] = jnp.sum(a_ref[...].astype(F32) * b_ref[...], axis=0, keepdims=True)

    blk = pl.BlockSpec((None, dh, t), lambda hh, i: (hh, 0, i))
    return _pcall(
        body, name=name, out_shape=jax.ShapeDtypeStruct((h, 1, s), F32), grid=(h, s // t),
        in_specs=[blk, blk], out_specs=pl.BlockSpec((None, 1, t), lambda hh, i: (hh, 0, i)),
        compiler_params=_cp("parallel", "parallel"),
    )(a_t, b_t)


def _fox_bwd(q_aug, k_aug, qkv, do, lse, delta, name, t=TQ, exchange=None):
    s = q_aug.shape[0]
    dh = HEAD_DIM
    t = min(t, s)
    n = s // t
    grid = (N_HEADS // PAIR, n)
    ex_in, ex_out, ex_specs, ex_scratch, hook = _riding_exchange(exchange, 6, 5, grid)

    def body(*refs):
        (q_ref, k_ref, v_ref, do_ref, lse_ref, dl_ref, dk_ref, dv_ref, dq_ref, dsum_ref, dqsum_ref,
         dk_s, dv_s, dsum_s), finish = hook(refs)
        kj = pl.program_id(1)

        @pl.when(kj == 0)
        def _():
            dq_ref[...] = jnp.zeros_like(dq_ref)
            dqsum_ref[...] = jnp.zeros_like(dqsum_ref)

        dk_s[...] = jnp.zeros_like(dk_s)
        dv_s[...] = jnp.zeros_like(dv_s)
        dsum_s[...] = jnp.zeros_like(dsum_s)

        def step(qi, diag):
            rows = pl.ds(pl.multiple_of(qi * t, t), t)
            for hh in range(PAIR):
                q = q_ref[rows, AUG * hh:AUG * (hh + 1)]
                k = k_ref[:, AUG * hh:AUG * (hh + 1)]
                st = lax.dot_general(k, q, _NT, preferred_element_type=F32)
                if diag:
                    st = jnp.where(_causal_keep(t, t), st, NEG)
                pt = jnp.exp2(st - lse_ref[hh, qi])
                do_v = do_ref[rows, dh * hh:dh * (hh + 1)]
                dpt = lax.dot_general(v_ref[:, dh * hh:dh * (hh + 1)], do_v, _NT, preferred_element_type=F32)
                ds32 = pt * (dpt - dl_ref[hh, qi])
                dsum_s[hh] += sum(ds32[:, 128 * u:128 * (u + 1)] for u in range(t // 128))
                dqsum_ref[hh, qi] += jnp.sum(ds32, axis=0, keepdims=True)
                dst = ds32.astype(BF16)
                dv_s[hh] += jnp.dot(pt.astype(BF16), do_v, preferred_element_type=F32)
                dk_s[hh] += jnp.dot(dst, q[:, :dh], preferred_element_type=F32)
                dq_ref[hh, qi] += lax.dot_general(k[:, :dh], dst, _TN, preferred_element_type=F32)

        def off_diagonal(qi, carry):
            step(qi, False)
            return carry

        step(kj, True)
        lax.fori_loop(kj + 1, n, off_diagonal, 0)
        for hh in range(PAIR):
            dk_ref[:, dh * hh:dh * (hh + 1)] = (LN2 * dk_s[hh]).astype(dk_ref.dtype)
            dv_ref[:, dh * hh:dh * (hh + 1)] = dv_s[hh].astype(dv_ref.dtype)
            dsum_ref[:, 128 * hh:128 * (hh + 1)] = dsum_s[hh]
        finish()

    stat = pl.BlockSpec((PAIR, n, 1, t), lambda hp, j: (hp, 0, 0, 0))
    return _pcall(
        body, name=name,
        out_shape=[jax.ShapeDtypeStruct((s, WIDTH), BF16), jax.ShapeDtypeStruct((s, WIDTH), BF16),
                   jax.ShapeDtypeStruct((N_HEADS, n, dh, t), F32), jax.ShapeDtypeStruct((s, N_HEADS * 128), F32),
                   jax.ShapeDtypeStruct((N_HEADS, n, 1, t), F32)] + ex_out,
        grid=grid,
        in_specs=[pl.BlockSpec((s, PAIR * AUG), lambda hp, j: (0, hp)),
                  pl.BlockSpec((t, PAIR * AUG), lambda hp, j: (j, hp)),
                  pl.BlockSpec((t, PAIR * dh), lambda hp, j: (j, V_FOX + hp)),
                  pl.BlockSpec((s, PAIR * dh), lambda hp, j: (0, hp)),
                  stat, stat] + ex_specs,
        out_specs=[pl.BlockSpec((t, PAIR * dh), lambda hp, j: (j, hp)),
                   pl.BlockSpec((t, PAIR * dh), lambda hp, j: (j, hp)),
                   pl.BlockSpec((PAIR, n, dh, t), lambda hp, j: (hp, 0, 0, 0)),
                   pl.BlockSpec((t, PAIR * 128), lambda hp, j: (j, hp)),
                   stat] + ex_specs,
        scratch_shapes=[pltpu.VMEM((PAIR, t, dh), F32), pltpu.VMEM((PAIR, t, dh), F32),
                        pltpu.VMEM((PAIR, t, 128), F32)] + ex_scratch,
        compiler_params=_cp("arbitrary", "arbitrary"),
    )(q_aug, k_aug, qkv, do, lse, delta, *ex_in)


N_SUB = TQ // SUB


def _pair_blockdiag(x2):
    lane = lax.broadcasted_iota(jnp.int32, x2.shape, 1)
    zero = jnp.zeros_like(x2)
    return jnp.concatenate([jnp.where(lane < HEAD_DIM, x2, zero), jnp.where(lane >= HEAD_DIM, x2, zero)], axis=0)


def _pair_lanes(ref, j):
    return jnp.concatenate([ref[hh, 0, :, SUB * j:SUB * (j + 1)] for hh in range(PAIR)], axis=1)


def _chunk_logits(kwin, qbd, b_ref, first_tile, j):
    which = jnp.where(first_tile, 1 + j, 0)
    bias = jnp.concatenate([b_ref[hh, which] for hh in range(PAIR)], axis=1)
    return lax.dot_general(kwin, qbd, _NT, preferred_element_type=F32) + bias


def _chunk_fwd(qkv, bias_t, name):
    s = qkv.shape[0]
    dh = HEAD_DIM
    n = s // TQ
    nsub = TQ // SUB
    prev = lambda i: jnp.maximum(i - 1, 0)

    def body(q_ref, kp_ref, kc_ref, vp_ref, vc_ref, b_ref, o_ref, lse_ref):
        first_tile = pl.program_id(1) == 0
        kw = jnp.concatenate([kp_ref[...], kc_ref[...]], axis=0)
        vw = jnp.concatenate([vp_ref[...], vc_ref[...]], axis=0)
        for j in range(nsub):
            rows, win = slice(SUB * j, SUB * (j + 1)), slice(SUB * j, SUB * j + SUBK)
            st = _chunk_logits(kw[win], _pair_blockdiag(q_ref[rows, :] * QK_SCALE2), b_ref, first_tile, j)
            m = jnp.max(st, axis=0, keepdims=True)
            pt = jnp.exp2(st - m)
            l = jnp.sum(pt, axis=0, keepdims=True)
            ot = lax.dot_general(vw[win], pt.astype(BF16), _TN, preferred_element_type=F32)
            stat = m + jnp.log2(l)
            for hh in range(PAIR):
                lanes = slice(SUB * hh, SUB * (hh + 1))
                o_ref[hh, :, rows] = ot[dh * hh:dh * (hh + 1), lanes] / l[:, lanes]
                lse_ref[hh, 0, :, rows] = stat[:, lanes]

    blk = lambda col, m: pl.BlockSpec((TQ, PAIR * dh), lambda hp, i: (m(i), col + hp))
    same = lambda i: i
    return _pcall(
        body, name=name,
        out_shape=(jax.ShapeDtypeStruct((N_HEADS, dh, s), F32), jax.ShapeDtypeStruct((N_HEADS, n, 1, TQ), F32)),
        grid=(N_HEADS // PAIR, n),
        in_specs=[blk(Q_CHUNK, same), blk(K_CHUNK, prev), blk(K_CHUNK, same), blk(V_CHUNK, prev), blk(V_CHUNK, same),
                  pl.BlockSpec((PAIR, 1 + N_SUB, SUBK, SUB), lambda hp, i: (hp, 0, 0, 0))],
        out_specs=(pl.BlockSpec((PAIR, dh, TQ), lambda hp, i: (hp, 0, i)),
                   pl.BlockSpec((PAIR, 1, 1, TQ), lambda hp, i: (hp, i, 0, 0))),
        compiler_params=_cp("parallel", "parallel"),
    )(qkv, qkv, qkv, qkv, qkv, bias_t)


def _chunk_bwd(qkv, do, lse, delta, bias_t, name):
    s = qkv.shape[0]
    dh = HEAD_DIM
    n = s // TQ
    nsub = TQ // SUB
    cur = lambda i: jnp.minimum(i, n - 1)
    prev = lambda i: jnp.maximum(cur(i) - 1, 0)
    done = lambda i: jnp.maximum(i - 1, 0)

    def body(q_ref, kp_ref, kc_ref, vp_ref, vc_ref, do_ref, lse_ref, dl_ref, b_ref,
             dq_ref, dk_ref, dv_ref, db_ref, dkw_s, dvw_s, ck_s, cv_s):
        i = pl.program_id(1)

        @pl.when(i == 0)
        def _():
            db_ref[...] = jnp.zeros_like(db_ref)
            ck_s[...] = jnp.zeros_like(ck_s)
            cv_s[...] = jnp.zeros_like(cv_s)

        dkw_s[...] = jnp.zeros_like(dkw_s)
        dvw_s[...] = jnp.zeros_like(dvw_s)

        @pl.when(i < n)
        def _():
            first_tile = i == 0
            kw = jnp.concatenate([kp_ref[...], kc_ref[...]], axis=0)
            vw = jnp.concatenate([vp_ref[...], vc_ref[...]], axis=0)
            for j in range(nsub):
                rows, win = slice(SUB * j, SUB * (j + 1)), slice(SUB * j, SUB * j + SUBK)
                qbd = _pair_blockdiag(q_ref[rows, :] * QK_SCALE2)
                dobd = _pair_blockdiag(do_ref[rows, :])
                st = _chunk_logits(kw[win], qbd, b_ref, first_tile, j)
                pt = jnp.exp2(st - _pair_lanes(lse_ref, j))
                dpt = lax.dot_general(vw[win], dobd, _NT, preferred_element_type=F32)
                dst = pt * (dpt - _pair_lanes(dl_ref, j))
                dsb = dst.astype(BF16)
                dvw_s[win, :] += jnp.dot(pt.astype(BF16), dobd, preferred_element_type=F32)
                dkw_s[win, :] += jnp.dot(dsb, qbd, preferred_element_type=F32)
                dqt = QK_SCALE * lax.dot_general(kw[win], dsb, _TN, preferred_element_type=F32)
                for hh in range(PAIR):
                    lanes = slice(SUB * hh, SUB * (hh + 1))
                    db_ref[hh] += dst[:, lanes]
                    dq_ref[hh, :, rows] = dqt[dh * hh:dh * (hh + 1), lanes]

        dk_ref[...] = (LN2 * (ck_s[...] + dkw_s[0:TQ, :])).astype(dk_ref.dtype)
        dv_ref[...] = (cv_s[...] + dvw_s[0:TQ, :]).astype(dv_ref.dtype)
        ck_s[...] = dkw_s[TQ:2 * TQ, :]
        cv_s[...] = dvw_s[TQ:2 * TQ, :]

    blk = lambda col, m: pl.BlockSpec((TQ, PAIR * dh), lambda hp, i: (m(i), col + hp))
    stat = pl.BlockSpec((PAIR, 1, 1, TQ), lambda hp, i: (hp, cur(i), 0, 0))
    bs = pl.BlockSpec((PAIR, SUBK, SUB), lambda hp, i: (hp, 0, 0))
    return _pcall(
        body, name=name,
        out_shape=(jax.ShapeDtypeStruct((N_HEADS, dh, s), F32), jax.ShapeDtypeStruct((s, WIDTH), BF16),
                   jax.ShapeDtypeStruct((s, WIDTH), BF16), jax.ShapeDtypeStruct((N_HEADS, SUBK, SUB), F32)),
        grid=(N_HEADS // PAIR, n + 1),
        in_specs=[blk(Q_CHUNK, cur), blk(K_CHUNK, prev), blk(K_CHUNK, cur), blk(V_CHUNK, prev), blk(V_CHUNK, cur),
                  blk(0, cur), stat, stat, pl.BlockSpec((PAIR, 1 + N_SUB, SUBK, SUB), lambda hp, i: (hp, 0, 0, 0))],
        out_specs=(pl.BlockSpec((PAIR, dh, TQ), lambda hp, i: (hp, 0, cur(i))), blk(0, done), blk(0, done), bs),
        scratch_shapes=[pltpu.VMEM((2 * TQ, PAIR * dh), F32), pltpu.VMEM((2 * TQ, PAIR * dh), F32),
                        pltpu.VMEM((TQ, PAIR * dh), F32), pltpu.VMEM((TQ, PAIR * dh), F32)],
        compiler_params=_cp("parallel", "arbitrary"),
    )(qkv, qkv, qkv, qkv, qkv, do, lse, delta, bias_t)


def _mod_part(c_all, w_ada, b_ada, name):
    nl, d, n = w_ada.shape
    b = c_all.shape[0]

    def body(c_ref, w_ref, b_ref, o_ref, cond_ref):
        cv = c_ref[...]
        cond = cv * _sigmoid(cv)
        cond_ref[...] = cond
        o_ref[...] = jnp.dot(cond, w_ref[...], preferred_element_type=F32, precision=HI) + b_ref[...]

    return _pcall(
        body, name=name,
        out_shape=(jax.ShapeDtypeStruct((nl, b, n), F32), jax.ShapeDtypeStruct((b, d), F32)),
        grid=(nl,),
        in_specs=[pl.BlockSpec((b, d), lambda l: (0, 0)), pl.BlockSpec((None, d, n), lambda l: (l, 0, 0)),
                  pl.BlockSpec((None, 1, n), lambda l: (l, 0, 0))],
        out_specs=(pl.BlockSpec((None, b, n), lambda l: (l, 0, 0)), pl.BlockSpec((b, d), lambda l: (0, 0))),
        compiler_params=_cp("arbitrary"),
    )(c_all, w_ada, b_ada)


def _adamw(parts, w, m, v, name, tr=1024):
    p, rl, cp = parts[0].shape
    r, c = w.shape
    tr = _tile(rl, tr, 16)
    per = rl // tr
    c1 = 1.0 / (1.0 - ADAM_B1 ** ADAM_STEP)
    c2 = 1.0 / (1.0 - ADAM_B2 ** ADAM_STEP)

    def body(*refs):
        p_refs, (w_ref, m_ref, v_ref, g_ref, d_ref, nm_ref, nv_ref) = refs[:len(parts)], refs[len(parts):]
        for which, p_ref in enumerate(p_refs):
            pl.when(pl.program_id(0) // per == which)(
                functools.partial(update, p_ref, w_ref, m_ref, v_ref, g_ref, d_ref, nm_ref, nv_ref))

    def update(p_ref, w_ref, m_ref, v_ref, g_ref, d_ref, nm_ref, nv_ref):
        g = p_ref[0].astype(F32)
        for i in range(1, p):
            g = g + p_ref[i].astype(F32)
        g = g[:, :c]
        nm = ADAM_B1 * m_ref[...] + (1.0 - ADAM_B1) * g
        nv = ADAM_B2 * v_ref[...] + (1.0 - ADAM_B2) * (g * g)
        g_ref[...] = g
        nm_ref[...] = nm
        nv_ref[...] = nv
        d_ref[...] = -ADAM_LR * ((nm * c1) / (jnp.sqrt(nv * c2) + ADAM_EPS) + ADAM_WD * w_ref[...])

    blk = pl.BlockSpec((tr, c), lambda i: (i, 0))
    out = jax.ShapeDtypeStruct((r, c), F32)
    return _pcall(
        body, name=name, out_shape=(out, out, out, out), grid=(r // tr,),
        in_specs=[pl.BlockSpec((p, tr, cp), lambda i, _w=which: (0, jnp.clip(i - _w * per, 0, per - 1), 0))
                  for which in range(len(parts))] + [blk, blk, blk],
        out_specs=(blk, blk, blk, blk),
        compiler_params=_cp("parallel"),
    )(*parts, w, m, v)


def _pad_lanes(a):
    pad = (-a.shape[-1]) % 128
    return jnp.pad(a, [(0, 0)] * (a.ndim - 1) + [(0, pad)]) if pad else a


def _rows128(a, mult=16):
    flat = a.reshape(-1)
    n = flat.shape[0]
    per = 128 * mult
    pad = (-n) % per
    if pad:
        flat = jnp.concatenate([flat, jnp.zeros((pad,), a.dtype)])
    return flat.reshape(-1, 128)


def _pack(arrs, mult=16):
    pieces = [_rows128(a, mult) for a in arrs]
    return jnp.concatenate(pieces, axis=0), [p.shape[0] for p in pieces]


def _unpack(packed, rows, shapes, lead=()):
    out, at = [], 0
    for r, shp in zip(rows, shapes):
        n = int(np.prod(shp))
        piece = packed[..., at:at + r, :].reshape(lead + (r * 128,))[..., :n]
        out.append(piece.reshape(lead + tuple(shp)))
        at += r
    return out


def _rel_bias_tile(table):
    h = table.shape[0]
    lo = REL_CLIP - (CHUNK - 1)
    n_far = LEFT_CHUNKS * CHUNK + CHUNK - 1 - REL_CLIP
    vec = jnp.concatenate([table[:, lo:2 * REL_CLIP], jnp.repeat(table[:, 2 * REL_CLIP:], n_far + 1, axis=1)], axis=1)
    rev = vec[:, ::-1]
    n_vec = BAND + CHUNK - 1
    skew = jnp.tile(rev, (1, CHUNK + 1))[:, :CHUNK * (n_vec + 1)].reshape(h, CHUNK, n_vec + 1)
    bias = skew[:, ::-1, :BAND]
    neg = jnp.full((h, CHUNK, CHUNK), NEG, F32)
    two = jnp.concatenate([jnp.concatenate([bias, neg], axis=2), jnp.concatenate([neg, bias], axis=2)], axis=1)
    plain = two.transpose(0, 2, 1) * LOG2E
    key = np.arange(SUBK)[:, None]
    return jnp.stack([plain] + [jnp.where(key >= TQ - SUB * j, plain, NEG) for j in range(N_SUB)], axis=1)


def _rel_bias_tile_grad(dbias_t):
    h = dbias_t.shape[0]
    two = dbias_t.transpose(0, 2, 1)
    dbias = two[:, :CHUNK, :BAND] + two[:, CHUNK:, CHUNK:]
    n_vec = BAND + CHUNK - 1
    dskew = jnp.pad(dbias[:, ::-1, :], ((0, 0), (0, 0), (0, n_vec + 1 - BAND))).reshape(h, CHUNK * (n_vec + 1))
    dskew = jnp.pad(dskew, ((0, 0), (0, (CHUNK + 1) * n_vec - CHUNK * (n_vec + 1))))
    drev = jnp.sum(dskew.reshape(h, CHUNK + 1, n_vec), axis=1)
    dvec = drev[:, ::-1]
    lo = REL_CLIP - (CHUNK - 1)
    n_near = 2 * REL_CLIP - lo
    return jnp.concatenate([jnp.zeros((h, lo), F32), dvec[:, :n_near],
                            jnp.sum(dvec[:, n_near:], axis=1, keepdims=True)], axis=1)


BIG = ("w_in", "w_br_fox", "w_br_chunk", "w_out", "w_up", "w_down")
SMALL = ("b_f", "rel_bias", "conv_b", "b_ada", "ln1_g", "ln1_b", "ln2_g", "ln2_b")
ORDER = ("w_in", "b_f", "rel_bias", "w_br_fox", "w_br_chunk", "w_out", "w_up", "conv_w", "conv_b", "w_down",
         "w_ada", "b_ada", "ln1_g", "ln1_b", "ln2_g", "ln2_b")


def _dest_major(g, axis):
    shp = g.shape
    g = g.reshape(shp[:axis] + (N_DEV, shp[axis] // N_DEV) + shp[axis + 1:])
    return jnp.moveaxis(g, axis, 0)


EARLY = ("w_down", "w_up", "w_out", "w_br_fox", "w_br_chunk")
LATE = ("w_in",)
REST = ("w_br_fox", "w_br_chunk", "w_out", "w_up", "w_down")


def _shard_axis(n):
    return 0 if n in ("w_out", "w_down") else 1


def _slab(g, n):
    return _pad_lanes(_dest_major(g, _shard_axis(n)).astype(BF16))


def _layer_fwd(x, h1, wl, rest_of, mod, next_mod, alpha, gather):
    sh1, sc1, g1, sh2, sc2, g2 = mod
    s = x.shape[0]
    qkv = _mm(h1, wl["w_qkv"], out_dtype=BF16, name="proj_qkv", tk=1024)
    gates = _mm(h1, wl["w_gates"], out_dtype=F32, name="proj_gates", tk=1024)
    f_t = _mm(wl["w_f_t"], h1, out_dtype=F32, name="proj_f", tb=True, tk=1024)[:N_HEADS]
    f_t = f_t.reshape(N_HEADS, s // 128, 128)
    b_f = jnp.broadcast_to(wl["b_f"].reshape(N_HEADS, 1, 1), (N_HEADS, 1, 128))
    cum = _forget_fwd(f_t, b_f, "forget_fwd").reshape(N_SPLIT, N_HEADS, s)
    cum_cols = cum.transpose(2, 1, 0).astype(BF16)

    ones = jnp.ones((s, N_HEADS, N_SPLIT), BF16)
    zeros = jnp.zeros((s, N_HEADS, AUG - HEAD_DIM - 2 * N_SPLIT), BF16)
    q_a = qkv[:, :WIDTH].reshape(s, N_HEADS, HEAD_DIM) * QK_SCALE2
    k_a = qkv[:, WIDTH:2 * WIDTH].reshape(s, N_HEADS, HEAD_DIM)
    q_aug = jnp.concatenate([q_a, cum_cols, ones, zeros], axis=-1).reshape(s, N_HEADS * AUG)
    k_aug = jnp.concatenate([k_a, ones, -cum_cols, zeros], axis=-1).reshape(s, N_HEADS * AUG)
    o_a_t, lse_a, *gathered = _fox_fwd(q_aug, k_aug, qkv, "fox_fwd_gather_%d" % len(gather), t=FOX_FWD_T,
                                       exchange=(False, gather))
    wl.update(rest_of(gathered[:len(REST)]))
    gathered = gathered[len(REST):]
    bias_t = _rel_bias_tile(wl["rel_bias"])
    o_c_t, lse_c = _chunk_fwd(qkv, bias_t, "chunk_fwd")
    o_a = o_a_t.transpose(2, 0, 1).reshape(s, WIDTH).astype(BF16)
    o_c = o_c_t.transpose(2, 0, 1).reshape(s, WIDTH).astype(BF16)

    merged = _merge_fwd(o_a, o_c, wl["w_br_fox"], wl["w_br_chunk"], gates, "merge_fwd")
    mix, x1, h2 = _mm_res_ln(merged, wl["w_out"], x, g1, wl["ln1_g"], wl["ln1_b"], sc2, sh2, alpha, "out_ln1")
    u0 = _mm(h2, wl["w_up"], out_dtype=F32, name="ffn_up", tk=1024, tn=1408, o_halves=True)
    act = _conv_act_fwd(u0, wl["conv_w"], wl["conv_b"], "conv_act_fwd")
    y2, x2, h_next = _mm_res_ln(act, wl["w_down"], x1, g2, wl["ln2_g"], wl["ln2_b"], next_mod[0], next_mod[1],
                                alpha, "down_ln2", tk=1408)
    saved = dict(x=x, h1=h1, gates=gates, f_t=f_t, b_f=b_f, q_aug=q_aug, k_aug=k_aug, qkv=qkv,
                 bias_t=bias_t, o_a_t=o_a_t, o_c_t=o_c_t, lse_a=lse_a, lse_c=lse_c,
                 o_a=o_a, o_c=o_c, merged=merged, mix=mix, x1=x1, h2=h2, u0=u0, act=act, y2=y2)
    return x2, h_next, saved, gathered


def _layer_bwd(dx2, sv, wl, mod, alpha, send, ride_late):
    sh1, sc1, g1, sh2, sc2, g2 = mod
    s = dx2.shape[0]
    g = {}
    dres2, dy2, sums = _ln_res_bwd(dx2, sv["x1"], sv["y2"], g2, wl["ln2_g"], alpha, "ln2_bwd")
    g["ln2_g"], g["ln2_b"], dg2 = sums[0], sums[1], sums[2]
    dact = _mm(dy2, wl["w_down_t"], out_dtype=F32, name="dact", tk=1024, tn=1408)
    g["w_down"] = _mm(sv["act"], dy2, out_dtype=F32, name="g_w_down", ta=True, tm=1408, tk=2048)
    du0, csum = _conv_act_bwd(dact, sv["u0"], wl["conv_w"], wl["conv_b"], "conv_act_bwd")
    g["conv_w"] = jnp.concatenate([csum[0, 0:3], csum[1, 0:3]], axis=1)
    g["conv_b"] = jnp.concatenate([csum[0, 3], csum[1, 3]])
    dh2 = _mm(du0, wl["w_up_t"], out_dtype=F32, name="dh2", tk=2816, a_halves=True)
    g["w_up"] = _mm(sv["h2"], du0, out_dtype=F32, name="g_w_up", ta=True, tk=2048, tn=1408, b_halves=True)
    dx1, sums = _ln_mod_bwd(dh2, sv["x1"], sc2, dres2, "ln_mod2_bwd")
    dsh2, dsc2 = sums[0], sums[1]

    dres1, dmix, sums = _ln_res_bwd(dx1, sv["x"], sv["mix"], g1, wl["ln1_g"], alpha, "ln1_bwd")
    g["ln1_g"], g["ln1_b"], dg1 = sums[0], sums[1], sums[2]
    g["w_out"] = _mm(sv["merged"], dmix, out_dtype=F32, name="g_w_out", ta=True, tk=2048)
    dba, dbc, do_a, do_c, dgates = _merge_bwd(
        dmix, wl["w_out_t"], sv["o_a"], sv["o_c"], wl["w_br_fox"], wl["w_br_chunk"],
        wl["w_br_fox_t"], wl["w_br_chunk_t"], sv["gates"], "merge_bwd")
    g["w_br_fox"] = _mm(sv["o_a"], dba, out_dtype=F32, name="g_w_br_fox", ta=True, tk=2048)
    g["w_br_chunk"] = _mm(sv["o_c"], dbc, out_dtype=F32, name="g_w_br_chunk", ta=True, tk=2048)
    tiles = lambda a, t: a.reshape(N_HEADS, s // min(t, s), 1, min(t, s))
    stats = lambda a: tiles(a, TQ)

    delta_a = _row_dot(do_a.T.reshape(N_HEADS, HEAD_DIM, s), sv["o_a_t"], "delta_fox")
    send = list(send) + [_slab(g[n], n) for n in EARLY]
    dk_a, dv_a, dq_a_t, dsum, dqsum, *landed = _fox_bwd(
        sv["q_aug"], sv["k_aug"], sv["qkv"], do_a, tiles(sv["lse_a"], FOX_BWD_T), tiles(delta_a, FOX_BWD_T),
        "fox_bwd_scatter_%d" % len(send), t=FOX_BWD_T, exchange=(True, send))
    dq_a = (dq_a_t * QK_SCALE).transpose(1, 3, 0, 2).reshape(s, WIDTH).astype(BF16)
    d_cum = dqsum.reshape(N_HEADS, s) - jnp.sum(dsum.reshape(s, N_HEADS, 128), axis=-1).T
    df_t, db_f = _forget_bwd(d_cum.reshape(N_HEADS, s // 128, 128), sv["f_t"], sv["b_f"], "forget_bwd")
    g["b_f"] = db_f[:, 0, 0]
    df_t = df_t.reshape(N_HEADS, s)

    delta_c = _row_dot(do_c.T.reshape(N_HEADS, HEAD_DIM, s), sv["o_c_t"], "delta_chunk")
    dq_c_t, dk_c, dv_c, dbias_t = _chunk_bwd(sv["qkv"], do_c, sv["lse_c"], stats(delta_c), sv["bias_t"], "chunk_bwd")
    g["rel_bias"] = _rel_bias_tile_grad(dbias_t)
    dq_c = dq_c_t.transpose(2, 0, 1).reshape(s, WIDTH).astype(BF16)

    dqkv = jnp.concatenate([dq_a, dk_a, dv_a, dq_c, dk_c, dv_c], axis=1)
    df_pad = jnp.zeros((16 - N_HEADS, s), F32)
    df16 = jnp.concatenate([df_t, df_pad], axis=0).astype(BF16)
    df_cols = jnp.concatenate([df16.T, jnp.zeros((s, AUG - 16), BF16)], axis=1)

    g_qkv = _mm(sv["h1"], dqkv, out_dtype=F32, name="g_w_qkv", ta=True, tk=2048)
    g_gates = _mm(sv["h1"], dgates, out_dtype=F32, name="g_w_gates", ta=True, tk=2048)
    g_f_t = _mm(df16, sv["h1"], out_dtype=F32, name="g_w_f", tk=1024)[:N_HEADS]
    g["w_in"] = jnp.concatenate([g_qkv[:, :3 * WIDTH], g_f_t.T, g_qkv[:, 3 * WIDTH:], g_gates], axis=1)
    late = [_slab(g[n], n) for n in LATE] if ride_late else None
    dh1, *landed_late = _mm(dqkv, wl["w_qkv_t"], out_dtype=F32, name="dh1_qkv_scatter", tk=1536,
                            exchange=(True, late)) if ride_late else \
        [_mm(dqkv, wl["w_qkv_t"], out_dtype=F32, name="dh1_qkv", tk=1536)]
    dh1g = _mm(dgates, wl["w_gates_t"], out_dtype=F32, name="dh1_gates", tk=2048)
    dx, sums = _ln_mod_bwd(dh1, sv["x"], sc1, dres1, "ln_mod1_bwd", extra=(dh1g, df_cols, wl["w_f_pad"]))
    dsh1, dsc1 = sums[0], sums[1]
    g["mod"] = jnp.concatenate([dsh1, dsc1, dg1, dsh2, dsc2, dg2])
    return dx, g, landed, landed_late


def kernel(x, c, w_in, b_f, rel_bias, w_br_fox, w_br_chunk, w_out, w_up, conv_w, conv_b, w_down, w_ada, b_ada, ln1_g, ln1_b, ln2_g, ln2_b, loss_target, m_w_in, m_b_f, m_rel_bias, m_w_br_fox, m_w_br_chunk, m_w_out, m_w_up, m_conv_w, m_conv_b, m_w_down, m_w_ada, m_b_ada, m_ln1_g, m_ln1_b, m_ln2_g, m_ln2_b, v_w_in, v_b_f, v_rel_bias, v_w_br_fox, v_w_br_chunk, v_w_out, v_w_up, v_conv_w, v_conv_b, v_w_down, v_w_ada, v_b_ada, v_ln1_g, v_ln1_b, v_ln2_g, v_ln2_b):
    w = dict(w_in=w_in, b_f=b_f, rel_bias=rel_bias, w_br_fox=w_br_fox, w_br_chunk=w_br_chunk, w_out=w_out,
             w_up=w_up, conv_w=conv_w, conv_b=conv_b, w_down=w_down, w_ada=w_ada, b_ada=b_ada,
             ln1_g=ln1_g, ln1_b=ln1_b, ln2_g=ln2_g, ln2_b=ln2_b)
    m = dict(w_in=m_w_in, b_f=m_b_f, rel_bias=m_rel_bias, w_br_fox=m_w_br_fox, w_br_chunk=m_w_br_chunk,
             w_out=m_w_out, w_up=m_w_up, conv_w=m_conv_w, conv_b=m_conv_b, w_down=m_w_down, w_ada=m_w_ada,
             b_ada=m_b_ada, ln1_g=m_ln1_g, ln1_b=m_ln1_b, ln2_g=m_ln2_g, ln2_b=m_ln2_b)
    v = dict(w_in=v_w_in, b_f=v_b_f, rel_bias=v_rel_bias, w_br_fox=v_w_br_fox, w_br_chunk=v_w_br_chunk,
             w_out=v_w_out, w_up=v_w_up, conv_w=v_conv_w, conv_b=v_conv_b, w_down=v_w_down, w_ada=v_w_ada,
             b_ada=v_b_ada, ln1_g=v_ln1_g, ln1_b=v_ln1_b, ln2_g=v_ln2_g, ln2_b=v_ln2_b)
    depth, d, _ = w_in.shape
    s = x.shape[1]
    alpha = (2.0 * depth) ** 0.25
    me = 4 * lax.axis_index("x") + 2 * lax.axis_index("y") + lax.axis_index("c")
    x0 = x.reshape(s, d)
    target = loss_target.reshape(s, d)

    small_in, small_rows = _pack([c, conv_w], mult=8)
    small_all = _all_gather([small_in], "gather_c_conv")[0]
    c_all, conv_w_all = _unpack(small_all, small_rows, [c.shape, conv_w.shape], lead=(N_DEV,))
    c_all = c_all.reshape(N_DEV, d)
    conv_w_full = conv_w_all.transpose(1, 2, 0, 3).reshape(depth, conv_w.shape[1], -1)
    n_ada = w_ada.shape[2]
    b_ada_mine = lax.dynamic_slice_in_dim(b_ada, me * n_ada, n_ada, axis=1).reshape(depth, 1, n_ada)
    mod_part, cond_all = _mod_part(c_all, w_ada, b_ada_mine, "mod_part")
    mod_all = _all_gather([mod_part.reshape(depth * N_DEV, n_ada)], "gather_mod")[0]
    mod_all = mod_all.reshape(N_DEV, depth, N_DEV, n_ada)
    mod_mine = lax.dynamic_index_in_dim(mod_all, me, axis=2, keepdims=False)
    mod_mine = mod_mine.transpose(1, 0, 2).reshape(depth, N_MOD, 1, d)

    shard = lambda n, l: _pad_lanes(w[n][l].astype(BF16))
    cols = np.cumsum([0, WIDTH, WIDTH, WIDTH, N_HEADS, WIDTH, WIDTH, WIDTH, d, d])

    def whole(n, p):
        rows, ncol = w[n].shape[1:]
        p = jnp.moveaxis(p[..., :ncol], 0, _shard_axis(n))
        return p.reshape((rows * N_DEV, ncol) if _shard_axis(n) == 0 else (rows, ncol * N_DEV))

    def first_weights(l, w_in_parts):
        wi = whole("w_in", w_in_parts)
        w_qkv = jnp.concatenate([wi[:, :cols[3]], wi[:, cols[4]:cols[7]]], axis=1)
        w_gates = wi[:, cols[7]:]
        w_f_t = jnp.concatenate([wi[:, cols[3]:cols[4]].T, jnp.zeros((16 - N_HEADS, d), BF16)], axis=0)
        w_f_pad = jnp.concatenate([w_f_t, jnp.zeros((AUG - 16, d), BF16)], axis=0)
        row = lambda a: a[l].reshape(1, -1)
        return dict(
            w_qkv=w_qkv, w_gates=w_gates, w_f_t=w_f_t, w_f_pad=w_f_pad, w_qkv_t=w_qkv.T, w_gates_t=w_gates.T,
            conv_w=conv_w_full[l].reshape(3, 2, -1).transpose(1, 0, 2), conv_b=conv_b[l].reshape(2, 1, -1),
            b_f=b_f[l], rel_bias=rel_bias[l],
            ln1_g=row(ln1_g), ln1_b=row(ln1_b), ln2_g=row(ln2_g), ln2_b=row(ln2_b))

    def rest_weights(parts):
        out = {}
        for n, p in zip(REST, parts):
            out[n] = whole(n, p)
            out[n + "_t"] = out[n].T
        return out

    xs, saved, layers = x0, [], []
    hs = _ln_mod(x0, mod_mine[0][1], mod_mine[0][0], "ln_mod1")
    w_in_parts = _all_gather([shard("w_in", 0)], "gather_weights")[0]
    for l in range(depth):
        layers.append(first_weights(l, w_in_parts))
        nxt = mod_mine[min(l + 1, depth - 1)]
        gather = [shard(n, l) for n in REST] + ([shard("w_in", l + 1)] if l + 1 < depth else [])
        xs, hs, sv, extra = _layer_fwd(xs, hs, layers[l], rest_weights, list(mod_mine[l]), (nxt[1], nxt[0]), alpha,
                                       gather)
        w_in_parts = extra[0] if extra else None
        saved.append(sv)
    dxs, loss_part = _loss_grad(xs, target, "loss_grad")
    loss = lax.psum(loss_part[0, 0], AXES)
    grads, big_parts, send = [None] * depth, {}, []
    for l in reversed(range(depth)):
        dxs, grads[l], landed, landed_late = _layer_bwd(dxs, saved[l], layers[l], list(mod_mine[l]), alpha, send,
                                                        ride_late=(l == 0))
        for n, part in zip([(n, l + 1) for n in LATE if send] + [(n, l) for n in EARLY], landed):
            big_parts[n] = part
        for n, part in zip(LATE, landed_late):
            big_parts[n, l] = part
        send = [_slab(grads[l][n], n) for n in LATE] if l else []
    grad_x = dxs.reshape(1, s, d)
    stack = lambda n: jnp.stack([grads[l][n] for l in range(depth)])

    rep = ["b_ada"] + [n for n in SMALL if n != "b_ada"]
    small_g = [stack("mod")] + [stack(n) for n in rep[1:]] + [stack("conv_w")]
    small_pack, small_g_rows = _pack(small_g, mult=8)
    small_parts = _all_gather([small_pack], "gather_small_grads")[0]
    n_rep_rows = sum(small_g_rows[:-1])
    dmod_all = _unpack(small_parts, small_g_rows[:1], [(depth, N_MOD * d)], lead=(N_DEV,))[0]
    cw_all = _unpack(small_parts[:, n_rep_rows:], small_g_rows[-1:], [small_g[-1].shape], lead=(N_DEV,))[0]

    out = {}
    for n in BIG:
        rows2d = lambda a: a.reshape(-1, a.shape[-1])
        res = _adamw([big_parts[n, l] for l in range(depth)], rows2d(w[n]), rows2d(m[n]), rows2d(v[n]),
                     "adamw_" + n, tr=512)
        for kind, r in zip(("grad", "delta", "new_m", "new_v"), res):
            out[kind, n] = r.reshape(w[n].shape)

    dmod_mine = lax.dynamic_slice_in_dim(dmod_all, me * n_ada, n_ada, axis=2)
    g_ada = jnp.stack([_mm(cond_all.T, dmod_mine[:, l], out_dtype=F32, name="g_w_ada", tk=N_DEV, tn=n_ada,
                           precision=HI) for l in range(depth)])
    lp = [_rows128(t["w_ada"], 8) for t in (w, m, v)]
    res = _adamw([_rows128(g_ada, 8)[None]], *lp, "adamw_ada")
    for kind, r in zip(("grad", "delta", "new_m", "new_v"), res):
        out[kind, "w_ada"] = _unpack(r, [lp[0].shape[0]], [w_ada.shape])[0]

    rp = [_pack([t[n] for n in rep], mult=8)[0] for t in (w, m, v)]
    res = _adamw([small_parts[:, :n_rep_rows]], *rp, "adamw_small")
    for kind, r in zip(("grad", "delta", "new_m", "new_v"), res):
        for n, a in zip(rep, _unpack(r, small_g_rows[:-1], [w[n].shape for n in rep])):
            out[kind, n] = a

    n_cw = conv_w.shape[2]
    cw_mine = lax.dynamic_slice_in_dim(cw_all, me * n_cw, n_cw, axis=3)
    cp = [_rows128(t["conv_w"], 8) for t in (w, m, v)]
    cw_in = jnp.stack([_rows128(cw_mine[i], 8) for i in range(N_DEV)])
    res = _adamw([cw_in], *cp, "adamw_conv_w")
    for kind, r in zip(("grad", "delta", "new_m", "new_v"), res):
        out[kind, "conv_w"] = _unpack(r, [cp[0].shape[0]], [conv_w.shape])[0]

    result = [loss, grad_x]
    for kind in ("grad", "delta", "new_m", "new_v"):
        result += [out[kind, n] for n in ORDER]
    return tuple(result)
```

```python
import functools
import math

import jax
import jax.numpy as jnp
import numpy as np
from jax import lax
from jax.experimental import pallas as pl
from jax.experimental.pallas import tpu as pltpu

F32 = jnp.float32
BF16 = jnp.bfloat16
HI = lax.Precision.HIGHEST
MESH_ID = pl.DeviceIdType.MESH
AXES = ("x", "y", "c")
N_DEV = 8

N_HEADS = 8
HEAD_DIM = 64
WIDTH = N_HEADS * HEAD_DIM
CHUNK = 64
LEFT_CHUNKS = 8
BAND = (LEFT_CHUNKS + 1) * CHUNK
REL_CLIP = 128
LN_EPS = 1e-5
N_MOD = 6
QK_SCALE = 1.0 / math.sqrt(HEAD_DIM)
LOG2E = math.log2(math.e)
LN2 = math.log(2.0)
QK_SCALE2 = QK_SCALE * LOG2E
NEG = -1e30
AUG = 128
N_SPLIT = 3
TQ = 512
FOX_FWD_T = 1024
FOX_BWD_T = 1024
SUB = 2 * CHUNK
SUBK = BAND + CHUNK

ADAM_LR, ADAM_B1, ADAM_B2, ADAM_EPS, ADAM_WD, ADAM_STEP = 0.001, 0.9, 0.999, 1e-08, 0.01, 10

VMEM_LIMIT = 56 * 2 ** 20


def _pcall(body, **kw):
    return pl.pallas_call(body, **kw)


def _cp(*sem):
    return pltpu.CompilerParams(dimension_semantics=sem if sem else None, vmem_limit_bytes=VMEM_LIMIT)


def _my_place():
    return lax.axis_index("x"), lax.axis_index("y"), lax.axis_index("c")


N_PEER = N_DEV - 1


def _comm_call(body, xs, out_shapes, name):
    n = len(xs)
    hbm = pl.BlockSpec(memory_space=pltpu.HBM)
    return _pcall(
        functools.partial(body, n), name=name, out_shape=out_shapes, in_specs=[hbm] * n, out_specs=[hbm] * n,
        scratch_shapes=[pltpu.SemaphoreType.DMA((N_PEER * n,)), pltpu.SemaphoreType.DMA((N_PEER * n,)),
                        pltpu.SemaphoreType.DMA((n,))],
    )(*xs)


def _all_gather(xs, name):
    def body(n, *refs):
        x_refs, out_refs, (send_sems, recv_sems, local_sems) = refs[:n], refs[n:2 * n], refs[2 * n:]
        mx, my, mc = _my_place()
        me, sibling = (mx, my, mc), (mx, my, 1 - mc)
        chips = [(1 - mx, my), (mx, 1 - my), (1 - mx, 1 - my)]

        def slot(a, px, py, pc):
            return out_refs[a].at[4 * px + 2 * py + pc]

        def copy(a, k, block, to, own=False):
            return pltpu.make_async_remote_copy(
                src_ref=x_refs[a] if own else slot(a, *block), dst_ref=slot(a, *block),
                send_sem=send_sems.at[a * N_PEER + k], recv_sem=recv_sems.at[a * N_PEER + k],
                device_id=to, device_id_type=MESH_ID)

        arrays = range(n)
        mine = [pltpu.make_async_copy(x_refs[a], slot(a, *me), local_sems.at[a]) for a in arrays]
        first = [copy(a, 0, me, sibling, own=True) for a in arrays]
        first += [copy(a, 1 + j, me, (*chip, mc), own=True) for a in arrays for j, chip in enumerate(chips)]
        for cp in mine + first:
            cp.start()
        passed = []
        for a in arrays:
            for j, chip in enumerate(chips):
                copy(a, 1 + j, (*chip, mc), me).wait_recv()
                passed.append(copy(a, 4 + j, (*chip, mc), sibling))
                passed[-1].start()
        for a in arrays:
            copy(a, 0, sibling, me).wait_recv()
            for j, chip in enumerate(chips):
                copy(a, 4 + j, (*chip, 1 - mc), me).wait_recv()
        for cp in first + passed:
            cp.wait_send()
        for cp in mine:
            cp.wait()

    return _comm_call(body, xs, [jax.ShapeDtypeStruct((N_DEV,) + x.shape, x.dtype) for x in xs], name)


def _direct_exchange(scatter, x_refs, out_refs, send_sems, recv_sems, local_sems):
    mx, my, mc = _my_place()
    me = 4 * mx + 2 * my + mc
    mine, copies = [], []
    for a, (x_ref, out_ref) in enumerate(zip(x_refs, out_refs)):
        mine.append(pltpu.make_async_copy(x_ref.at[me] if scatter else x_ref, out_ref.at[me], local_sems.at[a]))
        for k in range(N_PEER):
            px, py, pc = mx ^ ((k + 1) >> 2), my ^ (((k + 1) >> 1) & 1), mc ^ ((k + 1) & 1)
            copies.append(pltpu.make_async_remote_copy(
                src_ref=x_ref.at[4 * px + 2 * py + pc] if scatter else x_ref, dst_ref=out_ref.at[me],
                send_sem=send_sems.at[a * N_PEER + k], recv_sem=recv_sems.at[a * N_PEER + k],
                device_id=(px, py, pc), device_id_type=MESH_ID))

    def start():
        for cp in mine + copies:
            cp.start()

    def wait():
        for cp in copies:
            cp.wait_recv()
        for cp in copies:
            cp.wait_send()
        for cp in mine:
            cp.wait()

    return start, wait


def _exchange_shapes(scatter, xs):
    return [jax.ShapeDtypeStruct(x.shape if scatter else (N_DEV,) + x.shape, x.dtype) for x in xs]


def _exchange_scratch(n):
    return [pltpu.SemaphoreType.DMA((N_PEER * n,)), pltpu.SemaphoreType.DMA((N_PEER * n,)),
            pltpu.SemaphoreType.DMA((n,))]


def _tile(n, pref, unit=128):
    if n <= pref:
        return n
    t = pref - pref % unit
    while t > unit and n % t:
        t -= unit
    assert n % t == 0, (n, pref, unit)
    return t


def _mm(a, b, *, out_dtype, name, ta=False, tb=False, tm=1024, tn=1024, tk=512, precision=None,
        a_halves=False, b_halves=False, o_halves=False, exchange=None):
    a_shape = (a.shape[1], 2 * a.shape[2]) if a_halves else a.shape
    b_shape = (b.shape[1], 2 * b.shape[2]) if b_halves else b.shape
    (k_dim, m) = a_shape if ta else a_shape[::-1]
    n = b_shape[0] if tb else b_shape[1]
    tm, tn = _tile(m, tm, 8), _tile(n // 2 if (b_halves or o_halves) else n, tn)
    tk = _tile(k_dim // 2 if a_halves else k_dim, tk)
    nk = k_dim // tk
    dims = (((0 if ta else 1,), (1 if tb else 0,)), ((), ()))
    grid = (m // tm, n // tn, nk)
    ex_in, ex_out, ex_specs, ex_scratch, hook = _riding_exchange(exchange, 2, 1, grid)

    def body(*refs):
        (a_ref, b_ref, o_ref, acc_ref), finish = hook(refs)
        k = pl.program_id(2)
        if nk == 1:
            o_ref[...] = lax.dot_general(a_ref[...], b_ref[...], dims, preferred_element_type=F32,
                                         precision=precision).astype(o_ref.dtype)
            finish()
            return

        @pl.when(k == 0)
        def _():
            acc_ref[...] = jnp.zeros_like(acc_ref)

        acc_ref[...] += lax.dot_general(a_ref[...], b_ref[...], dims, preferred_element_type=F32,
                                        precision=precision)

        @pl.when(k == nk - 1)
        def _():
            o_ref[...] = acc_ref[...].astype(o_ref.dtype)

        finish()

    def spec(shape2, pick, halves, per_half):
        if not halves:
            return pl.BlockSpec(shape2, pick)

        def index(i, j, k):
            r, c = pick(i, j, k)
            return (c // per_half, r, c % per_half)
        return pl.BlockSpec((None,) + shape2, index)

    assert not (ta and a_halves) and not (tb and b_halves)
    a_spec = spec((tk, tm), lambda i, j, k: (k, i), False, 0) if ta else \
        spec((tm, tk), lambda i, j, k: (i, k), a_halves, (k_dim // 2) // tk if a_halves else 0)
    b_spec = spec((tn, tk), lambda i, j, k: (j, k), False, 0) if tb else \
        spec((tk, tn), lambda i, j, k: (k, j), b_halves, (n // 2) // tn if b_halves else 0)
    o_spec = spec((tm, tn), lambda i, j, k: (i, j), o_halves, (n // 2) // tn if o_halves else 0)
    out_shape = (2, m, n // 2) if o_halves else (m, n)
    res = _pcall(
        body, name=name,
        out_shape=[jax.ShapeDtypeStruct(out_shape, out_dtype)] + ex_out,
        grid=grid,
        in_specs=[a_spec, b_spec] + ex_specs,
        out_specs=[o_spec] + ex_specs,
        scratch_shapes=[pltpu.VMEM((tm, tn), F32)] + ex_scratch,
        compiler_params=_cp(*(["arbitrary"] * 3 if exchange else ["parallel", "parallel", "arbitrary"])),
    )(a, b, *ex_in)
    return res if exchange else res[0]


def _ln_stats(x):
    mu = jnp.mean(x, axis=-1, keepdims=True)
    xc = x - mu
    var = jnp.mean(xc * xc, axis=-1, keepdims=True)
    rstd = lax.rsqrt(var + LN_EPS)
    return xc * rstd, rstd


def _ln_mod(x, sc, sh, name, tm=512):
    s, d = x.shape
    tm = min(tm, s)

    def body(x_ref, sc_ref, sh_ref, h_ref):
        xhat, _ = _ln_stats(x_ref[...])
        h_ref[...] = (xhat * (1.0 + sc_ref[...]) + sh_ref[...]).astype(h_ref.dtype)

    row = pl.BlockSpec((1, d), lambda i: (0, 0))
    return _pcall(
        body, name=name, out_shape=jax.ShapeDtypeStruct((s, d), BF16), grid=(s // tm,),
        in_specs=[pl.BlockSpec((tm, d), lambda i: (i, 0)), row, row],
        out_specs=pl.BlockSpec((tm, d), lambda i: (i, 0)),
        compiler_params=_cp("parallel"),
    )(x, sc, sh)


def _mm_res_ln(a, w, xres, gate, ln_g, ln_b, sc, sh, alpha, name, tm=512, tk=512):
    s, k_dim = a.shape
    d = w.shape[1]
    tm, tk = _tile(s, tm, 8), _tile(k_dim, tk)
    nk = k_dim // tk

    def body(a_ref, w_ref, x_ref, g_ref, lg_ref, lb_ref, sc_ref, sh_ref, y_ref, xn_ref, h_ref, acc_ref):
        k = pl.program_id(1)

        @pl.when(k == 0)
        def _():
            acc_ref[...] = jnp.zeros_like(acc_ref)

        acc_ref[...] += jnp.dot(a_ref[...], w_ref[...], preferred_element_type=F32)

        @pl.when(k == nk - 1)
        def _():
            y = acc_ref[...]
            y_ref[...] = y
            z = alpha * x_ref[...] + (1.0 + g_ref[...]) * y
            zhat, _ = _ln_stats(z)
            xn = zhat * lg_ref[...] + lb_ref[...]
            xn_ref[...] = xn
            xhat, _ = _ln_stats(xn)
            h_ref[...] = (xhat * (1.0 + sc_ref[...]) + sh_ref[...]).astype(h_ref.dtype)

    row = pl.BlockSpec((1, d), lambda i, k: (0, 0))
    tile = pl.BlockSpec((tm, d), lambda i, k: (i, 0))
    return _pcall(
        body, name=name,
        out_shape=(jax.ShapeDtypeStruct((s, d), F32), jax.ShapeDtypeStruct((s, d), F32),
                   jax.ShapeDtypeStruct((s, d), BF16)),
        grid=(s // tm, nk),
        in_specs=[pl.BlockSpec((tm, tk), lambda i, k: (i, k)), pl.BlockSpec((tk, d), lambda i, k: (k, 0)),
                  tile, row, row, row, row, row],
        out_specs=(tile, tile, tile),
        scratch_shapes=[pltpu.VMEM((tm, d), F32)],
        compiler_params=_cp("parallel", "arbitrary"),
    )(a, w, xres, gate, ln_g, ln_b, sc, sh)


def _colsum(v):
    return jnp.sum(v, axis=0, keepdims=True)


def _ln_res_bwd(dxn, xres, y, gate, ln_g, alpha, name, tm=512):
    s, d = dxn.shape
    tm = min(tm, s)

    def body(dxn_ref, x_ref, y_ref, g_ref, lg_ref, dres_ref, dy_ref, sums_ref):
        @pl.when(pl.program_id(0) == 0)
        def _():
            sums_ref[...] = jnp.zeros_like(sums_ref)

        dxn_v, y_v = dxn_ref[...], y_ref[...]
        one_g = 1.0 + g_ref[...]
        zhat, rstd = _ln_stats(alpha * x_ref[...] + one_g * y_v)
        dzh = dxn_v * lg_ref[...]
        dz = rstd * (dzh - jnp.mean(dzh, axis=-1, keepdims=True)
                     - zhat * jnp.mean(dzh * zhat, axis=-1, keepdims=True))
        dres_ref[...] = alpha * dz
        dy_ref[...] = (one_g * dz).astype(dy_ref.dtype)
        sums_ref[0:1, :] += _colsum(dxn_v * zhat)
        sums_ref[1:2, :] += _colsum(dxn_v)
        sums_ref[2:3, :] += _colsum(dz * y_v)

    row = pl.BlockSpec((1, d), lambda i: (0, 0))
    tile = pl.BlockSpec((tm, d), lambda i: (i, 0))
    return _pcall(
        body, name=name,
        out_shape=(jax.ShapeDtypeStruct((s, d), F32), jax.ShapeDtypeStruct((s, d), BF16),
                   jax.ShapeDtypeStruct((8, d), F32)),
        grid=(s // tm,),
        in_specs=[tile, tile, tile, row, row],
        out_specs=(tile, tile, pl.BlockSpec((8, d), lambda i: (0, 0))),
        compiler_params=_cp("arbitrary"),
    )(dxn, xres, y, gate, ln_g)


def _ln_mod_bwd(dh, x, sc, dres, name, extra=None, tm=512):
    s, d = dh.shape
    tm = min(tm, s)

    def body(*refs):
        if extra is None:
            dh_ref, x_ref, sc_ref, dres_ref, dx_ref, sums_ref = refs
            dh_v = dh_ref[...]
        else:
            dh_ref, x_ref, sc_ref, dres_ref, dh2_ref, df_ref, wf_ref, dx_ref, sums_ref = refs
            dh_v = dh_ref[...] + dh2_ref[...] + jnp.dot(df_ref[...], wf_ref[...], preferred_element_type=F32)

        @pl.when(pl.program_id(0) == 0)
        def _():
            sums_ref[...] = jnp.zeros_like(sums_ref)

        xhat, rstd = _ln_stats(x_ref[...])
        dxh = dh_v * (1.0 + sc_ref[...])
        dx_ref[...] = dres_ref[...] + rstd * (dxh - jnp.mean(dxh, axis=-1, keepdims=True)
                                             - xhat * jnp.mean(dxh * xhat, axis=-1, keepdims=True))
        sums_ref[0:1, :] += _colsum(dh_v)
        sums_ref[1:2, :] += _colsum(dh_v * xhat)

    row = pl.BlockSpec((1, d), lambda i: (0, 0))
    tile = pl.BlockSpec((tm, d), lambda i: (i, 0))
    in_specs = [tile, tile, row, tile]
    args = [dh, x, sc, dres]
    if extra is not None:
        in_specs += [tile, pl.BlockSpec((tm, AUG), lambda i: (i, 0)), pl.BlockSpec((AUG, d), lambda i: (0, 0))]
        args += list(extra)
    return _pcall(
        body, name=name,
        out_shape=(jax.ShapeDtypeStruct((s, d), F32), jax.ShapeDtypeStruct((8, d), F32)),
        grid=(s // tm,), in_specs=in_specs,
        out_specs=(tile, pl.BlockSpec((8, d), lambda i: (0, 0))),
        compiler_params=_cp("arbitrary"),
    )(*args)


def _ln_mod_res_bwd(dh, x1, sc, dres, xres, y, gate, ln_g, alpha, name, tm=512):
    s, d = dh.shape
    tm = min(tm, s)

    def body(dh_ref, x1_ref, sc_ref, dres_ref, x_ref, y_ref, g_ref, lg_ref, dout_ref, dy_ref, smod_ref, sres_ref):
        @pl.when(pl.program_id(0) == 0)
        def _():
            smod_ref[...] = jnp.zeros_like(smod_ref)
            sres_ref[...] = jnp.zeros_like(sres_ref)

        dh_v = dh_ref[...]
        xhat, rstd = _ln_stats(x1_ref[...])
        dxh = dh_v * (1.0 + sc_ref[...])
        dxn_v = dres_ref[...] + rstd * (dxh - jnp.mean(dxh, axis=-1, keepdims=True)
                                        - xhat * jnp.mean(dxh * xhat, axis=-1, keepdims=True))
        smod_ref[0:1, :] += _colsum(dh_v)
        smod_ref[1:2, :] += _colsum(dh_v * xhat)

        y_v = y_ref[...]
        one_g = 1.0 + g_ref[...]
        zhat, zstd = _ln_stats(alpha * x_ref[...] + one_g * y_v)
        dzh = dxn_v * lg_ref[...]
        dz = zstd * (dzh - jnp.mean(dzh, axis=-1, keepdims=True)
                     - zhat * jnp.mean(dzh * zhat, axis=-1, keepdims=True))
        dout_ref[...] = alpha * dz
        dy_ref[...] = (one_g * dz).astype(dy_ref.dtype)
        sres_ref[0:1, :] += _colsum(dxn_v * zhat)
        sres_ref[1:2, :] += _colsum(dxn_v)
        sres_ref[2:3, :] += _colsum(dz * y_v)

    row = pl.BlockSpec((1, d), lambda i: (0, 0))
    tile = pl.BlockSpec((tm, d), lambda i: (i, 0))
    sums = pl.BlockSpec((8, d), lambda i: (0, 0))
    return _pcall(
        body, name=name,
        out_shape=(jax.ShapeDtypeStruct((s, d), F32), jax.ShapeDtypeStruct((s, d), BF16),
                   jax.ShapeDtypeStruct((8, d), F32), jax.ShapeDtypeStruct((8, d), F32)),
        grid=(s // tm,),
        in_specs=[tile, tile, row, tile, tile, tile, row, row],
        out_specs=(tile, tile, sums, sums),
        compiler_params=_cp("arbitrary"),
    )(dh, x1, sc, dres, xres, y, gate, ln_g)


def _loss_grad(xn, target, name, tm=512):
    s, d = xn.shape
    tm = min(tm, s)
    n = s // tm

    def body(x_ref, t_ref, dx_ref, loss_ref, acc_ref):
        i = pl.program_id(0)

        @pl.when(i == 0)
        def _():
            acc_ref[...] = jnp.zeros_like(acc_ref)

        err = x_ref[...] - t_ref[...]
        dx_ref[...] = err * (1.0 / d)
        acc_ref[...] += _colsum(err * err)

        @pl.when(i == n - 1)
        def _():
            loss_ref[...] = jnp.zeros_like(loss_ref) + (0.5 / d) * jnp.sum(acc_ref[...])

    tile = pl.BlockSpec((tm, d), lambda i: (i, 0))
    return _pcall(
        body, name=name,
        out_shape=(jax.ShapeDtypeStruct((s, d), F32), jax.ShapeDtypeStruct((8, 128), F32)),
        grid=(n,), in_specs=[tile, tile],
        out_specs=(tile, pl.BlockSpec((8, 128), lambda i: (0, 0))),
        scratch_shapes=[pltpu.VMEM((1, d), F32)],
        compiler_params=_cp("arbitrary"),
    )(xn, target)


def _sigmoid(v):
    return 0.5 * jnp.tanh(0.5 * v) + 0.5


def _merge_fwd(o_a, o_c, w_a, w_c, gates, name, tm=512):
    s, wd = o_a.shape
    d = w_a.shape[1]
    tm = min(tm, s)

    def body(oa_ref, oc_ref, wa_ref, wc_ref, ga_ref, gc_ref, m_ref):
        ba = jnp.dot(oa_ref[...], wa_ref[...], preferred_element_type=F32)
        bc = jnp.dot(oc_ref[...], wc_ref[...], preferred_element_type=F32)
        m_ref[...] = (_sigmoid(ga_ref[...]) * ba + _sigmoid(gc_ref[...]) * bc).astype(m_ref.dtype)

    o_spec = pl.BlockSpec((tm, wd), lambda i: (i, 0))
    w_spec = pl.BlockSpec((wd, d), lambda i: (0, 0))
    return _pcall(
        body, name=name, out_shape=jax.ShapeDtypeStruct((s, d), BF16), grid=(s // tm,),
        in_specs=[o_spec, o_spec, w_spec, w_spec,
                  pl.BlockSpec((tm, d), lambda i: (i, 0)), pl.BlockSpec((tm, d), lambda i: (i, 1))],
        out_specs=pl.BlockSpec((tm, d), lambda i: (i, 0)),
        compiler_params=_cp("parallel"),
    )(o_a, o_c, w_a, w_c, gates, gates)


def _merge_bwd(dmix, w_out_t, o_a, o_c, w_a, w_c, w_a_t, w_c_t, gates, name, tm=512):
    s, wd = o_a.shape
    d = w_a.shape[1]
    tm = min(tm, s)

    def body(dmix_ref, wot_ref, oa_ref, oc_ref, wa_ref, wc_ref, wat_ref, wct_ref, ga_ref, gc_ref,
             dba_ref, dbc_ref, doa_ref, doc_ref, dg_ref):
        dm = jnp.dot(dmix_ref[...], wot_ref[...], preferred_element_type=F32)
        for half, (o_ref, w_ref, wt_ref, g_ref, db_ref, do_ref) in enumerate((
                (oa_ref, wa_ref, wat_ref, ga_ref, dba_ref, doa_ref),
                (oc_ref, wc_ref, wct_ref, gc_ref, dbc_ref, doc_ref))):
            b = jnp.dot(o_ref[...], w_ref[...], preferred_element_type=F32)
            sg = _sigmoid(g_ref[...])
            db = (dm * sg).astype(BF16)
            db_ref[...] = db
            dg_ref[:, d * half:d * (half + 1)] = (dm * b * sg * (1.0 - sg)).astype(dg_ref.dtype)
            do_ref[...] = jnp.dot(db, wt_ref[...], preferred_element_type=F32).astype(do_ref.dtype)

    row_d = pl.BlockSpec((tm, d), lambda i: (i, 0))
    row_w = pl.BlockSpec((tm, wd), lambda i: (i, 0))
    full = lambda shp: pl.BlockSpec(shp, lambda i: (0, 0))
    return _pcall(
        body, name=name,
        out_shape=(jax.ShapeDtypeStruct((s, d), BF16), jax.ShapeDtypeStruct((s, d), BF16),
                   jax.ShapeDtypeStruct((s, wd), BF16), jax.ShapeDtypeStruct((s, wd), BF16),
                   jax.ShapeDtypeStruct((s, 2 * d), BF16)),
        grid=(s // tm,),
        in_specs=[row_d, full((d, d)), row_w, row_w, full((wd, d)), full((wd, d)), full((d, wd)), full((d, wd)),
                  row_d, pl.BlockSpec((tm, d), lambda i: (i, 1))],
        out_specs=(row_d, row_d, row_w, row_w, pl.BlockSpec((tm, 2 * d), lambda i: (i, 0))),
        compiler_params=_cp("parallel"),
    )(dmix, w_out_t, o_a, o_c, w_a, w_c, w_a_t, w_c_t, gates, gates)


def _shift_down(cur, k, fill_rows):
    out = pltpu.roll(cur, k, axis=0)
    rid = lax.broadcasted_iota(jnp.int32, cur.shape, 0)
    for r, fill in enumerate(fill_rows):
        out = jnp.where(rid == r, fill, out)
    return out


def _shift_up(cur, k, fill_rows):
    n = cur.shape[0]
    out = pltpu.roll(cur, n - k, axis=0)
    rid = lax.broadcasted_iota(jnp.int32, cur.shape, 0)
    for r, fill in enumerate(fill_rows):
        out = jnp.where(rid == n - k + r, fill, out)
    return out


def _conv_rows(cur, prev8, first, w, b):
    p6 = jnp.where(first, 0.0, prev8[6:7, :])
    p7 = jnp.where(first, 0.0, prev8[7:8, :])
    m1 = _shift_down(cur, 1, [p7])
    m2 = _shift_down(cur, 2, [p6, p7])
    u = b + w[0:1, :] * m2 + w[1:2, :] * m1 + w[2:3, :] * cur
    return u, m1, m2


def _conv_specs(tr, tc, order):
    r8 = tr // 8
    at = lambda f: (lambda *g: f(*order(*g)))
    return [pl.BlockSpec((2, tr, tc), at(lambda i, j: (0, i, j))),
            pl.BlockSpec((2, 8, tc), at(lambda i, j: (0, jnp.maximum(i * r8 - 1, 0), j))),
            pl.BlockSpec((2, 3, tc), at(lambda i, j: (0, 0, j))),
            pl.BlockSpec((2, 1, tc), at(lambda i, j: (0, 0, j)))]


def _conv_act_fwd(u0, conv_w, conv_b, name, tr=1024, tc=256):
    _, s, f = u0.shape
    tr, tc = _tile(s, tr, 8), _tile(f, tc)

    def body(u_ref, p_ref, w_ref, b_ref, act_ref):
        first = pl.program_id(0) == 0
        a, _, _ = _conv_rows(u_ref[0], p_ref[0], first, w_ref[0], b_ref[0])
        v, _, _ = _conv_rows(u_ref[1], p_ref[1], first, w_ref[1], b_ref[1])
        act_ref[...] = (a * _sigmoid(a) * v).astype(act_ref.dtype)

    return _pcall(
        body, name=name, out_shape=jax.ShapeDtypeStruct((s, f), BF16), grid=(s // tr, f // tc),
        in_specs=_conv_specs(tr, tc, lambda i, j: (i, j)),
        out_specs=pl.BlockSpec((tr, tc), lambda i, j: (i, j)),
        compiler_params=_cp("parallel", "parallel"),
    )(u0, u0, conv_w, conv_b)


def _act_grads(dact, a, v):
    sg = _sigmoid(a)
    return dact * v * sg * (1.0 + a * (1.0 - sg)), dact * a * sg


def _conv_act_bwd(dact, u0, conv_w, conv_b, name, tr=1024, tc=256):
    _, s, f = u0.shape
    tr, tc = _tile(s, tr, 8), _tile(f, tc)
    n = s // tr
    r8 = tr // 8

    def body(dact_ref, dnext_ref, u_ref, p_ref, unext_ref, w_ref, b_ref, du0_ref, sums_ref):
        i = pl.program_id(1)
        first, last = i == 0, i == n - 1

        @pl.when(first)
        def _():
            sums_ref[...] = jnp.zeros_like(sums_ref)

        cur = (u_ref[0], u_ref[1])
        (a, a1, a2), (v, v1, v2) = [_conv_rows(cur[hf], p_ref[hf], first, w_ref[hf], b_ref[hf]) for hf in range(2)]
        da, dv = _act_grads(dact_ref[...], a, v)
        (an, _, _), (vn, _, _) = [_conv_rows(unext_ref[hf], cur[hf][tr - 8:tr, :], False, w_ref[hf], b_ref[hf])
                                  for hf in range(2)]
        dan, dvn = _act_grads(dnext_ref[...], an, vn)
        for hf, g, gn, shifted in ((0, da, dan, (a2, a1, cur[0])), (1, dv, dvn, (v2, v1, cur[1]))):
            w = w_ref[hf]
            n0 = jnp.where(last, 0.0, gn[0:1, :])
            n1 = jnp.where(last, 0.0, gn[1:2, :])
            du0 = w[2:3, :] * g + w[1:2, :] * _shift_up(g, 1, [n0]) + w[0:1, :] * _shift_up(g, 2, [n0, n1])
            du0_ref[hf] = du0.astype(du0_ref.dtype)
            for r in range(3):
                sums_ref[hf, r:r + 1, :] += _colsum(g * shifted[r])
            sums_ref[hf, 3:4, :] += _colsum(g)

    nxt = lambda i: jnp.minimum((i + 1) * r8, s // 8 - 1)
    order = lambda j, i: (i, j)
    specs = _conv_specs(tr, tc, order)
    return _pcall(
        body, name=name,
        out_shape=(jax.ShapeDtypeStruct((2, s, f), BF16), jax.ShapeDtypeStruct((2, 8, f), F32)),
        grid=(f // tc, n),
        in_specs=[pl.BlockSpec((tr, tc), lambda j, i: (i, j)), pl.BlockSpec((8, tc), lambda j, i: (nxt(i), j)),
                  specs[0], specs[1], pl.BlockSpec((2, 8, tc), lambda j, i: (0, nxt(i), j)), specs[2], specs[3]],
        out_specs=(pl.BlockSpec((2, tr, tc), lambda j, i: (0, i, j)), pl.BlockSpec((2, 8, tc), lambda j, i: (0, 0, j))),
        compiler_params=_cp("parallel", "arbitrary"),
    )(dact, dact, u0, u0, u0, conv_w, conv_b)


def _cumsum_rows(v, reverse=False):
    r = v.shape[0]
    i128 = lax.broadcasted_iota(jnp.int32, (128, 128), 0), lax.broadcasted_iota(jnp.int32, (128, 128), 1)
    ir = lax.broadcasted_iota(jnp.int32, (r, r), 0), lax.broadcasted_iota(jnp.int32, (r, r), 1)
    in_row = (i128[0] >= i128[1] if reverse else i128[0] <= i128[1]).astype(F32)
    rows = (ir[1] > ir[0] if reverse else ir[1] < ir[0]).astype(F32)
    within = jnp.dot(v, in_row, preferred_element_type=F32, precision=HI)
    tot = jnp.broadcast_to(within[:, 0:1] if reverse else within[:, 127:128], (r, 128))
    return within + jnp.dot(rows, tot, preferred_element_type=F32, precision=HI)


def _forget_fwd(f_t, b_f, name):
    h, r, _ = f_t.shape

    def body(f_ref, b_ref, o_ref):
        z = f_ref[...] + b_ref[...]
        logf = jnp.minimum(z, 0.0) - jnp.log(1.0 + jnp.exp(-jnp.abs(z)))
        rest = _cumsum_rows(logf) * LOG2E
        for i in range(N_SPLIT):
            piece = rest.astype(BF16).astype(F32)
            o_ref[i] = piece
            rest = rest - piece

    return _pcall(
        body, name=name, out_shape=jax.ShapeDtypeStruct((N_SPLIT, h, r, 128), F32), grid=(h,),
        in_specs=[pl.BlockSpec((None, r, 128), lambda i: (i, 0, 0)), pl.BlockSpec((None, 1, 128), lambda i: (i, 0, 0))],
        out_specs=pl.BlockSpec((N_SPLIT, None, r, 128), lambda i: (0, i, 0, 0)),
        compiler_params=_cp("parallel"),
    )(f_t, b_f)


def _forget_bwd(d_cum, f_t, b_f, name):
    h, r, _ = f_t.shape

    def body(g_ref, f_ref, b_ref, df_ref, db_ref):
        df = _cumsum_rows(g_ref[...], reverse=True) * _sigmoid(-(f_ref[...] + b_ref[...]))
        df_ref[...] = df
        db_ref[...] = jnp.zeros_like(db_ref) + jnp.sum(df)

    blk = pl.BlockSpec((None, r, 128), lambda i: (i, 0, 0))
    one = pl.BlockSpec((None, 1, 128), lambda i: (i, 0, 0))
    return _pcall(
        body, name=name,
        out_shape=(jax.ShapeDtypeStruct((h, r, 128), F32), jax.ShapeDtypeStruct((h, 1, 128), F32)),
        grid=(h,), in_specs=[blk, blk, one], out_specs=(blk, one),
        compiler_params=_cp("parallel"),
    )(d_cum, f_t, b_f)


_NT = (((1,), (1,)), ((), ()))


def _causal_keep(tk, tq):
    return lax.broadcasted_iota(jnp.int32, (tk, tq), 0) <= lax.broadcasted_iota(jnp.int32, (tk, tq), 1)


_TN = (((0,), (0,)), ((), ()))
PAIR = 2
V_FOX = 2 * WIDTH // 128
Q_CHUNK, K_CHUNK, V_CHUNK = (3 * WIDTH) // 128, (4 * WIDTH) // 128, (5 * WIDTH) // 128


def _riding_exchange(exchange, n_in, n_out, grid):
    if exchange is None:
        return [], [], [], [], lambda refs: (refs, lambda: None)
    scatter, xs = exchange
    n = len(xs)
    hbm = pl.BlockSpec(memory_space=pltpu.HBM)

    def hook(refs):
        ins, x_refs = refs[:n_in], refs[n_in:n_in + n]
        outs, land_refs = refs[n_in + n:n_in + n + n_out], refs[n_in + n + n_out:n_in + 2 * n + n_out]
        rest = refs[n_in + 2 * n + n_out:]
        own_scratch, sems = rest[:len(rest) - 3], rest[len(rest) - 3:]
        start, wait = _direct_exchange(scatter, x_refs, land_refs, *sems)
        ids = [pl.program_id(a) for a in range(len(grid))]
        is_first = functools.reduce(jnp.logical_and, [i == 0 for i in ids])
        is_last = functools.reduce(jnp.logical_and, [i == g - 1 for i, g in zip(ids, grid)])
        pl.when(is_first)(start)
        return tuple(ins) + tuple(outs) + tuple(own_scratch), lambda: pl.when(is_last)(wait)

    return list(xs), _exchange_shapes(scatter, xs), [hbm] * n, _exchange_scratch(n), hook


def _fox_fwd(q_aug, k_aug, qkv, name, t=TQ, exchange=None):
    s = q_aug.shape[0]
    t = min(t, s)
    n = s // t
    dh = HEAD_DIM
    grid = (N_HEADS // PAIR, n)
    ex_in, ex_out, ex_specs, ex_scratch, hook = _riding_exchange(exchange, 3, 2, grid)

    def body(*refs):
        (q_ref, k_ref, v_ref, o_ref, lse_ref, m_s, l_s, acc_s), finish = hook(refs)
        qi = pl.program_id(1)
        m_s[...] = jnp.full_like(m_s, NEG)
        l_s[...] = jnp.zeros_like(l_s)
        acc_s[...] = jnp.zeros_like(acc_s)

        def step(kj, diag):
            rows = pl.ds(pl.multiple_of(kj * t, t), t)
            for hh in range(PAIR):
                st = lax.dot_general(k_ref[rows, AUG * hh:AUG * (hh + 1)], q_ref[:, AUG * hh:AUG * (hh + 1)], _NT,
                                     preferred_element_type=F32)
                if diag:
                    st = jnp.where(_causal_keep(t, t), st, NEG)
                m_prev = m_s[hh]
                m_new = jnp.maximum(m_prev, jnp.max(st, axis=0, keepdims=True))
                a = jnp.exp2(m_prev - m_new)
                pt = jnp.exp2(st - m_new)
                l_s[hh] = a * l_s[hh] + jnp.sum(pt, axis=0, keepdims=True)
                acc_s[hh] = a * acc_s[hh] + lax.dot_general(v_ref[rows, dh * hh:dh * (hh + 1)], pt.astype(BF16), _TN,
                                                            preferred_element_type=F32)
                m_s[hh] = m_new

        def off_diagonal(kj, carry):
            step(kj, False)
            return carry

        lax.fori_loop(0, qi, off_diagonal, 0)
        step(qi, True)
        for hh in range(PAIR):
            o_ref[hh] = acc_s[hh] / l_s[hh]
            lse_ref[hh, 0] = m_s[hh] + jnp.log2(l_s[hh])
        finish()

    return _pcall(
        body, name=name,
        out_shape=[jax.ShapeDtypeStruct((N_HEADS, dh, s), F32), jax.ShapeDtypeStruct((N_HEADS, n, 1, t), F32)] + ex_out,
        grid=grid,
        in_specs=[pl.BlockSpec((t, PAIR * AUG), lambda hp, i: (i, hp)),
                  pl.BlockSpec((s, PAIR * AUG), lambda hp, i: (0, hp)),
                  pl.BlockSpec((s, PAIR * dh), lambda hp, i: (0, V_FOX + hp))] + ex_specs,
        out_specs=[pl.BlockSpec((PAIR, dh, t), lambda hp, i: (hp, 0, i)),
                   pl.BlockSpec((PAIR, 1, 1, t), lambda hp, i: (hp, i, 0, 0))] + ex_specs,
        scratch_shapes=[pltpu.VMEM((PAIR, 1, t), F32), pltpu.VMEM((PAIR, 1, t), F32),
                        pltpu.VMEM((PAIR, dh, t), F32)] + ex_scratch,
        compiler_params=_cp("arbitrary", "arbitrary"),
    )(q_aug, k_aug, qkv, *ex_in)


def _row_dot(a_t, b_t, name, t=2048):
    h, dh, s = a_t.shape
    t = min(t, s)

    def body(a_ref, b_ref, o_ref):
        o_ref[...] = jnp.sum(a_ref[...].astype(F32) * b_ref[...], axis=0, keepdims=True)

    blk = pl.BlockSpec((None, dh, t), lambda hh, i: (hh, 0, i))
    return _pcall(
        body, name=name, out_shape=jax.ShapeDtypeStruct((h, 1, s), F32), grid=(h, s // t),
        in_specs=[blk, blk], out_specs=pl.BlockSpec((None, 1, t), lambda hh, i: (hh, 0, i)),
        compiler_params=_cp("parallel", "parallel"),
    )(a_t, b_t)


def _fox_bwd(q_aug, k_aug, qkv, do, lse, delta, name, t=TQ, exchange=None):
    s = q_aug.shape[0]
    dh = HEAD_DIM
    t = min(t, s)
    n = s // t
    grid = (N_HEADS // PAIR, n)
    ex_in, ex_out, ex_specs, ex_scratch, hook = _riding_exchange(exchange, 6, 5, grid)

    def body(*refs):
        (q_ref, k_ref, v_ref, do_ref, lse_ref, dl_ref, dk_ref, dv_ref, dq_ref, dsum_ref, dqsum_ref,
         dk_s, dv_s, dsum_s), finish = hook(refs)
        kj = pl.program_id(1)

        @pl.when(kj == 0)
        def _():
            dq_ref[...] = jnp.zeros_like(dq_ref)
            dqsum_ref[...] = jnp.zeros_like(dqsum_ref)

        dk_s[...] = jnp.zeros_like(dk_s)
        dv_s[...] = jnp.zeros_like(dv_s)
        dsum_s[...] = jnp.zeros_like(dsum_s)

        def step(qi, diag):
            rows = pl.ds(pl.multiple_of(qi * t, t), t)
            for hh in range(PAIR):
                q = q_ref[rows, AUG * hh:AUG * (hh + 1)]
                k = k_ref[:, AUG * hh:AUG * (hh + 1)]
                st = lax.dot_general(k, q, _NT, preferred_element_type=F32)
                if diag:
                    st = jnp.where(_causal_keep(t, t), st, NEG)
                pt = jnp.exp2(st - lse_ref[hh, qi])
                do_v = do_ref[rows, dh * hh:dh * (hh + 1)]
                dpt = lax.dot_general(v_ref[:, dh * hh:dh * (hh + 1)], do_v, _NT, preferred_element_type=F32)
                ds32 = pt * (dpt - dl_ref[hh, qi])
                dsum_s[hh] += sum(ds32[:, 128 * u:128 * (u + 1)] for u in range(t // 128))
                dqsum_ref[hh, qi] += jnp.sum(ds32, axis=0, keepdims=True)
                dst = ds32.astype(BF16)
                dv_s[hh] += jnp.dot(pt.astype(BF16), do_v, preferred_element_type=F32)
                dk_s[hh] += jnp.dot(dst, q[:, :dh], preferred_element_type=F32)
                dq_ref[hh, qi] += lax.dot_general(k[:, :dh], dst, _TN, preferred_element_type=F32)

        def off_diagonal(qi, carry):
            step(qi, False)
            return carry

        step(kj, True)
        lax.fori_loop(kj + 1, n, off_diagonal, 0)
        for hh in range(PAIR):
            dk_ref[:, dh * hh:dh * (hh + 1)] = (LN2 * dk_s[hh]).astype(dk_ref.dtype)
            dv_ref[:, dh * hh:dh * (hh + 1)] = dv_s[hh].astype(dv_ref.dtype)
            dsum_ref[:, 128 * hh:128 * (hh + 1)] = dsum_s[hh]
        finish()

    stat = pl.BlockSpec((PAIR, n, 1, t), lambda hp, j: (hp, 0, 0, 0))
    return _pcall(
        body, name=name,
        out_shape=[jax.ShapeDtypeStruct((s, WIDTH), BF16), jax.ShapeDtypeStruct((s, WIDTH), BF16),
                   jax.ShapeDtypeStruct((N_HEADS, n, dh, t), F32), jax.ShapeDtypeStruct((s, N_HEADS * 128), F32),
                   jax.ShapeDtypeStruct((N_HEADS, n, 1, t), F32)] + ex_out,
        grid=grid,
        in_specs=[pl.BlockSpec((s, PAIR * AUG), lambda hp, j: (0, hp)),
                  pl.BlockSpec((t, PAIR * AUG), lambda hp, j: (j, hp)),
                  pl.BlockSpec((t, PAIR * dh), lambda hp, j: (j, V_FOX + hp)),
                  pl.BlockSpec((s, PAIR * dh), lambda hp, j: (0, hp)),
                  stat, stat] + ex_specs,
        out_specs=[pl.BlockSpec((t, PAIR * dh), lambda hp, j: (j, hp)),
                   pl.BlockSpec((t, PAIR * dh), lambda hp, j: (j, hp)),
                   pl.BlockSpec((PAIR, n, dh, t), lambda hp, j: (hp, 0, 0, 0)),
                   pl.BlockSpec((t, PAIR * 128), lambda hp, j: (j, hp)),
                   stat] + ex_specs,
        scratch_shapes=[pltpu.VMEM((PAIR, t, dh), F32), pltpu.VMEM((PAIR, t, dh), F32),
                        pltpu.VMEM((PAIR, t, 128), F32)] + ex_scratch,
        compiler_params=_cp("arbitrary", "arbitrary"),
    )(q_aug, k_aug, qkv, do, lse, delta, *ex_in)


N_SUB = TQ // SUB


def _pair_blockdiag(x2):
    lane = lax.broadcasted_iota(jnp.int32, x2.shape, 1)
    zero = jnp.zeros_like(x2)
    return jnp.concatenate([jnp.where(lane < HEAD_DIM, x2, zero), jnp.where(lane >= HEAD_DIM, x2, zero)], axis=0)


def _pair_lanes(ref, j):
    return jnp.concatenate([ref[hh, 0, :, SUB * j:SUB * (j + 1)] for hh in range(PAIR)], axis=1)


def _chunk_logits(kwin, qbd, b_ref, first_tile, j):
    which = jnp.where(first_tile, 1 + j, 0)
    bias = jnp.concatenate([b_ref[hh, which] for hh in range(PAIR)], axis=1)
    return lax.dot_general(kwin, qbd, _NT, preferred_element_type=F32) + bias


def _chunk_fwd(qkv, bias_t, name):
    s = qkv.shape[0]
    dh = HEAD_DIM
    n = s // TQ
    nsub = TQ // SUB
    prev = lambda i: jnp.maximum(i - 1, 0)

    def body(q_ref, kp_ref, kc_ref, vp_ref, vc_ref, b_ref, o_ref, lse_ref):
        first_tile = pl.program_id(1) == 0
        kw = jnp.concatenate([kp_ref[...], kc_ref[...]], axis=0)
        vw = jnp.concatenate([vp_ref[...], vc_ref[...]], axis=0)
        for j in range(nsub):
            rows, win = slice(SUB * j, SUB * (j + 1)), slice(SUB * j, SUB * j + SUBK)
            st = _chunk_logits(kw[win], _pair_blockdiag(q_ref[rows, :] * QK_SCALE2), b_ref, first_tile, j)
            m = jnp.max(st, axis=0, keepdims=True)
            pt = jnp.exp2(st - m)
            l = jnp.sum(pt, axis=0, keepdims=True)
            ot = lax.dot_general(vw[win], pt.astype(BF16), _TN, preferred_element_type=F32)
            stat = m + jnp.log2(l)
            for hh in range(PAIR):
                lanes = slice(SUB * hh, SUB * (hh + 1))
                o_ref[hh, :, rows] = ot[dh * hh:dh * (hh + 1), lanes] / l[:, lanes]
                lse_ref[hh, 0, :, rows] = stat[:, lanes]

    blk = lambda col, m: pl.BlockSpec((TQ, PAIR * dh), lambda hp, i: (m(i), col + hp))
    same = lambda i: i
    return _pcall(
        body, name=name,
        out_shape=(jax.ShapeDtypeStruct((N_HEADS, dh, s), F32), jax.ShapeDtypeStruct((N_HEADS, n, 1, TQ), F32)),
        grid=(N_HEADS // PAIR, n),
        in_specs=[blk(Q_CHUNK, same), blk(K_CHUNK, prev), blk(K_CHUNK, same), blk(V_CHUNK, prev), blk(V_CHUNK, same),
                  pl.BlockSpec((PAIR, 1 + N_SUB, SUBK, SUB), lambda hp, i: (hp, 0, 0, 0))],
        out_specs=(pl.BlockSpec((PAIR, dh, TQ), lambda hp, i: (hp, 0, i)),
                   pl.BlockSpec((PAIR, 1, 1, TQ), lambda hp, i: (hp, i, 0, 0))),
        compiler_params=_cp("parallel", "parallel"),
    )(qkv, qkv, qkv, qkv, qkv, bias_t)


def _chunk_bwd(qkv, do, lse, delta, bias_t, name):
    s = qkv.shape[0]
    dh = HEAD_DIM
    n = s // TQ
    nsub = TQ // SUB
    cur = lambda i: jnp.minimum(i, n - 1)
    prev = lambda i: jnp.maximum(cur(i) - 1, 0)
    done = lambda i: jnp.maximum(i - 1, 0)

    def body(q_ref, kp_ref, kc_ref, vp_ref, vc_ref, do_ref, lse_ref, dl_ref, b_ref,
             dq_ref, dk_ref, dv_ref, db_ref, dkw_s, dvw_s, ck_s, cv_s):
        i = pl.program_id(1)

        @pl.when(i == 0)
        def _():
            db_ref[...] = jnp.zeros_like(db_ref)
            ck_s[...] = jnp.zeros_like(ck_s)
            cv_s[...] = jnp.zeros_like(cv_s)

        dkw_s[...] = jnp.zeros_like(dkw_s)
        dvw_s[...] = jnp.zeros_like(dvw_s)

        @pl.when(i < n)
        def _():
            first_tile = i == 0
            kw = jnp.concatenate([kp_ref[...], kc_ref[...]], axis=0)
            vw = jnp.concatenate([vp_ref[...], vc_ref[...]], axis=0)
            for j in range(nsub):
                rows, win = slice(SUB * j, SUB * (j + 1)), slice(SUB * j, SUB * j + SUBK)
                qbd = _pair_blockdiag(q_ref[rows, :] * QK_SCALE2)
                dobd = _pair_blockdiag(do_ref[rows, :])
                st = _chunk_logits(kw[win], qbd, b_ref, first_tile, j)
                pt = jnp.exp2(st - _pair_lanes(lse_ref, j))
                dpt = lax.dot_general(vw[win], dobd, _NT, preferred_element_type=F32)
                dst = pt * (dpt - _pair_lanes(dl_ref, j))
                dsb = dst.astype(BF16)
                dvw_s[win, :] += jnp.dot(pt.astype(BF16), dobd, preferred_element_type=F32)
                dkw_s[win, :] += jnp.dot(dsb, qbd, preferred_element_type=F32)
                dqt = QK_SCALE * lax.dot_general(kw[win], dsb, _TN, preferred_element_type=F32)
                for hh in range(PAIR):
                    lanes = slice(SUB * hh, SUB * (hh + 1))
                    db_ref[hh] += dst[:, lanes]
                    dq_ref[hh, :, rows] = dqt[dh * hh:dh * (hh + 1), lanes]

        dk_ref[...] = (LN2 * (ck_s[...] + dkw_s[0:TQ, :])).astype(dk_ref.dtype)
        dv_ref[...] = (cv_s[...] + dvw_s[0:TQ, :]).astype(dv_ref.dtype)
        ck_s[...] = dkw_s[TQ:2 * TQ, :]
        cv_s[...] = dvw_s[TQ:2 * TQ, :]

    blk = lambda col, m: pl.BlockSpec((TQ, PAIR * dh), lambda hp, i: (m(i), col + hp))
    stat = pl.BlockSpec((PAIR, 1, 1, TQ), lambda hp, i: (hp, cur(i), 0, 0))
    bs = pl.BlockSpec((PAIR, SUBK, SUB), lambda hp, i: (hp, 0, 0))
    return _pcall(
        body, name=name,
        out_shape=(jax.ShapeDtypeStruct((N_HEADS, dh, s), F32), jax.ShapeDtypeStruct((s, WIDTH), BF16),
                   jax.ShapeDtypeStruct((s, WIDTH), BF16), jax.ShapeDtypeStruct((N_HEADS, SUBK, SUB), F32)),
        grid=(N_HEADS // PAIR, n + 1),
        in_specs=[blk(Q_CHUNK, cur), blk(K_CHUNK, prev), blk(K_CHUNK, cur), blk(V_CHUNK, prev), blk(V_CHUNK, cur),
                  blk(0, cur), stat, stat, pl.BlockSpec((PAIR, 1 + N_SUB, SUBK, SUB), lambda hp, i: (hp, 0, 0, 0))],
        out_specs=(pl.BlockSpec((PAIR, dh, TQ), lambda hp, i: (hp, 0, cur(i))), blk(0, done), blk(0, done), bs),
        scratch_shapes=[pltpu.VMEM((2 * TQ, PAIR * dh), F32), pltpu.VMEM((2 * TQ, PAIR * dh), F32),
                        pltpu.VMEM((TQ, PAIR * dh), F32), pltpu.VMEM((TQ, PAIR * dh), F32)],
        compiler_params=_cp("parallel", "arbitrary"),
    )(qkv, qkv, qkv, qkv, qkv, do, lse, delta, bias_t)


def _mod_part(c_all, w_ada, b_ada, name):
    nl, d, n = w_ada.shape
    b = c_all.shape[0]

    def body(c_ref, w_ref, b_ref, o_ref, cond_ref):
        cv = c_ref[...]
        cond = cv * _sigmoid(cv)
        cond_ref[...] = cond
        o_ref[...] = jnp.dot(cond, w_ref[...], preferred_element_type=F32, precision=HI) + b_ref[...]

    return _pcall(
        body, name=name,
        out_shape=(jax.ShapeDtypeStruct((nl, b, n), F32), jax.ShapeDtypeStruct((b, d), F32)),
        grid=(nl,),
        in_specs=[pl.BlockSpec((b, d), lambda l: (0, 0)), pl.BlockSpec((None, d, n), lambda l: (l, 0, 0)),
                  pl.BlockSpec((None, 1, n), lambda l: (l, 0, 0))],
        out_specs=(pl.BlockSpec((None, b, n), lambda l: (l, 0, 0)), pl.BlockSpec((b, d), lambda l: (0, 0))),
        compiler_params=_cp("arbitrary"),
    )(c_all, w_ada, b_ada)


def _adamw(parts, w, m, v, name, tr=1024):
    p, rl, cp = parts[0].shape
    r, c = w.shape
    tr = _tile(rl, tr, 16)
    per = rl // tr
    c1 = 1.0 / (1.0 - ADAM_B1 ** ADAM_STEP)
    c2 = 1.0 / (1.0 - ADAM_B2 ** ADAM_STEP)

    def body(*refs):
        p_refs, (w_ref, m_ref, v_ref, g_ref, d_ref, nm_ref, nv_ref) = refs[:len(parts)], refs[len(parts):]
        for which, p_ref in enumerate(p_refs):
            pl.when(pl.program_id(0) // per == which)(
                functools.partial(update, p_ref, w_ref, m_ref, v_ref, g_ref, d_ref, nm_ref, nv_ref))

    def update(p_ref, w_ref, m_ref, v_ref, g_ref, d_ref, nm_ref, nv_ref):
        g = p_ref[0].astype(F32)
        for i in range(1, p):
            g = g + p_ref[i].astype(F32)
        g = g[:, :c]
        nm = ADAM_B1 * m_ref[...] + (1.0 - ADAM_B1) * g
        nv = ADAM_B2 * v_ref[...] + (1.0 - ADAM_B2) * (g * g)
        g_ref[...] = g
        nm_ref[...] = nm
        nv_ref[...] = nv
        d_ref[...] = -ADAM_LR * ((nm * c1) / (jnp.sqrt(nv * c2) + ADAM_EPS) + ADAM_WD * w_ref[...])

    blk = pl.BlockSpec((tr, c), lambda i: (i, 0))
    out = jax.ShapeDtypeStruct((r, c), F32)
    return _pcall(
        body, name=name, out_shape=(out, out, out, out), grid=(r // tr,),
        in_specs=[pl.BlockSpec((p, tr, cp), lambda i, _w=which: (0, jnp.clip(i - _w * per, 0, per - 1), 0))
                  for which in range(len(parts))] + [blk, blk, blk],
        out_specs=(blk, blk, blk, blk),
        compiler_params=_cp("parallel"),
    )(*parts, w, m, v)


def _pad_lanes(a):
    pad = (-a.shape[-1]) % 128
    return jnp.pad(a, [(0, 0)] * (a.ndim - 1) + [(0, pad)]) if pad else a


def _rows128(a, mult=16):
    flat = a.reshape(-1)
    n = flat.shape[0]
    per = 128 * mult
    pad = (-n) % per
    if pad:
        flat = jnp.concatenate([flat, jnp.zeros((pad,), a.dtype)])
    return flat.reshape(-1, 128)


def _pack(arrs, mult=16):
    pieces = [_rows128(a, mult) for a in arrs]
    return jnp.concatenate(pieces, axis=0), [p.shape[0] for p in pieces]


def _unpack(packed, rows, shapes, lead=()):
    out, at = [], 0
    for r, shp in zip(rows, shapes):
        n = int(np.prod(shp))
        piece = packed[..., at:at + r, :].reshape(lead + (r * 128,))[..., :n]
        out.append(piece.reshape(lead + tuple(shp)))
        at += r
    return out


def _rel_bias_tile(table):
    h = table.shape[0]
    lo = REL_CLIP - (CHUNK - 1)
    n_far = LEFT_CHUNKS * CHUNK + CHUNK - 1 - REL_CLIP
    vec = jnp.concatenate([table[:, lo:2 * REL_CLIP], jnp.repeat(table[:, 2 * REL_CLIP:], n_far + 1, axis=1)], axis=1)
    rev = vec[:, ::-1]
    n_vec = BAND + CHUNK - 1
    skew = jnp.tile(rev, (1, CHUNK + 1))[:, :CHUNK * (n_vec + 1)].reshape(h, CHUNK, n_vec + 1)
    bias = skew[:, ::-1, :BAND]
    neg = jnp.full((h, CHUNK, CHUNK), NEG, F32)
    two = jnp.concatenate([jnp.concatenate([bias, neg], axis=2), jnp.concatenate([neg, bias], axis=2)], axis=1)
    plain = two.transpose(0, 2, 1) * LOG2E
    key = np.arange(SUBK)[:, None]
    return jnp.stack([plain] + [jnp.where(key >= TQ - SUB * j, plain, NEG) for j in range(N_SUB)], axis=1)


def _rel_bias_tile_grad(dbias_t):
    h = dbias_t.shape[0]
    two = dbias_t.transpose(0, 2, 1)
    dbias = two[:, :CHUNK, :BAND] + two[:, CHUNK:, CHUNK:]
    n_vec = BAND + CHUNK - 1
    dskew = jnp.pad(dbias[:, ::-1, :], ((0, 0), (0, 0), (0, n_vec + 1 - BAND))).reshape(h, CHUNK * (n_vec + 1))
    dskew = jnp.pad(dskew, ((0, 0), (0, (CHUNK + 1) * n_vec - CHUNK * (n_vec + 1))))
    drev = jnp.sum(dskew.reshape(h, CHUNK + 1, n_vec), axis=1)
    dvec = drev[:, ::-1]
    lo = REL_CLIP - (CHUNK - 1)
    n_near = 2 * REL_CLIP - lo
    return jnp.concatenate([jnp.zeros((h, lo), F32), dvec[:, :n_near],
                            jnp.sum(dvec[:, n_near:], axis=1, keepdims=True)], axis=1)


BIG = ("w_in", "w_br_fox", "w_br_chunk", "w_out", "w_up", "w_down")
SMALL = ("b_f", "rel_bias", "conv_b", "b_ada", "ln1_g", "ln1_b", "ln2_g", "ln2_b")
ORDER = ("w_in", "b_f", "rel_bias", "w_br_fox", "w_br_chunk", "w_out", "w_up", "conv_w", "conv_b", "w_down",
         "w_ada", "b_ada", "ln1_g", "ln1_b", "ln2_g", "ln2_b")


def _dest_major(g, axis):
    shp = g.shape
    g = g.reshape(shp[:axis] + (N_DEV, shp[axis] // N_DEV) + shp[axis + 1:])
    return jnp.moveaxis(g, axis, 0)


EARLY = ("w_down", "w_up", "w_out", "w_br_fox", "w_br_chunk")
LATE = ("w_in",)
REST = ("w_br_fox", "w_br_chunk", "w_out", "w_up", "w_down")


def _shard_axis(n):
    return 0 if n in ("w_out", "w_down") else 1


def _slab(g, n):
    return _pad_lanes(_dest_major(g, _shard_axis(n)).astype(BF16))


def _layer_fwd(x, h1, wl, rest_of, mod, next_mod, alpha, gather):
    sh1, sc1, g1, sh2, sc2, g2 = mod
    s = x.shape[0]
    qkv = _mm(h1, wl["w_qkv"], out_dtype=BF16, name="proj_qkv", tk=1024)
    gates = _mm(h1, wl["w_gates"], out_dtype=F32, name="proj_gates", tk=1024)
    f_t = _mm(wl["w_f_t"], h1, out_dtype=F32, name="proj_f", tb=True, tk=1024)[:N_HEADS]
    f_t = f_t.reshape(N_HEADS, s // 128, 128)
    b_f = jnp.broadcast_to(wl["b_f"].reshape(N_HEADS, 1, 1), (N_HEADS, 1, 128))
    cum = _forget_fwd(f_t, b_f, "forget_fwd").reshape(N_SPLIT, N_HEADS, s)
    cum_cols = cum.transpose(2, 1, 0).astype(BF16)

    ones = jnp.ones((s, N_HEADS, N_SPLIT), BF16)
    zeros = jnp.zeros((s, N_HEADS, AUG - HEAD_DIM - 2 * N_SPLIT), BF16)
    q_a = qkv[:, :WIDTH].reshape(s, N_HEADS, HEAD_DIM) * QK_SCALE2
    k_a = qkv[:, WIDTH:2 * WIDTH].reshape(s, N_HEADS, HEAD_DIM)
    q_aug = jnp.concatenate([q_a, cum_cols, ones, zeros], axis=-1).reshape(s, N_HEADS * AUG)
    k_aug = jnp.concatenate([k_a, ones, -cum_cols, zeros], axis=-1).reshape(s, N_HEADS * AUG)
    o_a_t, lse_a, *gathered = _fox_fwd(q_aug, k_aug, qkv, "fox_fwd_gather_%d" % len(gather), t=FOX_FWD_T,
                                       exchange=(False, gather))
    wl.update(rest_of(gathered[:len(REST)]))
    gathered = gathered[len(REST):]
    bias_t = _rel_bias_tile(wl["rel_bias"])
    o_c_t, lse_c = _chunk_fwd(qkv, bias_t, "chunk_fwd")
    o_a = o_a_t.transpose(2, 0, 1).reshape(s, WIDTH).astype(BF16)
    o_c = o_c_t.transpose(2, 0, 1).reshape(s, WIDTH).astype(BF16)

    merged = _merge_fwd(o_a, o_c, wl["w_br_fox"], wl["w_br_chunk"], gates, "merge_fwd")
    mix, x1, h2 = _mm_res_ln(merged, wl["w_out"], x, g1, wl["ln1_g"], wl["ln1_b"], sc2, sh2, alpha, "out_ln1")
    u0 = _mm(h2, wl["w_up"], out_dtype=F32, name="ffn_up", tk=1024, tn=1408, o_halves=True)
    act = _conv_act_fwd(u0, wl["conv_w"], wl["conv_b"], "conv_act_fwd")
    y2, x2, h_next = _mm_res_ln(act, wl["w_down"], x1, g2, wl["ln2_g"], wl["ln2_b"], next_mod[0], next_mod[1],
                                alpha, "down_ln2", tk=1408)
    saved = dict(x=x, h1=h1, gates=gates, f_t=f_t, b_f=b_f, q_aug=q_aug, k_aug=k_aug, qkv=qkv,
                 bias_t=bias_t, o_a_t=o_a_t, o_c_t=o_c_t, lse_a=lse_a, lse_c=lse_c,
                 o_a=o_a, o_c=o_c, merged=merged, mix=mix, x1=x1, h2=h2, u0=u0, act=act, y2=y2)
    return x2, h_next, saved, gathered


def _layer_bwd(dx2, sv, wl, mod, alpha, send, ride_late):
    sh1, sc1, g1, sh2, sc2, g2 = mod
    s = dx2.shape[0]
    g = {}
    dres2, dy2, sums = _ln_res_bwd(dx2, sv["x1"], sv["y2"], g2, wl["ln2_g"], alpha, "ln2_bwd")
    g["ln2_g"], g["ln2_b"], dg2 = sums[0], sums[1], sums[2]
    dact = _mm(dy2, wl["w_down_t"], out_dtype=F32, name="dact", tk=1024, tn=1408)
    g["w_down"] = _mm(sv["act"], dy2, out_dtype=F32, name="g_w_down", ta=True, tm=1408, tk=2048)
    du0, csum = _conv_act_bwd(dact, sv["u0"], wl["conv_w"], wl["conv_b"], "conv_act_bwd")
    g["conv_w"] = jnp.concatenate([csum[0, 0:3], csum[1, 0:3]], axis=1)
    g["conv_b"] = jnp.concatenate([csum[0, 3], csum[1, 3]])
    dh2 = _mm(du0, wl["w_up_t"], out_dtype=F32, name="dh2", tk=2816, a_halves=True)
    g["w_up"] = _mm(sv["h2"], du0, out_dtype=F32, name="g_w_up", ta=True, tk=2048, tn=1408, b_halves=True)
    dres1, dmix, sums2, sums = _ln_mod_res_bwd(dh2, sv["x1"], sc2, dres2, sv["x"], sv["mix"], g1, wl["ln1_g"], alpha,
                                               "ln_mod2_ln1_bwd")
    dsh2, dsc2 = sums2[0], sums2[1]
    g["ln1_g"], g["ln1_b"], dg1 = sums[0], sums[1], sums[2]
    g["w_out"] = _mm(sv["merged"], dmix, out_dtype=F32, name="g_w_out", ta=True, tk=2048)
    dba, dbc, do_a, do_c, dgates = _merge_bwd(
        dmix, wl["w_out_t"], sv["o_a"], sv["o_c"], wl["w_br_fox"], wl["w_br_chunk"],
        wl["w_br_fox_t"], wl["w_br_chunk_t"], sv["gates"], "merge_bwd")
    g["w_br_fox"] = _mm(sv["o_a"], dba, out_dtype=F32, name="g_w_br_fox", ta=True, tk=2048)
    g["w_br_chunk"] = _mm(sv["o_c"], dbc, out_dtype=F32, name="g_w_br_chunk", ta=True, tk=2048)
    tiles = lambda a, t: a.reshape(N_HEADS, s // min(t, s), 1, min(t, s))
    stats = lambda a: tiles(a, TQ)

    delta_a = _row_dot(do_a.T.reshape(N_HEADS, HEAD_DIM, s), sv["o_a_t"], "delta_fox")
    send = list(send) + [_slab(g[n], n) for n in EARLY]
    dk_a, dv_a, dq_a_t, dsum, dqsum, *landed = _fox_bwd(
        sv["q_aug"], sv["k_aug"], sv["qkv"], do_a, tiles(sv["lse_a"], FOX_BWD_T), tiles(delta_a, FOX_BWD_T),
        "fox_bwd_scatter_%d" % len(send), t=FOX_BWD_T, exchange=(True, send))
    dq_a = (dq_a_t * QK_SCALE).transpose(1, 3, 0, 2).reshape(s, WIDTH).astype(BF16)
    d_cum = dqsum.reshape(N_HEADS, s) - jnp.sum(dsum.reshape(s, N_HEADS, 128), axis=-1).T
    df_t, db_f = _forget_bwd(d_cum.reshape(N_HEADS, s // 128, 128), sv["f_t"], sv["b_f"], "forget_bwd")
    g["b_f"] = db_f[:, 0, 0]
    df_t = df_t.reshape(N_HEADS, s)

    delta_c = _row_dot(do_c.T.reshape(N_HEADS, HEAD_DIM, s), sv["o_c_t"], "delta_chunk")
    dq_c_t, dk_c, dv_c, dbias_t = _chunk_bwd(sv["qkv"], do_c, sv["lse_c"], stats(delta_c), sv["bias_t"], "chunk_bwd")
    g["rel_bias"] = _rel_bias_tile_grad(dbias_t)
    dq_c = dq_c_t.transpose(2, 0, 1).reshape(s, WIDTH).astype(BF16)

    dqkv = jnp.concatenate([dq_a, dk_a, dv_a, dq_c, dk_c, dv_c], axis=1)
    df_pad = jnp.zeros((16 - N_HEADS, s), F32)
    df16 = jnp.concatenate([df_t, df_pad], axis=0).astype(BF16)
    df_cols = jnp.concatenate([df16.T, jnp.zeros((s, AUG - 16), BF16)], axis=1)

    g_qkv = _mm(sv["h1"], dqkv, out_dtype=F32, name="g_w_qkv", ta=True, tk=2048)
    g_gates = _mm(sv["h1"], dgates, out_dtype=F32, name="g_w_gates", ta=True, tk=2048)
    g_f_t = _mm(df16, sv["h1"], out_dtype=F32, name="g_w_f", tk=1024)[:N_HEADS]
    g["w_in"] = jnp.concatenate([g_qkv[:, :3 * WIDTH], g_f_t.T, g_qkv[:, 3 * WIDTH:], g_gates], axis=1)
    late = [_slab(g[n], n) for n in LATE] if ride_late else None
    dh1, *landed_late = _mm(dqkv, wl["w_qkv_t"], out_dtype=F32, name="dh1_qkv_scatter", tk=1536,
                            exchange=(True, late)) if ride_late else \
        [_mm(dqkv, wl["w_qkv_t"], out_dtype=F32, name="dh1_qkv", tk=1536)]
    dh1g = _mm(dgates, wl["w_gates_t"], out_dtype=F32, name="dh1_gates", tk=2048)
    dx, sums = _ln_mod_bwd(dh1, sv["x"], sc1, dres1, "ln_mod1_bwd", extra=(dh1g, df_cols, wl["w_f_pad"]))
    dsh1, dsc1 = sums[0], sums[1]
    g["mod"] = jnp.concatenate([dsh1, dsc1, dg1, dsh2, dsc2, dg2])
    return dx, g, landed, landed_late


def kernel(x, c, w_in, b_f, rel_bias, w_br_fox, w_br_chunk, w_out, w_up, conv_w, conv_b, w_down, w_ada, b_ada, ln1_g, ln1_b, ln2_g, ln2_b, loss_target, m_w_in, m_b_f, m_rel_bias, m_w_br_fox, m_w_br_chunk, m_w_out, m_w_up, m_conv_w, m_conv_b, m_w_down, m_w_ada, m_b_ada, m_ln1_g, m_ln1_b, m_ln2_g, m_ln2_b, v_w_in, v_b_f, v_rel_bias, v_w_br_fox, v_w_br_chunk, v_w_out, v_w_up, v_conv_w, v_conv_b, v_w_down, v_w_ada, v_b_ada, v_ln1_g, v_ln1_b, v_ln2_g, v_ln2_b):
    w = dict(w_in=w_in, b_f=b_f, rel_bias=rel_bias, w_br_fox=w_br_fox, w_br_chunk=w_br_chunk, w_out=w_out,
             w_up=w_up, conv_w=conv_w, conv_b=conv_b, w_down=w_down, w_ada=w_ada, b_ada=b_ada,
             ln1_g=ln1_g, ln1_b=ln1_b, ln2_g=ln2_g, ln2_b=ln2_b)
    m = dict(w_in=m_w_in, b_f=m_b_f, rel_bias=m_rel_bias, w_br_fox=m_w_br_fox, w_br_chunk=m_w_br_chunk,
             w_out=m_w_out, w_up=m_w_up, conv_w=m_conv_w, conv_b=m_conv_b, w_down=m_w_down, w_ada=m_w_ada,
             b_ada=m_b_ada, ln1_g=m_ln1_g, ln1_b=m_ln1_b, ln2_g=m_ln2_g, ln2_b=m_ln2_b)
    v = dict(w_in=v_w_in, b_f=v_b_f, rel_bias=v_rel_bias, w_br_fox=v_w_br_fox, w_br_chunk=v_w_br_chunk,
             w_out=v_w_out, w_up=v_w_up, conv_w=v_conv_w, conv_b=v_conv_b, w_down=v_w_down, w_ada=v_w_ada,
             b_ada=v_b_ada, ln1_g=v_ln1_g, ln1_b=v_ln1_b, ln2_g=v_ln2_g, ln2_b=v_ln2_b)
    depth, d, _ = w_in.shape
    s = x.shape[1]
    alpha = (2.0 * depth) ** 0.25
    me = 4 * lax.axis_index("x") + 2 * lax.axis_index("y") + lax.axis_index("c")
    x0 = x.reshape(s, d)
    target = loss_target.reshape(s, d)

    small_in, small_rows = _pack([c, conv_w], mult=8)
    small_all = _all_gather([small_in], "gather_c_conv")[0]
    c_all, conv_w_all = _unpack(small_all, small_rows, [c.shape, conv_w.shape], lead=(N_DEV,))
    c_all = c_all.reshape(N_DEV, d)
    conv_w_full = conv_w_all.transpose(1, 2, 0, 3).reshape(depth, conv_w.shape[1], -1)
    n_ada = w_ada.shape[2]
    b_ada_mine = lax.dynamic_slice_in_dim(b_ada, me * n_ada, n_ada, axis=1).reshape(depth, 1, n_ada)
    mod_part, cond_all = _mod_part(c_all, w_ada, b_ada_mine, "mod_part")
    mod_all = _all_gather([mod_part.reshape(depth * N_DEV, n_ada)], "gather_mod")[0]
    mod_all = mod_all.reshape(N_DEV, depth, N_DEV, n_ada)
    mod_mine = lax.dynamic_index_in_dim(mod_all, me, axis=2, keepdims=False)
    mod_mine = mod_mine.transpose(1, 0, 2).reshape(depth, N_MOD, 1, d)

    shard = lambda n, l: _pad_lanes(w[n][l].astype(BF16))
    cols = np.cumsum([0, WIDTH, WIDTH, WIDTH, N_HEADS, WIDTH, WIDTH, WIDTH, d, d])

    def whole(n, p):
        rows, ncol = w[n].shape[1:]
        p = jnp.moveaxis(p[..., :ncol], 0, _shard_axis(n))
        return p.reshape((rows * N_DEV, ncol) if _shard_axis(n) == 0 else (rows, ncol * N_DEV))

    def first_weights(l, w_in_parts):
        wi = whole("w_in", w_in_parts)
        w_qkv = jnp.concatenate([wi[:, :cols[3]], wi[:, cols[4]:cols[7]]], axis=1)
        w_gates = wi[:, cols[7]:]
        w_f_t = jnp.concatenate([wi[:, cols[3]:cols[4]].T, jnp.zeros((16 - N_HEADS, d), BF16)], axis=0)
        w_f_pad = jnp.concatenate([w_f_t, jnp.zeros((AUG - 16, d), BF16)], axis=0)
        row = lambda a: a[l].reshape(1, -1)
        return dict(
            w_qkv=w_qkv, w_gates=w_gates, w_f_t=w_f_t, w_f_pad=w_f_pad, w_qkv_t=w_qkv.T, w_gates_t=w_gates.T,
            conv_w=conv_w_full[l].reshape(3, 2, -1).transpose(1, 0, 2), conv_b=conv_b[l].reshape(2, 1, -1),
            b_f=b_f[l], rel_bias=rel_bias[l],
            ln1_g=row(ln1_g), ln1_b=row(ln1_b), ln2_g=row(ln2_g), ln2_b=row(ln2_b))

    def rest_weights(parts):
        out = {}
        for n, p in zip(REST, parts):
            out[n] = whole(n, p)
            out[n + "_t"] = out[n].T
        return out

    xs, saved, layers = x0, [], []
    hs = _ln_mod(x0, mod_mine[0][1], mod_mine[0][0], "ln_mod1")
    w_in_parts = _all_gather([shard("w_in", 0)], "gather_weights")[0]
    for l in range(depth):
        layers.append(first_weights(l, w_in_parts))
        nxt = mod_mine[min(l + 1, depth - 1)]
        gather = [shard(n, l) for n in REST] + ([shard("w_in", l + 1)] if l + 1 < depth else [])
        xs, hs, sv, extra = _layer_fwd(xs, hs, layers[l], rest_weights, list(mod_mine[l]), (nxt[1], nxt[0]), alpha,
                                       gather)
        w_in_parts = extra[0] if extra else None
        saved.append(sv)
    dxs, loss_part = _loss_grad(xs, target, "loss_grad")
    loss = lax.psum(loss_part[0, 0], AXES)
    grads, big_parts, send = [None] * depth, {}, []
    for l in reversed(range(depth)):
        dxs, grads[l], landed, landed_late = _layer_bwd(dxs, saved[l], layers[l], list(mod_mine[l]), alpha, send,
                                                        ride_late=(l == 0))
        for n, part in zip([(n, l + 1) for n in LATE if send] + [(n, l) for n in EARLY], landed):
            big_parts[n] = part
        for n, part in zip(LATE, landed_late):
            big_parts[n, l] = part
        send = [_slab(grads[l][n], n) for n in LATE] if l else []
    grad_x = dxs.reshape(1, s, d)
    stack = lambda n: jnp.stack([grads[l][n] for l in range(depth)])

    rep = ["b_ada"] + [n for n in SMALL if n != "b_ada"]
    small_g = [stack("mod")] + [stack(n) for n in rep[1:]] + [stack("conv_w")]
    small_pack, small_g_rows = _pack(small_g, mult=8)
    small_parts = _all_gather([small_pack], "gather_small_grads")[0]
    n_rep_rows = sum(small_g_rows[:-1])
    dmod_all = _unpack(small_parts, small_g_rows[:1], [(depth, N_MOD * d)], lead=(N_DEV,))[0]
    cw_all = _unpack(small_parts[:, n_rep_rows:], small_g_rows[-1:], [small_g[-1].shape], lead=(N_DEV,))[0]

    out = {}
    for n in BIG:
        rows2d = lambda a: a.reshape(-1, a.shape[-1])
        res = _adamw([big_parts[n, l] for l in range(depth)], rows2d(w[n]), rows2d(m[n]), rows2d(v[n]),
                     "adamw_" + n, tr=512)
        for kind, r in zip(("grad", "delta", "new_m", "new_v"), res):
            out[kind, n] = r.reshape(w[n].shape)

    dmod_mine = lax.dynamic_slice_in_dim(dmod_all, me * n_ada, n_ada, axis=2)
    g_ada = jnp.stack([_mm(cond_all.T, dmod_mine[:, l], out_dtype=F32, name="g_w_ada", tk=N_DEV, tn=n_ada,
                           precision=HI) for l in range(depth)])
    lp = [_rows128(t["w_ada"], 8) for t in (w, m, v)]
    res = _adamw([_rows128(g_ada, 8)[None]], *lp, "adamw_ada")
    for kind, r in zip(("grad", "delta", "new_m", "new_v"), res):
        out[kind, "w_ada"] = _unpack(r, [lp[0].shape[0]], [w_ada.shape])[0]

    rp = [_pack([t[n] for n in rep], mult=8)[0] for t in (w, m, v)]
    res = _adamw([small_parts[:, :n_rep_rows]], *rp, "adamw_small")
    for kind, r in zip(("grad", "delta", "new_m", "new_v"), res):
        for n, a in zip(rep, _unpack(r, small_g_rows[:-1], [w[n].shape for n in rep])):
            out[kind, n] = a

    n_cw = conv_w.shape[2]
    cw_mine = lax.dynamic_slice_in_dim(cw_all, me * n_cw, n_cw, axis=3)
    cp = [_rows128(t["conv_w"], 8) for t in (w, m, v)]
    cw_in = jnp.stack([_rows128(cw_mine[i], 8) for i in range(N_DEV)])
    res = _adamw([cw_in], *cp, "adamw_conv_w")
    for kind, r in zip(("grad", "delta", "new_m", "new_v"), res):
        out[kind, "conv_w"] = _unpack(r, [cp[0].shape[0]], [conv_w.shape])[0]

    result = [loss, grad_x]
    for kind in ("grad", "delta", "new_m", "new_v"):
        result += [out[kind, n] for n in ORDER]
    return tuple(result)
```
